```python
import jax, jax.numpy as jnp
from jax import lax
import numpy as np

D_MODEL = 1024
BATCH = 8
SEQ = 4096
DEPTH = 1

CHUNK = 64
Q_BLOCK = 128
MIX_WIDTH = D_MODEL
MLA_HEADS = 4
QK_NOPE_DIM = 128
QK_ROPE_DIM = 64
V_HEAD_DIM = 128
Q_LORA_RANK = 256
KV_LORA_RANK = 128
MLA_WIDTH = MLA_HEADS * V_HEAD_DIM
POOL_WIDTH = MIX_WIDTH - MLA_WIDTH
POOL_WINDOWS = (2, 4, 8, 16)
POOL_GROUPS = len(POOL_WINDOWS)
POOL_GROUP_DIM = POOL_WIDTH // POOL_GROUPS
IN_COLS = Q_LORA_RANK + KV_LORA_RANK + QK_ROPE_DIM + POOL_WIDTH
D_FF = ((8 * D_MODEL + 3 * 256 - 1) // (3 * 256)) * 256
ROPE_THETA = 10000.0
EPS = 1e-6
SM_SCALE = (QK_NOPE_DIM + QK_ROPE_DIM) ** -0.5
N_MOD = 6

kernel_name = "hybrid_mla_pool_adaln_block"


def rmsnorm(x, g):
    xf = x.astype(jnp.float32)
    y = xf * lax.rsqrt(jnp.mean(xf * xf, axis=-1, keepdims=True) + EPS)
    return (y * g.astype(jnp.float32)).astype(x.dtype)


def rope_tables(positions, dtype):
    half = QK_ROPE_DIM // 2
    freqs = jnp.power(ROPE_THETA, -jnp.arange(half, dtype=jnp.float32) / half)
    ang = positions.astype(jnp.float32)[..., None] * freqs
    return jnp.cos(ang).astype(dtype), jnp.sin(ang).astype(dtype)


def apply_rope(x, cos, sin):
    x1, x2 = jnp.split(x, 2, axis=-1)
    return jnp.concatenate([x1 * cos - x2 * sin, x1 * sin + x2 * cos], axis=-1)


def mla_mixer(cq_raw, ckv_raw, kr_raw, positions, g_q, g_kv, w_uq, w_uk, w_uv):
    B, S, _ = ckv_raw.shape
    c_q = rmsnorm(cq_raw, g_q)
    c_kv = rmsnorm(ckv_raw, g_kv)
    q = jnp.einsum('bsr,rhd->bshd', c_q, w_uq)
    q_nope, q_rope = q[..., :QK_NOPE_DIM], q[..., QK_NOPE_DIM:]
    cos, sin = rope_tables(positions, q.dtype)
    q_rope = apply_rope(q_rope, cos[:, :, None, :], sin[:, :, None, :])
    k_rope = apply_rope(kr_raw, cos, sin)
    q_lat = jnp.einsum('bshd,chd->bshc', q_nope, w_uk)
    nblk = S // Q_BLOCK
    q_lat_b = q_lat.reshape(B, nblk, Q_BLOCK, MLA_HEADS, KV_LORA_RANK).transpose(1, 0, 2, 3, 4)
    q_rope_b = q_rope.reshape(B, nblk, Q_BLOCK, MLA_HEADS, QK_ROPE_DIM).transpose(1, 0, 2, 3, 4)
    key_chunk = jnp.arange(S) // CHUNK

    def block(args):
        ql, qr, blk = args
        s = (jnp.einsum('bqhc,bkc->bhqk', ql, c_kv)
             + jnp.einsum('bqhr,bkr->bhqk', qr, k_rope)).astype(jnp.float32) * SM_SCALE
        q_chunk = (blk * Q_BLOCK + jnp.arange(Q_BLOCK)) // CHUNK
        mask = key_chunk[None, :] <= q_chunk[:, None]
        s = jnp.where(mask[None, None], s, -jnp.inf)
        p = jax.nn.softmax(s, axis=-1).astype(c_kv.dtype)
        o_lat = jnp.einsum('bhqk,bkc->bqhc', p, c_kv)
        o = jnp.einsum('bqhc,chv->bqhv', o_lat, w_uv)
        return o.reshape(B, Q_BLOCK, MLA_WIDTH)

    out = lax.map(block, (q_lat_b, q_rope_b, jnp.arange(nblk)))
    return out.transpose(1, 0, 2, 3).reshape(B, S, MLA_WIDTH)


def pool_mixer(u, w_pool, pool_scale):
    B, S, _ = u.shape
    ug = u.reshape(B, S, POOL_GROUPS, POOL_GROUP_DIM)
    cs = jnp.cumsum(ug.astype(jnp.float32), axis=1)
    t = jnp.arange(1, S + 1, dtype=jnp.float32)
    pooled = []
    for g, w in enumerate(POOL_WINDOWS):
        csg = cs[:, :, g]
        lagged = jnp.pad(csg, ((0, 0), (w, 0), (0, 0)))[:, :S]
        count = jnp.minimum(t, float(w))[None, :, None]
        pooled.append((csg - lagged) / count)
    pooled = jnp.stack(pooled, axis=2).astype(u.dtype) - ug
    y = jnp.einsum('bsgc,gcd->bsgd', pooled, w_pool).reshape(B, S, POOL_WIDTH)
    return y * pool_scale


def _fwd_setup_inputs(seed: int = 0) -> dict:
    key = jax.random.key(seed)
    ks = jax.random.split(key, 24)
    f32 = jnp.float32
    nrm = lambda k, shape, s: jax.random.normal(k, shape, f32) * s
    gain = lambda k, shape: 1.0 + 0.05 * jax.random.normal(k, shape, f32)
    x = jax.random.normal(ks[0], (BATCH, SEQ, D_MODEL), f32)
    c = jax.random.normal(ks[1], (BATCH, D_MODEL), f32)
    offset = jax.random.randint(ks[2], (BATCH, 1), 0, 8192, dtype=jnp.int32)
    positions = offset + jnp.arange(SEQ, dtype=jnp.int32)[None, :]
    return {
        "x": x,
        "c": c,
        "positions": positions,
        "w_ada": nrm(ks[3], (DEPTH, D_MODEL, N_MOD * D_MODEL), 0.5 * D_MODEL ** -0.5),
        "b_ada": nrm(ks[4], (DEPTH, N_MOD * D_MODEL), 0.02),
        "g_mix": gain(ks[5], (DEPTH, D_MODEL)),
        "w_in": nrm(ks[6], (DEPTH, D_MODEL, IN_COLS), D_MODEL ** -0.5),
        "g_q": gain(ks[7], (DEPTH, Q_LORA_RANK)),
        "g_kv": gain(ks[8], (DEPTH, KV_LORA_RANK)),
        "w_uq": nrm(ks[9], (DEPTH, Q_LORA_RANK, MLA_HEADS, QK_NOPE_DIM + QK_ROPE_DIM), Q_LORA_RANK ** -0.5),
        "w_uk": nrm(ks[10], (DEPTH, KV_LORA_RANK, MLA_HEADS, QK_NOPE_DIM), KV_LORA_RANK ** -0.5),
        "w_uv": nrm(ks[11], (DEPTH, KV_LORA_RANK, MLA_HEADS, V_HEAD_DIM), KV_LORA_RANK ** -0.5),
        "w_pool": nrm(ks[12], (DEPTH, POOL_GROUPS, POOL_GROUP_DIM, POOL_GROUP_DIM), POOL_GROUP_DIM ** -0.5),
        "pool_scale": gain(ks[13], (DEPTH, POOL_WIDTH)),
        "w_o": nrm(ks[14], (DEPTH, MIX_WIDTH, D_MODEL), MIX_WIDTH ** -0.5),
        "g_ffn": gain(ks[15], (DEPTH, D_MODEL)),
        "w_gate": nrm(ks[16], (DEPTH, D_MODEL, D_FF), D_MODEL ** -0.5),
        "w_up": nrm(ks[17], (DEPTH, D_MODEL, D_FF), D_MODEL ** -0.5),
        "w_down": nrm(ks[18], (DEPTH, D_FF, D_MODEL), D_FF ** -0.5),
        "g_final": gain(ks[19], (D_MODEL,)),
    }


def _fwd_reference(x, c, positions, w_ada, b_ada, g_mix, w_in, g_q, g_kv, w_uq, w_uk,
              w_uv, w_pool, pool_scale, w_o, g_ffn, w_gate, w_up, w_down, g_final):
    c_act = jax.nn.silu(c)
    for l in range(DEPTH):
        mod = (c_act @ w_ada[l] + b_ada[l])[:, None, :]
        sh1, sc1, gt1, sh2, sc2, gt2 = jnp.split(mod, N_MOD, axis=-1)

        h = rmsnorm(x, g_mix[l]) * (1.0 + sc1) + sh1
        proj = h @ w_in[l]
        o1 = Q_LORA_RANK
        o2 = o1 + KV_LORA_RANK
        o3 = o2 + QK_ROPE_DIM
        y_mla = mla_mixer(proj[..., :o1], proj[..., o1:o2], proj[..., o2:o3], positions,
                          g_q[l], g_kv[l], w_uq[l], w_uk[l], w_uv[l])
        y_pool = pool_mixer(proj[..., o3:], w_pool[l], pool_scale[l])
        mix = jnp.concatenate([y_mla, y_pool], axis=-1) @ w_o[l]
        x = x + gt1 * mix

        h = rmsnorm(x, g_ffn[l]) * (1.0 + sc2) + sh2
        ff = (jax.nn.silu(h @ w_gate[l]) * (h @ w_up[l])) @ w_down[l]
        x = x + gt2 * ff
    return rmsnorm(x, g_final)


import jax as _jax
import jax.numpy as _jnp

TWIN_FORMAT = 'train_step'
FWD_PARAMS = ['x', 'c', 'positions', 'w_ada', 'b_ada', 'g_mix', 'w_in', 'g_q', 'g_kv', 'w_uq', 'w_uk', 'w_uv', 'w_pool', 'pool_scale', 'w_o', 'g_ffn', 'w_gate', 'w_up', 'w_down', 'g_final']
TWIN_WEIGHTS = ['w_ada', 'b_ada', 'g_mix', 'w_in', 'g_q', 'g_kv', 'w_uq', 'w_uk', 'w_uv', 'w_pool', 'pool_scale', 'w_o', 'g_ffn', 'w_gate', 'w_up', 'w_down', 'g_final']
TWIN_DIFF_INPUT = 'x'
TWIN_INPUTS = ['x', 'c', 'positions', 'w_ada', 'b_ada', 'g_mix', 'w_in', 'g_q', 'g_kv', 'w_uq', 'w_uk', 'w_uv', 'w_pool', 'pool_scale', 'w_o', 'g_ffn', 'w_gate', 'w_up', 'w_down', 'g_final', 'loss_target', 'm_w_ada', 'm_b_ada', 'm_g_mix', 'm_w_in', 'm_g_q', 'm_g_kv', 'm_w_uq', 'm_w_uk', 'm_w_uv', 'm_w_pool', 'm_pool_scale', 'm_w_o', 'm_g_ffn', 'm_w_gate', 'm_w_up', 'm_w_down', 'm_g_final', 'v_w_ada', 'v_b_ada', 'v_g_mix', 'v_w_in', 'v_g_q', 'v_g_kv', 'v_w_uq', 'v_w_uk', 'v_w_uv', 'v_w_pool', 'v_pool_scale', 'v_w_o', 'v_g_ffn', 'v_w_gate', 'v_w_up', 'v_w_down', 'v_g_final']
TWIN_OUTPUTS = ['loss', 'grad_x', 'grad_w_ada', 'grad_b_ada', 'grad_g_mix', 'grad_w_in', 'grad_g_q', 'grad_g_kv', 'grad_w_uq', 'grad_w_uk', 'grad_w_uv', 'grad_w_pool', 'grad_pool_scale', 'grad_w_o', 'grad_g_ffn', 'grad_w_gate', 'grad_w_up', 'grad_w_down', 'grad_g_final', 'delta_w_ada', 'delta_b_ada', 'delta_g_mix', 'delta_w_in', 'delta_g_q', 'delta_g_kv', 'delta_w_uq', 'delta_w_uk', 'delta_w_uv', 'delta_w_pool', 'delta_pool_scale', 'delta_w_o', 'delta_g_ffn', 'delta_w_gate', 'delta_w_up', 'delta_w_down', 'delta_g_final', 'new_m_w_ada', 'new_m_b_ada', 'new_m_g_mix', 'new_m_w_in', 'new_m_g_q', 'new_m_g_kv', 'new_m_w_uq', 'new_m_w_uk', 'new_m_w_uv', 'new_m_w_pool', 'new_m_pool_scale', 'new_m_w_o', 'new_m_g_ffn', 'new_m_w_gate', 'new_m_w_up', 'new_m_w_down', 'new_m_g_final', 'new_v_w_ada', 'new_v_b_ada', 'new_v_g_mix', 'new_v_w_in', 'new_v_g_q', 'new_v_g_kv', 'new_v_w_uq', 'new_v_w_uk', 'new_v_w_uv', 'new_v_w_pool', 'new_v_pool_scale', 'new_v_w_o', 'new_v_g_ffn', 'new_v_w_gate', 'new_v_w_up', 'new_v_w_down', 'new_v_g_final']
TWIN_LEAF_KINDS = {'loss': 'loss', 'grad_x': 'grad_x', 'grad_w_ada': 'grad_w', 'grad_b_ada': 'grad_w', 'grad_g_mix': 'grad_w', 'grad_w_in': 'grad_w', 'grad_g_q': 'grad_w', 'grad_g_kv': 'grad_w', 'grad_w_uq': 'grad_w', 'grad_w_uk': 'grad_w', 'grad_w_uv': 'grad_w', 'grad_w_pool': 'grad_w', 'grad_pool_scale': 'grad_w', 'grad_w_o': 'grad_w', 'grad_g_ffn': 'grad_w', 'grad_w_gate': 'grad_w', 'grad_w_up': 'grad_w', 'grad_w_down': 'grad_w', 'grad_g_final': 'grad_w', 'delta_w_ada': 'delta_w', 'delta_b_ada': 'delta_w', 'delta_g_mix': 'delta_w', 'delta_w_in': 'delta_w', 'delta_g_q': 'delta_w', 'delta_g_kv': 'delta_w', 'delta_w_uq': 'delta_w', 'delta_w_uk': 'delta_w', 'delta_w_uv': 'delta_w', 'delta_w_pool': 'delta_w', 'delta_pool_scale': 'delta_w', 'delta_w_o': 'delta_w', 'delta_g_ffn': 'delta_w', 'delta_w_gate': 'delta_w', 'delta_w_up': 'delta_w', 'delta_w_down': 'delta_w', 'delta_g_final': 'delta_w', 'new_m_w_ada': 'new_m', 'new_m_b_ada': 'new_m', 'new_m_g_mix': 'new_m', 'new_m_w_in': 'new_m', 'new_m_g_q': 'new_m', 'new_m_g_kv': 'new_m', 'new_m_w_uq': 'new_m', 'new_m_w_uk': 'new_m', 'new_m_w_uv': 'new_m', 'new_m_w_pool': 'new_m', 'new_m_pool_scale': 'new_m', 'new_m_w_o': 'new_m', 'new_m_g_ffn': 'new_m', 'new_m_w_gate': 'new_m', 'new_m_w_up': 'new_m', 'new_m_w_down': 'new_m', 'new_m_g_final': 'new_m', 'new_v_w_ada': 'new_v', 'new_v_b_ada': 'new_v', 'new_v_g_mix': 'new_v', 'new_v_w_in': 'new_v', 'new_v_g_q': 'new_v', 'new_v_g_kv': 'new_v', 'new_v_w_uq': 'new_v', 'new_v_w_uk': 'new_v', 'new_v_w_uv': 'new_v', 'new_v_w_pool': 'new_v', 'new_v_pool_scale': 'new_v', 'new_v_w_o': 'new_v', 'new_v_g_ffn': 'new_v', 'new_v_w_gate': 'new_v', 'new_v_w_up': 'new_v', 'new_v_w_down': 'new_v', 'new_v_g_final': 'new_v'}


def _forward(args):
    return _fwd_reference(*[args[k] for k in FWD_PARAMS])


def _output_shape():
    out = _jax.eval_shape(lambda: _forward(_fwd_setup_inputs(0)))
    return out.shape, out.dtype

N_MICROBATCH = 1
ADAM_LR = 0.001
ADAM_B1 = 0.9
ADAM_B2 = 0.999
ADAM_EPS = 1e-08
ADAM_WD = 0.01
ADAM_STEP = 10
PER_EXAMPLE_BATCH_AXIS = {'x': 0, 'c': 0, 'positions': 0, 'loss_target': 0}
SHARED_INPUTS = []
_WEIGHT_DTYPES = {'w_ada': _jnp.float32, 'b_ada': _jnp.float32, 'g_mix': _jnp.float32, 'w_in': _jnp.float32, 'g_q': _jnp.float32, 'g_kv': _jnp.float32, 'w_uq': _jnp.float32, 'w_uk': _jnp.float32, 'w_uv': _jnp.float32, 'w_pool': _jnp.float32, 'pool_scale': _jnp.float32, 'w_o': _jnp.float32, 'g_ffn': _jnp.float32, 'w_gate': _jnp.float32, 'w_up': _jnp.float32, 'w_down': _jnp.float32, 'g_final': _jnp.float32}
MOMENT_SCALE = {'w_ada': 6.609831e-02, 'b_ada': 1.144177e-01, 'g_mix': 3.726753e-02, 'w_in': 4.017162e-02, 'g_q': 1.130255e-02, 'g_kv': 4.417573e-02, 'w_uq': 6.742302e-03, 'w_uk': 6.849222e-03, 'w_uv': 1.867457e-02, 'w_pool': 5.171079e-02, 'pool_scale': 5.217932e-02, 'w_o': 3.827799e-02, 'g_ffn': 5.680220e-02, 'w_gate': 2.404717e-02, 'w_up': 2.341400e-02, 'w_down': 3.854101e-02, 'g_final': 3.203493e+01}


def _to_microbatches(a, axis):
    t = _jnp.moveaxis(a, axis, 0)
    t = t.reshape((N_MICROBATCH, t.shape[0] // N_MICROBATCH) + t.shape[1:])
    return _jnp.moveaxis(t, 1, axis + 1)


def setup_inputs(seed: int = 0) -> dict:
    inp = _fwd_setup_inputs(seed)
    key = _jax.random.fold_in(_jax.random.key(seed), 7919)
    shape, _ = _output_shape()
    out = dict(inp)
    out["loss_target"] = _jax.random.normal(_jax.random.fold_in(key, 0), shape, _jnp.float32)
    for i, name in enumerate(TWIN_WEIGHTS):
        w = inp[name].astype(_jnp.float32)
        if MOMENT_SCALE is None:
            s = _jnp.sqrt(_jnp.mean(_jnp.square(w)) + 1e-30)
        else:
            s = MOMENT_SCALE[name]
        km, kv = _jax.random.split(_jax.random.fold_in(key, i + 1))
        out[name] = w
        out["m_" + name] = s * _jax.random.normal(km, w.shape, _jnp.float32)
        out["v_" + name] = (s * s) * _jax.random.uniform(kv, w.shape, _jnp.float32, 0.5, 1.5)
    if N_MICROBATCH > 1:
        for name, axis in PER_EXAMPLE_BATCH_AXIS.items():
            out[name] = _to_microbatches(out[name], axis)
    return {'x': out['x'], 'c': out['c'], 'positions': out['positions'], 'w_ada': out['w_ada'], 'b_ada': out['b_ada'], 'g_mix': out['g_mix'], 'w_in': out['w_in'], 'g_q': out['g_q'], 'g_kv': out['g_kv'], 'w_uq': out['w_uq'], 'w_uk': out['w_uk'], 'w_uv': out['w_uv'], 'w_pool': out['w_pool'], 'pool_scale': out['pool_scale'], 'w_o': out['w_o'], 'g_ffn': out['g_ffn'], 'w_gate': out['w_gate'], 'w_up': out['w_up'], 'w_down': out['w_down'], 'g_final': out['g_final'], 'loss_target': out['loss_target'], 'm_w_ada': out['m_w_ada'], 'm_b_ada': out['m_b_ada'], 'm_g_mix': out['m_g_mix'], 'm_w_in': out['m_w_in'], 'm_g_q': out['m_g_q'], 'm_g_kv': out['m_g_kv'], 'm_w_uq': out['m_w_uq'], 'm_w_uk': out['m_w_uk'], 'm_w_uv': out['m_w_uv'], 'm_w_pool': out['m_w_pool'], 'm_pool_scale': out['m_pool_scale'], 'm_w_o': out['m_w_o'], 'm_g_ffn': out['m_g_ffn'], 'm_w_gate': out['m_w_gate'], 'm_w_up': out['m_w_up'], 'm_w_down': out['m_w_down'], 'm_g_final': out['m_g_final'], 'v_w_ada': out['v_w_ada'], 'v_b_ada': out['v_b_ada'], 'v_g_mix': out['v_g_mix'], 'v_w_in': out['v_w_in'], 'v_g_q': out['v_g_q'], 'v_g_kv': out['v_g_kv'], 'v_w_uq': out['v_w_uq'], 'v_w_uk': out['v_w_uk'], 'v_w_uv': out['v_w_uv'], 'v_w_pool': out['v_w_pool'], 'v_pool_scale': out['v_pool_scale'], 'v_w_o': out['v_w_o'], 'v_g_ffn': out['v_g_ffn'], 'v_w_gate': out['v_w_gate'], 'v_w_up': out['v_w_up'], 'v_w_down': out['v_w_down'], 'v_g_final': out['v_g_final']}


def _loss(weights, diff, rest, loss_target):
    with _jax.named_scope("forward"):
        args = {**rest, TWIN_DIFF_INPUT: diff, **{k: w.astype(_WEIGHT_DTYPES[k]) for k, w in weights.items()}}
        y = _forward(args)
    with _jax.named_scope("loss_head"):
        err = _jnp.square(y.astype(_jnp.float32) - loss_target)
        return 0.5 * _jnp.sum(_jnp.mean(err, axis=-1)) if err.ndim else 0.5 * err


def _adamw(w, g, m, v):
    m = ADAM_B1 * m + (1.0 - ADAM_B1) * g
    v = ADAM_B2 * v + (1.0 - ADAM_B2) * _jnp.square(g)
    m_hat = m / (1.0 - ADAM_B1 ** ADAM_STEP)
    v_hat = v / (1.0 - ADAM_B2 ** ADAM_STEP)
    delta = -ADAM_LR * (m_hat / (_jnp.sqrt(v_hat) + ADAM_EPS) + ADAM_WD * w)
    return delta, m, v


def reference(x, c, positions, w_ada, b_ada, g_mix, w_in, g_q, g_kv, w_uq, w_uk, w_uv, w_pool, pool_scale, w_o, g_ffn, w_gate, w_up, w_down, g_final, loss_target, m_w_ada, m_b_ada, m_g_mix, m_w_in, m_g_q, m_g_kv, m_w_uq, m_w_uk, m_w_uv, m_w_pool, m_pool_scale, m_w_o, m_g_ffn, m_w_gate, m_w_up, m_w_down, m_g_final, v_w_ada, v_b_ada, v_g_mix, v_w_in, v_g_q, v_g_kv, v_w_uq, v_w_uk, v_w_uv, v_w_pool, v_pool_scale, v_w_o, v_g_ffn, v_w_gate, v_w_up, v_w_down, v_g_final):
    given = dict(x=x, c=c, positions=positions, w_ada=w_ada, b_ada=b_ada, g_mix=g_mix, w_in=w_in, g_q=g_q, g_kv=g_kv, w_uq=w_uq, w_uk=w_uk, w_uv=w_uv, w_pool=w_pool, pool_scale=pool_scale, w_o=w_o, g_ffn=g_ffn, w_gate=w_gate, w_up=w_up, w_down=w_down, g_final=g_final, loss_target=loss_target, m_w_ada=m_w_ada, m_b_ada=m_b_ada, m_g_mix=m_g_mix, m_w_in=m_w_in, m_g_q=m_g_q, m_g_kv=m_g_kv, m_w_uq=m_w_uq, m_w_uk=m_w_uk, m_w_uv=m_w_uv, m_w_pool=m_w_pool, m_pool_scale=m_pool_scale, m_w_o=m_w_o, m_g_ffn=m_g_ffn, m_w_gate=m_w_gate, m_w_up=m_w_up, m_w_down=m_w_down, m_g_final=m_g_final, v_w_ada=v_w_ada, v_b_ada=v_b_ada, v_g_mix=v_g_mix, v_w_in=v_w_in, v_g_q=v_g_q, v_g_kv=v_g_kv, v_w_uq=v_w_uq, v_w_uk=v_w_uk, v_w_uv=v_w_uv, v_w_pool=v_w_pool, v_pool_scale=v_pool_scale, v_w_o=v_w_o, v_g_ffn=v_g_ffn, v_w_gate=v_w_gate, v_w_up=v_w_up, v_w_down=v_w_down, v_g_final=v_g_final)
    weights = {n: given[n] for n in TWIN_WEIGHTS}
    shared = {n: given[n] for n in SHARED_INPUTS}
    per_example = {n: given[n] for n in ['x', 'c', 'positions']}
    grad_fn = _jax.value_and_grad(_loss, argnums=(0, 1))

    def one_microbatch(ex, loss_target):
        ex = dict(ex)
        diff = ex.pop(TWIN_DIFF_INPUT)
        return grad_fn(weights, diff, {**shared, **ex}, loss_target)

    if N_MICROBATCH == 1:
        loss, (grad_w, grad_x) = one_microbatch(per_example, given["loss_target"])
    else:
        def body(carry, xs):
            loss_sum, grad_sum = carry
            l_k, (gw_k, gx_k) = one_microbatch(xs[0], xs[1])
            with _jax.named_scope("update"):
                return (loss_sum + l_k, _jax.tree.map(_jnp.add, grad_sum, gw_k)), gx_k

        init = (_jnp.zeros((), _jnp.float32), _jax.tree.map(_jnp.zeros_like, weights))
        (loss, grad_w), grad_x = _jax.lax.scan(body, init, (per_example, given["loss_target"]))
    with _jax.named_scope("update"):
        delta_w, new_m, new_v = {}, {}, {}
        for n in TWIN_WEIGHTS:
            delta_w[n], new_m[n], new_v[n] = _adamw(weights[n], grad_w[n], given["m_" + n], given["v_" + n])
    return (loss, grad_x, *[grad_w[n] for n in TWIN_WEIGHTS], *[delta_w[n] for n in TWIN_WEIGHTS],
            *[new_m[n] for n in TWIN_WEIGHTS], *[new_v[n] for n in TWIN_WEIGHTS])
```

```python
import functools

import jax
import jax.numpy as jnp
from jax import lax
from jax.experimental import pallas as pl
from jax.experimental.pallas import tpu as pltpu

F32 = jnp.float32
BF16 = jnp.bfloat16

D_MODEL = 1024
HEADS = 4
NOPE = 128
ROPE = 64
HEAD_QK = NOPE + ROPE
Q_LORA = 256
KV_LORA = 128
POOL_W = 512
POOL_WINDOWS = (2, 4, 8, 16)
POOL_GROUP = 128
POOL_PAD = 16
D_FF = 2816
N_CHIPS = 4
FF_CHUNK = D_FF // N_CHIPS
N_MOD = 6
EPS = 1e-6
SM_SCALE = HEAD_QK ** -0.5
ROPE_THETA = 10000.0
QK_PAD = 256
CHUNK = 64
CHUNK_SHIFT = 6

ADAM_LR = 0.001
ADAM_B1 = 0.9
ADAM_B2 = 0.999
ADAM_EPS = 1e-08
ADAM_WD = 0.01
ADAM_STEP = 10

VMEM_LIMIT = 48 * 1024 * 1024
MESH = pl.DeviceIdType.MESH
ANY = pl.BlockSpec(memory_space=pl.ANY)
VMEM_SPEC = pl.BlockSpec(memory_space=pltpu.VMEM)

PROJ_W = 1024
O_CKV = 256
O_KR = 384
O_U = 512
Q_W = 768
O_QA = 512
O_QB = 640


def _params(sem=None, vmem=VMEM_LIMIT):
    kw = dict(vmem_limit_bytes=vmem)
    if sem is not None:
        kw["dimension_semantics"] = sem
    return pltpu.CompilerParams(**kw)


def _dot(a, b):
    return jnp.dot(a.astype(BF16), b.astype(BF16), preferred_element_type=F32)


def _dot_nt(a, b):
    return lax.dot_general(a.astype(BF16), b.astype(BF16), (((1,), (1,)), ((), ())), preferred_element_type=F32)


def _dot_tn(a, b):
    return lax.dot_general(a.astype(BF16), b.astype(BF16), (((0,), (0,)), ((), ())), preferred_element_type=F32)


def _row_tile(rows, target):
    best = rows
    for t in range(8, min(rows, target) + 1, 8):
        if rows % t == 0:
            best = t
    return best if rows % best == 0 and best <= target else rows


def _rms(x):
    r = lax.rsqrt(jnp.mean(x * x, axis=-1, keepdims=True) + EPS)
    return x * r, r


def _rms_bwd(dxh, xh, r):
    return r * (dxh - xh * jnp.mean(dxh * xh, axis=-1, keepdims=True))


def _lane_first_half(shape):
    lane = lax.broadcasted_iota(jnp.int32, shape, 1)
    return (lane & (ROPE - 1)) < (ROPE // 2)


def _rope(a, cos, sin):
    first = _lane_first_half(a.shape)
    up = pltpu.roll(a, 96, 1)
    dn = pltpu.roll(a, 32, 1)
    return a * cos + jnp.where(first, -up, dn) * sin


def _rope_bwd(d, cos, sin):
    first = _lane_first_half(d.shape)
    up = pltpu.roll(d, 96, 1)
    dn = pltpu.roll(d, 32, 1)
    return d * cos + jnp.where(first, up, -dn) * sin


RELATIONS = tuple((dx, dy, dc) for dx in (0, 1) for dy in (0, 1) for dc in (0, 1) if (dx, dy, dc) != (0, 0, 0))
CHIP_RELATIONS = ((1, 0), (0, 1), (1, 1))


def _flip(v, d):
    return 1 - v if d else v


def _place():
    return lax.axis_index("x"), lax.axis_index("y"), lax.axis_index("c")


def _remote(src, dst, send_sem, recv_sem, target):
    return pltpu.make_async_remote_copy(src_ref=src, dst_ref=dst, send_sem=send_sem, recv_sem=recv_sem,
                                        device_id=target, device_id_type=MESH)


def _mod_exchange(c_row, w_ada, b_ada):
    cols = w_ada.shape[1]

    def body(c_ref, w_ref, b_ref, mod_ref, call_ref, part_ref, send1, recv1, loc1, send2, recv2, loc2):
        x, y, c = _place()
        me = 4 * x + 2 * y + c
        own = pltpu.make_async_copy(c_ref, call_ref.at[pl.ds(me, 1)], loc1)
        own.start()
        sends = []
        for k, (dx, dy, dc) in enumerate(RELATIONS):
            cp = _remote(c_ref, call_ref.at[pl.ds(me, 1)], send1.at[k], recv1.at[k],
                         (_flip(x, dx), _flip(y, dy), _flip(c, dc)))
            cp.start()
            sends.append(cp)
        for k, (dx, dy, dc) in enumerate(RELATIONS):
            src = 4 * _flip(x, dx) + 2 * _flip(y, dy) + _flip(c, dc)
            _remote(c_ref, call_ref.at[pl.ds(src, 1)], send1.at[k], recv1.at[k], (x, y, c)).wait_recv()
        own.wait()
        for cp in sends:
            cp.wait_send()
        call = call_ref[...]
        act = call * jax.nn.sigmoid(call)
        part_ref[...] = _dot(act, w_ref[...]) + b_ref[...]
        chip = 2 * x + y
        mine = pltpu.make_async_copy(part_ref.at[pl.ds(me, 1)], mod_ref.at[pl.ds(chip, 1)], loc2)
        mine.start()
        sends = []
        for k, (dx, dy) in enumerate(CHIP_RELATIONS):
            tx, ty = _flip(x, dx), _flip(y, dy)
            tb = 4 * tx + 2 * ty + c
            cp = _remote(part_ref.at[pl.ds(tb, 1)], mod_ref.at[pl.ds(chip, 1)], send2.at[k], recv2.at[k], (tx, ty, c))
            cp.start()
            sends.append(cp)
        for k, (dx, dy) in enumerate(CHIP_RELATIONS):
            src_chip = 2 * _flip(x, dx) + _flip(y, dy)
            _remote(part_ref.at[pl.ds(me, 1)], mod_ref.at[pl.ds(src_chip, 1)], send2.at[k], recv2.at[k],
                    (x, y, c)).wait_recv()
        mine.wait()
        for cp in sends:
            cp.wait_send()

    return pl.pallas_call(
        body, name="mod_exchange",
        out_shape=[jax.ShapeDtypeStruct((N_CHIPS, cols), F32), jax.ShapeDtypeStruct((8, D_MODEL), F32)],
        in_specs=[VMEM_SPEC, VMEM_SPEC, VMEM_SPEC], out_specs=[VMEM_SPEC, VMEM_SPEC],
        scratch_shapes=[pltpu.VMEM((8, cols), F32),
                        pltpu.SemaphoreType.DMA((7,)), pltpu.SemaphoreType.DMA((7,)), pltpu.SemaphoreType.DMA,
                        pltpu.SemaphoreType.DMA((3,)), pltpu.SemaphoreType.DMA((3,)), pltpu.SemaphoreType.DMA],
        compiler_params=_params(),
    )(c_row, w_ada, b_ada)


def _weight_gather(shards):
    n = len(shards)

    def body(*refs):
        ins, outs = refs[:n], refs[n:2 * n]
        send_sems, recv_sems, loc_sems = refs[2 * n:]
        x, y, c = _place()
        chip = 2 * x + y
        local = []
        for w in range(n):
            cp = pltpu.make_async_copy(ins[w], outs[w].at[chip], loc_sems.at[w])
            cp.start()
            local.append(cp)
        sends = []
        for w in range(n):
            hr = ins[w].shape[0] // 2
            half = pl.ds(c * hr, hr)
            for k, (dx, dy) in enumerate(CHIP_RELATIONS):
                cp = _remote(ins[w].at[half], outs[w].at[chip, half], send_sems.at[w, k], recv_sems.at[w, k],
                             (_flip(x, dx), _flip(y, dy), c))
                cp.start()
                sends.append(cp)
        for w in range(n):
            hr = ins[w].shape[0] // 2
            half = pl.ds(c * hr, hr)
            for k, (dx, dy) in enumerate(CHIP_RELATIONS):
                src_chip = 2 * _flip(x, dx) + _flip(y, dy)
                got = outs[w].at[src_chip, half]
                _remote(got, got, send_sems.at[w, k], recv_sems.at[w, k], (x, y, c)).wait_recv()
                cp = _remote(got, got, send_sems.at[w, 3 + k], recv_sems.at[w, 3 + k], (x, y, 1 - c))
                cp.start()
                sends.append(cp)
        for w in range(n):
            hr = ins[w].shape[0] // 2
            other = pl.ds((1 - c) * hr, hr)
            for k, (dx, dy) in enumerate(CHIP_RELATIONS):
                src_chip = 2 * _flip(x, dx) + _flip(y, dy)
                got = outs[w].at[src_chip, other]
                _remote(got, got, send_sems.at[w, 3 + k], recv_sems.at[w, 3 + k], (x, y, c)).wait_recv()
        for cp in sends:
            cp.wait_send()
        for cp in local:
            cp.wait()

    return pl.pallas_call(
        body, name="weight_gather",
        out_shape=[jax.ShapeDtypeStruct((N_CHIPS,) + s.shape, s.dtype) for s in shards],
        in_specs=[ANY] * n, out_specs=[ANY] * n,
        scratch_shapes=[pltpu.SemaphoreType.DMA((n, 6)), pltpu.SemaphoreType.DMA((n, 6)),
                        pltpu.SemaphoreType.DMA((n,))],
        compiler_params=_params(),
    )(*shards)


def _grad_swap_halves(grads, dmod):
    n = len(grads)

    def body(*refs):
        ins, dmod_ref = refs[:n], refs[n]
        outs, dall_ref = refs[n + 1:2 * n + 1], refs[2 * n + 1]
        send_sems, recv_sems, dsend, drecv, dloc = refs[2 * n + 2:]
        x, y, c = _place()
        me = 4 * x + 2 * y + c
        sends = []
        for w in range(n):
            hr = ins[w].shape[1] // 2
            cp = _remote(ins[w].at[:, pl.ds((1 - c) * hr, hr), :], outs[w], send_sems.at[w], recv_sems.at[w],
                         (x, y, 1 - c))
            cp.start()
            sends.append(cp)
        own = pltpu.make_async_copy(dmod_ref, dall_ref.at[me], dloc)
        own.start()
        for k, (dx, dy, dc) in enumerate(RELATIONS):
            cp = _remote(dmod_ref, dall_ref.at[me], dsend.at[k], drecv.at[k],
                         (_flip(x, dx), _flip(y, dy), _flip(c, dc)))
            cp.start()
            sends.append(cp)
        for k, (dx, dy, dc) in enumerate(RELATIONS):
            src = 4 * _flip(x, dx) + 2 * _flip(y, dy) + _flip(c, dc)
            _remote(dmod_ref, dall_ref.at[src], dsend.at[k], drecv.at[k], (x, y, c)).wait_recv()
        for w in range(n):
            _remote(outs[w], outs[w], send_sems.at[w], recv_sems.at[w], (x, y, c)).wait_recv()
        own.wait()
        for cp in sends:
            cp.wait_send()

    out_shape = [jax.ShapeDtypeStruct((N_CHIPS, g.shape[1] // 2, g.shape[2]), F32) for g in grads]
    out_shape.append(jax.ShapeDtypeStruct((8,) + dmod.shape, F32))
    res = pl.pallas_call(
        body, name="grad_swap_halves",
        out_shape=out_shape, in_specs=[ANY] * (n + 1), out_specs=[ANY] * (n + 1),
        scratch_shapes=[pltpu.SemaphoreType.DMA((n,)), pltpu.SemaphoreType.DMA((n,)),
                        pltpu.SemaphoreType.DMA((7,)), pltpu.SemaphoreType.DMA((7,)), pltpu.SemaphoreType.DMA],
        compiler_params=_params(),
    )(*grads, dmod)
    return res[:n], res[n]


def _grad_chip_exchange(sums):
    n = len(sums)

    def body(*refs):
        ins, outs = refs[:n], refs[n:2 * n]
        send_sems, recv_sems = refs[2 * n:]
        x, y, c = _place()
        sends = []
        for w in range(n):
            for k, (dx, dy) in enumerate(CHIP_RELATIONS):
                tx, ty = _flip(x, dx), _flip(y, dy)
                cp = _remote(ins[w].at[2 * tx + ty], outs[w].at[k], send_sems.at[w, k], recv_sems.at[w, k], (tx, ty, c))
                cp.start()
                sends.append(cp)
        for w in range(n):
            for k in range(3):
                _remote(outs[w].at[k], outs[w].at[k], send_sems.at[w, k], recv_sems.at[w, k], (x, y, c)).wait_recv()
        for cp in sends:
            cp.wait_send()

    return pl.pallas_call(
        body, name="grad_chip_exchange",
        out_shape=[jax.ShapeDtypeStruct((3,) + s.shape[1:], F32) for s in sums],
        in_specs=[ANY] * n, out_specs=[ANY] * n,
        scratch_shapes=[pltpu.SemaphoreType.DMA((n, 3)), pltpu.SemaphoreType.DMA((n, 3))],
        compiler_params=_params(),
    )(*sums)


def _grad_finish(halves, small_half):
    n = len(halves)

    def body(*refs):
        ins, sm_ref = refs[:n], refs[n]
        outs, sall_ref = refs[n + 1:2 * n + 1], refs[2 * n + 1]
        send_sems, recv_sems, loc_sems, ssend, srecv, sloc = refs[2 * n + 2:]
        x, y, c = _place()
        chip = 2 * x + y
        local, sends = [], []
        for w in range(n):
            cp = pltpu.make_async_copy(ins[w], outs[w].at[c], loc_sems.at[w])
            cp.start()
            local.append(cp)
            cp = _remote(ins[w], outs[w].at[c], send_sems.at[w], recv_sems.at[w], (x, y, 1 - c))
            cp.start()
            sends.append(cp)
        cp = pltpu.make_async_copy(sm_ref, sall_ref.at[chip, c], sloc)
        cp.start()
        local.append(cp)
        for k, (dx, dy, dc) in enumerate(RELATIONS):
            cp = _remote(sm_ref, sall_ref.at[chip, c], ssend.at[k], srecv.at[k],
                         (_flip(x, dx), _flip(y, dy), _flip(c, dc)))
            cp.start()
            sends.append(cp)
        for k, (dx, dy, dc) in enumerate(RELATIONS):
            got = sall_ref.at[2 * _flip(x, dx) + _flip(y, dy), _flip(c, dc)]
            _remote(got, got, ssend.at[k], srecv.at[k], (x, y, c)).wait_recv()
        for w in range(n):
            got = outs[w].at[1 - c]
            _remote(got, got, send_sems.at[w], recv_sems.at[w], (x, y, c)).wait_recv()
        for cp in sends:
            cp.wait_send()
        for cp in local:
            cp.wait()

    out_shape = [jax.ShapeDtypeStruct((2,) + h.shape, F32) for h in halves]
    out_shape.append(jax.ShapeDtypeStruct((N_CHIPS, 2) + small_half.shape, F32))
    res = pl.pallas_call(
        body, name="grad_finish",
        out_shape=out_shape, in_specs=[ANY] * (n + 1), out_specs=[ANY] * (n + 1),
        scratch_shapes=[pltpu.SemaphoreType.DMA((n,)), pltpu.SemaphoreType.DMA((n,)), pltpu.SemaphoreType.DMA((n,)),
                        pltpu.SemaphoreType.DMA((7,)), pltpu.SemaphoreType.DMA((7,)), pltpu.SemaphoreType.DMA],
        compiler_params=_params(),
    )(*halves, small_half)
    return res[:n], res[n]


def _add_my_half(core, full, got, name):
    _, hr, cols = got.shape

    def body(core_ref, a_ref, b_ref, o_ref):
        o_ref[...] = a_ref[...] + b_ref[...]

    return pl.pallas_call(
        body, name=name,
        out_shape=jax.ShapeDtypeStruct(got.shape, F32),
        grid_spec=pltpu.PrefetchScalarGridSpec(
            num_scalar_prefetch=1, grid=(N_CHIPS,),
            in_specs=[pl.BlockSpec((None, hr, cols), lambda s, core_ref: (s, core_ref[0], 0)),
                      pl.BlockSpec((None, hr, cols), lambda s, core_ref: (s, 0, 0))],
            out_specs=pl.BlockSpec((None, hr, cols), lambda s, core_ref: (s, 0, 0))),
        compiler_params=_params(("arbitrary",)),
    )(core, full, got)


def _add_chips(chip, mine, got, name):
    _, hr, cols = mine.shape

    def body(chip_ref, a_ref, b_ref, o_ref):
        o_ref[...] = ((a_ref[...] + b_ref[0]) + b_ref[1]) + b_ref[2]

    return pl.pallas_call(
        body, name=name,
        out_shape=jax.ShapeDtypeStruct((hr, cols), F32),
        grid_spec=pltpu.PrefetchScalarGridSpec(
            num_scalar_prefetch=1, grid=(1,),
            in_specs=[pl.BlockSpec((None, hr, cols), lambda s, chip_ref: (chip_ref[0], 0, 0)),
                      pl.BlockSpec((3, hr, cols), lambda s, chip_ref: (0, 0, 0))],
            out_specs=pl.BlockSpec((hr, cols), lambda s, chip_ref: (0, 0))),
        compiler_params=_params(("arbitrary",)),
    )(chip, mine, got)


def _rope_tables(pos_col, freqs):
    S = pos_col.shape[0]
    T = _row_tile(S, 1024)

    def body(p_ref, f_ref, cos_ref, sin_ref):
        ang = p_ref[...].astype(F32) * f_ref[...]
        cos_ref[...] = jnp.cos(ang)
        sin_ref[...] = jnp.sin(ang)

    return pl.pallas_call(
        body, name="rope_tables", grid=(S // T,),
        out_shape=[jax.ShapeDtypeStruct((S, 128), F32)] * 2,
        in_specs=[pl.BlockSpec((T, 1), lambda i: (i, 0)), pl.BlockSpec((1, 128), lambda i: (0, 0))],
        out_specs=[pl.BlockSpec((T, 128), lambda i: (i, 0))] * 2,
        compiler_params=_params(("parallel",)),
    )(pos_col, freqs)


def _full(shape):
    zeros = (0,) * len(shape)
    return pl.BlockSpec(shape, lambda *_: zeros)


def _pre_attention(x, mod6, g_mix, g_q, g_kv, w_in, w_uq, w_uk_t, cos, sin, T):
    S = x.shape[0]

    def body(x_ref, mod_ref, gm_ref, gq_ref, gkv_ref, win_ref, wuq_ref, wuk_ref, cos_ref, sin_ref,
             proj_ref, q_ref, qc_ref, kc_ref):
        xh, _ = _rms(x_ref[...])
        h1 = (xh * gm_ref[...]) * (1.0 + mod_ref[1:2, :]) + mod_ref[0:1, :]
        proj = _dot(h1, win_ref[...])
        proj_ref[...] = proj
        cqh, _ = _rms(proj[:, :Q_LORA])
        c_q = cqh * gq_ref[...]
        ckvh, _ = _rms(proj[:, O_CKV:O_KR])
        c_kv = ckvh * gkv_ref[...]
        q = _dot(c_q, wuq_ref[...])
        q_ref[...] = q
        cos_t, sin_t = cos_ref[...], sin_ref[...]
        ropes = (_rope(q[:, O_QA:O_QB], cos_t, sin_t), _rope(q[:, O_QB:Q_W], cos_t, sin_t))
        low = lax.broadcasted_iota(jnp.int32, (T, 128), 1) < ROPE
        for h in range(HEADS):
            q_lat = _dot_nt(q[:, h * NOPE:(h + 1) * NOPE], wuk_ref[h])
            keep = low if h % 2 == 0 else jnp.logical_not(low)
            qc_ref[h, :, 0:KV_LORA] = q_lat.astype(BF16)
            qc_ref[h, :, KV_LORA:QK_PAD] = jnp.where(keep, ropes[h // 2], 0.0).astype(BF16)
        kc_ref[:, 0:KV_LORA] = c_kv.astype(BF16)
        kc_ref[:, KV_LORA:QK_PAD] = _rope(proj[:, O_KR:O_U], cos_t, sin_t).astype(BF16)

    row = lambda w: pl.BlockSpec((T, w), lambda i: (i, 0))
    return pl.pallas_call(
        body, name="pre_attention", grid=(S // T,),
        out_shape=[jax.ShapeDtypeStruct((S, PROJ_W), F32), jax.ShapeDtypeStruct((S, Q_W), F32),
                   jax.ShapeDtypeStruct((HEADS, S, QK_PAD), BF16), jax.ShapeDtypeStruct((S, QK_PAD), BF16)],
        in_specs=[row(D_MODEL), _full((N_MOD, D_MODEL)), _full((1, D_MODEL)), _full((1, Q_LORA)), _full((1, KV_LORA)),
                  _full((D_MODEL, PROJ_W)), _full((Q_LORA, Q_W)), _full((HEADS, KV_LORA, NOPE)), row(128), row(128)],
        out_specs=[row(PROJ_W), row(Q_W), pl.BlockSpec((HEADS, T, QK_PAD), lambda i: (0, i, 0)), row(QK_PAD)],
        compiler_params=_params(("parallel",)),
    )(x, mod6, g_mix, g_q, g_kv, w_in, w_uq, w_uk_t, cos, sin)


def _diag_mask(TQ, transposed):
    R = HEADS * TQ
    if transposed:
        key = lax.broadcasted_iota(jnp.int32, (TQ, R), 0) >> CHUNK_SHIFT
        qry = (lax.broadcasted_iota(jnp.int32, (TQ, R), 1) & (TQ - 1)) >> CHUNK_SHIFT
    else:
        qry = (lax.broadcasted_iota(jnp.int32, (R, TQ), 0) & (TQ - 1)) >> CHUNK_SHIFT
        key = lax.broadcasted_iota(jnp.int32, (R, TQ), 1) >> CHUNK_SHIFT
    return key <= qry


def _col_to_row(col):
    return jnp.transpose(jnp.broadcast_to(col, (col.shape[0], 128)))[0:1, :]


def _attention_fwd(qc, kc, w_uv_t, TQ):
    S = kc.shape[0]
    R = HEADS * TQ
    nq = S // TQ

    def body(q_ref, k_ref, wuv_ref, o_ref, y_ref, lse_ref, lser_ref, m_s, l_s, acc_s):
        i = pl.program_id(0)
        q = q_ref[...].reshape(R, QK_PAD)
        m_s[...] = jnp.full((R, 1), -jnp.inf, F32)
        l_s[...] = jnp.zeros((R, 1), F32)
        acc_s[...] = jnp.zeros((R, KV_LORA), F32)

        def step(j, masked):
            k = k_ref[pl.ds(pl.multiple_of(j * TQ, TQ), TQ), :]
            s = _dot_nt(q, k) * SM_SCALE
            if masked:
                s = jnp.where(_diag_mask(TQ, False), s, -jnp.inf)
            m_old = m_s[...]
            m_new = jnp.maximum(m_old, jnp.max(s, axis=1, keepdims=True))
            p = jnp.exp(s - m_new)
            alpha = jnp.exp(m_old - m_new)
            l_s[...] = alpha * l_s[...] + jnp.sum(p, axis=1, keepdims=True)
            acc_s[...] = alpha * acc_s[...] + _dot(p, k[:, :KV_LORA])
            m_s[...] = m_new

        def loop(j, carry):
            step(j, False)
            return carry

        lax.fori_loop(0, i, loop, 0)
        step(i, True)
        l = l_s[...]
        o = (acc_s[...] / l).astype(BF16)
        lse = m_s[...] + jnp.log(l)
        lse_ref[...] = lse.reshape(HEADS, TQ, 1)
        lser_ref[0] = _col_to_row(lse)
        for h in range(HEADS):
            oh = o[h * TQ:(h + 1) * TQ, :]
            o_ref[h] = oh
            y_ref[:, h * 128:(h + 1) * 128] = _dot(oh, wuv_ref[h]).astype(BF16)

    return pl.pallas_call(
        body, name="attention_fwd", grid=(nq,),
        out_shape=[jax.ShapeDtypeStruct((HEADS, S, KV_LORA), BF16), jax.ShapeDtypeStruct((S, HEADS * 128), BF16),
                   jax.ShapeDtypeStruct((HEADS, S, 1), F32), jax.ShapeDtypeStruct((nq, 1, R), F32)],
        in_specs=[pl.BlockSpec((HEADS, TQ, QK_PAD), lambda i: (0, i, 0)), _full((S, QK_PAD)),
                  _full((HEADS, KV_LORA, 128))],
        out_specs=[pl.BlockSpec((HEADS, TQ, KV_LORA), lambda i: (0, i, 0)), pl.BlockSpec((TQ, HEADS * 128), lambda i: (i, 0)),
                   pl.BlockSpec((HEADS, TQ, 1), lambda i: (0, i, 0)), pl.BlockSpec((1, 1, R), lambda i: (i, 0, 0))],
        scratch_shapes=[pltpu.VMEM((R, 1), F32), pltpu.VMEM((R, 1), F32), pltpu.VMEM((R, KV_LORA), F32)],
        compiler_params=_params(("parallel",)),
    )(qc, kc, w_uv_t)


def _pool_forward(proj):
    S = proj.shape[0]
    RB = _row_tile(S, 256)

    def body(proj_ref, out_ref, pad_ref, sem):
        cp = pltpu.make_async_copy(proj_ref.at[:, pl.ds(O_U, POOL_W)], pad_ref.at[pl.ds(POOL_PAD, S)], sem)
        cp.start()
        pad_ref[0:POOL_PAD, :] = jnp.zeros((POOL_PAD, POOL_W), F32)
        cp.wait()
        for g, win in enumerate(POOL_WINDOWS):
            cols = slice(g * POOL_GROUP, (g + 1) * POOL_GROUP)
            for r0 in range(0, S, RB):
                u = pad_ref[POOL_PAD + r0:POOL_PAD + r0 + RB, cols]
                acc = u
                for k in range(1, win):
                    acc = acc + pad_ref[POOL_PAD + r0 - k:POOL_PAD + r0 - k + RB, cols]
                if r0 == 0:
                    t1 = (lax.broadcasted_iota(jnp.int32, (RB, POOL_GROUP), 0) + 1).astype(F32)
                    mean = acc / jnp.minimum(t1, float(win))
                else:
                    mean = acc * (1.0 / win)
                out_ref[r0:r0 + RB, cols] = (mean - u).astype(BF16)

    return pl.pallas_call(
        body, name="pool_forward",
        out_shape=jax.ShapeDtypeStruct((S, POOL_W), BF16),
        in_specs=[ANY], out_specs=VMEM_SPEC,
        scratch_shapes=[pltpu.VMEM((S + POOL_PAD, POOL_W), F32), pltpu.SemaphoreType.DMA],
        compiler_params=_params(),
    )(proj)


def _pool_backward(dpooled):
    S = dpooled.shape[0]
    RB = _row_tile(S, 256)

    def body(dp_ref, out_ref, pad_ref, sem):
        cp = pltpu.make_async_copy(dp_ref, pad_ref.at[pl.ds(0, S)], sem)
        cp.start()
        pad_ref[S:S + POOL_PAD, :] = jnp.zeros((POOL_PAD, POOL_W), F32)
        cp.wait()
        for g, win in enumerate(POOL_WINDOWS):
            cols = slice(g * POOL_GROUP, (g + 1) * POOL_GROUP)
            head = pad_ref[0:POOL_PAD, cols]
            t1 = (lax.broadcasted_iota(jnp.int32, (POOL_PAD, POOL_GROUP), 0) + 1).astype(F32)
            pad_ref[0:POOL_PAD, cols] = head * (float(win) / jnp.minimum(t1, float(win)))
            for r0 in range(0, S, RB):
                acc = pad_ref[r0:r0 + RB, cols]
                for k in range(1, win):
                    acc = acc + pad_ref[r0 + k:r0 + k + RB, cols]
                own = pad_ref[r0:r0 + RB, cols]
                if r0 == 0:
                    own = jnp.concatenate([head, own[POOL_PAD:]], axis=0)
                out_ref[r0:r0 + RB, cols] = acc * (1.0 / win) - own

    return pl.pallas_call(
        body, name="pool_backward",
        out_shape=jax.ShapeDtypeStruct((S, POOL_W), F32),
        in_specs=[ANY], out_specs=VMEM_SPEC,
        scratch_shapes=[pltpu.VMEM((S + POOL_PAD, POOL_W), F32), pltpu.SemaphoreType.DMA],
        compiler_params=_params(),
    )(dpooled)


def _mix_out(y_mla, pooled, w_pool, pool_scale, w_o, x, mod6, T):
    S = x.shape[0]

    def body(ym_ref, pl_ref, wp_ref, ps_ref, wo_ref, x_ref, mod_ref, x1_ref, mix_ref, mi_ref):
        mi_ref[:, 0:512] = ym_ref[...]
        for g in range(len(POOL_WINDOWS)):
            cols = slice(g * POOL_GROUP, (g + 1) * POOL_GROUP)
            z = _dot(pl_ref[:, cols], wp_ref[g])
            mi_ref[:, 512 + g * POOL_GROUP:512 + (g + 1) * POOL_GROUP] = (z * ps_ref[:, cols]).astype(BF16)
        mix = _dot(mi_ref[...], wo_ref[...])
        mix_ref[...] = mix
        x1_ref[...] = x_ref[...] + mod_ref[2:3, :] * mix

    row = lambda w: pl.BlockSpec((T, w), lambda i: (i, 0))
    return pl.pallas_call(
        body, name="mix_out", grid=(S // T,),
        out_shape=[jax.ShapeDtypeStruct((S, D_MODEL), F32), jax.ShapeDtypeStruct((S, D_MODEL), F32),
                   jax.ShapeDtypeStruct((S, 1024), BF16)],
        in_specs=[row(512), row(POOL_W), _full((4, POOL_GROUP, POOL_GROUP)), _full((1, POOL_W)),
                  _full((1024, D_MODEL)), row(D_MODEL), _full((N_MOD, D_MODEL))],
        out_specs=[row(D_MODEL), row(D_MODEL), row(1024)],
        compiler_params=_params(("parallel",)),
    )(y_mla, pooled, w_pool, pool_scale, w_o, x, mod6)


def _ffn_forward(x1, mod6, g_ffn, g_final, target, w_gate, w_up, w_down, T):
    S = x1.shape[0]

    def body(x1_ref, mod_ref, gf_ref, gl_ref, tgt_ref, wg_ref, wu_ref, wd_ref,
             gate_ref, up_ref, dx2_ref, st_ref, h2_s, acc_s):
        i, j = pl.program_id(0), pl.program_id(1)

        @pl.when(jnp.logical_and(i == 0, j == 0))
        def _():
            st_ref[...] = jnp.zeros_like(st_ref)

        @pl.when(j == 0)
        def _():
            xh, _ = _rms(x1_ref[...])
            h2_s[...] = ((xh * gf_ref[...]) * (1.0 + mod_ref[4:5, :]) + mod_ref[3:4, :]).astype(BF16)
            acc_s[...] = jnp.zeros_like(acc_s)

        h2 = h2_s[...]
        gate = _dot(h2, wg_ref[...])
        up = _dot(h2, wu_ref[...])
        gate_ref[...] = gate
        up_ref[...] = up
        act = gate * jax.nn.sigmoid(gate) * up
        acc_s[...] += _dot(act, wd_ref[...])

        @pl.when(j == N_CHIPS - 1)
        def _():
            ff = acc_s[...]
            x2 = x1_ref[...] + mod_ref[5:6, :] * ff
            xh, r3 = _rms(x2)
            err = xh * gl_ref[...] - tgt_ref[...]
            dy = err * (1.0 / D_MODEL)
            dx2 = _rms_bwd(dy * gl_ref[...], xh, r3)
            dx2_ref[...] = dx2
            st_ref[0:1, :] += jnp.sum(dy * xh, axis=0, keepdims=True)
            st_ref[1:2, :] += jnp.sum(dx2 * ff, axis=0, keepdims=True)
            st_ref[2:3, :] += 0.5 * jnp.sum(err * dy)

    row = pl.BlockSpec((T, D_MODEL), lambda i, j: (i, 0))
    chunk_out = pl.BlockSpec((None, T, FF_CHUNK), lambda i, j: (j, i, 0))
    return pl.pallas_call(
        body, name="ffn_forward", grid=(S // T, N_CHIPS),
        out_shape=[jax.ShapeDtypeStruct((N_CHIPS, S, FF_CHUNK), F32), jax.ShapeDtypeStruct((N_CHIPS, S, FF_CHUNK), F32),
                   jax.ShapeDtypeStruct((S, D_MODEL), F32), jax.ShapeDtypeStruct((8, D_MODEL), F32)],
        in_specs=[row, _full((N_MOD, D_MODEL)), _full((1, D_MODEL)), _full((1, D_MODEL)), row,
                  pl.BlockSpec((None, D_MODEL, FF_CHUNK), lambda i, j: (j, 0, 0)),
                  pl.BlockSpec((None, D_MODEL, FF_CHUNK), lambda i, j: (j, 0, 0)),
                  pl.BlockSpec((None, FF_CHUNK, D_MODEL), lambda i, j: (j, 0, 0))],
        out_specs=[chunk_out, chunk_out, row, _full((8, D_MODEL))],
        scratch_shapes=[pltpu.VMEM((T, D_MODEL), BF16), pltpu.VMEM((T, D_MODEL), F32)],
        compiler_params=_params(("arbitrary", "arbitrary")),
    )(x1, mod6, g_ffn, g_final, target, w_gate, w_up, w_down)


def _ffn_backward(dx2, x1, gate, up, mod6, g_ffn, w_gate, w_up, w_down, T):
    S = x1.shape[0]

    def body(dx2_ref, x1_ref, gate_ref, up_ref, mod_ref, gf_ref, wg_ref, wu_ref, wd_ref,
             dgate_ref, dup_ref, act_ref, dff_ref, h2_ref, dx1_ref, st_ref, acc_s):
        i, j = pl.program_id(0), pl.program_id(1)

        @pl.when(jnp.logical_and(i == 0, j == 0))
        def _():
            st_ref[...] = jnp.zeros_like(st_ref)

        @pl.when(j == 0)
        def _():
            dff_ref[...] = (dx2_ref[...] * mod_ref[5:6, :]).astype(BF16)
            xh, _ = _rms(x1_ref[...])
            h2_ref[...] = ((xh * gf_ref[...]) * (1.0 + mod_ref[4:5, :]) + mod_ref[3:4, :]).astype(BF16)
            acc_s[...] = jnp.zeros_like(acc_s)

        gate, up = gate_ref[...], up_ref[...]
        sg = jax.nn.sigmoid(gate)
        silu = gate * sg
        act_ref[...] = (silu * up).astype(BF16)
        dact = _dot_nt(dff_ref[...], wd_ref[...])
        dup = (dact * silu).astype(BF16)
        dgate = (dact * up * (sg * (1.0 + gate * (1.0 - sg)))).astype(BF16)
        dup_ref[...] = dup
        dgate_ref[...] = dgate
        acc_s[...] += _dot_nt(dgate, wg_ref[...]) + _dot_nt(dup, wu_ref[...])

        @pl.when(j == N_CHIPS - 1)
        def _():
            dh2 = acc_s[...]
            xh, r2 = _rms(x1_ref[...])
            n2 = xh * gf_ref[...]
            st_ref[0:1, :] += jnp.sum(dh2, axis=0, keepdims=True)
            st_ref[1:2, :] += jnp.sum(dh2 * n2, axis=0, keepdims=True)
            dn2 = dh2 * (1.0 + mod_ref[4:5, :])
            st_ref[2:3, :] += jnp.sum(dn2 * xh, axis=0, keepdims=True)
            dx1_ref[...] = _rms_bwd(dn2 * gf_ref[...], xh, r2) + dx2_ref[...]

    row = pl.BlockSpec((T, D_MODEL), lambda i, j: (i, 0))
    chunk = pl.BlockSpec((None, T, FF_CHUNK), lambda i, j: (j, i, 0))
    big = jax.ShapeDtypeStruct((N_CHIPS, S, FF_CHUNK), BF16)
    return pl.pallas_call(
        body, name="ffn_backward", grid=(S // T, N_CHIPS),
        out_shape=[big, big, big, jax.ShapeDtypeStruct((S, D_MODEL), BF16), jax.ShapeDtypeStruct((S, D_MODEL), BF16),
                   jax.ShapeDtypeStruct((S, D_MODEL), F32), jax.ShapeDtypeStruct((8, D_MODEL), F32)],
        in_specs=[row, row, chunk, chunk, _full((N_MOD, D_MODEL)), _full((1, D_MODEL)),
                  pl.BlockSpec((None, D_MODEL, FF_CHUNK), lambda i, j: (j, 0, 0)),
                  pl.BlockSpec((None, D_MODEL, FF_CHUNK), lambda i, j: (j, 0, 0)),
                  pl.BlockSpec((None, FF_CHUNK, D_MODEL), lambda i, j: (j, 0, 0))],
        out_specs=[chunk, chunk, chunk, row, row, row, _full((8, D_MODEL))],
        scratch_shapes=[pltpu.VMEM((T, D_MODEL), F32)],
        compiler_params=_params(("arbitrary", "arbitrary")),
    )(dx2, x1, gate, up, mod6, g_ffn, w_gate, w_up, w_down)


def _tn_matmul(a, b, a_spec, b_spec, groups, m, n, steps, name):
    def body(a_ref, b_ref, o_ref):
        @pl.when(pl.program_id(1) == 0)
        def _():
            o_ref[...] = jnp.zeros_like(o_ref)

        o_ref[...] += _dot_tn(a_ref[...], b_ref[...])

    return pl.pallas_call(
        body, name=name, grid=(groups, steps),
        out_shape=jax.ShapeDtypeStruct((groups, m, n), F32),
        in_specs=[a_spec, b_spec],
        out_specs=pl.BlockSpec((None, m, n), lambda g, i: (g, 0, 0)),
        compiler_params=_params(("parallel", "arbitrary")),
    )(a, b)


def _mix_backward(dx1, mix, mod6, w_o, pooled, w_pool, pool_scale, w_uv_t, T):
    S = dx1.shape[0]

    def body(dx1_ref, mix_ref, mod_ref, wo_ref, pl_ref, wp_ref, ps_ref, wuv_ref,
             dmix_ref, dz_ref, dp_ref, dym_ref, do_ref, st_ref):
        @pl.when(pl.program_id(0) == 0)
        def _():
            st_ref[...] = jnp.zeros_like(st_ref)

        dx1 = dx1_ref[...]
        st_ref[0:1, :] += jnp.sum(dx1 * mix_ref[...], axis=0, keepdims=True)
        dmix = (dx1 * mod_ref[2:3, :]).astype(BF16)
        dmix_ref[...] = dmix
        dmi = _dot_nt(dmix, wo_ref[...])
        dym = dmi[:, 0:512].astype(BF16)
        dym_ref[...] = dym
        for g in range(len(POOL_WINDOWS)):
            cols = slice(g * POOL_GROUP, (g + 1) * POOL_GROUP)
            dyp = dmi[:, 512 + g * POOL_GROUP:512 + (g + 1) * POOL_GROUP]
            z = _dot(pl_ref[:, cols], wp_ref[g])
            st_ref[1:2, cols] += jnp.sum(dyp * z, axis=0, keepdims=True)
            dz = (dyp * ps_ref[:, cols]).astype(BF16)
            dz_ref[:, cols] = dz
            dp_ref[:, cols] = _dot_nt(dz, wp_ref[g])
        for h in range(HEADS):
            do_ref[h] = _dot_nt(dym[:, h * 128:(h + 1) * 128], wuv_ref[h]).astype(BF16)

    row = lambda w: pl.BlockSpec((T, w), lambda i: (i, 0))
    return pl.pallas_call(
        body, name="mix_backward", grid=(S // T,),
        out_shape=[jax.ShapeDtypeStruct((S, D_MODEL), BF16), jax.ShapeDtypeStruct((S, POOL_W), BF16),
                   jax.ShapeDtypeStruct((S, POOL_W), F32), jax.ShapeDtypeStruct((S, 512), BF16),
                   jax.ShapeDtypeStruct((HEADS, S, KV_LORA), BF16), jax.ShapeDtypeStruct((8, D_MODEL), F32)],
        in_specs=[row(D_MODEL), row(D_MODEL), _full((N_MOD, D_MODEL)), _full((1024, D_MODEL)), row(POOL_W),
                  _full((4, POOL_GROUP, POOL_GROUP)), _full((1, POOL_W)), _full((HEADS, KV_LORA, 128))],
        out_specs=[row(D_MODEL), row(POOL_W), row(POOL_W), row(512),
                   pl.BlockSpec((HEADS, T, KV_LORA), lambda i: (0, i, 0)), _full((8, D_MODEL))],
        compiler_params=_params(("arbitrary",)),
    )(dx1, mix, mod6, w_o, pooled, w_pool, pool_scale, w_uv_t)


def _attention_bwd_q(qc, kc, do, o_lat, lse, TQ):
    S = kc.shape[0]
    R = HEADS * TQ
    nq = S // TQ

    def body(q_ref, k_ref, do_ref, o_ref, lse_ref, dq_ref, dr_ref, acc_s):
        i = pl.program_id(0)
        q = q_ref[...].reshape(R, QK_PAD)
        do = do_ref[...].reshape(R, KV_LORA)
        lse = lse_ref[...].reshape(R, 1)
        delta = jnp.sum(do.astype(F32) * o_ref[...].reshape(R, KV_LORA).astype(F32), axis=1, keepdims=True)
        dr_ref[0] = _col_to_row(delta)
        acc_s[...] = jnp.zeros((R, QK_PAD), F32)

        def step(j, masked):
            k = k_ref[pl.ds(pl.multiple_of(j * TQ, TQ), TQ), :]
            s = _dot_nt(q, k) * SM_SCALE
            if masked:
                s = jnp.where(_diag_mask(TQ, False), s, -jnp.inf)
            p = jnp.exp(s - lse)
            dp = _dot_nt(do, k[:, :KV_LORA])
            acc_s[...] += _dot(p * (dp - delta), k)

        def loop(j, carry):
            step(j, False)
            return carry

        lax.fori_loop(0, i, loop, 0)
        step(i, True)
        dq_ref[...] = (acc_s[...] * SM_SCALE).reshape(HEADS, TQ, QK_PAD)

    tile = lambda w: pl.BlockSpec((HEADS, TQ, w), lambda i: (0, i, 0))
    return pl.pallas_call(
        body, name="attention_bwd_q", grid=(nq,),
        out_shape=[jax.ShapeDtypeStruct((HEADS, S, QK_PAD), F32), jax.ShapeDtypeStruct((nq, 1, R), F32)],
        in_specs=[tile(QK_PAD), _full((S, QK_PAD)), tile(KV_LORA), tile(KV_LORA), tile(1)],
        out_specs=[tile(QK_PAD), pl.BlockSpec((1, 1, R), lambda i: (i, 0, 0))],
        scratch_shapes=[pltpu.VMEM((R, QK_PAD), F32)],
        compiler_params=_params(("parallel",)),
    )(qc, kc, do, o_lat, lse)


def _attention_bwd_kv(qc, kc, do, lse_rows, delta_rows, TQ):
    S = kc.shape[0]
    R = HEADS * TQ
    nq = S // TQ

    def body(k_ref, q_ref, do_ref, lser_ref, dr_ref, dk_ref, dk_s, dv_s):
        j = pl.program_id(0)
        k = k_ref[...]
        v = k[:, :KV_LORA]
        dk_s[...] = jnp.zeros((TQ, QK_PAD), F32)
        dv_s[...] = jnp.zeros((TQ, KV_LORA), F32)

        def step(i, masked):
            rows = pl.ds(pl.multiple_of(i * TQ, TQ), TQ)
            q = q_ref[:, rows, :].reshape(R, QK_PAD)
            do = do_ref[:, rows, :].reshape(R, KV_LORA)
            st = _dot_nt(k, q) * SM_SCALE
            if masked:
                st = jnp.where(_diag_mask(TQ, True), st, -jnp.inf)
            pt = jnp.exp(st - lser_ref[i])
            dv_s[...] += _dot(pt, do)
            dpt = _dot_nt(v, do)
            dk_s[...] += _dot(pt * (dpt - dr_ref[i]), q)

        def loop(i, carry):
            step(i, False)
            return carry

        step(j, True)
        lax.fori_loop(j + 1, nq, loop, 0)
        dk = dk_s[...] * SM_SCALE
        dk_ref[:, 0:KV_LORA] = dk[:, 0:KV_LORA] + dv_s[...]
        dk_ref[:, KV_LORA:QK_PAD] = dk[:, KV_LORA:QK_PAD]

    return pl.pallas_call(
        body, name="attention_bwd_kv", grid=(nq,),
        out_shape=jax.ShapeDtypeStruct((S, QK_PAD), F32),
        in_specs=[pl.BlockSpec((TQ, QK_PAD), lambda j: (j, 0)), _full((HEADS, S, QK_PAD)), _full((HEADS, S, KV_LORA)),
                  _full((nq, 1, R)), _full((nq, 1, R))],
        out_specs=pl.BlockSpec((TQ, QK_PAD), lambda j: (j, 0)),
        scratch_shapes=[pltpu.VMEM((TQ, QK_PAD), F32), pltpu.VMEM((TQ, KV_LORA), F32)],
        compiler_params=_params(("parallel",)),
    )(kc, qc, do, lse_rows, delta_rows)


def _pre_attention_backward(x, dx1, proj, dqc, dkc, du, cos, sin, mod6, g_mix, g_q, g_kv, w_in, w_uq, w_uk_t, T):
    S = x.shape[0]

    def body(x_ref, dx1_ref, proj_ref, dqc_ref, dkc_ref, du_ref, cos_ref, sin_ref, mod_ref, gm_ref, gq_ref, gkv_ref,
             win_ref, wuq_ref, wuk_ref, gx_ref, dq_ref, cq_ref, dproj_ref, h1_ref, st_ref):
        @pl.when(pl.program_id(0) == 0)
        def _():
            st_ref[...] = jnp.zeros_like(st_ref)

        cos_t, sin_t = cos_ref[...], sin_ref[...]
        low = lax.broadcasted_iota(jnp.int32, (T, 128), 1) < ROPE
        for h in range(HEADS):
            dq_ref[:, h * NOPE:(h + 1) * NOPE] = _dot(dqc_ref[h, :, 0:KV_LORA], wuk_ref[h]).astype(BF16)
        for pair in range(2):
            d = jnp.where(low, dqc_ref[2 * pair, :, KV_LORA:QK_PAD], dqc_ref[2 * pair + 1, :, KV_LORA:QK_PAD])
            dq_ref[:, O_QA + 128 * pair:O_QA + 128 * (pair + 1)] = _rope_bwd(d, cos_t, sin_t).astype(BF16)
        dcq = _dot_nt(dq_ref[...], wuq_ref[...])
        cqh, rq = _rms(proj_ref[:, 0:Q_LORA])
        cq_ref[...] = (cqh * gq_ref[...]).astype(BF16)
        st_ref[3:4, 0:Q_LORA] += jnp.sum(dcq * cqh, axis=0, keepdims=True)
        dproj_ref[:, 0:Q_LORA] = _rms_bwd(dcq * gq_ref[...], cqh, rq).astype(BF16)
        dckv = dkc_ref[:, 0:KV_LORA]
        ckvh, rkv = _rms(proj_ref[:, O_CKV:O_KR])
        st_ref[4:5, 0:KV_LORA] += jnp.sum(dckv * ckvh, axis=0, keepdims=True)
        dproj_ref[:, O_CKV:O_KR] = _rms_bwd(dckv * gkv_ref[...], ckvh, rkv).astype(BF16)
        dkr = _rope_bwd(dkc_ref[:, KV_LORA:QK_PAD], cos_t, sin_t)
        dkr = jnp.where(low, dkr + pltpu.roll(dkr, ROPE, 1), 0.0)
        dproj_ref[:, O_KR:O_U] = dkr.astype(BF16)
        dproj_ref[:, O_U:PROJ_W] = du_ref[...].astype(BF16)
        dh1 = _dot_nt(dproj_ref[...], win_ref[...])
        xh, r1 = _rms(x_ref[...])
        n1 = xh * gm_ref[...]
        h1_ref[...] = (n1 * (1.0 + mod_ref[1:2, :]) + mod_ref[0:1, :]).astype(BF16)
        st_ref[0:1, :] += jnp.sum(dh1, axis=0, keepdims=True)
        st_ref[1:2, :] += jnp.sum(dh1 * n1, axis=0, keepdims=True)
        dn1 = dh1 * (1.0 + mod_ref[1:2, :])
        st_ref[2:3, :] += jnp.sum(dn1 * xh, axis=0, keepdims=True)
        gx_ref[...] = _rms_bwd(dn1 * gm_ref[...], xh, r1) + dx1_ref[...]

    row = lambda w: pl.BlockSpec((T, w), lambda i: (i, 0))
    return pl.pallas_call(
        body, name="pre_attention_backward", grid=(S // T,),
        out_shape=[jax.ShapeDtypeStruct((S, D_MODEL), F32), jax.ShapeDtypeStruct((S, Q_W), BF16),
                   jax.ShapeDtypeStruct((S, Q_LORA), BF16), jax.ShapeDtypeStruct((S, PROJ_W), BF16),
                   jax.ShapeDtypeStruct((S, D_MODEL), BF16), jax.ShapeDtypeStruct((8, D_MODEL), F32)],
        in_specs=[row(D_MODEL), row(D_MODEL), row(PROJ_W), pl.BlockSpec((HEADS, T, QK_PAD), lambda i: (0, i, 0)),
                  row(QK_PAD), row(POOL_W), row(128), row(128), _full((N_MOD, D_MODEL)), _full((1, D_MODEL)),
                  _full((1, Q_LORA)), _full((1, KV_LORA)), _full((D_MODEL, PROJ_W)), _full((Q_LORA, Q_W)),
                  _full((HEADS, KV_LORA, NOPE))],
        out_specs=[row(D_MODEL), row(Q_W), row(Q_LORA), row(PROJ_W), row(D_MODEL), _full((8, D_MODEL))],
        compiler_params=_params(("arbitrary",)),
    )(x, dx1, proj, dqc, dkc, du, cos, sin, mod6, g_mix, g_q, g_kv, w_in, w_uq, w_uk_t)


def _ada_grads(c_all, dmod_all, chip):
    cols = N_MOD * D_MODEL // N_CHIPS

    def body(col_ref, c_ref, dcol_ref, dall_ref, gw_ref, gb_ref):
        call = c_ref[...]
        act = call * jax.nn.sigmoid(call)
        gw_ref[...] = _dot_tn(act, dcol_ref[...])
        d = dall_ref[...]
        acc = d[0:1, :]
        for b in range(1, 8):
            acc = acc + d[b:b + 1, :]
        gb_ref[...] = acc

    return pl.pallas_call(
        body, name="ada_grads",
        out_shape=[jax.ShapeDtypeStruct((D_MODEL, cols), F32), jax.ShapeDtypeStruct((1, N_MOD * D_MODEL), F32)],
        grid_spec=pltpu.PrefetchScalarGridSpec(
            num_scalar_prefetch=1, grid=(1,),
            in_specs=[pl.BlockSpec((8, D_MODEL), lambda s, col_ref: (0, 0)),
                      pl.BlockSpec((8, cols), lambda s, col_ref: (0, col_ref[0])),
                      pl.BlockSpec((8, N_MOD * D_MODEL), lambda s, col_ref: (0, 0))],
            out_specs=[pl.BlockSpec((D_MODEL, cols), lambda s, col_ref: (0, 0)),
                       pl.BlockSpec((1, N_MOD * D_MODEL), lambda s, col_ref: (0, 0))]),
        compiler_params=_params(("arbitrary",)),
    )(chip, c_all, dmod_all, dmod_all)


def _adamw(w, g, m, v, name):
    rows, cols = w.shape
    T = _row_tile(rows, 256)

    def body(w_ref, g_ref, m_ref, v_ref, d_ref, nm_ref, nv_ref):
        g = g_ref[...]
        m2 = ADAM_B1 * m_ref[...] + (1.0 - ADAM_B1) * g
        v2 = ADAM_B2 * v_ref[...] + (1.0 - ADAM_B2) * (g * g)
        m_hat = m2 / (1.0 - ADAM_B1 ** ADAM_STEP)
        v_hat = v2 / (1.0 - ADAM_B2 ** ADAM_STEP)
        d_ref[...] = -ADAM_LR * (m_hat / (jnp.sqrt(v_hat) + ADAM_EPS) + ADAM_WD * w_ref[...])
        nm_ref[...] = m2
        nv_ref[...] = v2

    spec = pl.BlockSpec((T, cols), lambda i: (i, 0))
    return pl.pallas_call(
        body, name=name, grid=(rows // T,),
        out_shape=[jax.ShapeDtypeStruct((rows, cols), F32)] * 3,
        in_specs=[spec] * 4, out_specs=[spec] * 3,
        compiler_params=_params(("parallel",)),
    )(w, g, m, v)


SMALL_NAMES = ("w_uk", "w_uv", "w_pool", "g_mix", "g_q", "g_kv", "pool_scale", "g_ffn", "g_final", "b_ada")
SMALL_ROWS = 1664


def _pack_rows(parts):
    flat = jnp.concatenate([p.reshape(-1) for p in parts])
    pad = (-flat.shape[0]) % 128
    if pad:
        flat = jnp.concatenate([flat, jnp.zeros((pad,), F32)])
    return flat.reshape(-1, 128)


def kernel(x, c, positions, w_ada, b_ada, g_mix, w_in, g_q, g_kv, w_uq, w_uk, w_uv, w_pool, pool_scale, w_o, g_ffn, w_gate, w_up, w_down, g_final, loss_target, m_w_ada, m_b_ada, m_g_mix, m_w_in, m_g_q, m_g_kv, m_w_uq, m_w_uk, m_w_uv, m_w_pool, m_pool_scale, m_w_o, m_g_ffn, m_w_gate, m_w_up, m_w_down, m_g_final, v_w_ada, v_b_ada, v_g_mix, v_w_in, v_g_q, v_g_kv, v_w_uq, v_w_uk, v_w_uv, v_w_pool, v_pool_scale, v_w_o, v_g_ffn, v_w_gate, v_w_up, v_w_down, v_g_final):
    S = x.shape[1]
    T = _row_tile(S, 512)
    TQ = _row_tile(S, 256)
    TW = _row_tile(S, 1024)
    ix, iy, ic = lax.axis_index("x"), lax.axis_index("y"), lax.axis_index("c")
    chip = (2 * ix + iy).astype(jnp.int32)
    chip_arr = chip.reshape(1)
    core_arr = ic.astype(jnp.int32).reshape(1)

    xs, tgt = x[0], loss_target[0]

    ada_cols = w_ada.shape[2]
    b_cols = lax.dynamic_slice(b_ada, (0, chip * ada_cols), (1, ada_cols))
    mod, c_all = _mod_exchange(c, w_ada[0], b_cols)
    mod6 = mod.reshape(N_MOD, D_MODEL)

    win = w_in[0]
    win_p = jnp.concatenate([win[:, :O_KR + ROPE], win[:, O_KR:O_KR + ROPE], win[:, O_KR + ROPE:]], axis=1).astype(BF16)
    wuq = w_uq[0]
    wuq_p = jnp.concatenate([wuq[:, h, :NOPE] for h in range(HEADS)] + [wuq[:, h, NOPE:] for h in range(HEADS)],
                            axis=1).astype(BF16)
    gathered = _weight_gather([win_p, wuq_p, w_o[0].astype(BF16), w_gate[0].astype(BF16), w_up[0].astype(BF16),
                               w_down[0].astype(BF16)])
    w_in_f = gathered[0].reshape(D_MODEL, PROJ_W)
    w_uq_f = gathered[1].reshape(Q_LORA, Q_W)
    w_o_f = gathered[2].reshape(1024, D_MODEL)
    w_gate_f, w_up_f, w_down_f = gathered[3], gathered[4], gathered[5]
    w_uk_t = jnp.transpose(w_uk[0], (1, 0, 2)).astype(BF16)
    w_uv_t = jnp.transpose(w_uv[0], (1, 0, 2)).astype(BF16)
    w_pool_b = w_pool[0].astype(BF16)

    half = ROPE // 2
    freqs = jnp.power(ROPE_THETA, -jnp.arange(half, dtype=F32) / half)
    cos, sin = _rope_tables(positions.reshape(S, 1), jnp.tile(freqs, 4).reshape(1, 128))
    proj, q, qc, kc = _pre_attention(xs, mod6, g_mix, g_q, g_kv, w_in_f, w_uq_f, w_uk_t, cos, sin, T)
    o_lat, y_mla, lse, lse_rows = _attention_fwd(qc, kc, w_uv_t, TQ)
    pooled = _pool_forward(proj)
    x1, mix, mix_in = _mix_out(y_mla, pooled, w_pool_b, pool_scale, w_o_f, xs, mod6, T)
    gate, up, dx2, st_f = _ffn_forward(x1, mod6, g_ffn, g_final.reshape(1, D_MODEL), tgt, w_gate_f, w_up_f, w_down_f, T)

    dgate, dup, act, dff, h2, dx1, st_b = _ffn_backward(dx2, x1, gate, up, mod6, g_ffn, w_gate_f, w_up_f, w_down_f, T)
    steps = S // TW
    chunk_spec = pl.BlockSpec((None, TW, FF_CHUNK), lambda g, i: (g, i, 0))
    wide_spec = pl.BlockSpec((TW, D_MODEL), lambda g, i: (i, 0))
    g_down = _tn_matmul(act, dff, chunk_spec, wide_spec, N_CHIPS, FF_CHUNK, D_MODEL, steps, "grad_w_down")
    g_gate = _tn_matmul(h2, dgate, wide_spec, chunk_spec, N_CHIPS, D_MODEL, FF_CHUNK, steps, "grad_w_gate")
    g_up = _tn_matmul(h2, dup, wide_spec, chunk_spec, N_CHIPS, D_MODEL, FF_CHUNK, steps, "grad_w_up")

    dmix, dz, dpooled, dy_mla, do_lat, st_m = _mix_backward(dx1, mix, mod6, w_o_f, pooled, w_pool_b, pool_scale, w_uv_t, T)
    g_o = _tn_matmul(mix_in, dmix, wide_spec, wide_spec, 1, 1024, D_MODEL, steps, "grad_w_o")
    col128 = pl.BlockSpec((TW, 128), lambda g, i: (i, g))
    head128 = pl.BlockSpec((None, TW, 128), lambda g, i: (g, i, 0))
    g_pool = _tn_matmul(pooled, dz, col128, col128, 4, POOL_GROUP, POOL_GROUP, steps, "grad_w_pool")
    g_uv_t = _tn_matmul(o_lat, dy_mla, head128, col128, HEADS, KV_LORA, 128, steps, "grad_w_uv")
    du = _pool_backward(dpooled)
    dqc, delta_rows = _attention_bwd_q(qc, kc, do_lat, o_lat, lse, TQ)
    dkc = _attention_bwd_kv(qc, kc, do_lat, lse_rows, delta_rows, TQ)
    grad_x, dq, c_q, dproj, h1, st_p = _pre_attention_backward(
        xs, dx1, proj, dqc, dkc, du, cos, sin, mod6, g_mix, g_q, g_kv, w_in_f, w_uq_f, w_uk_t, T)
    lat_spec = pl.BlockSpec((None, TW, 128), lambda g, i: (g, i, 0))
    g_uk_t = _tn_matmul(dqc, q, lat_spec, col128, HEADS, KV_LORA, NOPE, steps, "grad_w_uk")
    g_uq_p = _tn_matmul(c_q, dq, pl.BlockSpec((TW, Q_LORA), lambda g, i: (i, 0)),
                        pl.BlockSpec((TW, Q_W), lambda g, i: (i, 0)), 1, Q_LORA, Q_W, steps, "grad_w_uq")
    g_in_p = _tn_matmul(h1, dproj, wide_spec, pl.BlockSpec((TW, PROJ_W), lambda g, i: (i, 0)), 1, D_MODEL, PROJ_W,
                        steps, "grad_w_in")

    g_in = jnp.concatenate([g_in_p[0][:, :O_KR + ROPE], g_in_p[0][:, O_U:]], axis=1).reshape(N_CHIPS, -1, 960)
    uq = g_uq_p[0]
    g_uq = jnp.concatenate([jnp.concatenate([uq[:, h * NOPE:(h + 1) * NOPE], uq[:, O_QA + h * ROPE:O_QA + (h + 1) * ROPE]],
                                            axis=1) for h in range(HEADS)], axis=1).reshape(N_CHIPS, -1, HEADS * HEAD_QK)
    small = _pack_rows([g_uk_t, g_uv_t, g_pool, st_p[2], st_p[3, :Q_LORA], st_p[4, :KV_LORA], st_m[1, :POOL_W],
                        st_b[2], st_f[0]])
    small = jnp.concatenate([small, jnp.zeros((SMALL_ROWS - small.shape[0], 128), F32)]).reshape(N_CHIPS, -1, 128)
    grads = [g_in, g_uq, g_o.reshape(N_CHIPS, -1, D_MODEL), g_gate, g_up, g_down, small]
    dmod = jnp.stack([st_p[0], st_p[1], st_m[0], st_b[0], st_b[1], st_f[1]]).reshape(48, 128)

    names = ("w_in", "w_uq", "w_o", "w_gate", "w_up", "w_down", "small")
    got, dmod_all = _grad_swap_halves(grads, dmod)
    chip_sums = [_add_my_half(core_arr, a, b, "add_half_" + n) for a, b, n in zip(grads, got, names)]
    others = _grad_chip_exchange(chip_sums)
    halves = [_add_chips(chip_arr, a, b, "add_chips_" + n) for a, b, n in zip(chip_sums, others, names)]
    fulls, small_all = _grad_finish(halves[:6], halves[6])
    gw_in, gw_uq, gw_o, gw_gate, gw_up, gw_down = [f.reshape(-1, f.shape[2]) for f in fulls]
    small_all = small_all.reshape(SMALL_ROWS * 128)

    gw_ada, gb_ada = _ada_grads(c_all, dmod_all.reshape(8, N_MOD * D_MODEL), chip_arr)

    n_sq = KV_LORA * HEADS * 128
    sizes = [n_sq, n_sq, n_sq, D_MODEL, Q_LORA, KV_LORA, POOL_W, D_MODEL, D_MODEL]
    offs = [0]
    for s_ in sizes:
        offs.append(offs[-1] + s_)
    piece = lambda k: small_all[offs[k]:offs[k + 1]]
    grads_small = {
        "w_uk": jnp.transpose(piece(0).reshape(HEADS, KV_LORA, NOPE), (1, 0, 2)),
        "w_uv": jnp.transpose(piece(1).reshape(HEADS, KV_LORA, 128), (1, 0, 2)),
        "w_pool": piece(2).reshape(4, POOL_GROUP, POOL_GROUP),
        "g_mix": piece(3), "g_q": piece(4), "g_kv": piece(5), "pool_scale": piece(6), "g_ffn": piece(7),
        "g_final": piece(8), "b_ada": gb_ada.reshape(-1),
    }
    weights_small = {"w_uk": w_uk, "w_uv": w_uv, "w_pool": w_pool, "g_mix": g_mix, "g_q": g_q, "g_kv": g_kv,
                     "pool_scale": pool_scale, "g_ffn": g_ffn, "g_final": g_final, "b_ada": b_ada}
    m_small = {"w_uk": m_w_uk, "w_uv": m_w_uv, "w_pool": m_w_pool, "g_mix": m_g_mix, "g_q": m_g_q, "g_kv": m_g_kv,
               "pool_scale": m_pool_scale, "g_ffn": m_g_ffn, "g_final": m_g_final, "b_ada": m_b_ada}
    v_small = {"w_uk": v_w_uk, "w_uv": v_w_uv, "w_pool": v_w_pool, "g_mix": v_g_mix, "g_q": v_g_q, "g_kv": v_g_kv,
               "pool_scale": v_pool_scale, "g_ffn": v_g_ffn, "g_final": v_g_final, "b_ada": v_b_ada}
    pack = lambda d: _pack_rows([d[n] for n in SMALL_NAMES])
    d_s, m_s, v_s = _adamw(pack(weights_small), pack(grads_small), pack(m_small), pack(v_small), "adamw_small")

    def unpack(flat2d):
        flat = flat2d.reshape(-1)
        out, o = {}, 0
        for n in SMALL_NAMES:
            size = weights_small[n].size
            out[n] = flat[o:o + size].reshape(weights_small[n].shape)
            o += size
        return out

    delta_s, newm_s, newv_s = unpack(d_s), unpack(m_s), unpack(v_s)

    big_g = {"w_ada": gw_ada, "w_in": gw_in, "w_uq": gw_uq, "w_o": gw_o, "w_gate": gw_gate, "w_up": gw_up,
             "w_down": gw_down}
    big_w = {"w_ada": w_ada, "w_in": w_in, "w_uq": w_uq, "w_o": w_o, "w_gate": w_gate, "w_up": w_up, "w_down": w_down}
    big_m = {"w_ada": m_w_ada, "w_in": m_w_in, "w_uq": m_w_uq, "w_o": m_w_o, "w_gate": m_w_gate, "w_up": m_w_up,
             "w_down": m_w_down}
    big_v = {"w_ada": v_w_ada, "w_in": v_w_in, "w_uq": v_w_uq, "w_o": v_w_o, "w_gate": v_w_gate, "w_up": v_w_up,
             "w_down": v_w_down}
    grad_out, delta_out, newm_out, newv_out = {}, {}, {}, {}
    for n, g2 in big_g.items():
        shape = big_w[n].shape
        flat = lambda a: a.reshape(g2.shape)
        d_, m_, v_ = _adamw(flat(big_w[n]), g2, flat(big_m[n]), flat(big_v[n]), "adamw_" + n)
        grad_out[n], delta_out[n], newm_out[n], newv_out[n] = (a.reshape(shape) for a in (g2, d_, m_, v_))
    for n in SMALL_NAMES:
        grad_out[n] = grads_small[n].reshape(weights_small[n].shape)
        delta_out[n], newm_out[n], newv_out[n] = delta_s[n], newm_s[n], newv_s[n]

    loss = lax.psum(st_f[2, 0], ("x", "y", "c"))
    order = ("w_ada", "b_ada", "g_mix", "w_in", "g_q", "g_kv", "w_uq", "w_uk", "w_uv", "w_pool", "pool_scale", "w_o",
             "g_ffn", "w_gate", "w_up", "w_down", "g_final")
    return (loss, grad_x.reshape(x.shape), *[grad_out[n] for n in order], *[delta_out[n] for n in order],
            *[newm_out[n] for n in order], *[newv_out[n] for n in order])
```

```python
import functools

import jax
import jax.numpy as jnp
from jax import lax
from jax.experimental import pallas as pl
from jax.experimental.pallas import tpu as pltpu

F32 = jnp.float32
BF16 = jnp.bfloat16

D_MODEL = 1024
HEADS = 4
NOPE = 128
ROPE = 64
HEAD_QK = NOPE + ROPE
Q_LORA = 256
KV_LORA = 128
POOL_W = 512
POOL_WINDOWS = (2, 4, 8, 16)
POOL_GROUP = 128
POOL_PAD = 16
D_FF = 2816
N_CHIPS = 4
FF_CHUNK = D_FF // N_CHIPS
N_MOD = 6
EPS = 1e-6
SM_SCALE = HEAD_QK ** -0.5
ROPE_THETA = 10000.0
QK_PAD = 256
CHUNK = 64
CHUNK_SHIFT = 6

ADAM_LR = 0.001
ADAM_B1 = 0.9
ADAM_B2 = 0.999
ADAM_EPS = 1e-08
ADAM_WD = 0.01
ADAM_STEP = 10

VMEM_LIMIT = 48 * 1024 * 1024
MESH = pl.DeviceIdType.MESH
ANY = pl.BlockSpec(memory_space=pl.ANY)
VMEM_SPEC = pl.BlockSpec(memory_space=pltpu.VMEM)

PROJ_W = 1024
O_CKV = 256
O_KR = 384
O_U = 512
Q_W = 768
O_QA = 512
O_QB = 640


def _params(sem=None, vmem=VMEM_LIMIT):
    kw = dict(vmem_limit_bytes=vmem)
    if sem is not None:
        kw["dimension_semantics"] = sem
    return pltpu.CompilerParams(**kw)


def _dot(a, b):
    return jnp.dot(a.astype(BF16), b.astype(BF16), preferred_element_type=F32)


def _dot_nt(a, b):
    return lax.dot_general(a.astype(BF16), b.astype(BF16), (((1,), (1,)), ((), ())), preferred_element_type=F32)


def _dot_tn(a, b):
    return lax.dot_general(a.astype(BF16), b.astype(BF16), (((0,), (0,)), ((), ())), preferred_element_type=F32)


def _row_tile(rows, target):
    best = rows
    for t in range(8, min(rows, target) + 1, 8):
        if rows % t == 0:
            best = t
    return best if rows % best == 0 and best <= target else rows


def _rms(x):
    r = lax.rsqrt(jnp.mean(x * x, axis=-1, keepdims=True) + EPS)
    return x * r, r


def _rms_bwd(dxh, xh, r):
    return r * (dxh - xh * jnp.mean(dxh * xh, axis=-1, keepdims=True))


def _lane_first_half(shape):
    lane = lax.broadcasted_iota(jnp.int32, shape, 1)
    return (lane & (ROPE - 1)) < (ROPE // 2)


def _rope(a, cos, sin):
    first = _lane_first_half(a.shape)
    up = pltpu.roll(a, 96, 1)
    dn = pltpu.roll(a, 32, 1)
    return a * cos + jnp.where(first, -up, dn) * sin


def _rope_bwd(d, cos, sin):
    first = _lane_first_half(d.shape)
    up = pltpu.roll(d, 96, 1)
    dn = pltpu.roll(d, 32, 1)
    return d * cos + jnp.where(first, up, -dn) * sin


RELATIONS = tuple((dx, dy, dc) for dx in (0, 1) for dy in (0, 1) for dc in (0, 1) if (dx, dy, dc) != (0, 0, 0))
CHIP_RELATIONS = ((1, 0), (0, 1), (1, 1))


def _flip(v, d):
    return 1 - v if d else v


def _place():
    return lax.axis_index("x"), lax.axis_index("y"), lax.axis_index("c")


def _remote(src, dst, send_sem, recv_sem, target):
    return pltpu.make_async_remote_copy(src_ref=src, dst_ref=dst, send_sem=send_sem, recv_sem=recv_sem,
                                        device_id=target, device_id_type=MESH)


def _mod_exchange(c_row, w_ada, b_ada):
    cols = w_ada.shape[1]

    def body(c_ref, w_ref, b_ref, mod_ref, call_ref, part_ref, send1, recv1, loc1, send2, recv2, loc2):
        x, y, c = _place()
        me = 4 * x + 2 * y + c
        own = pltpu.make_async_copy(c_ref, call_ref.at[pl.ds(me, 1)], loc1)
        own.start()
        sends = []
        for k, (dx, dy, dc) in enumerate(RELATIONS):
            cp = _remote(c_ref, call_ref.at[pl.ds(me, 1)], send1.at[k], recv1.at[k],
                         (_flip(x, dx), _flip(y, dy), _flip(c, dc)))
            cp.start()
            sends.append(cp)
        for k, (dx, dy, dc) in enumerate(RELATIONS):
            src = 4 * _flip(x, dx) + 2 * _flip(y, dy) + _flip(c, dc)
            _remote(c_ref, call_ref.at[pl.ds(src, 1)], send1.at[k], recv1.at[k], (x, y, c)).wait_recv()
        own.wait()
        for cp in sends:
            cp.wait_send()
        call = call_ref[...]
        act = call * jax.nn.sigmoid(call)
        part_ref[...] = _dot(act, w_ref[...]) + b_ref[...]
        chip = 2 * x + y
        mine = pltpu.make_async_copy(part_ref.at[pl.ds(me, 1)], mod_ref.at[pl.ds(chip, 1)], loc2)
        mine.start()
        sends = []
        for k, (dx, dy) in enumerate(CHIP_RELATIONS):
            tx, ty = _flip(x, dx), _flip(y, dy)
            tb = 4 * tx + 2 * ty + c
            cp = _remote(part_ref.at[pl.ds(tb, 1)], mod_ref.at[pl.ds(chip, 1)], send2.at[k], recv2.at[k], (tx, ty, c))
            cp.start()
            sends.append(cp)
        for k, (dx, dy) in enumerate(CHIP_RELATIONS):
            src_chip = 2 * _flip(x, dx) + _flip(y, dy)
            _remote(part_ref.at[pl.ds(me, 1)], mod_ref.at[pl.ds(src_chip, 1)], send2.at[k], recv2.at[k],
                    (x, y, c)).wait_recv()
        mine.wait()
        for cp in sends:
            cp.wait_send()

    return pl.pallas_call(
        body, name="mod_exchange",
        out_shape=[jax.ShapeDtypeStruct((N_CHIPS, cols), F32), jax.ShapeDtypeStruct((8, D_MODEL), F32)],
        in_specs=[VMEM_SPEC, VMEM_SPEC, VMEM_SPEC], out_specs=[VMEM_SPEC, VMEM_SPEC],
        scratch_shapes=[pltpu.VMEM((8, cols), F32),
                        pltpu.SemaphoreType.DMA((7,)), pltpu.SemaphoreType.DMA((7,)), pltpu.SemaphoreType.DMA,
                        pltpu.SemaphoreType.DMA((3,)), pltpu.SemaphoreType.DMA((3,)), pltpu.SemaphoreType.DMA],
        compiler_params=_params(),
    )(c_row, w_ada, b_ada)


def _weight_gather(shards):
    n = len(shards)

    def body(*refs):
        ins, outs = refs[:n], refs[n:2 * n]
        send_sems, recv_sems, loc_sems = refs[2 * n:]
        x, y, c = _place()
        chip = 2 * x + y
        local = []
        for w in range(n):
            cp = pltpu.make_async_copy(ins[w], outs[w].at[chip], loc_sems.at[w])
            cp.start()
            local.append(cp)
        sends = []
        for w in range(n):
            hr = ins[w].shape[0] // 2
            half = pl.ds(c * hr, hr)
            for k, (dx, dy) in enumerate(CHIP_RELATIONS):
                cp = _remote(ins[w].at[half], outs[w].at[chip, half], send_sems.at[w, k], recv_sems.at[w, k],
                             (_flip(x, dx), _flip(y, dy), c))
                cp.start()
                sends.append(cp)
        for w in range(n):
            hr = ins[w].shape[0] // 2
            half = pl.ds(c * hr, hr)
            for k, (dx, dy) in enumerate(CHIP_RELATIONS):
                src_chip = 2 * _flip(x, dx) + _flip(y, dy)
                got = outs[w].at[src_chip, half]
                _remote(got, got, send_sems.at[w, k], recv_sems.at[w, k], (x, y, c)).wait_recv()
                cp = _remote(got, got, send_sems.at[w, 3 + k], recv_sems.at[w, 3 + k], (x, y, 1 - c))
                cp.start()
                sends.append(cp)
        for w in range(n):
            hr = ins[w].shape[0] // 2
            other = pl.ds((1 - c) * hr, hr)
            for k, (dx, dy) in enumerate(CHIP_RELATIONS):
                src_chip = 2 * _flip(x, dx) + _flip(y, dy)
                got = outs[w].at[src_chip, other]
                _remote(got, got, send_sems.at[w, 3 + k], recv_sems.at[w, 3 + k], (x, y, c)).wait_recv()
        for cp in sends:
            cp.wait_send()
        for cp in local:
            cp.wait()

    return pl.pallas_call(
        body, name="weight_gather",
        out_shape=[jax.ShapeDtypeStruct((N_CHIPS,) + s.shape, s.dtype) for s in shards],
        in_specs=[ANY] * n, out_specs=[ANY] * n,
        scratch_shapes=[pltpu.SemaphoreType.DMA((n, 6)), pltpu.SemaphoreType.DMA((n, 6)),
                        pltpu.SemaphoreType.DMA((n,))],
        compiler_params=_params(),
    )(*shards)


def _grad_swap_halves(grads, dmod):
    n = len(grads)

    def body(*refs):
        ins, dmod_ref = refs[:n], refs[n]
        outs, dall_ref = refs[n + 1:2 * n + 1], refs[2 * n + 1]
        send_sems, recv_sems, dsend, drecv, dloc = refs[2 * n + 2:]
        x, y, c = _place()
        me = 4 * x + 2 * y + c
        sends = []
        for w in range(n):
            hr = ins[w].shape[1] // 2
            cp = _remote(ins[w].at[:, pl.ds((1 - c) * hr, hr), :], outs[w], send_sems.at[w], recv_sems.at[w],
                         (x, y, 1 - c))
            cp.start()
            sends.append(cp)
        own = pltpu.make_async_copy(dmod_ref, dall_ref.at[me], dloc)
        own.start()
        for k, (dx, dy, dc) in enumerate(RELATIONS):
            cp = _remote(dmod_ref, dall_ref.at[me], dsend.at[k], drecv.at[k],
                         (_flip(x, dx), _flip(y, dy), _flip(c, dc)))
            cp.start()
            sends.append(cp)
        for k, (dx, dy, dc) in enumerate(RELATIONS):
            src = 4 * _flip(x, dx) + 2 * _flip(y, dy) + _flip(c, dc)
            _remote(dmod_ref, dall_ref.at[src], dsend.at[k], drecv.at[k], (x, y, c)).wait_recv()
        for w in range(n):
            _remote(outs[w], outs[w], send_sems.at[w], recv_sems.at[w], (x, y, c)).wait_recv()
        own.wait()
        for cp in sends:
            cp.wait_send()

    out_shape = [jax.ShapeDtypeStruct((N_CHIPS, g.shape[1] // 2, g.shape[2]), F32) for g in grads]
    out_shape.append(jax.ShapeDtypeStruct((8,) + dmod.shape, F32))
    res = pl.pallas_call(
        body, name="grad_swap_halves",
        out_shape=out_shape, in_specs=[ANY] * (n + 1), out_specs=[ANY] * (n + 1),
        scratch_shapes=[pltpu.SemaphoreType.DMA((n,)), pltpu.SemaphoreType.DMA((n,)),
                        pltpu.SemaphoreType.DMA((7,)), pltpu.SemaphoreType.DMA((7,)), pltpu.SemaphoreType.DMA],
        compiler_params=_params(),
    )(*grads, dmod)
    return res[:n], res[n]


def _grad_chip_exchange(sums):
    n = len(sums)

    def body(*refs):
        ins, outs = refs[:n], refs[n:2 * n]
        send_sems, recv_sems = refs[2 * n:]
        x, y, c = _place()
        sends = []
        for w in range(n):
            for k, (dx, dy) in enumerate(CHIP_RELATIONS):
                tx, ty = _flip(x, dx), _flip(y, dy)
                cp = _remote(ins[w].at[2 * tx + ty], outs[w].at[k], send_sems.at[w, k], recv_sems.at[w, k], (tx, ty, c))
                cp.start()
                sends.append(cp)
        for w in range(n):
            for k in range(3):
                _remote(outs[w].at[k], outs[w].at[k], send_sems.at[w, k], recv_sems.at[w, k], (x, y, c)).wait_recv()
        for cp in sends:
            cp.wait_send()

    return pl.pallas_call(
        body, name="grad_chip_exchange",
        out_shape=[jax.ShapeDtypeStruct((3,) + s.shape[1:], F32) for s in sums],
        in_specs=[ANY] * n, out_specs=[ANY] * n,
        scratch_shapes=[pltpu.SemaphoreType.DMA((n, 3)), pltpu.SemaphoreType.DMA((n, 3))],
        compiler_params=_params(),
    )(*sums)


def _grad_finish(halves, small_half):
    n = len(halves)

    def body(*refs):
        ins, sm_ref = refs[:n], refs[n]
        outs, sall_ref = refs[n + 1:2 * n + 1], refs[2 * n + 1]
        send_sems, recv_sems, loc_sems, ssend, srecv, sloc = refs[2 * n + 2:]
        x, y, c = _place()
        chip = 2 * x + y
        local, sends = [], []
        for w in range(n):
            cp = pltpu.make_async_copy(ins[w], outs[w].at[c], loc_sems.at[w])
            cp.start()
            local.append(cp)
            cp = _remote(ins[w], outs[w].at[c], send_sems.at[w], recv_sems.at[w], (x, y, 1 - c))
            cp.start()
            sends.append(cp)
        cp = pltpu.make_async_copy(sm_ref, sall_ref.at[chip, c], sloc)
        cp.start()
        local.append(cp)
        for k, (dx, dy, dc) in enumerate(RELATIONS):
            cp = _remote(sm_ref, sall_ref.at[chip, c], ssend.at[k], srecv.at[k],
                         (_flip(x, dx), _flip(y, dy), _flip(c, dc)))
            cp.start()
            sends.append(cp)
        for k, (dx, dy, dc) in enumerate(RELATIONS):
            got = sall_ref.at[2 * _flip(x, dx) + _flip(y, dy), _flip(c, dc)]
            _remote(got, got, ssend.at[k], srecv.at[k], (x, y, c)).wait_recv()
        for w in range(n):
            got = outs[w].at[1 - c]
            _remote(got, got, send_sems.at[w], recv_sems.at[w], (x, y, c)).wait_recv()
        for cp in sends:
            cp.wait_send()
        for cp in local:
            cp.wait()

    out_shape = [jax.ShapeDtypeStruct((2,) + h.shape, F32) for h in halves]
    out_shape.append(jax.ShapeDtypeStruct((N_CHIPS, 2) + small_half.shape, F32))
    res = pl.pallas_call(
        body, name="grad_finish",
        out_shape=out_shape, in_specs=[ANY] * (n + 1), out_specs=[ANY] * (n + 1),
        scratch_shapes=[pltpu.SemaphoreType.DMA((n,)), pltpu.SemaphoreType.DMA((n,)), pltpu.SemaphoreType.DMA((n,)),
                        pltpu.SemaphoreType.DMA((7,)), pltpu.SemaphoreType.DMA((7,)), pltpu.SemaphoreType.DMA],
        compiler_params=_params(),
    )(*halves, small_half)
    return res[:n], res[n]


def _add_my_half(core, full, got, name):
    _, hr, cols = got.shape

    def body(core_ref, a_ref, b_ref, o_ref):
        o_ref[...] = a_ref[...] + b_ref[...]

    return pl.pallas_call(
        body, name=name,
        out_shape=jax.ShapeDtypeStruct(got.shape, F32),
        grid_spec=pltpu.PrefetchScalarGridSpec(
            num_scalar_prefetch=1, grid=(N_CHIPS,),
            in_specs=[pl.BlockSpec((None, hr, cols), lambda s, core_ref: (s, core_ref[0], 0)),
                      pl.BlockSpec((None, hr, cols), lambda s, core_ref: (s, 0, 0))],
            out_specs=pl.BlockSpec((None, hr, cols), lambda s, core_ref: (s, 0, 0))),
        compiler_params=_params(("arbitrary",)),
    )(core, full, got)


def _add_chips(chip, mine, got, name):
    _, hr, cols = mine.shape

    def body(chip_ref, a_ref, b_ref, o_ref):
        o_ref[...] = ((a_ref[...] + b_ref[0]) + b_ref[1]) + b_ref[2]

    return pl.pallas_call(
        body, name=name,
        out_shape=jax.ShapeDtypeStruct((hr, cols), F32),
        grid_spec=pltpu.PrefetchScalarGridSpec(
            num_scalar_prefetch=1, grid=(1,),
            in_specs=[pl.BlockSpec((None, hr, cols), lambda s, chip_ref: (chip_ref[0], 0, 0)),
                      pl.BlockSpec((3, hr, cols), lambda s, chip_ref: (0, 0, 0))],
            out_specs=pl.BlockSpec((hr, cols), lambda s, chip_ref: (0, 0))),
        compiler_params=_params(("arbitrary",)),
    )(chip, mine, got)


def _rope_tables(pos_col, freqs):
    S = pos_col.shape[0]
    T = _row_tile(S, 1024)

    def body(p_ref, f_ref, cos_ref, sin_ref):
        ang = p_ref[...].astype(F32) * f_ref[...]
        cos_ref[...] = jnp.cos(ang)
        sin_ref[...] = jnp.sin(ang)

    return pl.pallas_call(
        body, name="rope_tables", grid=(S // T,),
        out_shape=[jax.ShapeDtypeStruct((S, 128), F32)] * 2,
        in_specs=[pl.BlockSpec((T, 1), lambda i: (i, 0)), pl.BlockSpec((1, 128), lambda i: (0, 0))],
        out_specs=[pl.BlockSpec((T, 128), lambda i: (i, 0))] * 2,
        compiler_params=_params(("parallel",)),
    )(pos_col, freqs)


def _full(shape):
    zeros = (0,) * len(shape)
    return pl.BlockSpec(shape, lambda *_: zeros)


def _pre_attention(x, mod6, g_mix, g_q, g_kv, w_in, w_uq, w_uk_t, cos, sin, T, TQ):
    S = x.shape[0]

    def body(x_ref, mod_ref, gm_ref, gq_ref, gkv_ref, win_ref, wuq_ref, wuk_ref, cos_ref, sin_ref,
             proj_ref, q_ref, qc_ref, kc_ref, kct_ref):
        xh, _ = _rms(x_ref[...])
        h1 = (xh * gm_ref[...]) * (1.0 + mod_ref[1:2, :]) + mod_ref[0:1, :]
        proj = _dot(h1, win_ref[...])
        proj_ref[...] = proj
        cqh, _ = _rms(proj[:, :Q_LORA])
        c_q = cqh * gq_ref[...]
        ckvh, _ = _rms(proj[:, O_CKV:O_KR])
        c_kv = ckvh * gkv_ref[...]
        q = _dot(c_q, wuq_ref[...])
        q_ref[...] = q
        cos_t, sin_t = cos_ref[...], sin_ref[...]
        ropes = (_rope(q[:, O_QA:O_QB], cos_t, sin_t), _rope(q[:, O_QB:Q_W], cos_t, sin_t))
        low = lax.broadcasted_iota(jnp.int32, (T, 128), 1) < ROPE
        for h in range(HEADS):
            q_lat = _dot_nt(q[:, h * NOPE:(h + 1) * NOPE], wuk_ref[h])
            keep = low if h % 2 == 0 else jnp.logical_not(low)
            qc_ref[h, :, 0:KV_LORA] = q_lat.astype(BF16)
            qc_ref[h, :, KV_LORA:QK_PAD] = jnp.where(keep, ropes[h // 2], 0.0).astype(BF16)
        k_rope = _rope(proj[:, O_KR:O_U], cos_t, sin_t)
        kc_ref[:, 0:KV_LORA] = c_kv.astype(BF16)
        kc_ref[:, KV_LORA:QK_PAD] = k_rope.astype(BF16)
        lat_t, rope_t = jnp.transpose(c_kv), jnp.transpose(k_rope)
        for s in range(T // TQ):
            kct_ref[s, 0:KV_LORA, :] = lat_t[:, s * TQ:(s + 1) * TQ].astype(BF16)
            kct_ref[s, KV_LORA:QK_PAD, :] = rope_t[:, s * TQ:(s + 1) * TQ].astype(BF16)

    row = lambda w: pl.BlockSpec((T, w), lambda i: (i, 0))
    return pl.pallas_call(
        body, name="pre_attention", grid=(S // T,),
        out_shape=[jax.ShapeDtypeStruct((S, PROJ_W), F32), jax.ShapeDtypeStruct((S, Q_W), F32),
                   jax.ShapeDtypeStruct((HEADS, S, QK_PAD), BF16), jax.ShapeDtypeStruct((S, QK_PAD), BF16),
                   jax.ShapeDtypeStruct((S // TQ, QK_PAD, TQ), BF16)],
        in_specs=[row(D_MODEL), _full((N_MOD, D_MODEL)), _full((1, D_MODEL)), _full((1, Q_LORA)), _full((1, KV_LORA)),
                  _full((D_MODEL, PROJ_W)), _full((Q_LORA, Q_W)), _full((HEADS, KV_LORA, NOPE)), row(128), row(128)],
        out_specs=[row(PROJ_W), row(Q_W), pl.BlockSpec((HEADS, T, QK_PAD), lambda i: (0, i, 0)), row(QK_PAD),
                   pl.BlockSpec((T // TQ, QK_PAD, TQ), lambda i: (i, 0, 0))],
        compiler_params=_params(("parallel",)),
    )(x, mod6, g_mix, g_q, g_kv, w_in, w_uq, w_uk_t, cos, sin)


def _diag_mask(TQ, transposed):
    R = HEADS * TQ
    if transposed:
        key = lax.broadcasted_iota(jnp.int32, (TQ, R), 0) >> CHUNK_SHIFT
        qry = (lax.broadcasted_iota(jnp.int32, (TQ, R), 1) & (TQ - 1)) >> CHUNK_SHIFT
    else:
        qry = (lax.broadcasted_iota(jnp.int32, (R, TQ), 0) & (TQ - 1)) >> CHUNK_SHIFT
        key = lax.broadcasted_iota(jnp.int32, (R, TQ), 1) >> CHUNK_SHIFT
    return key <= qry


def _col_to_row(col):
    return jnp.transpose(jnp.broadcast_to(col, (col.shape[0], 128)))[0:1, :]


def _attention_fwd(qc, kc, kct, w_uv_t, TQ):
    S = kc.shape[0]
    R = HEADS * TQ
    nq = S // TQ

    def body(q_ref, k_ref, kt_ref, wuv_ref, o_ref, y_ref, lser_ref, m_s, l_s, acc_s):
        i = pl.program_id(0)
        q = q_ref[...].reshape(R, QK_PAD)
        m_s[...] = jnp.full((1, R), -jnp.inf, F32)
        l_s[...] = jnp.zeros((1, R), F32)
        acc_s[...] = jnp.zeros((KV_LORA, R), F32)

        def step(j, masked):
            k = k_ref[pl.ds(pl.multiple_of(j * TQ, TQ), TQ), :]
            st = _dot_nt(k, q) * SM_SCALE
            if masked:
                st = jnp.where(_diag_mask(TQ, True), st, -jnp.inf)
            m_old = m_s[...]
            m_new = jnp.maximum(m_old, jnp.max(st, axis=0, keepdims=True))
            pt = jnp.exp(st - m_new)
            alpha = jnp.exp(m_old - m_new)
            l_s[...] = alpha * l_s[...] + jnp.sum(pt, axis=0, keepdims=True)
            acc_s[...] = alpha * acc_s[...] + _dot(kt_ref[j, 0:KV_LORA, :], pt)
            m_s[...] = m_new

        def loop(j, carry):
            step(j, False)
            return carry

        lax.fori_loop(0, i, loop, 0)
        step(i, True)
        l = l_s[...]
        lser_ref[0] = m_s[...] + jnp.log(l)
        o = jnp.transpose(acc_s[...] / l).astype(BF16)
        for h in range(HEADS):
            oh = o[h * TQ:(h + 1) * TQ, :]
            o_ref[h] = oh
            y_ref[:, h * 128:(h + 1) * 128] = _dot(oh, wuv_ref[h]).astype(BF16)

    return pl.pallas_call(
        body, name="attention_fwd", grid=(nq,),
        out_shape=[jax.ShapeDtypeStruct((HEADS, S, KV_LORA), BF16), jax.ShapeDtypeStruct((S, HEADS * 128), BF16),
                   jax.ShapeDtypeStruct((nq, 1, R), F32)],
        in_specs=[pl.BlockSpec((HEADS, TQ, QK_PAD), lambda i: (0, i, 0)), _full((S, QK_PAD)),
                  _full((nq, QK_PAD, TQ)), _full((HEADS, KV_LORA, 128))],
        out_specs=[pl.BlockSpec((HEADS, TQ, KV_LORA), lambda i: (0, i, 0)), pl.BlockSpec((TQ, HEADS * 128), lambda i: (i, 0)),
                   pl.BlockSpec((1, 1, R), lambda i: (i, 0, 0))],
        scratch_shapes=[pltpu.VMEM((1, R), F32), pltpu.VMEM((1, R), F32), pltpu.VMEM((KV_LORA, R), F32)],
        compiler_params=_params(("parallel",)),
    )(qc, kc, kct, w_uv_t)


def _pool_forward(proj):
    S = proj.shape[0]
    RB = _row_tile(S, 256)

    def body(proj_ref, out_ref, pad_ref, sem):
        cp = pltpu.make_async_copy(proj_ref.at[:, pl.ds(O_U, POOL_W)], pad_ref.at[pl.ds(POOL_PAD, S)], sem)
        cp.start()
        pad_ref[0:POOL_PAD, :] = jnp.zeros((POOL_PAD, POOL_W), F32)
        cp.wait()
        for g, win in enumerate(POOL_WINDOWS):
            cols = slice(g * POOL_GROUP, (g + 1) * POOL_GROUP)
            for r0 in range(0, S, RB):
                u = pad_ref[POOL_PAD + r0:POOL_PAD + r0 + RB, cols]
                acc = u
                for k in range(1, win):
                    acc = acc + pad_ref[POOL_PAD + r0 - k:POOL_PAD + r0 - k + RB, cols]
                if r0 == 0:
                    t1 = (lax.broadcasted_iota(jnp.int32, (RB, POOL_GROUP), 0) + 1).astype(F32)
                    mean = acc / jnp.minimum(t1, float(win))
                else:
                    mean = acc * (1.0 / win)
                out_ref[r0:r0 + RB, cols] = (mean - u).astype(BF16)

    return pl.pallas_call(
        body, name="pool_forward",
        out_shape=jax.ShapeDtypeStruct((S, POOL_W), BF16),
        in_specs=[ANY], out_specs=VMEM_SPEC,
        scratch_shapes=[pltpu.VMEM((S + POOL_PAD, POOL_W), F32), pltpu.SemaphoreType.DMA],
        compiler_params=_params(),
    )(proj)


def _pool_backward(dpooled):
    S = dpooled.shape[0]
    RB = _row_tile(S, 256)

    def body(dp_ref, out_ref, pad_ref, sem):
        cp = pltpu.make_async_copy(dp_ref, pad_ref.at[pl.ds(0, S)], sem)
        cp.start()
        pad_ref[S:S + POOL_PAD, :] = jnp.zeros((POOL_PAD, POOL_W), F32)
        cp.wait()
        for g, win in enumerate(POOL_WINDOWS):
            cols = slice(g * POOL_GROUP, (g + 1) * POOL_GROUP)
            head = pad_ref[0:POOL_PAD, cols]
            t1 = (lax.broadcasted_iota(jnp.int32, (POOL_PAD, POOL_GROUP), 0) + 1).astype(F32)
            pad_ref[0:POOL_PAD, cols] = head * (float(win) / jnp.minimum(t1, float(win)))
            for r0 in range(0, S, RB):
                acc = pad_ref[r0:r0 + RB, cols]
                for k in range(1, win):
                    acc = acc + pad_ref[r0 + k:r0 + k + RB, cols]
                own = pad_ref[r0:r0 + RB, cols]
                if r0 == 0:
                    own = jnp.concatenate([head, own[POOL_PAD:]], axis=0)
                out_ref[r0:r0 + RB, cols] = acc * (1.0 / win) - own

    return pl.pallas_call(
        body, name="pool_backward",
        out_shape=jax.ShapeDtypeStruct((S, POOL_W), F32),
        in_specs=[ANY], out_specs=VMEM_SPEC,
        scratch_shapes=[pltpu.VMEM((S + POOL_PAD, POOL_W), F32), pltpu.SemaphoreType.DMA],
        compiler_params=_params(),
    )(dpooled)


def _mix_out(y_mla, pooled, w_pool, pool_scale, w_o, x, mod6, T):
    S = x.shape[0]

    def body(ym_ref, pl_ref, wp_ref, ps_ref, wo_ref, x_ref, mod_ref, x1_ref, mix_ref, mi_ref):
        mi_ref[:, 0:512] = ym_ref[...]
        for g in range(len(POOL_WINDOWS)):
            cols = slice(g * POOL_GROUP, (g + 1) * POOL_GROUP)
            z = _dot(pl_ref[:, cols], wp_ref[g])
            mi_ref[:, 512 + g * POOL_GROUP:512 + (g + 1) * POOL_GROUP] = (z * ps_ref[:, cols]).astype(BF16)
        mix = _dot(mi_ref[...], wo_ref[...])
        mix_ref[...] = mix
        x1_ref[...] = x_ref[...] + mod_ref[2:3, :] * mix

    row = lambda w: pl.BlockSpec((T, w), lambda i: (i, 0))
    return pl.pallas_call(
        body, name="mix_out", grid=(S // T,),
        out_shape=[jax.ShapeDtypeStruct((S, D_MODEL), F32), jax.ShapeDtypeStruct((S, D_MODEL), F32),
                   jax.ShapeDtypeStruct((S, 1024), BF16)],
        in_specs=[row(512), row(POOL_W), _full((4, POOL_GROUP, POOL_GROUP)), _full((1, POOL_W)),
                  _full((1024, D_MODEL)), row(D_MODEL), _full((N_MOD, D_MODEL))],
        out_specs=[row(D_MODEL), row(D_MODEL), row(1024)],
        compiler_params=_params(("parallel",)),
    )(y_mla, pooled, w_pool, pool_scale, w_o, x, mod6)


def _ffn_forward(x1, mod6, g_ffn, g_final, target, w_gate, w_up, w_down, T):
    S = x1.shape[0]

    def body(x1_ref, mod_ref, gf_ref, gl_ref, tgt_ref, wg_ref, wu_ref, wd_ref,
             gate_ref, up_ref, dx2_ref, st_ref, h2_s, acc_s):
        i, j = pl.program_id(0), pl.program_id(1)

        @pl.when(jnp.logical_and(i == 0, j == 0))
        def _():
            st_ref[...] = jnp.zeros_like(st_ref)

        @pl.when(j == 0)
        def _():
            xh, _ = _rms(x1_ref[...])
            h2_s[...] = ((xh * gf_ref[...]) * (1.0 + mod_ref[4:5, :]) + mod_ref[3:4, :]).astype(BF16)
            acc_s[...] = jnp.zeros_like(acc_s)

        h2 = h2_s[...]
        gate = _dot(h2, wg_ref[...])
        up = _dot(h2, wu_ref[...])
        gate_ref[...] = gate
        up_ref[...] = up
        act = gate * jax.nn.sigmoid(gate) * up
        acc_s[...] += _dot(act, wd_ref[...])

        @pl.when(j == N_CHIPS - 1)
        def _():
            ff = acc_s[...]
            x2 = x1_ref[...] + mod_ref[5:6, :] * ff
            xh, r3 = _rms(x2)
            err = xh * gl_ref[...] - tgt_ref[...]
            dy = err * (1.0 / D_MODEL)
            dx2 = _rms_bwd(dy * gl_ref[...], xh, r3)
            dx2_ref[...] = dx2
            st_ref[0:1, :] += jnp.sum(dy * xh, axis=0, keepdims=True)
            st_ref[1:2, :] += jnp.sum(dx2 * ff, axis=0, keepdims=True)
            st_ref[2:3, :] += 0.5 * jnp.sum(err * dy)

    row = pl.BlockSpec((T, D_MODEL), lambda i, j: (i, 0))
    chunk_out = pl.BlockSpec((None, T, FF_CHUNK), lambda i, j: (j, i, 0))
    return pl.pallas_call(
        body, name="ffn_forward", grid=(S // T, N_CHIPS),
        out_shape=[jax.ShapeDtypeStruct((N_CHIPS, S, FF_CHUNK), F32), jax.ShapeDtypeStruct((N_CHIPS, S, FF_CHUNK), F32),
                   jax.ShapeDtypeStruct((S, D_MODEL), F32), jax.ShapeDtypeStruct((8, D_MODEL), F32)],
        in_specs=[row, _full((N_MOD, D_MODEL)), _full((1, D_MODEL)), _full((1, D_MODEL)), row,
                  pl.BlockSpec((None, D_MODEL, FF_CHUNK), lambda i, j: (j, 0, 0)),
                  pl.BlockSpec((None, D_MODEL, FF_CHUNK), lambda i, j: (j, 0, 0)),
                  pl.BlockSpec((None, FF_CHUNK, D_MODEL), lambda i, j: (j, 0, 0))],
        out_specs=[chunk_out, chunk_out, row, _full((8, D_MODEL))],
        scratch_shapes=[pltpu.VMEM((T, D_MODEL), BF16), pltpu.VMEM((T, D_MODEL), F32)],
        compiler_params=_params(("arbitrary", "arbitrary")),
    )(x1, mod6, g_ffn, g_final, target, w_gate, w_up, w_down)


def _ffn_backward(dx2, x1, gate, up, mod6, g_ffn, w_gate, w_up, w_down, T):
    S = x1.shape[0]

    def body(dx2_ref, x1_ref, gate_ref, up_ref, mod_ref, gf_ref, wg_ref, wu_ref, wd_ref,
             dgate_ref, dup_ref, act_ref, dff_ref, h2_ref, dx1_ref, st_ref, acc_s):
        i, j = pl.program_id(0), pl.program_id(1)

        @pl.when(jnp.logical_and(i == 0, j == 0))
        def _():
            st_ref[...] = jnp.zeros_like(st_ref)

        @pl.when(j == 0)
        def _():
            dff_ref[...] = (dx2_ref[...] * mod_ref[5:6, :]).astype(BF16)
            xh, _ = _rms(x1_ref[...])
            h2_ref[...] = ((xh * gf_ref[...]) * (1.0 + mod_ref[4:5, :]) + mod_ref[3:4, :]).astype(BF16)
            acc_s[...] = jnp.zeros_like(acc_s)

        gate, up = gate_ref[...], up_ref[...]
        sg = jax.nn.sigmoid(gate)
        silu = gate * sg
        act_ref[...] = (silu * up).astype(BF16)
        dact = _dot_nt(dff_ref[...], wd_ref[...])
        dup = (dact * silu).astype(BF16)
        dgate = (dact * up * (sg * (1.0 + gate * (1.0 - sg)))).astype(BF16)
        dup_ref[...] = dup
        dgate_ref[...] = dgate
        acc_s[...] += _dot_nt(dgate, wg_ref[...]) + _dot_nt(dup, wu_ref[...])

        @pl.when(j == N_CHIPS - 1)
        def _():
            dh2 = acc_s[...]
            xh, r2 = _rms(x1_ref[...])
            n2 = xh * gf_ref[...]
            st_ref[0:1, :] += jnp.sum(dh2, axis=0, keepdims=True)
            st_ref[1:2, :] += jnp.sum(dh2 * n2, axis=0, keepdims=True)
            dn2 = dh2 * (1.0 + mod_ref[4:5, :])
            st_ref[2:3, :] += jnp.sum(dn2 * xh, axis=0, keepdims=True)
            dx1_ref[...] = _rms_bwd(dn2 * gf_ref[...], xh, r2) + dx2_ref[...]

    row = pl.BlockSpec((T, D_MODEL), lambda i, j: (i, 0))
    chunk = pl.BlockSpec((None, T, FF_CHUNK), lambda i, j: (j, i, 0))
    big = jax.ShapeDtypeStruct((N_CHIPS, S, FF_CHUNK), BF16)
    return pl.pallas_call(
        body, name="ffn_backward", grid=(S // T, N_CHIPS),
        out_shape=[big, big, big, jax.ShapeDtypeStruct((S, D_MODEL), BF16), jax.ShapeDtypeStruct((S, D_MODEL), BF16),
                   jax.ShapeDtypeStruct((S, D_MODEL), F32), jax.ShapeDtypeStruct((8, D_MODEL), F32)],
        in_specs=[row, row, chunk, chunk, _full((N_MOD, D_MODEL)), _full((1, D_MODEL)),
                  pl.BlockSpec((None, D_MODEL, FF_CHUNK), lambda i, j: (j, 0, 0)),
                  pl.BlockSpec((None, D_MODEL, FF_CHUNK), lambda i, j: (j, 0, 0)),
                  pl.BlockSpec((None, FF_CHUNK, D_MODEL), lambda i, j: (j, 0, 0))],
        out_specs=[chunk, chunk, chunk, row, row, row, _full((8, D_MODEL))],
        scratch_shapes=[pltpu.VMEM((T, D_MODEL), F32)],
        compiler_params=_params(("arbitrary", "arbitrary")),
    )(dx2, x1, gate, up, mod6, g_ffn, w_gate, w_up, w_down)


def _tn_matmul(a, b, a_spec, b_spec, groups, m, n, steps, name):
    def body(a_ref, b_ref, o_ref):
        @pl.when(pl.program_id(1) == 0)
        def _():
            o_ref[...] = jnp.zeros_like(o_ref)

        o_ref[...] += _dot_tn(a_ref[...], b_ref[...])

    return pl.pallas_call(
        body, name=name, grid=(groups, steps),
        out_shape=jax.ShapeDtypeStruct((groups, m, n), F32),
        in_specs=[a_spec, b_spec],
        out_specs=pl.BlockSpec((None, m, n), lambda g, i: (g, 0, 0)),
        compiler_params=_params(("parallel", "arbitrary")),
    )(a, b)


def _mix_backward(dx1, mix, mod6, w_o, pooled, w_pool, pool_scale, w_uv_t, o_lat, T, TQ):
    S = dx1.shape[0]

    def body(dx1_ref, mix_ref, mod_ref, wo_ref, pl_ref, wp_ref, ps_ref, wuv_ref, o_ref,
             dmix_ref, dz_ref, dp_ref, dym_ref, do_ref, dr_ref, st_ref):
        @pl.when(pl.program_id(0) == 0)
        def _():
            st_ref[...] = jnp.zeros_like(st_ref)

        dx1 = dx1_ref[...]
        st_ref[0:1, :] += jnp.sum(dx1 * mix_ref[...], axis=0, keepdims=True)
        dmix = (dx1 * mod_ref[2:3, :]).astype(BF16)
        dmix_ref[...] = dmix
        dmi = _dot_nt(dmix, wo_ref[...])
        dym = dmi[:, 0:512].astype(BF16)
        dym_ref[...] = dym
        for g in range(len(POOL_WINDOWS)):
            cols = slice(g * POOL_GROUP, (g + 1) * POOL_GROUP)
            dyp = dmi[:, 512 + g * POOL_GROUP:512 + (g + 1) * POOL_GROUP]
            z = _dot(pl_ref[:, cols], wp_ref[g])
            st_ref[1:2, cols] += jnp.sum(dyp * z, axis=0, keepdims=True)
            dz = (dyp * ps_ref[:, cols]).astype(BF16)
            dz_ref[:, cols] = dz
            dp_ref[:, cols] = _dot_nt(dz, wp_ref[g])
        for h in range(HEADS):
            do = _dot_nt(dym[:, h * 128:(h + 1) * 128], wuv_ref[h]).astype(BF16)
            do_ref[h] = do
            delta = _col_to_row(jnp.sum(do.astype(F32) * o_ref[h].astype(F32), axis=1, keepdims=True))
            for s in range(T // TQ):
                dr_ref[s, :, h * TQ:(h + 1) * TQ] = delta[:, s * TQ:(s + 1) * TQ]

    row = lambda w: pl.BlockSpec((T, w), lambda i: (i, 0))
    heads = pl.BlockSpec((HEADS, T, KV_LORA), lambda i: (0, i, 0))
    return pl.pallas_call(
        body, name="mix_backward", grid=(S // T,),
        out_shape=[jax.ShapeDtypeStruct((S, D_MODEL), BF16), jax.ShapeDtypeStruct((S, POOL_W), BF16),
                   jax.ShapeDtypeStruct((S, POOL_W), F32), jax.ShapeDtypeStruct((S, 512), BF16),
                   jax.ShapeDtypeStruct((HEADS, S, KV_LORA), BF16), jax.ShapeDtypeStruct((S // TQ, 1, HEADS * TQ), F32),
                   jax.ShapeDtypeStruct((8, D_MODEL), F32)],
        in_specs=[row(D_MODEL), row(D_MODEL), _full((N_MOD, D_MODEL)), _full((1024, D_MODEL)), row(POOL_W),
                  _full((4, POOL_GROUP, POOL_GROUP)), _full((1, POOL_W)), _full((HEADS, KV_LORA, 128)), heads],
        out_specs=[row(D_MODEL), row(POOL_W), row(POOL_W), row(512), heads,
                   pl.BlockSpec((T // TQ, 1, HEADS * TQ), lambda i: (i, 0, 0)), _full((8, D_MODEL))],
        compiler_params=_params(("arbitrary",)),
    )(dx1, mix, mod6, w_o, pooled, w_pool, pool_scale, w_uv_t, o_lat)


def _attention_bwd(qc, kc, kct, do, lse_rows, delta_rows, TQ):
    S = kc.shape[0]
    R = HEADS * TQ
    nq = S // TQ

    def body(k_ref, kt_ref, q_ref, do_ref, lser_ref, dr_ref, dk_ref, dqt_ref, dk_s, dv_s):
        j = pl.program_id(0)

        @pl.when(j == 0)
        def _():
            def zero(i, carry):
                dqt_ref[i] = jnp.zeros((QK_PAD, R), F32)
                return carry
            lax.fori_loop(0, nq, zero, 0)

        k = k_ref[...]
        kt = kt_ref[...]
        v = k[:, :KV_LORA]
        dk_s[...] = jnp.zeros((TQ, QK_PAD), F32)
        dv_s[...] = jnp.zeros((TQ, KV_LORA), F32)

        def step(i, masked):
            rows = pl.ds(pl.multiple_of(i * TQ, TQ), TQ)
            q = q_ref[:, rows, :].reshape(R, QK_PAD)
            do = do_ref[:, rows, :].reshape(R, KV_LORA)
            st = _dot_nt(k, q) * SM_SCALE
            if masked:
                st = jnp.where(_diag_mask(TQ, True), st, -jnp.inf)
            pt = jnp.exp(st - lser_ref[i])
            dv_s[...] += _dot(pt, do)
            dpt = _dot_nt(v, do)
            dst = (pt * (dpt - dr_ref[i])).astype(BF16)
            dk_s[...] += _dot(dst, q)
            dqt_ref[i] += _dot(kt, dst)

        def loop(i, carry):
            step(i, False)
            return carry

        step(j, True)
        lax.fori_loop(j + 1, nq, loop, 0)
        dk = dk_s[...] * SM_SCALE
        dk_ref[:, 0:KV_LORA] = dk[:, 0:KV_LORA] + dv_s[...]
        dk_ref[:, KV_LORA:QK_PAD] = dk[:, KV_LORA:QK_PAD]

    return pl.pallas_call(
        body, name="attention_bwd", grid=(nq,),
        out_shape=[jax.ShapeDtypeStruct((S, QK_PAD), F32), jax.ShapeDtypeStruct((nq, QK_PAD, R), F32)],
        in_specs=[pl.BlockSpec((TQ, QK_PAD), lambda j: (j, 0)), pl.BlockSpec((None, QK_PAD, TQ), lambda j: (j, 0, 0)),
                  VMEM_SPEC, VMEM_SPEC, VMEM_SPEC, VMEM_SPEC],
        out_specs=[pl.BlockSpec((TQ, QK_PAD), lambda j: (j, 0)), VMEM_SPEC],
        scratch_shapes=[pltpu.VMEM((TQ, QK_PAD), F32), pltpu.VMEM((TQ, KV_LORA), F32)],
        compiler_params=_params(("arbitrary",)),
    )(kc, kct, qc, do, lse_rows, delta_rows)


def _pre_attention_backward(x, dx1, proj, dqt, dkc, du, cos, sin, mod6, g_mix, g_q, g_kv, w_in, w_uq, w_uk_t, T, TQ):
    S = x.shape[0]

    def body(x_ref, dx1_ref, proj_ref, dqt_ref, dkc_ref, du_ref, cos_ref, sin_ref, mod_ref, gm_ref, gq_ref, gkv_ref,
             win_ref, wuq_ref, wuk_ref, gx_ref, dq_ref, dql_ref, cq_ref, dproj_ref, h1_ref, st_ref):
        @pl.when(pl.program_id(0) == 0)
        def _():
            st_ref[...] = jnp.zeros_like(st_ref)

        cos_t, sin_t = cos_ref[...], sin_ref[...]
        low = lax.broadcasted_iota(jnp.int32, (T, 128), 1) < ROPE
        rope_parts = []
        for h in range(HEADS):
            dqc = jnp.concatenate([jnp.transpose(dqt_ref[s, :, h * TQ:(h + 1) * TQ]) for s in range(T // TQ)], axis=0)
            dqc = dqc * SM_SCALE
            dql = dqc[:, 0:KV_LORA].astype(BF16)
            dql_ref[h] = dql
            dq_ref[:, h * NOPE:(h + 1) * NOPE] = _dot(dql, wuk_ref[h]).astype(BF16)
            rope_parts.append(dqc[:, KV_LORA:QK_PAD])
        for pair in range(2):
            d = jnp.where(low, rope_parts[2 * pair], rope_parts[2 * pair + 1])
            dq_ref[:, O_QA + 128 * pair:O_QA + 128 * (pair + 1)] = _rope_bwd(d, cos_t, sin_t).astype(BF16)
        dcq = _dot_nt(dq_ref[...], wuq_ref[...])
        cqh, rq = _rms(proj_ref[:, 0:Q_LORA])
        cq_ref[...] = (cqh * gq_ref[...]).astype(BF16)
        st_ref[3:4, 0:Q_LORA] += jnp.sum(dcq * cqh, axis=0, keepdims=True)
        dproj_ref[:, 0:Q_LORA] = _rms_bwd(dcq * gq_ref[...], cqh, rq).astype(BF16)
        dckv = dkc_ref[:, 0:KV_LORA]
        ckvh, rkv = _rms(proj_ref[:, O_CKV:O_KR])
        st_ref[4:5, 0:KV_LORA] += jnp.sum(dckv * ckvh, axis=0, keepdims=True)
        dproj_ref[:, O_CKV:O_KR] = _rms_bwd(dckv * gkv_ref[...], ckvh, rkv).astype(BF16)
        dkr = _rope_bwd(dkc_ref[:, KV_LORA:QK_PAD], cos_t, sin_t)
        dkr = jnp.where(low, dkr + pltpu.roll(dkr, ROPE, 1), 0.0)
        dproj_ref[:, O_KR:O_U] = dkr.astype(BF16)
        dproj_ref[:, O_U:PROJ_W] = du_ref[...].astype(BF16)
        dh1 = _dot_nt(dproj_ref[...], win_ref[...])
        xh, r1 = _rms(x_ref[...])
        n1 = xh * gm_ref[...]
        h1_ref[...] = (n1 * (1.0 + mod_ref[1:2, :]) + mod_ref[0:1, :]).astype(BF16)
        st_ref[0:1, :] += jnp.sum(dh1, axis=0, keepdims=True)
        st_ref[1:2, :] += jnp.sum(dh1 * n1, axis=0, keepdims=True)
        dn1 = dh1 * (1.0 + mod_ref[1:2, :])
        st_ref[2:3, :] += jnp.sum(dn1 * xh, axis=0, keepdims=True)
        gx_ref[...] = _rms_bwd(dn1 * gm_ref[...], xh, r1) + dx1_ref[...]

    row = lambda w: pl.BlockSpec((T, w), lambda i: (i, 0))
    return pl.pallas_call(
        body, name="pre_attention_backward", grid=(S // T,),
        out_shape=[jax.ShapeDtypeStruct((S, D_MODEL), F32), jax.ShapeDtypeStruct((S, Q_W), BF16),
                   jax.ShapeDtypeStruct((HEADS, S, KV_LORA), BF16),
                   jax.ShapeDtypeStruct((S, Q_LORA), BF16), jax.ShapeDtypeStruct((S, PROJ_W), BF16),
                   jax.ShapeDtypeStruct((S, D_MODEL), BF16), jax.ShapeDtypeStruct((8, D_MODEL), F32)],
        in_specs=[row(D_MODEL), row(D_MODEL), row(PROJ_W),
                  pl.BlockSpec((T // TQ, QK_PAD, HEADS * TQ), lambda i: (i, 0, 0)),
                  row(QK_PAD), row(POOL_W), row(128), row(128), _full((N_MOD, D_MODEL)), _full((1, D_MODEL)),
                  _full((1, Q_LORA)), _full((1, KV_LORA)), _full((D_MODEL, PROJ_W)), _full((Q_LORA, Q_W)),
                  _full((HEADS, KV_LORA, NOPE))],
        out_specs=[row(D_MODEL), row(Q_W), pl.BlockSpec((HEADS, T, KV_LORA), lambda i: (0, i, 0)), row(Q_LORA),
                   row(PROJ_W), row(D_MODEL), _full((8, D_MODEL))],
        compiler_params=_params(("arbitrary",)),
    )(x, dx1, proj, dqt, dkc, du, cos, sin, mod6, g_mix, g_q, g_kv, w_in, w_uq, w_uk_t)


def _ada_grads(c_all, dmod_all, chip):
    cols = N_MOD * D_MODEL // N_CHIPS

    def body(col_ref, c_ref, dcol_ref, dall_ref, gw_ref, gb_ref):
        call = c_ref[...]
        act = call * jax.nn.sigmoid(call)
        gw_ref[...] = _dot_tn(act, dcol_ref[...])
        d = dall_ref[...]
        acc = d[0:1, :]
        for b in range(1, 8):
            acc = acc + d[b:b + 1, :]
        gb_ref[...] = acc

    return pl.pallas_call(
        body, name="ada_grads",
        out_shape=[jax.ShapeDtypeStruct((D_MODEL, cols), F32), jax.ShapeDtypeStruct((1, N_MOD * D_MODEL), F32)],
        grid_spec=pltpu.PrefetchScalarGridSpec(
            num_scalar_prefetch=1, grid=(1,),
            in_specs=[pl.BlockSpec((8, D_MODEL), lambda s, col_ref: (0, 0)),
                      pl.BlockSpec((8, cols), lambda s, col_ref: (0, col_ref[0])),
                      pl.BlockSpec((8, N_MOD * D_MODEL), lambda s, col_ref: (0, 0))],
            out_specs=[pl.BlockSpec((D_MODEL, cols), lambda s, col_ref: (0, 0)),
                       pl.BlockSpec((1, N_MOD * D_MODEL), lambda s, col_ref: (0, 0))]),
        compiler_params=_params(("arbitrary",)),
    )(chip, c_all, dmod_all, dmod_all)


def _adamw(w, g, m, v, name):
    rows, cols = w.shape
    T = _row_tile(rows, 256)

    def body(w_ref, g_ref, m_ref, v_ref, d_ref, nm_ref, nv_ref):
        g = g_ref[...]
        m2 = ADAM_B1 * m_ref[...] + (1.0 - ADAM_B1) * g
        v2 = ADAM_B2 * v_ref[...] + (1.0 - ADAM_B2) * (g * g)
        m_hat = m2 / (1.0 - ADAM_B1 ** ADAM_STEP)
        v_hat = v2 / (1.0 - ADAM_B2 ** ADAM_STEP)
        d_ref[...] = -ADAM_LR * (m_hat / (jnp.sqrt(v_hat) + ADAM_EPS) + ADAM_WD * w_ref[...])
        nm_ref[...] = m2
        nv_ref[...] = v2

    spec = pl.BlockSpec((T, cols), lambda i: (i, 0))
    return pl.pallas_call(
        body, name=name, grid=(rows // T,),
        out_shape=[jax.ShapeDtypeStruct((rows, cols), F32)] * 3,
        in_specs=[spec] * 4, out_specs=[spec] * 3,
        compiler_params=_params(("parallel",)),
    )(w, g, m, v)


SMALL_NAMES = ("w_uk", "w_uv", "w_pool", "g_mix", "g_q", "g_kv", "pool_scale", "g_ffn", "g_final", "b_ada")
SMALL_ROWS = 1664


def _pack_rows(parts):
    flat = jnp.concatenate([p.reshape(-1) for p in parts])
    pad = (-flat.shape[0]) % 128
    if pad:
        flat = jnp.concatenate([flat, jnp.zeros((pad,), F32)])
    return flat.reshape(-1, 128)


def kernel(x, c, positions, w_ada, b_ada, g_mix, w_in, g_q, g_kv, w_uq, w_uk, w_uv, w_pool, pool_scale, w_o, g_ffn, w_gate, w_up, w_down, g_final, loss_target, m_w_ada, m_b_ada, m_g_mix, m_w_in, m_g_q, m_g_kv, m_w_uq, m_w_uk, m_w_uv, m_w_pool, m_pool_scale, m_w_o, m_g_ffn, m_w_gate, m_w_up, m_w_down, m_g_final, v_w_ada, v_b_ada, v_g_mix, v_w_in, v_g_q, v_g_kv, v_w_uq, v_w_uk, v_w_uv, v_w_pool, v_pool_scale, v_w_o, v_g_ffn, v_w_gate, v_w_up, v_w_down, v_g_final):
    S = x.shape[1]
    T = _row_tile(S, 512)
    TQ = _row_tile(S, 256)
    TW = _row_tile(S, 1024)
    ix, iy, ic = lax.axis_index("x"), lax.axis_index("y"), lax.axis_index("c")
    chip = (2 * ix + iy).astype(jnp.int32)
    chip_arr = chip.reshape(1)
    core_arr = ic.astype(jnp.int32).reshape(1)

    xs, tgt = x[0], loss_target[0]

    ada_cols = w_ada.shape[2]
    b_cols = lax.dynamic_slice(b_ada, (0, chip * ada_cols), (1, ada_cols))
    mod, c_all = _mod_exchange(c, w_ada[0], b_cols)
    mod6 = mod.reshape(N_MOD, D_MODEL)

    win = w_in[0]
    win_p = jnp.concatenate([win[:, :O_KR + ROPE], win[:, O_KR:O_KR + ROPE], win[:, O_KR + ROPE:]], axis=1).astype(BF16)
    wuq = w_uq[0]
    wuq_p = jnp.concatenate([wuq[:, h, :NOPE] for h in range(HEADS)] + [wuq[:, h, NOPE:] for h in range(HEADS)],
                            axis=1).astype(BF16)
    gathered = _weight_gather([win_p, wuq_p, w_o[0].astype(BF16), w_gate[0].astype(BF16), w_up[0].astype(BF16),
                               w_down[0].astype(BF16)])
    w_in_f = gathered[0].reshape(D_MODEL, PROJ_W)
    w_uq_f = gathered[1].reshape(Q_LORA, Q_W)
    w_o_f = gathered[2].reshape(1024, D_MODEL)
    w_gate_f, w_up_f, w_down_f = gathered[3], gathered[4], gathered[5]
    w_uk_t = jnp.transpose(w_uk[0], (1, 0, 2)).astype(BF16)
    w_uv_t = jnp.transpose(w_uv[0], (1, 0, 2)).astype(BF16)
    w_pool_b = w_pool[0].astype(BF16)

    half = ROPE // 2
    freqs = jnp.power(ROPE_THETA, -jnp.arange(half, dtype=F32) / half)
    cos, sin = _rope_tables(positions.reshape(S, 1), jnp.tile(freqs, 4).reshape(1, 128))
    proj, q, qc, kc, kct = _pre_attention(xs, mod6, g_mix, g_q, g_kv, w_in_f, w_uq_f, w_uk_t, cos, sin, T, TQ)
    o_lat, y_mla, lse_rows = _attention_fwd(qc, kc, kct, w_uv_t, TQ)
    pooled = _pool_forward(proj)
    x1, mix, mix_in = _mix_out(y_mla, pooled, w_pool_b, pool_scale, w_o_f, xs, mod6, T)
    gate, up, dx2, st_f = _ffn_forward(x1, mod6, g_ffn, g_final.reshape(1, D_MODEL), tgt, w_gate_f, w_up_f, w_down_f, T)

    dgate, dup, act, dff, h2, dx1, st_b = _ffn_backward(dx2, x1, gate, up, mod6, g_ffn, w_gate_f, w_up_f, w_down_f, T)
    steps = S // TW
    chunk_spec = pl.BlockSpec((None, TW, FF_CHUNK), lambda g, i: (g, i, 0))
    wide_spec = pl.BlockSpec((TW, D_MODEL), lambda g, i: (i, 0))
    g_down = _tn_matmul(act, dff, chunk_spec, wide_spec, N_CHIPS, FF_CHUNK, D_MODEL, steps, "grad_w_down")
    g_gate = _tn_matmul(h2, dgate, wide_spec, chunk_spec, N_CHIPS, D_MODEL, FF_CHUNK, steps, "grad_w_gate")
    g_up = _tn_matmul(h2, dup, wide_spec, chunk_spec, N_CHIPS, D_MODEL, FF_CHUNK, steps, "grad_w_up")

    dmix, dz, dpooled, dy_mla, do_lat, delta_rows, st_m = _mix_backward(
        dx1, mix, mod6, w_o_f, pooled, w_pool_b, pool_scale, w_uv_t, o_lat, T, TQ)
    g_o = _tn_matmul(mix_in, dmix, wide_spec, wide_spec, 1, 1024, D_MODEL, steps, "grad_w_o")
    col128 = pl.BlockSpec((TW, 128), lambda g, i: (i, g))
    head128 = pl.BlockSpec((None, TW, 128), lambda g, i: (g, i, 0))
    g_pool = _tn_matmul(pooled, dz, col128, col128, 4, POOL_GROUP, POOL_GROUP, steps, "grad_w_pool")
    g_uv_t = _tn_matmul(o_lat, dy_mla, head128, col128, HEADS, KV_LORA, 128, steps, "grad_w_uv")
    du = _pool_backward(dpooled)
    dkc, dqt = _attention_bwd(qc, kc, kct, do_lat, lse_rows, delta_rows, TQ)
    grad_x, dq, dql, c_q, dproj, h1, st_p = _pre_attention_backward(
        xs, dx1, proj, dqt, dkc, du, cos, sin, mod6, g_mix, g_q, g_kv, w_in_f, w_uq_f, w_uk_t, T, TQ)
    g_uk_t = _tn_matmul(dql, q, head128, col128, HEADS, KV_LORA, NOPE, steps, "grad_w_uk")
    g_uq_p = _tn_matmul(c_q, dq, pl.BlockSpec((TW, Q_LORA), lambda g, i: (i, 0)),
                        pl.BlockSpec((TW, Q_W), lambda g, i: (i, 0)), 1, Q_LORA, Q_W, steps, "grad_w_uq")
    g_in_p = _tn_matmul(h1, dproj, wide_spec, pl.BlockSpec((TW, PROJ_W), lambda g, i: (i, 0)), 1, D_MODEL, PROJ_W,
                        steps, "grad_w_in")

    g_in = jnp.concatenate([g_in_p[0][:, :O_KR + ROPE], g_in_p[0][:, O_U:]], axis=1).reshape(N_CHIPS, -1, 960)
    uq = g_uq_p[0]
    g_uq = jnp.concatenate([jnp.concatenate([uq[:, h * NOPE:(h + 1) * NOPE], uq[:, O_QA + h * ROPE:O_QA + (h + 1) * ROPE]],
                                            axis=1) for h in range(HEADS)], axis=1).reshape(N_CHIPS, -1, HEADS * HEAD_QK)
    small = _pack_rows([g_uk_t, g_uv_t, g_pool, st_p[2], st_p[3, :Q_LORA], st_p[4, :KV_LORA], st_m[1, :POOL_W],
                        st_b[2], st_f[0]])
    small = jnp.concatenate([small, jnp.zeros((SMALL_ROWS - small.shape[0], 128), F32)]).reshape(N_CHIPS, -1, 128)
    grads = [g_in, g_uq, g_o.reshape(N_CHIPS, -1, D_MODEL), g_gate, g_up, g_down, small]
    dmod = jnp.stack([st_p[0], st_p[1], st_m[0], st_b[0], st_b[1], st_f[1]]).reshape(48, 128)

    names = ("w_in", "w_uq", "w_o", "w_gate", "w_up", "w_down", "small")
    got, dmod_all = _grad_swap_halves(grads, dmod)
    chip_sums = [_add_my_half(core_arr, a, b, "add_half_" + n) for a, b, n in zip(grads, got, names)]
    others = _grad_chip_exchange(chip_sums)
    halves = [_add_chips(chip_arr, a, b, "add_chips_" + n) for a, b, n in zip(chip_sums, others, names)]
    fulls, small_all = _grad_finish(halves[:6], halves[6])
    gw_in, gw_uq, gw_o, gw_gate, gw_up, gw_down = [f.reshape(-1, f.shape[2]) for f in fulls]
    small_all = small_all.reshape(SMALL_ROWS * 128)

    gw_ada, gb_ada = _ada_grads(c_all, dmod_all.reshape(8, N_MOD * D_MODEL), chip_arr)

    n_sq = KV_LORA * HEADS * 128
    sizes = [n_sq, n_sq, n_sq, D_MODEL, Q_LORA, KV_LORA, POOL_W, D_MODEL, D_MODEL]
    offs = [0]
    for s_ in sizes:
        offs.append(offs[-1] + s_)
    piece = lambda k: small_all[offs[k]:offs[k + 1]]
    grads_small = {
        "w_uk": jnp.transpose(piece(0).reshape(HEADS, KV_LORA, NOPE), (1, 0, 2)),
        "w_uv": jnp.transpose(piece(1).reshape(HEADS, KV_LORA, 128), (1, 0, 2)),
        "w_pool": piece(2).reshape(4, POOL_GROUP, POOL_GROUP),
        "g_mix": piece(3), "g_q": piece(4), "g_kv": piece(5), "pool_scale": piece(6), "g_ffn": piece(7),
        "g_final": piece(8), "b_ada": gb_ada.reshape(-1),
    }
    weights_small = {"w_uk": w_uk, "w_uv": w_uv, "w_pool": w_pool, "g_mix": g_mix, "g_q": g_q, "g_kv": g_kv,
                     "pool_scale": pool_scale, "g_ffn": g_ffn, "g_final": g_final, "b_ada": b_ada}
    m_small = {"w_uk": m_w_uk, "w_uv": m_w_uv, "w_pool": m_w_pool, "g_mix": m_g_mix, "g_q": m_g_q, "g_kv": m_g_kv,
               "pool_scale": m_pool_scale, "g_ffn": m_g_ffn, "g_final": m_g_final, "b_ada": m_b_ada}
    v_small = {"w_uk": v_w_uk, "w_uv": v_w_uv, "w_pool": v_w_pool, "g_mix": v_g_mix, "g_q": v_g_q, "g_kv": v_g_kv,
               "pool_scale": v_pool_scale, "g_ffn": v_g_ffn, "g_final": v_g_final, "b_ada": v_b_ada}
    pack = lambda d: _pack_rows([d[n] for n in SMALL_NAMES])
    d_s, m_s, v_s = _adamw(pack(weights_small), pack(grads_small), pack(m_small), pack(v_small), "adamw_small")

    def unpack(flat2d):
        flat = flat2d.reshape(-1)
        out, o = {}, 0
        for n in SMALL_NAMES:
            size = weights_small[n].size
            out[n] = flat[o:o + size].reshape(weights_small[n].shape)
            o += size
        return out

    delta_s, newm_s, newv_s = unpack(d_s), unpack(m_s), unpack(v_s)

    big_g = {"w_ada": gw_ada, "w_in": gw_in, "w_uq": gw_uq, "w_o": gw_o, "w_gate": gw_gate, "w_up": gw_up,
             "w_down": gw_down}
    big_w = {"w_ada": w_ada, "w_in": w_in, "w_uq": w_uq, "w_o": w_o, "w_gate": w_gate, "w_up": w_up, "w_down": w_down}
    big_m = {"w_ada": m_w_ada, "w_in": m_w_in, "w_uq": m_w_uq, "w_o": m_w_o, "w_gate": m_w_gate, "w_up": m_w_up,
             "w_down": m_w_down}
    big_v = {"w_ada": v_w_ada, "w_in": v_w_in, "w_uq": v_w_uq, "w_o": v_w_o, "w_gate": v_w_gate, "w_up": v_w_up,
             "w_down": v_w_down}
    grad_out, delta_out, newm_out, newv_out = {}, {}, {}, {}
    for n, g2 in big_g.items():
        shape = big_w[n].shape
        flat = lambda a: a.reshape(g2.shape)
        d_, m_, v_ = _adamw(flat(big_w[n]), g2, flat(big_m[n]), flat(big_v[n]), "adamw_" + n)
        grad_out[n], delta_out[n], newm_out[n], newv_out[n] = (a.reshape(shape) for a in (g2, d_, m_, v_))
    for n in SMALL_NAMES:
        grad_out[n] = grads_small[n].reshape(weights_small[n].shape)
        delta_out[n], newm_out[n], newv_out[n] = delta_s[n], newm_s[n], newv_s[n]

    loss = lax.psum(st_f[2, 0], ("x", "y", "c"))
    order = ("w_ada", "b_ada", "g_mix", "w_in", "g_q", "g_kv", "w_uq", "w_uk", "w_uv", "w_pool", "pool_scale", "w_o",
             "g_ffn", "w_gate", "w_up", "w_down", "g_final")
    return (loss, grad_x.reshape(x.shape), *[grad_out[n] for n in order], *[delta_out[n] for n in order],
            *[newm_out[n] for n in order], *[newv_out[n] for n in order])
```

```python
import functools

import jax
import jax.numpy as jnp
from jax import lax
from jax.experimental import pallas as pl
from jax.experimental.pallas import tpu as pltpu

F32 = jnp.float32
BF16 = jnp.bfloat16

D_MODEL = 1024
HEADS = 4
NOPE = 128
ROPE = 64
HEAD_QK = NOPE + ROPE
Q_LORA = 256
KV_LORA = 128
POOL_W = 512
POOL_WINDOWS = (2, 4, 8, 16)
POOL_GROUP = 128
POOL_PAD = 16
D_FF = 2816
N_CHIPS = 4
FF_CHUNK = D_FF // N_CHIPS
N_MOD = 6
EPS = 1e-6
SM_SCALE = HEAD_QK ** -0.5
ROPE_THETA = 10000.0
QK_PAD = 256
CHUNK = 64
CHUNK_SHIFT = 6

ADAM_LR = 0.001
ADAM_B1 = 0.9
ADAM_B2 = 0.999
ADAM_EPS = 1e-08
ADAM_WD = 0.01
ADAM_STEP = 10

VMEM_LIMIT = 48 * 1024 * 1024
MESH = pl.DeviceIdType.MESH
ANY = pl.BlockSpec(memory_space=pl.ANY)
VMEM_SPEC = pl.BlockSpec(memory_space=pltpu.VMEM)

PROJ_W = 1024
O_CKV = 256
O_KR = 384
O_U = 512
Q_W = 768
O_QA = 512
O_QB = 640


def _params(sem=None, vmem=VMEM_LIMIT):
    kw = dict(vmem_limit_bytes=vmem)
    if sem is not None:
        kw["dimension_semantics"] = sem
    return pltpu.CompilerParams(**kw)


def _dot(a, b):
    return jnp.dot(a.astype(BF16), b.astype(BF16), preferred_element_type=F32)


def _dot_nt(a, b):
    return lax.dot_general(a.astype(BF16), b.astype(BF16), (((1,), (1,)), ((), ())), preferred_element_type=F32)


def _dot_tn(a, b):
    return lax.dot_general(a.astype(BF16), b.astype(BF16), (((0,), (0,)), ((), ())), preferred_element_type=F32)


def _row_tile(rows, target):
    best = rows
    for t in range(8, min(rows, target) + 1, 8):
        if rows % t == 0:
            best = t
    return best if rows % best == 0 and best <= target else rows


def _rms(x):
    r = lax.rsqrt(jnp.mean(x * x, axis=-1, keepdims=True) + EPS)
    return x * r, r


def _rms_bwd(dxh, xh, r):
    return r * (dxh - xh * jnp.mean(dxh * xh, axis=-1, keepdims=True))


def _lane_first_half(shape):
    lane = lax.broadcasted_iota(jnp.int32, shape, 1)
    return (lane & (ROPE - 1)) < (ROPE // 2)


def _rope(a, cos, sin):
    first = _lane_first_half(a.shape)
    up = pltpu.roll(a, 96, 1)
    dn = pltpu.roll(a, 32, 1)
    return a * cos + jnp.where(first, -up, dn) * sin


def _rope_bwd(d, cos, sin):
    first = _lane_first_half(d.shape)
    up = pltpu.roll(d, 96, 1)
    dn = pltpu.roll(d, 32, 1)
    return d * cos + jnp.where(first, up, -dn) * sin


RELATIONS = tuple((dx, dy, dc) for dx in (0, 1) for dy in (0, 1) for dc in (0, 1) if (dx, dy, dc) != (0, 0, 0))
CHIP_RELATIONS = ((1, 0), (0, 1), (1, 1))


def _flip(v, d):
    return 1 - v if d else v


def _place():
    return lax.axis_index("x"), lax.axis_index("y"), lax.axis_index("c")


def _remote(src, dst, send_sem, recv_sem, target):
    return pltpu.make_async_remote_copy(src_ref=src, dst_ref=dst, send_sem=send_sem, recv_sem=recv_sem,
                                        device_id=target, device_id_type=MESH)


def _mod_exchange(c_row, w_ada, b_ada):
    cols = w_ada.shape[1]

    def body(c_ref, w_ref, b_ref, mod_ref, call_ref, part_ref, send1, recv1, loc1, send2, recv2, loc2):
        x, y, c = _place()
        me = 4 * x + 2 * y + c
        own = pltpu.make_async_copy(c_ref, call_ref.at[pl.ds(me, 1)], loc1)
        own.start()
        sends = []
        for k, (dx, dy, dc) in enumerate(RELATIONS):
            cp = _remote(c_ref, call_ref.at[pl.ds(me, 1)], send1.at[k], recv1.at[k],
                         (_flip(x, dx), _flip(y, dy), _flip(c, dc)))
            cp.start()
            sends.append(cp)
        for k, (dx, dy, dc) in enumerate(RELATIONS):
            src = 4 * _flip(x, dx) + 2 * _flip(y, dy) + _flip(c, dc)
            _remote(c_ref, call_ref.at[pl.ds(src, 1)], send1.at[k], recv1.at[k], (x, y, c)).wait_recv()
        own.wait()
        for cp in sends:
            cp.wait_send()
        call = call_ref[...]
        act = call * jax.nn.sigmoid(call)
        part_ref[...] = _dot(act, w_ref[...]) + b_ref[...]
        chip = 2 * x + y
        mine = pltpu.make_async_copy(part_ref.at[pl.ds(me, 1)], mod_ref.at[pl.ds(chip, 1)], loc2)
        mine.start()
        sends = []
        for k, (dx, dy) in enumerate(CHIP_RELATIONS):
            tx, ty = _flip(x, dx), _flip(y, dy)
            tb = 4 * tx + 2 * ty + c
            cp = _remote(part_ref.at[pl.ds(tb, 1)], mod_ref.at[pl.ds(chip, 1)], send2.at[k], recv2.at[k], (tx, ty, c))
            cp.start()
            sends.append(cp)
        for k, (dx, dy) in enumerate(CHIP_RELATIONS):
            src_chip = 2 * _flip(x, dx) + _flip(y, dy)
            _remote(part_ref.at[pl.ds(me, 1)], mod_ref.at[pl.ds(src_chip, 1)], send2.at[k], recv2.at[k],
                    (x, y, c)).wait_recv()
        mine.wait()
        for cp in sends:
            cp.wait_send()

    return pl.pallas_call(
        body, name="mod_exchange",
        out_shape=[jax.ShapeDtypeStruct((N_CHIPS, cols), F32), jax.ShapeDtypeStruct((8, D_MODEL), F32)],
        in_specs=[VMEM_SPEC, VMEM_SPEC, VMEM_SPEC], out_specs=[VMEM_SPEC, VMEM_SPEC],
        scratch_shapes=[pltpu.VMEM((8, cols), F32),
                        pltpu.SemaphoreType.DMA((7,)), pltpu.SemaphoreType.DMA((7,)), pltpu.SemaphoreType.DMA,
                        pltpu.SemaphoreType.DMA((3,)), pltpu.SemaphoreType.DMA((3,)), pltpu.SemaphoreType.DMA],
        compiler_params=_params(),
    )(c_row, w_ada, b_ada)


def _weight_gather(shards):
    n = len(shards)

    def body(*refs):
        ins, outs = refs[:n], refs[n:2 * n]
        send_sems, recv_sems, loc_sems = refs[2 * n:]
        x, y, c = _place()
        chip = 2 * x + y
        local = []
        for w in range(n):
            cp = pltpu.make_async_copy(ins[w], outs[w].at[chip], loc_sems.at[w])
            cp.start()
            local.append(cp)
        sends = []
        for w in range(n):
            hr = ins[w].shape[0] // 2
            half = pl.ds(c * hr, hr)
            for k, (dx, dy) in enumerate(CHIP_RELATIONS):
                cp = _remote(ins[w].at[half], outs[w].at[chip, half], send_sems.at[w, k], recv_sems.at[w, k],
                             (_flip(x, dx), _flip(y, dy), c))
                cp.start()
                sends.append(cp)
        for w in range(n):
            hr = ins[w].shape[0] // 2
            half = pl.ds(c * hr, hr)
            for k, (dx, dy) in enumerate(CHIP_RELATIONS):
                src_chip = 2 * _flip(x, dx) + _flip(y, dy)
                got = outs[w].at[src_chip, half]
                _remote(got, got, send_sems.at[w, k], recv_sems.at[w, k], (x, y, c)).wait_recv()
                cp = _remote(got, got, send_sems.at[w, 3 + k], recv_sems.at[w, 3 + k], (x, y, 1 - c))
                cp.start()
                sends.append(cp)
        for w in range(n):
            hr = ins[w].shape[0] // 2
            other = pl.ds((1 - c) * hr, hr)
            for k, (dx, dy) in enumerate(CHIP_RELATIONS):
                src_chip = 2 * _flip(x, dx) + _flip(y, dy)
                got = outs[w].at[src_chip, other]
                _remote(got, got, send_sems.at[w, 3 + k], recv_sems.at[w, 3 + k], (x, y, c)).wait_recv()
        for cp in sends:
            cp.wait_send()
        for cp in local:
            cp.wait()

    return pl.pallas_call(
        body, name="weight_gather",
        out_shape=[jax.ShapeDtypeStruct((N_CHIPS,) + s.shape, s.dtype) for s in shards],
        in_specs=[ANY] * n, out_specs=[ANY] * n,
        scratch_shapes=[pltpu.SemaphoreType.DMA((n, 6)), pltpu.SemaphoreType.DMA((n, 6)),
                        pltpu.SemaphoreType.DMA((n,))],
        compiler_params=_params(),
    )(*shards)


HBM_SPEC = pl.BlockSpec(memory_space=pltpu.HBM)
SEM_SPEC = pl.BlockSpec(memory_space=pltpu.SEMAPHORE)
DATAFLOW = pltpu.SideEffectType.DATAFLOW_SIDE_EFFECTING


def _in_hbm(a):
    return pltpu.with_memory_space_constraint(a, pltpu.HBM)


def _hbm_like(arrays):
    return [pltpu.HBM(a.shape, a.dtype) for a in arrays]


def _split_start(name, srcs, land_shapes, n_remote, n_local, plan):
    n, m = len(srcs), len(land_shapes)
    lands = [lax.empty(s.shape, s.dtype) for s in land_shapes]

    def body(*refs):
        src_refs, land_refs = refs[:n], refs[n:n + m]
        send_sems, recv_sems = refs[n + m], refs[n + m + 1]
        token, loc_sems = refs[2 * (n + m) + 2], refs[2 * (n + m) + 3]
        remote, local = plan(_place(), src_refs, land_refs)
        assert len(remote) == n_remote and len(local) == n_local
        mine = []
        for i, (s, d) in enumerate(local):
            cp = pltpu.make_async_copy(s, d, loc_sems.at[i])
            cp.start()
            mine.append(cp)
        for i, (s, d, target) in enumerate(remote):
            _remote(s, d, send_sems.at[i], recv_sems.at[i], target).start()
        for cp in mine:
            cp.wait()
        token[...] = jnp.zeros_like(token)

    res = pl.pallas_call(
        body, name=name,
        out_shape=(pltpu.SemaphoreType.DMA((n_remote,)), pltpu.SemaphoreType.DMA((n_remote,)),
                   *_hbm_like(srcs), *_hbm_like(lands), jax.ShapeDtypeStruct((8, 128), F32)),
        in_specs=[HBM_SPEC] * (n + m),
        out_specs=(SEM_SPEC, SEM_SPEC, *([HBM_SPEC] * (n + m)), VMEM_SPEC),
        input_output_aliases={i: 2 + i for i in range(n + m)},
        scratch_shapes=[pltpu.SemaphoreType.DMA((max(n_local, 1),))],
        compiler_params=pltpu.CompilerParams(has_side_effects=DATAFLOW),
    )(*[_in_hbm(a) for a in srcs], *[_in_hbm(a) for a in lands])
    return res[0], res[1], list(res[2:2 + n]), list(res[2 + n:2 + n + m]), res[2 + n + m]


def _split_wait(name, send_sems, recv_sems, srcs, lands, after, plan):
    n, m = len(srcs), len(lands)

    def body(*refs):
        src_refs, land_refs = refs[:n], refs[n:n + m]
        send_sems, recv_sems = refs[n + m], refs[n + m + 1]
        place = _place()
        for i, (s, d) in enumerate(plan(place, src_refs, land_refs)):
            cp = _remote(s, d, send_sems.at[i], recv_sems.at[i], place)
            cp.wait_send()
            cp.wait_recv()

    res = pl.pallas_call(
        body, name=name,
        out_shape=(*_hbm_like(srcs), *_hbm_like(lands)),
        in_specs=[HBM_SPEC] * (n + m) + [SEM_SPEC, SEM_SPEC, ANY],
        out_specs=tuple([HBM_SPEC] * (n + m)),
        input_output_aliases={i: i for i in range(n + m)},
        compiler_params=pltpu.CompilerParams(has_side_effects=DATAFLOW),
    )(*srcs, *lands, send_sems, recv_sems, after)
    return list(res[n:])


def _split_relay(name, send_sems, recv_sems, srcs, lands, after, n_remote, plan_wait, plan_send):
    n, m = len(srcs), len(lands)

    def body(*refs):
        src_refs, land_refs = refs[:n], refs[n:n + m]
        old_send, old_recv = refs[n + m], refs[n + m + 1]
        new_send, new_recv = refs[n + m + 3], refs[n + m + 4]
        token = refs[n + m + 5 + m]
        place = _place()
        for i, (s, d) in enumerate(plan_wait(place, src_refs, land_refs)):
            cp = _remote(s, d, old_send.at[i], old_recv.at[i], place)
            cp.wait_send()
            cp.wait_recv()
        for i, (s, d, target) in enumerate(plan_send(place, land_refs)):
            _remote(s, d, new_send.at[i], new_recv.at[i], target).start()
        token[...] = jnp.zeros_like(token)

    res = pl.pallas_call(
        body, name=name,
        out_shape=(pltpu.SemaphoreType.DMA((n_remote,)), pltpu.SemaphoreType.DMA((n_remote,)),
                   *_hbm_like(lands), jax.ShapeDtypeStruct((8, 128), F32)),
        in_specs=[HBM_SPEC] * (n + m) + [SEM_SPEC, SEM_SPEC, ANY],
        out_specs=(SEM_SPEC, SEM_SPEC, *([HBM_SPEC] * m), VMEM_SPEC),
        input_output_aliases={n + i: 2 + i for i in range(m)},
        compiler_params=pltpu.CompilerParams(has_side_effects=DATAFLOW),
    )(*srcs, *lands, send_sems, recv_sems, after)
    return res[0], res[1], list(res[2:2 + m]), res[2 + m]


def _half(ref, core, axis=0):
    hr = ref.shape[axis] // 2
    return pl.ds(core * hr, hr)


def _plan_gather_start(place, src, land):
    x, y, c = place
    chip = 2 * x + y
    remote, local = [], []
    for s, l in zip(src, land):
        local.append((s, l.at[chip]))
        for dx, dy in CHIP_RELATIONS:
            remote.append((s.at[_half(s, c)], l.at[chip, _half(s, c)], (_flip(x, dx), _flip(y, dy), c)))
    return remote, local


def _plan_gather_landed(place, src, land):
    x, y, c = place
    return [(s.at[_half(s, c)], l.at[2 * _flip(x, dx) + _flip(y, dy), _half(s, c)])
            for s, l in zip(src, land) for dx, dy in CHIP_RELATIONS]


def _plan_gather_relay(place, land):
    x, y, c = place
    out = []
    for l in land:
        for dx, dy in CHIP_RELATIONS:
            got = l.at[2 * _flip(x, dx) + _flip(y, dy), _half(l, c, 1)]
            out.append((got, got, (x, y, 1 - c)))
    return out


def _plan_gather_wait(place, src, land):
    x, y, c = place
    out = []
    for l in land:
        for dx, dy in CHIP_RELATIONS:
            got = l.at[2 * _flip(x, dx) + _flip(y, dy), _half(l, 1 - c, 1)]
            out.append((got, got))
    return out


def _plan_swap_start(place, src, land):
    x, y, c = place
    return [(s.at[:, _half(s, 1 - c, 1), :], l, (x, y, 1 - c)) for s, l in zip(src, land)], []


def _plan_swap_wait(place, src, land):
    return [(s.at[:, _half(s, 0, 1), :], l) for s, l in zip(src, land)]


def _plan_exchange_start(place, src, land):
    x, y, c = place
    remote = []
    for s, l in zip(src, land):
        for k, (dx, dy) in enumerate(CHIP_RELATIONS):
            tx, ty = _flip(x, dx), _flip(y, dy)
            remote.append((s.at[2 * tx + ty], l.at[k], (tx, ty, c)))
    return remote, []


def _plan_exchange_wait(place, src, land):
    return [(s.at[0], l.at[k]) for s, l in zip(src, land) for k in range(3)]


def _plan_finish_start(place, src, land):
    x, y, c = place
    return ([(s, l.at[c], (x, y, 1 - c)) for s, l in zip(src, land)], [(s, l.at[c]) for s, l in zip(src, land)])


def _plan_finish_wait(place, src, land):
    x, y, c = place
    return [(s, l.at[1 - c]) for s, l in zip(src, land)]


def _grad_swap_halves(grads, dmod):
    n = len(grads)

    def body(*refs):
        ins, dmod_ref = refs[:n], refs[n]
        outs, dall_ref = refs[n + 1:2 * n + 1], refs[2 * n + 1]
        send_sems, recv_sems, dsend, drecv, dloc = refs[2 * n + 2:]
        x, y, c = _place()
        me = 4 * x + 2 * y + c
        sends = []
        for w in range(n):
            hr = ins[w].shape[1] // 2
            cp = _remote(ins[w].at[:, pl.ds((1 - c) * hr, hr), :], outs[w], send_sems.at[w], recv_sems.at[w],
                         (x, y, 1 - c))
            cp.start()
            sends.append(cp)
        own = pltpu.make_async_copy(dmod_ref, dall_ref.at[me], dloc)
        own.start()
        for k, (dx, dy, dc) in enumerate(RELATIONS):
            cp = _remote(dmod_ref, dall_ref.at[me], dsend.at[k], drecv.at[k],
                         (_flip(x, dx), _flip(y, dy), _flip(c, dc)))
            cp.start()
            sends.append(cp)
        for k, (dx, dy, dc) in enumerate(RELATIONS):
            src = 4 * _flip(x, dx) + 2 * _flip(y, dy) + _flip(c, dc)
            _remote(dmod_ref, dall_ref.at[src], dsend.at[k], drecv.at[k], (x, y, c)).wait_recv()
        for w in range(n):
            _remote(outs[w], outs[w], send_sems.at[w], recv_sems.at[w], (x, y, c)).wait_recv()
        own.wait()
        for cp in sends:
            cp.wait_send()

    out_shape = [jax.ShapeDtypeStruct((N_CHIPS, g.shape[1] // 2, g.shape[2]), F32) for g in grads]
    out_shape.append(jax.ShapeDtypeStruct((8,) + dmod.shape, F32))
    res = pl.pallas_call(
        body, name="grad_swap_halves",
        out_shape=out_shape, in_specs=[ANY] * (n + 1), out_specs=[ANY] * (n + 1),
        scratch_shapes=[pltpu.SemaphoreType.DMA((n,)), pltpu.SemaphoreType.DMA((n,)),
                        pltpu.SemaphoreType.DMA((7,)), pltpu.SemaphoreType.DMA((7,)), pltpu.SemaphoreType.DMA],
        compiler_params=_params(),
    )(*grads, dmod)
    return res[:n], res[n]


def _grad_chip_exchange(sums):
    n = len(sums)

    def body(*refs):
        ins, outs = refs[:n], refs[n:2 * n]
        send_sems, recv_sems = refs[2 * n:]
        x, y, c = _place()
        sends = []
        for w in range(n):
            for k, (dx, dy) in enumerate(CHIP_RELATIONS):
                tx, ty = _flip(x, dx), _flip(y, dy)
                cp = _remote(ins[w].at[2 * tx + ty], outs[w].at[k], send_sems.at[w, k], recv_sems.at[w, k], (tx, ty, c))
                cp.start()
                sends.append(cp)
        for w in range(n):
            for k in range(3):
                _remote(outs[w].at[k], outs[w].at[k], send_sems.at[w, k], recv_sems.at[w, k], (x, y, c)).wait_recv()
        for cp in sends:
            cp.wait_send()

    return pl.pallas_call(
        body, name="grad_chip_exchange",
        out_shape=[jax.ShapeDtypeStruct((3,) + s.shape[1:], F32) for s in sums],
        in_specs=[ANY] * n, out_specs=[ANY] * n,
        scratch_shapes=[pltpu.SemaphoreType.DMA((n, 3)), pltpu.SemaphoreType.DMA((n, 3))],
        compiler_params=_params(),
    )(*sums)


def _grad_finish(halves, small_half):
    n = len(halves)

    def body(*refs):
        ins, sm_ref = refs[:n], refs[n]
        outs, sall_ref = refs[n + 1:2 * n + 1], refs[2 * n + 1]
        send_sems, recv_sems, loc_sems, ssend, srecv, sloc = refs[2 * n + 2:]
        x, y, c = _place()
        chip = 2 * x + y
        local, sends = [], []
        for w in range(n):
            cp = pltpu.make_async_copy(ins[w], outs[w].at[c], loc_sems.at[w])
            cp.start()
            local.append(cp)
            cp = _remote(ins[w], outs[w].at[c], send_sems.at[w], recv_sems.at[w], (x, y, 1 - c))
            cp.start()
            sends.append(cp)
        cp = pltpu.make_async_copy(sm_ref, sall_ref.at[chip, c], sloc)
        cp.start()
        local.append(cp)
        for k, (dx, dy, dc) in enumerate(RELATIONS):
            cp = _remote(sm_ref, sall_ref.at[chip, c], ssend.at[k], srecv.at[k],
                         (_flip(x, dx), _flip(y, dy), _flip(c, dc)))
            cp.start()
            sends.append(cp)
        for k, (dx, dy, dc) in enumerate(RELATIONS):
            got = sall_ref.at[2 * _flip(x, dx) + _flip(y, dy), _flip(c, dc)]
            _remote(got, got, ssend.at[k], srecv.at[k], (x, y, c)).wait_recv()
        for w in range(n):
            got = outs[w].at[1 - c]
            _remote(got, got, send_sems.at[w], recv_sems.at[w], (x, y, c)).wait_recv()
        for cp in sends:
            cp.wait_send()
        for cp in local:
            cp.wait()

    out_shape = [jax.ShapeDtypeStruct((2,) + h.shape, F32) for h in halves]
    out_shape.append(jax.ShapeDtypeStruct((N_CHIPS, 2) + small_half.shape, F32))
    res = pl.pallas_call(
        body, name="grad_finish",
        out_shape=out_shape, in_specs=[ANY] * (n + 1), out_specs=[ANY] * (n + 1),
        scratch_shapes=[pltpu.SemaphoreType.DMA((n,)), pltpu.SemaphoreType.DMA((n,)), pltpu.SemaphoreType.DMA((n,)),
                        pltpu.SemaphoreType.DMA((7,)), pltpu.SemaphoreType.DMA((7,)), pltpu.SemaphoreType.DMA],
        compiler_params=_params(),
    )(*halves, small_half)
    return res[:n], res[n]


def _add_my_half(core, full, got, name):
    _, hr, cols = got.shape

    def body(core_ref, a_ref, b_ref, o_ref):
        o_ref[...] = a_ref[...] + b_ref[...]

    return pl.pallas_call(
        body, name=name,
        out_shape=jax.ShapeDtypeStruct(got.shape, F32),
        grid_spec=pltpu.PrefetchScalarGridSpec(
            num_scalar_prefetch=1, grid=(N_CHIPS,),
            in_specs=[pl.BlockSpec((None, hr, cols), lambda s, core_ref: (s, core_ref[0], 0)),
                      pl.BlockSpec((None, hr, cols), lambda s, core_ref: (s, 0, 0))],
            out_specs=pl.BlockSpec((None, hr, cols), lambda s, core_ref: (s, 0, 0))),
        compiler_params=_params(("arbitrary",)),
    )(core, full, got)


def _add_chips(chip, mine, got, name):
    _, hr, cols = mine.shape

    def body(chip_ref, a_ref, b_ref, o_ref):
        o_ref[...] = ((a_ref[...] + b_ref[0]) + b_ref[1]) + b_ref[2]

    return pl.pallas_call(
        body, name=name,
        out_shape=jax.ShapeDtypeStruct((hr, cols), F32),
        grid_spec=pltpu.PrefetchScalarGridSpec(
            num_scalar_prefetch=1, grid=(1,),
            in_specs=[pl.BlockSpec((None, hr, cols), lambda s, chip_ref: (chip_ref[0], 0, 0)),
                      pl.BlockSpec((3, hr, cols), lambda s, chip_ref: (0, 0, 0))],
            out_specs=pl.BlockSpec((hr, cols), lambda s, chip_ref: (0, 0))),
        compiler_params=_params(("arbitrary",)),
    )(chip, mine, got)


def _rope_tables(pos_col, freqs):
    S = pos_col.shape[0]
    T = _row_tile(S, 1024)

    def body(p_ref, f_ref, cos_ref, sin_ref):
        ang = p_ref[...].astype(F32) * f_ref[...]
        cos_ref[...] = jnp.cos(ang)
        sin_ref[...] = jnp.sin(ang)

    return pl.pallas_call(
        body, name="rope_tables", grid=(S // T,),
        out_shape=[jax.ShapeDtypeStruct((S, 128), F32)] * 2,
        in_specs=[pl.BlockSpec((T, 1), lambda i: (i, 0)), pl.BlockSpec((1, 128), lambda i: (0, 0))],
        out_specs=[pl.BlockSpec((T, 128), lambda i: (i, 0))] * 2,
        compiler_params=_params(("parallel",)),
    )(pos_col, freqs)


def _full(shape):
    zeros = (0,) * len(shape)
    return pl.BlockSpec(shape, lambda *_: zeros)


def _pre_attention(x, mod6, g_mix, g_q, g_kv, w_in, w_uq, w_uk_t, cos, sin, T, TQ):
    S = x.shape[0]

    def body(x_ref, mod_ref, gm_ref, gq_ref, gkv_ref, win_ref, wuq_ref, wuk_ref, cos_ref, sin_ref,
             proj_ref, q_ref, qc_ref, kc_ref, kct_ref):
        xh, _ = _rms(x_ref[...])
        h1 = (xh * gm_ref[...]) * (1.0 + mod_ref[1:2, :]) + mod_ref[0:1, :]
        proj = _dot(h1, win_ref[...])
        proj_ref[...] = proj
        cqh, _ = _rms(proj[:, :Q_LORA])
        c_q = cqh * gq_ref[...]
        ckvh, _ = _rms(proj[:, O_CKV:O_KR])
        c_kv = ckvh * gkv_ref[...]
        q = _dot(c_q, wuq_ref[...])
        q_ref[...] = q
        cos_t, sin_t = cos_ref[...], sin_ref[...]
        ropes = (_rope(q[:, O_QA:O_QB], cos_t, sin_t), _rope(q[:, O_QB:Q_W], cos_t, sin_t))
        low = lax.broadcasted_iota(jnp.int32, (T, 128), 1) < ROPE
        for h in range(HEADS):
            q_lat = _dot_nt(q[:, h * NOPE:(h + 1) * NOPE], wuk_ref[h])
            keep = low if h % 2 == 0 else jnp.logical_not(low)
            qc_ref[h, :, 0:KV_LORA] = q_lat.astype(BF16)
            qc_ref[h, :, KV_LORA:QK_PAD] = jnp.where(keep, ropes[h // 2], 0.0).astype(BF16)
        k_rope = _rope(proj[:, O_KR:O_U], cos_t, sin_t)
        kc_ref[:, 0:KV_LORA] = c_kv.astype(BF16)
        kc_ref[:, KV_LORA:QK_PAD] = k_rope.astype(BF16)
        lat_t, rope_t = jnp.transpose(c_kv), jnp.transpose(k_rope)
        for s in range(T // TQ):
            kct_ref[s, 0:KV_LORA, :] = lat_t[:, s * TQ:(s + 1) * TQ].astype(BF16)
            kct_ref[s, KV_LORA:QK_PAD, :] = rope_t[:, s * TQ:(s + 1) * TQ].astype(BF16)

    row = lambda w: pl.BlockSpec((T, w), lambda i: (i, 0))
    return pl.pallas_call(
        body, name="pre_attention", grid=(S // T,),
        out_shape=[jax.ShapeDtypeStruct((S, PROJ_W), F32), jax.ShapeDtypeStruct((S, Q_W), F32),
                   jax.ShapeDtypeStruct((HEADS, S, QK_PAD), BF16), jax.ShapeDtypeStruct((S, QK_PAD), BF16),
                   jax.ShapeDtypeStruct((S // TQ, QK_PAD, TQ), BF16)],
        in_specs=[row(D_MODEL), _full((N_MOD, D_MODEL)), _full((1, D_MODEL)), _full((1, Q_LORA)), _full((1, KV_LORA)),
                  _full((D_MODEL, PROJ_W)), _full((Q_LORA, Q_W)), _full((HEADS, KV_LORA, NOPE)), row(128), row(128)],
        out_specs=[row(PROJ_W), row(Q_W), pl.BlockSpec((HEADS, T, QK_PAD), lambda i: (0, i, 0)), row(QK_PAD),
                   pl.BlockSpec((T // TQ, QK_PAD, TQ), lambda i: (i, 0, 0))],
        compiler_params=_params(("parallel",)),
    )(x, mod6, g_mix, g_q, g_kv, w_in, w_uq, w_uk_t, cos, sin)


def _diag_mask(TQ, transposed):
    R = HEADS * TQ
    if transposed:
        key = lax.broadcasted_iota(jnp.int32, (TQ, R), 0) >> CHUNK_SHIFT
        qry = (lax.broadcasted_iota(jnp.int32, (TQ, R), 1) & (TQ - 1)) >> CHUNK_SHIFT
    else:
        qry = (lax.broadcasted_iota(jnp.int32, (R, TQ), 0) & (TQ - 1)) >> CHUNK_SHIFT
        key = lax.broadcasted_iota(jnp.int32, (R, TQ), 1) >> CHUNK_SHIFT
    return key <= qry


def _col_to_row(col):
    return jnp.transpose(jnp.broadcast_to(col, (col.shape[0], 128)))[0:1, :]


def _attention_fwd(qc, kc, kct, w_uv_t, TQ):
    S = kc.shape[0]
    R = HEADS * TQ
    nq = S // TQ

    def body(q_ref, k_ref, kt_ref, wuv_ref, o_ref, y_ref, lser_ref, m_s, l_s, acc_s):
        i = pl.program_id(0)
        q = q_ref[...].reshape(R, QK_PAD)
        m_s[...] = jnp.full((1, R), -jnp.inf, F32)
        l_s[...] = jnp.zeros((1, R), F32)
        acc_s[...] = jnp.zeros((KV_LORA, R), F32)

        def step(j, masked):
            k = k_ref[pl.ds(pl.multiple_of(j * TQ, TQ), TQ), :]
            st = _dot_nt(k, q) * SM_SCALE
            if masked:
                st = jnp.where(_diag_mask(TQ, True), st, -jnp.inf)
            m_old = m_s[...]
            m_new = jnp.maximum(m_old, jnp.max(st, axis=0, keepdims=True))
            pt = jnp.exp(st - m_new)
            alpha = jnp.exp(m_old - m_new)
            l_s[...] = alpha * l_s[...] + jnp.sum(pt, axis=0, keepdims=True)
            acc_s[...] = alpha * acc_s[...] + _dot(kt_ref[j, 0:KV_LORA, :], pt)
            m_s[...] = m_new

        def loop(j, carry):
            step(j, False)
            return carry

        lax.fori_loop(0, i, loop, 0)
        step(i, True)
        l = l_s[...]
        lser_ref[0] = m_s[...] + jnp.log(l)
        o = jnp.transpose(acc_s[...] / l).astype(BF16)
        for h in range(HEADS):
            oh = o[h * TQ:(h + 1) * TQ, :]
            o_ref[h] = oh
            y_ref[:, h * 128:(h + 1) * 128] = _dot(oh, wuv_ref[h]).astype(BF16)

    return pl.pallas_call(
        body, name="attention_fwd", grid=(nq,),
        out_shape=[jax.ShapeDtypeStruct((HEADS, S, KV_LORA), BF16), jax.ShapeDtypeStruct((S, HEADS * 128), BF16),
                   jax.ShapeDtypeStruct((nq, 1, R), F32)],
        in_specs=[pl.BlockSpec((HEADS, TQ, QK_PAD), lambda i: (0, i, 0)), _full((S, QK_PAD)),
                  _full((nq, QK_PAD, TQ)), _full((HEADS, KV_LORA, 128))],
        out_specs=[pl.BlockSpec((HEADS, TQ, KV_LORA), lambda i: (0, i, 0)), pl.BlockSpec((TQ, HEADS * 128), lambda i: (i, 0)),
                   pl.BlockSpec((1, 1, R), lambda i: (i, 0, 0))],
        scratch_shapes=[pltpu.VMEM((1, R), F32), pltpu.VMEM((1, R), F32), pltpu.VMEM((KV_LORA, R), F32)],
        compiler_params=_params(("parallel",)),
    )(qc, kc, kct, w_uv_t)


def _pool_forward(proj):
    S = proj.shape[0]
    RB = _row_tile(S, 256)

    def body(proj_ref, out_ref, pad_ref, sem):
        cp = pltpu.make_async_copy(proj_ref.at[:, pl.ds(O_U, POOL_W)], pad_ref.at[pl.ds(POOL_PAD, S)], sem)
        cp.start()
        pad_ref[0:POOL_PAD, :] = jnp.zeros((POOL_PAD, POOL_W), F32)
        cp.wait()
        for g, win in enumerate(POOL_WINDOWS):
            cols = slice(g * POOL_GROUP, (g + 1) * POOL_GROUP)
            for r0 in range(0, S, RB):
                u = pad_ref[POOL_PAD + r0:POOL_PAD + r0 + RB, cols]
                acc = u
                for k in range(1, win):
                    acc = acc + pad_ref[POOL_PAD + r0 - k:POOL_PAD + r0 - k + RB, cols]
                if r0 == 0:
                    t1 = (lax.broadcasted_iota(jnp.int32, (RB, POOL_GROUP), 0) + 1).astype(F32)
                    mean = acc / jnp.minimum(t1, float(win))
                else:
                    mean = acc * (1.0 / win)
                out_ref[r0:r0 + RB, cols] = (mean - u).astype(BF16)

    return pl.pallas_call(
        body, name="pool_forward",
        out_shape=jax.ShapeDtypeStruct((S, POOL_W), BF16),
        in_specs=[ANY], out_specs=VMEM_SPEC,
        scratch_shapes=[pltpu.VMEM((S + POOL_PAD, POOL_W), F32), pltpu.SemaphoreType.DMA],
        compiler_params=_params(),
    )(proj)


def _pool_backward(dpooled):
    S = dpooled.shape[0]
    RB = _row_tile(S, 256)

    def body(dp_ref, out_ref, pad_ref, sem):
        cp = pltpu.make_async_copy(dp_ref, pad_ref.at[pl.ds(0, S)], sem)
        cp.start()
        pad_ref[S:S + POOL_PAD, :] = jnp.zeros((POOL_PAD, POOL_W), F32)
        cp.wait()
        for g, win in enumerate(POOL_WINDOWS):
            cols = slice(g * POOL_GROUP, (g + 1) * POOL_GROUP)
            head = pad_ref[0:POOL_PAD, cols]
            t1 = (lax.broadcasted_iota(jnp.int32, (POOL_PAD, POOL_GROUP), 0) + 1).astype(F32)
            pad_ref[0:POOL_PAD, cols] = head * (float(win) / jnp.minimum(t1, float(win)))
            for r0 in range(0, S, RB):
                acc = pad_ref[r0:r0 + RB, cols]
                for k in range(1, win):
                    acc = acc + pad_ref[r0 + k:r0 + k + RB, cols]
                own = pad_ref[r0:r0 + RB, cols]
                if r0 == 0:
                    own = jnp.concatenate([head, own[POOL_PAD:]], axis=0)
                out_ref[r0:r0 + RB, cols] = acc * (1.0 / win) - own

    return pl.pallas_call(
        body, name="pool_backward",
        out_shape=jax.ShapeDtypeStruct((S, POOL_W), F32),
        in_specs=[ANY], out_specs=VMEM_SPEC,
        scratch_shapes=[pltpu.VMEM((S + POOL_PAD, POOL_W), F32), pltpu.SemaphoreType.DMA],
        compiler_params=_params(),
    )(dpooled)


def _mix_out(y_mla, pooled, w_pool, pool_scale, w_o, x, mod6, T):
    S = x.shape[0]

    def body(ym_ref, pl_ref, wp_ref, ps_ref, wo_ref, x_ref, mod_ref, x1_ref, mix_ref, mi_ref):
        mi_ref[:, 0:512] = ym_ref[...]
        for g in range(len(POOL_WINDOWS)):
            cols = slice(g * POOL_GROUP, (g + 1) * POOL_GROUP)
            z = _dot(pl_ref[:, cols], wp_ref[g])
            mi_ref[:, 512 + g * POOL_GROUP:512 + (g + 1) * POOL_GROUP] = (z * ps_ref[:, cols]).astype(BF16)
        mix = _dot(mi_ref[...], wo_ref[...])
        mix_ref[...] = mix
        x1_ref[...] = x_ref[...] + mod_ref[2:3, :] * mix

    row = lambda w: pl.BlockSpec((T, w), lambda i: (i, 0))
    return pl.pallas_call(
        body, name="mix_out", grid=(S // T,),
        out_shape=[jax.ShapeDtypeStruct((S, D_MODEL), F32), jax.ShapeDtypeStruct((S, D_MODEL), F32),
                   jax.ShapeDtypeStruct((S, 1024), BF16)],
        in_specs=[row(512), row(POOL_W), _full((4, POOL_GROUP, POOL_GROUP)), _full((1, POOL_W)),
                  _full((1024, D_MODEL)), row(D_MODEL), _full((N_MOD, D_MODEL))],
        out_specs=[row(D_MODEL), row(D_MODEL), row(1024)],
        compiler_params=_params(("parallel",)),
    )(y_mla, pooled, w_pool, pool_scale, w_o, x, mod6)


def _ffn_forward(x1, mod6, g_ffn, g_final, target, w_gate, w_up, w_down, T):
    S = x1.shape[0]

    def body(x1_ref, mod_ref, gf_ref, gl_ref, tgt_ref, wg_ref, wu_ref, wd_ref,
             gate_ref, up_ref, dx2_ref, st_ref, h2_s, acc_s):
        i, j = pl.program_id(0), pl.program_id(1)

        @pl.when(jnp.logical_and(i == 0, j == 0))
        def _():
            st_ref[...] = jnp.zeros_like(st_ref)

        @pl.when(j == 0)
        def _():
            xh, _ = _rms(x1_ref[...])
            h2_s[...] = ((xh * gf_ref[...]) * (1.0 + mod_ref[4:5, :]) + mod_ref[3:4, :]).astype(BF16)
            acc_s[...] = jnp.zeros_like(acc_s)

        h2 = h2_s[...]
        gate = _dot(h2, wg_ref[...])
        up = _dot(h2, wu_ref[...])
        gate_ref[...] = gate
        up_ref[...] = up
        act = gate * jax.nn.sigmoid(gate) * up
        acc_s[...] += _dot(act, wd_ref[...])

        @pl.when(j == N_CHIPS - 1)
        def _():
            ff = acc_s[...]
            x2 = x1_ref[...] + mod_ref[5:6, :] * ff
            xh, r3 = _rms(x2)
            err = xh * gl_ref[...] - tgt_ref[...]
            dy = err * (1.0 / D_MODEL)
            dx2 = _rms_bwd(dy * gl_ref[...], xh, r3)
            dx2_ref[...] = dx2
            st_ref[0:1, :] += jnp.sum(dy * xh, axis=0, keepdims=True)
            st_ref[1:2, :] += jnp.sum(dx2 * ff, axis=0, keepdims=True)
            st_ref[2:3, :] += 0.5 * jnp.sum(err * dy)

    row = pl.BlockSpec((T, D_MODEL), lambda i, j: (i, 0))
    chunk_out = pl.BlockSpec((None, T, FF_CHUNK), lambda i, j: (j, i, 0))
    return pl.pallas_call(
        body, name="ffn_forward", grid=(S // T, N_CHIPS),
        out_shape=[jax.ShapeDtypeStruct((N_CHIPS, S, FF_CHUNK), F32), jax.ShapeDtypeStruct((N_CHIPS, S, FF_CHUNK), F32),
                   jax.ShapeDtypeStruct((S, D_MODEL), F32), jax.ShapeDtypeStruct((8, D_MODEL), F32)],
        in_specs=[row, _full((N_MOD, D_MODEL)), _full((1, D_MODEL)), _full((1, D_MODEL)), row,
                  pl.BlockSpec((None, D_MODEL, FF_CHUNK), lambda i, j: (j, 0, 0)),
                  pl.BlockSpec((None, D_MODEL, FF_CHUNK), lambda i, j: (j, 0, 0)),
                  pl.BlockSpec((None, FF_CHUNK, D_MODEL), lambda i, j: (j, 0, 0))],
        out_specs=[chunk_out, chunk_out, row, _full((8, D_MODEL))],
        scratch_shapes=[pltpu.VMEM((T, D_MODEL), BF16), pltpu.VMEM((T, D_MODEL), F32)],
        compiler_params=_params(("arbitrary", "arbitrary")),
    )(x1, mod6, g_ffn, g_final, target, w_gate, w_up, w_down)


def _ffn_backward(dx2, x1, gate, up, mod6, g_ffn, w_gate, w_up, w_down, T):
    S = x1.shape[0]

    def body(dx2_ref, x1_ref, gate_ref, up_ref, mod_ref, gf_ref, wg_ref, wu_ref, wd_ref,
             dgate_ref, dup_ref, act_ref, dff_ref, h2_ref, dx1_ref, st_ref, acc_s):
        i, j = pl.program_id(0), pl.program_id(1)

        @pl.when(jnp.logical_and(i == 0, j == 0))
        def _():
            st_ref[...] = jnp.zeros_like(st_ref)

        @pl.when(j == 0)
        def _():
            dff_ref[...] = (dx2_ref[...] * mod_ref[5:6, :]).astype(BF16)
            xh, _ = _rms(x1_ref[...])
            h2_ref[...] = ((xh * gf_ref[...]) * (1.0 + mod_ref[4:5, :]) + mod_ref[3:4, :]).astype(BF16)
            acc_s[...] = jnp.zeros_like(acc_s)

        gate, up = gate_ref[...], up_ref[...]
        sg = jax.nn.sigmoid(gate)
        silu = gate * sg
        act_ref[...] = (silu * up).astype(BF16)
        dact = _dot_nt(dff_ref[...], wd_ref[...])
        dup = (dact * silu).astype(BF16)
        dgate = (dact * up * (sg * (1.0 + gate * (1.0 - sg)))).astype(BF16)
        dup_ref[...] = dup
        dgate_ref[...] = dgate
        acc_s[...] += _dot_nt(dgate, wg_ref[...]) + _dot_nt(dup, wu_ref[...])

        @pl.when(j == N_CHIPS - 1)
        def _():
            dh2 = acc_s[...]
            xh, r2 = _rms(x1_ref[...])
            n2 = xh * gf_ref[...]
            st_ref[0:1, :] += jnp.sum(dh2, axis=0, keepdims=True)
            st_ref[1:2, :] += jnp.sum(dh2 * n2, axis=0, keepdims=True)
            dn2 = dh2 * (1.0 + mod_ref[4:5, :])
            st_ref[2:3, :] += jnp.sum(dn2 * xh, axis=0, keepdims=True)
            dx1_ref[...] = _rms_bwd(dn2 * gf_ref[...], xh, r2) + dx2_ref[...]

    row = pl.BlockSpec((T, D_MODEL), lambda i, j: (i, 0))
    chunk = pl.BlockSpec((None, T, FF_CHUNK), lambda i, j: (j, i, 0))
    big = jax.ShapeDtypeStruct((N_CHIPS, S, FF_CHUNK), BF16)
    return pl.pallas_call(
        body, name="ffn_backward", grid=(S // T, N_CHIPS),
        out_shape=[big, big, big, jax.ShapeDtypeStruct((S, D_MODEL), BF16), jax.ShapeDtypeStruct((S, D_MODEL), BF16),
                   jax.ShapeDtypeStruct((S, D_MODEL), F32), jax.ShapeDtypeStruct((8, D_MODEL), F32)],
        in_specs=[row, row, chunk, chunk, _full((N_MOD, D_MODEL)), _full((1, D_MODEL)),
                  pl.BlockSpec((None, D_MODEL, FF_CHUNK), lambda i, j: (j, 0, 0)),
                  pl.BlockSpec((None, D_MODEL, FF_CHUNK), lambda i, j: (j, 0, 0)),
                  pl.BlockSpec((None, FF_CHUNK, D_MODEL), lambda i, j: (j, 0, 0))],
        out_specs=[chunk, chunk, chunk, row, row, row, _full((8, D_MODEL))],
        scratch_shapes=[pltpu.VMEM((T, D_MODEL), F32)],
        compiler_params=_params(("arbitrary", "arbitrary")),
    )(dx2, x1, gate, up, mod6, g_ffn, w_gate, w_up, w_down)


def _tn_matmul(a, b, a_spec, b_spec, groups, m, n, steps, name):
    def body(a_ref, b_ref, o_ref):
        @pl.when(pl.program_id(1) == 0)
        def _():
            o_ref[...] = jnp.zeros_like(o_ref)

        o_ref[...] += _dot_tn(a_ref[...], b_ref[...])

    return pl.pallas_call(
        body, name=name, grid=(groups, steps),
        out_shape=jax.ShapeDtypeStruct((groups, m, n), F32),
        in_specs=[a_spec, b_spec],
        out_specs=pl.BlockSpec((None, m, n), lambda g, i: (g, 0, 0)),
        compiler_params=_params(("parallel", "arbitrary")),
    )(a, b)


def _mix_backward(dx1, mix, mod6, w_o, pooled, w_pool, pool_scale, w_uv_t, o_lat, T, TQ):
    S = dx1.shape[0]

    def body(dx1_ref, mix_ref, mod_ref, wo_ref, pl_ref, wp_ref, ps_ref, wuv_ref, o_ref,
             dmix_ref, dz_ref, dp_ref, dym_ref, do_ref, dr_ref, st_ref):
        @pl.when(pl.program_id(0) == 0)
        def _():
            st_ref[...] = jnp.zeros_like(st_ref)

        dx1 = dx1_ref[...]
        st_ref[0:1, :] += jnp.sum(dx1 * mix_ref[...], axis=0, keepdims=True)
        dmix = (dx1 * mod_ref[2:3, :]).astype(BF16)
        dmix_ref[...] = dmix
        dmi = _dot_nt(dmix, wo_ref[...])
        dym = dmi[:, 0:512].astype(BF16)
        dym_ref[...] = dym
        for g in range(len(POOL_WINDOWS)):
            cols = slice(g * POOL_GROUP, (g + 1) * POOL_GROUP)
            dyp = dmi[:, 512 + g * POOL_GROUP:512 + (g + 1) * POOL_GROUP]
            z = _dot(pl_ref[:, cols], wp_ref[g])
            st_ref[1:2, cols] += jnp.sum(dyp * z, axis=0, keepdims=True)
            dz = (dyp * ps_ref[:, cols]).astype(BF16)
            dz_ref[:, cols] = dz
            dp_ref[:, cols] = _dot_nt(dz, wp_ref[g])
        for h in range(HEADS):
            do = _dot_nt(dym[:, h * 128:(h + 1) * 128], wuv_ref[h]).astype(BF16)
            do_ref[h] = do
            delta = _col_to_row(jnp.sum(do.astype(F32) * o_ref[h].astype(F32), axis=1, keepdims=True))
            for s in range(T // TQ):
                dr_ref[s, :, h * TQ:(h + 1) * TQ] = delta[:, s * TQ:(s + 1) * TQ]

    row = lambda w: pl.BlockSpec((T, w), lambda i: (i, 0))
    heads = pl.BlockSpec((HEADS, T, KV_LORA), lambda i: (0, i, 0))
    return pl.pallas_call(
        body, name="mix_backward", grid=(S // T,),
        out_shape=[jax.ShapeDtypeStruct((S, D_MODEL), BF16), jax.ShapeDtypeStruct((S, POOL_W), BF16),
                   jax.ShapeDtypeStruct((S, POOL_W), F32), jax.ShapeDtypeStruct((S, 512), BF16),
                   jax.ShapeDtypeStruct((HEADS, S, KV_LORA), BF16), jax.ShapeDtypeStruct((S // TQ, 1, HEADS * TQ), F32),
                   jax.ShapeDtypeStruct((8, D_MODEL), F32)],
        in_specs=[row(D_MODEL), row(D_MODEL), _full((N_MOD, D_MODEL)), _full((1024, D_MODEL)), row(POOL_W),
                  _full((4, POOL_GROUP, POOL_GROUP)), _full((1, POOL_W)), _full((HEADS, KV_LORA, 128)), heads],
        out_specs=[row(D_MODEL), row(POOL_W), row(POOL_W), row(512), heads,
                   pl.BlockSpec((T // TQ, 1, HEADS * TQ), lambda i: (i, 0, 0)), _full((8, D_MODEL))],
        compiler_params=_params(("arbitrary",)),
    )(dx1, mix, mod6, w_o, pooled, w_pool, pool_scale, w_uv_t, o_lat)


def _attention_bwd(qc, kc, kct, do, lse_rows, delta_rows, TQ):
    S = kc.shape[0]
    R = HEADS * TQ
    nq = S // TQ

    def body(k_ref, kt_ref, q_ref, do_ref, lser_ref, dr_ref, dk_ref, dqt_ref, dk_s, dv_s):
        j = pl.program_id(0)

        @pl.when(j == 0)
        def _():
            def zero(i, carry):
                dqt_ref[i] = jnp.zeros((QK_PAD, R), F32)
                return carry
            lax.fori_loop(0, nq, zero, 0)

        k = k_ref[...]
        kt = kt_ref[...]
        v = k[:, :KV_LORA]
        dk_s[...] = jnp.zeros((TQ, QK_PAD), F32)
        dv_s[...] = jnp.zeros((TQ, KV_LORA), F32)

        def step(i, masked):
            rows = pl.ds(pl.multiple_of(i * TQ, TQ), TQ)
            q = q_ref[:, rows, :].reshape(R, QK_PAD)
            do = do_ref[:, rows, :].reshape(R, KV_LORA)
            st = _dot_nt(k, q) * SM_SCALE
            if masked:
                st = jnp.where(_diag_mask(TQ, True), st, -jnp.inf)
            pt = jnp.exp(st - lser_ref[i])
            dv_s[...] += _dot(pt, do)
            dpt = _dot_nt(v, do)
            dst = (pt * (dpt - dr_ref[i])).astype(BF16)
            dk_s[...] += _dot(dst, q)
            dqt_ref[i] += _dot(kt, dst)

        def loop(i, carry):
            step(i, False)
            return carry

        step(j, True)
        lax.fori_loop(j + 1, nq, loop, 0)
        dk = dk_s[...] * SM_SCALE
        dk_ref[:, 0:KV_LORA] = dk[:, 0:KV_LORA] + dv_s[...]
        dk_ref[:, KV_LORA:QK_PAD] = dk[:, KV_LORA:QK_PAD]

    return pl.pallas_call(
        body, name="attention_bwd", grid=(nq,),
        out_shape=[jax.ShapeDtypeStruct((S, QK_PAD), F32), jax.ShapeDtypeStruct((nq, QK_PAD, R), F32)],
        in_specs=[pl.BlockSpec((TQ, QK_PAD), lambda j: (j, 0)), pl.BlockSpec((None, QK_PAD, TQ), lambda j: (j, 0, 0)),
                  VMEM_SPEC, VMEM_SPEC, VMEM_SPEC, VMEM_SPEC],
        out_specs=[pl.BlockSpec((TQ, QK_PAD), lambda j: (j, 0)), VMEM_SPEC],
        scratch_shapes=[pltpu.VMEM((TQ, QK_PAD), F32), pltpu.VMEM((TQ, KV_LORA), F32)],
        compiler_params=_params(("arbitrary",)),
    )(kc, kct, qc, do, lse_rows, delta_rows)


def _pre_attention_backward(x, dx1, proj, dqt, dkc, du, cos, sin, mod6, g_mix, g_q, g_kv, w_in, w_uq, w_uk_t, T, TQ):
    S = x.shape[0]

    def body(x_ref, dx1_ref, proj_ref, dqt_ref, dkc_ref, du_ref, cos_ref, sin_ref, mod_ref, gm_ref, gq_ref, gkv_ref,
             win_ref, wuq_ref, wuk_ref, gx_ref, dq_ref, dql_ref, cq_ref, dproj_ref, h1_ref, st_ref):
        @pl.when(pl.program_id(0) == 0)
        def _():
            st_ref[...] = jnp.zeros_like(st_ref)

        cos_t, sin_t = cos_ref[...], sin_ref[...]
        low = lax.broadcasted_iota(jnp.int32, (T, 128), 1) < ROPE
        rope_parts = []
        for h in range(HEADS):
            dqc = jnp.concatenate([jnp.transpose(dqt_ref[s, :, h * TQ:(h + 1) * TQ]) for s in range(T // TQ)], axis=0)
            dqc = dqc * SM_SCALE
            dql = dqc[:, 0:KV_LORA].astype(BF16)
            dql_ref[h] = dql
            dq_ref[:, h * NOPE:(h + 1) * NOPE] = _dot(dql, wuk_ref[h]).astype(BF16)
            rope_parts.append(dqc[:, KV_LORA:QK_PAD])
        for pair in range(2):
            d = jnp.where(low, rope_parts[2 * pair], rope_parts[2 * pair + 1])
            dq_ref[:, O_QA + 128 * pair:O_QA + 128 * (pair + 1)] = _rope_bwd(d, cos_t, sin_t).astype(BF16)
        dcq = _dot_nt(dq_ref[...], wuq_ref[...])
        cqh, rq = _rms(proj_ref[:, 0:Q_LORA])
        cq_ref[...] = (cqh * gq_ref[...]).astype(BF16)
        st_ref[3:4, 0:Q_LORA] += jnp.sum(dcq * cqh, axis=0, keepdims=True)
        dproj_ref[:, 0:Q_LORA] = _rms_bwd(dcq * gq_ref[...], cqh, rq).astype(BF16)
        dckv = dkc_ref[:, 0:KV_LORA]
        ckvh, rkv = _rms(proj_ref[:, O_CKV:O_KR])
        st_ref[4:5, 0:KV_LORA] += jnp.sum(dckv * ckvh, axis=0, keepdims=True)
        dproj_ref[:, O_CKV:O_KR] = _rms_bwd(dckv * gkv_ref[...], ckvh, rkv).astype(BF16)
        dkr = _rope_bwd(dkc_ref[:, KV_LORA:QK_PAD], cos_t, sin_t)
        dkr = jnp.where(low, dkr + pltpu.roll(dkr, ROPE, 1), 0.0)
        dproj_ref[:, O_KR:O_U] = dkr.astype(BF16)
        dproj_ref[:, O_U:PROJ_W] = du_ref[...].astype(BF16)
        dh1 = _dot_nt(dproj_ref[...], win_ref[...])
        xh, r1 = _rms(x_ref[...])
        n1 = xh * gm_ref[...]
        h1_ref[...] = (n1 * (1.0 + mod_ref[1:2, :]) + mod_ref[0:1, :]).astype(BF16)
        st_ref[0:1, :] += jnp.sum(dh1, axis=0, keepdims=True)
        st_ref[1:2, :] += jnp.sum(dh1 * n1, axis=0, keepdims=True)
        dn1 = dh1 * (1.0 + mod_ref[1:2, :])
        st_ref[2:3, :] += jnp.sum(dn1 * xh, axis=0, keepdims=True)
        gx_ref[...] = _rms_bwd(dn1 * gm_ref[...], xh, r1) + dx1_ref[...]

    row = lambda w: pl.BlockSpec((T, w), lambda i: (i, 0))
    return pl.pallas_call(
        body, name="pre_attention_backward", grid=(S // T,),
        out_shape=[jax.ShapeDtypeStruct((S, D_MODEL), F32), jax.ShapeDtypeStruct((S, Q_W), BF16),
                   jax.ShapeDtypeStruct((HEADS, S, KV_LORA), BF16),
                   jax.ShapeDtypeStruct((S, Q_LORA), BF16), jax.ShapeDtypeStruct((S, PROJ_W), BF16),
                   jax.ShapeDtypeStruct((S, D_MODEL), BF16), jax.ShapeDtypeStruct((8, D_MODEL), F32)],
        in_specs=[row(D_MODEL), row(D_MODEL), row(PROJ_W),
                  pl.BlockSpec((T // TQ, QK_PAD, HEADS * TQ), lambda i: (i, 0, 0)),
                  row(QK_PAD), row(POOL_W), row(128), row(128), _full((N_MOD, D_MODEL)), _full((1, D_MODEL)),
                  _full((1, Q_LORA)), _full((1, KV_LORA)), _full((D_MODEL, PROJ_W)), _full((Q_LORA, Q_W)),
                  _full((HEADS, KV_LORA, NOPE))],
        out_specs=[row(D_MODEL), row(Q_W), pl.BlockSpec((HEADS, T, KV_LORA), lambda i: (0, i, 0)), row(Q_LORA),
                   row(PROJ_W), row(D_MODEL), _full((8, D_MODEL))],
        compiler_params=_params(("arbitrary",)),
    )(x, dx1, proj, dqt, dkc, du, cos, sin, mod6, g_mix, g_q, g_kv, w_in, w_uq, w_uk_t)


def _ada_grads(c_all, dmod_all, chip):
    cols = N_MOD * D_MODEL // N_CHIPS

    def body(col_ref, c_ref, dcol_ref, dall_ref, gw_ref, gb_ref):
        call = c_ref[...]
        act = call * jax.nn.sigmoid(call)
        gw_ref[...] = _dot_tn(act, dcol_ref[...])
        d = dall_ref[...]
        acc = d[0:1, :]
        for b in range(1, 8):
            acc = acc + d[b:b + 1, :]
        gb_ref[...] = acc

    return pl.pallas_call(
        body, name="ada_grads",
        out_shape=[jax.ShapeDtypeStruct((D_MODEL, cols), F32), jax.ShapeDtypeStruct((1, N_MOD * D_MODEL), F32)],
        grid_spec=pltpu.PrefetchScalarGridSpec(
            num_scalar_prefetch=1, grid=(1,),
            in_specs=[pl.BlockSpec((8, D_MODEL), lambda s, col_ref: (0, 0)),
                      pl.BlockSpec((8, cols), lambda s, col_ref: (0, col_ref[0])),
                      pl.BlockSpec((8, N_MOD * D_MODEL), lambda s, col_ref: (0, 0))],
            out_specs=[pl.BlockSpec((D_MODEL, cols), lambda s, col_ref: (0, 0)),
                       pl.BlockSpec((1, N_MOD * D_MODEL), lambda s, col_ref: (0, 0))]),
        compiler_params=_params(("arbitrary",)),
    )(chip, c_all, dmod_all, dmod_all)


def _adamw(w, g, m, v, name):
    rows, cols = w.shape
    T = _row_tile(rows, 256)

    def body(w_ref, g_ref, m_ref, v_ref, d_ref, nm_ref, nv_ref):
        g = g_ref[...]
        m2 = ADAM_B1 * m_ref[...] + (1.0 - ADAM_B1) * g
        v2 = ADAM_B2 * v_ref[...] + (1.0 - ADAM_B2) * (g * g)
        m_hat = m2 / (1.0 - ADAM_B1 ** ADAM_STEP)
        v_hat = v2 / (1.0 - ADAM_B2 ** ADAM_STEP)
        d_ref[...] = -ADAM_LR * (m_hat / (jnp.sqrt(v_hat) + ADAM_EPS) + ADAM_WD * w_ref[...])
        nm_ref[...] = m2
        nv_ref[...] = v2

    spec = pl.BlockSpec((T, cols), lambda i: (i, 0))
    return pl.pallas_call(
        body, name=name, grid=(rows // T,),
        out_shape=[jax.ShapeDtypeStruct((rows, cols), F32)] * 3,
        in_specs=[spec] * 4, out_specs=[spec] * 3,
        compiler_params=_params(("parallel",)),
    )(w, g, m, v)


SMALL_NAMES = ("w_uk", "w_uv", "w_pool", "g_mix", "g_q", "g_kv", "pool_scale", "g_ffn", "g_final", "b_ada")
SMALL_ROWS = 1664


def _pack_rows(parts):
    flat = jnp.concatenate([p.reshape(-1) for p in parts])
    pad = (-flat.shape[0]) % 128
    if pad:
        flat = jnp.concatenate([flat, jnp.zeros((pad,), F32)])
    return flat.reshape(-1, 128)


def kernel(x, c, positions, w_ada, b_ada, g_mix, w_in, g_q, g_kv, w_uq, w_uk, w_uv, w_pool, pool_scale, w_o, g_ffn, w_gate, w_up, w_down, g_final, loss_target, m_w_ada, m_b_ada, m_g_mix, m_w_in, m_g_q, m_g_kv, m_w_uq, m_w_uk, m_w_uv, m_w_pool, m_pool_scale, m_w_o, m_g_ffn, m_w_gate, m_w_up, m_w_down, m_g_final, v_w_ada, v_b_ada, v_g_mix, v_w_in, v_g_q, v_g_kv, v_w_uq, v_w_uk, v_w_uv, v_w_pool, v_pool_scale, v_w_o, v_g_ffn, v_w_gate, v_w_up, v_w_down, v_g_final):
    S = x.shape[1]
    T = _row_tile(S, 512)
    TQ = _row_tile(S, 256)
    TW = _row_tile(S, 1024)
    ix, iy, ic = lax.axis_index("x"), lax.axis_index("y"), lax.axis_index("c")
    chip = (2 * ix + iy).astype(jnp.int32)
    chip_arr = chip.reshape(1)
    core_arr = ic.astype(jnp.int32).reshape(1)

    xs, tgt = x[0], loss_target[0]

    ada_cols = w_ada.shape[2]
    b_cols = lax.dynamic_slice(b_ada, (0, chip * ada_cols), (1, ada_cols))
    mod, c_all = _mod_exchange(c, w_ada[0], b_cols)
    mod6 = mod.reshape(N_MOD, D_MODEL)

    win = w_in[0]
    win_p = jnp.concatenate([win[:, :O_KR + ROPE], win[:, O_KR:O_KR + ROPE], win[:, O_KR + ROPE:]], axis=1).astype(BF16)
    wuq = w_uq[0]
    wuq_p = jnp.concatenate([wuq[:, h, :NOPE] for h in range(HEADS)] + [wuq[:, h, NOPE:] for h in range(HEADS)],
                            axis=1).astype(BF16)
    first = _weight_gather([win_p, wuq_p])
    w_in_f = first[0].reshape(D_MODEL, PROJ_W)
    w_uq_f = first[1].reshape(Q_LORA, Q_W)
    w_uk_t = jnp.transpose(w_uk[0], (1, 0, 2)).astype(BF16)
    w_uv_t = jnp.transpose(w_uv[0], (1, 0, 2)).astype(BF16)
    w_pool_b = w_pool[0].astype(BF16)
    later = [w_o[0].astype(BF16), w_gate[0].astype(BF16), w_up[0].astype(BF16), w_down[0].astype(BF16)]
    later, mod6, w_in_f = lax.optimization_barrier((later, mod6, w_in_f))
    wg_send, wg_recv, later, wg_lands, token = _split_start(
        "weights_start", later, [jax.ShapeDtypeStruct((N_CHIPS,) + a.shape, BF16) for a in later],
        3 * len(later), len(later), _plan_gather_start)
    mod6 = mod6 + token[0, 0]

    half = ROPE // 2
    freqs = jnp.power(ROPE_THETA, -jnp.arange(half, dtype=F32) / half)
    cos, sin = _rope_tables(positions.reshape(S, 1), jnp.tile(freqs, 4).reshape(1, 128))
    proj, q, qc, kc, kct = _pre_attention(xs, mod6, g_mix, g_q, g_kv, w_in_f, w_uq_f, w_uk_t, cos, sin, T, TQ)
    o_lat, y_mla, lse_rows = _attention_fwd(qc, kc, kct, w_uv_t, TQ)
    wg_send, wg_recv, wg_lands, token = _split_relay(
        "weights_relay", wg_send, wg_recv, later, wg_lands, y_mla, 3 * len(later), _plan_gather_landed,
        _plan_gather_relay)
    pooled = _pool_forward(proj)
    wg_lands = _split_wait("weights_wait", wg_send, wg_recv, [], wg_lands, pooled, _plan_gather_wait)
    w_o_f = wg_lands[0].reshape(1024, D_MODEL)
    w_gate_f, w_up_f, w_down_f = wg_lands[1], wg_lands[2], wg_lands[3]
    x1, mix, mix_in = _mix_out(y_mla, pooled, w_pool_b, pool_scale, w_o_f, xs, mod6, T)
    gate, up, dx2, st_f = _ffn_forward(x1, mod6, g_ffn, g_final.reshape(1, D_MODEL), tgt, w_gate_f, w_up_f, w_down_f, T)

    dgate, dup, act, dff, h2, dx1, st_b = _ffn_backward(dx2, x1, gate, up, mod6, g_ffn, w_gate_f, w_up_f, w_down_f, T)
    steps = S // TW
    chunk_spec = pl.BlockSpec((None, TW, FF_CHUNK), lambda g, i: (g, i, 0))
    wide_spec = pl.BlockSpec((TW, D_MODEL), lambda g, i: (i, 0))
    g_down = _tn_matmul(act, dff, chunk_spec, wide_spec, N_CHIPS, FF_CHUNK, D_MODEL, steps, "grad_w_down")
    g_gate = _tn_matmul(h2, dgate, wide_spec, chunk_spec, N_CHIPS, D_MODEL, FF_CHUNK, steps, "grad_w_gate")
    g_up = _tn_matmul(h2, dup, wide_spec, chunk_spec, N_CHIPS, D_MODEL, FF_CHUNK, steps, "grad_w_up")

    ffn_names = ("w_gate", "w_up", "w_down")
    ffn_grads = [g_gate, g_up, g_down]
    f_send, f_recv, ffn_grads, f_lands, token = _split_start(
        "ffn_swap_start", ffn_grads,
        [jax.ShapeDtypeStruct((N_CHIPS, g.shape[1] // 2, g.shape[2]), F32) for g in ffn_grads], 3, 0, _plan_swap_start)
    mod6 = mod6 + token[0, 0]

    dmix, dz, dpooled, dy_mla, do_lat, delta_rows, st_m = _mix_backward(
        dx1, mix, mod6, w_o_f, pooled, w_pool_b, pool_scale, w_uv_t, o_lat, T, TQ)
    g_o = _tn_matmul(mix_in, dmix, wide_spec, wide_spec, 1, 1024, D_MODEL, steps, "grad_w_o")
    col128 = pl.BlockSpec((TW, 128), lambda g, i: (i, g))
    head128 = pl.BlockSpec((None, TW, 128), lambda g, i: (g, i, 0))
    g_pool = _tn_matmul(pooled, dz, col128, col128, 4, POOL_GROUP, POOL_GROUP, steps, "grad_w_pool")
    g_uv_t = _tn_matmul(o_lat, dy_mla, head128, col128, HEADS, KV_LORA, 128, steps, "grad_w_uv")
    du = _pool_backward(dpooled)
    f_got = _split_wait("ffn_swap_wait", f_send, f_recv, ffn_grads, f_lands, du, _plan_swap_wait)
    f_sums = [_add_my_half(core_arr, a, b, "add_half_" + n) for a, b, n in zip(ffn_grads, f_got, ffn_names)]
    f_send, f_recv, f_sums, f_lands, token = _split_start(
        "ffn_exchange_start", f_sums, [jax.ShapeDtypeStruct((3,) + s.shape[1:], F32) for s in f_sums], 9, 0,
        _plan_exchange_start)
    delta_rows = delta_rows + token[0, 0]
    dkc, dqt = _attention_bwd(qc, kc, kct, do_lat, lse_rows, delta_rows, TQ)
    grad_x, dq, dql, c_q, dproj, h1, st_p = _pre_attention_backward(
        xs, dx1, proj, dqt, dkc, du, cos, sin, mod6, g_mix, g_q, g_kv, w_in_f, w_uq_f, w_uk_t, T, TQ)
    g_uk_t = _tn_matmul(dql, q, head128, col128, HEADS, KV_LORA, NOPE, steps, "grad_w_uk")
    g_uq_p = _tn_matmul(c_q, dq, pl.BlockSpec((TW, Q_LORA), lambda g, i: (i, 0)),
                        pl.BlockSpec((TW, Q_W), lambda g, i: (i, 0)), 1, Q_LORA, Q_W, steps, "grad_w_uq")
    g_in_p = _tn_matmul(h1, dproj, wide_spec, pl.BlockSpec((TW, PROJ_W), lambda g, i: (i, 0)), 1, D_MODEL, PROJ_W,
                        steps, "grad_w_in")

    g_in = jnp.concatenate([g_in_p[0][:, :O_KR + ROPE], g_in_p[0][:, O_U:]], axis=1).reshape(N_CHIPS, -1, 960)
    uq = g_uq_p[0]
    g_uq = jnp.concatenate([jnp.concatenate([uq[:, h * NOPE:(h + 1) * NOPE], uq[:, O_QA + h * ROPE:O_QA + (h + 1) * ROPE]],
                                            axis=1) for h in range(HEADS)], axis=1).reshape(N_CHIPS, -1, HEADS * HEAD_QK)
    small = _pack_rows([g_uk_t, g_uv_t, g_pool, st_p[2], st_p[3, :Q_LORA], st_p[4, :KV_LORA], st_m[1, :POOL_W],
                        st_b[2], st_f[0]])
    small = jnp.concatenate([small, jnp.zeros((SMALL_ROWS - small.shape[0], 128), F32)]).reshape(N_CHIPS, -1, 128)
    grads = [g_in, g_uq, g_o.reshape(N_CHIPS, -1, D_MODEL), small]
    dmod = jnp.stack([st_p[0], st_p[1], st_m[0], st_b[0], st_b[1], st_f[1]]).reshape(48, 128)

    f_others = _split_wait("ffn_exchange_wait", f_send, f_recv, f_sums, f_lands, g_in_p, _plan_exchange_wait)
    f_halves = [_add_chips(chip_arr, a, b, "add_chips_" + n) for a, b, n in zip(f_sums, f_others, ffn_names)]
    f_send, f_recv, f_halves, f_lands, token = _split_start(
        "ffn_finish_start", f_halves, [jax.ShapeDtypeStruct((2,) + h.shape, F32) for h in f_halves], 3, 3,
        _plan_finish_start)
    dmod = dmod + token[0, 0]

    names = ("w_in", "w_uq", "w_o", "small")
    got, dmod_all = _grad_swap_halves(grads, dmod)
    chip_sums = [_add_my_half(core_arr, a, b, "add_half_" + n) for a, b, n in zip(grads, got, names)]
    others = _grad_chip_exchange(chip_sums)
    halves = [_add_chips(chip_arr, a, b, "add_chips_" + n) for a, b, n in zip(chip_sums, others, names)]
    fulls, small_all = _grad_finish(halves[:3], halves[3])
    gw_in, gw_uq, gw_o = [f.reshape(-1, f.shape[2]) for f in fulls]
    f_fulls = _split_wait("ffn_finish_wait", f_send, f_recv, f_halves, f_lands, small_all, _plan_finish_wait)
    gw_gate, gw_up, gw_down = [f.reshape(-1, f.shape[2]) for f in f_fulls]
    small_all = small_all.reshape(SMALL_ROWS * 128)

    gw_ada, gb_ada = _ada_grads(c_all, dmod_all.reshape(8, N_MOD * D_MODEL), chip_arr)

    n_sq = KV_LORA * HEADS * 128
    sizes = [n_sq, n_sq, n_sq, D_MODEL, Q_LORA, KV_LORA, POOL_W, D_MODEL, D_MODEL]
    offs = [0]
    for s_ in sizes:
        offs.append(offs[-1] + s_)
    piece = lambda k: small_all[offs[k]:offs[k + 1]]
    grads_small = {
        "w_uk": jnp.transpose(piece(0).reshape(HEADS, KV_LORA, NOPE), (1, 0, 2)),
        "w_uv": jnp.transpose(piece(1).reshape(HEADS, KV_LORA, 128), (1, 0, 2)),
        "w_pool": piece(2).reshape(4, POOL_GROUP, POOL_GROUP),
        "g_mix": piece(3), "g_q": piece(4), "g_kv": piece(5), "pool_scale": piece(6), "g_ffn": piece(7),
        "g_final": piece(8), "b_ada": gb_ada.reshape(-1),
    }
    weights_small = {"w_uk": w_uk, "w_uv": w_uv, "w_pool": w_pool, "g_mix": g_mix, "g_q": g_q, "g_kv": g_kv,
                     "pool_scale": pool_scale, "g_ffn": g_ffn, "g_final": g_final, "b_ada": b_ada}
    m_small = {"w_uk": m_w_uk, "w_uv": m_w_uv, "w_pool": m_w_pool, "g_mix": m_g_mix, "g_q": m_g_q, "g_kv": m_g_kv,
               "pool_scale": m_pool_scale, "g_ffn": m_g_ffn, "g_final": m_g_final, "b_ada": m_b_ada}
    v_small = {"w_uk": v_w_uk, "w_uv": v_w_uv, "w_pool": v_w_pool, "g_mix": v_g_mix, "g_q": v_g_q, "g_kv": v_g_kv,
               "pool_scale": v_pool_scale, "g_ffn": v_g_ffn, "g_final": v_g_final, "b_ada": v_b_ada}
    pack = lambda d: _pack_rows([d[n] for n in SMALL_NAMES])
    d_s, m_s, v_s = _adamw(pack(weights_small), pack(grads_small), pack(m_small), pack(v_small), "adamw_small")

    def unpack(flat2d):
        flat = flat2d.reshape(-1)
        out, o = {}, 0
        for n in SMALL_NAMES:
            size = weights_small[n].size
            out[n] = flat[o:o + size].reshape(weights_small[n].shape)
            o += size
        return out

    delta_s, newm_s, newv_s = unpack(d_s), unpack(m_s), unpack(v_s)

    big_g = {"w_ada": gw_ada, "w_in": gw_in, "w_uq": gw_uq, "w_o": gw_o, "w_gate": gw_gate, "w_up": gw_up,
             "w_down": gw_down}
    big_w = {"w_ada": w_ada, "w_in": w_in, "w_uq": w_uq, "w_o": w_o, "w_gate": w_gate, "w_up": w_up, "w_down": w_down}
    big_m = {"w_ada": m_w_ada, "w_in": m_w_in, "w_uq": m_w_uq, "w_o": m_w_o, "w_gate": m_w_gate, "w_up": m_w_up,
             "w_down": m_w_down}
    big_v = {"w_ada": v_w_ada, "w_in": v_w_in, "w_uq": v_w_uq, "w_o": v_w_o, "w_gate": v_w_gate, "w_up": v_w_up,
             "w_down": v_w_down}
    grad_out, delta_out, newm_out, newv_out = {}, {}, {}, {}
    for n, g2 in big_g.items():
        shape = big_w[n].shape
        flat = lambda a: a.reshape(g2.shape)
        d_, m_, v_ = _adamw(flat(big_w[n]), g2, flat(big_m[n]), flat(big_v[n]), "adamw_" + n)
        grad_out[n], delta_out[n], newm_out[n], newv_out[n] = (a.reshape(shape) for a in (g2, d_, m_, v_))
    for n in SMALL_NAMES:
        grad_out[n] = grads_small[n].reshape(weights_small[n].shape)
        delta_out[n], newm_out[n], newv_out[n] = delta_s[n], newm_s[n], newv_s[n]

    loss = lax.psum(st_f[2, 0], ("x", "y", "c"))
    order = ("w_ada", "b_ada", "g_mix", "w_in", "g_q", "g_kv", "w_uq", "w_uk", "w_uv", "w_pool", "pool_scale", "w_o",
             "g_ffn", "w_gate", "w_up", "w_down", "g_final")
    return (loss, grad_x.reshape(x.shape), *[grad_out[n] for n in order], *[delta_out[n] for n in order],
            *[newm_out[n] for n in order], *[newv_out[n] for n in order])
```

```python
import functools

import jax
import jax.numpy as jnp
from jax import lax
from jax.experimental import pallas as pl
from jax.experimental.pallas import tpu as pltpu

F32 = jnp.float32
BF16 = jnp.bfloat16

D_MODEL = 1024
HEADS = 4
NOPE = 128
ROPE = 64
HEAD_QK = NOPE + ROPE
Q_LORA = 256
KV_LORA = 128
POOL_W = 512
POOL_WINDOWS = (2, 4, 8, 16)
POOL_GROUP = 128
POOL_PAD = 16
D_FF = 2816
N_CHIPS = 4
FF_CHUNK = D_FF // N_CHIPS
N_MOD = 6
EPS = 1e-6
SM_SCALE = HEAD_QK ** -0.5
ROPE_THETA = 10000.0
QK_PAD = 256
CHUNK = 64
CHUNK_SHIFT = 6

ADAM_LR = 0.001
ADAM_B1 = 0.9
ADAM_B2 = 0.999
ADAM_EPS = 1e-08
ADAM_WD = 0.01
ADAM_STEP = 10

VMEM_LIMIT = 48 * 1024 * 1024
MESH = pl.DeviceIdType.MESH
ANY = pl.BlockSpec(memory_space=pl.ANY)
VMEM_SPEC = pl.BlockSpec(memory_space=pltpu.VMEM)

PROJ_W = 1024
O_CKV = 256
O_KR = 384
O_U = 512
Q_W = 768
O_QA = 512
O_QB = 640


def _params(sem=None, vmem=VMEM_LIMIT):
    kw = dict(vmem_limit_bytes=vmem)
    if sem is not None:
        kw["dimension_semantics"] = sem
    return pltpu.CompilerParams(**kw)


def _dot(a, b):
    return jnp.dot(a.astype(BF16), b.astype(BF16), preferred_element_type=F32)


def _dot_nt(a, b):
    return lax.dot_general(a.astype(BF16), b.astype(BF16), (((1,), (1,)), ((), ())), preferred_element_type=F32)


def _dot_tn(a, b):
    return lax.dot_general(a.astype(BF16), b.astype(BF16), (((0,), (0,)), ((), ())), preferred_element_type=F32)


def _row_tile(rows, target):
    best = rows
    for t in range(8, min(rows, target) + 1, 8):
        if rows % t == 0:
            best = t
    return best if rows % best == 0 and best <= target else rows


def _rms(x):
    r = lax.rsqrt(jnp.mean(x * x, axis=-1, keepdims=True) + EPS)
    return x * r, r


def _rms_bwd(dxh, xh, r):
    return r * (dxh - xh * jnp.mean(dxh * xh, axis=-1, keepdims=True))


def _lane_first_half(shape):
    lane = lax.broadcasted_iota(jnp.int32, shape, 1)
    return (lane & (ROPE - 1)) < (ROPE // 2)


def _rope(a, cos, sin):
    first = _lane_first_half(a.shape)
    up = pltpu.roll(a, 96, 1)
    dn = pltpu.roll(a, 32, 1)
    return a * cos + jnp.where(first, -up, dn) * sin


def _rope_bwd(d, cos, sin):
    first = _lane_first_half(d.shape)
    up = pltpu.roll(d, 96, 1)
    dn = pltpu.roll(d, 32, 1)
    return d * cos + jnp.where(first, up, -dn) * sin


RELATIONS = tuple((dx, dy, dc) for dx in (0, 1) for dy in (0, 1) for dc in (0, 1) if (dx, dy, dc) != (0, 0, 0))
CHIP_RELATIONS = ((1, 0), (0, 1), (1, 1))


def _flip(v, d):
    return 1 - v if d else v


def _place():
    return lax.axis_index("x"), lax.axis_index("y"), lax.axis_index("c")


def _remote(src, dst, send_sem, recv_sem, target):
    return pltpu.make_async_remote_copy(src_ref=src, dst_ref=dst, send_sem=send_sem, recv_sem=recv_sem,
                                        device_id=target, device_id_type=MESH)


def _mod_exchange(c_row, w_ada, b_ada):
    cols = w_ada.shape[1]

    def body(c_ref, w_ref, b_ref, mod_ref, call_ref, part_ref, send1, recv1, loc1, send2, recv2, loc2):
        x, y, c = _place()
        me = 4 * x + 2 * y + c
        own = pltpu.make_async_copy(c_ref, call_ref.at[pl.ds(me, 1)], loc1)
        own.start()
        sends = []
        for k, (dx, dy, dc) in enumerate(RELATIONS):
            cp = _remote(c_ref, call_ref.at[pl.ds(me, 1)], send1.at[k], recv1.at[k],
                         (_flip(x, dx), _flip(y, dy), _flip(c, dc)))
            cp.start()
            sends.append(cp)
        for k, (dx, dy, dc) in enumerate(RELATIONS):
            src = 4 * _flip(x, dx) + 2 * _flip(y, dy) + _flip(c, dc)
            _remote(c_ref, call_ref.at[pl.ds(src, 1)], send1.at[k], recv1.at[k], (x, y, c)).wait_recv()
        own.wait()
        for cp in sends:
            cp.wait_send()
        call = call_ref[...]
        act = call * jax.nn.sigmoid(call)
        part_ref[...] = _dot(act, w_ref[...]) + b_ref[...]
        chip = 2 * x + y
        mine = pltpu.make_async_copy(part_ref.at[pl.ds(me, 1)], mod_ref.at[pl.ds(chip, 1)], loc2)
        mine.start()
        sends = []
        for k, (dx, dy) in enumerate(CHIP_RELATIONS):
            tx, ty = _flip(x, dx), _flip(y, dy)
            tb = 4 * tx + 2 * ty + c
            cp = _remote(part_ref.at[pl.ds(tb, 1)], mod_ref.at[pl.ds(chip, 1)], send2.at[k], recv2.at[k], (tx, ty, c))
            cp.start()
            sends.append(cp)
        for k, (dx, dy) in enumerate(CHIP_RELATIONS):
            src_chip = 2 * _flip(x, dx) + _flip(y, dy)
            _remote(part_ref.at[pl.ds(me, 1)], mod_ref.at[pl.ds(src_chip, 1)], send2.at[k], recv2.at[k],
                    (x, y, c)).wait_recv()
        mine.wait()
        for cp in sends:
            cp.wait_send()

    return pl.pallas_call(
        body, name="mod_exchange",
        out_shape=[jax.ShapeDtypeStruct((N_CHIPS, cols), F32), jax.ShapeDtypeStruct((8, D_MODEL), F32)],
        in_specs=[VMEM_SPEC, VMEM_SPEC, VMEM_SPEC], out_specs=[VMEM_SPEC, VMEM_SPEC],
        scratch_shapes=[pltpu.VMEM((8, cols), F32),
                        pltpu.SemaphoreType.DMA((7,)), pltpu.SemaphoreType.DMA((7,)), pltpu.SemaphoreType.DMA,
                        pltpu.SemaphoreType.DMA((3,)), pltpu.SemaphoreType.DMA((3,)), pltpu.SemaphoreType.DMA],
        compiler_params=_params(),
    )(c_row, w_ada, b_ada)


def _weight_gather(shards):
    n = len(shards)

    def body(*refs):
        ins, outs = refs[:n], refs[n:2 * n]
        send_sems, recv_sems, loc_sems = refs[2 * n:]
        x, y, c = _place()
        chip = 2 * x + y
        local = []
        for w in range(n):
            cp = pltpu.make_async_copy(ins[w], outs[w].at[chip], loc_sems.at[w])
            cp.start()
            local.append(cp)
        sends = []
        for w in range(n):
            hr = ins[w].shape[0] // 2
            half = pl.ds(c * hr, hr)
            for k, (dx, dy) in enumerate(CHIP_RELATIONS):
                cp = _remote(ins[w].at[half], outs[w].at[chip, half], send_sems.at[w, k], recv_sems.at[w, k],
                             (_flip(x, dx), _flip(y, dy), c))
                cp.start()
                sends.append(cp)
        for w in range(n):
            hr = ins[w].shape[0] // 2
            half = pl.ds(c * hr, hr)
            for k, (dx, dy) in enumerate(CHIP_RELATIONS):
                src_chip = 2 * _flip(x, dx) + _flip(y, dy)
                got = outs[w].at[src_chip, half]
                _remote(got, got, send_sems.at[w, k], recv_sems.at[w, k], (x, y, c)).wait_recv()
                cp = _remote(got, got, send_sems.at[w, 3 + k], recv_sems.at[w, 3 + k], (x, y, 1 - c))
                cp.start()
                sends.append(cp)
        for w in range(n):
            hr = ins[w].shape[0] // 2
            other = pl.ds((1 - c) * hr, hr)
            for k, (dx, dy) in enumerate(CHIP_RELATIONS):
                src_chip = 2 * _flip(x, dx) + _flip(y, dy)
                got = outs[w].at[src_chip, other]
                _remote(got, got, send_sems.at[w, 3 + k], recv_sems.at[w, 3 + k], (x, y, c)).wait_recv()
        for cp in sends:
            cp.wait_send()
        for cp in local:
            cp.wait()

    return pl.pallas_call(
        body, name="weight_gather",
        out_shape=[jax.ShapeDtypeStruct((N_CHIPS,) + s.shape, s.dtype) for s in shards],
        in_specs=[VMEM_SPEC] * n, out_specs=[ANY] * n,
        scratch_shapes=[pltpu.SemaphoreType.DMA((n, 6)), pltpu.SemaphoreType.DMA((n, 6)),
                        pltpu.SemaphoreType.DMA((n,))],
        compiler_params=_params(),
    )(*shards)


HBM_SPEC = pl.BlockSpec(memory_space=pltpu.HBM)
SEM_SPEC = pl.BlockSpec(memory_space=pltpu.SEMAPHORE)
DATAFLOW = pltpu.SideEffectType.DATAFLOW_SIDE_EFFECTING


def _in_hbm(a):
    return pltpu.with_memory_space_constraint(a, pltpu.HBM)


def _hbm_like(arrays):
    return [pltpu.HBM(a.shape, a.dtype) for a in arrays]


def _split_start(name, srcs, lands, n_remote, plan):
    lands = [lax.empty(a.shape, a.dtype) if isinstance(a, jax.ShapeDtypeStruct) else a for a in lands]
    n, m = len(srcs), len(lands)

    def body(*refs):
        src_refs, land_refs = refs[:n], refs[n:n + m]
        send_sems, recv_sems, token = refs[n + m], refs[n + m + 1], refs[n + 2 * m + 2]
        remote = plan(_place(), src_refs, land_refs)
        assert len(remote) == n_remote
        for i, (s, d, target) in enumerate(remote):
            _remote(s, d, send_sems.at[i], recv_sems.at[i], target).start()
        token[...] = jnp.zeros_like(token)

    res = pl.pallas_call(
        body, name=name,
        out_shape=(pltpu.SemaphoreType.DMA((n_remote,)), pltpu.SemaphoreType.DMA((n_remote,)),
                   *_hbm_like(lands), jax.ShapeDtypeStruct((8, 128), F32)),
        in_specs=[HBM_SPEC] * (n + m),
        out_specs=(SEM_SPEC, SEM_SPEC, *([HBM_SPEC] * m), VMEM_SPEC),
        input_output_aliases={n + i: 2 + i for i in range(m)},
        compiler_params=pltpu.CompilerParams(has_side_effects=DATAFLOW),
    )(*[_in_hbm(a) for a in srcs], *[_in_hbm(a) for a in lands])
    return res[0], res[1], list(res[2:2 + m]), res[2 + m]


def _split_wait(name, send_sems, recv_sems, srcs, lands, after, plan):
    n, m = len(srcs), len(lands)

    def body(*refs):
        src_refs, land_refs = refs[:n], refs[n:n + m]
        send_sems, recv_sems = refs[n + m], refs[n + m + 1]
        place = _place()
        for i, (s, d) in enumerate(plan(place, src_refs, land_refs)):
            cp = _remote(s, d, send_sems.at[i], recv_sems.at[i], place)
            cp.wait_send()
            cp.wait_recv()

    res = pl.pallas_call(
        body, name=name,
        out_shape=tuple(_hbm_like(lands)),
        in_specs=[HBM_SPEC] * (n + m) + [SEM_SPEC, SEM_SPEC, ANY],
        out_specs=tuple([HBM_SPEC] * m),
        input_output_aliases={n + i: i for i in range(m)},
        compiler_params=pltpu.CompilerParams(has_side_effects=DATAFLOW),
    )(*srcs, *lands, send_sems, recv_sems, after)
    return list(res)


def _split_relay(name, send_sems, recv_sems, srcs, lands, after, n_remote, plan_wait, plan_send):
    n, m = len(srcs), len(lands)

    def body(*refs):
        src_refs, land_refs = refs[:n], refs[n:n + m]
        old_send, old_recv = refs[n + m], refs[n + m + 1]
        new_send, new_recv = refs[n + m + 3], refs[n + m + 4]
        token = refs[n + m + 5 + m]
        place = _place()
        for i, (s, d) in enumerate(plan_wait(place, src_refs, land_refs)):
            cp = _remote(s, d, old_send.at[i], old_recv.at[i], place)
            cp.wait_send()
            cp.wait_recv()
        for i, (s, d, target) in enumerate(plan_send(place, land_refs)):
            _remote(s, d, new_send.at[i], new_recv.at[i], target).start()
        token[...] = jnp.zeros_like(token)

    res = pl.pallas_call(
        body, name=name,
        out_shape=(pltpu.SemaphoreType.DMA((n_remote,)), pltpu.SemaphoreType.DMA((n_remote,)),
                   *_hbm_like(lands), jax.ShapeDtypeStruct((8, 128), F32)),
        in_specs=[HBM_SPEC] * (n + m) + [SEM_SPEC, SEM_SPEC, ANY],
        out_specs=(SEM_SPEC, SEM_SPEC, *([HBM_SPEC] * m), VMEM_SPEC),
        input_output_aliases={n + i: 2 + i for i in range(m)},
        compiler_params=pltpu.CompilerParams(has_side_effects=DATAFLOW),
    )(*srcs, *lands, send_sems, recv_sems, after)
    return res[0], res[1], list(res[2:2 + m]), res[2 + m]


def _half(ref, core, axis=0):
    hr = ref.shape[axis] // 2
    return pl.ds(core * hr, hr)


def _plan_gather_start(place, src, land):
    x, y, c = place
    chip = 2 * x + y
    return [(s.at[_half(s, c)], l.at[chip, _half(s, c)], (_flip(x, dx), _flip(y, dy), c))
            for s, l in zip(src, land) for dx, dy in CHIP_RELATIONS]


def _plan_gather_landed(place, src, land):
    x, y, c = place
    return [(s.at[_half(s, c)], l.at[2 * _flip(x, dx) + _flip(y, dy), _half(s, c)])
            for s, l in zip(src, land) for dx, dy in CHIP_RELATIONS]


def _plan_gather_relay(place, land):
    x, y, c = place
    out = []
    for l in land:
        for dx, dy in CHIP_RELATIONS:
            got = l.at[2 * _flip(x, dx) + _flip(y, dy), _half(l, c, 1)]
            out.append((got, got, (x, y, 1 - c)))
    return out


def _plan_gather_wait(place, src, land):
    x, y, c = place
    out = []
    for l in land:
        for dx, dy in CHIP_RELATIONS:
            got = l.at[2 * _flip(x, dx) + _flip(y, dy), _half(l, 1 - c, 1)]
            out.append((got, got))
    return out


def _plan_swap_start(place, src, land):
    x, y, c = place
    return [(s.at[:, _half(s, 1 - c, 1), :], l, (x, y, 1 - c)) for s, l in zip(src, land)]


def _plan_swap_wait(place, src, land):
    return [(s.at[:, _half(s, 0, 1), :], l) for s, l in zip(src, land)]


def _plan_exchange_start(place, src, land):
    x, y, c = place
    remote = []
    for s, l in zip(src, land):
        for k, (dx, dy) in enumerate(CHIP_RELATIONS):
            tx, ty = _flip(x, dx), _flip(y, dy)
            remote.append((s.at[2 * tx + ty], l.at[k], (tx, ty, c)))
    return remote


def _plan_exchange_wait(place, src, land):
    return [(s.at[0], l.at[k]) for s, l in zip(src, land) for k in range(3)]


def _plan_finish_start(place, src, land):
    x, y, c = place
    return [(l.at[c], l.at[c], (x, y, 1 - c)) for l in land]


def _plan_finish_wait(place, src, land):
    x, y, c = place
    return [(l.at[c], l.at[1 - c]) for l in land]


def _grad_swap_halves(grads, dmod):
    n = len(grads)

    def body(*refs):
        ins, dmod_ref = refs[:n], refs[n]
        outs, dall_ref = refs[n + 1:2 * n + 1], refs[2 * n + 1]
        send_sems, recv_sems, dsend, drecv, dloc = refs[2 * n + 2:]
        x, y, c = _place()
        me = 4 * x + 2 * y + c
        sends = []
        for w in range(n):
            hr = ins[w].shape[1] // 2
            cp = _remote(ins[w].at[:, pl.ds((1 - c) * hr, hr), :], outs[w], send_sems.at[w], recv_sems.at[w],
                         (x, y, 1 - c))
            cp.start()
            sends.append(cp)
        own = pltpu.make_async_copy(dmod_ref, dall_ref.at[me], dloc)
        own.start()
        for k, (dx, dy, dc) in enumerate(RELATIONS):
            cp = _remote(dmod_ref, dall_ref.at[me], dsend.at[k], drecv.at[k],
                         (_flip(x, dx), _flip(y, dy), _flip(c, dc)))
            cp.start()
            sends.append(cp)
        for k, (dx, dy, dc) in enumerate(RELATIONS):
            src = 4 * _flip(x, dx) + 2 * _flip(y, dy) + _flip(c, dc)
            _remote(dmod_ref, dall_ref.at[src], dsend.at[k], drecv.at[k], (x, y, c)).wait_recv()
        for w in range(n):
            _remote(outs[w], outs[w], send_sems.at[w], recv_sems.at[w], (x, y, c)).wait_recv()
        own.wait()
        for cp in sends:
            cp.wait_send()

    out_shape = [jax.ShapeDtypeStruct((N_CHIPS, g.shape[1] // 2, g.shape[2]), F32) for g in grads]
    out_shape.append(jax.ShapeDtypeStruct((8,) + dmod.shape, F32))
    res = pl.pallas_call(
        body, name="grad_swap_halves",
        out_shape=out_shape, in_specs=[ANY] * n + [VMEM_SPEC], out_specs=[ANY] * (n + 1),
        scratch_shapes=[pltpu.SemaphoreType.DMA((n,)), pltpu.SemaphoreType.DMA((n,)),
                        pltpu.SemaphoreType.DMA((7,)), pltpu.SemaphoreType.DMA((7,)), pltpu.SemaphoreType.DMA],
        compiler_params=_params(),
    )(*grads, dmod)
    return res[:n], res[n]


def _grad_chip_exchange(sums):
    n = len(sums)

    def body(*refs):
        ins, outs = refs[:n], refs[n:2 * n]
        send_sems, recv_sems = refs[2 * n:]
        x, y, c = _place()
        sends = []
        for w in range(n):
            for k, (dx, dy) in enumerate(CHIP_RELATIONS):
                tx, ty = _flip(x, dx), _flip(y, dy)
                cp = _remote(ins[w].at[2 * tx + ty], outs[w].at[k], send_sems.at[w, k], recv_sems.at[w, k], (tx, ty, c))
                cp.start()
                sends.append(cp)
        for w in range(n):
            for k in range(3):
                _remote(outs[w].at[k], outs[w].at[k], send_sems.at[w, k], recv_sems.at[w, k], (x, y, c)).wait_recv()
        for cp in sends:
            cp.wait_send()

    return pl.pallas_call(
        body, name="grad_chip_exchange",
        out_shape=[jax.ShapeDtypeStruct((3,) + s.shape[1:], F32) for s in sums],
        in_specs=[ANY] * n, out_specs=[ANY] * n,
        scratch_shapes=[pltpu.SemaphoreType.DMA((n, 3)), pltpu.SemaphoreType.DMA((n, 3))],
        compiler_params=_params(),
    )(*sums)


def _grad_finish(halves, small_half):
    n = len(halves)

    def body(*refs):
        ins, sm_ref = refs[:n], refs[n]
        outs, sall_ref = refs[n + 1:2 * n + 1], refs[2 * n + 1]
        send_sems, recv_sems, loc_sems, ssend, srecv, sloc = refs[2 * n + 2:]
        x, y, c = _place()
        chip = 2 * x + y
        local, sends = [], []
        for w in range(n):
            cp = pltpu.make_async_copy(ins[w], outs[w].at[c], loc_sems.at[w])
            cp.start()
            local.append(cp)
            cp = _remote(ins[w], outs[w].at[c], send_sems.at[w], recv_sems.at[w], (x, y, 1 - c))
            cp.start()
            sends.append(cp)
        cp = pltpu.make_async_copy(sm_ref, sall_ref.at[chip, c], sloc)
        cp.start()
        local.append(cp)
        for k, (dx, dy, dc) in enumerate(RELATIONS):
            cp = _remote(sm_ref, sall_ref.at[chip, c], ssend.at[k], srecv.at[k],
                         (_flip(x, dx), _flip(y, dy), _flip(c, dc)))
            cp.start()
            sends.append(cp)
        for k, (dx, dy, dc) in enumerate(RELATIONS):
            got = sall_ref.at[2 * _flip(x, dx) + _flip(y, dy), _flip(c, dc)]
            _remote(got, got, ssend.at[k], srecv.at[k], (x, y, c)).wait_recv()
        for w in range(n):
            got = outs[w].at[1 - c]
            _remote(got, got, send_sems.at[w], recv_sems.at[w], (x, y, c)).wait_recv()
        for cp in sends:
            cp.wait_send()
        for cp in local:
            cp.wait()

    out_shape = [jax.ShapeDtypeStruct((2,) + h.shape, F32) for h in halves]
    out_shape.append(jax.ShapeDtypeStruct((N_CHIPS, 2) + small_half.shape, F32))
    res = pl.pallas_call(
        body, name="grad_finish",
        out_shape=out_shape, in_specs=[VMEM_SPEC] * (n + 1), out_specs=[ANY] * (n + 1),
        scratch_shapes=[pltpu.SemaphoreType.DMA((n,)), pltpu.SemaphoreType.DMA((n,)), pltpu.SemaphoreType.DMA((n,)),
                        pltpu.SemaphoreType.DMA((7,)), pltpu.SemaphoreType.DMA((7,)), pltpu.SemaphoreType.DMA],
        compiler_params=_params(),
    )(*halves, small_half)
    return res[:n], res[n]


def _add_my_half(core, full, got, name):
    _, hr, cols = got.shape

    def body(core_ref, a_ref, b_ref, o_ref):
        o_ref[...] = a_ref[...] + b_ref[...]

    return pl.pallas_call(
        body, name=name,
        out_shape=jax.ShapeDtypeStruct(got.shape, F32),
        grid_spec=pltpu.PrefetchScalarGridSpec(
            num_scalar_prefetch=1, grid=(N_CHIPS,),
            in_specs=[pl.BlockSpec((None, hr, cols), lambda s, core_ref: (s, core_ref[0], 0)),
                      pl.BlockSpec((None, hr, cols), lambda s, core_ref: (s, 0, 0))],
            out_specs=pl.BlockSpec((None, hr, cols), lambda s, core_ref: (s, 0, 0))),
        compiler_params=_params(("arbitrary",)),
    )(core, full, got)


def _add_chips(chip, mine, got, name):
    _, hr, cols = mine.shape

    def body(chip_ref, a_ref, b_ref, o_ref):
        o_ref[...] = ((a_ref[...] + b_ref[0]) + b_ref[1]) + b_ref[2]

    return pl.pallas_call(
        body, name=name,
        out_shape=jax.ShapeDtypeStruct((hr, cols), F32),
        grid_spec=pltpu.PrefetchScalarGridSpec(
            num_scalar_prefetch=1, grid=(1,),
            in_specs=[pl.BlockSpec((None, hr, cols), lambda s, chip_ref: (chip_ref[0], 0, 0)),
                      pl.BlockSpec((3, hr, cols), lambda s, chip_ref: (0, 0, 0))],
            out_specs=pl.BlockSpec((hr, cols), lambda s, chip_ref: (0, 0))),
        compiler_params=_params(("arbitrary",)),
    )(chip, mine, got)


def _add_chips_into_pair(chip_core, mine, got, name):
    _, hr, cols = mine.shape

    def body(cc_ref, a_ref, b_ref, o_ref):
        o_ref[...] = ((a_ref[...] + b_ref[0]) + b_ref[1]) + b_ref[2]

    return pl.pallas_call(
        body, name=name,
        out_shape=jax.ShapeDtypeStruct((2, hr, cols), F32),
        grid_spec=pltpu.PrefetchScalarGridSpec(
            num_scalar_prefetch=1, grid=(1,),
            in_specs=[pl.BlockSpec((None, hr, cols), lambda s, cc_ref: (cc_ref[0], 0, 0)),
                      pl.BlockSpec((3, hr, cols), lambda s, cc_ref: (0, 0, 0))],
            out_specs=pl.BlockSpec((None, hr, cols), lambda s, cc_ref: (cc_ref[1], 0, 0))),
        compiler_params=_params(("arbitrary",)),
    )(chip_core, mine, got)


def _place_shards(chip, shards):
    n = len(shards)

    def body(chip_ref, *refs):
        for w in range(n):
            refs[n + w][...] = refs[w][...]

    return pl.pallas_call(
        body, name="place_shards",
        out_shape=[jax.ShapeDtypeStruct((N_CHIPS,) + s.shape, s.dtype) for s in shards],
        grid_spec=pltpu.PrefetchScalarGridSpec(
            num_scalar_prefetch=1, grid=(1,),
            in_specs=[pl.BlockSpec(s.shape, lambda i, chip_ref: (0, 0)) for s in shards],
            out_specs=[pl.BlockSpec((None,) + s.shape, lambda i, chip_ref: (chip_ref[0], 0, 0)) for s in shards]),
        compiler_params=_params(("arbitrary",)),
    )(chip, *shards)


def _rope_tables(pos_col, freqs):
    S = pos_col.shape[0]
    T = _row_tile(S, 1024)

    def body(p_ref, f_ref, cos_ref, sin_ref):
        ang = p_ref[...].astype(F32) * f_ref[...]
        cos_ref[...] = jnp.cos(ang)
        sin_ref[...] = jnp.sin(ang)

    return pl.pallas_call(
        body, name="rope_tables", grid=(S // T,),
        out_shape=[jax.ShapeDtypeStruct((S, 128), F32)] * 2,
        in_specs=[pl.BlockSpec((T, 1), lambda i: (i, 0)), pl.BlockSpec((1, 128), lambda i: (0, 0))],
        out_specs=[pl.BlockSpec((T, 128), lambda i: (i, 0))] * 2,
        compiler_params=_params(("parallel",)),
    )(pos_col, freqs)


def _full(shape):
    zeros = (0,) * len(shape)
    return pl.BlockSpec(shape, lambda *_: zeros)


def _pre_attention(x, mod6, g_mix, g_q, g_kv, w_in, w_uq, w_uk_t, cos, sin, T, TQ):
    S = x.shape[0]

    def body(x_ref, mod_ref, gm_ref, gq_ref, gkv_ref, win_ref, wuq_ref, wuk_ref, cos_ref, sin_ref,
             proj_ref, q_ref, qc_ref, kc_ref, kct_ref):
        xh, _ = _rms(x_ref[...])
        h1 = (xh * gm_ref[...]) * (1.0 + mod_ref[1:2, :]) + mod_ref[0:1, :]
        proj = _dot(h1, win_ref[...])
        proj_ref[...] = proj
        cqh, _ = _rms(proj[:, :Q_LORA])
        c_q = cqh * gq_ref[...]
        ckvh, _ = _rms(proj[:, O_CKV:O_KR])
        c_kv = ckvh * gkv_ref[...]
        q = _dot(c_q, wuq_ref[...])
        q_ref[...] = q
        cos_t, sin_t = cos_ref[...], sin_ref[...]
        ropes = (_rope(q[:, O_QA:O_QB], cos_t, sin_t), _rope(q[:, O_QB:Q_W], cos_t, sin_t))
        low = lax.broadcasted_iota(jnp.int32, (T, 128), 1) < ROPE
        for h in range(HEADS):
            q_lat = _dot_nt(q[:, h * NOPE:(h + 1) * NOPE], wuk_ref[h])
            keep = low if h % 2 == 0 else jnp.logical_not(low)
            qc_ref[h, :, 0:KV_LORA] = q_lat.astype(BF16)
            qc_ref[h, :, KV_LORA:QK_PAD] = jnp.where(keep, ropes[h // 2], 0.0).astype(BF16)
        k_rope = _rope(proj[:, O_KR:O_U], cos_t, sin_t)
        kc_ref[:, 0:KV_LORA] = c_kv.astype(BF16)
        kc_ref[:, KV_LORA:QK_PAD] = k_rope.astype(BF16)
        lat_t, rope_t = jnp.transpose(c_kv), jnp.transpose(k_rope)
        for s in range(T // TQ):
            kct_ref[s, 0:KV_LORA, :] = lat_t[:, s * TQ:(s + 1) * TQ].astype(BF16)
            kct_ref[s, KV_LORA:QK_PAD, :] = rope_t[:, s * TQ:(s + 1) * TQ].astype(BF16)

    row = lambda w: pl.BlockSpec((T, w), lambda i: (i, 0))
    return pl.pallas_call(
        body, name="pre_attention", grid=(S // T,),
        out_shape=[jax.ShapeDtypeStruct((S, PROJ_W), F32), jax.ShapeDtypeStruct((S, Q_W), F32),
                   jax.ShapeDtypeStruct((HEADS, S, QK_PAD), BF16), jax.ShapeDtypeStruct((S, QK_PAD), BF16),
                   jax.ShapeDtypeStruct((S // TQ, QK_PAD, TQ), BF16)],
        in_specs=[row(D_MODEL), _full((N_MOD, D_MODEL)), _full((1, D_MODEL)), _full((1, Q_LORA)), _full((1, KV_LORA)),
                  _full((D_MODEL, PROJ_W)), _full((Q_LORA, Q_W)), _full((HEADS, KV_LORA, NOPE)), row(128), row(128)],
        out_specs=[row(PROJ_W), row(Q_W), pl.BlockSpec((HEADS, T, QK_PAD), lambda i: (0, i, 0)), row(QK_PAD),
                   pl.BlockSpec((T // TQ, QK_PAD, TQ), lambda i: (i, 0, 0))],
        compiler_params=_params(("parallel",)),
    )(x, mod6, g_mix, g_q, g_kv, w_in, w_uq, w_uk_t, cos, sin)


def _diag_mask(TQ, transposed):
    R = HEADS * TQ
    if transposed:
        key = lax.broadcasted_iota(jnp.int32, (TQ, R), 0) >> CHUNK_SHIFT
        qry = (lax.broadcasted_iota(jnp.int32, (TQ, R), 1) & (TQ - 1)) >> CHUNK_SHIFT
    else:
        qry = (lax.broadcasted_iota(jnp.int32, (R, TQ), 0) & (TQ - 1)) >> CHUNK_SHIFT
        key = lax.broadcasted_iota(jnp.int32, (R, TQ), 1) >> CHUNK_SHIFT
    return key <= qry


def _col_to_row(col):
    return jnp.transpose(jnp.broadcast_to(col, (col.shape[0], 128)))[0:1, :]


def _attention_fwd(qc, kc, kct, w_uv_t, TQ):
    S = kc.shape[0]
    R = HEADS * TQ
    nq = S // TQ

    def body(q_ref, k_ref, kt_ref, wuv_ref, o_ref, y_ref, lser_ref, m_s, l_s, acc_s):
        i = pl.program_id(0)
        q = q_ref[...].reshape(R, QK_PAD)
        m_s[...] = jnp.full((1, R), -jnp.inf, F32)
        l_s[...] = jnp.zeros((1, R), F32)
        acc_s[...] = jnp.zeros((KV_LORA, R), F32)

        def step(j, masked):
            k = k_ref[pl.ds(pl.multiple_of(j * TQ, TQ), TQ), :]
            st = _dot_nt(k, q) * SM_SCALE
            if masked:
                st = jnp.where(_diag_mask(TQ, True), st, -jnp.inf)
            m_old = m_s[...]
            m_new = jnp.maximum(m_old, jnp.max(st, axis=0, keepdims=True))
            pt = jnp.exp(st - m_new)
            alpha = jnp.exp(m_old - m_new)
            l_s[...] = alpha * l_s[...] + jnp.sum(pt, axis=0, keepdims=True)
            acc_s[...] = alpha * acc_s[...] + _dot(kt_ref[j, 0:KV_LORA, :], pt)
            m_s[...] = m_new

        def loop(j, carry):
            step(j, False)
            return carry

        lax.fori_loop(0, i, loop, 0)
        step(i, True)
        l = l_s[...]
        lser_ref[0] = m_s[...] + jnp.log(l)
        o = jnp.transpose(acc_s[...] / l).astype(BF16)
        for h in range(HEADS):
            oh = o[h * TQ:(h + 1) * TQ, :]
            o_ref[h] = oh
            y_ref[:, h * 128:(h + 1) * 128] = _dot(oh, wuv_ref[h]).astype(BF16)

    return pl.pallas_call(
        body, name="attention_fwd", grid=(nq,),
        out_shape=[jax.ShapeDtypeStruct((HEADS, S, KV_LORA), BF16), jax.ShapeDtypeStruct((S, HEADS * 128), BF16),
                   jax.ShapeDtypeStruct((nq, 1, R), F32)],
        in_specs=[pl.BlockSpec((HEADS, TQ, QK_PAD), lambda i: (0, i, 0)), _full((S, QK_PAD)),
                  _full((nq, QK_PAD, TQ)), _full((HEADS, KV_LORA, 128))],
        out_specs=[pl.BlockSpec((HEADS, TQ, KV_LORA), lambda i: (0, i, 0)), pl.BlockSpec((TQ, HEADS * 128), lambda i: (i, 0)),
                   pl.BlockSpec((1, 1, R), lambda i: (i, 0, 0))],
        scratch_shapes=[pltpu.VMEM((1, R), F32), pltpu.VMEM((1, R), F32), pltpu.VMEM((KV_LORA, R), F32)],
        compiler_params=_params(("parallel",)),
    )(qc, kc, kct, w_uv_t)


def _pool_forward(proj):
    S = proj.shape[0]
    RB = _row_tile(S, 256)

    def body(proj_ref, out_ref, pad_ref, sem):
        cp = pltpu.make_async_copy(proj_ref.at[:, pl.ds(O_U, POOL_W)], pad_ref.at[pl.ds(POOL_PAD, S)], sem)
        cp.start()
        pad_ref[0:POOL_PAD, :] = jnp.zeros((POOL_PAD, POOL_W), F32)
        cp.wait()
        for g, win in enumerate(POOL_WINDOWS):
            cols = slice(g * POOL_GROUP, (g + 1) * POOL_GROUP)
            for r0 in range(0, S, RB):
                u = pad_ref[POOL_PAD + r0:POOL_PAD + r0 + RB, cols]
                acc = u
                for k in range(1, win):
                    acc = acc + pad_ref[POOL_PAD + r0 - k:POOL_PAD + r0 - k + RB, cols]
                if r0 == 0:
                    t1 = (lax.broadcasted_iota(jnp.int32, (RB, POOL_GROUP), 0) + 1).astype(F32)
                    mean = acc / jnp.minimum(t1, float(win))
                else:
                    mean = acc * (1.0 / win)
                out_ref[r0:r0 + RB, cols] = (mean - u).astype(BF16)

    return pl.pallas_call(
        body, name="pool_forward",
        out_shape=jax.ShapeDtypeStruct((S, POOL_W), BF16),
        in_specs=[ANY], out_specs=VMEM_SPEC,
        scratch_shapes=[pltpu.VMEM((S + POOL_PAD, POOL_W), F32), pltpu.SemaphoreType.DMA],
        compiler_params=_params(),
    )(proj)


def _pool_backward(dpooled):
    S = dpooled.shape[0]
    RB = _row_tile(S, 256)

    def body(dp_ref, out_ref, pad_ref, sem):
        cp = pltpu.make_async_copy(dp_ref, pad_ref.at[pl.ds(0, S)], sem)
        cp.start()
        pad_ref[S:S + POOL_PAD, :] = jnp.zeros((POOL_PAD, POOL_W), F32)
        cp.wait()
        for g, win in enumerate(POOL_WINDOWS):
            cols = slice(g * POOL_GROUP, (g + 1) * POOL_GROUP)
            head = pad_ref[0:POOL_PAD, cols]
            t1 = (lax.broadcasted_iota(jnp.int32, (POOL_PAD, POOL_GROUP), 0) + 1).astype(F32)
            pad_ref[0:POOL_PAD, cols] = head * (float(win) / jnp.minimum(t1, float(win)))
            for r0 in range(0, S, RB):
                acc = pad_ref[r0:r0 + RB, cols]
                for k in range(1, win):
                    acc = acc + pad_ref[r0 + k:r0 + k + RB, cols]
                own = pad_ref[r0:r0 + RB, cols]
                if r0 == 0:
                    own = jnp.concatenate([head, own[POOL_PAD:]], axis=0)
                out_ref[r0:r0 + RB, cols] = acc * (1.0 / win) - own

    return pl.pallas_call(
        body, name="pool_backward",
        out_shape=jax.ShapeDtypeStruct((S, POOL_W), F32),
        in_specs=[ANY], out_specs=VMEM_SPEC,
        scratch_shapes=[pltpu.VMEM((S + POOL_PAD, POOL_W), F32), pltpu.SemaphoreType.DMA],
        compiler_params=_params(),
    )(dpooled)


def _mix_out(y_mla, pooled, w_pool, pool_scale, w_o, x, mod6, T):
    S = x.shape[0]

    def body(ym_ref, pl_ref, wp_ref, ps_ref, wo_ref, x_ref, mod_ref, x1_ref, mix_ref, mi_ref):
        mi_ref[:, 0:512] = ym_ref[...]
        for g in range(len(POOL_WINDOWS)):
            cols = slice(g * POOL_GROUP, (g + 1) * POOL_GROUP)
            z = _dot(pl_ref[:, cols], wp_ref[g])
            mi_ref[:, 512 + g * POOL_GROUP:512 + (g + 1) * POOL_GROUP] = (z * ps_ref[:, cols]).astype(BF16)
        mix = _dot(mi_ref[...], wo_ref[...])
        mix_ref[...] = mix
        x1_ref[...] = x_ref[...] + mod_ref[2:3, :] * mix

    row = lambda w: pl.BlockSpec((T, w), lambda i: (i, 0))
    return pl.pallas_call(
        body, name="mix_out", grid=(S // T,),
        out_shape=[jax.ShapeDtypeStruct((S, D_MODEL), F32), jax.ShapeDtypeStruct((S, D_MODEL), F32),
                   jax.ShapeDtypeStruct((S, 1024), BF16)],
        in_specs=[row(512), row(POOL_W), _full((4, POOL_GROUP, POOL_GROUP)), _full((1, POOL_W)),
                  _full((1024, D_MODEL)), row(D_MODEL), _full((N_MOD, D_MODEL))],
        out_specs=[row(D_MODEL), row(D_MODEL), row(1024)],
        compiler_params=_params(("parallel",)),
    )(y_mla, pooled, w_pool, pool_scale, w_o, x, mod6)


def _ffn_forward(x1, mod6, g_ffn, g_final, target, w_gate, w_up, w_down, T):
    S = x1.shape[0]

    def body(x1_ref, mod_ref, gf_ref, gl_ref, tgt_ref, wg_ref, wu_ref, wd_ref,
             gate_ref, up_ref, dx2_ref, st_ref, h2_s, acc_s):
        i, j = pl.program_id(0), pl.program_id(1)

        @pl.when(jnp.logical_and(i == 0, j == 0))
        def _():
            st_ref[...] = jnp.zeros_like(st_ref)

        @pl.when(j == 0)
        def _():
            xh, _ = _rms(x1_ref[...])
            h2_s[...] = ((xh * gf_ref[...]) * (1.0 + mod_ref[4:5, :]) + mod_ref[3:4, :]).astype(BF16)
            acc_s[...] = jnp.zeros_like(acc_s)

        h2 = h2_s[...]
        gate = _dot(h2, wg_ref[...])
        up = _dot(h2, wu_ref[...])
        gate_ref[...] = gate
        up_ref[...] = up
        act = gate * jax.nn.sigmoid(gate) * up
        acc_s[...] += _dot(act, wd_ref[...])

        @pl.when(j == N_CHIPS - 1)
        def _():
            ff = acc_s[...]
            x2 = x1_ref[...] + mod_ref[5:6, :] * ff
            xh, r3 = _rms(x2)
            err = xh * gl_ref[...] - tgt_ref[...]
            dy = err * (1.0 / D_MODEL)
            dx2 = _rms_bwd(dy * gl_ref[...], xh, r3)
            dx2_ref[...] = dx2
            st_ref[0:1, :] += jnp.sum(dy * xh, axis=0, keepdims=True)
            st_ref[1:2, :] += jnp.sum(dx2 * ff, axis=0, keepdims=True)
            st_ref[2:3, :] += 0.5 * jnp.sum(err * dy)

    row = pl.BlockSpec((T, D_MODEL), lambda i, j: (i, 0))
    chunk_out = pl.BlockSpec((None, T, FF_CHUNK), lambda i, j: (j, i, 0))
    return pl.pallas_call(
        body, name="ffn_forward", grid=(S // T, N_CHIPS),
        out_shape=[jax.ShapeDtypeStruct((N_CHIPS, S, FF_CHUNK), F32), jax.ShapeDtypeStruct((N_CHIPS, S, FF_CHUNK), F32),
                   jax.ShapeDtypeStruct((S, D_MODEL), F32), jax.ShapeDtypeStruct((8, D_MODEL), F32)],
        in_specs=[row, _full((N_MOD, D_MODEL)), _full((1, D_MODEL)), _full((1, D_MODEL)), row,
                  pl.BlockSpec((None, D_MODEL, FF_CHUNK), lambda i, j: (j, 0, 0)),
                  pl.BlockSpec((None, D_MODEL, FF_CHUNK), lambda i, j: (j, 0, 0)),
                  pl.BlockSpec((None, FF_CHUNK, D_MODEL), lambda i, j: (j, 0, 0))],
        out_specs=[chunk_out, chunk_out, row, _full((8, D_MODEL))],
        scratch_shapes=[pltpu.VMEM((T, D_MODEL), BF16), pltpu.VMEM((T, D_MODEL), F32)],
        compiler_params=_params(("arbitrary", "arbitrary")),
    )(x1, mod6, g_ffn, g_final, target, w_gate, w_up, w_down)


def _ffn_backward(dx2, x1, gate, up, mod6, g_ffn, w_gate, w_up, w_down, T):
    S = x1.shape[0]

    def body(dx2_ref, x1_ref, gate_ref, up_ref, mod_ref, gf_ref, wg_ref, wu_ref, wd_ref,
             dgate_ref, dup_ref, act_ref, dff_ref, h2_ref, dx1_ref, st_ref, acc_s):
        i, j = pl.program_id(0), pl.program_id(1)

        @pl.when(jnp.logical_and(i == 0, j == 0))
        def _():
            st_ref[...] = jnp.zeros_like(st_ref)

        @pl.when(j == 0)
        def _():
            dff_ref[...] = (dx2_ref[...] * mod_ref[5:6, :]).astype(BF16)
            xh, _ = _rms(x1_ref[...])
            h2_ref[...] = ((xh * gf_ref[...]) * (1.0 + mod_ref[4:5, :]) + mod_ref[3:4, :]).astype(BF16)
            acc_s[...] = jnp.zeros_like(acc_s)

        gate, up = gate_ref[...], up_ref[...]
        sg = jax.nn.sigmoid(gate)
        silu = gate * sg
        act_ref[...] = (silu * up).astype(BF16)
        dact = _dot_nt(dff_ref[...], wd_ref[...])
        dup = (dact * silu).astype(BF16)
        dgate = (dact * up * (sg * (1.0 + gate * (1.0 - sg)))).astype(BF16)
        dup_ref[...] = dup
        dgate_ref[...] = dgate
        acc_s[...] += _dot_nt(dgate, wg_ref[...]) + _dot_nt(dup, wu_ref[...])

        @pl.when(j == N_CHIPS - 1)
        def _():
            dh2 = acc_s[...]
            xh, r2 = _rms(x1_ref[...])
            n2 = xh * gf_ref[...]
            st_ref[0:1, :] += jnp.sum(dh2, axis=0, keepdims=True)
            st_ref[1:2, :] += jnp.sum(dh2 * n2, axis=0, keepdims=True)
            dn2 = dh2 * (1.0 + mod_ref[4:5, :])
            st_ref[2:3, :] += jnp.sum(dn2 * xh, axis=0, keepdims=True)
            dx1_ref[...] = _rms_bwd(dn2 * gf_ref[...], xh, r2) + dx2_ref[...]

    row = pl.BlockSpec((T, D_MODEL), lambda i, j: (i, 0))
    chunk = pl.BlockSpec((None, T, FF_CHUNK), lambda i, j: (j, i, 0))
    big = jax.ShapeDtypeStruct((N_CHIPS, S, FF_CHUNK), BF16)
    return pl.pallas_call(
        body, name="ffn_backward", grid=(S // T, N_CHIPS),
        out_shape=[big, big, big, jax.ShapeDtypeStruct((S, D_MODEL), BF16), jax.ShapeDtypeStruct((S, D_MODEL), BF16),
                   jax.ShapeDtypeStruct((S, D_MODEL), F32), jax.ShapeDtypeStruct((8, D_MODEL), F32)],
        in_specs=[row, row, chunk, chunk, _full((N_MOD, D_MODEL)), _full((1, D_MODEL)),
                  pl.BlockSpec((None, D_MODEL, FF_CHUNK), lambda i, j: (j, 0, 0)),
                  pl.BlockSpec((None, D_MODEL, FF_CHUNK), lambda i, j: (j, 0, 0)),
                  pl.BlockSpec((None, FF_CHUNK, D_MODEL), lambda i, j: (j, 0, 0))],
        out_specs=[chunk, chunk, chunk, row, row, row, _full((8, D_MODEL))],
        scratch_shapes=[pltpu.VMEM((T, D_MODEL), F32)],
        compiler_params=_params(("arbitrary", "arbitrary")),
    )(dx2, x1, gate, up, mod6, g_ffn, w_gate, w_up, w_down)


def _tn_matmul(a, b, a_spec, b_spec, groups, m, n, steps, name):
    def body(a_ref, b_ref, o_ref):
        @pl.when(pl.program_id(1) == 0)
        def _():
            o_ref[...] = jnp.zeros_like(o_ref)

        o_ref[...] += _dot_tn(a_ref[...], b_ref[...])

    return pl.pallas_call(
        body, name=name, grid=(groups, steps),
        out_shape=jax.ShapeDtypeStruct((groups, m, n), F32),
        in_specs=[a_spec, b_spec],
        out_specs=pl.BlockSpec((None, m, n), lambda g, i: (g, 0, 0)),
        compiler_params=_params(("parallel", "arbitrary")),
    )(a, b)


def _mix_backward(dx1, mix, mod6, w_o, pooled, w_pool, pool_scale, w_uv_t, o_lat, T, TQ):
    S = dx1.shape[0]

    def body(dx1_ref, mix_ref, mod_ref, wo_ref, pl_ref, wp_ref, ps_ref, wuv_ref, o_ref,
             dmix_ref, dz_ref, dp_ref, dym_ref, do_ref, dr_ref, st_ref):
        @pl.when(pl.program_id(0) == 0)
        def _():
            st_ref[...] = jnp.zeros_like(st_ref)

        dx1 = dx1_ref[...]
        st_ref[0:1, :] += jnp.sum(dx1 * mix_ref[...], axis=0, keepdims=True)
        dmix = (dx1 * mod_ref[2:3, :]).astype(BF16)
        dmix_ref[...] = dmix
        dmi = _dot_nt(dmix, wo_ref[...])
        dym = dmi[:, 0:512].astype(BF16)
        dym_ref[...] = dym
        for g in range(len(POOL_WINDOWS)):
            cols = slice(g * POOL_GROUP, (g + 1) * POOL_GROUP)
            dyp = dmi[:, 512 + g * POOL_GROUP:512 + (g + 1) * POOL_GROUP]
            z = _dot(pl_ref[:, cols], wp_ref[g])
            st_ref[1:2, cols] += jnp.sum(dyp * z, axis=0, keepdims=True)
            dz = (dyp * ps_ref[:, cols]).astype(BF16)
            dz_ref[:, cols] = dz
            dp_ref[:, cols] = _dot_nt(dz, wp_ref[g])
        for h in range(HEADS):
            do = _dot_nt(dym[:, h * 128:(h + 1) * 128], wuv_ref[h]).astype(BF16)
            do_ref[h] = do
            delta = _col_to_row(jnp.sum(do.astype(F32) * o_ref[h].astype(F32), axis=1, keepdims=True))
            for s in range(T // TQ):
                dr_ref[s, :, h * TQ:(h + 1) * TQ] = delta[:, s * TQ:(s + 1) * TQ]

    row = lambda w: pl.BlockSpec((T, w), lambda i: (i, 0))
    heads = pl.BlockSpec((HEADS, T, KV_LORA), lambda i: (0, i, 0))
    return pl.pallas_call(
        body, name="mix_backward", grid=(S // T,),
        out_shape=[jax.ShapeDtypeStruct((S, D_MODEL), BF16), jax.ShapeDtypeStruct((S, POOL_W), BF16),
                   jax.ShapeDtypeStruct((S, POOL_W), F32), jax.ShapeDtypeStruct((S, 512), BF16),
                   jax.ShapeDtypeStruct((HEADS, S, KV_LORA), BF16), jax.ShapeDtypeStruct((S // TQ, 1, HEADS * TQ), F32),
                   jax.ShapeDtypeStruct((8, D_MODEL), F32)],
        in_specs=[row(D_MODEL), row(D_MODEL), _full((N_MOD, D_MODEL)), _full((1024, D_MODEL)), row(POOL_W),
                  _full((4, POOL_GROUP, POOL_GROUP)), _full((1, POOL_W)), _full((HEADS, KV_LORA, 128)), heads],
        out_specs=[row(D_MODEL), row(POOL_W), row(POOL_W), row(512), heads,
                   pl.BlockSpec((T // TQ, 1, HEADS * TQ), lambda i: (i, 0, 0)), _full((8, D_MODEL))],
        compiler_params=_params(("arbitrary",)),
    )(dx1, mix, mod6, w_o, pooled, w_pool, pool_scale, w_uv_t, o_lat)


def _attention_bwd(qc, kc, kct, do, lse_rows, delta_rows, TQ):
    S = kc.shape[0]
    R = HEADS * TQ
    nq = S // TQ

    def body(k_ref, kt_ref, q_ref, do_ref, lser_ref, dr_ref, dk_ref, dqt_ref, dk_s, dv_s):
        j = pl.program_id(0)

        @pl.when(j == 0)
        def _():
            def zero(i, carry):
                dqt_ref[i] = jnp.zeros((QK_PAD, R), F32)
                return carry
            lax.fori_loop(0, nq, zero, 0)

        k = k_ref[...]
        kt = kt_ref[...]
        v = k[:, :KV_LORA]
        dk_s[...] = jnp.zeros((TQ, QK_PAD), F32)
        dv_s[...] = jnp.zeros((TQ, KV_LORA), F32)

        def step(i, masked):
            rows = pl.ds(pl.multiple_of(i * TQ, TQ), TQ)
            q = q_ref[:, rows, :].reshape(R, QK_PAD)
            do = do_ref[:, rows, :].reshape(R, KV_LORA)
            st = _dot_nt(k, q) * SM_SCALE
            if masked:
                st = jnp.where(_diag_mask(TQ, True), st, -jnp.inf)
            pt = jnp.exp(st - lser_ref[i])
            dv_s[...] += _dot(pt, do)
            dpt = _dot_nt(v, do)
            dst = (pt * (dpt - dr_ref[i])).astype(BF16)
            dk_s[...] += _dot(dst, q)
            dqt_ref[i] += _dot(kt, dst)

        def loop(i, carry):
            step(i, False)
            return carry

        step(j, True)
        lax.fori_loop(j + 1, nq, loop, 0)
        dk = dk_s[...] * SM_SCALE
        dk_ref[:, 0:KV_LORA] = dk[:, 0:KV_LORA] + dv_s[...]
        dk_ref[:, KV_LORA:QK_PAD] = dk[:, KV_LORA:QK_PAD]

    return pl.pallas_call(
        body, name="attention_bwd", grid=(nq,),
        out_shape=[jax.ShapeDtypeStruct((S, QK_PAD), F32), jax.ShapeDtypeStruct((nq, QK_PAD, R), F32)],
        in_specs=[pl.BlockSpec((TQ, QK_PAD), lambda j: (j, 0)), pl.BlockSpec((None, QK_PAD, TQ), lambda j: (j, 0, 0)),
                  VMEM_SPEC, VMEM_SPEC, VMEM_SPEC, VMEM_SPEC],
        out_specs=[pl.BlockSpec((TQ, QK_PAD), lambda j: (j, 0)), VMEM_SPEC],
        scratch_shapes=[pltpu.VMEM((TQ, QK_PAD), F32), pltpu.VMEM((TQ, KV_LORA), F32)],
        compiler_params=_params(("arbitrary",)),
    )(kc, kct, qc, do, lse_rows, delta_rows)


def _pre_attention_backward(x, dx1, proj, dqt, dkc, du, cos, sin, mod6, g_mix, g_q, g_kv, w_in, w_uq, w_uk_t, T, TQ):
    S = x.shape[0]

    def body(x_ref, dx1_ref, proj_ref, dqt_ref, dkc_ref, du_ref, cos_ref, sin_ref, mod_ref, gm_ref, gq_ref, gkv_ref,
             win_ref, wuq_ref, wuk_ref, gx_ref, dq_ref, dql_ref, cq_ref, dproj_ref, h1_ref, st_ref):
        @pl.when(pl.program_id(0) == 0)
        def _():
            st_ref[...] = jnp.zeros_like(st_ref)

        cos_t, sin_t = cos_ref[...], sin_ref[...]
        low = lax.broadcasted_iota(jnp.int32, (T, 128), 1) < ROPE
        rope_parts = []
        for h in range(HEADS):
            dqc = jnp.concatenate([jnp.transpose(dqt_ref[s, :, h * TQ:(h + 1) * TQ]) for s in range(T // TQ)], axis=0)
            dqc = dqc * SM_SCALE
            dql = dqc[:, 0:KV_LORA].astype(BF16)
            dql_ref[h] = dql
            dq_ref[:, h * NOPE:(h + 1) * NOPE] = _dot(dql, wuk_ref[h]).astype(BF16)
            rope_parts.append(dqc[:, KV_LORA:QK_PAD])
        for pair in range(2):
            d = jnp.where(low, rope_parts[2 * pair], rope_parts[2 * pair + 1])
            dq_ref[:, O_QA + 128 * pair:O_QA + 128 * (pair + 1)] = _rope_bwd(d, cos_t, sin_t).astype(BF16)
        dcq = _dot_nt(dq_ref[...], wuq_ref[...])
        cqh, rq = _rms(proj_ref[:, 0:Q_LORA])
        cq_ref[...] = (cqh * gq_ref[...]).astype(BF16)
        st_ref[3:4, 0:Q_LORA] += jnp.sum(dcq * cqh, axis=0, keepdims=True)
        dproj_ref[:, 0:Q_LORA] = _rms_bwd(dcq * gq_ref[...], cqh, rq).astype(BF16)
        dckv = dkc_ref[:, 0:KV_LORA]
        ckvh, rkv = _rms(proj_ref[:, O_CKV:O_KR])
        st_ref[4:5, 0:KV_LORA] += jnp.sum(dckv * ckvh, axis=0, keepdims=True)
        dproj_ref[:, O_CKV:O_KR] = _rms_bwd(dckv * gkv_ref[...], ckvh, rkv).astype(BF16)
        dkr = _rope_bwd(dkc_ref[:, KV_LORA:QK_PAD], cos_t, sin_t)
        dkr = jnp.where(low, dkr + pltpu.roll(dkr, ROPE, 1), 0.0)
        dproj_ref[:, O_KR:O_U] = dkr.astype(BF16)
        dproj_ref[:, O_U:PROJ_W] = du_ref[...].astype(BF16)
        dh1 = _dot_nt(dproj_ref[...], win_ref[...])
        xh, r1 = _rms(x_ref[...])
        n1 = xh * gm_ref[...]
        h1_ref[...] = (n1 * (1.0 + mod_ref[1:2, :]) + mod_ref[0:1, :]).astype(BF16)
        st_ref[0:1, :] += jnp.sum(dh1, axis=0, keepdims=True)
        st_ref[1:2, :] += jnp.sum(dh1 * n1, axis=0, keepdims=True)
        dn1 = dh1 * (1.0 + mod_ref[1:2, :])
        st_ref[2:3, :] += jnp.sum(dn1 * xh, axis=0, keepdims=True)
        gx_ref[...] = _rms_bwd(dn1 * gm_ref[...], xh, r1) + dx1_ref[...]

    row = lambda w: pl.BlockSpec((T, w), lambda i: (i, 0))
    return pl.pallas_call(
        body, name="pre_attention_backward", grid=(S // T,),
        out_shape=[jax.ShapeDtypeStruct((S, D_MODEL), F32), jax.ShapeDtypeStruct((S, Q_W), BF16),
                   jax.ShapeDtypeStruct((HEADS, S, KV_LORA), BF16),
                   jax.ShapeDtypeStruct((S, Q_LORA), BF16), jax.ShapeDtypeStruct((S, PROJ_W), BF16),
                   jax.ShapeDtypeStruct((S, D_MODEL), BF16), jax.ShapeDtypeStruct((8, D_MODEL), F32)],
        in_specs=[row(D_MODEL), row(D_MODEL), row(PROJ_W),
                  pl.BlockSpec((T // TQ, QK_PAD, HEADS * TQ), lambda i: (i, 0, 0)),
                  row(QK_PAD), row(POOL_W), row(128), row(128), _full((N_MOD, D_MODEL)), _full((1, D_MODEL)),
                  _full((1, Q_LORA)), _full((1, KV_LORA)), _full((D_MODEL, PROJ_W)), _full((Q_LORA, Q_W)),
                  _full((HEADS, KV_LORA, NOPE))],
        out_specs=[row(D_MODEL), row(Q_W), pl.BlockSpec((HEADS, T, KV_LORA), lambda i: (0, i, 0)), row(Q_LORA),
                   row(PROJ_W), row(D_MODEL), _full((8, D_MODEL))],
        compiler_params=_params(("arbitrary",)),
    )(x, dx1, proj, dqt, dkc, du, cos, sin, mod6, g_mix, g_q, g_kv, w_in, w_uq, w_uk_t)


def _ada_grads(c_all, dmod_all, chip):
    cols = N_MOD * D_MODEL // N_CHIPS

    def body(col_ref, c_ref, dcol_ref, dall_ref, gw_ref, gb_ref):
        call = c_ref[...]
        act = call * jax.nn.sigmoid(call)
        gw_ref[...] = _dot_tn(act, dcol_ref[...])
        d = dall_ref[...]
        acc = d[0:1, :]
        for b in range(1, 8):
            acc = acc + d[b:b + 1, :]
        gb_ref[...] = acc

    return pl.pallas_call(
        body, name="ada_grads",
        out_shape=[jax.ShapeDtypeStruct((D_MODEL, cols), F32), jax.ShapeDtypeStruct((1, N_MOD * D_MODEL), F32)],
        grid_spec=pltpu.PrefetchScalarGridSpec(
            num_scalar_prefetch=1, grid=(1,),
            in_specs=[pl.BlockSpec((8, D_MODEL), lambda s, col_ref: (0, 0)),
                      pl.BlockSpec((8, cols), lambda s, col_ref: (0, col_ref[0])),
                      pl.BlockSpec((8, N_MOD * D_MODEL), lambda s, col_ref: (0, 0))],
            out_specs=[pl.BlockSpec((D_MODEL, cols), lambda s, col_ref: (0, 0)),
                       pl.BlockSpec((1, N_MOD * D_MODEL), lambda s, col_ref: (0, 0))]),
        compiler_params=_params(("arbitrary",)),
    )(chip, c_all, dmod_all, dmod_all)


def _adamw(w, g, m, v, name):
    rows, cols = w.shape
    T = _row_tile(rows, 256)

    def body(w_ref, g_ref, m_ref, v_ref, d_ref, nm_ref, nv_ref):
        g = g_ref[...]
        m2 = ADAM_B1 * m_ref[...] + (1.0 - ADAM_B1) * g
        v2 = ADAM_B2 * v_ref[...] + (1.0 - ADAM_B2) * (g * g)
        m_hat = m2 / (1.0 - ADAM_B1 ** ADAM_STEP)
        v_hat = v2 / (1.0 - ADAM_B2 ** ADAM_STEP)
        d_ref[...] = -ADAM_LR * (m_hat / (jnp.sqrt(v_hat) + ADAM_EPS) + ADAM_WD * w_ref[...])
        nm_ref[...] = m2
        nv_ref[...] = v2

    spec = pl.BlockSpec((T, cols), lambda i: (i, 0))
    return pl.pallas_call(
        body, name=name, grid=(rows // T,),
        out_shape=[jax.ShapeDtypeStruct((rows, cols), F32)] * 3,
        in_specs=[spec] * 4, out_specs=[spec] * 3,
        compiler_params=_params(("parallel",)),
    )(w, g, m, v)


SMALL_NAMES = ("w_uk", "w_uv", "w_pool", "g_mix", "g_q", "g_kv", "pool_scale", "g_ffn", "g_final", "b_ada")
SMALL_ROWS = 1664


def _pack_rows(parts):
    flat = jnp.concatenate([p.reshape(-1) for p in parts])
    pad = (-flat.shape[0]) % 128
    if pad:
        flat = jnp.concatenate([flat, jnp.zeros((pad,), F32)])
    return flat.reshape(-1, 128)


def kernel(x, c, positions, w_ada, b_ada, g_mix, w_in, g_q, g_kv, w_uq, w_uk, w_uv, w_pool, pool_scale, w_o, g_ffn, w_gate, w_up, w_down, g_final, loss_target, m_w_ada, m_b_ada, m_g_mix, m_w_in, m_g_q, m_g_kv, m_w_uq, m_w_uk, m_w_uv, m_w_pool, m_pool_scale, m_w_o, m_g_ffn, m_w_gate, m_w_up, m_w_down, m_g_final, v_w_ada, v_b_ada, v_g_mix, v_w_in, v_g_q, v_g_kv, v_w_uq, v_w_uk, v_w_uv, v_w_pool, v_pool_scale, v_w_o, v_g_ffn, v_w_gate, v_w_up, v_w_down, v_g_final):
    S = x.shape[1]
    T = _row_tile(S, 512)
    TQ = _row_tile(S, 256)
    TW = _row_tile(S, 1024)
    ix, iy, ic = lax.axis_index("x"), lax.axis_index("y"), lax.axis_index("c")
    chip = (2 * ix + iy).astype(jnp.int32)
    chip_arr = chip.reshape(1)
    core_arr = ic.astype(jnp.int32).reshape(1)

    xs, tgt = x[0], loss_target[0]

    ada_cols = w_ada.shape[2]
    b_cols = lax.dynamic_slice(b_ada, (0, chip * ada_cols), (1, ada_cols))
    mod, c_all = _mod_exchange(c, w_ada[0], b_cols)
    mod6 = mod.reshape(N_MOD, D_MODEL)

    win = w_in[0]
    win_p = jnp.concatenate([win[:, :O_KR + ROPE], win[:, O_KR:O_KR + ROPE], win[:, O_KR + ROPE:]], axis=1).astype(BF16)
    wuq = w_uq[0]
    wuq_p = jnp.concatenate([wuq[:, h, :NOPE] for h in range(HEADS)] + [wuq[:, h, NOPE:] for h in range(HEADS)],
                            axis=1).astype(BF16)
    first = _weight_gather([win_p, wuq_p])
    w_in_f = first[0].reshape(D_MODEL, PROJ_W)
    w_uq_f = first[1].reshape(Q_LORA, Q_W)
    w_uk_t = jnp.transpose(w_uk[0], (1, 0, 2)).astype(BF16)
    w_uv_t = jnp.transpose(w_uv[0], (1, 0, 2)).astype(BF16)
    w_pool_b = w_pool[0].astype(BF16)
    later = [w_o[0].astype(BF16), w_gate[0].astype(BF16), w_up[0].astype(BF16), w_down[0].astype(BF16)]
    wg_lands = _place_shards(chip_arr, later)
    wg_lands, mod6, w_in_f = lax.optimization_barrier((wg_lands, mod6, w_in_f))
    wg_send, wg_recv, wg_lands, token = _split_start(
        "weights_start", later, wg_lands, 3 * len(later), _plan_gather_start)
    mod6 = mod6 + token[0, 0]

    half = ROPE // 2
    freqs = jnp.power(ROPE_THETA, -jnp.arange(half, dtype=F32) / half)
    cos, sin = _rope_tables(positions.reshape(S, 1), jnp.tile(freqs, 4).reshape(1, 128))
    proj, q, qc, kc, kct = _pre_attention(xs, mod6, g_mix, g_q, g_kv, w_in_f, w_uq_f, w_uk_t, cos, sin, T, TQ)
    o_lat, y_mla, lse_rows = _attention_fwd(qc, kc, kct, w_uv_t, TQ)
    wg_send, wg_recv, wg_lands, token = _split_relay(
        "weights_relay", wg_send, wg_recv, later, wg_lands, y_mla, 3 * len(later), _plan_gather_landed,
        _plan_gather_relay)
    pooled = _pool_forward(proj)
    wg_lands = _split_wait("weights_wait", wg_send, wg_recv, [], wg_lands, pooled, _plan_gather_wait)
    w_o_f = wg_lands[0].reshape(1024, D_MODEL)
    w_gate_f, w_up_f, w_down_f = wg_lands[1], wg_lands[2], wg_lands[3]
    x1, mix, mix_in = _mix_out(y_mla, pooled, w_pool_b, pool_scale, w_o_f, xs, mod6, T)
    gate, up, dx2, st_f = _ffn_forward(x1, mod6, g_ffn, g_final.reshape(1, D_MODEL), tgt, w_gate_f, w_up_f, w_down_f, T)

    dgate, dup, act, dff, h2, dx1, st_b = _ffn_backward(dx2, x1, gate, up, mod6, g_ffn, w_gate_f, w_up_f, w_down_f, T)
    steps = S // TW
    chunk_spec = pl.BlockSpec((None, TW, FF_CHUNK), lambda g, i: (g, i, 0))
    wide_spec = pl.BlockSpec((TW, D_MODEL), lambda g, i: (i, 0))
    g_down = _tn_matmul(act, dff, chunk_spec, wide_spec, N_CHIPS, FF_CHUNK, D_MODEL, steps, "grad_w_down")
    g_gate = _tn_matmul(h2, dgate, wide_spec, chunk_spec, N_CHIPS, D_MODEL, FF_CHUNK, steps, "grad_w_gate")
    g_up = _tn_matmul(h2, dup, wide_spec, chunk_spec, N_CHIPS, D_MODEL, FF_CHUNK, steps, "grad_w_up")

    ffn_names = ("w_gate", "w_up", "w_down")
    ffn_grads = [g_gate, g_up, g_down]
    f_send, f_recv, f_lands, token = _split_start(
        "ffn_swap_start", ffn_grads,
        [jax.ShapeDtypeStruct((N_CHIPS, g.shape[1] // 2, g.shape[2]), F32) for g in ffn_grads], 3, _plan_swap_start)
    mod6 = mod6 + token[0, 0]

    dmix, dz, dpooled, dy_mla, do_lat, delta_rows, st_m = _mix_backward(
        dx1, mix, mod6, w_o_f, pooled, w_pool_b, pool_scale, w_uv_t, o_lat, T, TQ)
    g_o = _tn_matmul(mix_in, dmix, wide_spec, wide_spec, 1, 1024, D_MODEL, steps, "grad_w_o")
    col128 = pl.BlockSpec((TW, 128), lambda g, i: (i, g))
    head128 = pl.BlockSpec((None, TW, 128), lambda g, i: (g, i, 0))
    g_pool = _tn_matmul(pooled, dz, col128, col128, 4, POOL_GROUP, POOL_GROUP, steps, "grad_w_pool")
    g_uv_t = _tn_matmul(o_lat, dy_mla, head128, col128, HEADS, KV_LORA, 128, steps, "grad_w_uv")
    du = _pool_backward(dpooled)
    f_got = _split_wait("ffn_swap_wait", f_send, f_recv, ffn_grads, f_lands, du, _plan_swap_wait)
    f_sums = [_add_my_half(core_arr, a, b, "add_half_" + n) for a, b, n in zip(ffn_grads, f_got, ffn_names)]
    f_send, f_recv, f_lands, token = _split_start(
        "ffn_exchange_start", f_sums, [jax.ShapeDtypeStruct((3,) + s.shape[1:], F32) for s in f_sums], 9,
        _plan_exchange_start)
    delta_rows = delta_rows + token[0, 0]
    dkc, dqt = _attention_bwd(qc, kc, kct, do_lat, lse_rows, delta_rows, TQ)
    grad_x, dq, dql, c_q, dproj, h1, st_p = _pre_attention_backward(
        xs, dx1, proj, dqt, dkc, du, cos, sin, mod6, g_mix, g_q, g_kv, w_in_f, w_uq_f, w_uk_t, T, TQ)
    g_uk_t = _tn_matmul(dql, q, head128, col128, HEADS, KV_LORA, NOPE, steps, "grad_w_uk")
    g_uq_p = _tn_matmul(c_q, dq, pl.BlockSpec((TW, Q_LORA), lambda g, i: (i, 0)),
                        pl.BlockSpec((TW, Q_W), lambda g, i: (i, 0)), 1, Q_LORA, Q_W, steps, "grad_w_uq")
    g_in_p = _tn_matmul(h1, dproj, wide_spec, pl.BlockSpec((TW, PROJ_W), lambda g, i: (i, 0)), 1, D_MODEL, PROJ_W,
                        steps, "grad_w_in")

    g_in = jnp.concatenate([g_in_p[0][:, :O_KR + ROPE], g_in_p[0][:, O_U:]], axis=1).reshape(N_CHIPS, -1, 960)
    uq = g_uq_p[0]
    g_uq = jnp.concatenate([jnp.concatenate([uq[:, h * NOPE:(h + 1) * NOPE], uq[:, O_QA + h * ROPE:O_QA + (h + 1) * ROPE]],
                                            axis=1) for h in range(HEADS)], axis=1).reshape(N_CHIPS, -1, HEADS * HEAD_QK)
    small = _pack_rows([g_uk_t, g_uv_t, g_pool, st_p[2], st_p[3, :Q_LORA], st_p[4, :KV_LORA], st_m[1, :POOL_W],
                        st_b[2], st_f[0]])
    small = jnp.concatenate([small, jnp.zeros((SMALL_ROWS - small.shape[0], 128), F32)]).reshape(N_CHIPS, -1, 128)
    grads = [g_in, g_uq, g_o.reshape(N_CHIPS, -1, D_MODEL), small]
    dmod = jnp.stack([st_p[0], st_p[1], st_m[0], st_b[0], st_b[1], st_f[1]]).reshape(48, 128)

    f_others = _split_wait("ffn_exchange_wait", f_send, f_recv, f_sums, f_lands, g_in_p, _plan_exchange_wait)
    chip_core = jnp.concatenate([chip_arr, core_arr])
    f_pairs = [_add_chips_into_pair(chip_core, a, b, "add_chips_" + n) for a, b, n in zip(f_sums, f_others, ffn_names)]
    f_send, f_recv, f_pairs, token = _split_start("ffn_finish_start", [], f_pairs, 3, _plan_finish_start)
    dmod = dmod + token[0, 0]

    names = ("w_in", "w_uq", "w_o", "small")
    got, dmod_all = _grad_swap_halves(grads, dmod)
    chip_sums = [_add_my_half(core_arr, a, b, "add_half_" + n) for a, b, n in zip(grads, got, names)]
    others = _grad_chip_exchange(chip_sums)
    halves = [_add_chips(chip_arr, a, b, "add_chips_" + n) for a, b, n in zip(chip_sums, others, names)]
    fulls, small_all = _grad_finish(halves[:3], halves[3])
    gw_in, gw_uq, gw_o = [f.reshape(-1, f.shape[2]) for f in fulls]
    f_fulls = _split_wait("ffn_finish_wait", f_send, f_recv, [], f_pairs, small_all, _plan_finish_wait)
    gw_gate, gw_up, gw_down = [f.reshape(-1, f.shape[2]) for f in f_fulls]
    small_all = small_all.reshape(SMALL_ROWS * 128)

    gw_ada, gb_ada = _ada_grads(c_all, dmod_all.reshape(8, N_MOD * D_MODEL), chip_arr)

    n_sq = KV_LORA * HEADS * 128
    sizes = [n_sq, n_sq, n_sq, D_MODEL, Q_LORA, KV_LORA, POOL_W, D_MODEL, D_MODEL]
    offs = [0]
    for s_ in sizes:
        offs.append(offs[-1] + s_)
    piece = lambda k: small_all[offs[k]:offs[k + 1]]
    grads_small = {
        "w_uk": jnp.transpose(piece(0).reshape(HEADS, KV_LORA, NOPE), (1, 0, 2)),
        "w_uv": jnp.transpose(piece(1).reshape(HEADS, KV_LORA, 128), (1, 0, 2)),
        "w_pool": piece(2).reshape(4, POOL_GROUP, POOL_GROUP),
        "g_mix": piece(3), "g_q": piece(4), "g_kv": piece(5), "pool_scale": piece(6), "g_ffn": piece(7),
        "g_final": piece(8), "b_ada": gb_ada.reshape(-1),
    }
    weights_small = {"w_uk": w_uk, "w_uv": w_uv, "w_pool": w_pool, "g_mix": g_mix, "g_q": g_q, "g_kv": g_kv,
                     "pool_scale": pool_scale, "g_ffn": g_ffn, "g_final": g_final, "b_ada": b_ada}
    m_small = {"w_uk": m_w_uk, "w_uv": m_w_uv, "w_pool": m_w_pool, "g_mix": m_g_mix, "g_q": m_g_q, "g_kv": m_g_kv,
               "pool_scale": m_pool_scale, "g_ffn": m_g_ffn, "g_final": m_g_final, "b_ada": m_b_ada}
    v_small = {"w_uk": v_w_uk, "w_uv": v_w_uv, "w_pool": v_w_pool, "g_mix": v_g_mix, "g_q": v_g_q, "g_kv": v_g_kv,
               "pool_scale": v_pool_scale, "g_ffn": v_g_ffn, "g_final": v_g_final, "b_ada": v_b_ada}
    pack = lambda d: _pack_rows([d[n] for n in SMALL_NAMES])
    d_s, m_s, v_s = _adamw(pack(weights_small), pack(grads_small), pack(m_small), pack(v_small), "adamw_small")

    def unpack(flat2d):
        flat = flat2d.reshape(-1)
        out, o = {}, 0
        for n in SMALL_NAMES:
            size = weights_small[n].size
            out[n] = flat[o:o + size].reshape(weights_small[n].shape)
            o += size
        return out

    delta_s, newm_s, newv_s = unpack(d_s), unpack(m_s), unpack(v_s)

    big_g = {"w_ada": gw_ada, "w_in": gw_in, "w_uq": gw_uq, "w_o": gw_o, "w_gate": gw_gate, "w_up": gw_up,
             "w_down": gw_down}
    big_w = {"w_ada": w_ada, "w_in": w_in, "w_uq": w_uq, "w_o": w_o, "w_gate": w_gate, "w_up": w_up, "w_down": w_down}
    big_m = {"w_ada": m_w_ada, "w_in": m_w_in, "w_uq": m_w_uq, "w_o": m_w_o, "w_gate": m_w_gate, "w_up": m_w_up,
             "w_down": m_w_down}
    big_v = {"w_ada": v_w_ada, "w_in": v_w_in, "w_uq": v_w_uq, "w_o": v_w_o, "w_gate": v_w_gate, "w_up": v_w_up,
             "w_down": v_w_down}
    grad_out, delta_out, newm_out, newv_out = {}, {}, {}, {}
    for n, g2 in big_g.items():
        shape = big_w[n].shape
        flat = lambda a: a.reshape(g2.shape)
        d_, m_, v_ = _adamw(flat(big_w[n]), g2, flat(big_m[n]), flat(big_v[n]), "adamw_" + n)
        grad_out[n], delta_out[n], newm_out[n], newv_out[n] = (a.reshape(shape) for a in (g2, d_, m_, v_))
    for n in SMALL_NAMES:
        grad_out[n] = grads_small[n].reshape(weights_small[n].shape)
        delta_out[n], newm_out[n], newv_out[n] = delta_s[n], newm_s[n], newv_s[n]

    loss = lax.psum(st_f[2, 0], ("x", "y", "c"))
    order = ("w_ada", "b_ada", "g_mix", "w_in", "g_q", "g_kv", "w_uq", "w_uk", "w_uv", "w_pool", "pool_scale", "w_o",
             "g_ffn", "w_gate", "w_up", "w_down", "g_final")
    return (loss, grad_x.reshape(x.shape), *[grad_out[n] for n in order], *[delta_out[n] for n in order],
            *[newm_out[n] for n in order], *[newv_out[n] for n in order])
```

```python
import functools

import jax
import jax.numpy as jnp
from jax import lax
from jax.experimental import pallas as pl
from jax.experimental.pallas import tpu as pltpu

F32 = jnp.float32
BF16 = jnp.bfloat16

D_MODEL = 1024
HEADS = 4
NOPE = 128
ROPE = 64
HEAD_QK = NOPE + ROPE
Q_LORA = 256
KV_LORA = 128
POOL_W = 512
POOL_WINDOWS = (2, 4, 8, 16)
POOL_GROUP = 128
POOL_PAD = 16
D_FF = 2816
N_CHIPS = 4
FF_CHUNK = D_FF // N_CHIPS
N_MOD = 6
EPS = 1e-6
SM_SCALE = HEAD_QK ** -0.5
ROPE_THETA = 10000.0
QK_PAD = 256
CHUNK = 64
CHUNK_SHIFT = 6

ADAM_LR = 0.001
ADAM_B1 = 0.9
ADAM_B2 = 0.999
ADAM_EPS = 1e-08
ADAM_WD = 0.01
ADAM_STEP = 10

VMEM_LIMIT = 48 * 1024 * 1024
MESH = pl.DeviceIdType.MESH
ANY = pl.BlockSpec(memory_space=pl.ANY)
VMEM_SPEC = pl.BlockSpec(memory_space=pltpu.VMEM)

PROJ_W = 1024
O_CKV = 256
O_KR = 384
O_U = 512
Q_W = 768
O_QA = 512
O_QB = 640


def _params(sem=None, vmem=VMEM_LIMIT):
    kw = dict(vmem_limit_bytes=vmem)
    if sem is not None:
        kw["dimension_semantics"] = sem
    return pltpu.CompilerParams(**kw)


def _dot(a, b):
    return jnp.dot(a.astype(BF16), b.astype(BF16), preferred_element_type=F32)


def _dot_nt(a, b):
    return lax.dot_general(a.astype(BF16), b.astype(BF16), (((1,), (1,)), ((), ())), preferred_element_type=F32)


def _dot_tn(a, b):
    return lax.dot_general(a.astype(BF16), b.astype(BF16), (((0,), (0,)), ((), ())), preferred_element_type=F32)


def _row_tile(rows, target):
    best = rows
    for t in range(8, min(rows, target) + 1, 8):
        if rows % t == 0:
            best = t
    return best if rows % best == 0 and best <= target else rows


def _rms(x):
    r = lax.rsqrt(jnp.mean(x * x, axis=-1, keepdims=True) + EPS)
    return x * r, r


def _rms_bwd(dxh, xh, r):
    return r * (dxh - xh * jnp.mean(dxh * xh, axis=-1, keepdims=True))


def _lane_first_half(shape):
    lane = lax.broadcasted_iota(jnp.int32, shape, 1)
    return (lane & (ROPE - 1)) < (ROPE // 2)


def _rope(a, cos, sin):
    first = _lane_first_half(a.shape)
    up = pltpu.roll(a, 96, 1)
    dn = pltpu.roll(a, 32, 1)
    return a * cos + jnp.where(first, -up, dn) * sin


def _rope_bwd(d, cos, sin):
    first = _lane_first_half(d.shape)
    up = pltpu.roll(d, 96, 1)
    dn = pltpu.roll(d, 32, 1)
    return d * cos + jnp.where(first, up, -dn) * sin


RELATIONS = tuple((dx, dy, dc) for dx in (0, 1) for dy in (0, 1) for dc in (0, 1) if (dx, dy, dc) != (0, 0, 0))
CHIP_RELATIONS = ((1, 0), (0, 1), (1, 1))


def _flip(v, d):
    return 1 - v if d else v


def _place():
    return lax.axis_index("x"), lax.axis_index("y"), lax.axis_index("c")


def _remote(src, dst, send_sem, recv_sem, target):
    return pltpu.make_async_remote_copy(src_ref=src, dst_ref=dst, send_sem=send_sem, recv_sem=recv_sem,
                                        device_id=target, device_id_type=MESH)


def _mod_exchange(c_row, w_ada, b_ada):
    cols = w_ada.shape[1]

    def body(c_ref, w_ref, b_ref, mod_ref, call_ref, part_ref, send1, recv1, loc1, send2, recv2, loc2):
        x, y, c = _place()
        me = 4 * x + 2 * y + c
        own = pltpu.make_async_copy(c_ref, call_ref.at[pl.ds(me, 1)], loc1)
        own.start()
        sends = []
        for k, (dx, dy, dc) in enumerate(RELATIONS):
            cp = _remote(c_ref, call_ref.at[pl.ds(me, 1)], send1.at[k], recv1.at[k],
                         (_flip(x, dx), _flip(y, dy), _flip(c, dc)))
            cp.start()
            sends.append(cp)
        for k, (dx, dy, dc) in enumerate(RELATIONS):
            src = 4 * _flip(x, dx) + 2 * _flip(y, dy) + _flip(c, dc)
            _remote(c_ref, call_ref.at[pl.ds(src, 1)], send1.at[k], recv1.at[k], (x, y, c)).wait_recv()
        own.wait()
        for cp in sends:
            cp.wait_send()
        call = call_ref[...]
        act = call * jax.nn.sigmoid(call)
        part_ref[...] = _dot(act, w_ref[...]) + b_ref[...]
        chip = 2 * x + y
        mine = pltpu.make_async_copy(part_ref.at[pl.ds(me, 1)], mod_ref.at[pl.ds(chip, 1)], loc2)
        mine.start()
        sends = []
        for k, (dx, dy) in enumerate(CHIP_RELATIONS):
            tx, ty = _flip(x, dx), _flip(y, dy)
            tb = 4 * tx + 2 * ty + c
            cp = _remote(part_ref.at[pl.ds(tb, 1)], mod_ref.at[pl.ds(chip, 1)], send2.at[k], recv2.at[k], (tx, ty, c))
            cp.start()
            sends.append(cp)
        for k, (dx, dy) in enumerate(CHIP_RELATIONS):
            src_chip = 2 * _flip(x, dx) + _flip(y, dy)
            _remote(part_ref.at[pl.ds(me, 1)], mod_ref.at[pl.ds(src_chip, 1)], send2.at[k], recv2.at[k],
                    (x, y, c)).wait_recv()
        mine.wait()
        for cp in sends:
            cp.wait_send()

    return pl.pallas_call(
        body, name="mod_exchange",
        out_shape=[jax.ShapeDtypeStruct((N_CHIPS, cols), F32), jax.ShapeDtypeStruct((8, D_MODEL), F32)],
        in_specs=[VMEM_SPEC, VMEM_SPEC, VMEM_SPEC], out_specs=[VMEM_SPEC, VMEM_SPEC],
        scratch_shapes=[pltpu.VMEM((8, cols), F32),
                        pltpu.SemaphoreType.DMA((7,)), pltpu.SemaphoreType.DMA((7,)), pltpu.SemaphoreType.DMA,
                        pltpu.SemaphoreType.DMA((3,)), pltpu.SemaphoreType.DMA((3,)), pltpu.SemaphoreType.DMA],
        compiler_params=_params(),
    )(c_row, w_ada, b_ada)


def _weight_gather(shards):
    n = len(shards)

    def body(*refs):
        ins, outs = refs[:n], refs[n:2 * n]
        send_sems, recv_sems, loc_sems = refs[2 * n:]
        x, y, c = _place()
        chip = 2 * x + y
        local = []
        for w in range(n):
            cp = pltpu.make_async_copy(ins[w], outs[w].at[chip], loc_sems.at[w])
            cp.start()
            local.append(cp)
        sends = []
        for w in range(n):
            hr = ins[w].shape[0] // 2
            half = pl.ds(c * hr, hr)
            for k, (dx, dy) in enumerate(CHIP_RELATIONS):
                cp = _remote(ins[w].at[half], outs[w].at[chip, half], send_sems.at[w, k], recv_sems.at[w, k],
                             (_flip(x, dx), _flip(y, dy), c))
                cp.start()
                sends.append(cp)
        for w in range(n):
            hr = ins[w].shape[0] // 2
            half = pl.ds(c * hr, hr)
            for k, (dx, dy) in enumerate(CHIP_RELATIONS):
                src_chip = 2 * _flip(x, dx) + _flip(y, dy)
                got = outs[w].at[src_chip, half]
                _remote(got, got, send_sems.at[w, k], recv_sems.at[w, k], (x, y, c)).wait_recv()
                cp = _remote(got, got, send_sems.at[w, 3 + k], recv_sems.at[w, 3 + k], (x, y, 1 - c))
                cp.start()
                sends.append(cp)
        for w in range(n):
            hr = ins[w].shape[0] // 2
            other = pl.ds((1 - c) * hr, hr)
            for k, (dx, dy) in enumerate(CHIP_RELATIONS):
                src_chip = 2 * _flip(x, dx) + _flip(y, dy)
                got = outs[w].at[src_chip, other]
                _remote(got, got, send_sems.at[w, 3 + k], recv_sems.at[w, 3 + k], (x, y, c)).wait_recv()
        for cp in sends:
            cp.wait_send()
        for cp in local:
            cp.wait()

    return pl.pallas_call(
        body, name="weight_gather",
        out_shape=[jax.ShapeDtypeStruct((N_CHIPS,) + s.shape, s.dtype) for s in shards],
        in_specs=[VMEM_SPEC] * n, out_specs=[ANY] * n,
        scratch_shapes=[pltpu.SemaphoreType.DMA((n, 6)), pltpu.SemaphoreType.DMA((n, 6)),
                        pltpu.SemaphoreType.DMA((n,))],
        compiler_params=_params(),
    )(*shards)


HBM_SPEC = pl.BlockSpec(memory_space=pltpu.HBM)
SEM_SPEC = pl.BlockSpec(memory_space=pltpu.SEMAPHORE)
DATAFLOW = pltpu.SideEffectType.DATAFLOW_SIDE_EFFECTING


def _in_hbm(a):
    return pltpu.with_memory_space_constraint(a, pltpu.HBM)


def _hbm_like(arrays):
    return [pltpu.HBM(a.shape, a.dtype) for a in arrays]


def _split_start(name, srcs, lands, n_remote, plan):
    lands = [lax.empty(a.shape, a.dtype) if isinstance(a, jax.ShapeDtypeStruct) else a for a in lands]
    n, m = len(srcs), len(lands)

    def body(*refs):
        src_refs, land_refs = refs[:n], refs[n:n + m]
        send_sems, recv_sems, token = refs[n + m], refs[n + m + 1], refs[n + 2 * m + 2]
        remote = plan(_place(), src_refs, land_refs)
        assert len(remote) == n_remote
        for i, (s, d, target) in enumerate(remote):
            _remote(s, d, send_sems.at[i], recv_sems.at[i], target).start()
        token[...] = jnp.zeros_like(token)

    res = pl.pallas_call(
        body, name=name,
        out_shape=(pltpu.SemaphoreType.DMA((n_remote,)), pltpu.SemaphoreType.DMA((n_remote,)),
                   *_hbm_like(lands), jax.ShapeDtypeStruct((8, 128), F32)),
        in_specs=[HBM_SPEC] * (n + m),
        out_specs=(SEM_SPEC, SEM_SPEC, *([HBM_SPEC] * m), VMEM_SPEC),
        input_output_aliases={n + i: 2 + i for i in range(m)},
        compiler_params=pltpu.CompilerParams(has_side_effects=DATAFLOW),
    )(*[_in_hbm(a) for a in srcs], *[_in_hbm(a) for a in lands])
    return res[0], res[1], list(res[2:2 + m]), res[2 + m]


def _split_wait(name, send_sems, recv_sems, srcs, lands, after, plan):
    n, m = len(srcs), len(lands)

    def body(*refs):
        src_refs, land_refs = refs[:n], refs[n:n + m]
        send_sems, recv_sems = refs[n + m], refs[n + m + 1]
        place = _place()
        for i, (s, d) in enumerate(plan(place, src_refs, land_refs)):
            cp = _remote(s, d, send_sems.at[i], recv_sems.at[i], place)
            cp.wait_send()
            cp.wait_recv()

    res = pl.pallas_call(
        body, name=name,
        out_shape=tuple(_hbm_like(lands)),
        in_specs=[HBM_SPEC] * (n + m) + [SEM_SPEC, SEM_SPEC, ANY],
        out_specs=tuple([HBM_SPEC] * m),
        input_output_aliases={n + i: i for i in range(m)},
        compiler_params=pltpu.CompilerParams(has_side_effects=DATAFLOW),
    )(*srcs, *lands, send_sems, recv_sems, after)
    return list(res)


def _split_relay(name, send_sems, recv_sems, srcs, lands, after, n_remote, plan_wait, plan_send):
    n, m = len(srcs), len(lands)

    def body(*refs):
        src_refs, land_refs = refs[:n], refs[n:n + m]
        old_send, old_recv = refs[n + m], refs[n + m + 1]
        new_send, new_recv = refs[n + m + 3], refs[n + m + 4]
        token = refs[n + m + 5 + m]
        place = _place()
        for i, (s, d) in enumerate(plan_wait(place, src_refs, land_refs)):
            cp = _remote(s, d, old_send.at[i], old_recv.at[i], place)
            cp.wait_send()
            cp.wait_recv()
        for i, (s, d, target) in enumerate(plan_send(place, land_refs)):
            _remote(s, d, new_send.at[i], new_recv.at[i], target).start()
        token[...] = jnp.zeros_like(token)

    res = pl.pallas_call(
        body, name=name,
        out_shape=(pltpu.SemaphoreType.DMA((n_remote,)), pltpu.SemaphoreType.DMA((n_remote,)),
                   *_hbm_like(lands), jax.ShapeDtypeStruct((8, 128), F32)),
        in_specs=[HBM_SPEC] * (n + m) + [SEM_SPEC, SEM_SPEC, ANY],
        out_specs=(SEM_SPEC, SEM_SPEC, *([HBM_SPEC] * m), VMEM_SPEC),
        input_output_aliases={n + i: 2 + i for i in range(m)},
        compiler_params=pltpu.CompilerParams(has_side_effects=DATAFLOW),
    )(*srcs, *lands, send_sems, recv_sems, after)
    return res[0], res[1], list(res[2:2 + m]), res[2 + m]


def _half(ref, core, axis=0):
    hr = ref.shape[axis] // 2
    return pl.ds(core * hr, hr)


def _plan_gather_start(place, src, land):
    x, y, c = place
    chip = 2 * x + y
    return [(s.at[_half(s, c)], l.at[chip, _half(s, c)], (_flip(x, dx), _flip(y, dy), c))
            for s, l in zip(src, land) for dx, dy in CHIP_RELATIONS]


def _plan_gather_landed(place, src, land):
    x, y, c = place
    return [(s.at[_half(s, c)], l.at[2 * _flip(x, dx) + _flip(y, dy), _half(s, c)])
            for s, l in zip(src, land) for dx, dy in CHIP_RELATIONS]


def _plan_gather_relay(place, land):
    x, y, c = place
    out = []
    for l in land:
        for dx, dy in CHIP_RELATIONS:
            got = l.at[2 * _flip(x, dx) + _flip(y, dy), _half(l, c, 1)]
            out.append((got, got, (x, y, 1 - c)))
    return out


def _plan_gather_wait(place, src, land):
    x, y, c = place
    out = []
    for l in land:
        for dx, dy in CHIP_RELATIONS:
            got = l.at[2 * _flip(x, dx) + _flip(y, dy), _half(l, 1 - c, 1)]
            out.append((got, got))
    return out


def _plan_swap_start(place, src, land):
    x, y, c = place
    return [(s.at[:, _half(s, 1 - c, 1), :], l, (x, y, 1 - c)) for s, l in zip(src, land)]


def _plan_swap_wait(place, src, land):
    return [(s.at[:, _half(s, 0, 1), :], l) for s, l in zip(src, land)]


def _plan_exchange_start(place, src, land):
    x, y, c = place
    remote = []
    for s, l in zip(src, land):
        for k, (dx, dy) in enumerate(CHIP_RELATIONS):
            tx, ty = _flip(x, dx), _flip(y, dy)
            remote.append((s.at[2 * tx + ty], l.at[k], (tx, ty, c)))
    return remote


def _plan_exchange_wait(place, src, land):
    return [(s.at[0], l.at[k]) for s, l in zip(src, land) for k in range(3)]


def _plan_finish_start(place, src, land):
    x, y, c = place
    return [(l.at[c], l.at[c], (x, y, 1 - c)) for l in land]


def _plan_finish_wait(place, src, land):
    x, y, c = place
    return [(l.at[c], l.at[1 - c]) for l in land]


def _grad_swap_halves(grads, dmod):
    n = len(grads)

    def body(*refs):
        ins, dmod_ref = refs[:n], refs[n]
        outs, dall_ref = refs[n + 1:2 * n + 1], refs[2 * n + 1]
        send_sems, recv_sems, dsend, drecv, dloc = refs[2 * n + 2:]
        x, y, c = _place()
        me = 4 * x + 2 * y + c
        sends = []
        for w in range(n):
            hr = ins[w].shape[1] // 2
            cp = _remote(ins[w].at[:, pl.ds((1 - c) * hr, hr), :], outs[w], send_sems.at[w], recv_sems.at[w],
                         (x, y, 1 - c))
            cp.start()
            sends.append(cp)
        own = pltpu.make_async_copy(dmod_ref, dall_ref.at[me], dloc)
        own.start()
        for k, (dx, dy, dc) in enumerate(RELATIONS):
            cp = _remote(dmod_ref, dall_ref.at[me], dsend.at[k], drecv.at[k],
                         (_flip(x, dx), _flip(y, dy), _flip(c, dc)))
            cp.start()
            sends.append(cp)
        for k, (dx, dy, dc) in enumerate(RELATIONS):
            src = 4 * _flip(x, dx) + 2 * _flip(y, dy) + _flip(c, dc)
            _remote(dmod_ref, dall_ref.at[src], dsend.at[k], drecv.at[k], (x, y, c)).wait_recv()
        for w in range(n):
            _remote(outs[w], outs[w], send_sems.at[w], recv_sems.at[w], (x, y, c)).wait_recv()
        own.wait()
        for cp in sends:
            cp.wait_send()

    out_shape = [jax.ShapeDtypeStruct((N_CHIPS, g.shape[1] // 2, g.shape[2]), F32) for g in grads]
    out_shape.append(jax.ShapeDtypeStruct((8,) + dmod.shape, F32))
    res = pl.pallas_call(
        body, name="grad_swap_halves",
        out_shape=out_shape, in_specs=[ANY] * n + [VMEM_SPEC], out_specs=[ANY] * (n + 1),
        scratch_shapes=[pltpu.SemaphoreType.DMA((n,)), pltpu.SemaphoreType.DMA((n,)),
                        pltpu.SemaphoreType.DMA((7,)), pltpu.SemaphoreType.DMA((7,)), pltpu.SemaphoreType.DMA],
        compiler_params=_params(),
    )(*grads, dmod)
    return res[:n], res[n]


def _grad_finish(halves, small_half):
    n = len(halves)

    def body(*refs):
        ins, sm_ref = refs[:n], refs[n]
        outs, sall_ref = refs[n + 1:2 * n + 1], refs[2 * n + 1]
        send_sems, recv_sems, loc_sems, ssend, srecv, sloc = refs[2 * n + 2:]
        x, y, c = _place()
        chip = 2 * x + y
        local, sends = [], []
        for w in range(n):
            cp = pltpu.make_async_copy(ins[w], outs[w].at[c], loc_sems.at[w])
            cp.start()
            local.append(cp)
            cp = _remote(ins[w], outs[w].at[c], send_sems.at[w], recv_sems.at[w], (x, y, 1 - c))
            cp.start()
            sends.append(cp)
        cp = pltpu.make_async_copy(sm_ref, sall_ref.at[chip, c], sloc)
        cp.start()
        local.append(cp)
        for k, (dx, dy, dc) in enumerate(RELATIONS):
            cp = _remote(sm_ref, sall_ref.at[chip, c], ssend.at[k], srecv.at[k],
                         (_flip(x, dx), _flip(y, dy), _flip(c, dc)))
            cp.start()
            sends.append(cp)
        for k, (dx, dy, dc) in enumerate(RELATIONS):
            got = sall_ref.at[2 * _flip(x, dx) + _flip(y, dy), _flip(c, dc)]
            _remote(got, got, ssend.at[k], srecv.at[k], (x, y, c)).wait_recv()
        for w in range(n):
            got = outs[w].at[1 - c]
            _remote(got, got, send_sems.at[w], recv_sems.at[w], (x, y, c)).wait_recv()
        for cp in sends:
            cp.wait_send()
        for cp in local:
            cp.wait()

    out_shape = [jax.ShapeDtypeStruct((2,) + h.shape, F32) for h in halves]
    out_shape.append(jax.ShapeDtypeStruct((N_CHIPS, 2) + small_half.shape, F32))
    res = pl.pallas_call(
        body, name="grad_finish",
        out_shape=out_shape, in_specs=[VMEM_SPEC] * (n + 1), out_specs=[ANY] * (n + 1),
        scratch_shapes=[pltpu.SemaphoreType.DMA((n,)), pltpu.SemaphoreType.DMA((n,)), pltpu.SemaphoreType.DMA((n,)),
                        pltpu.SemaphoreType.DMA((7,)), pltpu.SemaphoreType.DMA((7,)), pltpu.SemaphoreType.DMA],
        compiler_params=_params(),
    )(*halves, small_half)
    return res[:n], res[n]


def _add_my_half(core, full, got, name):
    _, hr, cols = got.shape

    def body(core_ref, a_ref, b_ref, o_ref):
        o_ref[...] = a_ref[...] + b_ref[...]

    return pl.pallas_call(
        body, name=name,
        out_shape=jax.ShapeDtypeStruct(got.shape, F32),
        grid_spec=pltpu.PrefetchScalarGridSpec(
            num_scalar_prefetch=1, grid=(N_CHIPS,),
            in_specs=[pl.BlockSpec((None, hr, cols), lambda s, core_ref: (s, core_ref[0], 0)),
                      pl.BlockSpec((None, hr, cols), lambda s, core_ref: (s, 0, 0))],
            out_specs=pl.BlockSpec((None, hr, cols), lambda s, core_ref: (s, 0, 0))),
        compiler_params=_params(("arbitrary",)),
    )(core, full, got)


def _add_chips(chip, mine, got, name):
    _, hr, cols = mine.shape

    def body(chip_ref, a_ref, b_ref, o_ref):
        o_ref[...] = ((a_ref[...] + b_ref[0]) + b_ref[1]) + b_ref[2]

    return pl.pallas_call(
        body, name=name,
        out_shape=jax.ShapeDtypeStruct((hr, cols), F32),
        grid_spec=pltpu.PrefetchScalarGridSpec(
            num_scalar_prefetch=1, grid=(1,),
            in_specs=[pl.BlockSpec((None, hr, cols), lambda s, chip_ref: (chip_ref[0], 0, 0)),
                      pl.BlockSpec((3, hr, cols), lambda s, chip_ref: (0, 0, 0))],
            out_specs=pl.BlockSpec((hr, cols), lambda s, chip_ref: (0, 0))),
        compiler_params=_params(("arbitrary",)),
    )(chip, mine, got)


def _add_chips_into_pair(chip_core, mine, got, name):
    _, hr, cols = mine.shape

    def body(cc_ref, a_ref, b_ref, o_ref):
        o_ref[...] = ((a_ref[...] + b_ref[0]) + b_ref[1]) + b_ref[2]

    return pl.pallas_call(
        body, name=name,
        out_shape=jax.ShapeDtypeStruct((2, hr, cols), F32),
        grid_spec=pltpu.PrefetchScalarGridSpec(
            num_scalar_prefetch=1, grid=(1,),
            in_specs=[pl.BlockSpec((None, hr, cols), lambda s, cc_ref: (cc_ref[0], 0, 0)),
                      pl.BlockSpec((3, hr, cols), lambda s, cc_ref: (0, 0, 0))],
            out_specs=pl.BlockSpec((None, hr, cols), lambda s, cc_ref: (cc_ref[1], 0, 0))),
        compiler_params=_params(("arbitrary",)),
    )(chip_core, mine, got)


def _place_shards(chip, shards):
    n = len(shards)

    def body(chip_ref, *refs):
        for w in range(n):
            refs[n + w][...] = refs[w][...]

    return pl.pallas_call(
        body, name="place_shards",
        out_shape=[jax.ShapeDtypeStruct((N_CHIPS,) + s.shape, s.dtype) for s in shards],
        grid_spec=pltpu.PrefetchScalarGridSpec(
            num_scalar_prefetch=1, grid=(1,),
            in_specs=[pl.BlockSpec(s.shape, lambda i, chip_ref: (0, 0)) for s in shards],
            out_specs=[pl.BlockSpec((None,) + s.shape, lambda i, chip_ref: (chip_ref[0], 0, 0)) for s in shards]),
        compiler_params=_params(("arbitrary",)),
    )(chip, *shards)


def _rope_tables(pos_col, freqs):
    S = pos_col.shape[0]
    T = _row_tile(S, 1024)

    def body(p_ref, f_ref, cos_ref, sin_ref):
        ang = p_ref[...].astype(F32) * f_ref[...]
        cos_ref[...] = jnp.cos(ang)
        sin_ref[...] = jnp.sin(ang)

    return pl.pallas_call(
        body, name="rope_tables", grid=(S // T,),
        out_shape=[jax.ShapeDtypeStruct((S, 128), F32)] * 2,
        in_specs=[pl.BlockSpec((T, 1), lambda i: (i, 0)), pl.BlockSpec((1, 128), lambda i: (0, 0))],
        out_specs=[pl.BlockSpec((T, 128), lambda i: (i, 0))] * 2,
        compiler_params=_params(("parallel",)),
    )(pos_col, freqs)


def _full(shape):
    zeros = (0,) * len(shape)
    return pl.BlockSpec(shape, lambda *_: zeros)


def _pre_attention(x, mod6, g_mix, g_q, g_kv, w_in, w_uq, w_uk_t, cos, sin, T, TQ):
    S = x.shape[0]

    def body(x_ref, mod_ref, gm_ref, gq_ref, gkv_ref, win_ref, wuq_ref, wuk_ref, cos_ref, sin_ref,
             proj_ref, q_ref, qc_ref, kc_ref, kct_ref):
        xh, _ = _rms(x_ref[...])
        h1 = ((xh * gm_ref[...]) * (1.0 + mod_ref[1:2, :]) + mod_ref[0:1, :]).astype(BF16)
        rows_in = D_MODEL // N_CHIPS
        proj = _dot_nt(h1[:, 0:rows_in], win_ref[0])
        for j in range(1, N_CHIPS):
            proj = proj + _dot_nt(h1[:, j * rows_in:(j + 1) * rows_in], win_ref[j])
        proj_ref[...] = proj
        cqh, _ = _rms(proj[:, :Q_LORA])
        c_q = cqh * gq_ref[...]
        ckvh, _ = _rms(proj[:, O_CKV:O_KR])
        c_kv = ckvh * gkv_ref[...]
        q = _dot(c_q, wuq_ref[...])
        q_ref[...] = q
        cos_t, sin_t = cos_ref[...], sin_ref[...]
        ropes = (_rope(q[:, O_QA:O_QB], cos_t, sin_t), _rope(q[:, O_QB:Q_W], cos_t, sin_t))
        low = lax.broadcasted_iota(jnp.int32, (T, 128), 1) < ROPE
        for h in range(HEADS):
            q_lat = _dot_nt(q[:, h * NOPE:(h + 1) * NOPE], wuk_ref[h])
            keep = low if h % 2 == 0 else jnp.logical_not(low)
            qc_ref[h, :, 0:KV_LORA] = q_lat.astype(BF16)
            qc_ref[h, :, KV_LORA:QK_PAD] = jnp.where(keep, ropes[h // 2], 0.0).astype(BF16)
        k_rope = _rope(proj[:, O_KR:O_U], cos_t, sin_t)
        kc_ref[:, 0:KV_LORA] = c_kv.astype(BF16)
        kc_ref[:, KV_LORA:QK_PAD] = k_rope.astype(BF16)
        lat_t, rope_t = jnp.transpose(c_kv), jnp.transpose(k_rope)
        for s in range(T // TQ):
            kct_ref[s, 0:KV_LORA, :] = lat_t[:, s * TQ:(s + 1) * TQ].astype(BF16)
            kct_ref[s, KV_LORA:QK_PAD, :] = rope_t[:, s * TQ:(s + 1) * TQ].astype(BF16)

    row = lambda w: pl.BlockSpec((T, w), lambda i: (i, 0))
    return pl.pallas_call(
        body, name="pre_attention", grid=(S // T,),
        out_shape=[jax.ShapeDtypeStruct((S, PROJ_W), F32), jax.ShapeDtypeStruct((S, Q_W), F32),
                   jax.ShapeDtypeStruct((HEADS, S, QK_PAD), BF16), jax.ShapeDtypeStruct((S, QK_PAD), BF16),
                   jax.ShapeDtypeStruct((S // TQ, QK_PAD, TQ), BF16)],
        in_specs=[row(D_MODEL), _full((N_MOD, D_MODEL)), _full((1, D_MODEL)), _full((1, Q_LORA)), _full((1, KV_LORA)),
                  _full((N_CHIPS, PROJ_W, D_MODEL // N_CHIPS)), _full((Q_LORA, Q_W)), _full((HEADS, KV_LORA, NOPE)),
                  row(128), row(128)],
        out_specs=[row(PROJ_W), row(Q_W), pl.BlockSpec((HEADS, T, QK_PAD), lambda i: (0, i, 0)), row(QK_PAD),
                   pl.BlockSpec((T // TQ, QK_PAD, TQ), lambda i: (i, 0, 0))],
        compiler_params=_params(("parallel",)),
    )(x, mod6, g_mix, g_q, g_kv, w_in, w_uq, w_uk_t, cos, sin)


def _diag_mask(TQ, transposed):
    R = HEADS * TQ
    if transposed:
        key = lax.broadcasted_iota(jnp.int32, (TQ, R), 0) >> CHUNK_SHIFT
        qry = (lax.broadcasted_iota(jnp.int32, (TQ, R), 1) & (TQ - 1)) >> CHUNK_SHIFT
    else:
        qry = (lax.broadcasted_iota(jnp.int32, (R, TQ), 0) & (TQ - 1)) >> CHUNK_SHIFT
        key = lax.broadcasted_iota(jnp.int32, (R, TQ), 1) >> CHUNK_SHIFT
    return key <= qry


def _col_to_row(col):
    return jnp.transpose(jnp.broadcast_to(col, (col.shape[0], 128)))[0:1, :]


def _attention_fwd(qc, kc, kct, w_uv_t, TQ):
    S = kc.shape[0]
    R = HEADS * TQ
    nq = S // TQ

    def body(q_ref, k_ref, kt_ref, wuv_ref, o_ref, y_ref, lser_ref, m_s, l_s, acc_s):
        i = pl.program_id(0)
        q = q_ref[...].reshape(R, QK_PAD)
        m_s[...] = jnp.full((1, R), -jnp.inf, F32)
        l_s[...] = jnp.zeros((1, R), F32)
        acc_s[...] = jnp.zeros((KV_LORA, R), F32)

        def step(j, masked):
            k = k_ref[pl.ds(pl.multiple_of(j * TQ, TQ), TQ), :]
            st = _dot_nt(k, q) * SM_SCALE
            if masked:
                st = jnp.where(_diag_mask(TQ, True), st, -jnp.inf)
            m_old = m_s[...]
            m_new = jnp.maximum(m_old, jnp.max(st, axis=0, keepdims=True))
            pt = jnp.exp(st - m_new)
            alpha = jnp.exp(m_old - m_new)
            l_s[...] = alpha * l_s[...] + jnp.sum(pt, axis=0, keepdims=True)
            acc_s[...] = alpha * acc_s[...] + _dot(kt_ref[j, 0:KV_LORA, :], pt)
            m_s[...] = m_new

        def loop(j, carry):
            step(j, False)
            return carry

        lax.fori_loop(0, i, loop, 0)
        step(i, True)
        l = l_s[...]
        lser_ref[0] = m_s[...] + jnp.log(l)
        o = jnp.transpose(acc_s[...] / l).astype(BF16)
        for h in range(HEADS):
            oh = o[h * TQ:(h + 1) * TQ, :]
            o_ref[h] = oh
            y_ref[:, h * 128:(h + 1) * 128] = _dot(oh, wuv_ref[h]).astype(BF16)

    return pl.pallas_call(
        body, name="attention_fwd", grid=(nq,),
        out_shape=[jax.ShapeDtypeStruct((HEADS, S, KV_LORA), BF16), jax.ShapeDtypeStruct((S, HEADS * 128), BF16),
                   jax.ShapeDtypeStruct((nq, 1, R), F32)],
        in_specs=[pl.BlockSpec((HEADS, TQ, QK_PAD), lambda i: (0, i, 0)), _full((S, QK_PAD)),
                  _full((nq, QK_PAD, TQ)), _full((HEADS, KV_LORA, 128))],
        out_specs=[pl.BlockSpec((HEADS, TQ, KV_LORA), lambda i: (0, i, 0)), pl.BlockSpec((TQ, HEADS * 128), lambda i: (i, 0)),
                   pl.BlockSpec((1, 1, R), lambda i: (i, 0, 0))],
        scratch_shapes=[pltpu.VMEM((1, R), F32), pltpu.VMEM((1, R), F32), pltpu.VMEM((KV_LORA, R), F32)],
        compiler_params=_params(("parallel",)),
    )(qc, kc, kct, w_uv_t)


def _pool_forward(proj):
    S = proj.shape[0]
    RB = _row_tile(S, 256)

    def body(proj_ref, out_ref, pad_ref, sem):
        cp = pltpu.make_async_copy(proj_ref.at[:, pl.ds(O_U, POOL_W)], pad_ref.at[pl.ds(POOL_PAD, S)], sem)
        cp.start()
        pad_ref[0:POOL_PAD, :] = jnp.zeros((POOL_PAD, POOL_W), F32)
        cp.wait()
        for g, win in enumerate(POOL_WINDOWS):
            cols = slice(g * POOL_GROUP, (g + 1) * POOL_GROUP)
            for r0 in range(0, S, RB):
                u = pad_ref[POOL_PAD + r0:POOL_PAD + r0 + RB, cols]
                acc = u
                for k in range(1, win):
                    acc = acc + pad_ref[POOL_PAD + r0 - k:POOL_PAD + r0 - k + RB, cols]
                if r0 == 0:
                    t1 = (lax.broadcasted_iota(jnp.int32, (RB, POOL_GROUP), 0) + 1).astype(F32)
                    mean = acc / jnp.minimum(t1, float(win))
                else:
                    mean = acc * (1.0 / win)
                out_ref[r0:r0 + RB, cols] = (mean - u).astype(BF16)

    return pl.pallas_call(
        body, name="pool_forward",
        out_shape=jax.ShapeDtypeStruct((S, POOL_W), BF16),
        in_specs=[ANY], out_specs=VMEM_SPEC,
        scratch_shapes=[pltpu.VMEM((S + POOL_PAD, POOL_W), F32), pltpu.SemaphoreType.DMA],
        compiler_params=_params(),
    )(proj)


def _pool_backward(dpooled, after):
    S = dpooled.shape[0]
    RB = _row_tile(S, 256)

    def body(dp_ref, after_ref, out_ref, pad_ref, sem):
        cp = pltpu.make_async_copy(dp_ref, pad_ref.at[pl.ds(0, S)], sem)
        cp.start()
        pad_ref[S:S + POOL_PAD, :] = jnp.zeros((POOL_PAD, POOL_W), F32)
        cp.wait()
        for g, win in enumerate(POOL_WINDOWS):
            cols = slice(g * POOL_GROUP, (g + 1) * POOL_GROUP)
            head = pad_ref[0:POOL_PAD, cols]
            t1 = (lax.broadcasted_iota(jnp.int32, (POOL_PAD, POOL_GROUP), 0) + 1).astype(F32)
            pad_ref[0:POOL_PAD, cols] = head * (float(win) / jnp.minimum(t1, float(win)))
            for r0 in range(0, S, RB):
                acc = pad_ref[r0:r0 + RB, cols]
                for k in range(1, win):
                    acc = acc + pad_ref[r0 + k:r0 + k + RB, cols]
                own = pad_ref[r0:r0 + RB, cols]
                if r0 == 0:
                    own = jnp.concatenate([head, own[POOL_PAD:]], axis=0)
                out_ref[r0:r0 + RB, cols] = acc * (1.0 / win) - own

    return pl.pallas_call(
        body, name="pool_backward",
        out_shape=jax.ShapeDtypeStruct((S, POOL_W), F32),
        in_specs=[ANY, ANY], out_specs=VMEM_SPEC,
        scratch_shapes=[pltpu.VMEM((S + POOL_PAD, POOL_W), F32), pltpu.SemaphoreType.DMA],
        compiler_params=_params(),
    )(dpooled, after)


def _mix_out(y_mla, pooled, w_pool, pool_scale, w_o, x, mod6, T):
    S = x.shape[0]

    def body(ym_ref, pl_ref, wp_ref, ps_ref, wo_ref, x_ref, mod_ref, x1_ref, mix_ref, mi_ref):
        mi_ref[:, 0:512] = ym_ref[...]
        for g in range(len(POOL_WINDOWS)):
            cols = slice(g * POOL_GROUP, (g + 1) * POOL_GROUP)
            z = _dot(pl_ref[:, cols], wp_ref[g])
            mi_ref[:, 512 + g * POOL_GROUP:512 + (g + 1) * POOL_GROUP] = (z * ps_ref[:, cols]).astype(BF16)
        mix = _dot(mi_ref[...], wo_ref[...])
        mix_ref[...] = mix
        x1_ref[...] = x_ref[...] + mod_ref[2:3, :] * mix

    row = lambda w: pl.BlockSpec((T, w), lambda i: (i, 0))
    return pl.pallas_call(
        body, name="mix_out", grid=(S // T,),
        out_shape=[jax.ShapeDtypeStruct((S, D_MODEL), F32), jax.ShapeDtypeStruct((S, D_MODEL), F32),
                   jax.ShapeDtypeStruct((S, 1024), BF16)],
        in_specs=[row(512), row(POOL_W), _full((4, POOL_GROUP, POOL_GROUP)), _full((1, POOL_W)),
                  _full((1024, D_MODEL)), row(D_MODEL), _full((N_MOD, D_MODEL))],
        out_specs=[row(D_MODEL), row(D_MODEL), row(1024)],
        compiler_params=_params(("parallel",)),
    )(y_mla, pooled, w_pool, pool_scale, w_o, x, mod6)


def _ffn_forward(x1, mod6, g_ffn, g_final, target, w_gate, w_up, w_down, T):
    S = x1.shape[0]

    def body(x1_ref, mod_ref, gf_ref, gl_ref, tgt_ref, wg_ref, wu_ref, wd_ref,
             gate_ref, up_ref, dx2_ref, st_ref, h2_s, acc_s):
        i, j = pl.program_id(0), pl.program_id(1)

        @pl.when(jnp.logical_and(i == 0, j == 0))
        def _():
            st_ref[...] = jnp.zeros_like(st_ref)

        @pl.when(j == 0)
        def _():
            xh, _ = _rms(x1_ref[...])
            h2_s[...] = ((xh * gf_ref[...]) * (1.0 + mod_ref[4:5, :]) + mod_ref[3:4, :]).astype(BF16)
            acc_s[...] = jnp.zeros_like(acc_s)

        h2 = h2_s[...]
        gate = _dot_nt(h2, wg_ref[...])
        up = _dot_nt(h2, wu_ref[...])
        gate_ref[...] = gate
        up_ref[...] = up
        act = gate * jax.nn.sigmoid(gate) * up
        acc_s[...] += _dot(act, wd_ref[...])

        @pl.when(j == N_CHIPS - 1)
        def _():
            ff = acc_s[...]
            x2 = x1_ref[...] + mod_ref[5:6, :] * ff
            xh, r3 = _rms(x2)
            err = xh * gl_ref[...] - tgt_ref[...]
            dy = err * (1.0 / D_MODEL)
            dx2 = _rms_bwd(dy * gl_ref[...], xh, r3)
            dx2_ref[...] = dx2
            st_ref[0:1, :] += jnp.sum(dy * xh, axis=0, keepdims=True)
            st_ref[1:2, :] += jnp.sum(dx2 * ff, axis=0, keepdims=True)
            st_ref[2:3, :] += 0.5 * jnp.sum(err * dy)

    row = pl.BlockSpec((T, D_MODEL), lambda i, j: (i, 0))
    chunk_out = pl.BlockSpec((None, T, FF_CHUNK), lambda i, j: (j, i, 0))
    return pl.pallas_call(
        body, name="ffn_forward", grid=(S // T, N_CHIPS),
        out_shape=[jax.ShapeDtypeStruct((N_CHIPS, S, FF_CHUNK), F32), jax.ShapeDtypeStruct((N_CHIPS, S, FF_CHUNK), F32),
                   jax.ShapeDtypeStruct((S, D_MODEL), F32), jax.ShapeDtypeStruct((8, D_MODEL), F32)],
        in_specs=[row, _full((N_MOD, D_MODEL)), _full((1, D_MODEL)), _full((1, D_MODEL)), row,
                  pl.BlockSpec((None, FF_CHUNK, D_MODEL), lambda i, j: (j, 0, 0)),
                  pl.BlockSpec((None, FF_CHUNK, D_MODEL), lambda i, j: (j, 0, 0)),
                  pl.BlockSpec((None, FF_CHUNK, D_MODEL), lambda i, j: (j, 0, 0))],
        out_specs=[chunk_out, chunk_out, row, _full((8, D_MODEL))],
        scratch_shapes=[pltpu.VMEM((T, D_MODEL), BF16), pltpu.VMEM((T, D_MODEL), F32)],
        compiler_params=_params(("arbitrary", "arbitrary")),
    )(x1, mod6, g_ffn, g_final, target, w_gate, w_up, w_down)


def _ffn_backward(dx2, x1, gate, up, mod6, g_ffn, w_gate, w_up, w_down, T):
    S = x1.shape[0]

    def body(dx2_ref, x1_ref, gate_ref, up_ref, mod_ref, gf_ref, wg_ref, wu_ref, wd_ref,
             dgate_ref, dup_ref, act_ref, dff_ref, h2_ref, dx1_ref, st_ref, acc_s):
        i, j = pl.program_id(0), pl.program_id(1)

        @pl.when(jnp.logical_and(i == 0, j == 0))
        def _():
            st_ref[...] = jnp.zeros_like(st_ref)

        @pl.when(j == 0)
        def _():
            dff_ref[...] = (dx2_ref[...] * mod_ref[5:6, :]).astype(BF16)
            xh, _ = _rms(x1_ref[...])
            h2_ref[...] = ((xh * gf_ref[...]) * (1.0 + mod_ref[4:5, :]) + mod_ref[3:4, :]).astype(BF16)
            acc_s[...] = jnp.zeros_like(acc_s)

        gate, up = gate_ref[...], up_ref[...]
        sg = jax.nn.sigmoid(gate)
        silu = gate * sg
        act_ref[...] = (silu * up).astype(BF16)
        dact = _dot_nt(dff_ref[...], wd_ref[...])
        dup = (dact * silu).astype(BF16)
        dgate = (dact * up * (sg * (1.0 + gate * (1.0 - sg)))).astype(BF16)
        dup_ref[...] = dup
        dgate_ref[...] = dgate
        acc_s[...] += _dot(dgate, wg_ref[...]) + _dot(dup, wu_ref[...])

        @pl.when(j == N_CHIPS - 1)
        def _():
            dh2 = acc_s[...]
            xh, r2 = _rms(x1_ref[...])
            n2 = xh * gf_ref[...]
            st_ref[0:1, :] += jnp.sum(dh2, axis=0, keepdims=True)
            st_ref[1:2, :] += jnp.sum(dh2 * n2, axis=0, keepdims=True)
            dn2 = dh2 * (1.0 + mod_ref[4:5, :])
            st_ref[2:3, :] += jnp.sum(dn2 * xh, axis=0, keepdims=True)
            dx1_ref[...] = _rms_bwd(dn2 * gf_ref[...], xh, r2) + dx2_ref[...]

    row = pl.BlockSpec((T, D_MODEL), lambda i, j: (i, 0))
    chunk = pl.BlockSpec((None, T, FF_CHUNK), lambda i, j: (j, i, 0))
    big = jax.ShapeDtypeStruct((N_CHIPS, S, FF_CHUNK), BF16)
    return pl.pallas_call(
        body, name="ffn_backward", grid=(S // T, N_CHIPS),
        out_shape=[big, big, big, jax.ShapeDtypeStruct((S, D_MODEL), BF16), jax.ShapeDtypeStruct((S, D_MODEL), BF16),
                   jax.ShapeDtypeStruct((S, D_MODEL), F32), jax.ShapeDtypeStruct((8, D_MODEL), F32)],
        in_specs=[row, row, chunk, chunk, _full((N_MOD, D_MODEL)), _full((1, D_MODEL)),
                  pl.BlockSpec((None, FF_CHUNK, D_MODEL), lambda i, j: (j, 0, 0)),
                  pl.BlockSpec((None, FF_CHUNK, D_MODEL), lambda i, j: (j, 0, 0)),
                  pl.BlockSpec((None, FF_CHUNK, D_MODEL), lambda i, j: (j, 0, 0))],
        out_specs=[chunk, chunk, chunk, row, row, row, _full((8, D_MODEL))],
        scratch_shapes=[pltpu.VMEM((T, D_MODEL), F32)],
        compiler_params=_params(("arbitrary", "arbitrary")),
    )(dx2, x1, gate, up, mod6, g_ffn, w_gate, w_up, w_down)


def _tn_matmul(a, b, a_spec, b_spec, groups, m, n, steps, name):
    def body(a_ref, b_ref, o_ref):
        @pl.when(pl.program_id(1) == 0)
        def _():
            o_ref[...] = jnp.zeros_like(o_ref)

        o_ref[...] += _dot_tn(a_ref[...], b_ref[...])

    return pl.pallas_call(
        body, name=name, grid=(groups, steps),
        out_shape=jax.ShapeDtypeStruct((groups, m, n), F32),
        in_specs=[a_spec, b_spec],
        out_specs=pl.BlockSpec((None, m, n), lambda g, i: (g, 0, 0)),
        compiler_params=_params(("parallel", "arbitrary")),
    )(a, b)


def _mix_backward(dx1, mix, mod6, w_o, pooled, w_pool, pool_scale, w_uv_t, o_lat, T, TQ):
    S = dx1.shape[0]

    def body(dx1_ref, mix_ref, mod_ref, wo_ref, pl_ref, wp_ref, ps_ref, wuv_ref, o_ref,
             dmix_ref, dz_ref, dp_ref, dym_ref, do_ref, dr_ref, st_ref):
        @pl.when(pl.program_id(0) == 0)
        def _():
            st_ref[...] = jnp.zeros_like(st_ref)

        dx1 = dx1_ref[...]
        st_ref[0:1, :] += jnp.sum(dx1 * mix_ref[...], axis=0, keepdims=True)
        dmix = (dx1 * mod_ref[2:3, :]).astype(BF16)
        dmix_ref[...] = dmix
        dmi = _dot_nt(dmix, wo_ref[...])
        dym = dmi[:, 0:512].astype(BF16)
        dym_ref[...] = dym
        for g in range(len(POOL_WINDOWS)):
            cols = slice(g * POOL_GROUP, (g + 1) * POOL_GROUP)
            dyp = dmi[:, 512 + g * POOL_GROUP:512 + (g + 1) * POOL_GROUP]
            z = _dot(pl_ref[:, cols], wp_ref[g])
            st_ref[1:2, cols] += jnp.sum(dyp * z, axis=0, keepdims=True)
            dz = (dyp * ps_ref[:, cols]).astype(BF16)
            dz_ref[:, cols] = dz
            dp_ref[:, cols] = _dot_nt(dz, wp_ref[g])
        for h in range(HEADS):
            do = _dot_nt(dym[:, h * 128:(h + 1) * 128], wuv_ref[h]).astype(BF16)
            do_ref[h] = do
            delta = _col_to_row(jnp.sum(do.astype(F32) * o_ref[h].astype(F32), axis=1, keepdims=True))
            for s in range(T // TQ):
                dr_ref[s, :, h * TQ:(h + 1) * TQ] = delta[:, s * TQ:(s + 1) * TQ]

    row = lambda w: pl.BlockSpec((T, w), lambda i: (i, 0))
    heads = pl.BlockSpec((HEADS, T, KV_LORA), lambda i: (0, i, 0))
    return pl.pallas_call(
        body, name="mix_backward", grid=(S // T,),
        out_shape=[jax.ShapeDtypeStruct((S, D_MODEL), BF16), jax.ShapeDtypeStruct((S, POOL_W), BF16),
                   jax.ShapeDtypeStruct((S, POOL_W), F32), jax.ShapeDtypeStruct((S, 512), BF16),
                   jax.ShapeDtypeStruct((HEADS, S, KV_LORA), BF16), jax.ShapeDtypeStruct((S // TQ, 1, HEADS * TQ), F32),
                   jax.ShapeDtypeStruct((8, D_MODEL), F32)],
        in_specs=[row(D_MODEL), row(D_MODEL), _full((N_MOD, D_MODEL)), _full((1024, D_MODEL)), row(POOL_W),
                  _full((4, POOL_GROUP, POOL_GROUP)), _full((1, POOL_W)), _full((HEADS, KV_LORA, 128)), heads],
        out_specs=[row(D_MODEL), row(POOL_W), row(POOL_W), row(512), heads,
                   pl.BlockSpec((T // TQ, 1, HEADS * TQ), lambda i: (i, 0, 0)), _full((8, D_MODEL))],
        compiler_params=_params(("arbitrary",)),
    )(dx1, mix, mod6, w_o, pooled, w_pool, pool_scale, w_uv_t, o_lat)


def _attention_bwd(qc, kc, kct, do, lse_rows, delta_rows, TQ):
    S = kc.shape[0]
    R = HEADS * TQ
    nq = S // TQ

    def body(k_ref, kt_ref, q_ref, do_ref, lser_ref, dr_ref, dk_ref, dqt_ref, dk_s, dv_s):
        j = pl.program_id(0)

        @pl.when(j == 0)
        def _():
            def zero(i, carry):
                dqt_ref[i] = jnp.zeros((QK_PAD, R), F32)
                return carry
            lax.fori_loop(0, nq, zero, 0)

        k = k_ref[...]
        kt = kt_ref[...]
        v = k[:, :KV_LORA]
        dk_s[...] = jnp.zeros((TQ, QK_PAD), F32)
        dv_s[...] = jnp.zeros((TQ, KV_LORA), F32)

        def step(i, masked):
            rows = pl.ds(pl.multiple_of(i * TQ, TQ), TQ)
            q = q_ref[:, rows, :].reshape(R, QK_PAD)
            do = do_ref[:, rows, :].reshape(R, KV_LORA)
            st = _dot_nt(k, q) * SM_SCALE
            if masked:
                st = jnp.where(_diag_mask(TQ, True), st, -jnp.inf)
            pt = jnp.exp(st - lser_ref[i])
            dv_s[...] += _dot(pt, do)
            dpt = _dot_nt(v, do)
            dst = (pt * (dpt - dr_ref[i])).astype(BF16)
            dk_s[...] += _dot(dst, q)
            dqt_ref[i] += _dot(kt, dst)

        def loop(i, carry):
            step(i, False)
            return carry

        step(j, True)
        lax.fori_loop(j + 1, nq, loop, 0)
        dk = dk_s[...] * SM_SCALE
        dk_ref[:, 0:KV_LORA] = dk[:, 0:KV_LORA] + dv_s[...]
        dk_ref[:, KV_LORA:QK_PAD] = dk[:, KV_LORA:QK_PAD]

    return pl.pallas_call(
        body, name="attention_bwd", grid=(nq,),
        out_shape=[jax.ShapeDtypeStruct((S, QK_PAD), F32), jax.ShapeDtypeStruct((nq, QK_PAD, R), F32)],
        in_specs=[pl.BlockSpec((TQ, QK_PAD), lambda j: (j, 0)), pl.BlockSpec((None, QK_PAD, TQ), lambda j: (j, 0, 0)),
                  VMEM_SPEC, VMEM_SPEC, VMEM_SPEC, VMEM_SPEC],
        out_specs=[pl.BlockSpec((TQ, QK_PAD), lambda j: (j, 0)), VMEM_SPEC],
        scratch_shapes=[pltpu.VMEM((TQ, QK_PAD), F32), pltpu.VMEM((TQ, KV_LORA), F32)],
        compiler_params=_params(("arbitrary",)),
    )(kc, kct, qc, do, lse_rows, delta_rows)


def _pre_attention_backward(x, dx1, proj, dqt, dkc, du, cos, sin, mod6, g_mix, g_q, g_kv, w_in, w_uq, w_uk_t, T, TQ):
    S = x.shape[0]

    def body(x_ref, dx1_ref, proj_ref, dqt_ref, dkc_ref, du_ref, cos_ref, sin_ref, mod_ref, gm_ref, gq_ref, gkv_ref,
             win_ref, wuq_ref, wuk_ref, gx_ref, dq_ref, dql_ref, cq_ref, dproj_ref, h1_ref, st_ref):
        @pl.when(pl.program_id(0) == 0)
        def _():
            st_ref[...] = jnp.zeros_like(st_ref)

        cos_t, sin_t = cos_ref[...], sin_ref[...]
        low = lax.broadcasted_iota(jnp.int32, (T, 128), 1) < ROPE
        rope_parts = []
        for h in range(HEADS):
            dqc = jnp.concatenate([jnp.transpose(dqt_ref[s, :, h * TQ:(h + 1) * TQ]) for s in range(T // TQ)], axis=0)
            dqc = dqc * SM_SCALE
            dql = dqc[:, 0:KV_LORA].astype(BF16)
            dql_ref[h] = dql
            dq_ref[:, h * NOPE:(h + 1) * NOPE] = _dot(dql, wuk_ref[h]).astype(BF16)
            rope_parts.append(dqc[:, KV_LORA:QK_PAD])
        for pair in range(2):
            d = jnp.where(low, rope_parts[2 * pair], rope_parts[2 * pair + 1])
            dq_ref[:, O_QA + 128 * pair:O_QA + 128 * (pair + 1)] = _rope_bwd(d, cos_t, sin_t).astype(BF16)
        dcq = _dot_nt(dq_ref[...], wuq_ref[...])
        cqh, rq = _rms(proj_ref[:, 0:Q_LORA])
        cq_ref[...] = (cqh * gq_ref[...]).astype(BF16)
        st_ref[3:4, 0:Q_LORA] += jnp.sum(dcq * cqh, axis=0, keepdims=True)
        dproj_ref[:, 0:Q_LORA] = _rms_bwd(dcq * gq_ref[...], cqh, rq).astype(BF16)
        dckv = dkc_ref[:, 0:KV_LORA]
        ckvh, rkv = _rms(proj_ref[:, O_CKV:O_KR])
        st_ref[4:5, 0:KV_LORA] += jnp.sum(dckv * ckvh, axis=0, keepdims=True)
        dproj_ref[:, O_CKV:O_KR] = _rms_bwd(dckv * gkv_ref[...], ckvh, rkv).astype(BF16)
        dkr = _rope_bwd(dkc_ref[:, KV_LORA:QK_PAD], cos_t, sin_t)
        dkr = jnp.where(low, dkr + pltpu.roll(dkr, ROPE, 1), 0.0)
        dproj_ref[:, O_KR:O_U] = dkr.astype(BF16)
        dproj_ref[:, O_U:PROJ_W] = du_ref[...].astype(BF16)
        dproj = dproj_ref[...]
        dh1 = jnp.concatenate([_dot(dproj, win_ref[j]) for j in range(N_CHIPS)], axis=1)
        xh, r1 = _rms(x_ref[...])
        n1 = xh * gm_ref[...]
        h1_ref[...] = (n1 * (1.0 + mod_ref[1:2, :]) + mod_ref[0:1, :]).astype(BF16)
        st_ref[0:1, :] += jnp.sum(dh1, axis=0, keepdims=True)
        st_ref[1:2, :] += jnp.sum(dh1 * n1, axis=0, keepdims=True)
        dn1 = dh1 * (1.0 + mod_ref[1:2, :])
        st_ref[2:3, :] += jnp.sum(dn1 * xh, axis=0, keepdims=True)
        gx_ref[...] = _rms_bwd(dn1 * gm_ref[...], xh, r1) + dx1_ref[...]

    row = lambda w: pl.BlockSpec((T, w), lambda i: (i, 0))
    return pl.pallas_call(
        body, name="pre_attention_backward", grid=(S // T,),
        out_shape=[jax.ShapeDtypeStruct((S, D_MODEL), F32), jax.ShapeDtypeStruct((S, Q_W), BF16),
                   jax.ShapeDtypeStruct((HEADS, S, KV_LORA), BF16),
                   jax.ShapeDtypeStruct((S, Q_LORA), BF16), jax.ShapeDtypeStruct((S, PROJ_W), BF16),
                   jax.ShapeDtypeStruct((S, D_MODEL), BF16), jax.ShapeDtypeStruct((8, D_MODEL), F32)],
        in_specs=[row(D_MODEL), row(D_MODEL), row(PROJ_W),
                  pl.BlockSpec((T // TQ, QK_PAD, HEADS * TQ), lambda i: (i, 0, 0)),
                  row(QK_PAD), row(POOL_W), row(128), row(128), _full((N_MOD, D_MODEL)), _full((1, D_MODEL)),
                  _full((1, Q_LORA)), _full((1, KV_LORA)), _full((N_CHIPS, PROJ_W, D_MODEL // N_CHIPS)),
                  _full((Q_LORA, Q_W)), _full((HEADS, KV_LORA, NOPE))],
        out_specs=[row(D_MODEL), row(Q_W), pl.BlockSpec((HEADS, T, KV_LORA), lambda i: (0, i, 0)), row(Q_LORA),
                   row(PROJ_W), row(D_MODEL), _full((8, D_MODEL))],
        compiler_params=_params(("arbitrary",)),
    )(x, dx1, proj, dqt, dkc, du, cos, sin, mod6, g_mix, g_q, g_kv, w_in, w_uq, w_uk_t)


def _ada_grads(c_all, dmod_all, chip):
    cols = N_MOD * D_MODEL // N_CHIPS
    width = dmod_all.shape[1]

    def body(col_ref, c_ref, dcol_ref, dall_ref, gw_ref, gb_ref):
        call = c_ref[...]
        act = call * jax.nn.sigmoid(call)
        gw_ref[...] = _dot_tn(act, dcol_ref[...])
        d = dall_ref[...]
        acc = d[0:1, :]
        for b in range(1, 8):
            acc = acc + d[b:b + 1, :]
        gb_ref[...] = acc

    return pl.pallas_call(
        body, name="ada_grads",
        out_shape=[jax.ShapeDtypeStruct((D_MODEL, cols), F32), jax.ShapeDtypeStruct((1, width), F32)],
        grid_spec=pltpu.PrefetchScalarGridSpec(
            num_scalar_prefetch=1, grid=(1,),
            in_specs=[pl.BlockSpec((8, D_MODEL), lambda s, col_ref: (0, 0)),
                      pl.BlockSpec((8, cols), lambda s, col_ref: (0, col_ref[0])),
                      pl.BlockSpec((8, width), lambda s, col_ref: (0, 0))],
            out_specs=[pl.BlockSpec((D_MODEL, cols), lambda s, col_ref: (0, 0)),
                       pl.BlockSpec((1, width), lambda s, col_ref: (0, 0))]),
        compiler_params=_params(("arbitrary",)),
    )(chip, c_all, dmod_all, dmod_all)


def _adamw(w, g, m, v, name):
    rows, cols = w.shape
    T = _row_tile(rows, 256)

    def body(w_ref, g_ref, m_ref, v_ref, d_ref, nm_ref, nv_ref):
        g = g_ref[...]
        m2 = ADAM_B1 * m_ref[...] + (1.0 - ADAM_B1) * g
        v2 = ADAM_B2 * v_ref[...] + (1.0 - ADAM_B2) * (g * g)
        m_hat = m2 / (1.0 - ADAM_B1 ** ADAM_STEP)
        v_hat = v2 / (1.0 - ADAM_B2 ** ADAM_STEP)
        d_ref[...] = -ADAM_LR * (m_hat / (jnp.sqrt(v_hat) + ADAM_EPS) + ADAM_WD * w_ref[...])
        nm_ref[...] = m2
        nv_ref[...] = v2

    spec = pl.BlockSpec((T, cols), lambda i: (i, 0))
    return pl.pallas_call(
        body, name=name, grid=(rows // T,),
        out_shape=[jax.ShapeDtypeStruct((rows, cols), F32)] * 3,
        in_specs=[spec] * 4, out_specs=[spec] * 3,
        compiler_params=_params(("parallel",)),
    )(w, g, m, v)


SMALL_NAMES = ("w_uk", "w_uv", "w_pool", "g_mix", "g_q", "g_kv", "pool_scale", "g_ffn", "g_final", "b_ada")
SMALL_ROWS = 1664


def _pack_rows(parts):
    flat = jnp.concatenate([p.reshape(-1) for p in parts])
    pad = (-flat.shape[0]) % 128
    if pad:
        flat = jnp.concatenate([flat, jnp.zeros((pad,), F32)])
    return flat.reshape(-1, 128)


def kernel(x, c, positions, w_ada, b_ada, g_mix, w_in, g_q, g_kv, w_uq, w_uk, w_uv, w_pool, pool_scale, w_o, g_ffn, w_gate, w_up, w_down, g_final, loss_target, m_w_ada, m_b_ada, m_g_mix, m_w_in, m_g_q, m_g_kv, m_w_uq, m_w_uk, m_w_uv, m_w_pool, m_pool_scale, m_w_o, m_g_ffn, m_w_gate, m_w_up, m_w_down, m_g_final, v_w_ada, v_b_ada, v_g_mix, v_w_in, v_g_q, v_g_kv, v_w_uq, v_w_uk, v_w_uv, v_w_pool, v_pool_scale, v_w_o, v_g_ffn, v_w_gate, v_w_up, v_w_down, v_g_final):
    S = x.shape[1]
    T = _row_tile(S, 512)
    TQ = _row_tile(S, 256)
    TW = _row_tile(S, 1024)
    ix, iy, ic = lax.axis_index("x"), lax.axis_index("y"), lax.axis_index("c")
    chip = (2 * ix + iy).astype(jnp.int32)
    chip_arr = chip.reshape(1)
    core_arr = ic.astype(jnp.int32).reshape(1)

    xs, tgt = x[0], loss_target[0]

    ada_cols = w_ada.shape[2]
    b_cols = lax.dynamic_slice(b_ada, (0, chip * ada_cols), (1, ada_cols))
    mod, c_all = _mod_exchange(c, w_ada[0], b_cols)
    mod6 = mod.reshape(N_MOD, D_MODEL)

    tr = lambda a: jnp.transpose(a[0])
    win_t = tr(w_in)
    win_p = jnp.concatenate([win_t[:O_KR + ROPE], win_t[O_KR:O_KR + ROPE], win_t[O_KR + ROPE:]], axis=0).astype(BF16)
    wuq = w_uq[0]
    wuq_p = jnp.concatenate([wuq[:, h, :NOPE] for h in range(HEADS)] + [wuq[:, h, NOPE:] for h in range(HEADS)],
                            axis=1).astype(BF16)
    first = _weight_gather([win_p, wuq_p])
    w_in_f = first[0]
    w_uq_f = first[1].reshape(Q_LORA, Q_W)
    w_uk_t = jnp.transpose(w_uk[0], (1, 0, 2)).astype(BF16)
    w_uv_t = jnp.transpose(w_uv[0], (1, 0, 2)).astype(BF16)
    w_pool_b = w_pool[0].astype(BF16)
    later = [w_o[0].astype(BF16), tr(w_gate).astype(BF16), tr(w_up).astype(BF16), w_down[0].astype(BF16)]
    wg_lands = _place_shards(chip_arr, later)
    wg_lands, mod6, w_in_f = lax.optimization_barrier((wg_lands, mod6, w_in_f))
    wg_send, wg_recv, wg_lands, token = _split_start(
        "weights_start", later, wg_lands, 3 * len(later), _plan_gather_start)
    mod6 = mod6 + token[0, 0]

    half = ROPE // 2
    freqs = jnp.power(ROPE_THETA, -jnp.arange(half, dtype=F32) / half)
    cos, sin = _rope_tables(positions.reshape(S, 1), jnp.tile(freqs, 4).reshape(1, 128))
    proj, q, qc, kc, kct = _pre_attention(xs, mod6, g_mix, g_q, g_kv, w_in_f, w_uq_f, w_uk_t, cos, sin, T, TQ)
    o_lat, y_mla, lse_rows = _attention_fwd(qc, kc, kct, w_uv_t, TQ)
    wg_send, wg_recv, wg_lands, token = _split_relay(
        "weights_relay", wg_send, wg_recv, later, wg_lands, y_mla, 3 * len(later), _plan_gather_landed,
        _plan_gather_relay)
    pooled = _pool_forward(proj)
    wg_lands = _split_wait("weights_wait", wg_send, wg_recv, [], wg_lands, pooled, _plan_gather_wait)
    w_o_f = wg_lands[0].reshape(1024, D_MODEL)
    w_gate_f, w_up_f, w_down_f = wg_lands[1], wg_lands[2], wg_lands[3]
    x1, mix, mix_in = _mix_out(y_mla, pooled, w_pool_b, pool_scale, w_o_f, xs, mod6, T)
    gate, up, dx2, st_f = _ffn_forward(x1, mod6, g_ffn, g_final.reshape(1, D_MODEL), tgt, w_gate_f, w_up_f, w_down_f, T)

    dgate, dup, act, dff, h2, dx1, st_b = _ffn_backward(dx2, x1, gate, up, mod6, g_ffn, w_gate_f, w_up_f, w_down_f, T)
    steps = S // TW
    chunk_spec = pl.BlockSpec((None, TW, FF_CHUNK), lambda g, i: (g, i, 0))
    wide_spec = pl.BlockSpec((TW, D_MODEL), lambda g, i: (i, 0))
    g_down = _tn_matmul(act, dff, chunk_spec, wide_spec, N_CHIPS, FF_CHUNK, D_MODEL, steps, "grad_w_down")
    g_gate = _tn_matmul(dgate, h2, chunk_spec, wide_spec, N_CHIPS, FF_CHUNK, D_MODEL, steps, "grad_w_gate")
    g_up = _tn_matmul(dup, h2, chunk_spec, wide_spec, N_CHIPS, FF_CHUNK, D_MODEL, steps, "grad_w_up")
    dmix, dz, dpooled, dy_mla, do_lat, delta_rows, st_m = _mix_backward(
        dx1, mix, mod6, w_o_f, pooled, w_pool_b, pool_scale, w_uv_t, o_lat, T, TQ)
    g_o = _tn_matmul(mix_in, dmix, wide_spec, wide_spec, 1, 1024, D_MODEL, steps, "grad_w_o")

    far_names = ("w_gate", "w_up", "w_down", "w_o")
    far_grads = [g_gate, g_up, g_down, g_o.reshape(N_CHIPS, -1, D_MODEL)]
    f_send, f_recv, f_lands, token = _split_start(
        "far_swap_start", far_grads,
        [jax.ShapeDtypeStruct((N_CHIPS, g.shape[1] // 2, g.shape[2]), F32) for g in far_grads], len(far_grads),
        _plan_swap_start)
    col128 = pl.BlockSpec((TW, 128), lambda g, i: (i, g))
    head128 = pl.BlockSpec((None, TW, 128), lambda g, i: (g, i, 0))
    du = _pool_backward(dpooled, token)
    g_pool = _tn_matmul(pooled, dz, col128, col128, 4, POOL_GROUP, POOL_GROUP, steps, "grad_w_pool")
    g_uv_t = _tn_matmul(o_lat, dy_mla, head128, col128, HEADS, KV_LORA, 128, steps, "grad_w_uv")
    f_got = _split_wait("far_swap_wait", f_send, f_recv, far_grads, f_lands, g_uv_t, _plan_swap_wait)
    f_sums = [_add_my_half(core_arr, a, b, "add_half_" + n) for a, b, n in zip(far_grads, f_got, far_names)]
    f_send, f_recv, f_lands, token = _split_start(
        "far_exchange_start", f_sums, [jax.ShapeDtypeStruct((3,) + s.shape[1:], F32) for s in f_sums],
        3 * len(f_sums), _plan_exchange_start)
    delta_rows = delta_rows + token[0, 0]
    dkc, dqt = _attention_bwd(qc, kc, kct, do_lat, lse_rows, delta_rows, TQ)
    grad_x, dq, dql, c_q, dproj, h1, st_p = _pre_attention_backward(
        xs, dx1, proj, dqt, dkc, du, cos, sin, mod6, g_mix, g_q, g_kv, w_in_f, w_uq_f, w_uk_t, T, TQ)
    g_uk_t = _tn_matmul(dql, q, head128, col128, HEADS, KV_LORA, NOPE, steps, "grad_w_uk")
    g_uq_p = _tn_matmul(c_q, dq, pl.BlockSpec((TW, Q_LORA), lambda g, i: (i, 0)),
                        pl.BlockSpec((TW, Q_W), lambda g, i: (i, 0)), 1, Q_LORA, Q_W, steps, "grad_w_uq")
    rows_in = D_MODEL // N_CHIPS
    g_in_p = _tn_matmul(dproj, h1, pl.BlockSpec((TW, PROJ_W), lambda g, i: (i, 0)),
                        pl.BlockSpec((TW, rows_in), lambda g, i: (i, g)), N_CHIPS, PROJ_W, rows_in, steps, "grad_w_in")

    g_in = jnp.concatenate([g_in_p[:, :O_KR + ROPE], g_in_p[:, O_U:]], axis=1)
    uq = g_uq_p[0]
    g_uq = jnp.concatenate([jnp.concatenate([uq[:, h * NOPE:(h + 1) * NOPE], uq[:, O_QA + h * ROPE:O_QA + (h + 1) * ROPE]],
                                            axis=1) for h in range(HEADS)], axis=1).reshape(N_CHIPS, -1, HEADS * HEAD_QK)
    small = _pack_rows([g_uk_t, g_uv_t, g_pool, st_p[2], st_p[3, :Q_LORA], st_p[4, :KV_LORA], st_m[1, :POOL_W],
                        st_b[2], st_f[0]])
    small = jnp.concatenate([small, jnp.zeros((SMALL_ROWS - small.shape[0], 128), F32)]).reshape(N_CHIPS, -1, 128)
    grads = [g_in, g_uq, small]
    dmod = jnp.concatenate([jnp.stack([st_p[0], st_p[1], st_m[0], st_b[0], st_b[1], st_f[1]]).reshape(48, 128),
                            jnp.zeros((8, 128), F32).at[0, 0].set(st_f[2, 0])])

    f_others = _split_wait("far_exchange_wait", f_send, f_recv, f_sums, f_lands, g_in_p, _plan_exchange_wait)
    chip_core = jnp.concatenate([chip_arr, core_arr])
    f_pairs = [_add_chips_into_pair(chip_core, a, b, "add_chips_" + n) for a, b, n in zip(f_sums, f_others, far_names)]
    f_send, f_recv, f_pairs, token = _split_start("far_finish_start", [], f_pairs, len(f_pairs), _plan_finish_start)
    dmod = dmod + token[0, 0]

    names = ("w_in", "w_uq", "small")
    got, dmod_all = _grad_swap_halves(grads, dmod)
    chip_sums = [_add_my_half(core_arr, a, b, "add_half_" + n) for a, b, n in zip(grads, got, names)]
    n_send, n_recv, n_lands, token = _split_start(
        "near_exchange_start", chip_sums, [jax.ShapeDtypeStruct((3,) + s.shape[1:], F32) for s in chip_sums],
        3 * len(chip_sums), _plan_exchange_start)
    f_fulls = _split_wait("far_finish_wait", f_send, f_recv, [], f_pairs, token, _plan_finish_wait)
    gw_gate, gw_up, gw_down, gw_o = [f.reshape(-1, f.shape[2]) for f in f_fulls]
    gw_ada, gb_ada = _ada_grads(c_all, dmod_all.reshape(8, -1), chip_arr)
    loss = gb_ada[0, N_MOD * D_MODEL]
    gb_ada = gb_ada[:, :N_MOD * D_MODEL]

    untr = lambda a: jnp.transpose(a)[None]
    grad_out, delta_out, newm_out, newv_out = {}, {}, {}, {}

    def adam_sharded(n, w, g2, m, v, transposed):
        view = (lambda a: jnp.transpose(a[0])) if transposed else (lambda a: a.reshape(g2.shape))
        back = untr if transposed else (lambda a: a.reshape(w.shape))
        d_, m_, v_ = _adamw(view(w), g2, view(m), view(v), "adamw_" + n)
        grad_out[n], delta_out[n], newm_out[n], newv_out[n] = back(g2), back(d_), back(m_), back(v_)
        return d_

    adam_sharded("w_gate", w_gate, gw_gate, m_w_gate, v_w_gate, True)
    adam_sharded("w_up", w_up, gw_up, m_w_up, v_w_up, True)
    adam_sharded("w_down", w_down, gw_down, m_w_down, v_w_down, False)
    adam_sharded("w_o", w_o, gw_o, m_w_o, v_w_o, False)
    last = adam_sharded("w_ada", w_ada, gw_ada, m_w_ada, v_w_ada, False)

    others = _split_wait("near_exchange_wait", n_send, n_recv, chip_sums, n_lands, last, _plan_exchange_wait)
    halves = [_add_chips(chip_arr, a, b, "add_chips_" + n) for a, b, n in zip(chip_sums, others, names)]
    fulls, small_all = _grad_finish(halves[:2], halves[2])
    gw_in, gw_uq = [f.reshape(-1, f.shape[2]) for f in fulls]
    small_all = small_all.reshape(SMALL_ROWS * 128)
    adam_sharded("w_in", w_in, gw_in, m_w_in, v_w_in, True)
    adam_sharded("w_uq", w_uq, gw_uq, m_w_uq, v_w_uq, False)

    n_sq = KV_LORA * HEADS * 128
    sizes = [n_sq, n_sq, n_sq, D_MODEL, Q_LORA, KV_LORA, POOL_W, D_MODEL, D_MODEL]
    offs = [0]
    for s_ in sizes:
        offs.append(offs[-1] + s_)
    piece = lambda k: small_all[offs[k]:offs[k + 1]]
    grads_small = {
        "w_uk": jnp.transpose(piece(0).reshape(HEADS, KV_LORA, NOPE), (1, 0, 2)),
        "w_uv": jnp.transpose(piece(1).reshape(HEADS, KV_LORA, 128), (1, 0, 2)),
        "w_pool": piece(2).reshape(4, POOL_GROUP, POOL_GROUP),
        "g_mix": piece(3), "g_q": piece(4), "g_kv": piece(5), "pool_scale": piece(6), "g_ffn": piece(7),
        "g_final": piece(8), "b_ada": gb_ada.reshape(-1),
    }
    weights_small = {"w_uk": w_uk, "w_uv": w_uv, "w_pool": w_pool, "g_mix": g_mix, "g_q": g_q, "g_kv": g_kv,
                     "pool_scale": pool_scale, "g_ffn": g_ffn, "g_final": g_final, "b_ada": b_ada}
    m_small = {"w_uk": m_w_uk, "w_uv": m_w_uv, "w_pool": m_w_pool, "g_mix": m_g_mix, "g_q": m_g_q, "g_kv": m_g_kv,
               "pool_scale": m_pool_scale, "g_ffn": m_g_ffn, "g_final": m_g_final, "b_ada": m_b_ada}
    v_small = {"w_uk": v_w_uk, "w_uv": v_w_uv, "w_pool": v_w_pool, "g_mix": v_g_mix, "g_q": v_g_q, "g_kv": v_g_kv,
               "pool_scale": v_pool_scale, "g_ffn": v_g_ffn, "g_final": v_g_final, "b_ada": v_b_ada}
    pack = lambda d: _pack_rows([d[n] for n in SMALL_NAMES])
    d_s, m_s, v_s = _adamw(pack(weights_small), pack(grads_small), pack(m_small), pack(v_small), "adamw_small")

    def unpack(flat2d):
        flat = flat2d.reshape(-1)
        out, o = {}, 0
        for n in SMALL_NAMES:
            size = weights_small[n].size
            out[n] = flat[o:o + size].reshape(weights_small[n].shape)
            o += size
        return out

    delta_s, newm_s, newv_s = unpack(d_s), unpack(m_s), unpack(v_s)

    for n in SMALL_NAMES:
        grad_out[n] = grads_small[n].reshape(weights_small[n].shape)
        delta_out[n], newm_out[n], newv_out[n] = delta_s[n], newm_s[n], newv_s[n]

    order = ("w_ada", "b_ada", "g_mix", "w_in", "g_q", "g_kv", "w_uq", "w_uk", "w_uv", "w_pool", "pool_scale", "w_o",
             "g_ffn", "w_gate", "w_up", "w_down", "g_final")
    return (loss, grad_x.reshape(x.shape), *[grad_out[n] for n in order], *[delta_out[n] for n in order],
            *[newm_out[n] for n in order], *[newv_out[n] for n in order])
```

```python
import functools

import jax
import jax.numpy as jnp
from jax import lax
from jax.experimental import pallas as pl
from jax.experimental.pallas import tpu as pltpu

F32 = jnp.float32
BF16 = jnp.bfloat16

D_MODEL = 1024
HEADS = 4
NOPE = 128
ROPE = 64
HEAD_QK = NOPE + ROPE
Q_LORA = 256
KV_LORA = 128
POOL_W = 512
POOL_WINDOWS = (2, 4, 8, 16)
POOL_GROUP = 128
POOL_PAD = 16
D_FF = 2816
N_CHIPS = 4
FF_CHUNK = D_FF // N_CHIPS
N_MOD = 6
EPS = 1e-6
SM_SCALE = HEAD_QK ** -0.5
ROPE_THETA = 10000.0
QK_PAD = 256
CHUNK = 64
CHUNK_SHIFT = 6

ADAM_LR = 0.001
ADAM_B1 = 0.9
ADAM_B2 = 0.999
ADAM_EPS = 1e-08
ADAM_WD = 0.01
ADAM_STEP = 10

VMEM_LIMIT = 48 * 1024 * 1024
MESH = pl.DeviceIdType.MESH
ANY = pl.BlockSpec(memory_space=pl.ANY)
VMEM_SPEC = pl.BlockSpec(memory_space=pltpu.VMEM)

PROJ_W = 1024
O_CKV = 256
O_KR = 384
O_U = 512
Q_W = 768
O_QA = 512
O_QB = 640


def _params(sem=None, vmem=VMEM_LIMIT):
    kw = dict(vmem_limit_bytes=vmem)
    if sem is not None:
        kw["dimension_semantics"] = sem
    return pltpu.CompilerParams(**kw)


def _dot(a, b):
    return jnp.dot(a.astype(BF16), b.astype(BF16), preferred_element_type=F32)


def _dot_nt(a, b):
    return lax.dot_general(a.astype(BF16), b.astype(BF16), (((1,), (1,)), ((), ())), preferred_element_type=F32)


def _dot_tn(a, b):
    return lax.dot_general(a.astype(BF16), b.astype(BF16), (((0,), (0,)), ((), ())), preferred_element_type=F32)


def _row_tile(rows, target):
    best = rows
    for t in range(8, min(rows, target) + 1, 8):
        if rows % t == 0:
            best = t
    return best if rows % best == 0 and best <= target else rows


def _rms(x):
    r = lax.rsqrt(jnp.mean(x * x, axis=-1, keepdims=True) + EPS)
    return x * r, r


def _rms_bwd(dxh, xh, r):
    return r * (dxh - xh * jnp.mean(dxh * xh, axis=-1, keepdims=True))


def _lane_first_half(shape):
    lane = lax.broadcasted_iota(jnp.int32, shape, 1)
    return (lane & (ROPE - 1)) < (ROPE // 2)


def _rope(a, cos, sin):
    first = _lane_first_half(a.shape)
    up = pltpu.roll(a, 96, 1)
    dn = pltpu.roll(a, 32, 1)
    return a * cos + jnp.where(first, -up, dn) * sin


def _rope_bwd(d, cos, sin):
    first = _lane_first_half(d.shape)
    up = pltpu.roll(d, 96, 1)
    dn = pltpu.roll(d, 32, 1)
    return d * cos + jnp.where(first, up, -dn) * sin


RELATIONS = tuple((dx, dy, dc) for dx in (0, 1) for dy in (0, 1) for dc in (0, 1) if (dx, dy, dc) != (0, 0, 0))
CHIP_RELATIONS = ((1, 0), (0, 1), (1, 1))


def _flip(v, d):
    return 1 - v if d else v


def _place():
    return lax.axis_index("x"), lax.axis_index("y"), lax.axis_index("c")


def _remote(src, dst, send_sem, recv_sem, target):
    return pltpu.make_async_remote_copy(src_ref=src, dst_ref=dst, send_sem=send_sem, recv_sem=recv_sem,
                                        device_id=target, device_id_type=MESH)


def _mod_exchange(c_row, w_ada, b_ada):
    cols = w_ada.shape[1]

    def body(c_ref, w_ref, b_ref, mod_ref, call_ref, part_ref, send1, recv1, loc1, send2, recv2, loc2):
        x, y, c = _place()
        me = 4 * x + 2 * y + c
        own = pltpu.make_async_copy(c_ref, call_ref.at[pl.ds(me, 1)], loc1)
        own.start()
        sends = []
        for k, (dx, dy, dc) in enumerate(RELATIONS):
            cp = _remote(c_ref, call_ref.at[pl.ds(me, 1)], send1.at[k], recv1.at[k],
                         (_flip(x, dx), _flip(y, dy), _flip(c, dc)))
            cp.start()
            sends.append(cp)
        for k, (dx, dy, dc) in enumerate(RELATIONS):
            src = 4 * _flip(x, dx) + 2 * _flip(y, dy) + _flip(c, dc)
            _remote(c_ref, call_ref.at[pl.ds(src, 1)], send1.at[k], recv1.at[k], (x, y, c)).wait_recv()
        own.wait()
        for cp in sends:
            cp.wait_send()
        call = call_ref[...]
        act = call * jax.nn.sigmoid(call)
        part_ref[...] = _dot(act, w_ref[...]) + b_ref[...]
        chip = 2 * x + y
        mine = pltpu.make_async_copy(part_ref.at[pl.ds(me, 1)], mod_ref.at[pl.ds(chip, 1)], loc2)
        mine.start()
        sends = []
        for k, (dx, dy) in enumerate(CHIP_RELATIONS):
            tx, ty = _flip(x, dx), _flip(y, dy)
            tb = 4 * tx + 2 * ty + c
            cp = _remote(part_ref.at[pl.ds(tb, 1)], mod_ref.at[pl.ds(chip, 1)], send2.at[k], recv2.at[k], (tx, ty, c))
            cp.start()
            sends.append(cp)
        for k, (dx, dy) in enumerate(CHIP_RELATIONS):
            src_chip = 2 * _flip(x, dx) + _flip(y, dy)
            _remote(part_ref.at[pl.ds(me, 1)], mod_ref.at[pl.ds(src_chip, 1)], send2.at[k], recv2.at[k],
                    (x, y, c)).wait_recv()
        mine.wait()
        for cp in sends:
            cp.wait_send()

    return pl.pallas_call(
        body, name="mod_exchange",
        out_shape=[jax.ShapeDtypeStruct((N_CHIPS, cols), F32), jax.ShapeDtypeStruct((8, D_MODEL), F32)],
        in_specs=[VMEM_SPEC, VMEM_SPEC, VMEM_SPEC], out_specs=[VMEM_SPEC, VMEM_SPEC],
        scratch_shapes=[pltpu.VMEM((8, cols), F32),
                        pltpu.SemaphoreType.DMA((7,)), pltpu.SemaphoreType.DMA((7,)), pltpu.SemaphoreType.DMA,
                        pltpu.SemaphoreType.DMA((3,)), pltpu.SemaphoreType.DMA((3,)), pltpu.SemaphoreType.DMA],
        compiler_params=_params(),
    )(c_row, w_ada, b_ada)


def _weight_gather(shards):
    n = len(shards)

    def body(*refs):
        ins, outs = refs[:n], refs[n:2 * n]
        send_sems, recv_sems, loc_sems = refs[2 * n:]
        x, y, c = _place()
        chip = 2 * x + y
        local = []
        for w in range(n):
            cp = pltpu.make_async_copy(ins[w], outs[w].at[chip], loc_sems.at[w])
            cp.start()
            local.append(cp)
        sends = []
        for w in range(n):
            hr = ins[w].shape[0] // 2
            half = pl.ds(c * hr, hr)
            for k, (dx, dy) in enumerate(CHIP_RELATIONS):
                cp = _remote(ins[w].at[half], outs[w].at[chip, half], send_sems.at[w, k], recv_sems.at[w, k],
                             (_flip(x, dx), _flip(y, dy), c))
                cp.start()
                sends.append(cp)
        for w in range(n):
            hr = ins[w].shape[0] // 2
            half = pl.ds(c * hr, hr)
            for k, (dx, dy) in enumerate(CHIP_RELATIONS):
                src_chip = 2 * _flip(x, dx) + _flip(y, dy)
                got = outs[w].at[src_chip, half]
                _remote(got, got, send_sems.at[w, k], recv_sems.at[w, k], (x, y, c)).wait_recv()
                cp = _remote(got, got, send_sems.at[w, 3 + k], recv_sems.at[w, 3 + k], (x, y, 1 - c))
                cp.start()
                sends.append(cp)
        for w in range(n):
            hr = ins[w].shape[0] // 2
            other = pl.ds((1 - c) * hr, hr)
            for k, (dx, dy) in enumerate(CHIP_RELATIONS):
                src_chip = 2 * _flip(x, dx) + _flip(y, dy)
                got = outs[w].at[src_chip, other]
                _remote(got, got, send_sems.at[w, 3 + k], recv_sems.at[w, 3 + k], (x, y, c)).wait_recv()
        for cp in sends:
            cp.wait_send()
        for cp in local:
            cp.wait()

    return pl.pallas_call(
        body, name="weight_gather",
        out_shape=[jax.ShapeDtypeStruct((N_CHIPS,) + s.shape, s.dtype) for s in shards],
        in_specs=[VMEM_SPEC] * n, out_specs=[ANY] * n,
        scratch_shapes=[pltpu.SemaphoreType.DMA((n, 6)), pltpu.SemaphoreType.DMA((n, 6)),
                        pltpu.SemaphoreType.DMA((n,))],
        compiler_params=_params(),
    )(*shards)


HBM_SPEC = pl.BlockSpec(memory_space=pltpu.HBM)
SEM_SPEC = pl.BlockSpec(memory_space=pltpu.SEMAPHORE)
DATAFLOW = pltpu.SideEffectType.DATAFLOW_SIDE_EFFECTING


def _in_hbm(a):
    return pltpu.with_memory_space_constraint(a, pltpu.HBM)


def _hbm_like(arrays):
    return [pltpu.HBM(a.shape, a.dtype) for a in arrays]


def _split_start(name, srcs, lands, n_remote, plan):
    lands = [lax.empty(a.shape, a.dtype) if isinstance(a, jax.ShapeDtypeStruct) else a for a in lands]
    n, m = len(srcs), len(lands)

    def body(*refs):
        src_refs, land_refs = refs[:n], refs[n:n + m]
        send_sems, recv_sems, token = refs[n + m], refs[n + m + 1], refs[n + 2 * m + 2]
        remote = plan(_place(), src_refs, land_refs)
        assert len(remote) == n_remote
        for i, (s, d, target) in enumerate(remote):
            _remote(s, d, send_sems.at[i], recv_sems.at[i], target).start()
        token[...] = jnp.zeros_like(token)

    res = pl.pallas_call(
        body, name=name,
        out_shape=(pltpu.SemaphoreType.DMA((n_remote,)), pltpu.SemaphoreType.DMA((n_remote,)),
                   *_hbm_like(lands), jax.ShapeDtypeStruct((8, 128), F32)),
        in_specs=[HBM_SPEC] * (n + m),
        out_specs=(SEM_SPEC, SEM_SPEC, *([HBM_SPEC] * m), VMEM_SPEC),
        input_output_aliases={n + i: 2 + i for i in range(m)},
        compiler_params=pltpu.CompilerParams(has_side_effects=DATAFLOW),
    )(*[_in_hbm(a) for a in srcs], *[_in_hbm(a) for a in lands])
    return res[0], res[1], list(res[2:2 + m]), res[2 + m]


def _split_wait(name, send_sems, recv_sems, srcs, lands, after, plan):
    n, m = len(srcs), len(lands)

    def body(*refs):
        src_refs, land_refs = refs[:n], refs[n:n + m]
        send_sems, recv_sems = refs[n + m], refs[n + m + 1]
        place = _place()
        for i, (s, d) in enumerate(plan(place, src_refs, land_refs)):
            cp = _remote(s, d, send_sems.at[i], recv_sems.at[i], place)
            cp.wait_send()
            cp.wait_recv()

    res = pl.pallas_call(
        body, name=name,
        out_shape=tuple(_hbm_like(lands)),
        in_specs=[HBM_SPEC] * (n + m) + [SEM_SPEC, SEM_SPEC, ANY],
        out_specs=tuple([HBM_SPEC] * m),
        input_output_aliases={n + i: i for i in range(m)},
        compiler_params=pltpu.CompilerParams(has_side_effects=DATAFLOW),
    )(*srcs, *lands, send_sems, recv_sems, after)
    return list(res)


def _split_relay(name, send_sems, recv_sems, srcs, lands, after, n_remote, plan_wait, plan_send):
    n, m = len(srcs), len(lands)

    def body(*refs):
        src_refs, land_refs = refs[:n], refs[n:n + m]
        old_send, old_recv = refs[n + m], refs[n + m + 1]
        new_send, new_recv = refs[n + m + 3], refs[n + m + 4]
        token = refs[n + m + 5 + m]
        place = _place()
        for i, (s, d) in enumerate(plan_wait(place, src_refs, land_refs)):
            cp = _remote(s, d, old_send.at[i], old_recv.at[i], place)
            cp.wait_send()
            cp.wait_recv()
        for i, (s, d, target) in enumerate(plan_send(place, land_refs)):
            _remote(s, d, new_send.at[i], new_recv.at[i], target).start()
        token[...] = jnp.zeros_like(token)

    res = pl.pallas_call(
        body, name=name,
        out_shape=(pltpu.SemaphoreType.DMA((n_remote,)), pltpu.SemaphoreType.DMA((n_remote,)),
                   *_hbm_like(lands), jax.ShapeDtypeStruct((8, 128), F32)),
        in_specs=[HBM_SPEC] * (n + m) + [SEM_SPEC, SEM_SPEC, ANY],
        out_specs=(SEM_SPEC, SEM_SPEC, *([HBM_SPEC] * m), VMEM_SPEC),
        input_output_aliases={n + i: 2 + i for i in range(m)},
        compiler_params=pltpu.CompilerParams(has_side_effects=DATAFLOW),
    )(*srcs, *lands, send_sems, recv_sems, after)
    return res[0], res[1], list(res[2:2 + m]), res[2 + m]


def _half(ref, core, axis=0):
    hr = ref.shape[axis] // 2
    return pl.ds(core * hr, hr)


def _plan_gather_start(place, src, land):
    x, y, c = place
    chip = 2 * x + y
    return [(s.at[_half(s, c)], l.at[chip, _half(s, c)], (_flip(x, dx), _flip(y, dy), c))
            for s, l in zip(src, land) for dx, dy in CHIP_RELATIONS]


def _plan_gather_landed(place, src, land):
    x, y, c = place
    return [(s.at[_half(s, c)], l.at[2 * _flip(x, dx) + _flip(y, dy), _half(s, c)])
            for s, l in zip(src, land) for dx, dy in CHIP_RELATIONS]


def _plan_gather_relay(place, land):
    x, y, c = place
    out = []
    for l in land:
        for dx, dy in CHIP_RELATIONS:
            got = l.at[2 * _flip(x, dx) + _flip(y, dy), _half(l, c, 1)]
            out.append((got, got, (x, y, 1 - c)))
    return out


def _plan_gather_wait(place, src, land):
    x, y, c = place
    out = []
    for l in land:
        for dx, dy in CHIP_RELATIONS:
            got = l.at[2 * _flip(x, dx) + _flip(y, dy), _half(l, 1 - c, 1)]
            out.append((got, got))
    return out


def _plan_swap_start(place, src, land):
    x, y, c = place
    return [(s.at[:, _half(s, 1 - c, 1), :], l, (x, y, 1 - c)) for s, l in zip(src, land)]


def _plan_swap_wait(place, src, land):
    return [(s.at[:, _half(s, 0, 1), :], l) for s, l in zip(src, land)]


def _plan_exchange_start(place, src, land):
    x, y, c = place
    remote = []
    for s, l in zip(src, land):
        for k, (dx, dy) in enumerate(CHIP_RELATIONS):
            tx, ty = _flip(x, dx), _flip(y, dy)
            remote.append((s.at[2 * tx + ty], l.at[k], (tx, ty, c)))
    return remote


def _plan_exchange_wait(place, src, land):
    return [(s.at[0], l.at[k]) for s, l in zip(src, land) for k in range(3)]


def _plan_finish_start(place, src, land):
    x, y, c = place
    return [(l.at[c], l.at[c], (x, y, 1 - c)) for l in land]


def _plan_finish_wait(place, src, land):
    x, y, c = place
    return [(l.at[c], l.at[1 - c]) for l in land]


def _grad_swap_halves(grads, dmod):
    n = len(grads)

    def body(*refs):
        ins, dmod_ref = refs[:n], refs[n]
        outs, dall_ref = refs[n + 1:2 * n + 1], refs[2 * n + 1]
        send_sems, recv_sems, dsend, drecv, dloc = refs[2 * n + 2:]
        x, y, c = _place()
        me = 4 * x + 2 * y + c
        sends = []
        for w in range(n):
            hr = ins[w].shape[1] // 2
            cp = _remote(ins[w].at[:, pl.ds((1 - c) * hr, hr), :], outs[w], send_sems.at[w], recv_sems.at[w],
                         (x, y, 1 - c))
            cp.start()
            sends.append(cp)
        own = pltpu.make_async_copy(dmod_ref, dall_ref.at[me], dloc)
        own.start()
        for k, (dx, dy, dc) in enumerate(RELATIONS):
            cp = _remote(dmod_ref, dall_ref.at[me], dsend.at[k], drecv.at[k],
                         (_flip(x, dx), _flip(y, dy), _flip(c, dc)))
            cp.start()
            sends.append(cp)
        for k, (dx, dy, dc) in enumerate(RELATIONS):
            src = 4 * _flip(x, dx) + 2 * _flip(y, dy) + _flip(c, dc)
            _remote(dmod_ref, dall_ref.at[src], dsend.at[k], drecv.at[k], (x, y, c)).wait_recv()
        for w in range(n):
            _remote(outs[w], outs[w], send_sems.at[w], recv_sems.at[w], (x, y, c)).wait_recv()
        own.wait()
        for cp in sends:
            cp.wait_send()

    out_shape = [jax.ShapeDtypeStruct((N_CHIPS, g.shape[1] // 2, g.shape[2]), F32) for g in grads]
    out_shape.append(jax.ShapeDtypeStruct((8,) + dmod.shape, F32))
    res = pl.pallas_call(
        body, name="grad_swap_halves",
        out_shape=out_shape, in_specs=[ANY] * n + [VMEM_SPEC], out_specs=[ANY] * (n + 1),
        scratch_shapes=[pltpu.SemaphoreType.DMA((n,)), pltpu.SemaphoreType.DMA((n,)),
                        pltpu.SemaphoreType.DMA((7,)), pltpu.SemaphoreType.DMA((7,)), pltpu.SemaphoreType.DMA],
        compiler_params=_params(),
    )(*grads, dmod)
    return res[:n], res[n]


def _grad_finish(halves, small_half):
    n = len(halves)

    def body(*refs):
        ins, sm_ref = refs[:n], refs[n]
        outs, sall_ref = refs[n + 1:2 * n + 1], refs[2 * n + 1]
        send_sems, recv_sems, loc_sems, ssend, srecv, sloc = refs[2 * n + 2:]
        x, y, c = _place()
        chip = 2 * x + y
        local, sends = [], []
        for w in range(n):
            cp = pltpu.make_async_copy(ins[w], outs[w].at[c], loc_sems.at[w])
            cp.start()
            local.append(cp)
            cp = _remote(ins[w], outs[w].at[c], send_sems.at[w], recv_sems.at[w], (x, y, 1 - c))
            cp.start()
            sends.append(cp)
        cp = pltpu.make_async_copy(sm_ref, sall_ref.at[chip, c], sloc)
        cp.start()
        local.append(cp)
        for k, (dx, dy, dc) in enumerate(RELATIONS):
            cp = _remote(sm_ref, sall_ref.at[chip, c], ssend.at[k], srecv.at[k],
                         (_flip(x, dx), _flip(y, dy), _flip(c, dc)))
            cp.start()
            sends.append(cp)
        for k, (dx, dy, dc) in enumerate(RELATIONS):
            got = sall_ref.at[2 * _flip(x, dx) + _flip(y, dy), _flip(c, dc)]
            _remote(got, got, ssend.at[k], srecv.at[k], (x, y, c)).wait_recv()
        for w in range(n):
            got = outs[w].at[1 - c]
            _remote(got, got, send_sems.at[w], recv_sems.at[w], (x, y, c)).wait_recv()
        for cp in sends:
            cp.wait_send()
        for cp in local:
            cp.wait()

    out_shape = [jax.ShapeDtypeStruct((2,) + h.shape, F32) for h in halves]
    out_shape.append(jax.ShapeDtypeStruct((N_CHIPS, 2) + small_half.shape, F32))
    res = pl.pallas_call(
        body, name="grad_finish",
        out_shape=out_shape, in_specs=[VMEM_SPEC] * (n + 1), out_specs=[ANY] * (n + 1),
        scratch_shapes=[pltpu.SemaphoreType.DMA((n,)), pltpu.SemaphoreType.DMA((n,)), pltpu.SemaphoreType.DMA((n,)),
                        pltpu.SemaphoreType.DMA((7,)), pltpu.SemaphoreType.DMA((7,)), pltpu.SemaphoreType.DMA],
        compiler_params=_params(),
    )(*halves, small_half)
    return res[:n], res[n]


def _add_my_half(core, full, got, name):
    _, hr, cols = got.shape

    def body(core_ref, a_ref, b_ref, o_ref):
        o_ref[...] = a_ref[...] + b_ref[...]

    return pl.pallas_call(
        body, name=name,
        out_shape=jax.ShapeDtypeStruct(got.shape, F32),
        grid_spec=pltpu.PrefetchScalarGridSpec(
            num_scalar_prefetch=1, grid=(N_CHIPS,),
            in_specs=[pl.BlockSpec((None, hr, cols), lambda s, core_ref: (s, core_ref[0], 0)),
                      pl.BlockSpec((None, hr, cols), lambda s, core_ref: (s, 0, 0))],
            out_specs=pl.BlockSpec((None, hr, cols), lambda s, core_ref: (s, 0, 0))),
        compiler_params=_params(("arbitrary",)),
    )(core, full, got)


def _add_chips(chip, mine, got, name):
    _, hr, cols = mine.shape

    def body(chip_ref, a_ref, b_ref, o_ref):
        o_ref[...] = ((a_ref[...] + b_ref[0]) + b_ref[1]) + b_ref[2]

    return pl.pallas_call(
        body, name=name,
        out_shape=jax.ShapeDtypeStruct((hr, cols), F32),
        grid_spec=pltpu.PrefetchScalarGridSpec(
            num_scalar_prefetch=1, grid=(1,),
            in_specs=[pl.BlockSpec((None, hr, cols), lambda s, chip_ref: (chip_ref[0], 0, 0)),
                      pl.BlockSpec((3, hr, cols), lambda s, chip_ref: (0, 0, 0))],
            out_specs=pl.BlockSpec((hr, cols), lambda s, chip_ref: (0, 0))),
        compiler_params=_params(("arbitrary",)),
    )(chip, mine, got)


def _add_chips_into_pair(chip_core, mine, got, name):
    _, hr, cols = mine.shape

    def body(cc_ref, a_ref, b_ref, o_ref):
        o_ref[...] = ((a_ref[...] + b_ref[0]) + b_ref[1]) + b_ref[2]

    return pl.pallas_call(
        body, name=name,
        out_shape=jax.ShapeDtypeStruct((2, hr, cols), F32),
        grid_spec=pltpu.PrefetchScalarGridSpec(
            num_scalar_prefetch=1, grid=(1,),
            in_specs=[pl.BlockSpec((None, hr, cols), lambda s, cc_ref: (cc_ref[0], 0, 0)),
                      pl.BlockSpec((3, hr, cols), lambda s, cc_ref: (0, 0, 0))],
            out_specs=pl.BlockSpec((None, hr, cols), lambda s, cc_ref: (cc_ref[1], 0, 0))),
        compiler_params=_params(("arbitrary",)),
    )(chip_core, mine, got)


def _place_shards(chip, shards):
    n = len(shards)

    def body(chip_ref, *refs):
        for w in range(n):
            refs[n + w][...] = refs[w][...]

    return pl.pallas_call(
        body, name="place_shards",
        out_shape=[jax.ShapeDtypeStruct((N_CHIPS,) + s.shape, s.dtype) for s in shards],
        grid_spec=pltpu.PrefetchScalarGridSpec(
            num_scalar_prefetch=1, grid=(1,),
            in_specs=[pl.BlockSpec(s.shape, lambda i, chip_ref: (0, 0)) for s in shards],
            out_specs=[pl.BlockSpec((None,) + s.shape, lambda i, chip_ref: (chip_ref[0], 0, 0)) for s in shards]),
        compiler_params=_params(("arbitrary",)),
    )(chip, *shards)


def _rope_tables(pos_col, freqs):
    S = pos_col.shape[0]
    T = _row_tile(S, 1024)

    def body(p_ref, f_ref, cos_ref, sin_ref):
        ang = p_ref[...].astype(F32) * f_ref[...]
        cos_ref[...] = jnp.cos(ang)
        sin_ref[...] = jnp.sin(ang)

    return pl.pallas_call(
        body, name="rope_tables", grid=(S // T,),
        out_shape=[jax.ShapeDtypeStruct((S, 128), F32)] * 2,
        in_specs=[pl.BlockSpec((T, 1), lambda i: (i, 0)), pl.BlockSpec((1, 128), lambda i: (0, 0))],
        out_specs=[pl.BlockSpec((T, 128), lambda i: (i, 0))] * 2,
        compiler_params=_params(("parallel",)),
    )(pos_col, freqs)


def _full(shape):
    zeros = (0,) * len(shape)
    return pl.BlockSpec(shape, lambda *_: zeros)


def _pre_attention(x, mod6, g_mix, g_q, g_kv, w_in, w_uq, w_uk_t, cos, sin, T, TQ):
    S = x.shape[0]

    def body(x_ref, mod_ref, gm_ref, gq_ref, gkv_ref, win_ref, wuq_ref, wuk_ref, cos_ref, sin_ref,
             proj_ref, q_ref, qc_ref, kc_ref, kct_ref):
        xh, _ = _rms(x_ref[...])
        h1 = ((xh * gm_ref[...]) * (1.0 + mod_ref[1:2, :]) + mod_ref[0:1, :]).astype(BF16)
        rows_in = D_MODEL // N_CHIPS
        proj = _dot_nt(h1[:, 0:rows_in], win_ref[0])
        for j in range(1, N_CHIPS):
            proj = proj + _dot_nt(h1[:, j * rows_in:(j + 1) * rows_in], win_ref[j])
        proj_ref[...] = proj
        cqh, _ = _rms(proj[:, :Q_LORA])
        c_q = cqh * gq_ref[...]
        ckvh, _ = _rms(proj[:, O_CKV:O_KR])
        c_kv = ckvh * gkv_ref[...]
        q = _dot(c_q, wuq_ref[...])
        q_ref[...] = q
        cos_t, sin_t = cos_ref[...], sin_ref[...]
        ropes = (_rope(q[:, O_QA:O_QB], cos_t, sin_t), _rope(q[:, O_QB:Q_W], cos_t, sin_t))
        low = lax.broadcasted_iota(jnp.int32, (T, 128), 1) < ROPE
        for h in range(HEADS):
            q_lat = _dot_nt(q[:, h * NOPE:(h + 1) * NOPE], wuk_ref[h])
            keep = low if h % 2 == 0 else jnp.logical_not(low)
            qc_ref[h, :, 0:KV_LORA] = q_lat.astype(BF16)
            qc_ref[h, :, KV_LORA:QK_PAD] = jnp.where(keep, ropes[h // 2], 0.0).astype(BF16)
        k_rope = _rope(proj[:, O_KR:O_U], cos_t, sin_t)
        kc_ref[:, 0:KV_LORA] = c_kv.astype(BF16)
        kc_ref[:, KV_LORA:QK_PAD] = k_rope.astype(BF16)
        lat_t, rope_t = jnp.transpose(c_kv), jnp.transpose(k_rope)
        for s in range(T // TQ):
            kct_ref[s, 0:KV_LORA, :] = lat_t[:, s * TQ:(s + 1) * TQ].astype(BF16)
            kct_ref[s, KV_LORA:QK_PAD, :] = rope_t[:, s * TQ:(s + 1) * TQ].astype(BF16)

    row = lambda w: pl.BlockSpec((T, w), lambda i: (i, 0))
    return pl.pallas_call(
        body, name="pre_attention", grid=(S // T,),
        out_shape=[jax.ShapeDtypeStruct((S, PROJ_W), F32), jax.ShapeDtypeStruct((S, Q_W), F32),
                   jax.ShapeDtypeStruct((HEADS, S, QK_PAD), BF16), jax.ShapeDtypeStruct((S, QK_PAD), BF16),
                   jax.ShapeDtypeStruct((S // TQ, QK_PAD, TQ), BF16)],
        in_specs=[row(D_MODEL), _full((N_MOD, D_MODEL)), _full((1, D_MODEL)), _full((1, Q_LORA)), _full((1, KV_LORA)),
                  _full((N_CHIPS, PROJ_W, D_MODEL // N_CHIPS)), _full((Q_LORA, Q_W)), _full((HEADS, KV_LORA, NOPE)),
                  row(128), row(128)],
        out_specs=[row(PROJ_W), row(Q_W), pl.BlockSpec((HEADS, T, QK_PAD), lambda i: (0, i, 0)), row(QK_PAD),
                   pl.BlockSpec((T // TQ, QK_PAD, TQ), lambda i: (i, 0, 0))],
        compiler_params=_params(("parallel",)),
    )(x, mod6, g_mix, g_q, g_kv, w_in, w_uq, w_uk_t, cos, sin)


def _diag_mask(TQ, transposed):
    R = HEADS * TQ
    if transposed:
        key = lax.broadcasted_iota(jnp.int32, (TQ, R), 0) >> CHUNK_SHIFT
        qry = (lax.broadcasted_iota(jnp.int32, (TQ, R), 1) & (TQ - 1)) >> CHUNK_SHIFT
    else:
        qry = (lax.broadcasted_iota(jnp.int32, (R, TQ), 0) & (TQ - 1)) >> CHUNK_SHIFT
        key = lax.broadcasted_iota(jnp.int32, (R, TQ), 1) >> CHUNK_SHIFT
    return key <= qry


def _col_to_row(col):
    return jnp.transpose(jnp.broadcast_to(col, (col.shape[0], 128)))[0:1, :]


def _attention_fwd(qc, kc, kct, w_uv_t, TQ):
    S = kc.shape[0]
    R = HEADS * TQ
    nq = S // TQ

    def body(q_ref, k_ref, kt_ref, wuv_ref, o_ref, y_ref, lser_ref, m_s, l_s, acc_s):
        i = pl.program_id(0)
        q = q_ref[...].reshape(R, QK_PAD)
        m_s[...] = jnp.full((1, R), -jnp.inf, F32)
        l_s[...] = jnp.zeros((1, R), F32)
        acc_s[...] = jnp.zeros((KV_LORA, R), F32)

        def step(j, masked):
            k = k_ref[pl.ds(pl.multiple_of(j * TQ, TQ), TQ), :]
            st = _dot_nt(k, q) * SM_SCALE
            if masked:
                st = jnp.where(_diag_mask(TQ, True), st, -jnp.inf)
            m_old = m_s[...]
            m_new = jnp.maximum(m_old, jnp.max(st, axis=0, keepdims=True))
            pt = jnp.exp(st - m_new)
            alpha = jnp.exp(m_old - m_new)
            l_s[...] = alpha * l_s[...] + jnp.sum(pt, axis=0, keepdims=True)
            acc_s[...] = alpha * acc_s[...] + _dot(kt_ref[j, 0:KV_LORA, :], pt)
            m_s[...] = m_new

        def loop(j, carry):
            step(j, False)
            return carry

        lax.fori_loop(0, i, loop, 0)
        step(i, True)
        l = l_s[...]
        lser_ref[0] = m_s[...] + jnp.log(l)
        o = jnp.transpose(acc_s[...] / l).astype(BF16)
        for h in range(HEADS):
            oh = o[h * TQ:(h + 1) * TQ, :]
            o_ref[h] = oh
            y_ref[:, h * 128:(h + 1) * 128] = _dot(oh, wuv_ref[h]).astype(BF16)

    return pl.pallas_call(
        body, name="attention_fwd", grid=(nq,),
        out_shape=[jax.ShapeDtypeStruct((HEADS, S, KV_LORA), BF16), jax.ShapeDtypeStruct((S, HEADS * 128), BF16),
                   jax.ShapeDtypeStruct((nq, 1, R), F32)],
        in_specs=[pl.BlockSpec((HEADS, TQ, QK_PAD), lambda i: (0, i, 0)), _full((S, QK_PAD)),
                  _full((nq, QK_PAD, TQ)), _full((HEADS, KV_LORA, 128))],
        out_specs=[pl.BlockSpec((HEADS, TQ, KV_LORA), lambda i: (0, i, 0)), pl.BlockSpec((TQ, HEADS * 128), lambda i: (i, 0)),
                   pl.BlockSpec((1, 1, R), lambda i: (i, 0, 0))],
        scratch_shapes=[pltpu.VMEM((1, R), F32), pltpu.VMEM((1, R), F32), pltpu.VMEM((KV_LORA, R), F32)],
        compiler_params=_params(("parallel",)),
    )(qc, kc, kct, w_uv_t)


def _pool_forward(proj):
    S = proj.shape[0]
    RB = _row_tile(S, 256)

    def body(proj_ref, out_ref, pad_ref, sem):
        cp = pltpu.make_async_copy(proj_ref.at[:, pl.ds(O_U, POOL_W)], pad_ref.at[pl.ds(POOL_PAD, S)], sem)
        cp.start()
        pad_ref[0:POOL_PAD, :] = jnp.zeros((POOL_PAD, POOL_W), F32)
        cp.wait()
        for g, win in enumerate(POOL_WINDOWS):
            cols = slice(g * POOL_GROUP, (g + 1) * POOL_GROUP)
            for r0 in range(0, S, RB):
                u = pad_ref[POOL_PAD + r0:POOL_PAD + r0 + RB, cols]
                acc = u
                for k in range(1, win):
                    acc = acc + pad_ref[POOL_PAD + r0 - k:POOL_PAD + r0 - k + RB, cols]
                if r0 == 0:
                    t1 = (lax.broadcasted_iota(jnp.int32, (RB, POOL_GROUP), 0) + 1).astype(F32)
                    mean = acc / jnp.minimum(t1, float(win))
                else:
                    mean = acc * (1.0 / win)
                out_ref[r0:r0 + RB, cols] = (mean - u).astype(BF16)

    return pl.pallas_call(
        body, name="pool_forward",
        out_shape=jax.ShapeDtypeStruct((S, POOL_W), BF16),
        in_specs=[ANY], out_specs=VMEM_SPEC,
        scratch_shapes=[pltpu.VMEM((S + POOL_PAD, POOL_W), F32), pltpu.SemaphoreType.DMA],
        compiler_params=_params(),
    )(proj)


def _pool_backward(dpooled, after):
    S = dpooled.shape[0]
    RB = _row_tile(S, 256)

    def body(dp_ref, after_ref, out_ref, pad_ref, sem):
        cp = pltpu.make_async_copy(dp_ref, pad_ref.at[pl.ds(0, S)], sem)
        cp.start()
        pad_ref[S:S + POOL_PAD, :] = jnp.zeros((POOL_PAD, POOL_W), F32)
        cp.wait()
        for g, win in enumerate(POOL_WINDOWS):
            cols = slice(g * POOL_GROUP, (g + 1) * POOL_GROUP)
            head = pad_ref[0:POOL_PAD, cols]
            t1 = (lax.broadcasted_iota(jnp.int32, (POOL_PAD, POOL_GROUP), 0) + 1).astype(F32)
            pad_ref[0:POOL_PAD, cols] = head * (float(win) / jnp.minimum(t1, float(win)))
            for r0 in range(0, S, RB):
                acc = pad_ref[r0:r0 + RB, cols]
                for k in range(1, win):
                    acc = acc + pad_ref[r0 + k:r0 + k + RB, cols]
                own = pad_ref[r0:r0 + RB, cols]
                if r0 == 0:
                    own = jnp.concatenate([head, own[POOL_PAD:]], axis=0)
                out_ref[r0:r0 + RB, cols] = acc * (1.0 / win) - own

    return pl.pallas_call(
        body, name="pool_backward",
        out_shape=jax.ShapeDtypeStruct((S, POOL_W), F32),
        in_specs=[ANY, ANY], out_specs=VMEM_SPEC,
        scratch_shapes=[pltpu.VMEM((S + POOL_PAD, POOL_W), F32), pltpu.SemaphoreType.DMA],
        compiler_params=_params(),
    )(dpooled, after)


def _mix_out(y_mla, pooled, w_pool, pool_scale, w_o, x, mod6, T):
    S = x.shape[0]

    def body(ym_ref, pl_ref, wp_ref, ps_ref, wo_ref, x_ref, mod_ref, x1_ref, mix_ref, mi_ref):
        mi_ref[:, 0:512] = ym_ref[...]
        for g in range(len(POOL_WINDOWS)):
            cols = slice(g * POOL_GROUP, (g + 1) * POOL_GROUP)
            z = _dot(pl_ref[:, cols], wp_ref[g])
            mi_ref[:, 512 + g * POOL_GROUP:512 + (g + 1) * POOL_GROUP] = (z * ps_ref[:, cols]).astype(BF16)
        mix = _dot(mi_ref[...], wo_ref[...])
        mix_ref[...] = mix
        x1_ref[...] = x_ref[...] + mod_ref[2:3, :] * mix

    row = lambda w: pl.BlockSpec((T, w), lambda i: (i, 0))
    return pl.pallas_call(
        body, name="mix_out", grid=(S // T,),
        out_shape=[jax.ShapeDtypeStruct((S, D_MODEL), F32), jax.ShapeDtypeStruct((S, D_MODEL), F32),
                   jax.ShapeDtypeStruct((S, 1024), BF16)],
        in_specs=[row(512), row(POOL_W), _full((4, POOL_GROUP, POOL_GROUP)), _full((1, POOL_W)),
                  _full((1024, D_MODEL)), row(D_MODEL), _full((N_MOD, D_MODEL))],
        out_specs=[row(D_MODEL), row(D_MODEL), row(1024)],
        compiler_params=_params(("parallel",)),
    )(y_mla, pooled, w_pool, pool_scale, w_o, x, mod6)


def _ffn_forward(x1, mod6, g_ffn, g_final, target, w_gate, w_up, w_down, T):
    S = x1.shape[0]

    def body(x1_ref, mod_ref, gf_ref, gl_ref, tgt_ref, wg_ref, wu_ref, wd_ref,
             gate_ref, up_ref, act_ref, h2_ref, dff_ref, dx2_ref, st_ref, acc_s):
        i, j = pl.program_id(0), pl.program_id(1)

        @pl.when(jnp.logical_and(i == 0, j == 0))
        def _():
            st_ref[...] = jnp.zeros_like(st_ref)

        @pl.when(j == 0)
        def _():
            xh, _ = _rms(x1_ref[...])
            h2_ref[...] = ((xh * gf_ref[...]) * (1.0 + mod_ref[4:5, :]) + mod_ref[3:4, :]).astype(BF16)
            acc_s[...] = jnp.zeros_like(acc_s)

        h2 = h2_ref[...]
        gate = _dot_nt(h2, wg_ref[j])
        up = _dot_nt(h2, wu_ref[j])
        gate_ref[...] = gate.astype(BF16)
        up_ref[...] = up.astype(BF16)
        act = (gate * jax.nn.sigmoid(gate) * up).astype(BF16)
        act_ref[...] = act
        acc_s[...] += _dot(act, wd_ref[j])

        @pl.when(j == N_CHIPS - 1)
        def _():
            ff = acc_s[...]
            x2 = x1_ref[...] + mod_ref[5:6, :] * ff
            xh, r3 = _rms(x2)
            err = xh * gl_ref[...] - tgt_ref[...]
            dy = err * (1.0 / D_MODEL)
            dx2 = _rms_bwd(dy * gl_ref[...], xh, r3)
            dx2_ref[...] = dx2
            dff_ref[...] = (dx2 * mod_ref[5:6, :]).astype(BF16)
            st_ref[0:1, :] += jnp.sum(dy * xh, axis=0, keepdims=True)
            st_ref[1:2, :] += jnp.sum(dx2 * ff, axis=0, keepdims=True)
            st_ref[2:3, :] += 0.5 * jnp.sum(err * dy)

    row = pl.BlockSpec((T, D_MODEL), lambda i, j: (i, 0))
    chunk_out = pl.BlockSpec((None, T, FF_CHUNK), lambda i, j: (j, i, 0))
    big = jax.ShapeDtypeStruct((N_CHIPS, S, FF_CHUNK), BF16)
    wide = jax.ShapeDtypeStruct((S, D_MODEL), BF16)
    return pl.pallas_call(
        body, name="ffn_forward", grid=(S // T, N_CHIPS),
        out_shape=[big, big, big, wide, wide, jax.ShapeDtypeStruct((S, D_MODEL), F32),
                   jax.ShapeDtypeStruct((8, D_MODEL), F32)],
        in_specs=[row, _full((N_MOD, D_MODEL)), _full((1, D_MODEL)), _full((1, D_MODEL)), row,
                  VMEM_SPEC, VMEM_SPEC, VMEM_SPEC],
        out_specs=[chunk_out, chunk_out, chunk_out, row, row, row, _full((8, D_MODEL))],
        scratch_shapes=[pltpu.VMEM((T, D_MODEL), F32)],
        compiler_params=_params(("arbitrary", "arbitrary")),
    )(x1, mod6, g_ffn, g_final, target, w_gate, w_up, w_down)


def _ffn_backward(dx2, x1, dff, gate, up, mod6, g_ffn, w_gate, w_up, w_down, T):
    S = x1.shape[0]

    def body(dx2_ref, x1_ref, dff_ref, gate_ref, up_ref, mod_ref, gf_ref, wg_ref, wu_ref, wd_ref,
             dgate_ref, dup_ref, dx1_ref, st_ref, acc_s):
        i, j = pl.program_id(0), pl.program_id(1)

        @pl.when(jnp.logical_and(i == 0, j == 0))
        def _():
            st_ref[...] = jnp.zeros_like(st_ref)

        @pl.when(j == 0)
        def _():
            acc_s[...] = jnp.zeros_like(acc_s)

        gate, up = gate_ref[...].astype(F32), up_ref[...].astype(F32)
        sg = jax.nn.sigmoid(gate)
        silu = gate * sg
        dact = _dot_nt(dff_ref[...], wd_ref[j])
        dup = (dact * silu).astype(BF16)
        dgate = (dact * up * (sg * (1.0 + gate * (1.0 - sg)))).astype(BF16)
        dup_ref[...] = dup
        dgate_ref[...] = dgate
        acc_s[...] += _dot(dgate, wg_ref[j]) + _dot(dup, wu_ref[j])

        @pl.when(j == N_CHIPS - 1)
        def _():
            dh2 = acc_s[...]
            xh, r2 = _rms(x1_ref[...])
            n2 = xh * gf_ref[...]
            st_ref[0:1, :] += jnp.sum(dh2, axis=0, keepdims=True)
            st_ref[1:2, :] += jnp.sum(dh2 * n2, axis=0, keepdims=True)
            dn2 = dh2 * (1.0 + mod_ref[4:5, :])
            st_ref[2:3, :] += jnp.sum(dn2 * xh, axis=0, keepdims=True)
            dx1_ref[...] = _rms_bwd(dn2 * gf_ref[...], xh, r2) + dx2_ref[...]

    row = pl.BlockSpec((T, D_MODEL), lambda i, j: (i, 0))
    chunk = pl.BlockSpec((None, T, FF_CHUNK), lambda i, j: (j, i, 0))
    big = jax.ShapeDtypeStruct((N_CHIPS, S, FF_CHUNK), BF16)
    return pl.pallas_call(
        body, name="ffn_backward", grid=(S // T, N_CHIPS),
        out_shape=[big, big, jax.ShapeDtypeStruct((S, D_MODEL), F32), jax.ShapeDtypeStruct((8, D_MODEL), F32)],
        in_specs=[row, row, row, chunk, chunk, _full((N_MOD, D_MODEL)), _full((1, D_MODEL)),
                  VMEM_SPEC, VMEM_SPEC, VMEM_SPEC],
        out_specs=[chunk, chunk, row, _full((8, D_MODEL))],
        scratch_shapes=[pltpu.VMEM((T, D_MODEL), F32)],
        compiler_params=_params(("arbitrary", "arbitrary")),
    )(dx2, x1, dff, gate, up, mod6, g_ffn, w_gate, w_up, w_down)


def _tn_matmul(a, b, a_spec, b_spec, groups, m, n, steps, name):
    def body(a_ref, b_ref, o_ref):
        @pl.when(pl.program_id(1) == 0)
        def _():
            o_ref[...] = jnp.zeros_like(o_ref)

        o_ref[...] += _dot_tn(a_ref[...], b_ref[...])

    return pl.pallas_call(
        body, name=name, grid=(groups, steps),
        out_shape=jax.ShapeDtypeStruct((groups, m, n), F32),
        in_specs=[a_spec, b_spec],
        out_specs=pl.BlockSpec((None, m, n), lambda g, i: (g, 0, 0)),
        compiler_params=_params(("parallel", "arbitrary")),
    )(a, b)


def _mix_backward(dx1, mix, mod6, w_o, pooled, w_pool, pool_scale, w_uv_t, o_lat, T, TQ):
    S = dx1.shape[0]

    def body(dx1_ref, mix_ref, mod_ref, wo_ref, pl_ref, wp_ref, ps_ref, wuv_ref, o_ref,
             dmix_ref, dp_ref, do_ref, dr_ref, gp_ref, guv_ref, st_ref):
        @pl.when(pl.program_id(0) == 0)
        def _():
            st_ref[...] = jnp.zeros_like(st_ref)
            gp_ref[...] = jnp.zeros_like(gp_ref)
            guv_ref[...] = jnp.zeros_like(guv_ref)

        dx1 = dx1_ref[...]
        st_ref[0:1, :] += jnp.sum(dx1 * mix_ref[...], axis=0, keepdims=True)
        dmix = (dx1 * mod_ref[2:3, :]).astype(BF16)
        dmix_ref[...] = dmix
        dmi = _dot_nt(dmix, wo_ref[...])
        dym = dmi[:, 0:512].astype(BF16)
        for g in range(len(POOL_WINDOWS)):
            cols = slice(g * POOL_GROUP, (g + 1) * POOL_GROUP)
            dyp = dmi[:, 512 + g * POOL_GROUP:512 + (g + 1) * POOL_GROUP]
            pooled_g = pl_ref[:, cols]
            z = _dot(pooled_g, wp_ref[g])
            st_ref[1:2, cols] += jnp.sum(dyp * z, axis=0, keepdims=True)
            dz = (dyp * ps_ref[:, cols]).astype(BF16)
            gp_ref[g] += _dot_tn(pooled_g, dz)
            dp_ref[:, cols] = _dot_nt(dz, wp_ref[g])
        for h in range(HEADS):
            dym_h = dym[:, h * 128:(h + 1) * 128]
            do = _dot_nt(dym_h, wuv_ref[h]).astype(BF16)
            do_ref[h] = do
            o_h = o_ref[h]
            guv_ref[h] += _dot_tn(o_h, dym_h)
            delta = _col_to_row(jnp.sum(do.astype(F32) * o_h.astype(F32), axis=1, keepdims=True))
            for s in range(T // TQ):
                dr_ref[s, :, h * TQ:(h + 1) * TQ] = delta[:, s * TQ:(s + 1) * TQ]

    row = lambda w: pl.BlockSpec((T, w), lambda i: (i, 0))
    heads = pl.BlockSpec((HEADS, T, KV_LORA), lambda i: (0, i, 0))
    square = jax.ShapeDtypeStruct((4, 128, 128), F32)
    return pl.pallas_call(
        body, name="mix_backward", grid=(S // T,),
        out_shape=[jax.ShapeDtypeStruct((S, D_MODEL), BF16),
                   jax.ShapeDtypeStruct((S, POOL_W), F32),
                   jax.ShapeDtypeStruct((HEADS, S, KV_LORA), BF16), jax.ShapeDtypeStruct((S // TQ, 1, HEADS * TQ), F32),
                   square, square, jax.ShapeDtypeStruct((8, D_MODEL), F32)],
        in_specs=[row(D_MODEL), row(D_MODEL), _full((N_MOD, D_MODEL)), _full((1024, D_MODEL)), row(POOL_W),
                  _full((4, POOL_GROUP, POOL_GROUP)), _full((1, POOL_W)), _full((HEADS, KV_LORA, 128)), heads],
        out_specs=[row(D_MODEL), row(POOL_W), heads,
                   pl.BlockSpec((T // TQ, 1, HEADS * TQ), lambda i: (i, 0, 0)), _full((4, 128, 128)),
                   _full((4, 128, 128)), _full((8, D_MODEL))],
        compiler_params=_params(("arbitrary",)),
    )(dx1, mix, mod6, w_o, pooled, w_pool, pool_scale, w_uv_t, o_lat)


def _attention_bwd(qc, kc, kct, do, lse_rows, delta_rows, TQ):
    S = kc.shape[0]
    R = HEADS * TQ
    nq = S // TQ

    def body(k_ref, kt_ref, q_ref, do_ref, lser_ref, dr_ref, dk_ref, dqt_ref, dk_s, dv_s):
        j = pl.program_id(0)

        @pl.when(j == 0)
        def _():
            def zero(i, carry):
                dqt_ref[i] = jnp.zeros((QK_PAD, R), F32)
                return carry
            lax.fori_loop(0, nq, zero, 0)

        k = k_ref[...]
        kt = kt_ref[...]
        v = k[:, :KV_LORA]
        dk_s[...] = jnp.zeros((TQ, QK_PAD), F32)
        dv_s[...] = jnp.zeros((TQ, KV_LORA), F32)

        def step(i, masked):
            rows = pl.ds(pl.multiple_of(i * TQ, TQ), TQ)
            q = q_ref[:, rows, :].reshape(R, QK_PAD)
            do = do_ref[:, rows, :].reshape(R, KV_LORA)
            st = _dot_nt(k, q) * SM_SCALE
            if masked:
                st = jnp.where(_diag_mask(TQ, True), st, -jnp.inf)
            pt = jnp.exp(st - lser_ref[i])
            dv_s[...] += _dot(pt, do)
            dpt = _dot_nt(v, do)
            dst = (pt * (dpt - dr_ref[i])).astype(BF16)
            dk_s[...] += _dot(dst, q)
            dqt_ref[i] += _dot(kt, dst)

        def loop(i, carry):
            step(i, False)
            return carry

        step(j, True)
        lax.fori_loop(j + 1, nq, loop, 0)
        dk = dk_s[...] * SM_SCALE
        dk_ref[:, 0:KV_LORA] = dk[:, 0:KV_LORA] + dv_s[...]
        dk_ref[:, KV_LORA:QK_PAD] = dk[:, KV_LORA:QK_PAD]

    return pl.pallas_call(
        body, name="attention_bwd", grid=(nq,),
        out_shape=[jax.ShapeDtypeStruct((S, QK_PAD), F32), jax.ShapeDtypeStruct((nq, QK_PAD, R), F32)],
        in_specs=[pl.BlockSpec((TQ, QK_PAD), lambda j: (j, 0)), pl.BlockSpec((None, QK_PAD, TQ), lambda j: (j, 0, 0)),
                  VMEM_SPEC, VMEM_SPEC, VMEM_SPEC, VMEM_SPEC],
        out_specs=[pl.BlockSpec((TQ, QK_PAD), lambda j: (j, 0)), VMEM_SPEC],
        scratch_shapes=[pltpu.VMEM((TQ, QK_PAD), F32), pltpu.VMEM((TQ, KV_LORA), F32)],
        compiler_params=_params(("arbitrary",)),
    )(kc, kct, qc, do, lse_rows, delta_rows)


def _pre_attention_backward(x, dx1, proj, q, dqt, dkc, du, cos, sin, mod6, g_mix, g_q, g_kv, w_in, w_uq, w_uk_t, T, TQ):
    S = x.shape[0]

    def body(x_ref, dx1_ref, proj_ref, q_ref, dqt_ref, dkc_ref, du_ref, cos_ref, sin_ref, mod_ref, gm_ref, gq_ref,
             gkv_ref, win_ref, wuq_ref, wuk_ref, gx_ref, dproj_ref, h1_ref, guk_ref, guq_ref, st_ref, dq_ref):
        @pl.when(pl.program_id(0) == 0)
        def _():
            st_ref[...] = jnp.zeros_like(st_ref)
            guk_ref[...] = jnp.zeros_like(guk_ref)
            guq_ref[...] = jnp.zeros_like(guq_ref)

        cos_t, sin_t = cos_ref[...], sin_ref[...]
        low = lax.broadcasted_iota(jnp.int32, (T, 128), 1) < ROPE
        rope_parts = []
        for h in range(HEADS):
            dqc = jnp.concatenate([jnp.transpose(dqt_ref[s, :, h * TQ:(h + 1) * TQ]) for s in range(T // TQ)], axis=0)
            dqc = dqc * SM_SCALE
            dql = dqc[:, 0:KV_LORA].astype(BF16)
            guk_ref[h] += _dot_tn(dql, q_ref[:, h * NOPE:(h + 1) * NOPE])
            dq_ref[:, h * NOPE:(h + 1) * NOPE] = _dot(dql, wuk_ref[h]).astype(BF16)
            rope_parts.append(dqc[:, KV_LORA:QK_PAD])
        for pair in range(2):
            d = jnp.where(low, rope_parts[2 * pair], rope_parts[2 * pair + 1])
            dq_ref[:, O_QA + 128 * pair:O_QA + 128 * (pair + 1)] = _rope_bwd(d, cos_t, sin_t).astype(BF16)
        dq = dq_ref[...]
        dcq = _dot_nt(dq, wuq_ref[...])
        cqh, rq = _rms(proj_ref[:, 0:Q_LORA])
        guq_ref[...] += _dot_tn(cqh * gq_ref[...], dq)
        st_ref[3:4, 0:Q_LORA] += jnp.sum(dcq * cqh, axis=0, keepdims=True)
        dproj_ref[:, 0:Q_LORA] = _rms_bwd(dcq * gq_ref[...], cqh, rq).astype(BF16)
        dckv = dkc_ref[:, 0:KV_LORA]
        ckvh, rkv = _rms(proj_ref[:, O_CKV:O_KR])
        st_ref[4:5, 0:KV_LORA] += jnp.sum(dckv * ckvh, axis=0, keepdims=True)
        dproj_ref[:, O_CKV:O_KR] = _rms_bwd(dckv * gkv_ref[...], ckvh, rkv).astype(BF16)
        dkr = _rope_bwd(dkc_ref[:, KV_LORA:QK_PAD], cos_t, sin_t)
        dkr = jnp.where(low, dkr + pltpu.roll(dkr, ROPE, 1), 0.0)
        dproj_ref[:, O_KR:O_U] = dkr.astype(BF16)
        dproj_ref[:, O_U:PROJ_W] = du_ref[...].astype(BF16)
        dproj = dproj_ref[...]
        dh1 = jnp.concatenate([_dot(dproj, win_ref[j]) for j in range(N_CHIPS)], axis=1)
        xh, r1 = _rms(x_ref[...])
        n1 = xh * gm_ref[...]
        h1_ref[...] = (n1 * (1.0 + mod_ref[1:2, :]) + mod_ref[0:1, :]).astype(BF16)
        st_ref[0:1, :] += jnp.sum(dh1, axis=0, keepdims=True)
        st_ref[1:2, :] += jnp.sum(dh1 * n1, axis=0, keepdims=True)
        dn1 = dh1 * (1.0 + mod_ref[1:2, :])
        st_ref[2:3, :] += jnp.sum(dn1 * xh, axis=0, keepdims=True)
        gx_ref[...] = _rms_bwd(dn1 * gm_ref[...], xh, r1) + dx1_ref[...]

    row = lambda w: pl.BlockSpec((T, w), lambda i: (i, 0))
    return pl.pallas_call(
        body, name="pre_attention_backward", grid=(S // T,),
        out_shape=[jax.ShapeDtypeStruct((S, D_MODEL), F32), jax.ShapeDtypeStruct((S, PROJ_W), BF16),
                   jax.ShapeDtypeStruct((S, D_MODEL), BF16), jax.ShapeDtypeStruct((HEADS, KV_LORA, NOPE), F32),
                   jax.ShapeDtypeStruct((Q_LORA, Q_W), F32), jax.ShapeDtypeStruct((8, D_MODEL), F32)],
        in_specs=[row(D_MODEL), row(D_MODEL), row(PROJ_W), row(HEADS * NOPE),
                  pl.BlockSpec((T // TQ, QK_PAD, HEADS * TQ), lambda i: (i, 0, 0)),
                  row(QK_PAD), row(POOL_W), row(128), row(128), _full((N_MOD, D_MODEL)), _full((1, D_MODEL)),
                  _full((1, Q_LORA)), _full((1, KV_LORA)), _full((N_CHIPS, PROJ_W, D_MODEL // N_CHIPS)),
                  _full((Q_LORA, Q_W)), _full((HEADS, KV_LORA, NOPE))],
        out_specs=[row(D_MODEL), row(PROJ_W), row(D_MODEL), _full((HEADS, KV_LORA, NOPE)), _full((Q_LORA, Q_W)),
                   _full((8, D_MODEL))],
        scratch_shapes=[pltpu.VMEM((T, Q_W), BF16)],
        compiler_params=_params(("arbitrary",)),
    )(x, dx1, proj, q, dqt, dkc, du, cos, sin, mod6, g_mix, g_q, g_kv, w_in, w_uq, w_uk_t)


def _ada_grads(c_all, dmod_all, chip):
    cols = N_MOD * D_MODEL // N_CHIPS
    width = dmod_all.shape[1]

    def body(col_ref, c_ref, dcol_ref, dall_ref, gw_ref, gb_ref):
        call = c_ref[...]
        act = call * jax.nn.sigmoid(call)
        gw_ref[...] = _dot_tn(act, dcol_ref[...])
        d = dall_ref[...]
        acc = d[0:1, :]
        for b in range(1, 8):
            acc = acc + d[b:b + 1, :]
        gb_ref[...] = acc

    return pl.pallas_call(
        body, name="ada_grads",
        out_shape=[jax.ShapeDtypeStruct((D_MODEL, cols), F32), jax.ShapeDtypeStruct((1, width), F32)],
        grid_spec=pltpu.PrefetchScalarGridSpec(
            num_scalar_prefetch=1, grid=(1,),
            in_specs=[pl.BlockSpec((8, D_MODEL), lambda s, col_ref: (0, 0)),
                      pl.BlockSpec((8, cols), lambda s, col_ref: (0, col_ref[0])),
                      pl.BlockSpec((8, width), lambda s, col_ref: (0, 0))],
            out_specs=[pl.BlockSpec((D_MODEL, cols), lambda s, col_ref: (0, 0)),
                       pl.BlockSpec((1, width), lambda s, col_ref: (0, 0))]),
        compiler_params=_params(("arbitrary",)),
    )(chip, c_all, dmod_all, dmod_all)


def _adamw(w, g, m, v, name):
    rows, cols = w.shape
    T = _row_tile(rows, 256)

    def body(w_ref, g_ref, m_ref, v_ref, d_ref, nm_ref, nv_ref):
        g = g_ref[...]
        m2 = ADAM_B1 * m_ref[...] + (1.0 - ADAM_B1) * g
        v2 = ADAM_B2 * v_ref[...] + (1.0 - ADAM_B2) * (g * g)
        m_hat = m2 / (1.0 - ADAM_B1 ** ADAM_STEP)
        v_hat = v2 / (1.0 - ADAM_B2 ** ADAM_STEP)
        d_ref[...] = -ADAM_LR * (m_hat / (jnp.sqrt(v_hat) + ADAM_EPS) + ADAM_WD * w_ref[...])
        nm_ref[...] = m2
        nv_ref[...] = v2

    spec = pl.BlockSpec((T, cols), lambda i: (i, 0))
    return pl.pallas_call(
        body, name=name, grid=(rows // T,),
        out_shape=[jax.ShapeDtypeStruct((rows, cols), F32)] * 3,
        in_specs=[spec] * 4, out_specs=[spec] * 3,
        compiler_params=_params(("parallel",)),
    )(w, g, m, v)


SMALL_NAMES = ("w_uk", "w_uv", "w_pool", "g_mix", "g_q", "g_kv", "pool_scale", "g_ffn", "g_final", "b_ada")
SMALL_ROWS = 1664


def _pack_rows(parts):
    flat = jnp.concatenate([p.reshape(-1) for p in parts])
    pad = (-flat.shape[0]) % 128
    if pad:
        flat = jnp.concatenate([flat, jnp.zeros((pad,), F32)])
    return flat.reshape(-1, 128)


def kernel(x, c, positions, w_ada, b_ada, g_mix, w_in, g_q, g_kv, w_uq, w_uk, w_uv, w_pool, pool_scale, w_o, g_ffn, w_gate, w_up, w_down, g_final, loss_target, m_w_ada, m_b_ada, m_g_mix, m_w_in, m_g_q, m_g_kv, m_w_uq, m_w_uk, m_w_uv, m_w_pool, m_pool_scale, m_w_o, m_g_ffn, m_w_gate, m_w_up, m_w_down, m_g_final, v_w_ada, v_b_ada, v_g_mix, v_w_in, v_g_q, v_g_kv, v_w_uq, v_w_uk, v_w_uv, v_w_pool, v_pool_scale, v_w_o, v_g_ffn, v_w_gate, v_w_up, v_w_down, v_g_final):
    S = x.shape[1]
    T = _row_tile(S, 512)
    TQ = _row_tile(S, 256)
    TW = _row_tile(S, 1024)
    ix, iy, ic = lax.axis_index("x"), lax.axis_index("y"), lax.axis_index("c")
    chip = (2 * ix + iy).astype(jnp.int32)
    chip_arr = chip.reshape(1)
    core_arr = ic.astype(jnp.int32).reshape(1)

    xs, tgt = x[0], loss_target[0]

    ada_cols = w_ada.shape[2]
    b_cols = lax.dynamic_slice(b_ada, (0, chip * ada_cols), (1, ada_cols))
    mod, c_all = _mod_exchange(c, w_ada[0], b_cols)
    mod6 = mod.reshape(N_MOD, D_MODEL)

    tr = lambda a: jnp.transpose(a[0])
    win_t = tr(w_in)
    win_p = jnp.concatenate([win_t[:O_KR + ROPE], win_t[O_KR:O_KR + ROPE], win_t[O_KR + ROPE:]], axis=0).astype(BF16)
    wuq = w_uq[0]
    wuq_p = jnp.concatenate([wuq[:, h, :NOPE] for h in range(HEADS)] + [wuq[:, h, NOPE:] for h in range(HEADS)],
                            axis=1).astype(BF16)
    first = _weight_gather([win_p, wuq_p])
    w_in_f = first[0]
    w_uq_f = first[1].reshape(Q_LORA, Q_W)
    w_uk_t = jnp.transpose(w_uk[0], (1, 0, 2)).astype(BF16)
    w_uv_t = jnp.transpose(w_uv[0], (1, 0, 2)).astype(BF16)
    w_pool_b = w_pool[0].astype(BF16)
    later = [w_o[0].astype(BF16), tr(w_gate).astype(BF16), tr(w_up).astype(BF16), w_down[0].astype(BF16)]
    wg_lands = _place_shards(chip_arr, later)
    wg_lands, mod6, w_in_f = lax.optimization_barrier((wg_lands, mod6, w_in_f))
    wg_send, wg_recv, wg_lands, token = _split_start(
        "weights_start", later, wg_lands, 3 * len(later), _plan_gather_start)
    mod6 = mod6 + token[0, 0]

    half = ROPE // 2
    freqs = jnp.power(ROPE_THETA, -jnp.arange(half, dtype=F32) / half)
    cos, sin = _rope_tables(positions.reshape(S, 1), jnp.tile(freqs, 4).reshape(1, 128))
    proj, q, qc, kc, kct = _pre_attention(xs, mod6, g_mix, g_q, g_kv, w_in_f, w_uq_f, w_uk_t, cos, sin, T, TQ)
    o_lat, y_mla, lse_rows = _attention_fwd(qc, kc, kct, w_uv_t, TQ)
    wg_send, wg_recv, wg_lands, token = _split_relay(
        "weights_relay", wg_send, wg_recv, later, wg_lands, y_mla, 3 * len(later), _plan_gather_landed,
        _plan_gather_relay)
    pooled = _pool_forward(proj)
    wg_lands = _split_wait("weights_wait", wg_send, wg_recv, [], wg_lands, pooled, _plan_gather_wait)
    w_o_f = wg_lands[0].reshape(1024, D_MODEL)
    w_gate_f, w_up_f, w_down_f = wg_lands[1], wg_lands[2], wg_lands[3]
    x1, mix, mix_in = _mix_out(y_mla, pooled, w_pool_b, pool_scale, w_o_f, xs, mod6, T)
    gate, up, act, h2, dff, dx2, st_f = _ffn_forward(
        x1, mod6, g_ffn, g_final.reshape(1, D_MODEL), tgt, w_gate_f, w_up_f, w_down_f, T)

    dgate, dup, dx1, st_b = _ffn_backward(dx2, x1, dff, gate, up, mod6, g_ffn, w_gate_f, w_up_f, w_down_f, T)
    steps = S // TW
    chunk_spec = pl.BlockSpec((None, TW, FF_CHUNK), lambda g, i: (g, i, 0))
    wide_spec = pl.BlockSpec((TW, D_MODEL), lambda g, i: (i, 0))
    g_down = _tn_matmul(act, dff, chunk_spec, wide_spec, N_CHIPS, FF_CHUNK, D_MODEL, steps, "grad_w_down")
    g_gate = _tn_matmul(dgate, h2, chunk_spec, wide_spec, N_CHIPS, FF_CHUNK, D_MODEL, steps, "grad_w_gate")
    g_up = _tn_matmul(dup, h2, chunk_spec, wide_spec, N_CHIPS, FF_CHUNK, D_MODEL, steps, "grad_w_up")

    half_shapes = lambda gs: [jax.ShapeDtypeStruct((N_CHIPS, g.shape[1] // 2, g.shape[2]), F32) for g in gs]
    ffn_grads = [g_gate, g_up, g_down]
    f_send, f_recv, f_lands, token = _split_start(
        "ffn_swap_start", ffn_grads, half_shapes(ffn_grads), len(ffn_grads), _plan_swap_start)
    dmix, dpooled, do_lat, delta_rows, g_pool, g_uv_t, st_m = _mix_backward(
        dx1, mix, mod6 + token[0, 0], w_o_f, pooled, w_pool_b, pool_scale, w_uv_t, o_lat, T, TQ)
    g_o = [_tn_matmul(mix_in, dmix, wide_spec, wide_spec, 1, 1024, D_MODEL, steps, "grad_w_o").reshape(N_CHIPS, -1, D_MODEL)]
    o_send, o_recv, o_lands, token = _split_start("w_o_swap_start", g_o, half_shapes(g_o), 1, _plan_swap_start)
    du = _pool_backward(dpooled, token)
    f_got = _split_wait("ffn_swap_wait", f_send, f_recv, ffn_grads, f_lands, du, _plan_swap_wait)
    f_got += _split_wait("w_o_swap_wait", o_send, o_recv, g_o, o_lands, du, _plan_swap_wait)
    far_names = ("w_gate", "w_up", "w_down", "w_o")
    far_grads = ffn_grads + g_o
    f_sums = [_add_my_half(core_arr, a, b, "add_half_" + n) for a, b, n in zip(far_grads, f_got, far_names)]
    f_send, f_recv, f_lands, token = _split_start(
        "far_exchange_start", f_sums, [jax.ShapeDtypeStruct((3,) + s.shape[1:], F32) for s in f_sums],
        3 * len(f_sums), _plan_exchange_start)
    delta_rows = delta_rows + token[0, 0]
    dkc, dqt = _attention_bwd(qc, kc, kct, do_lat, lse_rows, delta_rows, TQ)
    grad_x, dproj, h1, g_uk_t, uq, st_p = _pre_attention_backward(
        xs, dx1, proj, q, dqt, dkc, du, cos, sin, mod6, g_mix, g_q, g_kv, w_in_f, w_uq_f, w_uk_t, T, TQ)
    rows_in = D_MODEL // N_CHIPS
    g_in_p = _tn_matmul(dproj, h1, pl.BlockSpec((TW, PROJ_W), lambda g, i: (i, 0)),
                        pl.BlockSpec((TW, rows_in), lambda g, i: (i, g)), N_CHIPS, PROJ_W, rows_in, steps, "grad_w_in")

    g_in = jnp.concatenate([g_in_p[:, :O_KR + ROPE], g_in_p[:, O_U:]], axis=1)
    g_uq = jnp.concatenate([jnp.concatenate([uq[:, h * NOPE:(h + 1) * NOPE], uq[:, O_QA + h * ROPE:O_QA + (h + 1) * ROPE]],
                                            axis=1) for h in range(HEADS)], axis=1).reshape(N_CHIPS, -1, HEADS * HEAD_QK)
    small = _pack_rows([g_uk_t, g_uv_t, g_pool, st_p[2], st_p[3, :Q_LORA], st_p[4, :KV_LORA], st_m[1, :POOL_W],
                        st_b[2], st_f[0]])
    small = jnp.concatenate([small, jnp.zeros((SMALL_ROWS - small.shape[0], 128), F32)]).reshape(N_CHIPS, -1, 128)
    grads = [g_in, g_uq, small]
    dmod = jnp.concatenate([jnp.stack([st_p[0], st_p[1], st_m[0], st_b[0], st_b[1], st_f[1]]).reshape(48, 128),
                            jnp.zeros((8, 128), F32).at[0, 0].set(st_f[2, 0])])

    f_others = _split_wait("far_exchange_wait", f_send, f_recv, f_sums, f_lands, g_in_p, _plan_exchange_wait)
    chip_core = jnp.concatenate([chip_arr, core_arr])
    f_pairs = [_add_chips_into_pair(chip_core, a, b, "add_chips_" + n) for a, b, n in zip(f_sums, f_others, far_names)]
    f_send, f_recv, f_pairs, token = _split_start("far_finish_start", [], f_pairs, len(f_pairs), _plan_finish_start)
    dmod = dmod + token[0, 0]

    names = ("w_in", "w_uq", "small")
    got, dmod_all = _grad_swap_halves(grads, dmod)
    chip_sums = [_add_my_half(core_arr, a, b, "add_half_" + n) for a, b, n in zip(grads, got, names)]
    n_send, n_recv, n_lands, token = _split_start(
        "near_exchange_start", chip_sums, [jax.ShapeDtypeStruct((3,) + s.shape[1:], F32) for s in chip_sums],
        3 * len(chip_sums), _plan_exchange_start)
    f_fulls = _split_wait("far_finish_wait", f_send, f_recv, [], f_pairs, token, _plan_finish_wait)
    gw_gate, gw_up, gw_down, gw_o = [f.reshape(-1, f.shape[2]) for f in f_fulls]
    gw_ada, gb_ada = _ada_grads(c_all, dmod_all.reshape(8, -1), chip_arr)
    loss = gb_ada[0, N_MOD * D_MODEL]
    gb_ada = gb_ada[:, :N_MOD * D_MODEL]

    untr = lambda a: jnp.transpose(a)[None]
    grad_out, delta_out, newm_out, newv_out = {}, {}, {}, {}

    def adam_sharded(n, w, g2, m, v, transposed):
        view = (lambda a: jnp.transpose(a[0])) if transposed else (lambda a: a.reshape(g2.shape))
        back = untr if transposed else (lambda a: a.reshape(w.shape))
        d_, m_, v_ = _adamw(view(w), g2, view(m), view(v), "adamw_" + n)
        grad_out[n], delta_out[n], newm_out[n], newv_out[n] = back(g2), back(d_), back(m_), back(v_)
        return d_

    adam_sharded("w_gate", w_gate, gw_gate, m_w_gate, v_w_gate, True)
    adam_sharded("w_up", w_up, gw_up, m_w_up, v_w_up, True)
    adam_sharded("w_down", w_down, gw_down, m_w_down, v_w_down, False)
    adam_sharded("w_o", w_o, gw_o, m_w_o, v_w_o, False)
    last = adam_sharded("w_ada", w_ada, gw_ada, m_w_ada, v_w_ada, False)

    others = _split_wait("near_exchange_wait", n_send, n_recv, chip_sums, n_lands, last, _plan_exchange_wait)
    halves = [_add_chips(chip_arr, a, b, "add_chips_" + n) for a, b, n in zip(chip_sums, others, names)]
    fulls, small_all = _grad_finish(halves[:2], halves[2])
    gw_in, gw_uq = [f.reshape(-1, f.shape[2]) for f in fulls]
    small_all = small_all.reshape(SMALL_ROWS * 128)
    adam_sharded("w_in", w_in, gw_in, m_w_in, v_w_in, True)
    adam_sharded("w_uq", w_uq, gw_uq, m_w_uq, v_w_uq, False)

    n_sq = KV_LORA * HEADS * 128
    sizes = [n_sq, n_sq, n_sq, D_MODEL, Q_LORA, KV_LORA, POOL_W, D_MODEL, D_MODEL]
    offs = [0]
    for s_ in sizes:
        offs.append(offs[-1] + s_)
    piece = lambda k: small_all[offs[k]:offs[k + 1]]
    grads_small = {
        "w_uk": jnp.transpose(piece(0).reshape(HEADS, KV_LORA, NOPE), (1, 0, 2)),
        "w_uv": jnp.transpose(piece(1).reshape(HEADS, KV_LORA, 128), (1, 0, 2)),
        "w_pool": piece(2).reshape(4, POOL_GROUP, POOL_GROUP),
        "g_mix": piece(3), "g_q": piece(4), "g_kv": piece(5), "pool_scale": piece(6), "g_ffn": piece(7),
        "g_final": piece(8), "b_ada": gb_ada.reshape(-1),
    }
    weights_small = {"w_uk": w_uk, "w_uv": w_uv, "w_pool": w_pool, "g_mix": g_mix, "g_q": g_q, "g_kv": g_kv,
                     "pool_scale": pool_scale, "g_ffn": g_ffn, "g_final": g_final, "b_ada": b_ada}
    m_small = {"w_uk": m_w_uk, "w_uv": m_w_uv, "w_pool": m_w_pool, "g_mix": m_g_mix, "g_q": m_g_q, "g_kv": m_g_kv,
               "pool_scale": m_pool_scale, "g_ffn": m_g_ffn, "g_final": m_g_final, "b_ada": m_b_ada}
    v_small = {"w_uk": v_w_uk, "w_uv": v_w_uv, "w_pool": v_w_pool, "g_mix": v_g_mix, "g_q": v_g_q, "g_kv": v_g_kv,
               "pool_scale": v_pool_scale, "g_ffn": v_g_ffn, "g_final": v_g_final, "b_ada": v_b_ada}
    pack = lambda d: _pack_rows([d[n] for n in SMALL_NAMES])
    d_s, m_s, v_s = _adamw(pack(weights_small), pack(grads_small), pack(m_small), pack(v_small), "adamw_small")

    def unpack(flat2d):
        flat = flat2d.reshape(-1)
        out, o = {}, 0
        for n in SMALL_NAMES:
            size = weights_small[n].size
            out[n] = flat[o:o + size].reshape(weights_small[n].shape)
            o += size
        return out

    delta_s, newm_s, newv_s = unpack(d_s), unpack(m_s), unpack(v_s)

    for n in SMALL_NAMES:
        grad_out[n] = grads_small[n].reshape(weights_small[n].shape)
        delta_out[n], newm_out[n], newv_out[n] = delta_s[n], newm_s[n], newv_s[n]

    order = ("w_ada", "b_ada", "g_mix", "w_in", "g_q", "g_kv", "w_uq", "w_uk", "w_uv", "w_pool", "pool_scale", "w_o",
             "g_ffn", "w_gate", "w_up", "w_down", "g_final")
    return (loss, grad_x.reshape(x.shape), *[grad_out[n] for n in order], *[delta_out[n] for n in order],
            *[newm_out[n] for n in order], *[newv_out[n] for n in order])
```

```python
import functools

import jax
import jax.numpy as jnp
from jax import lax
from jax.experimental import pallas as pl
from jax.experimental.pallas import tpu as pltpu

F32 = jnp.float32
BF16 = jnp.bfloat16

D_MODEL = 1024
HEADS = 4
NOPE = 128
ROPE = 64
HEAD_QK = NOPE + ROPE
Q_LORA = 256
KV_LORA = 128
POOL_W = 512
POOL_WINDOWS = (2, 4, 8, 16)
POOL_GROUP = 128
POOL_PAD = 16
D_FF = 2816
N_CHIPS = 4
FF_CHUNK = D_FF // N_CHIPS
N_MOD = 6
EPS = 1e-6
SM_SCALE = HEAD_QK ** -0.5
ROPE_THETA = 10000.0
QK_PAD = 256
CHUNK = 64
CHUNK_SHIFT = 6

ADAM_LR = 0.001
ADAM_B1 = 0.9
ADAM_B2 = 0.999
ADAM_EPS = 1e-08
ADAM_WD = 0.01
ADAM_STEP = 10

VMEM_LIMIT = 48 * 1024 * 1024
MESH = pl.DeviceIdType.MESH
ANY = pl.BlockSpec(memory_space=pl.ANY)
VMEM_SPEC = pl.BlockSpec(memory_space=pltpu.VMEM)

PROJ_W = 1024
O_CKV = 256
O_KR = 384
O_U = 512
Q_W = 768
O_QA = 512
O_QB = 640


def _params(sem=None, vmem=VMEM_LIMIT):
    kw = dict(vmem_limit_bytes=vmem)
    if sem is not None:
        kw["dimension_semantics"] = sem
    return pltpu.CompilerParams(**kw)


def _dot(a, b):
    return jnp.dot(a.astype(BF16), b.astype(BF16), preferred_element_type=F32)


def _dot_nt(a, b):
    return lax.dot_general(a.astype(BF16), b.astype(BF16), (((1,), (1,)), ((), ())), preferred_element_type=F32)


def _dot_tn(a, b):
    return lax.dot_general(a.astype(BF16), b.astype(BF16), (((0,), (0,)), ((), ())), preferred_element_type=F32)


def _row_tile(rows, target):
    best = rows
    for t in range(8, min(rows, target) + 1, 8):
        if rows % t == 0:
            best = t
    return best if rows % best == 0 and best <= target else rows


def _rms(x):
    r = lax.rsqrt(jnp.mean(x * x, axis=-1, keepdims=True) + EPS)
    return x * r, r


def _rms_bwd(dxh, xh, r):
    return r * (dxh - xh * jnp.mean(dxh * xh, axis=-1, keepdims=True))


def _lane_first_half(shape):
    lane = lax.broadcasted_iota(jnp.int32, shape, 1)
    return (lane & (ROPE - 1)) < (ROPE // 2)


def _rope(a, cos, sin):
    first = _lane_first_half(a.shape)
    up = pltpu.roll(a, 96, 1)
    dn = pltpu.roll(a, 32, 1)
    return a * cos + jnp.where(first, -up, dn) * sin


def _rope_bwd(d, cos, sin):
    first = _lane_first_half(d.shape)
    up = pltpu.roll(d, 96, 1)
    dn = pltpu.roll(d, 32, 1)
    return d * cos + jnp.where(first, up, -dn) * sin


RELATIONS = tuple((dx, dy, dc) for dx in (0, 1) for dy in (0, 1) for dc in (0, 1) if (dx, dy, dc) != (0, 0, 0))
CHIP_RELATIONS = ((1, 0), (0, 1), (1, 1))


def _flip(v, d):
    return 1 - v if d else v


def _place():
    return lax.axis_index("x"), lax.axis_index("y"), lax.axis_index("c")


def _remote(src, dst, send_sem, recv_sem, target):
    return pltpu.make_async_remote_copy(src_ref=src, dst_ref=dst, send_sem=send_sem, recv_sem=recv_sem,
                                        device_id=target, device_id_type=MESH)


def _mod_exchange(c_row, w_ada, b_ada):
    cols = w_ada.shape[1]

    def body(c_ref, w_ref, b_ref, mod_ref, call_ref, part_ref, send1, recv1, loc1, send2, recv2, loc2):
        x, y, c = _place()
        me = 4 * x + 2 * y + c
        own = pltpu.make_async_copy(c_ref, call_ref.at[pl.ds(me, 1)], loc1)
        own.start()
        sends = []
        for k, (dx, dy, dc) in enumerate(RELATIONS):
            cp = _remote(c_ref, call_ref.at[pl.ds(me, 1)], send1.at[k], recv1.at[k],
                         (_flip(x, dx), _flip(y, dy), _flip(c, dc)))
            cp.start()
            sends.append(cp)
        for k, (dx, dy, dc) in enumerate(RELATIONS):
            src = 4 * _flip(x, dx) + 2 * _flip(y, dy) + _flip(c, dc)
            _remote(c_ref, call_ref.at[pl.ds(src, 1)], send1.at[k], recv1.at[k], (x, y, c)).wait_recv()
        own.wait()
        for cp in sends:
            cp.wait_send()
        call = call_ref[...]
        act = call * jax.nn.sigmoid(call)
        part_ref[...] = _dot(act, w_ref[...]) + b_ref[...]
        chip = 2 * x + y
        mine = pltpu.make_async_copy(part_ref.at[pl.ds(me, 1)], mod_ref.at[pl.ds(chip, 1)], loc2)
        mine.start()
        sends = []
        for k, (dx, dy) in enumerate(CHIP_RELATIONS):
            tx, ty = _flip(x, dx), _flip(y, dy)
            tb = 4 * tx + 2 * ty + c
            cp = _remote(part_ref.at[pl.ds(tb, 1)], mod_ref.at[pl.ds(chip, 1)], send2.at[k], recv2.at[k], (tx, ty, c))
            cp.start()
            sends.append(cp)
        for k, (dx, dy) in enumerate(CHIP_RELATIONS):
            src_chip = 2 * _flip(x, dx) + _flip(y, dy)
            _remote(part_ref.at[pl.ds(me, 1)], mod_ref.at[pl.ds(src_chip, 1)], send2.at[k], recv2.at[k],
                    (x, y, c)).wait_recv()
        mine.wait()
        for cp in sends:
            cp.wait_send()

    return pl.pallas_call(
        body, name="mod_exchange",
        out_shape=[jax.ShapeDtypeStruct((N_CHIPS, cols), F32), jax.ShapeDtypeStruct((8, D_MODEL), F32)],
        in_specs=[VMEM_SPEC, VMEM_SPEC, VMEM_SPEC], out_specs=[VMEM_SPEC, VMEM_SPEC],
        scratch_shapes=[pltpu.VMEM((8, cols), F32),
                        pltpu.SemaphoreType.DMA((7,)), pltpu.SemaphoreType.DMA((7,)), pltpu.SemaphoreType.DMA,
                        pltpu.SemaphoreType.DMA((3,)), pltpu.SemaphoreType.DMA((3,)), pltpu.SemaphoreType.DMA],
        compiler_params=_params(),
    )(c_row, w_ada, b_ada)


def _weight_gather(shards):
    n = len(shards)

    def body(*refs):
        ins, outs = refs[:n], refs[n:2 * n]
        send_sems, recv_sems, loc_sems = refs[2 * n:]
        x, y, c = _place()
        chip = 2 * x + y
        local = []
        for w in range(n):
            cp = pltpu.make_async_copy(ins[w], outs[w].at[chip], loc_sems.at[w])
            cp.start()
            local.append(cp)
        sends = []
        for w in range(n):
            hr = ins[w].shape[0] // 2
            half = pl.ds(c * hr, hr)
            for k, (dx, dy) in enumerate(CHIP_RELATIONS):
                cp = _remote(ins[w].at[half], outs[w].at[chip, half], send_sems.at[w, k], recv_sems.at[w, k],
                             (_flip(x, dx), _flip(y, dy), c))
                cp.start()
                sends.append(cp)
        for w in range(n):
            hr = ins[w].shape[0] // 2
            half = pl.ds(c * hr, hr)
            for k, (dx, dy) in enumerate(CHIP_RELATIONS):
                src_chip = 2 * _flip(x, dx) + _flip(y, dy)
                got = outs[w].at[src_chip, half]
                _remote(got, got, send_sems.at[w, k], recv_sems.at[w, k], (x, y, c)).wait_recv()
                cp = _remote(got, got, send_sems.at[w, 3 + k], recv_sems.at[w, 3 + k], (x, y, 1 - c))
                cp.start()
                sends.append(cp)
        for w in range(n):
            hr = ins[w].shape[0] // 2
            other = pl.ds((1 - c) * hr, hr)
            for k, (dx, dy) in enumerate(CHIP_RELATIONS):
                src_chip = 2 * _flip(x, dx) + _flip(y, dy)
                got = outs[w].at[src_chip, other]
                _remote(got, got, send_sems.at[w, 3 + k], recv_sems.at[w, 3 + k], (x, y, c)).wait_recv()
        for cp in sends:
            cp.wait_send()
        for cp in local:
            cp.wait()

    return pl.pallas_call(
        body, name="weight_gather",
        out_shape=[jax.ShapeDtypeStruct((N_CHIPS,) + s.shape, s.dtype) for s in shards],
        in_specs=[VMEM_SPEC] * n, out_specs=[ANY] * n,
        scratch_shapes=[pltpu.SemaphoreType.DMA((n, 6)), pltpu.SemaphoreType.DMA((n, 6)),
                        pltpu.SemaphoreType.DMA((n,))],
        compiler_params=_params(),
    )(*shards)


HBM_SPEC = pl.BlockSpec(memory_space=pltpu.HBM)
SEM_SPEC = pl.BlockSpec(memory_space=pltpu.SEMAPHORE)
DATAFLOW = pltpu.SideEffectType.DATAFLOW_SIDE_EFFECTING


def _in_hbm(a):
    return pltpu.with_memory_space_constraint(a, pltpu.HBM)


def _hbm_like(arrays):
    return [pltpu.HBM(a.shape, a.dtype) for a in arrays]


def _split_start(name, srcs, lands, n_remote, plan):
    lands = [lax.empty(a.shape, a.dtype) if isinstance(a, jax.ShapeDtypeStruct) else a for a in lands]
    n, m = len(srcs), len(lands)

    def body(*refs):
        src_refs, land_refs = refs[:n], refs[n:n + m]
        send_sems, recv_sems, token = refs[n + m], refs[n + m + 1], refs[n + 2 * m + 2]
        remote = plan(_place(), src_refs, land_refs)
        assert len(remote) == n_remote
        for i, (s, d, target) in enumerate(remote):
            _remote(s, d, send_sems.at[i], recv_sems.at[i], target).start()
        token[...] = jnp.zeros_like(token)

    res = pl.pallas_call(
        body, name=name,
        out_shape=(pltpu.SemaphoreType.DMA((n_remote,)), pltpu.SemaphoreType.DMA((n_remote,)),
                   *_hbm_like(lands), jax.ShapeDtypeStruct((8, 128), F32)),
        in_specs=[HBM_SPEC] * (n + m),
        out_specs=(SEM_SPEC, SEM_SPEC, *([HBM_SPEC] * m), VMEM_SPEC),
        input_output_aliases={n + i: 2 + i for i in range(m)},
        compiler_params=pltpu.CompilerParams(has_side_effects=DATAFLOW),
    )(*[_in_hbm(a) for a in srcs], *[_in_hbm(a) for a in lands])
    return res[0], res[1], list(res[2:2 + m]), res[2 + m]


def _split_wait(name, send_sems, recv_sems, srcs, lands, after, plan):
    n, m = len(srcs), len(lands)

    def body(*refs):
        src_refs, land_refs = refs[:n], refs[n:n + m]
        send_sems, recv_sems = refs[n + m], refs[n + m + 1]
        place = _place()
        for i, (s, d) in enumerate(plan(place, src_refs, land_refs)):
            cp = _remote(s, d, send_sems.at[i], recv_sems.at[i], place)
            cp.wait_send()
            cp.wait_recv()

    res = pl.pallas_call(
        body, name=name,
        out_shape=tuple(_hbm_like(lands)),
        in_specs=[HBM_SPEC] * (n + m) + [SEM_SPEC, SEM_SPEC, ANY],
        out_specs=tuple([HBM_SPEC] * m),
        input_output_aliases={n + i: i for i in range(m)},
        compiler_params=pltpu.CompilerParams(has_side_effects=DATAFLOW),
    )(*srcs, *lands, send_sems, recv_sems, after)
    return list(res)


def _split_relay(name, send_sems, recv_sems, srcs, lands, after, n_remote, plan_wait, plan_send):
    n, m = len(srcs), len(lands)

    def body(*refs):
        src_refs, land_refs = refs[:n], refs[n:n + m]
        old_send, old_recv = refs[n + m], refs[n + m + 1]
        new_send, new_recv = refs[n + m + 3], refs[n + m + 4]
        token = refs[n + m + 5 + m]
        place = _place()
        for i, (s, d) in enumerate(plan_wait(place, src_refs, land_refs)):
            cp = _remote(s, d, old_send.at[i], old_recv.at[i], place)
            cp.wait_send()
            cp.wait_recv()
        for i, (s, d, target) in enumerate(plan_send(place, land_refs)):
            _remote(s, d, new_send.at[i], new_recv.at[i], target).start()
        token[...] = jnp.zeros_like(token)

    res = pl.pallas_call(
        body, name=name,
        out_shape=(pltpu.SemaphoreType.DMA((n_remote,)), pltpu.SemaphoreType.DMA((n_remote,)),
                   *_hbm_like(lands), jax.ShapeDtypeStruct((8, 128), F32)),
        in_specs=[HBM_SPEC] * (n + m) + [SEM_SPEC, SEM_SPEC, ANY],
        out_specs=(SEM_SPEC, SEM_SPEC, *([HBM_SPEC] * m), VMEM_SPEC),
        input_output_aliases={n + i: 2 + i for i in range(m)},
        compiler_params=pltpu.CompilerParams(has_side_effects=DATAFLOW),
    )(*srcs, *lands, send_sems, recv_sems, after)
    return res[0], res[1], list(res[2:2 + m]), res[2 + m]


def _half(ref, core, axis=0):
    hr = ref.shape[axis] // 2
    return pl.ds(core * hr, hr)


def _plan_gather_start(place, src, land):
    x, y, c = place
    chip = 2 * x + y
    return [(s.at[_half(s, c)], l.at[chip, _half(s, c)], (_flip(x, dx), _flip(y, dy), c))
            for s, l in zip(src, land) for dx, dy in CHIP_RELATIONS]


def _plan_gather_landed(place, src, land):
    x, y, c = place
    return [(s.at[_half(s, c)], l.at[2 * _flip(x, dx) + _flip(y, dy), _half(s, c)])
            for s, l in zip(src, land) for dx, dy in CHIP_RELATIONS]


def _plan_gather_relay(place, land):
    x, y, c = place
    out = []
    for l in land:
        for dx, dy in CHIP_RELATIONS:
            got = l.at[2 * _flip(x, dx) + _flip(y, dy), _half(l, c, 1)]
            out.append((got, got, (x, y, 1 - c)))
    return out


def _plan_gather_wait(place, src, land):
    x, y, c = place
    out = []
    for l in land:
        for dx, dy in CHIP_RELATIONS:
            got = l.at[2 * _flip(x, dx) + _flip(y, dy), _half(l, 1 - c, 1)]
            out.append((got, got))
    return out


def _plan_swap_start(place, src, land):
    x, y, c = place
    return [(s.at[:, _half(s, 1 - c, 1), :], l, (x, y, 1 - c)) for s, l in zip(src, land)]


def _plan_swap_wait(place, src, land):
    return [(s.at[:, _half(s, 0, 1), :], l) for s, l in zip(src, land)]


def _plan_exchange_start(place, src, land):
    x, y, c = place
    remote = []
    for s, l in zip(src, land):
        for k, (dx, dy) in enumerate(CHIP_RELATIONS):
            tx, ty = _flip(x, dx), _flip(y, dy)
            remote.append((s.at[2 * tx + ty], l.at[k], (tx, ty, c)))
    return remote


def _plan_exchange_wait(place, src, land):
    return [(s.at[0], l.at[k]) for s, l in zip(src, land) for k in range(3)]


def _plan_finish_start(place, src, land):
    x, y, c = place
    return [(l.at[c], l.at[c], (x, y, 1 - c)) for l in land]


def _plan_finish_wait(place, src, land):
    x, y, c = place
    return [(l.at[c], l.at[1 - c]) for l in land]


def _grad_swap_halves(grads, dmod):
    n = len(grads)

    def body(*refs):
        ins, dmod_ref = refs[:n], refs[n]
        outs, dall_ref = refs[n + 1:2 * n + 1], refs[2 * n + 1]
        send_sems, recv_sems, dsend, drecv, dloc = refs[2 * n + 2:]
        x, y, c = _place()
        me = 4 * x + 2 * y + c
        sends = []
        for w in range(n):
            hr = ins[w].shape[1] // 2
            cp = _remote(ins[w].at[:, pl.ds((1 - c) * hr, hr), :], outs[w], send_sems.at[w], recv_sems.at[w],
                         (x, y, 1 - c))
            cp.start()
            sends.append(cp)
        own = pltpu.make_async_copy(dmod_ref, dall_ref.at[me], dloc)
        own.start()
        for k, (dx, dy, dc) in enumerate(RELATIONS):
            cp = _remote(dmod_ref, dall_ref.at[me], dsend.at[k], drecv.at[k],
                         (_flip(x, dx), _flip(y, dy), _flip(c, dc)))
            cp.start()
            sends.append(cp)
        for k, (dx, dy, dc) in enumerate(RELATIONS):
            src = 4 * _flip(x, dx) + 2 * _flip(y, dy) + _flip(c, dc)
            _remote(dmod_ref, dall_ref.at[src], dsend.at[k], drecv.at[k], (x, y, c)).wait_recv()
        for w in range(n):
            _remote(outs[w], outs[w], send_sems.at[w], recv_sems.at[w], (x, y, c)).wait_recv()
        own.wait()
        for cp in sends:
            cp.wait_send()

    out_shape = [jax.ShapeDtypeStruct((N_CHIPS, g.shape[1] // 2, g.shape[2]), F32) for g in grads]
    out_shape.append(jax.ShapeDtypeStruct((8,) + dmod.shape, F32))
    res = pl.pallas_call(
        body, name="grad_swap_halves",
        out_shape=out_shape, in_specs=[ANY] * n + [VMEM_SPEC], out_specs=[ANY] * (n + 1),
        scratch_shapes=[pltpu.SemaphoreType.DMA((n,)), pltpu.SemaphoreType.DMA((n,)),
                        pltpu.SemaphoreType.DMA((7,)), pltpu.SemaphoreType.DMA((7,)), pltpu.SemaphoreType.DMA],
        compiler_params=_params(),
    )(*grads, dmod)
    return res[:n], res[n]


def _grad_finish(halves, small_half):
    n = len(halves)

    def body(*refs):
        ins, sm_ref = refs[:n], refs[n]
        outs, sall_ref = refs[n + 1:2 * n + 1], refs[2 * n + 1]
        send_sems, recv_sems, loc_sems, ssend, srecv, sloc = refs[2 * n + 2:]
        x, y, c = _place()
        chip = 2 * x + y
        local, sends = [], []
        for w in range(n):
            cp = pltpu.make_async_copy(ins[w], outs[w].at[c], loc_sems.at[w])
            cp.start()
            local.append(cp)
            cp = _remote(ins[w], outs[w].at[c], send_sems.at[w], recv_sems.at[w], (x, y, 1 - c))
            cp.start()
            sends.append(cp)
        cp = pltpu.make_async_copy(sm_ref, sall_ref.at[chip, c], sloc)
        cp.start()
        local.append(cp)
        for k, (dx, dy, dc) in enumerate(RELATIONS):
            cp = _remote(sm_ref, sall_ref.at[chip, c], ssend.at[k], srecv.at[k],
                         (_flip(x, dx), _flip(y, dy), _flip(c, dc)))
            cp.start()
            sends.append(cp)
        for k, (dx, dy, dc) in enumerate(RELATIONS):
            got = sall_ref.at[2 * _flip(x, dx) + _flip(y, dy), _flip(c, dc)]
            _remote(got, got, ssend.at[k], srecv.at[k], (x, y, c)).wait_recv()
        for w in range(n):
            got = outs[w].at[1 - c]
            _remote(got, got, send_sems.at[w], recv_sems.at[w], (x, y, c)).wait_recv()
        for cp in sends:
            cp.wait_send()
        for cp in local:
            cp.wait()

    out_shape = [jax.ShapeDtypeStruct((2,) + h.shape, F32) for h in halves]
    out_shape.append(jax.ShapeDtypeStruct((N_CHIPS, 2) + small_half.shape, F32))
    res = pl.pallas_call(
        body, name="grad_finish",
        out_shape=out_shape, in_specs=[VMEM_SPEC] * (n + 1), out_specs=[ANY] * (n + 1),
        scratch_shapes=[pltpu.SemaphoreType.DMA((n,)), pltpu.SemaphoreType.DMA((n,)), pltpu.SemaphoreType.DMA((n,)),
                        pltpu.SemaphoreType.DMA((7,)), pltpu.SemaphoreType.DMA((7,)), pltpu.SemaphoreType.DMA],
        compiler_params=_params(),
    )(*halves, small_half)
    return res[:n], res[n]


def _add_my_half(core, full, got, name):
    _, hr, cols = got.shape

    def body(core_ref, a_ref, b_ref, o_ref):
        o_ref[...] = a_ref[...] + b_ref[...]

    return pl.pallas_call(
        body, name=name,
        out_shape=jax.ShapeDtypeStruct(got.shape, F32),
        grid_spec=pltpu.PrefetchScalarGridSpec(
            num_scalar_prefetch=1, grid=(N_CHIPS,),
            in_specs=[pl.BlockSpec((None, hr, cols), lambda s, core_ref: (s, core_ref[0], 0)),
                      pl.BlockSpec((None, hr, cols), lambda s, core_ref: (s, 0, 0))],
            out_specs=pl.BlockSpec((None, hr, cols), lambda s, core_ref: (s, 0, 0))),
        compiler_params=_params(("arbitrary",)),
    )(core, full, got)


def _add_chips(chip, mine, got, name):
    _, hr, cols = mine.shape

    def body(chip_ref, a_ref, b_ref, o_ref):
        o_ref[...] = ((a_ref[...] + b_ref[0]) + b_ref[1]) + b_ref[2]

    return pl.pallas_call(
        body, name=name,
        out_shape=jax.ShapeDtypeStruct((hr, cols), F32),
        grid_spec=pltpu.PrefetchScalarGridSpec(
            num_scalar_prefetch=1, grid=(1,),
            in_specs=[pl.BlockSpec((None, hr, cols), lambda s, chip_ref: (chip_ref[0], 0, 0)),
                      pl.BlockSpec((3, hr, cols), lambda s, chip_ref: (0, 0, 0))],
            out_specs=pl.BlockSpec((hr, cols), lambda s, chip_ref: (0, 0))),
        compiler_params=_params(("arbitrary",)),
    )(chip, mine, got)


def _add_chips_into_pair(chip_core, mine, got, name):
    _, hr, cols = mine.shape

    def body(cc_ref, a_ref, b_ref, o_ref):
        o_ref[...] = ((a_ref[...] + b_ref[0]) + b_ref[1]) + b_ref[2]

    return pl.pallas_call(
        body, name=name,
        out_shape=jax.ShapeDtypeStruct((2, hr, cols), F32),
        grid_spec=pltpu.PrefetchScalarGridSpec(
            num_scalar_prefetch=1, grid=(1,),
            in_specs=[pl.BlockSpec((None, hr, cols), lambda s, cc_ref: (cc_ref[0], 0, 0)),
                      pl.BlockSpec((3, hr, cols), lambda s, cc_ref: (0, 0, 0))],
            out_specs=pl.BlockSpec((None, hr, cols), lambda s, cc_ref: (cc_ref[1], 0, 0))),
        compiler_params=_params(("arbitrary",)),
    )(chip_core, mine, got)


def _place_shards(chip, shards):
    n = len(shards)

    def body(chip_ref, *refs):
        for w in range(n):
            refs[n + w][...] = refs[w][...]

    return pl.pallas_call(
        body, name="place_shards",
        out_shape=[jax.ShapeDtypeStruct((N_CHIPS,) + s.shape, s.dtype) for s in shards],
        grid_spec=pltpu.PrefetchScalarGridSpec(
            num_scalar_prefetch=1, grid=(1,),
            in_specs=[pl.BlockSpec(s.shape, lambda i, chip_ref: (0, 0)) for s in shards],
            out_specs=[pl.BlockSpec((None,) + s.shape, lambda i, chip_ref: (chip_ref[0], 0, 0)) for s in shards]),
        compiler_params=_params(("arbitrary",)),
    )(chip, *shards)


def _rope_tables(pos_col, freqs):
    S = pos_col.shape[0]
    T = _row_tile(S, 1024)

    def body(p_ref, f_ref, cos_ref, sin_ref):
        ang = p_ref[...].astype(F32) * f_ref[...]
        cos_ref[...] = jnp.cos(ang)
        sin_ref[...] = jnp.sin(ang)

    return pl.pallas_call(
        body, name="rope_tables", grid=(S // T,),
        out_shape=[jax.ShapeDtypeStruct((S, 128), F32)] * 2,
        in_specs=[pl.BlockSpec((T, 1), lambda i: (i, 0)), pl.BlockSpec((1, 128), lambda i: (0, 0))],
        out_specs=[pl.BlockSpec((T, 128), lambda i: (i, 0))] * 2,
        compiler_params=_params(("parallel",)),
    )(pos_col, freqs)


def _full(shape):
    zeros = (0,) * len(shape)
    return pl.BlockSpec(shape, lambda *_: zeros)


def _pre_attention(x, mod6, g_mix, g_q, g_kv, w_in, w_uq, w_uk_t, cos, sin, T, TQ):
    S = x.shape[0]

    def body(x_ref, mod_ref, gm_ref, gq_ref, gkv_ref, win_ref, wuq_ref, wuk_ref, cos_ref, sin_ref,
             proj_ref, q_ref, qc_ref, kc_ref, kct_ref):
        xh, _ = _rms(x_ref[...])
        h1 = ((xh * gm_ref[...]) * (1.0 + mod_ref[1:2, :]) + mod_ref[0:1, :]).astype(BF16)
        rows_in = D_MODEL // N_CHIPS
        proj = _dot_nt(h1[:, 0:rows_in], win_ref[0])
        for j in range(1, N_CHIPS):
            proj = proj + _dot_nt(h1[:, j * rows_in:(j + 1) * rows_in], win_ref[j])
        proj_ref[...] = proj
        cqh, _ = _rms(proj[:, :Q_LORA])
        c_q = cqh * gq_ref[...]
        ckvh, _ = _rms(proj[:, O_CKV:O_KR])
        c_kv = ckvh * gkv_ref[...]
        q = _dot(c_q, wuq_ref[...])
        q_ref[...] = q
        cos_t, sin_t = cos_ref[...], sin_ref[...]
        ropes = (_rope(q[:, O_QA:O_QB], cos_t, sin_t), _rope(q[:, O_QB:Q_W], cos_t, sin_t))
        low = lax.broadcasted_iota(jnp.int32, (T, 128), 1) < ROPE
        for h in range(HEADS):
            q_lat = _dot_nt(q[:, h * NOPE:(h + 1) * NOPE], wuk_ref[h])
            keep = low if h % 2 == 0 else jnp.logical_not(low)
            qc_ref[h, :, 0:KV_LORA] = q_lat.astype(BF16)
            qc_ref[h, :, KV_LORA:QK_PAD] = jnp.where(keep, ropes[h // 2], 0.0).astype(BF16)
        k_rope = _rope(proj[:, O_KR:O_U], cos_t, sin_t)
        kc_ref[:, 0:KV_LORA] = c_kv.astype(BF16)
        kc_ref[:, KV_LORA:QK_PAD] = k_rope.astype(BF16)
        lat_t, rope_t = jnp.transpose(c_kv), jnp.transpose(k_rope)
        for s in range(T // TQ):
            kct_ref[s, 0:KV_LORA, :] = lat_t[:, s * TQ:(s + 1) * TQ].astype(BF16)
            kct_ref[s, KV_LORA:QK_PAD, :] = rope_t[:, s * TQ:(s + 1) * TQ].astype(BF16)

    row = lambda w: pl.BlockSpec((T, w), lambda i: (i, 0))
    return pl.pallas_call(
        body, name="pre_attention", grid=(S // T,),
        out_shape=[jax.ShapeDtypeStruct((S, PROJ_W), F32), jax.ShapeDtypeStruct((S, Q_W), F32),
                   jax.ShapeDtypeStruct((HEADS, S, QK_PAD), BF16), jax.ShapeDtypeStruct((S, QK_PAD), BF16),
                   jax.ShapeDtypeStruct((S // TQ, QK_PAD, TQ), BF16)],
        in_specs=[row(D_MODEL), _full((N_MOD, D_MODEL)), _full((1, D_MODEL)), _full((1, Q_LORA)), _full((1, KV_LORA)),
                  _full((N_CHIPS, PROJ_W, D_MODEL // N_CHIPS)), _full((Q_LORA, Q_W)), _full((HEADS, KV_LORA, NOPE)),
                  row(128), row(128)],
        out_specs=[row(PROJ_W), row(Q_W), pl.BlockSpec((HEADS, T, QK_PAD), lambda i: (0, i, 0)), row(QK_PAD),
                   pl.BlockSpec((T // TQ, QK_PAD, TQ), lambda i: (i, 0, 0))],
        compiler_params=_params(("parallel",)),
    )(x, mod6, g_mix, g_q, g_kv, w_in, w_uq, w_uk_t, cos, sin)


def _diag_mask(TQ, width):
    key = lax.broadcasted_iota(jnp.int32, (TQ, width), 0) >> CHUNK_SHIFT
    qry = (lax.broadcasted_iota(jnp.int32, (TQ, width), 1) & (TQ - 1)) >> CHUNK_SHIFT
    return key <= qry


def _col_to_row(col):
    return jnp.transpose(jnp.broadcast_to(col, (col.shape[0], 128)))[0:1, :]


def _attention_fwd(qc, kc, kct, w_uv_t, TQ):
    S = kc.shape[0]
    R = HEADS * TQ
    nq = S // TQ

    def body(q_ref, k_ref, kt_ref, wuv_ref, o_ref, y_ref, lser_ref, m_s, l_s, acc_s, st_s):
        i = pl.program_id(0)
        q = q_ref[...].reshape(R, QK_PAD)
        m_s[...] = jnp.full((1, R), -jnp.inf, F32)
        l_s[...] = jnp.zeros((1, R), F32)
        acc_s[...] = jnp.zeros((KV_LORA, R), F32)

        def scores(j):
            return _dot_nt(k_ref[pl.ds(pl.multiple_of(j * TQ, TQ), TQ), :], q) * SM_SCALE

        def update(j, st):
            m_old = m_s[...]
            m_new = jnp.maximum(m_old, jnp.max(st, axis=0, keepdims=True))
            pt = jnp.exp(st - m_new)
            alpha = jnp.exp(m_old - m_new)
            l_s[...] = alpha * l_s[...] + jnp.sum(pt, axis=0, keepdims=True)
            acc_s[...] = alpha * acc_s[...] + _dot(kt_ref[j, 0:KV_LORA, :], pt)
            m_s[...] = m_new

        st_s[...] = scores(0)

        def loop(j, carry):
            st = st_s[...]
            st_s[...] = scores(j + 1)
            update(j, st)
            return carry

        lax.fori_loop(0, i, loop, 0)
        update(i, jnp.where(_diag_mask(TQ, R), st_s[...], -jnp.inf))
        l = l_s[...]
        lser_ref[0] = m_s[...] + jnp.log(l)
        o = jnp.transpose(acc_s[...] / l).astype(BF16)
        for h in range(HEADS):
            oh = o[h * TQ:(h + 1) * TQ, :]
            o_ref[h] = oh
            y_ref[:, h * 128:(h + 1) * 128] = _dot(oh, wuv_ref[h]).astype(BF16)

    return pl.pallas_call(
        body, name="attention_fwd", grid=(nq,),
        out_shape=[jax.ShapeDtypeStruct((HEADS, S, KV_LORA), BF16), jax.ShapeDtypeStruct((S, HEADS * 128), BF16),
                   jax.ShapeDtypeStruct((nq, 1, R), F32)],
        in_specs=[pl.BlockSpec((HEADS, TQ, QK_PAD), lambda i: (0, i, 0)), _full((S, QK_PAD)),
                  _full((nq, QK_PAD, TQ)), _full((HEADS, KV_LORA, 128))],
        out_specs=[pl.BlockSpec((HEADS, TQ, KV_LORA), lambda i: (0, i, 0)), pl.BlockSpec((TQ, HEADS * 128), lambda i: (i, 0)),
                   pl.BlockSpec((1, 1, R), lambda i: (i, 0, 0))],
        scratch_shapes=[pltpu.VMEM((1, R), F32), pltpu.VMEM((1, R), F32), pltpu.VMEM((KV_LORA, R), F32),
                        pltpu.VMEM((TQ, R), F32)],
        compiler_params=_params(("parallel",)),
    )(qc, kc, kct, w_uv_t)


def _pool_forward(proj):
    S = proj.shape[0]
    RB = _row_tile(S, 256)

    def body(proj_ref, out_ref, pad_ref, sem):
        cp = pltpu.make_async_copy(proj_ref.at[:, pl.ds(O_U, POOL_W)], pad_ref.at[pl.ds(POOL_PAD, S)], sem)
        cp.start()
        pad_ref[0:POOL_PAD, :] = jnp.zeros((POOL_PAD, POOL_W), F32)
        cp.wait()
        for g, win in enumerate(POOL_WINDOWS):
            cols = slice(g * POOL_GROUP, (g + 1) * POOL_GROUP)
            for r0 in range(0, S, RB):
                u = pad_ref[POOL_PAD + r0:POOL_PAD + r0 + RB, cols]
                acc = u
                for k in range(1, win):
                    acc = acc + pad_ref[POOL_PAD + r0 - k:POOL_PAD + r0 - k + RB, cols]
                if r0 == 0:
                    t1 = (lax.broadcasted_iota(jnp.int32, (RB, POOL_GROUP), 0) + 1).astype(F32)
                    mean = acc / jnp.minimum(t1, float(win))
                else:
                    mean = acc * (1.0 / win)
                out_ref[r0:r0 + RB, cols] = (mean - u).astype(BF16)

    return pl.pallas_call(
        body, name="pool_forward",
        out_shape=jax.ShapeDtypeStruct((S, POOL_W), BF16),
        in_specs=[ANY], out_specs=VMEM_SPEC,
        scratch_shapes=[pltpu.VMEM((S + POOL_PAD, POOL_W), F32), pltpu.SemaphoreType.DMA],
        compiler_params=_params(),
    )(proj)


def _pool_backward(dpooled, after):
    S = dpooled.shape[0]
    RB = _row_tile(S, 256)

    def body(dp_ref, after_ref, out_ref, pad_ref, sem):
        cp = pltpu.make_async_copy(dp_ref, pad_ref.at[pl.ds(0, S)], sem)
        cp.start()
        pad_ref[S:S + POOL_PAD, :] = jnp.zeros((POOL_PAD, POOL_W), F32)
        cp.wait()
        for g, win in enumerate(POOL_WINDOWS):
            cols = slice(g * POOL_GROUP, (g + 1) * POOL_GROUP)
            head = pad_ref[0:POOL_PAD, cols]
            t1 = (lax.broadcasted_iota(jnp.int32, (POOL_PAD, POOL_GROUP), 0) + 1).astype(F32)
            pad_ref[0:POOL_PAD, cols] = head * (float(win) / jnp.minimum(t1, float(win)))
            for r0 in range(0, S, RB):
                acc = pad_ref[r0:r0 + RB, cols]
                for k in range(1, win):
                    acc = acc + pad_ref[r0 + k:r0 + k + RB, cols]
                own = pad_ref[r0:r0 + RB, cols]
                if r0 == 0:
                    own = jnp.concatenate([head, own[POOL_PAD:]], axis=0)
                out_ref[r0:r0 + RB, cols] = acc * (1.0 / win) - own

    return pl.pallas_call(
        body, name="pool_backward",
        out_shape=jax.ShapeDtypeStruct((S, POOL_W), F32),
        in_specs=[ANY, ANY], out_specs=VMEM_SPEC,
        scratch_shapes=[pltpu.VMEM((S + POOL_PAD, POOL_W), F32), pltpu.SemaphoreType.DMA],
        compiler_params=_params(),
    )(dpooled, after)


def _mix_out(y_mla, pooled, w_pool, pool_scale, w_o, x, mod6, T):
    S = x.shape[0]

    def body(ym_ref, pl_ref, wp_ref, ps_ref, wo_ref, x_ref, mod_ref, x1_ref, mix_ref, mi_ref):
        mi_ref[:, 0:512] = ym_ref[...]
        for g in range(len(POOL_WINDOWS)):
            cols = slice(g * POOL_GROUP, (g + 1) * POOL_GROUP)
            z = _dot(pl_ref[:, cols], wp_ref[g])
            mi_ref[:, 512 + g * POOL_GROUP:512 + (g + 1) * POOL_GROUP] = (z * ps_ref[:, cols]).astype(BF16)
        mix = _dot(mi_ref[...], wo_ref[...])
        mix_ref[...] = mix
        x1_ref[...] = x_ref[...] + mod_ref[2:3, :] * mix

    row = lambda w: pl.BlockSpec((T, w), lambda i: (i, 0))
    return pl.pallas_call(
        body, name="mix_out", grid=(S // T,),
        out_shape=[jax.ShapeDtypeStruct((S, D_MODEL), F32), jax.ShapeDtypeStruct((S, D_MODEL), F32),
                   jax.ShapeDtypeStruct((S, 1024), BF16)],
        in_specs=[row(512), row(POOL_W), _full((4, POOL_GROUP, POOL_GROUP)), _full((1, POOL_W)),
                  _full((1024, D_MODEL)), row(D_MODEL), _full((N_MOD, D_MODEL))],
        out_specs=[row(D_MODEL), row(D_MODEL), row(1024)],
        compiler_params=_params(("parallel",)),
    )(y_mla, pooled, w_pool, pool_scale, w_o, x, mod6)


def _ffn_forward(x1, mod6, g_ffn, g_final, target, w_gate, w_up, w_down, T):
    S = x1.shape[0]

    def body(x1_ref, mod_ref, gf_ref, gl_ref, tgt_ref, wg_ref, wu_ref, wd_ref,
             gate_ref, up_ref, act_ref, h2_ref, dff_ref, dx2_ref, st_ref, acc_s):
        i, j = pl.program_id(0), pl.program_id(1)

        @pl.when(jnp.logical_and(i == 0, j == 0))
        def _():
            st_ref[...] = jnp.zeros_like(st_ref)

        @pl.when(j == 0)
        def _():
            xh, _ = _rms(x1_ref[...])
            h2_ref[...] = ((xh * gf_ref[...]) * (1.0 + mod_ref[4:5, :]) + mod_ref[3:4, :]).astype(BF16)
            acc_s[...] = jnp.zeros_like(acc_s)

        h2 = h2_ref[...]
        gate = _dot_nt(h2, wg_ref[j])
        up = _dot_nt(h2, wu_ref[j])
        gate_ref[...] = gate.astype(BF16)
        up_ref[...] = up.astype(BF16)
        act = (gate * jax.nn.sigmoid(gate) * up).astype(BF16)
        act_ref[...] = act
        acc_s[...] += _dot(act, wd_ref[j])

        @pl.when(j == N_CHIPS - 1)
        def _():
            ff = acc_s[...]
            x2 = x1_ref[...] + mod_ref[5:6, :] * ff
            xh, r3 = _rms(x2)
            err = xh * gl_ref[...] - tgt_ref[...]
            dy = err * (1.0 / D_MODEL)
            dx2 = _rms_bwd(dy * gl_ref[...], xh, r3)
            dx2_ref[...] = dx2
            dff_ref[...] = (dx2 * mod_ref[5:6, :]).astype(BF16)
            st_ref[0:1, :] += jnp.sum(dy * xh, axis=0, keepdims=True)
            st_ref[1:2, :] += jnp.sum(dx2 * ff, axis=0, keepdims=True)
            st_ref[2:3, :] += 0.5 * jnp.sum(err * dy)

    row = pl.BlockSpec((T, D_MODEL), lambda i, j: (i, 0))
    chunk_out = pl.BlockSpec((None, T, FF_CHUNK), lambda i, j: (j, i, 0))
    big = jax.ShapeDtypeStruct((N_CHIPS, S, FF_CHUNK), BF16)
    wide = jax.ShapeDtypeStruct((S, D_MODEL), BF16)
    return pl.pallas_call(
        body, name="ffn_forward", grid=(S // T, N_CHIPS),
        out_shape=[big, big, big, wide, wide, jax.ShapeDtypeStruct((S, D_MODEL), F32),
                   jax.ShapeDtypeStruct((8, D_MODEL), F32)],
        in_specs=[row, _full((N_MOD, D_MODEL)), _full((1, D_MODEL)), _full((1, D_MODEL)), row,
                  VMEM_SPEC, VMEM_SPEC, VMEM_SPEC],
        out_specs=[chunk_out, chunk_out, chunk_out, row, row, row, _full((8, D_MODEL))],
        scratch_shapes=[pltpu.VMEM((T, D_MODEL), F32)],
        compiler_params=_params(("arbitrary", "arbitrary")),
    )(x1, mod6, g_ffn, g_final, target, w_gate, w_up, w_down)


def _ffn_backward(dx2, x1, dff, gate, up, mod6, g_ffn, w_gate, w_up, w_down, T):
    S = x1.shape[0]

    def body(dx2_ref, x1_ref, dff_ref, gate_ref, up_ref, mod_ref, gf_ref, wg_ref, wu_ref, wd_ref,
             dgate_ref, dup_ref, dx1_ref, st_ref, acc_s):
        i, j = pl.program_id(0), pl.program_id(1)

        @pl.when(jnp.logical_and(i == 0, j == 0))
        def _():
            st_ref[...] = jnp.zeros_like(st_ref)

        @pl.when(j == 0)
        def _():
            acc_s[...] = jnp.zeros_like(acc_s)

        for r0 in range(0, T, T // 2):
            rows = slice(r0, r0 + T // 2)
            gate, up = gate_ref[rows, :].astype(F32), up_ref[rows, :].astype(F32)
            sg = jax.nn.sigmoid(gate)
            dact = _dot_nt(dff_ref[rows, :], wd_ref[j])
            dup = (dact * (gate * sg)).astype(BF16)
            dgate = (dact * up * (sg * (1.0 + gate * (1.0 - sg)))).astype(BF16)
            dup_ref[rows, :] = dup
            dgate_ref[rows, :] = dgate
            acc_s[rows, :] += _dot(dgate, wg_ref[j]) + _dot(dup, wu_ref[j])

        @pl.when(j == N_CHIPS - 1)
        def _():
            dh2 = acc_s[...]
            xh, r2 = _rms(x1_ref[...])
            n2 = xh * gf_ref[...]
            st_ref[0:1, :] += jnp.sum(dh2, axis=0, keepdims=True)
            st_ref[1:2, :] += jnp.sum(dh2 * n2, axis=0, keepdims=True)
            dn2 = dh2 * (1.0 + mod_ref[4:5, :])
            st_ref[2:3, :] += jnp.sum(dn2 * xh, axis=0, keepdims=True)
            dx1_ref[...] = _rms_bwd(dn2 * gf_ref[...], xh, r2) + dx2_ref[...]

    row = pl.BlockSpec((T, D_MODEL), lambda i, j: (i, 0))
    chunk = pl.BlockSpec((None, T, FF_CHUNK), lambda i, j: (j, i, 0))
    big = jax.ShapeDtypeStruct((N_CHIPS, S, FF_CHUNK), BF16)
    return pl.pallas_call(
        body, name="ffn_backward", grid=(S // T, N_CHIPS),
        out_shape=[big, big, jax.ShapeDtypeStruct((S, D_MODEL), F32), jax.ShapeDtypeStruct((8, D_MODEL), F32)],
        in_specs=[row, row, row, chunk, chunk, _full((N_MOD, D_MODEL)), _full((1, D_MODEL)),
                  VMEM_SPEC, VMEM_SPEC, VMEM_SPEC],
        out_specs=[chunk, chunk, row, _full((8, D_MODEL))],
        scratch_shapes=[pltpu.VMEM((T, D_MODEL), F32)],
        compiler_params=_params(("arbitrary", "arbitrary")),
    )(dx2, x1, dff, gate, up, mod6, g_ffn, w_gate, w_up, w_down)


def _tn_matmul(a, b, a_spec, b_spec, groups, m, n, steps, name):
    def body(a_ref, b_ref, o_ref):
        @pl.when(pl.program_id(1) == 0)
        def _():
            o_ref[...] = jnp.zeros_like(o_ref)

        o_ref[...] += _dot_tn(a_ref[...], b_ref[...])

    return pl.pallas_call(
        body, name=name, grid=(groups, steps),
        out_shape=jax.ShapeDtypeStruct((groups, m, n), F32),
        in_specs=[a_spec, b_spec],
        out_specs=pl.BlockSpec((None, m, n), lambda g, i: (g, 0, 0)),
        compiler_params=_params(("parallel", "arbitrary")),
    )(a, b)


def _mix_backward(dx1, mix, mod6, w_o, pooled, w_pool, pool_scale, w_uv_t, o_lat, T, TQ):
    S = dx1.shape[0]

    def body(dx1_ref, mix_ref, mod_ref, wo_ref, pl_ref, wp_ref, ps_ref, wuv_ref, o_ref,
             dmix_ref, dp_ref, do_ref, dr_ref, gp_ref, guv_ref, st_ref):
        @pl.when(pl.program_id(0) == 0)
        def _():
            st_ref[...] = jnp.zeros_like(st_ref)
            gp_ref[...] = jnp.zeros_like(gp_ref)
            guv_ref[...] = jnp.zeros_like(guv_ref)

        dx1 = dx1_ref[...]
        st_ref[0:1, :] += jnp.sum(dx1 * mix_ref[...], axis=0, keepdims=True)
        dmix = (dx1 * mod_ref[2:3, :]).astype(BF16)
        dmix_ref[...] = dmix
        dmi = _dot_nt(dmix, wo_ref[...])
        dym = dmi[:, 0:512].astype(BF16)
        for g in range(len(POOL_WINDOWS)):
            cols = slice(g * POOL_GROUP, (g + 1) * POOL_GROUP)
            dyp = dmi[:, 512 + g * POOL_GROUP:512 + (g + 1) * POOL_GROUP]
            pooled_g = pl_ref[:, cols]
            z = _dot(pooled_g, wp_ref[g])
            st_ref[1:2, cols] += jnp.sum(dyp * z, axis=0, keepdims=True)
            dz = (dyp * ps_ref[:, cols]).astype(BF16)
            gp_ref[g] += _dot_tn(pooled_g, dz)
            dp_ref[:, cols] = _dot_nt(dz, wp_ref[g])
        for h in range(HEADS):
            dym_h = dym[:, h * 128:(h + 1) * 128]
            do = _dot_nt(dym_h, wuv_ref[h]).astype(BF16)
            do_ref[h] = do
            o_h = o_ref[h]
            guv_ref[h] += _dot_tn(o_h, dym_h)
            delta = _col_to_row(jnp.sum(do.astype(F32) * o_h.astype(F32), axis=1, keepdims=True))
            for s in range(T // TQ):
                dr_ref[s, :, h * TQ:(h + 1) * TQ] = delta[:, s * TQ:(s + 1) * TQ]

    row = lambda w: pl.BlockSpec((T, w), lambda i: (i, 0))
    heads = pl.BlockSpec((HEADS, T, KV_LORA), lambda i: (0, i, 0))
    square = jax.ShapeDtypeStruct((4, 128, 128), F32)
    return pl.pallas_call(
        body, name="mix_backward", grid=(S // T,),
        out_shape=[jax.ShapeDtypeStruct((S, D_MODEL), BF16),
                   jax.ShapeDtypeStruct((S, POOL_W), F32),
                   jax.ShapeDtypeStruct((HEADS, S, KV_LORA), BF16), jax.ShapeDtypeStruct((S // TQ, 1, HEADS * TQ), F32),
                   square, square, jax.ShapeDtypeStruct((8, D_MODEL), F32)],
        in_specs=[row(D_MODEL), row(D_MODEL), _full((N_MOD, D_MODEL)), _full((1024, D_MODEL)), row(POOL_W),
                  _full((4, POOL_GROUP, POOL_GROUP)), _full((1, POOL_W)), _full((HEADS, KV_LORA, 128)), heads],
        out_specs=[row(D_MODEL), row(POOL_W), heads,
                   pl.BlockSpec((T // TQ, 1, HEADS * TQ), lambda i: (i, 0, 0)), _full((4, 128, 128)),
                   _full((4, 128, 128)), _full((8, D_MODEL))],
        compiler_params=_params(("arbitrary",)),
    )(dx1, mix, mod6, w_o, pooled, w_pool, pool_scale, w_uv_t, o_lat)


def _attention_bwd(qc, kc, kct, do, lse_rows, delta_rows, TQ):
    S = kc.shape[0]
    R = HEADS * TQ
    nq = S // TQ

    def body(k_ref, kt_ref, q_ref, do_ref, lser_ref, dr_ref, dk_ref, dqt_ref, dk_s, dv_s):
        j = pl.program_id(0)

        @pl.when(j == 0)
        def _():
            def zero(i, carry):
                dqt_ref[i] = jnp.zeros((QK_PAD, R), F32)
                return carry
            lax.fori_loop(0, nq, zero, 0)

        k = k_ref[...]
        kt = kt_ref[...]
        v = k[:, :KV_LORA]
        dk_s[...] = jnp.zeros((TQ, QK_PAD), F32)
        dv_s[...] = jnp.zeros((TQ, KV_LORA), F32)

        def step(i, masked):
            rows = pl.ds(pl.multiple_of(i * TQ, TQ), TQ)
            q = q_ref[:, rows, :].reshape(R, QK_PAD)
            do = do_ref[:, rows, :].reshape(R, KV_LORA)
            st = _dot_nt(k, q) * SM_SCALE
            if masked:
                st = jnp.where(_diag_mask(TQ, R), st, -jnp.inf)
            pt = jnp.exp(st - lser_ref[i])
            dv_s[...] += _dot(pt, do)
            dpt = _dot_nt(v, do)
            dst = (pt * (dpt - dr_ref[i])).astype(BF16)
            dk_s[...] += _dot(dst, q)
            dqt_ref[i] += _dot(kt, dst)

        def loop(i, carry):
            step(i, False)
            return carry

        step(j, True)
        lax.fori_loop(j + 1, nq, loop, 0)
        dk = dk_s[...] * SM_SCALE
        dk_ref[:, 0:KV_LORA] = dk[:, 0:KV_LORA] + dv_s[...]
        dk_ref[:, KV_LORA:QK_PAD] = dk[:, KV_LORA:QK_PAD]

    return pl.pallas_call(
        body, name="attention_bwd", grid=(nq,),
        out_shape=[jax.ShapeDtypeStruct((S, QK_PAD), F32), jax.ShapeDtypeStruct((nq, QK_PAD, R), F32)],
        in_specs=[pl.BlockSpec((TQ, QK_PAD), lambda j: (j, 0)), pl.BlockSpec((None, QK_PAD, TQ), lambda j: (j, 0, 0)),
                  VMEM_SPEC, VMEM_SPEC, VMEM_SPEC, VMEM_SPEC],
        out_specs=[pl.BlockSpec((TQ, QK_PAD), lambda j: (j, 0)), VMEM_SPEC],
        scratch_shapes=[pltpu.VMEM((TQ, QK_PAD), F32), pltpu.VMEM((TQ, KV_LORA), F32)],
        compiler_params=_params(("arbitrary",)),
    )(kc, kct, qc, do, lse_rows, delta_rows)


def _pre_attention_backward(x, dx1, proj, q, dqt, dkc, du, cos, sin, mod6, g_mix, g_q, g_kv, w_in, w_uq, w_uk_t, T, TQ):
    S = x.shape[0]

    def body(x_ref, dx1_ref, proj_ref, q_ref, dqt_ref, dkc_ref, du_ref, cos_ref, sin_ref, mod_ref, gm_ref, gq_ref,
             gkv_ref, win_ref, wuq_ref, wuk_ref, gx_ref, dproj_ref, h1_ref, guk_ref, guq_ref, st_ref, dq_ref):
        @pl.when(pl.program_id(0) == 0)
        def _():
            st_ref[...] = jnp.zeros_like(st_ref)
            guk_ref[...] = jnp.zeros_like(guk_ref)
            guq_ref[...] = jnp.zeros_like(guq_ref)

        cos_t, sin_t = cos_ref[...], sin_ref[...]
        low = lax.broadcasted_iota(jnp.int32, (T, 128), 1) < ROPE
        rope_parts = []
        for h in range(HEADS):
            dqc = jnp.concatenate([jnp.transpose(dqt_ref[s, :, h * TQ:(h + 1) * TQ]) for s in range(T // TQ)], axis=0)
            dqc = dqc * SM_SCALE
            dql = dqc[:, 0:KV_LORA].astype(BF16)
            guk_ref[h] += _dot_tn(dql, q_ref[:, h * NOPE:(h + 1) * NOPE])
            dq_ref[:, h * NOPE:(h + 1) * NOPE] = _dot(dql, wuk_ref[h]).astype(BF16)
            rope_parts.append(dqc[:, KV_LORA:QK_PAD])
        for pair in range(2):
            d = jnp.where(low, rope_parts[2 * pair], rope_parts[2 * pair + 1])
            dq_ref[:, O_QA + 128 * pair:O_QA + 128 * (pair + 1)] = _rope_bwd(d, cos_t, sin_t).astype(BF16)
        dq = dq_ref[...]
        dcq = _dot_nt(dq, wuq_ref[...])
        cqh, rq = _rms(proj_ref[:, 0:Q_LORA])
        guq_ref[...] += _dot_tn(cqh * gq_ref[...], dq)
        st_ref[3:4, 0:Q_LORA] += jnp.sum(dcq * cqh, axis=0, keepdims=True)
        dproj_ref[:, 0:Q_LORA] = _rms_bwd(dcq * gq_ref[...], cqh, rq).astype(BF16)
        dckv = dkc_ref[:, 0:KV_LORA]
        ckvh, rkv = _rms(proj_ref[:, O_CKV:O_KR])
        st_ref[4:5, 0:KV_LORA] += jnp.sum(dckv * ckvh, axis=0, keepdims=True)
        dproj_ref[:, O_CKV:O_KR] = _rms_bwd(dckv * gkv_ref[...], ckvh, rkv).astype(BF16)
        dkr = _rope_bwd(dkc_ref[:, KV_LORA:QK_PAD], cos_t, sin_t)
        dkr = jnp.where(low, dkr + pltpu.roll(dkr, ROPE, 1), 0.0)
        dproj_ref[:, O_KR:O_U] = dkr.astype(BF16)
        dproj_ref[:, O_U:PROJ_W] = du_ref[...].astype(BF16)
        dproj = dproj_ref[...]
        dh1 = jnp.concatenate([_dot(dproj, win_ref[j]) for j in range(N_CHIPS)], axis=1)
        xh, r1 = _rms(x_ref[...])
        n1 = xh * gm_ref[...]
        h1_ref[...] = (n1 * (1.0 + mod_ref[1:2, :]) + mod_ref[0:1, :]).astype(BF16)
        st_ref[0:1, :] += jnp.sum(dh1, axis=0, keepdims=True)
        st_ref[1:2, :] += jnp.sum(dh1 * n1, axis=0, keepdims=True)
        dn1 = dh1 * (1.0 + mod_ref[1:2, :])
        st_ref[2:3, :] += jnp.sum(dn1 * xh, axis=0, keepdims=True)
        gx_ref[...] = _rms_bwd(dn1 * gm_ref[...], xh, r1) + dx1_ref[...]

    row = lambda w: pl.BlockSpec((T, w), lambda i: (i, 0))
    return pl.pallas_call(
        body, name="pre_attention_backward", grid=(S // T,),
        out_shape=[jax.ShapeDtypeStruct((S, D_MODEL), F32), jax.ShapeDtypeStruct((S, PROJ_W), BF16),
                   jax.ShapeDtypeStruct((S, D_MODEL), BF16), jax.ShapeDtypeStruct((HEADS, KV_LORA, NOPE), F32),
                   jax.ShapeDtypeStruct((Q_LORA, Q_W), F32), jax.ShapeDtypeStruct((8, D_MODEL), F32)],
        in_specs=[row(D_MODEL), row(D_MODEL), row(PROJ_W), row(HEADS * NOPE),
                  pl.BlockSpec((T // TQ, QK_PAD, HEADS * TQ), lambda i: (i, 0, 0)),
                  row(QK_PAD), row(POOL_W), row(128), row(128), _full((N_MOD, D_MODEL)), _full((1, D_MODEL)),
                  _full((1, Q_LORA)), _full((1, KV_LORA)), _full((N_CHIPS, PROJ_W, D_MODEL // N_CHIPS)),
                  _full((Q_LORA, Q_W)), _full((HEADS, KV_LORA, NOPE))],
        out_specs=[row(D_MODEL), row(PROJ_W), row(D_MODEL), _full((HEADS, KV_LORA, NOPE)), _full((Q_LORA, Q_W)),
                   _full((8, D_MODEL))],
        scratch_shapes=[pltpu.VMEM((T, Q_W), BF16)],
        compiler_params=_params(("arbitrary",)),
    )(x, dx1, proj, q, dqt, dkc, du, cos, sin, mod6, g_mix, g_q, g_kv, w_in, w_uq, w_uk_t)


def _ada_grads(c_all, dmod_all, chip):
    cols = N_MOD * D_MODEL // N_CHIPS
    width = dmod_all.shape[1]

    def body(col_ref, c_ref, dcol_ref, dall_ref, gw_ref, gb_ref):
        call = c_ref[...]
        act = call * jax.nn.sigmoid(call)
        gw_ref[...] = _dot_tn(act, dcol_ref[...])
        d = dall_ref[...]
        acc = d[0:1, :]
        for b in range(1, 8):
            acc = acc + d[b:b + 1, :]
        gb_ref[...] = acc

    return pl.pallas_call(
        body, name="ada_grads",
        out_shape=[jax.ShapeDtypeStruct((D_MODEL, cols), F32), jax.ShapeDtypeStruct((1, width), F32)],
        grid_spec=pltpu.PrefetchScalarGridSpec(
            num_scalar_prefetch=1, grid=(1,),
            in_specs=[pl.BlockSpec((8, D_MODEL), lambda s, col_ref: (0, 0)),
                      pl.BlockSpec((8, cols), lambda s, col_ref: (0, col_ref[0])),
                      pl.BlockSpec((8, width), lambda s, col_ref: (0, 0))],
            out_specs=[pl.BlockSpec((D_MODEL, cols), lambda s, col_ref: (0, 0)),
                       pl.BlockSpec((1, width), lambda s, col_ref: (0, 0))]),
        compiler_params=_params(("arbitrary",)),
    )(chip, c_all, dmod_all, dmod_all)


def _adamw(w, g, m, v, name):
    rows, cols = w.shape
    T = _row_tile(rows, 256)

    def body(w_ref, g_ref, m_ref, v_ref, d_ref, nm_ref, nv_ref):
        g = g_ref[...]
        m2 = ADAM_B1 * m_ref[...] + (1.0 - ADAM_B1) * g
        v2 = ADAM_B2 * v_ref[...] + (1.0 - ADAM_B2) * (g * g)
        m_hat = m2 / (1.0 - ADAM_B1 ** ADAM_STEP)
        v_hat = v2 / (1.0 - ADAM_B2 ** ADAM_STEP)
        d_ref[...] = -ADAM_LR * (m_hat / (jnp.sqrt(v_hat) + ADAM_EPS) + ADAM_WD * w_ref[...])
        nm_ref[...] = m2
        nv_ref[...] = v2

    spec = pl.BlockSpec((T, cols), lambda i: (i, 0))
    return pl.pallas_call(
        body, name=name, grid=(rows // T,),
        out_shape=[jax.ShapeDtypeStruct((rows, cols), F32)] * 3,
        in_specs=[spec] * 4, out_specs=[spec] * 3,
        compiler_params=_params(("parallel",)),
    )(w, g, m, v)


SMALL_NAMES = ("w_uk", "w_uv", "w_pool", "g_mix", "g_q", "g_kv", "pool_scale", "g_ffn", "g_final", "b_ada")
SMALL_ROWS = 1664


def _pack_rows(parts):
    flat = jnp.concatenate([p.reshape(-1) for p in parts])
    pad = (-flat.shape[0]) % 128
    if pad:
        flat = jnp.concatenate([flat, jnp.zeros((pad,), F32)])
    return flat.reshape(-1, 128)


def kernel(x, c, positions, w_ada, b_ada, g_mix, w_in, g_q, g_kv, w_uq, w_uk, w_uv, w_pool, pool_scale, w_o, g_ffn, w_gate, w_up, w_down, g_final, loss_target, m_w_ada, m_b_ada, m_g_mix, m_w_in, m_g_q, m_g_kv, m_w_uq, m_w_uk, m_w_uv, m_w_pool, m_pool_scale, m_w_o, m_g_ffn, m_w_gate, m_w_up, m_w_down, m_g_final, v_w_ada, v_b_ada, v_g_mix, v_w_in, v_g_q, v_g_kv, v_w_uq, v_w_uk, v_w_uv, v_w_pool, v_pool_scale, v_w_o, v_g_ffn, v_w_gate, v_w_up, v_w_down, v_g_final):
    S = x.shape[1]
    T = _row_tile(S, 512)
    TQ = _row_tile(S, 256)
    TW = _row_tile(S, 1024)
    ix, iy, ic = lax.axis_index("x"), lax.axis_index("y"), lax.axis_index("c")
    chip = (2 * ix + iy).astype(jnp.int32)
    chip_arr = chip.reshape(1)
    core_arr = ic.astype(jnp.int32).reshape(1)

    xs, tgt = x[0], loss_target[0]

    ada_cols = w_ada.shape[2]
    b_cols = lax.dynamic_slice(b_ada, (0, chip * ada_cols), (1, ada_cols))
    mod, c_all = _mod_exchange(c, w_ada[0], b_cols)
    mod6 = mod.reshape(N_MOD, D_MODEL)

    tr = lambda a: jnp.transpose(a[0])
    win_t = tr(w_in)
    win_p = jnp.concatenate([win_t[:O_KR + ROPE], win_t[O_KR:O_KR + ROPE], win_t[O_KR + ROPE:]], axis=0).astype(BF16)
    wuq = w_uq[0]
    wuq_p = jnp.concatenate([wuq[:, h, :NOPE] for h in range(HEADS)] + [wuq[:, h, NOPE:] for h in range(HEADS)],
                            axis=1).astype(BF16)
    first = _weight_gather([win_p, wuq_p])
    w_in_f = first[0]
    w_uq_f = first[1].reshape(Q_LORA, Q_W)
    w_uk_t = jnp.transpose(w_uk[0], (1, 0, 2)).astype(BF16)
    w_uv_t = jnp.transpose(w_uv[0], (1, 0, 2)).astype(BF16)
    w_pool_b = w_pool[0].astype(BF16)
    later = [w_o[0].astype(BF16), tr(w_gate).astype(BF16), tr(w_up).astype(BF16), w_down[0].astype(BF16)]
    wg_lands = _place_shards(chip_arr, later)
    wg_lands, mod6, w_in_f = lax.optimization_barrier((wg_lands, mod6, w_in_f))
    wg_send, wg_recv, wg_lands, token = _split_start(
        "weights_start", later, wg_lands, 3 * len(later), _plan_gather_start)
    mod6 = mod6 + token[0, 0]

    half = ROPE // 2
    freqs = jnp.power(ROPE_THETA, -jnp.arange(half, dtype=F32) / half)
    cos, sin = _rope_tables(positions.reshape(S, 1), jnp.tile(freqs, 4).reshape(1, 128))
    proj, q, qc, kc, kct = _pre_attention(xs, mod6, g_mix, g_q, g_kv, w_in_f, w_uq_f, w_uk_t, cos, sin, T, TQ)
    o_lat, y_mla, lse_rows = _attention_fwd(qc, kc, kct, w_uv_t, TQ)
    wg_send, wg_recv, wg_lands, token = _split_relay(
        "weights_relay", wg_send, wg_recv, later, wg_lands, y_mla, 3 * len(later), _plan_gather_landed,
        _plan_gather_relay)
    pooled = _pool_forward(proj)
    wg_lands = _split_wait("weights_wait", wg_send, wg_recv, [], wg_lands, pooled, _plan_gather_wait)
    w_o_f = wg_lands[0].reshape(1024, D_MODEL)
    w_gate_f, w_up_f, w_down_f = wg_lands[1], wg_lands[2], wg_lands[3]
    x1, mix, mix_in = _mix_out(y_mla, pooled, w_pool_b, pool_scale, w_o_f, xs, mod6, T)
    gate, up, act, h2, dff, dx2, st_f = _ffn_forward(
        x1, mod6, g_ffn, g_final.reshape(1, D_MODEL), tgt, w_gate_f, w_up_f, w_down_f, T)

    dgate, dup, dx1, st_b = _ffn_backward(dx2, x1, dff, gate, up, mod6, g_ffn, w_gate_f, w_up_f, w_down_f, T)
    steps = S // TW
    chunk_spec = pl.BlockSpec((None, TW, FF_CHUNK), lambda g, i: (g, i, 0))
    wide_spec = pl.BlockSpec((TW, D_MODEL), lambda g, i: (i, 0))
    g_down = _tn_matmul(act, dff, chunk_spec, wide_spec, N_CHIPS, FF_CHUNK, D_MODEL, steps, "grad_w_down")
    g_gate = _tn_matmul(dgate, h2, chunk_spec, wide_spec, N_CHIPS, FF_CHUNK, D_MODEL, steps, "grad_w_gate")
    g_up = _tn_matmul(dup, h2, chunk_spec, wide_spec, N_CHIPS, FF_CHUNK, D_MODEL, steps, "grad_w_up")

    half_shapes = lambda gs: [jax.ShapeDtypeStruct((N_CHIPS, g.shape[1] // 2, g.shape[2]), F32) for g in gs]
    ffn_grads = [g_gate, g_up, g_down]
    f_send, f_recv, f_lands, token = _split_start(
        "ffn_swap_start", ffn_grads, half_shapes(ffn_grads), len(ffn_grads), _plan_swap_start)
    dmix, dpooled, do_lat, delta_rows, g_pool, g_uv_t, st_m = _mix_backward(
        dx1, mix, mod6 + token[0, 0], w_o_f, pooled, w_pool_b, pool_scale, w_uv_t, o_lat, T, TQ)
    g_o = [_tn_matmul(mix_in, dmix, wide_spec, wide_spec, 1, 1024, D_MODEL, steps, "grad_w_o").reshape(N_CHIPS, -1, D_MODEL)]
    o_send, o_recv, o_lands, token = _split_start("w_o_swap_start", g_o, half_shapes(g_o), 1, _plan_swap_start)
    du = _pool_backward(dpooled, token)
    f_got = _split_wait("ffn_swap_wait", f_send, f_recv, ffn_grads, f_lands, du, _plan_swap_wait)
    f_got += _split_wait("w_o_swap_wait", o_send, o_recv, g_o, o_lands, du, _plan_swap_wait)
    far_names = ("w_gate", "w_up", "w_down", "w_o")
    far_grads = ffn_grads + g_o
    f_sums = [_add_my_half(core_arr, a, b, "add_half_" + n) for a, b, n in zip(far_grads, f_got, far_names)]
    f_send, f_recv, f_lands, token = _split_start(
        "far_exchange_start", f_sums, [jax.ShapeDtypeStruct((3,) + s.shape[1:], F32) for s in f_sums],
        3 * len(f_sums), _plan_exchange_start)
    delta_rows = delta_rows + token[0, 0]
    dkc, dqt = _attention_bwd(qc, kc, kct, do_lat, lse_rows, delta_rows, TQ)
    grad_x, dproj, h1, g_uk_t, uq, st_p = _pre_attention_backward(
        xs, dx1, proj, q, dqt, dkc, du, cos, sin, mod6, g_mix, g_q, g_kv, w_in_f, w_uq_f, w_uk_t, T, TQ)
    rows_in = D_MODEL // N_CHIPS
    g_in_p = _tn_matmul(dproj, h1, pl.BlockSpec((TW, PROJ_W), lambda g, i: (i, 0)),
                        pl.BlockSpec((TW, rows_in), lambda g, i: (i, g)), N_CHIPS, PROJ_W, rows_in, steps, "grad_w_in")

    g_in = jnp.concatenate([g_in_p[:, :O_KR + ROPE], g_in_p[:, O_U:]], axis=1)
    g_uq = jnp.concatenate([jnp.concatenate([uq[:, h * NOPE:(h + 1) * NOPE], uq[:, O_QA + h * ROPE:O_QA + (h + 1) * ROPE]],
                                            axis=1) for h in range(HEADS)], axis=1).reshape(N_CHIPS, -1, HEADS * HEAD_QK)
    small = _pack_rows([g_uk_t, g_uv_t, g_pool, st_p[2], st_p[3, :Q_LORA], st_p[4, :KV_LORA], st_m[1, :POOL_W],
                        st_b[2], st_f[0]])
    small = jnp.concatenate([small, jnp.zeros((SMALL_ROWS - small.shape[0], 128), F32)]).reshape(N_CHIPS, -1, 128)
    grads = [g_in, g_uq, small]
    dmod = jnp.concatenate([jnp.stack([st_p[0], st_p[1], st_m[0], st_b[0], st_b[1], st_f[1]]).reshape(48, 128),
                            jnp.zeros((8, 128), F32).at[0, 0].set(st_f[2, 0])])

    f_others = _split_wait("far_exchange_wait", f_send, f_recv, f_sums, f_lands, g_in_p, _plan_exchange_wait)
    chip_core = jnp.concatenate([chip_arr, core_arr])
    f_pairs = [_add_chips_into_pair(chip_core, a, b, "add_chips_" + n) for a, b, n in zip(f_sums, f_others, far_names)]
    f_send, f_recv, f_pairs, token = _split_start("far_finish_start", [], f_pairs, len(f_pairs), _plan_finish_start)
    dmod = dmod + token[0, 0]

    names = ("w_in", "w_uq", "small")
    got, dmod_all = _grad_swap_halves(grads, dmod)
    chip_sums = [_add_my_half(core_arr, a, b, "add_half_" + n) for a, b, n in zip(grads, got, names)]
    n_send, n_recv, n_lands, token = _split_start(
        "near_exchange_start", chip_sums, [jax.ShapeDtypeStruct((3,) + s.shape[1:], F32) for s in chip_sums],
        3 * len(chip_sums), _plan_exchange_start)
    f_fulls = _split_wait("far_finish_wait", f_send, f_recv, [], f_pairs, token, _plan_finish_wait)
    gw_gate, gw_up, gw_down, gw_o = [f.reshape(-1, f.shape[2]) for f in f_fulls]
    gw_ada, gb_ada = _ada_grads(c_all, dmod_all.reshape(8, -1), chip_arr)
    loss = gb_ada[0, N_MOD * D_MODEL]
    gb_ada = gb_ada[:, :N_MOD * D_MODEL]

    untr = lambda a: jnp.transpose(a)[None]
    grad_out, delta_out, newm_out, newv_out = {}, {}, {}, {}

    def adam_sharded(n, w, g2, m, v, transposed):
        view = (lambda a: jnp.transpose(a[0])) if transposed else (lambda a: a.reshape(g2.shape))
        back = untr if transposed else (lambda a: a.reshape(w.shape))
        d_, m_, v_ = _adamw(view(w), g2, view(m), view(v), "adamw_" + n)
        grad_out[n], delta_out[n], newm_out[n], newv_out[n] = back(g2), back(d_), back(m_), back(v_)
        return d_

    done = [adam_sharded("w_gate", w_gate, gw_gate, m_w_gate, v_w_gate, True),
            adam_sharded("w_up", w_up, gw_up, m_w_up, v_w_up, True),
            adam_sharded("w_down", w_down, gw_down, m_w_down, v_w_down, False),
            adam_sharded("w_o", w_o, gw_o, m_w_o, v_w_o, False),
            adam_sharded("w_ada", w_ada, gw_ada, m_w_ada, v_w_ada, False)]
    after_all = jnp.stack([d[0, 0] for d in done])

    others = _split_wait("near_exchange_wait", n_send, n_recv, chip_sums, n_lands, after_all, _plan_exchange_wait)
    halves = [_add_chips(chip_arr, a, b, "add_chips_" + n) for a, b, n in zip(chip_sums, others, names)]
    fulls, small_all = _grad_finish(halves[:2], halves[2])
    gw_in, gw_uq = [f.reshape(-1, f.shape[2]) for f in fulls]
    small_all = small_all.reshape(SMALL_ROWS * 128)
    adam_sharded("w_in", w_in, gw_in, m_w_in, v_w_in, True)
    adam_sharded("w_uq", w_uq, gw_uq, m_w_uq, v_w_uq, False)

    n_sq = KV_LORA * HEADS * 128
    sizes = [n_sq, n_sq, n_sq, D_MODEL, Q_LORA, KV_LORA, POOL_W, D_MODEL, D_MODEL]
    offs = [0]
    for s_ in sizes:
        offs.append(offs[-1] + s_)
    piece = lambda k: small_all[offs[k]:offs[k + 1]]
    grads_small = {
        "w_uk": jnp.transpose(piece(0).reshape(HEADS, KV_LORA, NOPE), (1, 0, 2)),
        "w_uv": jnp.transpose(piece(1).reshape(HEADS, KV_LORA, 128), (1, 0, 2)),
        "w_pool": piece(2).reshape(4, POOL_GROUP, POOL_GROUP),
        "g_mix": piece(3), "g_q": piece(4), "g_kv": piece(5), "pool_scale": piece(6), "g_ffn": piece(7),
        "g_final": piece(8), "b_ada": gb_ada.reshape(-1),
    }
    weights_small = {"w_uk": w_uk, "w_uv": w_uv, "w_pool": w_pool, "g_mix": g_mix, "g_q": g_q, "g_kv": g_kv,
                     "pool_scale": pool_scale, "g_ffn": g_ffn, "g_final": g_final, "b_ada": b_ada}
    m_small = {"w_uk": m_w_uk, "w_uv": m_w_uv, "w_pool": m_w_pool, "g_mix": m_g_mix, "g_q": m_g_q, "g_kv": m_g_kv,
               "pool_scale": m_pool_scale, "g_ffn": m_g_ffn, "g_final": m_g_final, "b_ada": m_b_ada}
    v_small = {"w_uk": v_w_uk, "w_uv": v_w_uv, "w_pool": v_w_pool, "g_mix": v_g_mix, "g_q": v_g_q, "g_kv": v_g_kv,
               "pool_scale": v_pool_scale, "g_ffn": v_g_ffn, "g_final": v_g_final, "b_ada": v_b_ada}
    pack = lambda d: _pack_rows([d[n] for n in SMALL_NAMES])
    d_s, m_s, v_s = _adamw(pack(weights_small), pack(grads_small), pack(m_small), pack(v_small), "adamw_small")

    def unpack(flat2d):
        flat = flat2d.reshape(-1)
        out, o = {}, 0
        for n in SMALL_NAMES:
            size = weights_small[n].size
            out[n] = flat[o:o + size].reshape(weights_small[n].shape)
            o += size
        return out

    delta_s, newm_s, newv_s = unpack(d_s), unpack(m_s), unpack(v_s)

    for n in SMALL_NAMES:
        grad_out[n] = grads_small[n].reshape(weights_small[n].shape)
        delta_out[n], newm_out[n], newv_out[n] = delta_s[n], newm_s[n], newv_s[n]

    order = ("w_ada", "b_ada", "g_mix", "w_in", "g_q", "g_kv", "w_uq", "w_uk", "w_uv", "w_pool", "pool_scale", "w_o",
             "g_ffn", "w_gate", "w_up", "w_down", "g_final")
    return (loss, grad_x.reshape(x.shape), *[grad_out[n] for n in order], *[delta_out[n] for n in order],
            *[newm_out[n] for n in order], *[newv_out[n] for n in order])
```

```python
import functools

import jax
import jax.numpy as jnp
from jax import lax
from jax.experimental import pallas as pl
from jax.experimental.pallas import tpu as pltpu

F32 = jnp.float32
BF16 = jnp.bfloat16

D_MODEL = 1024
HEADS = 4
NOPE = 128
ROPE = 64
HEAD_QK = NOPE + ROPE
Q_LORA = 256
KV_LORA = 128
POOL_W = 512
POOL_WINDOWS = (2, 4, 8, 16)
POOL_GROUP = 128
POOL_PAD = 16
D_FF = 2816
N_CHIPS = 4
FF_CHUNK = D_FF // N_CHIPS
N_MOD = 6
EPS = 1e-6
SM_SCALE = HEAD_QK ** -0.5
ROPE_THETA = 10000.0
QK_PAD = 256
CHUNK = 64
CHUNK_SHIFT = 6

ADAM_LR = 0.001
ADAM_B1 = 0.9
ADAM_B2 = 0.999
ADAM_EPS = 1e-08
ADAM_WD = 0.01
ADAM_STEP = 10

VMEM_LIMIT = 48 * 1024 * 1024
MESH = pl.DeviceIdType.MESH
ANY = pl.BlockSpec(memory_space=pl.ANY)
VMEM_SPEC = pl.BlockSpec(memory_space=pltpu.VMEM)

PROJ_W = 1024
O_CKV = 256
O_KR = 384
O_U = 512
Q_W = 768
O_QA = 512
O_QB = 640


def _params(sem=None, vmem=VMEM_LIMIT):
    kw = dict(vmem_limit_bytes=vmem)
    if sem is not None:
        kw["dimension_semantics"] = sem
    return pltpu.CompilerParams(**kw)


def _dot(a, b):
    return jnp.dot(a.astype(BF16), b.astype(BF16), preferred_element_type=F32)


def _dot_nt(a, b):
    return lax.dot_general(a.astype(BF16), b.astype(BF16), (((1,), (1,)), ((), ())), preferred_element_type=F32)


def _dot_tn(a, b):
    return lax.dot_general(a.astype(BF16), b.astype(BF16), (((0,), (0,)), ((), ())), preferred_element_type=F32)


def _row_tile(rows, target):
    best = rows
    for t in range(8, min(rows, target) + 1, 8):
        if rows % t == 0:
            best = t
    return best if rows % best == 0 and best <= target else rows


def _rms(x):
    r = lax.rsqrt(jnp.mean(x * x, axis=-1, keepdims=True) + EPS)
    return x * r, r


def _rms_bwd(dxh, xh, r):
    return r * (dxh - xh * jnp.mean(dxh * xh, axis=-1, keepdims=True))


def _lane_first_half(shape):
    lane = lax.broadcasted_iota(jnp.int32, shape, 1)
    return (lane & (ROPE - 1)) < (ROPE // 2)


def _rope(a, cos, sin):
    first = _lane_first_half(a.shape)
    up = pltpu.roll(a, 96, 1)
    dn = pltpu.roll(a, 32, 1)
    return a * cos + jnp.where(first, -up, dn) * sin


def _rope_bwd(d, cos, sin):
    first = _lane_first_half(d.shape)
    up = pltpu.roll(d, 96, 1)
    dn = pltpu.roll(d, 32, 1)
    return d * cos + jnp.where(first, up, -dn) * sin


RELATIONS = tuple((dx, dy, dc) for dx in (0, 1) for dy in (0, 1) for dc in (0, 1) if (dx, dy, dc) != (0, 0, 0))
CHIP_RELATIONS = ((1, 0), (0, 1), (1, 1))


def _flip(v, d):
    return 1 - v if d else v


def _place():
    return lax.axis_index("x"), lax.axis_index("y"), lax.axis_index("c")


def _remote(src, dst, send_sem, recv_sem, target):
    return pltpu.make_async_remote_copy(src_ref=src, dst_ref=dst, send_sem=send_sem, recv_sem=recv_sem,
                                        device_id=target, device_id_type=MESH)


def _mod_exchange(c_row, w_ada, b_ada):
    cols = w_ada.shape[1]

    def body(c_ref, w_ref, b_ref, mod_ref, call_ref, part_ref, send1, recv1, loc1, send2, recv2, loc2):
        x, y, c = _place()
        me = 4 * x + 2 * y + c
        own = pltpu.make_async_copy(c_ref, call_ref.at[pl.ds(me, 1)], loc1)
        own.start()
        sends = []
        for k, (dx, dy, dc) in enumerate(RELATIONS):
            cp = _remote(c_ref, call_ref.at[pl.ds(me, 1)], send1.at[k], recv1.at[k],
                         (_flip(x, dx), _flip(y, dy), _flip(c, dc)))
            cp.start()
            sends.append(cp)
        for k, (dx, dy, dc) in enumerate(RELATIONS):
            src = 4 * _flip(x, dx) + 2 * _flip(y, dy) + _flip(c, dc)
            _remote(c_ref, call_ref.at[pl.ds(src, 1)], send1.at[k], recv1.at[k], (x, y, c)).wait_recv()
        own.wait()
        for cp in sends:
            cp.wait_send()
        call = call_ref[...]
        act = call * jax.nn.sigmoid(call)
        part_ref[...] = _dot(act, w_ref[...]) + b_ref[...]
        chip = 2 * x + y
        mine = pltpu.make_async_copy(part_ref.at[pl.ds(me, 1)], mod_ref.at[pl.ds(chip, 1)], loc2)
        mine.start()
        sends = []
        for k, (dx, dy) in enumerate(CHIP_RELATIONS):
            tx, ty = _flip(x, dx), _flip(y, dy)
            tb = 4 * tx + 2 * ty + c
            cp = _remote(part_ref.at[pl.ds(tb, 1)], mod_ref.at[pl.ds(chip, 1)], send2.at[k], recv2.at[k], (tx, ty, c))
            cp.start()
            sends.append(cp)
        for k, (dx, dy) in enumerate(CHIP_RELATIONS):
            src_chip = 2 * _flip(x, dx) + _flip(y, dy)
            _remote(part_ref.at[pl.ds(me, 1)], mod_ref.at[pl.ds(src_chip, 1)], send2.at[k], recv2.at[k],
                    (x, y, c)).wait_recv()
        mine.wait()
        for cp in sends:
            cp.wait_send()

    return pl.pallas_call(
        body, name="mod_exchange",
        out_shape=[jax.ShapeDtypeStruct((N_CHIPS, cols), F32), jax.ShapeDtypeStruct((8, D_MODEL), F32)],
        in_specs=[VMEM_SPEC, VMEM_SPEC, VMEM_SPEC], out_specs=[VMEM_SPEC, VMEM_SPEC],
        scratch_shapes=[pltpu.VMEM((8, cols), F32),
                        pltpu.SemaphoreType.DMA((7,)), pltpu.SemaphoreType.DMA((7,)), pltpu.SemaphoreType.DMA,
                        pltpu.SemaphoreType.DMA((3,)), pltpu.SemaphoreType.DMA((3,)), pltpu.SemaphoreType.DMA],
        compiler_params=_params(),
    )(c_row, w_ada, b_ada)


def _weight_gather(shards):
    n = len(shards)

    def body(*refs):
        ins, outs = refs[:n], refs[n:2 * n]
        send_sems, recv_sems, loc_sems = refs[2 * n:]
        x, y, c = _place()
        chip = 2 * x + y
        local = []
        for w in range(n):
            cp = pltpu.make_async_copy(ins[w], outs[w].at[chip], loc_sems.at[w])
            cp.start()
            local.append(cp)
        sends = []
        for w in range(n):
            hr = ins[w].shape[0] // 2
            half = pl.ds(c * hr, hr)
            for k, (dx, dy) in enumerate(CHIP_RELATIONS):
                cp = _remote(ins[w].at[half], outs[w].at[chip, half], send_sems.at[w, k], recv_sems.at[w, k],
                             (_flip(x, dx), _flip(y, dy), c))
                cp.start()
                sends.append(cp)
        for w in range(n):
            hr = ins[w].shape[0] // 2
            half = pl.ds(c * hr, hr)
            for k, (dx, dy) in enumerate(CHIP_RELATIONS):
                src_chip = 2 * _flip(x, dx) + _flip(y, dy)
                got = outs[w].at[src_chip, half]
                _remote(got, got, send_sems.at[w, k], recv_sems.at[w, k], (x, y, c)).wait_recv()
                cp = _remote(got, got, send_sems.at[w, 3 + k], recv_sems.at[w, 3 + k], (x, y, 1 - c))
                cp.start()
                sends.append(cp)
        for w in range(n):
            hr = ins[w].shape[0] // 2
            other = pl.ds((1 - c) * hr, hr)
            for k, (dx, dy) in enumerate(CHIP_RELATIONS):
                src_chip = 2 * _flip(x, dx) + _flip(y, dy)
                got = outs[w].at[src_chip, other]
                _remote(got, got, send_sems.at[w, 3 + k], recv_sems.at[w, 3 + k], (x, y, c)).wait_recv()
        for cp in sends:
            cp.wait_send()
        for cp in local:
            cp.wait()

    return pl.pallas_call(
        body, name="weight_gather",
        out_shape=[pltpu.HBM((N_CHIPS,) + s.shape, s.dtype) for s in shards],
        in_specs=[VMEM_SPEC] * n, out_specs=[ANY] * n,
        scratch_shapes=[pltpu.SemaphoreType.DMA((n, 6)), pltpu.SemaphoreType.DMA((n, 6)),
                        pltpu.SemaphoreType.DMA((n,))],
        compiler_params=_params(),
    )(*shards)


HBM_SPEC = pl.BlockSpec(memory_space=pltpu.HBM)
SEM_SPEC = pl.BlockSpec(memory_space=pltpu.SEMAPHORE)
DATAFLOW = pltpu.SideEffectType.DATAFLOW_SIDE_EFFECTING


def _in_hbm(a):
    return pltpu.with_memory_space_constraint(a, pltpu.HBM)


def _hbm(*arrays):
    return tuple(_in_hbm(a) for a in arrays)


def _hbm_like(arrays):
    return [pltpu.HBM(a.shape, a.dtype) for a in arrays]


def _split_start(name, srcs, lands, n_remote, plan):
    lands = [lax.empty(a.shape, a.dtype) if isinstance(a, jax.ShapeDtypeStruct) else a for a in lands]
    n, m = len(srcs), len(lands)

    def body(*refs):
        src_refs, land_refs = refs[:n], refs[n:n + m]
        send_sems, recv_sems, token = refs[n + m], refs[n + m + 1], refs[n + 2 * m + 2]
        remote = plan(_place(), src_refs, land_refs)
        assert len(remote) == n_remote
        for i, (s, d, target) in enumerate(remote):
            _remote(s, d, send_sems.at[i], recv_sems.at[i], target).start()
        token[...] = jnp.zeros_like(token)

    res = pl.pallas_call(
        body, name=name,
        out_shape=(pltpu.SemaphoreType.DMA((n_remote,)), pltpu.SemaphoreType.DMA((n_remote,)),
                   *_hbm_like(lands), jax.ShapeDtypeStruct((8, 128), F32)),
        in_specs=[HBM_SPEC] * (n + m),
        out_specs=(SEM_SPEC, SEM_SPEC, *([HBM_SPEC] * m), VMEM_SPEC),
        input_output_aliases={n + i: 2 + i for i in range(m)},
        compiler_params=pltpu.CompilerParams(has_side_effects=DATAFLOW),
    )(*[_in_hbm(a) for a in srcs], *[_in_hbm(a) for a in lands])
    return res[0], res[1], list(res[2:2 + m]), res[2 + m]


def _split_wait(name, send_sems, recv_sems, srcs, lands, after, plan):
    n, m = len(srcs), len(lands)

    def body(*refs):
        src_refs, land_refs = refs[:n], refs[n:n + m]
        send_sems, recv_sems = refs[n + m], refs[n + m + 1]
        place = _place()
        for i, (s, d) in enumerate(plan(place, src_refs, land_refs)):
            cp = _remote(s, d, send_sems.at[i], recv_sems.at[i], place)
            cp.wait_send()
            cp.wait_recv()

    res = pl.pallas_call(
        body, name=name,
        out_shape=tuple(_hbm_like(lands)),
        in_specs=[HBM_SPEC] * (n + m) + [SEM_SPEC, SEM_SPEC, ANY],
        out_specs=tuple([HBM_SPEC] * m),
        input_output_aliases={n + i: i for i in range(m)},
        compiler_params=pltpu.CompilerParams(has_side_effects=DATAFLOW),
    )(*srcs, *lands, send_sems, recv_sems, after)
    return list(res)


def _split_relay(name, send_sems, recv_sems, srcs, lands, after, n_remote, plan_wait, plan_send):
    n, m = len(srcs), len(lands)

    def body(*refs):
        src_refs, land_refs = refs[:n], refs[n:n + m]
        old_send, old_recv = refs[n + m], refs[n + m + 1]
        new_send, new_recv = refs[n + m + 3], refs[n + m + 4]
        token = refs[n + m + 5 + m]
        place = _place()
        for i, (s, d) in enumerate(plan_wait(place, src_refs, land_refs)):
            cp = _remote(s, d, old_send.at[i], old_recv.at[i], place)
            cp.wait_send()
            cp.wait_recv()
        for i, (s, d, target) in enumerate(plan_send(place, land_refs)):
            _remote(s, d, new_send.at[i], new_recv.at[i], target).start()
        token[...] = jnp.zeros_like(token)

    res = pl.pallas_call(
        body, name=name,
        out_shape=(pltpu.SemaphoreType.DMA((n_remote,)), pltpu.SemaphoreType.DMA((n_remote,)),
                   *_hbm_like(lands), jax.ShapeDtypeStruct((8, 128), F32)),
        in_specs=[HBM_SPEC] * (n + m) + [SEM_SPEC, SEM_SPEC, ANY],
        out_specs=(SEM_SPEC, SEM_SPEC, *([HBM_SPEC] * m), VMEM_SPEC),
        input_output_aliases={n + i: 2 + i for i in range(m)},
        compiler_params=pltpu.CompilerParams(has_side_effects=DATAFLOW),
    )(*srcs, *lands, send_sems, recv_sems, after)
    return res[0], res[1], list(res[2:2 + m]), res[2 + m]


def _half(ref, core, axis=0):
    hr = ref.shape[axis] // 2
    return pl.ds(core * hr, hr)


def _plan_gather_start(place, src, land):
    x, y, c = place
    chip = 2 * x + y
    return [(s.at[_half(s, c)], l.at[chip, _half(s, c)], (_flip(x, dx), _flip(y, dy), c))
            for s, l in zip(src, land) for dx, dy in CHIP_RELATIONS]


def _plan_gather_landed(place, src, land):
    x, y, c = place
    return [(s.at[_half(s, c)], l.at[2 * _flip(x, dx) + _flip(y, dy), _half(s, c)])
            for s, l in zip(src, land) for dx, dy in CHIP_RELATIONS]


def _plan_gather_relay(place, land):
    x, y, c = place
    out = []
    for l in land:
        for dx, dy in CHIP_RELATIONS:
            got = l.at[2 * _flip(x, dx) + _flip(y, dy), _half(l, c, 1)]
            out.append((got, got, (x, y, 1 - c)))
    return out


def _plan_gather_wait(place, src, land):
    x, y, c = place
    out = []
    for l in land:
        for dx, dy in CHIP_RELATIONS:
            got = l.at[2 * _flip(x, dx) + _flip(y, dy), _half(l, 1 - c, 1)]
            out.append((got, got))
    return out


def _plan_swap_start(place, src, land):
    x, y, c = place
    return [(s.at[:, _half(s, 1 - c, 1), :], l, (x, y, 1 - c)) for s, l in zip(src, land)]


def _plan_swap_wait(place, src, land):
    return [(s.at[:, _half(s, 0, 1), :], l) for s, l in zip(src, land)]


def _plan_exchange_start(place, src, land):
    x, y, c = place
    remote = []
    for s, l in zip(src, land):
        for k, (dx, dy) in enumerate(CHIP_RELATIONS):
            tx, ty = _flip(x, dx), _flip(y, dy)
            remote.append((s.at[2 * tx + ty], l.at[k], (tx, ty, c)))
    return remote


def _plan_exchange_wait(place, src, land):
    return [(s.at[0], l.at[k]) for s, l in zip(src, land) for k in range(3)]


def _plan_finish_start(place, src, land):
    x, y, c = place
    return [(l.at[c], l.at[c], (x, y, 1 - c)) for l in land]


def _plan_finish_wait(place, src, land):
    x, y, c = place
    return [(l.at[c], l.at[1 - c]) for l in land]


def _grad_swap_halves(grads, dmod):
    n = len(grads)

    def body(*refs):
        ins, dmod_ref = refs[:n], refs[n]
        outs, dall_ref = refs[n + 1:2 * n + 1], refs[2 * n + 1]
        send_sems, recv_sems, dsend, drecv, dloc = refs[2 * n + 2:]
        x, y, c = _place()
        me = 4 * x + 2 * y + c
        sends = []
        for w in range(n):
            hr = ins[w].shape[1] // 2
            cp = _remote(ins[w].at[:, pl.ds((1 - c) * hr, hr), :], outs[w], send_sems.at[w], recv_sems.at[w],
                         (x, y, 1 - c))
            cp.start()
            sends.append(cp)
        own = pltpu.make_async_copy(dmod_ref, dall_ref.at[me], dloc)
        own.start()
        for k, (dx, dy, dc) in enumerate(RELATIONS):
            cp = _remote(dmod_ref, dall_ref.at[me], dsend.at[k], drecv.at[k],
                         (_flip(x, dx), _flip(y, dy), _flip(c, dc)))
            cp.start()
            sends.append(cp)
        for k, (dx, dy, dc) in enumerate(RELATIONS):
            src = 4 * _flip(x, dx) + 2 * _flip(y, dy) + _flip(c, dc)
            _remote(dmod_ref, dall_ref.at[src], dsend.at[k], drecv.at[k], (x, y, c)).wait_recv()
        for w in range(n):
            _remote(outs[w], outs[w], send_sems.at[w], recv_sems.at[w], (x, y, c)).wait_recv()
        own.wait()
        for cp in sends:
            cp.wait_send()

    out_shape = [pltpu.HBM((N_CHIPS, g.shape[1] // 2, g.shape[2]), F32) for g in grads]
    out_shape.append(pltpu.HBM((8,) + dmod.shape, F32))
    res = pl.pallas_call(
        body, name="grad_swap_halves",
        out_shape=out_shape, in_specs=[ANY] * n + [VMEM_SPEC], out_specs=[ANY] * (n + 1),
        scratch_shapes=[pltpu.SemaphoreType.DMA((n,)), pltpu.SemaphoreType.DMA((n,)),
                        pltpu.SemaphoreType.DMA((7,)), pltpu.SemaphoreType.DMA((7,)), pltpu.SemaphoreType.DMA],
        compiler_params=_params(),
    )(*grads, dmod)
    return res[:n], res[n]


def _grad_finish(halves, small_half):
    n = len(halves)

    def body(*refs):
        ins, sm_ref = refs[:n], refs[n]
        outs, sall_ref = refs[n + 1:2 * n + 1], refs[2 * n + 1]
        send_sems, recv_sems, loc_sems, ssend, srecv, sloc = refs[2 * n + 2:]
        x, y, c = _place()
        chip = 2 * x + y
        local, sends = [], []
        for w in range(n):
            cp = pltpu.make_async_copy(ins[w], outs[w].at[c], loc_sems.at[w])
            cp.start()
            local.append(cp)
            cp = _remote(ins[w], outs[w].at[c], send_sems.at[w], recv_sems.at[w], (x, y, 1 - c))
            cp.start()
            sends.append(cp)
        cp = pltpu.make_async_copy(sm_ref, sall_ref.at[chip, c], sloc)
        cp.start()
        local.append(cp)
        for k, (dx, dy, dc) in enumerate(RELATIONS):
            cp = _remote(sm_ref, sall_ref.at[chip, c], ssend.at[k], srecv.at[k],
                         (_flip(x, dx), _flip(y, dy), _flip(c, dc)))
            cp.start()
            sends.append(cp)
        for k, (dx, dy, dc) in enumerate(RELATIONS):
            got = sall_ref.at[2 * _flip(x, dx) + _flip(y, dy), _flip(c, dc)]
            _remote(got, got, ssend.at[k], srecv.at[k], (x, y, c)).wait_recv()
        for w in range(n):
            got = outs[w].at[1 - c]
            _remote(got, got, send_sems.at[w], recv_sems.at[w], (x, y, c)).wait_recv()
        for cp in sends:
            cp.wait_send()
        for cp in local:
            cp.wait()

    out_shape = [jax.ShapeDtypeStruct((2,) + h.shape, F32) for h in halves]
    out_shape.append(jax.ShapeDtypeStruct((N_CHIPS, 2) + small_half.shape, F32))
    res = pl.pallas_call(
        body, name="grad_finish",
        out_shape=out_shape, in_specs=[VMEM_SPEC] * (n + 1), out_specs=[ANY] * (n + 1),
        scratch_shapes=[pltpu.SemaphoreType.DMA((n,)), pltpu.SemaphoreType.DMA((n,)), pltpu.SemaphoreType.DMA((n,)),
                        pltpu.SemaphoreType.DMA((7,)), pltpu.SemaphoreType.DMA((7,)), pltpu.SemaphoreType.DMA],
        compiler_params=_params(),
    )(*halves, small_half)
    return res[:n], res[n]


def _add_my_half(core, full, got, name):
    _, hr, cols = got.shape

    def body(core_ref, a_ref, b_ref, o_ref):
        o_ref[...] = a_ref[...] + b_ref[...]

    return pl.pallas_call(
        body, name=name,
        out_shape=pltpu.HBM(got.shape, F32),
        grid_spec=pltpu.PrefetchScalarGridSpec(
            num_scalar_prefetch=1, grid=(N_CHIPS,),
            in_specs=[pl.BlockSpec((None, hr, cols), lambda s, core_ref: (s, core_ref[0], 0)),
                      pl.BlockSpec((None, hr, cols), lambda s, core_ref: (s, 0, 0))],
            out_specs=pl.BlockSpec((None, hr, cols), lambda s, core_ref: (s, 0, 0))),
        compiler_params=_params(("arbitrary",)),
    )(core, *_hbm(full, got))


def _add_chips(chip, mine, got, name):
    _, hr, cols = mine.shape

    def body(chip_ref, a_ref, b_ref, o_ref):
        o_ref[...] = ((a_ref[...] + b_ref[0]) + b_ref[1]) + b_ref[2]

    return pl.pallas_call(
        body, name=name,
        out_shape=jax.ShapeDtypeStruct((hr, cols), F32),
        grid_spec=pltpu.PrefetchScalarGridSpec(
            num_scalar_prefetch=1, grid=(1,),
            in_specs=[pl.BlockSpec((None, hr, cols), lambda s, chip_ref: (chip_ref[0], 0, 0)),
                      pl.BlockSpec((3, hr, cols), lambda s, chip_ref: (0, 0, 0))],
            out_specs=pl.BlockSpec((hr, cols), lambda s, chip_ref: (0, 0))),
        compiler_params=_params(("arbitrary",)),
    )(chip, *_hbm(mine, got))


def _add_chips_into_pair(chip_core, mine, got, name):
    _, hr, cols = mine.shape

    def body(cc_ref, a_ref, b_ref, o_ref):
        o_ref[...] = ((a_ref[...] + b_ref[0]) + b_ref[1]) + b_ref[2]

    return pl.pallas_call(
        body, name=name,
        out_shape=pltpu.HBM((2, hr, cols), F32),
        grid_spec=pltpu.PrefetchScalarGridSpec(
            num_scalar_prefetch=1, grid=(1,),
            in_specs=[pl.BlockSpec((None, hr, cols), lambda s, cc_ref: (cc_ref[0], 0, 0)),
                      pl.BlockSpec((3, hr, cols), lambda s, cc_ref: (0, 0, 0))],
            out_specs=pl.BlockSpec((None, hr, cols), lambda s, cc_ref: (cc_ref[1], 0, 0))),
        compiler_params=_params(("arbitrary",)),
    )(chip_core, *_hbm(mine, got))


def _place_shards(chip, shards):
    n = len(shards)

    def body(chip_ref, *refs):
        for w in range(n):
            refs[n + w][...] = refs[w][...]

    return pl.pallas_call(
        body, name="place_shards",
        out_shape=[pltpu.HBM((N_CHIPS,) + s.shape, s.dtype) for s in shards],
        grid_spec=pltpu.PrefetchScalarGridSpec(
            num_scalar_prefetch=1, grid=(1,),
            in_specs=[pl.BlockSpec(s.shape, lambda i, chip_ref: (0, 0)) for s in shards],
            out_specs=[pl.BlockSpec((None,) + s.shape, lambda i, chip_ref: (chip_ref[0], 0, 0)) for s in shards]),
        compiler_params=_params(("arbitrary",)),
    )(chip, *shards)


def _rope_tables(pos_col, freqs):
    S = pos_col.shape[0]
    T = _row_tile(S, 1024)

    def body(p_ref, f_ref, cos_ref, sin_ref):
        ang = p_ref[...].astype(F32) * f_ref[...]
        cos_ref[...] = jnp.cos(ang)
        sin_ref[...] = jnp.sin(ang)

    return pl.pallas_call(
        body, name="rope_tables", grid=(S // T,),
        out_shape=[pltpu.HBM((S, 128), F32)] * 2,
        in_specs=[pl.BlockSpec((T, 1), lambda i: (i, 0)), pl.BlockSpec((1, 128), lambda i: (0, 0))],
        out_specs=[pl.BlockSpec((T, 128), lambda i: (i, 0))] * 2,
        compiler_params=_params(("parallel",)),
    )(*_hbm(pos_col, freqs))


def _full(shape):
    zeros = (0,) * len(shape)
    return pl.BlockSpec(shape, lambda *_: zeros)


def _pre_attention(x, mod6, g_mix, g_q, g_kv, w_in, w_uq, w_uk_t, cos, sin, T, TQ):
    S = x.shape[0]

    def body(x_ref, mod_ref, gm_ref, gq_ref, gkv_ref, win_ref, wuq_ref, wuk_ref, cos_ref, sin_ref,
             proj_ref, q_ref, qc_ref, kc_ref, kct_ref):
        xh, _ = _rms(x_ref[...])
        h1 = ((xh * gm_ref[...]) * (1.0 + mod_ref[1:2, :]) + mod_ref[0:1, :]).astype(BF16)
        rows_in = D_MODEL // N_CHIPS
        proj = _dot_nt(h1[:, 0:rows_in], win_ref[0])
        for j in range(1, N_CHIPS):
            proj = proj + _dot_nt(h1[:, j * rows_in:(j + 1) * rows_in], win_ref[j])
        proj_ref[...] = proj
        cqh, _ = _rms(proj[:, :Q_LORA])
        c_q = cqh * gq_ref[...]
        ckvh, _ = _rms(proj[:, O_CKV:O_KR])
        c_kv = ckvh * gkv_ref[...]
        q = _dot(c_q, wuq_ref[...])
        q_ref[...] = q
        cos_t, sin_t = cos_ref[...], sin_ref[...]
        ropes = (_rope(q[:, O_QA:O_QB], cos_t, sin_t), _rope(q[:, O_QB:Q_W], cos_t, sin_t))
        low = lax.broadcasted_iota(jnp.int32, (T, 128), 1) < ROPE
        for h in range(HEADS):
            q_lat = _dot_nt(q[:, h * NOPE:(h + 1) * NOPE], wuk_ref[h])
            keep = low if h % 2 == 0 else jnp.logical_not(low)
            qc_ref[h, :, 0:KV_LORA] = q_lat.astype(BF16)
            qc_ref[h, :, KV_LORA:QK_PAD] = jnp.where(keep, ropes[h // 2], 0.0).astype(BF16)
        k_rope = _rope(proj[:, O_KR:O_U], cos_t, sin_t)
        kc_ref[:, 0:KV_LORA] = c_kv.astype(BF16)
        kc_ref[:, KV_LORA:QK_PAD] = k_rope.astype(BF16)
        lat_t, rope_t = jnp.transpose(c_kv), jnp.transpose(k_rope)
        for s in range(T // TQ):
            kct_ref[s, 0:KV_LORA, :] = lat_t[:, s * TQ:(s + 1) * TQ].astype(BF16)
            kct_ref[s, KV_LORA:QK_PAD, :] = rope_t[:, s * TQ:(s + 1) * TQ].astype(BF16)

    row = lambda w: pl.BlockSpec((T, w), lambda i: (i, 0))
    return pl.pallas_call(
        body, name="pre_attention", grid=(S // T,),
        out_shape=[pltpu.HBM((S, PROJ_W), F32), pltpu.HBM((S, Q_W), F32), pltpu.HBM((HEADS, S, QK_PAD), BF16),
                   pltpu.HBM((S, QK_PAD), BF16), pltpu.HBM((S // TQ, QK_PAD, TQ), BF16)],
        in_specs=[row(D_MODEL), _full((N_MOD, D_MODEL)), _full((1, D_MODEL)), _full((1, Q_LORA)), _full((1, KV_LORA)),
                  _full((N_CHIPS, PROJ_W, D_MODEL // N_CHIPS)), _full((Q_LORA, Q_W)), _full((HEADS, KV_LORA, NOPE)),
                  row(128), row(128)],
        out_specs=[row(PROJ_W), row(Q_W), pl.BlockSpec((HEADS, T, QK_PAD), lambda i: (0, i, 0)), row(QK_PAD),
                   pl.BlockSpec((T // TQ, QK_PAD, TQ), lambda i: (i, 0, 0))],
        compiler_params=_params(("parallel",)),
    )(*_hbm(x, mod6, g_mix, g_q, g_kv, w_in, w_uq, w_uk_t, cos, sin))


def _diag_mask(TQ, width):
    key = lax.broadcasted_iota(jnp.int32, (TQ, width), 0) >> CHUNK_SHIFT
    qry = (lax.broadcasted_iota(jnp.int32, (TQ, width), 1) & (TQ - 1)) >> CHUNK_SHIFT
    return key <= qry


def _col_to_row(col):
    return jnp.transpose(jnp.broadcast_to(col, (col.shape[0], 128)))[0:1, :]


def _attention_fwd(qc, kc, kct, w_uv_t, TQ):
    S = kc.shape[0]
    R = HEADS * TQ
    nq = S // TQ

    def body(q_ref, k_ref, kt_ref, wuv_ref, o_ref, y_ref, lser_ref, m_s, l_s, acc_s, st_s):
        i = pl.program_id(0)
        q = q_ref[...].reshape(R, QK_PAD)
        m_s[...] = jnp.full((1, R), -jnp.inf, F32)
        l_s[...] = jnp.zeros((1, R), F32)
        acc_s[...] = jnp.zeros((KV_LORA, R), F32)

        def scores(j):
            return _dot_nt(k_ref[pl.ds(pl.multiple_of(j * TQ, TQ), TQ), :], q) * SM_SCALE

        def update(j, st):
            m_old = m_s[...]
            m_new = jnp.maximum(m_old, jnp.max(st, axis=0, keepdims=True))
            pt = jnp.exp(st - m_new)
            alpha = jnp.exp(m_old - m_new)
            l_s[...] = alpha * l_s[...] + jnp.sum(pt, axis=0, keepdims=True)
            acc_s[...] = alpha * acc_s[...] + _dot(kt_ref[j, 0:KV_LORA, :], pt)
            m_s[...] = m_new

        st_s[...] = scores(0)

        def loop(j, carry):
            st = st_s[...]
            st_s[...] = scores(j + 1)
            update(j, st)
            return carry

        lax.fori_loop(0, i, loop, 0)
        update(i, jnp.where(_diag_mask(TQ, R), st_s[...], -jnp.inf))
        l = l_s[...]
        lser_ref[0] = m_s[...] + jnp.log(l)
        o = jnp.transpose(acc_s[...] / l).astype(BF16)
        for h in range(HEADS):
            oh = o[h * TQ:(h + 1) * TQ, :]
            o_ref[h] = oh
            y_ref[:, h * 128:(h + 1) * 128] = _dot(oh, wuv_ref[h]).astype(BF16)

    return pl.pallas_call(
        body, name="attention_fwd", grid=(nq,),
        out_shape=[pltpu.HBM((HEADS, S, KV_LORA), BF16), pltpu.HBM((S, HEADS * 128), BF16),
                   pltpu.HBM((nq, 1, R), F32)],
        in_specs=[pl.BlockSpec((HEADS, TQ, QK_PAD), lambda i: (0, i, 0)), _full((S, QK_PAD)),
                  _full((nq, QK_PAD, TQ)), _full((HEADS, KV_LORA, 128))],
        out_specs=[pl.BlockSpec((HEADS, TQ, KV_LORA), lambda i: (0, i, 0)), pl.BlockSpec((TQ, HEADS * 128), lambda i: (i, 0)),
                   pl.BlockSpec((1, 1, R), lambda i: (i, 0, 0))],
        scratch_shapes=[pltpu.VMEM((1, R), F32), pltpu.VMEM((1, R), F32), pltpu.VMEM((KV_LORA, R), F32),
                        pltpu.VMEM((TQ, R), F32)],
        compiler_params=_params(("parallel",)),
    )(*_hbm(qc, kc, kct, w_uv_t))


def _pool_forward(proj):
    S = proj.shape[0]
    RB = _row_tile(S, 256)

    def body(proj_ref, out_ref, pad_ref, sem):
        cp = pltpu.make_async_copy(proj_ref.at[:, pl.ds(O_U, POOL_W)], pad_ref.at[pl.ds(POOL_PAD, S)], sem)
        cp.start()
        pad_ref[0:POOL_PAD, :] = jnp.zeros((POOL_PAD, POOL_W), F32)
        cp.wait()
        for g, win in enumerate(POOL_WINDOWS):
            cols = slice(g * POOL_GROUP, (g + 1) * POOL_GROUP)
            for r0 in range(0, S, RB):
                u = pad_ref[POOL_PAD + r0:POOL_PAD + r0 + RB, cols]
                acc = u
                for k in range(1, win):
                    acc = acc + pad_ref[POOL_PAD + r0 - k:POOL_PAD + r0 - k + RB, cols]
                if r0 == 0:
                    t1 = (lax.broadcasted_iota(jnp.int32, (RB, POOL_GROUP), 0) + 1).astype(F32)
                    mean = acc / jnp.minimum(t1, float(win))
                else:
                    mean = acc * (1.0 / win)
                out_ref[r0:r0 + RB, cols] = (mean - u).astype(BF16)

    return pl.pallas_call(
        body, name="pool_forward",
        out_shape=jax.ShapeDtypeStruct((S, POOL_W), BF16),
        in_specs=[ANY], out_specs=VMEM_SPEC,
        scratch_shapes=[pltpu.VMEM((S + POOL_PAD, POOL_W), F32), pltpu.SemaphoreType.DMA],
        compiler_params=_params(),
    )(proj)


def _pool_backward(dpooled, after):
    S = dpooled.shape[0]
    RB = _row_tile(S, 256)

    def body(dp_ref, after_ref, out_ref, pad_ref, sem):
        cp = pltpu.make_async_copy(dp_ref, pad_ref.at[pl.ds(0, S)], sem)
        cp.start()
        pad_ref[S:S + POOL_PAD, :] = jnp.zeros((POOL_PAD, POOL_W), F32)
        cp.wait()
        for g, win in enumerate(POOL_WINDOWS):
            cols = slice(g * POOL_GROUP, (g + 1) * POOL_GROUP)
            head = pad_ref[0:POOL_PAD, cols]
            t1 = (lax.broadcasted_iota(jnp.int32, (POOL_PAD, POOL_GROUP), 0) + 1).astype(F32)
            pad_ref[0:POOL_PAD, cols] = head * (float(win) / jnp.minimum(t1, float(win)))
            for r0 in range(0, S, RB):
                acc = pad_ref[r0:r0 + RB, cols]
                for k in range(1, win):
                    acc = acc + pad_ref[r0 + k:r0 + k + RB, cols]
                own = pad_ref[r0:r0 + RB, cols]
                if r0 == 0:
                    own = jnp.concatenate([head, own[POOL_PAD:]], axis=0)
                out_ref[r0:r0 + RB, cols] = acc * (1.0 / win) - own

    return pl.pallas_call(
        body, name="pool_backward",
        out_shape=jax.ShapeDtypeStruct((S, POOL_W), F32),
        in_specs=[ANY, ANY], out_specs=VMEM_SPEC,
        scratch_shapes=[pltpu.VMEM((S + POOL_PAD, POOL_W), F32), pltpu.SemaphoreType.DMA],
        compiler_params=_params(),
    )(dpooled, after)


def _mix_out(y_mla, pooled, w_pool, pool_scale, w_o, x, mod6, T):
    S = x.shape[0]

    def body(ym_ref, pl_ref, wp_ref, ps_ref, wo_ref, x_ref, mod_ref, x1_ref, mix_ref, mi_ref):
        mi_ref[:, 0:512] = ym_ref[...]
        for g in range(len(POOL_WINDOWS)):
            cols = slice(g * POOL_GROUP, (g + 1) * POOL_GROUP)
            z = _dot(pl_ref[:, cols], wp_ref[g])
            mi_ref[:, 512 + g * POOL_GROUP:512 + (g + 1) * POOL_GROUP] = (z * ps_ref[:, cols]).astype(BF16)
        mix = _dot(mi_ref[...], wo_ref[...])
        mix_ref[...] = mix
        x1_ref[...] = x_ref[...] + mod_ref[2:3, :] * mix

    row = lambda w: pl.BlockSpec((T, w), lambda i: (i, 0))
    return pl.pallas_call(
        body, name="mix_out", grid=(S // T,),
        out_shape=[pltpu.HBM((S, D_MODEL), F32), pltpu.HBM((S, D_MODEL), F32), pltpu.HBM((S, 1024), BF16)],
        in_specs=[row(512), row(POOL_W), _full((4, POOL_GROUP, POOL_GROUP)), _full((1, POOL_W)),
                  _full((1024, D_MODEL)), row(D_MODEL), _full((N_MOD, D_MODEL))],
        out_specs=[row(D_MODEL), row(D_MODEL), row(1024)],
        compiler_params=_params(("parallel",)),
    )(*_hbm(y_mla, pooled, w_pool, pool_scale, w_o, x, mod6))


def _ffn_forward(x1, mod6, g_ffn, g_final, target, w_gate, w_up, w_down, T):
    S = x1.shape[0]

    def body(x1_ref, mod_ref, gf_ref, gl_ref, tgt_ref, wg_ref, wu_ref, wd_ref,
             gate_ref, up_ref, act_ref, h2_ref, dff_ref, dx2_ref, st_ref, acc_s):
        i, j = pl.program_id(0), pl.program_id(1)

        @pl.when(jnp.logical_and(i == 0, j == 0))
        def _():
            st_ref[...] = jnp.zeros_like(st_ref)

        @pl.when(j == 0)
        def _():
            xh, _ = _rms(x1_ref[...])
            h2_ref[...] = ((xh * gf_ref[...]) * (1.0 + mod_ref[4:5, :]) + mod_ref[3:4, :]).astype(BF16)
            acc_s[...] = jnp.zeros_like(acc_s)

        h2 = h2_ref[...]
        gate = _dot_nt(h2, wg_ref[j])
        up = _dot_nt(h2, wu_ref[j])
        gate_ref[...] = gate.astype(BF16)
        up_ref[...] = up.astype(BF16)
        act = (gate * jax.nn.sigmoid(gate) * up).astype(BF16)
        act_ref[...] = act
        acc_s[...] += _dot(act, wd_ref[j])

        @pl.when(j == N_CHIPS - 1)
        def _():
            ff = acc_s[...]
            x2 = x1_ref[...] + mod_ref[5:6, :] * ff
            xh, r3 = _rms(x2)
            err = xh * gl_ref[...] - tgt_ref[...]
            dy = err * (1.0 / D_MODEL)
            dx2 = _rms_bwd(dy * gl_ref[...], xh, r3)
            dx2_ref[...] = dx2
            dff_ref[...] = (dx2 * mod_ref[5:6, :]).astype(BF16)
            st_ref[0:1, :] += jnp.sum(dy * xh, axis=0, keepdims=True)
            st_ref[1:2, :] += jnp.sum(dx2 * ff, axis=0, keepdims=True)
            st_ref[2:3, :] += 0.5 * jnp.sum(err * dy)

    row = pl.BlockSpec((T, D_MODEL), lambda i, j: (i, 0))
    chunk_out = pl.BlockSpec((None, T, FF_CHUNK), lambda i, j: (j, i, 0))
    big = pltpu.HBM((N_CHIPS, S, FF_CHUNK), BF16)
    wide = pltpu.HBM((S, D_MODEL), BF16)
    return pl.pallas_call(
        body, name="ffn_forward", grid=(S // T, N_CHIPS),
        out_shape=[big, big, big, wide, wide, pltpu.HBM((S, D_MODEL), F32), jax.ShapeDtypeStruct((8, D_MODEL), F32)],
        in_specs=[row, _full((N_MOD, D_MODEL)), _full((1, D_MODEL)), _full((1, D_MODEL)), row,
                  VMEM_SPEC, VMEM_SPEC, VMEM_SPEC],
        out_specs=[chunk_out, chunk_out, chunk_out, row, row, row, _full((8, D_MODEL))],
        scratch_shapes=[pltpu.VMEM((T, D_MODEL), F32)],
        compiler_params=_params(("arbitrary", "arbitrary")),
    )(*_hbm(x1, mod6, g_ffn, g_final, target), w_gate, w_up, w_down)


def _ffn_backward(dx2, x1, dff, gate, up, mod6, g_ffn, w_gate, w_up, w_down, T):
    S = x1.shape[0]

    def body(dx2_ref, x1_ref, dff_ref, gate_ref, up_ref, mod_ref, gf_ref, wg_ref, wu_ref, wd_ref,
             dgate_ref, dup_ref, dx1_ref, st_ref, acc_s):
        i, j = pl.program_id(0), pl.program_id(1)

        @pl.when(jnp.logical_and(i == 0, j == 0))
        def _():
            st_ref[...] = jnp.zeros_like(st_ref)

        @pl.when(j == 0)
        def _():
            acc_s[...] = jnp.zeros_like(acc_s)

        for r0 in range(0, T, T // 2):
            rows = slice(r0, r0 + T // 2)
            gate, up = gate_ref[rows, :].astype(F32), up_ref[rows, :].astype(F32)
            sg = jax.nn.sigmoid(gate)
            dact = _dot_nt(dff_ref[rows, :], wd_ref[j])
            dup = (dact * (gate * sg)).astype(BF16)
            dgate = (dact * up * (sg * (1.0 + gate * (1.0 - sg)))).astype(BF16)
            dup_ref[rows, :] = dup
            dgate_ref[rows, :] = dgate
            acc_s[rows, :] += _dot(dgate, wg_ref[j]) + _dot(dup, wu_ref[j])

        @pl.when(j == N_CHIPS - 1)
        def _():
            dh2 = acc_s[...]
            xh, r2 = _rms(x1_ref[...])
            n2 = xh * gf_ref[...]
            st_ref[0:1, :] += jnp.sum(dh2, axis=0, keepdims=True)
            st_ref[1:2, :] += jnp.sum(dh2 * n2, axis=0, keepdims=True)
            dn2 = dh2 * (1.0 + mod_ref[4:5, :])
            st_ref[2:3, :] += jnp.sum(dn2 * xh, axis=0, keepdims=True)
            dx1_ref[...] = _rms_bwd(dn2 * gf_ref[...], xh, r2) + dx2_ref[...]

    row = pl.BlockSpec((T, D_MODEL), lambda i, j: (i, 0))
    chunk = pl.BlockSpec((None, T, FF_CHUNK), lambda i, j: (j, i, 0))
    big = pltpu.HBM((N_CHIPS, S, FF_CHUNK), BF16)
    return pl.pallas_call(
        body, name="ffn_backward", grid=(S // T, N_CHIPS),
        out_shape=[big, big, pltpu.HBM((S, D_MODEL), F32), jax.ShapeDtypeStruct((8, D_MODEL), F32)],
        in_specs=[row, row, row, chunk, chunk, _full((N_MOD, D_MODEL)), _full((1, D_MODEL)),
                  VMEM_SPEC, VMEM_SPEC, VMEM_SPEC],
        out_specs=[chunk, chunk, row, _full((8, D_MODEL))],
        scratch_shapes=[pltpu.VMEM((T, D_MODEL), F32)],
        compiler_params=_params(("arbitrary", "arbitrary")),
    )(*_hbm(dx2, x1, dff, gate, up, mod6, g_ffn), w_gate, w_up, w_down)


def _tn_matmul(a, b, a_spec, b_spec, groups, m, n, steps, name):
    def body(a_ref, b_ref, o_ref):
        @pl.when(pl.program_id(1) == 0)
        def _():
            o_ref[...] = jnp.zeros_like(o_ref)

        o_ref[...] += _dot_tn(a_ref[...], b_ref[...])

    return pl.pallas_call(
        body, name=name, grid=(groups, steps),
        out_shape=pltpu.HBM((groups, m, n), F32),
        in_specs=[a_spec, b_spec],
        out_specs=pl.BlockSpec((None, m, n), lambda g, i: (g, 0, 0)),
        compiler_params=_params(("parallel", "arbitrary")),
    )(*_hbm(a, b))


def _mix_backward(dx1, mix, mod6, w_o, pooled, w_pool, pool_scale, w_uv_t, o_lat, T, TQ):
    S = dx1.shape[0]

    def body(dx1_ref, mix_ref, mod_ref, wo_ref, pl_ref, wp_ref, ps_ref, wuv_ref, o_ref,
             dmix_ref, dp_ref, do_ref, dr_ref, gp_ref, guv_ref, st_ref):
        @pl.when(pl.program_id(0) == 0)
        def _():
            st_ref[...] = jnp.zeros_like(st_ref)
            gp_ref[...] = jnp.zeros_like(gp_ref)
            guv_ref[...] = jnp.zeros_like(guv_ref)

        dx1 = dx1_ref[...]
        st_ref[0:1, :] += jnp.sum(dx1 * mix_ref[...], axis=0, keepdims=True)
        dmix = (dx1 * mod_ref[2:3, :]).astype(BF16)
        dmix_ref[...] = dmix
        dmi = _dot_nt(dmix, wo_ref[...])
        dym = dmi[:, 0:512].astype(BF16)
        for g in range(len(POOL_WINDOWS)):
            cols = slice(g * POOL_GROUP, (g + 1) * POOL_GROUP)
            dyp = dmi[:, 512 + g * POOL_GROUP:512 + (g + 1) * POOL_GROUP]
            pooled_g = pl_ref[:, cols]
            z = _dot(pooled_g, wp_ref[g])
            st_ref[1:2, cols] += jnp.sum(dyp * z, axis=0, keepdims=True)
            dz = (dyp * ps_ref[:, cols]).astype(BF16)
            gp_ref[g] += _dot_tn(pooled_g, dz)
            dp_ref[:, cols] = _dot_nt(dz, wp_ref[g])
        for h in range(HEADS):
            dym_h = dym[:, h * 128:(h + 1) * 128]
            do = _dot_nt(dym_h, wuv_ref[h]).astype(BF16)
            do_ref[h] = do
            o_h = o_ref[h]
            guv_ref[h] += _dot_tn(o_h, dym_h)
            delta = _col_to_row(jnp.sum(do.astype(F32) * o_h.astype(F32), axis=1, keepdims=True))
            for s in range(T // TQ):
                dr_ref[s, :, h * TQ:(h + 1) * TQ] = delta[:, s * TQ:(s + 1) * TQ]

    row = lambda w: pl.BlockSpec((T, w), lambda i: (i, 0))
    heads = pl.BlockSpec((HEADS, T, KV_LORA), lambda i: (0, i, 0))
    square = jax.ShapeDtypeStruct((4, 128, 128), F32)
    return pl.pallas_call(
        body, name="mix_backward", grid=(S // T,),
        out_shape=[pltpu.HBM((S, D_MODEL), BF16), pltpu.HBM((S, POOL_W), F32), pltpu.HBM((HEADS, S, KV_LORA), BF16),
                   pltpu.HBM((S // TQ, 1, HEADS * TQ), F32), square, square, jax.ShapeDtypeStruct((8, D_MODEL), F32)],
        in_specs=[row(D_MODEL), row(D_MODEL), _full((N_MOD, D_MODEL)), _full((1024, D_MODEL)), row(POOL_W),
                  _full((4, POOL_GROUP, POOL_GROUP)), _full((1, POOL_W)), _full((HEADS, KV_LORA, 128)), heads],
        out_specs=[row(D_MODEL), row(POOL_W), heads,
                   pl.BlockSpec((T // TQ, 1, HEADS * TQ), lambda i: (i, 0, 0)), _full((4, 128, 128)),
                   _full((4, 128, 128)), _full((8, D_MODEL))],
        compiler_params=_params(("arbitrary",)),
    )(*_hbm(dx1, mix, mod6, w_o, pooled, w_pool, pool_scale, w_uv_t, o_lat))


def _attention_bwd(qc, kc, kct, do, lse_rows, delta_rows, TQ):
    S = kc.shape[0]
    R = HEADS * TQ
    nq = S // TQ

    def body(k_ref, kt_ref, q_ref, do_ref, lser_ref, dr_ref, dk_ref, dqt_ref, dk_s, dv_s):
        j = pl.program_id(0)

        @pl.when(j == 0)
        def _():
            def zero(i, carry):
                dqt_ref[i] = jnp.zeros((QK_PAD, R), F32)
                return carry
            lax.fori_loop(0, nq, zero, 0)

        k = k_ref[...]
        kt = kt_ref[...]
        v = k[:, :KV_LORA]
        dk_s[...] = jnp.zeros((TQ, QK_PAD), F32)
        dv_s[...] = jnp.zeros((TQ, KV_LORA), F32)

        def step(i, masked):
            rows = pl.ds(pl.multiple_of(i * TQ, TQ), TQ)
            q = q_ref[:, rows, :].reshape(R, QK_PAD)
            do = do_ref[:, rows, :].reshape(R, KV_LORA)
            st = _dot_nt(k, q) * SM_SCALE
            if masked:
                st = jnp.where(_diag_mask(TQ, R), st, -jnp.inf)
            pt = jnp.exp(st - lser_ref[i])
            dv_s[...] += _dot(pt, do)
            dpt = _dot_nt(v, do)
            dst = (pt * (dpt - dr_ref[i])).astype(BF16)
            dk_s[...] += _dot(dst, q)
            dqt_ref[i] += _dot(kt, dst)

        def loop(i, carry):
            step(i, False)
            return carry

        step(j, True)
        lax.fori_loop(j + 1, nq, loop, 0)
        dk = dk_s[...] * SM_SCALE
        dk_ref[:, 0:KV_LORA] = dk[:, 0:KV_LORA] + dv_s[...]
        dk_ref[:, KV_LORA:QK_PAD] = dk[:, KV_LORA:QK_PAD]

    return pl.pallas_call(
        body, name="attention_bwd", grid=(nq,),
        out_shape=[pltpu.HBM((S, QK_PAD), F32), jax.ShapeDtypeStruct((nq, QK_PAD, R), F32)],
        in_specs=[pl.BlockSpec((TQ, QK_PAD), lambda j: (j, 0)), pl.BlockSpec((None, QK_PAD, TQ), lambda j: (j, 0, 0)),
                  VMEM_SPEC, VMEM_SPEC, VMEM_SPEC, VMEM_SPEC],
        out_specs=[pl.BlockSpec((TQ, QK_PAD), lambda j: (j, 0)), VMEM_SPEC],
        scratch_shapes=[pltpu.VMEM((TQ, QK_PAD), F32), pltpu.VMEM((TQ, KV_LORA), F32)],
        compiler_params=_params(("arbitrary",)),
    )(*_hbm(kc, kct), qc, do, lse_rows, delta_rows)


def _pre_attention_backward(x, dx1, proj, q, dqt, dkc, du, cos, sin, mod6, g_mix, g_q, g_kv, w_in, w_uq, w_uk_t, T, TQ):
    S = x.shape[0]

    def body(x_ref, dx1_ref, proj_ref, q_ref, dqt_ref, dkc_ref, du_ref, cos_ref, sin_ref, mod_ref, gm_ref, gq_ref,
             gkv_ref, win_ref, wuq_ref, wuk_ref, gx_ref, dproj_ref, h1_ref, guk_ref, guq_ref, st_ref, dq_ref):
        @pl.when(pl.program_id(0) == 0)
        def _():
            st_ref[...] = jnp.zeros_like(st_ref)
            guk_ref[...] = jnp.zeros_like(guk_ref)
            guq_ref[...] = jnp.zeros_like(guq_ref)

        cos_t, sin_t = cos_ref[...], sin_ref[...]
        low = lax.broadcasted_iota(jnp.int32, (T, 128), 1) < ROPE
        rope_parts = []
        for h in range(HEADS):
            dqc = jnp.concatenate([jnp.transpose(dqt_ref[s, :, h * TQ:(h + 1) * TQ]) for s in range(T // TQ)], axis=0)
            dqc = dqc * SM_SCALE
            dql = dqc[:, 0:KV_LORA].astype(BF16)
            guk_ref[h] += _dot_tn(dql, q_ref[:, h * NOPE:(h + 1) * NOPE])
            dq_ref[:, h * NOPE:(h + 1) * NOPE] = _dot(dql, wuk_ref[h]).astype(BF16)
            rope_parts.append(dqc[:, KV_LORA:QK_PAD])
        for pair in range(2):
            d = jnp.where(low, rope_parts[2 * pair], rope_parts[2 * pair + 1])
            dq_ref[:, O_QA + 128 * pair:O_QA + 128 * (pair + 1)] = _rope_bwd(d, cos_t, sin_t).astype(BF16)
        dq = dq_ref[...]
        dcq = _dot_nt(dq, wuq_ref[...])
        cqh, rq = _rms(proj_ref[:, 0:Q_LORA])
        guq_ref[...] += _dot_tn(cqh * gq_ref[...], dq)
        st_ref[3:4, 0:Q_LORA] += jnp.sum(dcq * cqh, axis=0, keepdims=True)
        dproj_ref[:, 0:Q_LORA] = _rms_bwd(dcq * gq_ref[...], cqh, rq).astype(BF16)
        dckv = dkc_ref[:, 0:KV_LORA]
        ckvh, rkv = _rms(proj_ref[:, O_CKV:O_KR])
        st_ref[4:5, 0:KV_LORA] += jnp.sum(dckv * ckvh, axis=0, keepdims=True)
        dproj_ref[:, O_CKV:O_KR] = _rms_bwd(dckv * gkv_ref[...], ckvh, rkv).astype(BF16)
        dkr = _rope_bwd(dkc_ref[:, KV_LORA:QK_PAD], cos_t, sin_t)
        dkr = jnp.where(low, dkr + pltpu.roll(dkr, ROPE, 1), 0.0)
        dproj_ref[:, O_KR:O_U] = dkr.astype(BF16)
        dproj_ref[:, O_U:PROJ_W] = du_ref[...].astype(BF16)
        dproj = dproj_ref[...]
        dh1 = jnp.concatenate([_dot(dproj, win_ref[j]) for j in range(N_CHIPS)], axis=1)
        xh, r1 = _rms(x_ref[...])
        n1 = xh * gm_ref[...]
        h1_ref[...] = (n1 * (1.0 + mod_ref[1:2, :]) + mod_ref[0:1, :]).astype(BF16)
        st_ref[0:1, :] += jnp.sum(dh1, axis=0, keepdims=True)
        st_ref[1:2, :] += jnp.sum(dh1 * n1, axis=0, keepdims=True)
        dn1 = dh1 * (1.0 + mod_ref[1:2, :])
        st_ref[2:3, :] += jnp.sum(dn1 * xh, axis=0, keepdims=True)
        gx_ref[...] = _rms_bwd(dn1 * gm_ref[...], xh, r1) + dx1_ref[...]

    row = lambda w: pl.BlockSpec((T, w), lambda i: (i, 0))
    return pl.pallas_call(
        body, name="pre_attention_backward", grid=(S // T,),
        out_shape=[jax.ShapeDtypeStruct((S, D_MODEL), F32), pltpu.HBM((S, PROJ_W), BF16),
                   pltpu.HBM((S, D_MODEL), BF16), jax.ShapeDtypeStruct((HEADS, KV_LORA, NOPE), F32),
                   jax.ShapeDtypeStruct((Q_LORA, Q_W), F32), jax.ShapeDtypeStruct((8, D_MODEL), F32)],
        in_specs=[row(D_MODEL), row(D_MODEL), row(PROJ_W), row(HEADS * NOPE),
                  pl.BlockSpec((T // TQ, QK_PAD, HEADS * TQ), lambda i: (i, 0, 0)),
                  row(QK_PAD), row(POOL_W), row(128), row(128), _full((N_MOD, D_MODEL)), _full((1, D_MODEL)),
                  _full((1, Q_LORA)), _full((1, KV_LORA)), _full((N_CHIPS, PROJ_W, D_MODEL // N_CHIPS)),
                  _full((Q_LORA, Q_W)), _full((HEADS, KV_LORA, NOPE))],
        out_specs=[row(D_MODEL), row(PROJ_W), row(D_MODEL), _full((HEADS, KV_LORA, NOPE)), _full((Q_LORA, Q_W)),
                   _full((8, D_MODEL))],
        scratch_shapes=[pltpu.VMEM((T, Q_W), BF16)],
        compiler_params=_params(("arbitrary",)),
    )(*_hbm(x, dx1, proj, q, dqt, dkc, du, cos, sin, mod6, g_mix, g_q, g_kv, w_in, w_uq, w_uk_t))


def _ada_grads(c_all, dmod_all, chip):
    cols = N_MOD * D_MODEL // N_CHIPS
    width = dmod_all.shape[1]

    def body(col_ref, c_ref, dcol_ref, dall_ref, gw_ref, gb_ref):
        call = c_ref[...]
        act = call * jax.nn.sigmoid(call)
        gw_ref[...] = _dot_tn(act, dcol_ref[...])
        d = dall_ref[...]
        acc = d[0:1, :]
        for b in range(1, 8):
            acc = acc + d[b:b + 1, :]
        gb_ref[...] = acc

    return pl.pallas_call(
        body, name="ada_grads",
        out_shape=[jax.ShapeDtypeStruct((D_MODEL, cols), F32), jax.ShapeDtypeStruct((1, width), F32)],
        grid_spec=pltpu.PrefetchScalarGridSpec(
            num_scalar_prefetch=1, grid=(1,),
            in_specs=[pl.BlockSpec((8, D_MODEL), lambda s, col_ref: (0, 0)),
                      pl.BlockSpec((8, cols), lambda s, col_ref: (0, col_ref[0])),
                      pl.BlockSpec((8, width), lambda s, col_ref: (0, 0))],
            out_specs=[pl.BlockSpec((D_MODEL, cols), lambda s, col_ref: (0, 0)),
                       pl.BlockSpec((1, width), lambda s, col_ref: (0, 0))]),
        compiler_params=_params(("arbitrary",)),
    )(chip, *_hbm(c_all, dmod_all, dmod_all))


def _adamw(w, g, m, v, name):
    rows, cols = w.shape
    T = _row_tile(rows, 256)

    def body(w_ref, g_ref, m_ref, v_ref, d_ref, nm_ref, nv_ref):
        g = g_ref[...]
        m2 = ADAM_B1 * m_ref[...] + (1.0 - ADAM_B1) * g
        v2 = ADAM_B2 * v_ref[...] + (1.0 - ADAM_B2) * (g * g)
        m_hat = m2 / (1.0 - ADAM_B1 ** ADAM_STEP)
        v_hat = v2 / (1.0 - ADAM_B2 ** ADAM_STEP)
        d_ref[...] = -ADAM_LR * (m_hat / (jnp.sqrt(v_hat) + ADAM_EPS) + ADAM_WD * w_ref[...])
        nm_ref[...] = m2
        nv_ref[...] = v2

    spec = pl.BlockSpec((T, cols), lambda i: (i, 0))
    return pl.pallas_call(
        body, name=name, grid=(rows // T,),
        out_shape=[jax.ShapeDtypeStruct((rows, cols), F32)] * 3,
        in_specs=[spec] * 4, out_specs=[spec] * 3,
        compiler_params=_params(("parallel",)),
    )(*_hbm(w, g, m, v))


SMALL_NAMES = ("w_uk", "w_uv", "w_pool", "g_mix", "g_q", "g_kv", "pool_scale", "g_ffn", "g_final", "b_ada")
SMALL_ROWS = 1664


def _pack_rows(parts):
    flat = jnp.concatenate([p.reshape(-1) for p in parts])
    pad = (-flat.shape[0]) % 128
    if pad:
        flat = jnp.concatenate([flat, jnp.zeros((pad,), F32)])
    return flat.reshape(-1, 128)


def kernel(x, c, positions, w_ada, b_ada, g_mix, w_in, g_q, g_kv, w_uq, w_uk, w_uv, w_pool, pool_scale, w_o, g_ffn, w_gate, w_up, w_down, g_final, loss_target, m_w_ada, m_b_ada, m_g_mix, m_w_in, m_g_q, m_g_kv, m_w_uq, m_w_uk, m_w_uv, m_w_pool, m_pool_scale, m_w_o, m_g_ffn, m_w_gate, m_w_up, m_w_down, m_g_final, v_w_ada, v_b_ada, v_g_mix, v_w_in, v_g_q, v_g_kv, v_w_uq, v_w_uk, v_w_uv, v_w_pool, v_pool_scale, v_w_o, v_g_ffn, v_w_gate, v_w_up, v_w_down, v_g_final):
    S = x.shape[1]
    T = _row_tile(S, 512)
    TQ = _row_tile(S, 256)
    TW = _row_tile(S, 1024)
    ix, iy, ic = lax.axis_index("x"), lax.axis_index("y"), lax.axis_index("c")
    chip = (2 * ix + iy).astype(jnp.int32)
    chip_arr = chip.reshape(1)
    core_arr = ic.astype(jnp.int32).reshape(1)

    xs, tgt = x[0], loss_target[0]

    ada_cols = w_ada.shape[2]
    b_cols = lax.dynamic_slice(b_ada, (0, chip * ada_cols), (1, ada_cols))
    mod, c_all = _mod_exchange(c, w_ada[0], b_cols)
    mod6 = mod.reshape(N_MOD, D_MODEL)

    tr = lambda a: jnp.transpose(a[0])
    win_t = tr(w_in)
    win_p = jnp.concatenate([win_t[:O_KR + ROPE], win_t[O_KR:O_KR + ROPE], win_t[O_KR + ROPE:]], axis=0).astype(BF16)
    wuq = w_uq[0]
    wuq_p = jnp.concatenate([wuq[:, h, :NOPE] for h in range(HEADS)] + [wuq[:, h, NOPE:] for h in range(HEADS)],
                            axis=1).astype(BF16)
    first = _weight_gather([win_p, wuq_p])
    w_in_f = first[0]
    w_uq_f = first[1].reshape(Q_LORA, Q_W)
    w_uk_t = jnp.transpose(w_uk[0], (1, 0, 2)).astype(BF16)
    w_uv_t = jnp.transpose(w_uv[0], (1, 0, 2)).astype(BF16)
    w_pool_b = w_pool[0].astype(BF16)
    later = [w_o[0].astype(BF16), tr(w_gate).astype(BF16), tr(w_up).astype(BF16), w_down[0].astype(BF16)]
    wg_lands = _place_shards(chip_arr, later)
    wg_lands, mod6, w_in_f = lax.optimization_barrier((wg_lands, mod6, w_in_f))
    wg_send, wg_recv, wg_lands, token = _split_start(
        "weights_start", later, wg_lands, 3 * len(later), _plan_gather_start)
    mod6 = mod6 + token[0, 0]

    half = ROPE // 2
    freqs = jnp.power(ROPE_THETA, -jnp.arange(half, dtype=F32) / half)
    cos, sin = _rope_tables(positions.reshape(S, 1), jnp.tile(freqs, 4).reshape(1, 128))
    proj, q, qc, kc, kct = _pre_attention(xs, mod6, g_mix, g_q, g_kv, w_in_f, w_uq_f, w_uk_t, cos, sin, T, TQ)
    o_lat, y_mla, lse_rows = _attention_fwd(qc, kc, kct, w_uv_t, TQ)
    wg_send, wg_recv, wg_lands, token = _split_relay(
        "weights_relay", wg_send, wg_recv, later, wg_lands, y_mla, 3 * len(later), _plan_gather_landed,
        _plan_gather_relay)
    pooled = _pool_forward(proj)
    wg_lands = _split_wait("weights_wait", wg_send, wg_recv, [], wg_lands, pooled, _plan_gather_wait)
    w_o_f = wg_lands[0].reshape(1024, D_MODEL)
    w_gate_f, w_up_f, w_down_f = wg_lands[1], wg_lands[2], wg_lands[3]
    x1, mix, mix_in = _mix_out(y_mla, pooled, w_pool_b, pool_scale, w_o_f, xs, mod6, T)
    gate, up, act, h2, dff, dx2, st_f = _ffn_forward(
        x1, mod6, g_ffn, g_final.reshape(1, D_MODEL), tgt, w_gate_f, w_up_f, w_down_f, T)

    dgate, dup, dx1, st_b = _ffn_backward(dx2, x1, dff, gate, up, mod6, g_ffn, w_gate_f, w_up_f, w_down_f, T)
    steps = S // TW
    chunk_spec = pl.BlockSpec((None, TW, FF_CHUNK), lambda g, i: (g, i, 0))
    wide_spec = pl.BlockSpec((TW, D_MODEL), lambda g, i: (i, 0))
    g_down = _tn_matmul(act, dff, chunk_spec, wide_spec, N_CHIPS, FF_CHUNK, D_MODEL, steps, "grad_w_down")
    g_gate = _tn_matmul(dgate, h2, chunk_spec, wide_spec, N_CHIPS, FF_CHUNK, D_MODEL, steps, "grad_w_gate")
    g_up = _tn_matmul(dup, h2, chunk_spec, wide_spec, N_CHIPS, FF_CHUNK, D_MODEL, steps, "grad_w_up")

    half_shapes = lambda gs: [jax.ShapeDtypeStruct((N_CHIPS, g.shape[1] // 2, g.shape[2]), F32) for g in gs]
    ffn_grads = [g_gate, g_up, g_down]
    f_send, f_recv, f_lands, token = _split_start(
        "ffn_swap_start", ffn_grads, half_shapes(ffn_grads), len(ffn_grads), _plan_swap_start)
    dmix, dpooled, do_lat, delta_rows, g_pool, g_uv_t, st_m = _mix_backward(
        dx1, mix, mod6 + token[0, 0], w_o_f, pooled, w_pool_b, pool_scale, w_uv_t, o_lat, T, TQ)
    g_o = [_tn_matmul(mix_in, dmix, wide_spec, wide_spec, 1, 1024, D_MODEL, steps, "grad_w_o").reshape(N_CHIPS, -1, D_MODEL)]
    o_send, o_recv, o_lands, token = _split_start("w_o_swap_start", g_o, half_shapes(g_o), 1, _plan_swap_start)
    du = _pool_backward(dpooled, token)
    f_got = _split_wait("ffn_swap_wait", f_send, f_recv, ffn_grads, f_lands, du, _plan_swap_wait)
    f_got += _split_wait("w_o_swap_wait", o_send, o_recv, g_o, o_lands, du, _plan_swap_wait)
    far_names = ("w_gate", "w_up", "w_down", "w_o")
    far_grads = ffn_grads + g_o
    f_sums = [_add_my_half(core_arr, a, b, "add_half_" + n) for a, b, n in zip(far_grads, f_got, far_names)]
    f_send, f_recv, f_lands, token = _split_start(
        "far_exchange_start", f_sums, [jax.ShapeDtypeStruct((3,) + s.shape[1:], F32) for s in f_sums],
        3 * len(f_sums), _plan_exchange_start)
    delta_rows = delta_rows + token[0, 0]
    dkc, dqt = _attention_bwd(qc, kc, kct, do_lat, lse_rows, delta_rows, TQ)
    grad_x, dproj, h1, g_uk_t, uq, st_p = _pre_attention_backward(
        xs, dx1, proj, q, dqt, dkc, du, cos, sin, mod6, g_mix, g_q, g_kv, w_in_f, w_uq_f, w_uk_t, T, TQ)
    rows_in = D_MODEL // N_CHIPS
    g_in_p = _tn_matmul(dproj, h1, pl.BlockSpec((TW, PROJ_W), lambda g, i: (i, 0)),
                        pl.BlockSpec((TW, rows_in), lambda g, i: (i, g)), N_CHIPS, PROJ_W, rows_in, steps, "grad_w_in")

    g_in = jnp.concatenate([g_in_p[:, :O_KR + ROPE], g_in_p[:, O_U:]], axis=1)
    g_uq = jnp.concatenate([jnp.concatenate([uq[:, h * NOPE:(h + 1) * NOPE], uq[:, O_QA + h * ROPE:O_QA + (h + 1) * ROPE]],
                                            axis=1) for h in range(HEADS)], axis=1).reshape(N_CHIPS, -1, HEADS * HEAD_QK)
    small = _pack_rows([g_uk_t, g_uv_t, g_pool, st_p[2], st_p[3, :Q_LORA], st_p[4, :KV_LORA], st_m[1, :POOL_W],
                        st_b[2], st_f[0]])
    small = jnp.concatenate([small, jnp.zeros((SMALL_ROWS - small.shape[0], 128), F32)]).reshape(N_CHIPS, -1, 128)
    grads = [g_in, g_uq, small]
    dmod = jnp.concatenate([jnp.stack([st_p[0], st_p[1], st_m[0], st_b[0], st_b[1], st_f[1]]).reshape(48, 128),
                            jnp.zeros((8, 128), F32).at[0, 0].set(st_f[2, 0])])

    f_others = _split_wait("far_exchange_wait", f_send, f_recv, f_sums, f_lands, g_in_p, _plan_exchange_wait)
    chip_core = jnp.concatenate([chip_arr, core_arr])
    f_pairs = [_add_chips_into_pair(chip_core, a, b, "add_chips_" + n) for a, b, n in zip(f_sums, f_others, far_names)]
    f_send, f_recv, f_pairs, token = _split_start("far_finish_start", [], f_pairs, len(f_pairs), _plan_finish_start)
    dmod = dmod + token[0, 0]

    names = ("w_in", "w_uq", "small")
    got, dmod_all = _grad_swap_halves(grads, dmod)
    chip_sums = [_add_my_half(core_arr, a, b, "add_half_" + n) for a, b, n in zip(grads, got, names)]
    n_send, n_recv, n_lands, token = _split_start(
        "near_exchange_start", chip_sums, [jax.ShapeDtypeStruct((3,) + s.shape[1:], F32) for s in chip_sums],
        3 * len(chip_sums), _plan_exchange_start)
    f_fulls = _split_wait("far_finish_wait", f_send, f_recv, [], f_pairs, token, _plan_finish_wait)
    gw_gate, gw_up, gw_down, gw_o = [f.reshape(-1, f.shape[2]) for f in f_fulls]
    gw_ada, gb_ada = _ada_grads(c_all, dmod_all.reshape(8, -1), chip_arr)
    loss = gb_ada[0, N_MOD * D_MODEL]
    gb_ada = gb_ada[:, :N_MOD * D_MODEL]

    untr = lambda a: jnp.transpose(a)[None]
    grad_out, delta_out, newm_out, newv_out = {}, {}, {}, {}

    def adam_sharded(n, w, g2, m, v, transposed):
        view = (lambda a: jnp.transpose(a[0])) if transposed else (lambda a: a.reshape(g2.shape))
        back = untr if transposed else (lambda a: a.reshape(w.shape))
        d_, m_, v_ = _adamw(view(w), g2, view(m), view(v), "adamw_" + n)
        grad_out[n], delta_out[n], newm_out[n], newv_out[n] = back(g2), back(d_), back(m_), back(v_)
        return d_

    done = [adam_sharded("w_gate", w_gate, gw_gate, m_w_gate, v_w_gate, True),
            adam_sharded("w_up", w_up, gw_up, m_w_up, v_w_up, True),
            adam_sharded("w_down", w_down, gw_down, m_w_down, v_w_down, False),
            adam_sharded("w_o", w_o, gw_o, m_w_o, v_w_o, False),
            adam_sharded("w_ada", w_ada, gw_ada, m_w_ada, v_w_ada, False)]
    after_all = jnp.stack([d[0, 0] for d in done])

    others = _split_wait("near_exchange_wait", n_send, n_recv, chip_sums, n_lands, after_all, _plan_exchange_wait)
    halves = [_add_chips(chip_arr, a, b, "add_chips_" + n) for a, b, n in zip(chip_sums, others, names)]
    fulls, small_all = _grad_finish(halves[:2], halves[2])
    gw_in, gw_uq = [f.reshape(-1, f.shape[2]) for f in fulls]
    small_all = small_all.reshape(SMALL_ROWS * 128)
    adam_sharded("w_in", w_in, gw_in, m_w_in, v_w_in, True)
    adam_sharded("w_uq", w_uq, gw_uq, m_w_uq, v_w_uq, False)

    n_sq = KV_LORA * HEADS * 128
    sizes = [n_sq, n_sq, n_sq, D_MODEL, Q_LORA, KV_LORA, POOL_W, D_MODEL, D_MODEL]
    offs = [0]
    for s_ in sizes:
        offs.append(offs[-1] + s_)
    piece = lambda k: small_all[offs[k]:offs[k + 1]]
    grads_small = {
        "w_uk": jnp.transpose(piece(0).reshape(HEADS, KV_LORA, NOPE), (1, 0, 2)),
        "w_uv": jnp.transpose(piece(1).reshape(HEADS, KV_LORA, 128), (1, 0, 2)),
        "w_pool": piece(2).reshape(4, POOL_GROUP, POOL_GROUP),
        "g_mix": piece(3), "g_q": piece(4), "g_kv": piece(5), "pool_scale": piece(6), "g_ffn": piece(7),
        "g_final": piece(8), "b_ada": gb_ada.reshape(-1),
    }
    weights_small = {"w_uk": w_uk, "w_uv": w_uv, "w_pool": w_pool, "g_mix": g_mix, "g_q": g_q, "g_kv": g_kv,
                     "pool_scale": pool_scale, "g_ffn": g_ffn, "g_final": g_final, "b_ada": b_ada}
    m_small = {"w_uk": m_w_uk, "w_uv": m_w_uv, "w_pool": m_w_pool, "g_mix": m_g_mix, "g_q": m_g_q, "g_kv": m_g_kv,
               "pool_scale": m_pool_scale, "g_ffn": m_g_ffn, "g_final": m_g_final, "b_ada": m_b_ada}
    v_small = {"w_uk": v_w_uk, "w_uv": v_w_uv, "w_pool": v_w_pool, "g_mix": v_g_mix, "g_q": v_g_q, "g_kv": v_g_kv,
               "pool_scale": v_pool_scale, "g_ffn": v_g_ffn, "g_final": v_g_final, "b_ada": v_b_ada}
    pack = lambda d: _pack_rows([d[n] for n in SMALL_NAMES])
    d_s, m_s, v_s = _adamw(pack(weights_small), pack(grads_small), pack(m_small), pack(v_small), "adamw_small")

    def unpack(flat2d):
        flat = flat2d.reshape(-1)
        out, o = {}, 0
        for n in SMALL_NAMES:
            size = weights_small[n].size
            out[n] = flat[o:o + size].reshape(weights_small[n].shape)
            o += size
        return out

    delta_s, newm_s, newv_s = unpack(d_s), unpack(m_s), unpack(v_s)

    for n in SMALL_NAMES:
        grad_out[n] = grads_small[n].reshape(weights_small[n].shape)
        delta_out[n], newm_out[n], newv_out[n] = delta_s[n], newm_s[n], newv_s[n]

    order = ("w_ada", "b_ada", "g_mix", "w_in", "g_q", "g_kv", "w_uq", "w_uk", "w_uv", "w_pool", "pool_scale", "w_o",
             "g_ffn", "w_gate", "w_up", "w_down", "g_final")
    return (loss, grad_x.reshape(x.shape), *[grad_out[n] for n in order], *[delta_out[n] for n in order],
            *[newm_out[n] for n in order], *[newv_out[n] for n in order])
```

```python
import functools

import jax
import jax.numpy as jnp
from jax import lax
from jax.experimental import pallas as pl
from jax.experimental.pallas import tpu as pltpu

F32 = jnp.float32
BF16 = jnp.bfloat16

D_MODEL = 1024
HEADS = 4
NOPE = 128
ROPE = 64
HEAD_QK = NOPE + ROPE
Q_LORA = 256
KV_LORA = 128
POOL_W = 512
POOL_WINDOWS = (2, 4, 8, 16)
POOL_GROUP = 128
POOL_PAD = 16
D_FF = 2816
N_CHIPS = 4
FF_CHUNK = D_FF // N_CHIPS
N_MOD = 6
EPS = 1e-6
SM_SCALE = HEAD_QK ** -0.5
ROPE_THETA = 10000.0
QK_PAD = 256
CHUNK = 64
CHUNK_SHIFT = 6

ADAM_LR = 0.001
ADAM_B1 = 0.9
ADAM_B2 = 0.999
ADAM_EPS = 1e-08
ADAM_WD = 0.01
ADAM_STEP = 10

VMEM_LIMIT = 48 * 1024 * 1024
MESH = pl.DeviceIdType.MESH
ANY = pl.BlockSpec(memory_space=pl.ANY)
VMEM_SPEC = pl.BlockSpec(memory_space=pltpu.VMEM)

PROJ_W = 1024
O_CKV = 256
O_KR = 384
O_U = 512
Q_W = 768
O_QA = 512
O_QB = 640


def _params(sem=None, vmem=VMEM_LIMIT):
    kw = dict(vmem_limit_bytes=vmem)
    if sem is not None:
        kw["dimension_semantics"] = sem
    return pltpu.CompilerParams(**kw)


def _dot(a, b):
    return jnp.dot(a.astype(BF16), b.astype(BF16), preferred_element_type=F32)


def _dot_nt(a, b):
    return lax.dot_general(a.astype(BF16), b.astype(BF16), (((1,), (1,)), ((), ())), preferred_element_type=F32)


def _dot_tn(a, b):
    return lax.dot_general(a.astype(BF16), b.astype(BF16), (((0,), (0,)), ((), ())), preferred_element_type=F32)


def _row_tile(rows, target):
    best = rows
    for t in range(8, min(rows, target) + 1, 8):
        if rows % t == 0:
            best = t
    return best if rows % best == 0 and best <= target else rows


def _rms(x):
    r = lax.rsqrt(jnp.mean(x * x, axis=-1, keepdims=True) + EPS)
    return x * r, r


def _rms_bwd(dxh, xh, r):
    return r * (dxh - xh * jnp.mean(dxh * xh, axis=-1, keepdims=True))


def _lane_first_half(shape):
    lane = lax.broadcasted_iota(jnp.int32, shape, 1)
    return (lane & (ROPE - 1)) < (ROPE // 2)


def _rope(a, cos, sin):
    first = _lane_first_half(a.shape)
    up = pltpu.roll(a, 96, 1)
    dn = pltpu.roll(a, 32, 1)
    return a * cos + jnp.where(first, -up, dn) * sin


def _rope_bwd(d, cos, sin):
    first = _lane_first_half(d.shape)
    up = pltpu.roll(d, 96, 1)
    dn = pltpu.roll(d, 32, 1)
    return d * cos + jnp.where(first, up, -dn) * sin


RELATIONS = tuple((dx, dy, dc) for dx in (0, 1) for dy in (0, 1) for dc in (0, 1) if (dx, dy, dc) != (0, 0, 0))
CHIP_RELATIONS = ((1, 0), (0, 1), (1, 1))


def _flip(v, d):
    return 1 - v if d else v


def _place():
    return lax.axis_index("x"), lax.axis_index("y"), lax.axis_index("c")


def _remote(src, dst, send_sem, recv_sem, target):
    return pltpu.make_async_remote_copy(src_ref=src, dst_ref=dst, send_sem=send_sem, recv_sem=recv_sem,
                                        device_id=target, device_id_type=MESH)


def _mod_exchange(c_row, w_ada, b_ada):
    cols = w_ada.shape[1]

    def body(c_ref, w_ref, b_ref, mod_ref, call_ref, part_ref, send1, recv1, loc1, send2, recv2, loc2):
        x, y, c = _place()
        me = 4 * x + 2 * y + c
        own = pltpu.make_async_copy(c_ref, call_ref.at[pl.ds(me, 1)], loc1)
        own.start()
        sends = []
        for k, (dx, dy, dc) in enumerate(RELATIONS):
            cp = _remote(c_ref, call_ref.at[pl.ds(me, 1)], send1.at[k], recv1.at[k],
                         (_flip(x, dx), _flip(y, dy), _flip(c, dc)))
            cp.start()
            sends.append(cp)
        for k, (dx, dy, dc) in enumerate(RELATIONS):
            src = 4 * _flip(x, dx) + 2 * _flip(y, dy) + _flip(c, dc)
            _remote(c_ref, call_ref.at[pl.ds(src, 1)], send1.at[k], recv1.at[k], (x, y, c)).wait_recv()
        own.wait()
        for cp in sends:
            cp.wait_send()
        call = call_ref[...]
        act = call * jax.nn.sigmoid(call)
        part_ref[...] = _dot(act, w_ref[...]) + b_ref[...]
        chip = 2 * x + y
        mine = pltpu.make_async_copy(part_ref.at[pl.ds(me, 1)], mod_ref.at[pl.ds(chip, 1)], loc2)
        mine.start()
        sends = []
        for k, (dx, dy) in enumerate(CHIP_RELATIONS):
            tx, ty = _flip(x, dx), _flip(y, dy)
            tb = 4 * tx + 2 * ty + c
            cp = _remote(part_ref.at[pl.ds(tb, 1)], mod_ref.at[pl.ds(chip, 1)], send2.at[k], recv2.at[k], (tx, ty, c))
            cp.start()
            sends.append(cp)
        for k, (dx, dy) in enumerate(CHIP_RELATIONS):
            src_chip = 2 * _flip(x, dx) + _flip(y, dy)
            _remote(part_ref.at[pl.ds(me, 1)], mod_ref.at[pl.ds(src_chip, 1)], send2.at[k], recv2.at[k],
                    (x, y, c)).wait_recv()
        mine.wait()
        for cp in sends:
            cp.wait_send()

    return pl.pallas_call(
        body, name="mod_exchange",
        out_shape=[jax.ShapeDtypeStruct((N_CHIPS, cols), F32), jax.ShapeDtypeStruct((8, D_MODEL), F32)],
        in_specs=[VMEM_SPEC, VMEM_SPEC, VMEM_SPEC], out_specs=[VMEM_SPEC, VMEM_SPEC],
        scratch_shapes=[pltpu.VMEM((8, cols), F32),
                        pltpu.SemaphoreType.DMA((7,)), pltpu.SemaphoreType.DMA((7,)), pltpu.SemaphoreType.DMA,
                        pltpu.SemaphoreType.DMA((3,)), pltpu.SemaphoreType.DMA((3,)), pltpu.SemaphoreType.DMA],
        compiler_params=_params(),
    )(c_row, w_ada, b_ada)


HBM_SPEC = pl.BlockSpec(memory_space=pltpu.HBM)
SEM_SPEC = pl.BlockSpec(memory_space=pltpu.SEMAPHORE)
DATAFLOW = pltpu.SideEffectType.DATAFLOW_SIDE_EFFECTING


def _in_hbm(a):
    return pltpu.with_memory_space_constraint(a, pltpu.HBM)


def _hbm(*arrays):
    return tuple(_in_hbm(a) for a in arrays)


def _hbm_like(arrays):
    return [pltpu.HBM(a.shape, a.dtype) for a in arrays]


def _split_start(name, srcs, lands, n_remote, plan):
    lands = [lax.empty(a.shape, a.dtype) if isinstance(a, jax.ShapeDtypeStruct) else a for a in lands]
    n, m = len(srcs), len(lands)

    def body(*refs):
        src_refs, land_refs = refs[:n], refs[n:n + m]
        send_sems, recv_sems, token = refs[n + m], refs[n + m + 1], refs[n + 2 * m + 2]
        remote = plan(_place(), src_refs, land_refs)
        assert len(remote) == n_remote
        for i, (s, d, target) in enumerate(remote):
            _remote(s, d, send_sems.at[i], recv_sems.at[i], target).start()
        token[...] = jnp.zeros_like(token)

    res = pl.pallas_call(
        body, name=name,
        out_shape=(pltpu.SemaphoreType.DMA((n_remote,)), pltpu.SemaphoreType.DMA((n_remote,)),
                   *_hbm_like(lands), jax.ShapeDtypeStruct((8, 128), F32)),
        in_specs=[HBM_SPEC] * (n + m),
        out_specs=(SEM_SPEC, SEM_SPEC, *([HBM_SPEC] * m), VMEM_SPEC),
        input_output_aliases={n + i: 2 + i for i in range(m)},
        compiler_params=pltpu.CompilerParams(has_side_effects=DATAFLOW),
    )(*[_in_hbm(a) for a in srcs], *[_in_hbm(a) for a in lands])
    return res[0], res[1], list(res[2:2 + m]), res[2 + m]


def _split_wait(name, send_sems, recv_sems, srcs, lands, after, plan):
    n, m = len(srcs), len(lands)

    def body(*refs):
        src_refs, land_refs = refs[:n], refs[n:n + m]
        send_sems, recv_sems = refs[n + m], refs[n + m + 1]
        place = _place()
        for i, (s, d) in enumerate(plan(place, src_refs, land_refs)):
            cp = _remote(s, d, send_sems.at[i], recv_sems.at[i], place)
            cp.wait_send()
            cp.wait_recv()

    res = pl.pallas_call(
        body, name=name,
        out_shape=tuple(_hbm_like(lands)),
        in_specs=[HBM_SPEC] * (n + m) + [SEM_SPEC, SEM_SPEC, ANY],
        out_specs=tuple([HBM_SPEC] * m),
        input_output_aliases={n + i: i for i in range(m)},
        compiler_params=pltpu.CompilerParams(has_side_effects=DATAFLOW),
    )(*srcs, *lands, send_sems, recv_sems, after)
    return list(res)


def _split_relay(name, send_sems, recv_sems, srcs, lands, after, n_remote, plan_wait, plan_send):
    n, m = len(srcs), len(lands)

    def body(*refs):
        src_refs, land_refs = refs[:n], refs[n:n + m]
        old_send, old_recv = refs[n + m], refs[n + m + 1]
        new_send, new_recv = refs[n + m + 3], refs[n + m + 4]
        token = refs[n + m + 5 + m]
        place = _place()
        for i, (s, d) in enumerate(plan_wait(place, src_refs, land_refs)):
            cp = _remote(s, d, old_send.at[i], old_recv.at[i], place)
            cp.wait_send()
            cp.wait_recv()
        for i, (s, d, target) in enumerate(plan_send(place, land_refs)):
            _remote(s, d, new_send.at[i], new_recv.at[i], target).start()
        token[...] = jnp.zeros_like(token)

    res = pl.pallas_call(
        body, name=name,
        out_shape=(pltpu.SemaphoreType.DMA((n_remote,)), pltpu.SemaphoreType.DMA((n_remote,)),
                   *_hbm_like(lands), jax.ShapeDtypeStruct((8, 128), F32)),
        in_specs=[HBM_SPEC] * (n + m) + [SEM_SPEC, SEM_SPEC, ANY],
        out_specs=(SEM_SPEC, SEM_SPEC, *([HBM_SPEC] * m), VMEM_SPEC),
        input_output_aliases={n + i: 2 + i for i in range(m)},
        compiler_params=pltpu.CompilerParams(has_side_effects=DATAFLOW),
    )(*srcs, *lands, send_sems, recv_sems, after)
    return res[0], res[1], list(res[2:2 + m]), res[2 + m]


def _half(ref, core, axis=0):
    hr = ref.shape[axis] // 2
    return pl.ds(core * hr, hr)


def _plan_gather_start(place, src, land):
    x, y, c = place
    chip = 2 * x + y
    return [(s.at[_half(s, c)], l.at[chip, _half(s, c)], (_flip(x, dx), _flip(y, dy), c))
            for s, l in zip(src, land) for dx, dy in CHIP_RELATIONS]


def _plan_gather_landed(place, src, land):
    x, y, c = place
    return [(s.at[_half(s, c)], l.at[2 * _flip(x, dx) + _flip(y, dy), _half(s, c)])
            for s, l in zip(src, land) for dx, dy in CHIP_RELATIONS]


def _plan_gather_relay(place, land):
    x, y, c = place
    out = []
    for l in land:
        for dx, dy in CHIP_RELATIONS:
            got = l.at[2 * _flip(x, dx) + _flip(y, dy), _half(l, c, 1)]
            out.append((got, got, (x, y, 1 - c)))
    return out


def _plan_gather_wait(place, src, land):
    x, y, c = place
    out = []
    for l in land:
        for dx, dy in CHIP_RELATIONS:
            got = l.at[2 * _flip(x, dx) + _flip(y, dy), _half(l, 1 - c, 1)]
            out.append((got, got))
    return out


def _plan_swap_start(place, src, land):
    x, y, c = place
    return [(s.at[:, _half(s, 1 - c, 1), :], l, (x, y, 1 - c)) for s, l in zip(src, land)]


def _plan_swap_wait(place, src, land):
    return [(s.at[:, _half(s, 0, 1), :], l) for s, l in zip(src, land)]


def _plan_exchange_start(place, src, land):
    x, y, c = place
    remote = []
    for s, l in zip(src, land):
        for k, (dx, dy) in enumerate(CHIP_RELATIONS):
            tx, ty = _flip(x, dx), _flip(y, dy)
            remote.append((s.at[2 * tx + ty], l.at[k], (tx, ty, c)))
    return remote


def _plan_exchange_wait(place, src, land):
    return [(s.at[0], l.at[k]) for s, l in zip(src, land) for k in range(3)]


def _plan_finish_start(place, src, land):
    x, y, c = place
    return [(l.at[c], l.at[c], (x, y, 1 - c)) for l in land]


def _plan_finish_wait(place, src, land):
    x, y, c = place
    return [(l.at[c], l.at[1 - c]) for l in land]


def _grad_swap_halves(grads, dmod):
    n = len(grads)

    def body(*refs):
        ins, dmod_ref = refs[:n], refs[n]
        outs, dall_ref = refs[n + 1:2 * n + 1], refs[2 * n + 1]
        send_sems, recv_sems, dsend, drecv, dloc = refs[2 * n + 2:]
        x, y, c = _place()
        me = 4 * x + 2 * y + c
        sends = []
        for w in range(n):
            hr = ins[w].shape[1] // 2
            cp = _remote(ins[w].at[:, pl.ds((1 - c) * hr, hr), :], outs[w], send_sems.at[w], recv_sems.at[w],
                         (x, y, 1 - c))
            cp.start()
            sends.append(cp)
        own = pltpu.make_async_copy(dmod_ref, dall_ref.at[me], dloc)
        own.start()
        for k, (dx, dy, dc) in enumerate(RELATIONS):
            cp = _remote(dmod_ref, dall_ref.at[me], dsend.at[k], drecv.at[k],
                         (_flip(x, dx), _flip(y, dy), _flip(c, dc)))
            cp.start()
            sends.append(cp)
        for k, (dx, dy, dc) in enumerate(RELATIONS):
            src = 4 * _flip(x, dx) + 2 * _flip(y, dy) + _flip(c, dc)
            _remote(dmod_ref, dall_ref.at[src], dsend.at[k], drecv.at[k], (x, y, c)).wait_recv()
        for w in range(n):
            _remote(outs[w], outs[w], send_sems.at[w], recv_sems.at[w], (x, y, c)).wait_recv()
        own.wait()
        for cp in sends:
            cp.wait_send()

    out_shape = [pltpu.HBM((N_CHIPS, g.shape[1] // 2, g.shape[2]), F32) for g in grads]
    out_shape.append(pltpu.HBM((8,) + dmod.shape, F32))
    res = pl.pallas_call(
        body, name="grad_swap_halves",
        out_shape=out_shape, in_specs=[ANY] * n + [VMEM_SPEC], out_specs=[ANY] * (n + 1),
        scratch_shapes=[pltpu.SemaphoreType.DMA((n,)), pltpu.SemaphoreType.DMA((n,)),
                        pltpu.SemaphoreType.DMA((7,)), pltpu.SemaphoreType.DMA((7,)), pltpu.SemaphoreType.DMA],
        compiler_params=_params(),
    )(*grads, dmod)
    return res[:n], res[n]


def _grad_finish(halves, small_half):
    n = len(halves)

    def body(*refs):
        ins, sm_ref = refs[:n], refs[n]
        outs, sall_ref = refs[n + 1:2 * n + 1], refs[2 * n + 1]
        send_sems, recv_sems, loc_sems, ssend, srecv, sloc = refs[2 * n + 2:]
        x, y, c = _place()
        chip = 2 * x + y
        local, sends = [], []
        for w in range(n):
            cp = pltpu.make_async_copy(ins[w], outs[w].at[c], loc_sems.at[w])
            cp.start()
            local.append(cp)
            cp = _remote(ins[w], outs[w].at[c], send_sems.at[w], recv_sems.at[w], (x, y, 1 - c))
            cp.start()
            sends.append(cp)
        cp = pltpu.make_async_copy(sm_ref, sall_ref.at[chip, c], sloc)
        cp.start()
        local.append(cp)
        for k, (dx, dy, dc) in enumerate(RELATIONS):
            cp = _remote(sm_ref, sall_ref.at[chip, c], ssend.at[k], srecv.at[k],
                         (_flip(x, dx), _flip(y, dy), _flip(c, dc)))
            cp.start()
            sends.append(cp)
        for k, (dx, dy, dc) in enumerate(RELATIONS):
            got = sall_ref.at[2 * _flip(x, dx) + _flip(y, dy), _flip(c, dc)]
            _remote(got, got, ssend.at[k], srecv.at[k], (x, y, c)).wait_recv()
        for w in range(n):
            got = outs[w].at[1 - c]
            _remote(got, got, send_sems.at[w], recv_sems.at[w], (x, y, c)).wait_recv()
        for cp in sends:
            cp.wait_send()
        for cp in local:
            cp.wait()

    out_shape = [jax.ShapeDtypeStruct((2,) + h.shape, F32) for h in halves]
    out_shape.append(jax.ShapeDtypeStruct((N_CHIPS, 2) + small_half.shape, F32))
    res = pl.pallas_call(
        body, name="grad_finish",
        out_shape=out_shape, in_specs=[VMEM_SPEC] * (n + 1), out_specs=[ANY] * (n + 1),
        scratch_shapes=[pltpu.SemaphoreType.DMA((n,)), pltpu.SemaphoreType.DMA((n,)), pltpu.SemaphoreType.DMA((n,)),
                        pltpu.SemaphoreType.DMA((7,)), pltpu.SemaphoreType.DMA((7,)), pltpu.SemaphoreType.DMA],
        compiler_params=_params(),
    )(*halves, small_half)
    return res[:n], res[n]


def _add_my_half(core, full, got, name):
    _, hr, cols = got.shape

    def body(core_ref, a_ref, b_ref, o_ref):
        o_ref[...] = a_ref[...] + b_ref[...]

    return pl.pallas_call(
        body, name=name,
        out_shape=pltpu.HBM(got.shape, F32),
        grid_spec=pltpu.PrefetchScalarGridSpec(
            num_scalar_prefetch=1, grid=(N_CHIPS,),
            in_specs=[pl.BlockSpec((None, hr, cols), lambda s, core_ref: (s, core_ref[0], 0)),
                      pl.BlockSpec((None, hr, cols), lambda s, core_ref: (s, 0, 0))],
            out_specs=pl.BlockSpec((None, hr, cols), lambda s, core_ref: (s, 0, 0))),
        compiler_params=_params(("arbitrary",)),
    )(core, *_hbm(full, got))


def _add_chips(chip, mine, got, name):
    _, hr, cols = mine.shape

    def body(chip_ref, a_ref, b_ref, o_ref):
        o_ref[...] = ((a_ref[...] + b_ref[0]) + b_ref[1]) + b_ref[2]

    return pl.pallas_call(
        body, name=name,
        out_shape=jax.ShapeDtypeStruct((hr, cols), F32),
        grid_spec=pltpu.PrefetchScalarGridSpec(
            num_scalar_prefetch=1, grid=(1,),
            in_specs=[pl.BlockSpec((None, hr, cols), lambda s, chip_ref: (chip_ref[0], 0, 0)),
                      pl.BlockSpec((3, hr, cols), lambda s, chip_ref: (0, 0, 0))],
            out_specs=pl.BlockSpec((hr, cols), lambda s, chip_ref: (0, 0))),
        compiler_params=_params(("arbitrary",)),
    )(chip, *_hbm(mine, got))


def _add_chips_into_pair(chip_core, mine, got, name):
    _, hr, cols = mine.shape

    def body(cc_ref, a_ref, b_ref, o_ref):
        o_ref[...] = ((a_ref[...] + b_ref[0]) + b_ref[1]) + b_ref[2]

    return pl.pallas_call(
        body, name=name,
        out_shape=pltpu.HBM((2, hr, cols), F32),
        grid_spec=pltpu.PrefetchScalarGridSpec(
            num_scalar_prefetch=1, grid=(1,),
            in_specs=[pl.BlockSpec((None, hr, cols), lambda s, cc_ref: (cc_ref[0], 0, 0)),
                      pl.BlockSpec((3, hr, cols), lambda s, cc_ref: (0, 0, 0))],
            out_specs=pl.BlockSpec((None, hr, cols), lambda s, cc_ref: (cc_ref[1], 0, 0))),
        compiler_params=_params(("arbitrary",)),
    )(chip_core, *_hbm(mine, got))


def _place_shards(chip, shards):
    n = len(shards)

    def body(chip_ref, *refs):
        for w in range(n):
            refs[n + w][...] = refs[w][...]

    return pl.pallas_call(
        body, name="place_shards",
        out_shape=[pltpu.HBM((N_CHIPS,) + s.shape, s.dtype) for s in shards],
        grid_spec=pltpu.PrefetchScalarGridSpec(
            num_scalar_prefetch=1, grid=(1,),
            in_specs=[pl.BlockSpec(s.shape, lambda i, chip_ref: (0, 0)) for s in shards],
            out_specs=[pl.BlockSpec((None,) + s.shape, lambda i, chip_ref: (chip_ref[0], 0, 0)) for s in shards]),
        compiler_params=_params(("arbitrary",)),
    )(chip, *shards)


def _rope_tables(pos_col, freqs):
    S = pos_col.shape[0]
    T = _row_tile(S, 1024)

    def body(p_ref, f_ref, cos_ref, sin_ref):
        ang = p_ref[...].astype(F32) * f_ref[...]
        cos_ref[...] = jnp.cos(ang)
        sin_ref[...] = jnp.sin(ang)

    return pl.pallas_call(
        body, name="rope_tables", grid=(S // T,),
        out_shape=[pltpu.HBM((S, 128), F32)] * 2,
        in_specs=[pl.BlockSpec((T, 1), lambda i: (i, 0)), pl.BlockSpec((1, 128), lambda i: (0, 0))],
        out_specs=[pl.BlockSpec((T, 128), lambda i: (i, 0))] * 2,
        compiler_params=_params(("parallel",)),
    )(*_hbm(pos_col, freqs))


def _full(shape):
    zeros = (0,) * len(shape)
    return pl.BlockSpec(shape, lambda *_: zeros)


def _pre_attention(x, mod6, g_mix, g_q, g_kv, w_in, w_uq, w_uk_t, cos, sin, T, TQ):
    S = x.shape[0]

    def body(x_ref, mod_ref, gm_ref, gq_ref, gkv_ref, win_ref, wuq_ref, wuk_ref, cos_ref, sin_ref,
             proj_ref, q_ref, qc_ref, kc_ref, kct_ref):
        xh, _ = _rms(x_ref[...])
        h1 = ((xh * gm_ref[...]) * (1.0 + mod_ref[1:2, :]) + mod_ref[0:1, :]).astype(BF16)
        rows_in = D_MODEL // N_CHIPS
        proj = _dot_nt(h1[:, 0:rows_in], win_ref[0])
        for j in range(1, N_CHIPS):
            proj = proj + _dot_nt(h1[:, j * rows_in:(j + 1) * rows_in], win_ref[j])
        proj_ref[...] = proj
        cqh, _ = _rms(proj[:, :Q_LORA])
        c_q = cqh * gq_ref[...]
        ckvh, _ = _rms(proj[:, O_CKV:O_KR])
        c_kv = ckvh * gkv_ref[...]
        q = _dot(c_q, wuq_ref[...])
        q_ref[...] = q
        cos_t, sin_t = cos_ref[...], sin_ref[...]
        ropes = (_rope(q[:, O_QA:O_QB], cos_t, sin_t), _rope(q[:, O_QB:Q_W], cos_t, sin_t))
        low = lax.broadcasted_iota(jnp.int32, (T, 128), 1) < ROPE
        for h in range(HEADS):
            q_lat = _dot_nt(q[:, h * NOPE:(h + 1) * NOPE], wuk_ref[h])
            keep = low if h % 2 == 0 else jnp.logical_not(low)
            qc_ref[h, :, 0:KV_LORA] = q_lat.astype(BF16)
            qc_ref[h, :, KV_LORA:QK_PAD] = jnp.where(keep, ropes[h // 2], 0.0).astype(BF16)
        k_rope = _rope(proj[:, O_KR:O_U], cos_t, sin_t)
        kc_ref[:, 0:KV_LORA] = c_kv.astype(BF16)
        kc_ref[:, KV_LORA:QK_PAD] = k_rope.astype(BF16)
        lat_t, rope_t = jnp.transpose(c_kv), jnp.transpose(k_rope)
        for s in range(T // TQ):
            kct_ref[s, 0:KV_LORA, :] = lat_t[:, s * TQ:(s + 1) * TQ].astype(BF16)
            kct_ref[s, KV_LORA:QK_PAD, :] = rope_t[:, s * TQ:(s + 1) * TQ].astype(BF16)

    row = lambda w: pl.BlockSpec((T, w), lambda i: (i, 0))
    return pl.pallas_call(
        body, name="pre_attention", grid=(S // T,),
        out_shape=[pltpu.HBM((S, PROJ_W), F32), pltpu.HBM((S, Q_W), F32), pltpu.HBM((HEADS, S, QK_PAD), BF16),
                   pltpu.HBM((S, QK_PAD), BF16), pltpu.HBM((S // TQ, QK_PAD, TQ), BF16)],
        in_specs=[row(D_MODEL), _full((N_MOD, D_MODEL)), _full((1, D_MODEL)), _full((1, Q_LORA)), _full((1, KV_LORA)),
                  _full((N_CHIPS, PROJ_W, D_MODEL // N_CHIPS)), _full((Q_LORA, Q_W)), _full((HEADS, KV_LORA, NOPE)),
                  row(128), row(128)],
        out_specs=[row(PROJ_W), row(Q_W), pl.BlockSpec((HEADS, T, QK_PAD), lambda i: (0, i, 0)), row(QK_PAD),
                   pl.BlockSpec((T // TQ, QK_PAD, TQ), lambda i: (i, 0, 0))],
        compiler_params=_params(("parallel",)),
    )(*_hbm(x, mod6, g_mix, g_q, g_kv, w_in, w_uq, w_uk_t, cos, sin))


def _diag_mask(TQ, width):
    key = lax.broadcasted_iota(jnp.int32, (TQ, width), 0) >> CHUNK_SHIFT
    qry = (lax.broadcasted_iota(jnp.int32, (TQ, width), 1) & (TQ - 1)) >> CHUNK_SHIFT
    return key <= qry


def _col_to_row(col):
    return jnp.transpose(jnp.broadcast_to(col, (col.shape[0], 128)))[0:1, :]


def _attention_fwd(qc, kc, kct, w_uv_t, TQ):
    S = kc.shape[0]
    R = HEADS * TQ
    nq = S // TQ

    def body(q_ref, k_ref, kt_ref, wuv_ref, o_ref, y_ref, lser_ref, m_s, l_s, acc_s, st_s):
        i = pl.program_id(0)
        q = q_ref[...].reshape(R, QK_PAD)
        m_s[...] = jnp.full((1, R), -jnp.inf, F32)
        l_s[...] = jnp.zeros((1, R), F32)
        acc_s[...] = jnp.zeros((KV_LORA, R), F32)

        def scores(j):
            return _dot_nt(k_ref[pl.ds(pl.multiple_of(j * TQ, TQ), TQ), :], q) * SM_SCALE

        def update(j, st):
            m_old = m_s[...]
            m_new = jnp.maximum(m_old, jnp.max(st, axis=0, keepdims=True))
            pt = jnp.exp(st - m_new)
            alpha = jnp.exp(m_old - m_new)
            l_s[...] = alpha * l_s[...] + jnp.sum(pt, axis=0, keepdims=True)
            acc_s[...] = alpha * acc_s[...] + _dot(kt_ref[j, 0:KV_LORA, :], pt)
            m_s[...] = m_new

        st_s[...] = scores(0)

        def loop(j, carry):
            st = st_s[...]
            st_s[...] = scores(j + 1)
            update(j, st)
            return carry

        lax.fori_loop(0, i, loop, 0)
        update(i, jnp.where(_diag_mask(TQ, R), st_s[...], -jnp.inf))
        l = l_s[...]
        lser_ref[0] = m_s[...] + jnp.log(l)
        o = jnp.transpose(acc_s[...] / l).astype(BF16)
        for h in range(HEADS):
            oh = o[h * TQ:(h + 1) * TQ, :]
            o_ref[h] = oh
            y_ref[:, h * 128:(h + 1) * 128] = _dot(oh, wuv_ref[h]).astype(BF16)

    return pl.pallas_call(
        body, name="attention_fwd", grid=(nq,),
        out_shape=[pltpu.HBM((HEADS, S, KV_LORA), BF16), pltpu.HBM((S, HEADS * 128), BF16),
                   pltpu.HBM((nq, 1, R), F32)],
        in_specs=[pl.BlockSpec((HEADS, TQ, QK_PAD), lambda i: (0, i, 0)), _full((S, QK_PAD)),
                  _full((nq, QK_PAD, TQ)), _full((HEADS, KV_LORA, 128))],
        out_specs=[pl.BlockSpec((HEADS, TQ, KV_LORA), lambda i: (0, i, 0)), pl.BlockSpec((TQ, HEADS * 128), lambda i: (i, 0)),
                   pl.BlockSpec((1, 1, R), lambda i: (i, 0, 0))],
        scratch_shapes=[pltpu.VMEM((1, R), F32), pltpu.VMEM((1, R), F32), pltpu.VMEM((KV_LORA, R), F32),
                        pltpu.VMEM((TQ, R), F32)],
        compiler_params=_params(("parallel",)),
    )(*_hbm(qc, kc, kct, w_uv_t))


def _pool_forward(proj):
    S = proj.shape[0]
    RB = _row_tile(S, 256)

    def body(proj_ref, out_ref, pad_ref, sem):
        cp = pltpu.make_async_copy(proj_ref.at[:, pl.ds(O_U, POOL_W)], pad_ref.at[pl.ds(POOL_PAD, S)], sem)
        cp.start()
        pad_ref[0:POOL_PAD, :] = jnp.zeros((POOL_PAD, POOL_W), F32)
        cp.wait()
        for g, win in enumerate(POOL_WINDOWS):
            cols = slice(g * POOL_GROUP, (g + 1) * POOL_GROUP)
            for r0 in range(0, S, RB):
                u = pad_ref[POOL_PAD + r0:POOL_PAD + r0 + RB, cols]
                acc = u
                for k in range(1, win):
                    acc = acc + pad_ref[POOL_PAD + r0 - k:POOL_PAD + r0 - k + RB, cols]
                if r0 == 0:
                    t1 = (lax.broadcasted_iota(jnp.int32, (RB, POOL_GROUP), 0) + 1).astype(F32)
                    mean = acc / jnp.minimum(t1, float(win))
                else:
                    mean = acc * (1.0 / win)
                out_ref[r0:r0 + RB, cols] = (mean - u).astype(BF16)

    return pl.pallas_call(
        body, name="pool_forward",
        out_shape=jax.ShapeDtypeStruct((S, POOL_W), BF16),
        in_specs=[ANY], out_specs=VMEM_SPEC,
        scratch_shapes=[pltpu.VMEM((S + POOL_PAD, POOL_W), F32), pltpu.SemaphoreType.DMA],
        compiler_params=_params(),
    )(proj)


def _pool_backward(dpooled, after):
    S = dpooled.shape[0]
    RB = _row_tile(S, 256)

    def body(dp_ref, after_ref, out_ref, pad_ref, sem):
        cp = pltpu.make_async_copy(dp_ref, pad_ref.at[pl.ds(0, S)], sem)
        cp.start()
        pad_ref[S:S + POOL_PAD, :] = jnp.zeros((POOL_PAD, POOL_W), F32)
        cp.wait()
        for g, win in enumerate(POOL_WINDOWS):
            cols = slice(g * POOL_GROUP, (g + 1) * POOL_GROUP)
            head = pad_ref[0:POOL_PAD, cols]
            t1 = (lax.broadcasted_iota(jnp.int32, (POOL_PAD, POOL_GROUP), 0) + 1).astype(F32)
            pad_ref[0:POOL_PAD, cols] = head * (float(win) / jnp.minimum(t1, float(win)))
            for r0 in range(0, S, RB):
                acc = pad_ref[r0:r0 + RB, cols]
                for k in range(1, win):
                    acc = acc + pad_ref[r0 + k:r0 + k + RB, cols]
                own = pad_ref[r0:r0 + RB, cols]
                if r0 == 0:
                    own = jnp.concatenate([head, own[POOL_PAD:]], axis=0)
                out_ref[r0:r0 + RB, cols] = acc * (1.0 / win) - own

    return pl.pallas_call(
        body, name="pool_backward",
        out_shape=jax.ShapeDtypeStruct((S, POOL_W), F32),
        in_specs=[ANY, ANY], out_specs=VMEM_SPEC,
        scratch_shapes=[pltpu.VMEM((S + POOL_PAD, POOL_W), F32), pltpu.SemaphoreType.DMA],
        compiler_params=_params(),
    )(dpooled, after)


def _mix_out(y_mla, pooled, w_pool, pool_scale, w_o, x, mod6, T):
    S = x.shape[0]

    def body(ym_ref, pl_ref, wp_ref, ps_ref, wo_ref, x_ref, mod_ref, x1_ref, mix_ref, mi_ref):
        mi_ref[:, 0:512] = ym_ref[...]
        for g in range(len(POOL_WINDOWS)):
            cols = slice(g * POOL_GROUP, (g + 1) * POOL_GROUP)
            z = _dot(pl_ref[:, cols], wp_ref[g])
            mi_ref[:, 512 + g * POOL_GROUP:512 + (g + 1) * POOL_GROUP] = (z * ps_ref[:, cols]).astype(BF16)
        mix = _dot(mi_ref[...], wo_ref[...])
        mix_ref[...] = mix
        x1_ref[...] = x_ref[...] + mod_ref[2:3, :] * mix

    row = lambda w: pl.BlockSpec((T, w), lambda i: (i, 0))
    return pl.pallas_call(
        body, name="mix_out", grid=(S // T,),
        out_shape=[pltpu.HBM((S, D_MODEL), F32), pltpu.HBM((S, D_MODEL), F32), pltpu.HBM((S, 1024), BF16)],
        in_specs=[row(512), row(POOL_W), _full((4, POOL_GROUP, POOL_GROUP)), _full((1, POOL_W)),
                  _full((1024, D_MODEL)), row(D_MODEL), _full((N_MOD, D_MODEL))],
        out_specs=[row(D_MODEL), row(D_MODEL), row(1024)],
        compiler_params=_params(("parallel",)),
    )(*_hbm(y_mla, pooled, w_pool, pool_scale, w_o, x, mod6))


def _ffn_forward(x1, mod6, g_ffn, g_final, target, w_gate, w_up, w_down, T):
    S = x1.shape[0]

    def body(x1_ref, mod_ref, gf_ref, gl_ref, tgt_ref, wg_ref, wu_ref, wd_ref,
             gate_ref, up_ref, act_ref, h2_ref, dff_ref, dx2_ref, st_ref, acc_s):
        i, j = pl.program_id(0), pl.program_id(1)

        @pl.when(jnp.logical_and(i == 0, j == 0))
        def _():
            st_ref[...] = jnp.zeros_like(st_ref)

        @pl.when(j == 0)
        def _():
            xh, _ = _rms(x1_ref[...])
            h2_ref[...] = ((xh * gf_ref[...]) * (1.0 + mod_ref[4:5, :]) + mod_ref[3:4, :]).astype(BF16)
            acc_s[...] = jnp.zeros_like(acc_s)

        h2 = h2_ref[...]
        gate = _dot_nt(h2, wg_ref[j])
        up = _dot_nt(h2, wu_ref[j])
        gate_ref[...] = gate.astype(BF16)
        up_ref[...] = up.astype(BF16)
        act = (gate * jax.nn.sigmoid(gate) * up).astype(BF16)
        act_ref[...] = act
        acc_s[...] += _dot(act, wd_ref[j])

        @pl.when(j == N_CHIPS - 1)
        def _():
            ff = acc_s[...]
            x2 = x1_ref[...] + mod_ref[5:6, :] * ff
            xh, r3 = _rms(x2)
            err = xh * gl_ref[...] - tgt_ref[...]
            dy = err * (1.0 / D_MODEL)
            dx2 = _rms_bwd(dy * gl_ref[...], xh, r3)
            dx2_ref[...] = dx2
            dff_ref[...] = (dx2 * mod_ref[5:6, :]).astype(BF16)
            st_ref[0:1, :] += jnp.sum(dy * xh, axis=0, keepdims=True)
            st_ref[1:2, :] += jnp.sum(dx2 * ff, axis=0, keepdims=True)
            st_ref[2:3, :] += 0.5 * jnp.sum(err * dy)

    row = pl.BlockSpec((T, D_MODEL), lambda i, j: (i, 0))
    chunk_out = pl.BlockSpec((None, T, FF_CHUNK), lambda i, j: (j, i, 0))
    big = pltpu.HBM((N_CHIPS, S, FF_CHUNK), BF16)
    wide = pltpu.HBM((S, D_MODEL), BF16)
    return pl.pallas_call(
        body, name="ffn_forward", grid=(S // T, N_CHIPS),
        out_shape=[big, big, big, wide, wide, pltpu.HBM((S, D_MODEL), F32), jax.ShapeDtypeStruct((8, D_MODEL), F32)],
        in_specs=[row, _full((N_MOD, D_MODEL)), _full((1, D_MODEL)), _full((1, D_MODEL)), row,
                  VMEM_SPEC, VMEM_SPEC, VMEM_SPEC],
        out_specs=[chunk_out, chunk_out, chunk_out, row, row, row, _full((8, D_MODEL))],
        scratch_shapes=[pltpu.VMEM((T, D_MODEL), F32)],
        compiler_params=_params(("arbitrary", "arbitrary")),
    )(*_hbm(x1, mod6, g_ffn, g_final, target), w_gate, w_up, w_down)


def _ffn_backward(dx2, x1, dff, gate, up, mod6, g_ffn, w_gate, w_up, w_down, T):
    S = x1.shape[0]

    def body(dx2_ref, x1_ref, dff_ref, gate_ref, up_ref, mod_ref, gf_ref, wg_ref, wu_ref, wd_ref,
             dgate_ref, dup_ref, dx1_ref, st_ref, acc_s):
        i, j = pl.program_id(0), pl.program_id(1)

        @pl.when(jnp.logical_and(i == 0, j == 0))
        def _():
            st_ref[...] = jnp.zeros_like(st_ref)

        @pl.when(j == 0)
        def _():
            acc_s[...] = jnp.zeros_like(acc_s)

        for r0 in range(0, T, T // 2):
            rows = slice(r0, r0 + T // 2)
            gate, up = gate_ref[rows, :].astype(F32), up_ref[rows, :].astype(F32)
            sg = jax.nn.sigmoid(gate)
            dact = _dot_nt(dff_ref[rows, :], wd_ref[j])
            dup = (dact * (gate * sg)).astype(BF16)
            dgate = (dact * up * (sg * (1.0 + gate * (1.0 - sg)))).astype(BF16)
            dup_ref[rows, :] = dup
            dgate_ref[rows, :] = dgate
            acc_s[rows, :] += _dot(dgate, wg_ref[j]) + _dot(dup, wu_ref[j])

        @pl.when(j == N_CHIPS - 1)
        def _():
            dh2 = acc_s[...]
            xh, r2 = _rms(x1_ref[...])
            n2 = xh * gf_ref[...]
            st_ref[0:1, :] += jnp.sum(dh2, axis=0, keepdims=True)
            st_ref[1:2, :] += jnp.sum(dh2 * n2, axis=0, keepdims=True)
            dn2 = dh2 * (1.0 + mod_ref[4:5, :])
            st_ref[2:3, :] += jnp.sum(dn2 * xh, axis=0, keepdims=True)
            dx1_ref[...] = _rms_bwd(dn2 * gf_ref[...], xh, r2) + dx2_ref[...]

    row = pl.BlockSpec((T, D_MODEL), lambda i, j: (i, 0))
    chunk = pl.BlockSpec((None, T, FF_CHUNK), lambda i, j: (j, i, 0))
    big = pltpu.HBM((N_CHIPS, S, FF_CHUNK), BF16)
    return pl.pallas_call(
        body, name="ffn_backward", grid=(S // T, N_CHIPS),
        out_shape=[big, big, pltpu.HBM((S, D_MODEL), F32), jax.ShapeDtypeStruct((8, D_MODEL), F32)],
        in_specs=[row, row, row, chunk, chunk, _full((N_MOD, D_MODEL)), _full((1, D_MODEL)),
                  VMEM_SPEC, VMEM_SPEC, VMEM_SPEC],
        out_specs=[chunk, chunk, row, _full((8, D_MODEL))],
        scratch_shapes=[pltpu.VMEM((T, D_MODEL), F32)],
        compiler_params=_params(("arbitrary", "arbitrary")),
    )(*_hbm(dx2, x1, dff, gate, up, mod6, g_ffn), w_gate, w_up, w_down)


def _tn_matmul(a, b, a_spec, b_spec, groups, m, n, steps, name):
    def body(a_ref, b_ref, o_ref):
        @pl.when(pl.program_id(1) == 0)
        def _():
            o_ref[...] = jnp.zeros_like(o_ref)

        o_ref[...] += _dot_tn(a_ref[...], b_ref[...])

    return pl.pallas_call(
        body, name=name, grid=(groups, steps),
        out_shape=pltpu.HBM((groups, m, n), F32),
        in_specs=[a_spec, b_spec],
        out_specs=pl.BlockSpec((None, m, n), lambda g, i: (g, 0, 0)),
        compiler_params=_params(("parallel", "arbitrary")),
    )(*_hbm(a, b))


def _mix_backward(dx1, mix, mod6, w_o, pooled, w_pool, pool_scale, w_uv_t, o_lat, T, TQ):
    S = dx1.shape[0]

    def body(dx1_ref, mix_ref, mod_ref, wo_ref, pl_ref, wp_ref, ps_ref, wuv_ref, o_ref,
             dmix_ref, dp_ref, do_ref, dr_ref, gp_ref, guv_ref, st_ref):
        @pl.when(pl.program_id(0) == 0)
        def _():
            st_ref[...] = jnp.zeros_like(st_ref)
            gp_ref[...] = jnp.zeros_like(gp_ref)
            guv_ref[...] = jnp.zeros_like(guv_ref)

        dx1 = dx1_ref[...]
        st_ref[0:1, :] += jnp.sum(dx1 * mix_ref[...], axis=0, keepdims=True)
        dmix = (dx1 * mod_ref[2:3, :]).astype(BF16)
        dmix_ref[...] = dmix
        dmi = _dot_nt(dmix, wo_ref[...])
        dym = dmi[:, 0:512].astype(BF16)
        for g in range(len(POOL_WINDOWS)):
            cols = slice(g * POOL_GROUP, (g + 1) * POOL_GROUP)
            dyp = dmi[:, 512 + g * POOL_GROUP:512 + (g + 1) * POOL_GROUP]
            pooled_g = pl_ref[:, cols]
            z = _dot(pooled_g, wp_ref[g])
            st_ref[1:2, cols] += jnp.sum(dyp * z, axis=0, keepdims=True)
            dz = (dyp * ps_ref[:, cols]).astype(BF16)
            gp_ref[g] += _dot_tn(pooled_g, dz)
            dp_ref[:, cols] = _dot_nt(dz, wp_ref[g])
        for h in range(HEADS):
            dym_h = dym[:, h * 128:(h + 1) * 128]
            do = _dot_nt(dym_h, wuv_ref[h]).astype(BF16)
            do_ref[h] = do
            o_h = o_ref[h]
            guv_ref[h] += _dot_tn(o_h, dym_h)
            delta = _col_to_row(jnp.sum(do.astype(F32) * o_h.astype(F32), axis=1, keepdims=True))
            for s in range(T // TQ):
                dr_ref[s, :, h * TQ:(h + 1) * TQ] = delta[:, s * TQ:(s + 1) * TQ]

    row = lambda w: pl.BlockSpec((T, w), lambda i: (i, 0))
    heads = pl.BlockSpec((HEADS, T, KV_LORA), lambda i: (0, i, 0))
    square = jax.ShapeDtypeStruct((4, 128, 128), F32)
    return pl.pallas_call(
        body, name="mix_backward", grid=(S // T,),
        out_shape=[pltpu.HBM((S, D_MODEL), BF16), pltpu.HBM((S, POOL_W), F32), pltpu.HBM((HEADS, S, KV_LORA), BF16),
                   pltpu.HBM((S // TQ, 1, HEADS * TQ), F32), square, square, jax.ShapeDtypeStruct((8, D_MODEL), F32)],
        in_specs=[row(D_MODEL), row(D_MODEL), _full((N_MOD, D_MODEL)), _full((1024, D_MODEL)), row(POOL_W),
                  _full((4, POOL_GROUP, POOL_GROUP)), _full((1, POOL_W)), _full((HEADS, KV_LORA, 128)), heads],
        out_specs=[row(D_MODEL), row(POOL_W), heads,
                   pl.BlockSpec((T // TQ, 1, HEADS * TQ), lambda i: (i, 0, 0)), _full((4, 128, 128)),
                   _full((4, 128, 128)), _full((8, D_MODEL))],
        compiler_params=_params(("arbitrary",)),
    )(*_hbm(dx1, mix, mod6, w_o, pooled, w_pool, pool_scale, w_uv_t, o_lat))


def _attention_bwd(qc, kc, kct, do, lse_rows, delta_rows, TQ):
    S = kc.shape[0]
    R = HEADS * TQ
    nq = S // TQ

    def body(k_ref, kt_ref, q_ref, do_ref, lser_ref, dr_ref, dk_ref, dqt_ref, dk_s, dv_s):
        j = pl.program_id(0)

        @pl.when(j == 0)
        def _():
            def zero(i, carry):
                dqt_ref[i] = jnp.zeros((QK_PAD, R), F32)
                return carry
            lax.fori_loop(0, nq, zero, 0)

        k = k_ref[...]
        kt = kt_ref[...]
        v = k[:, :KV_LORA]
        dk_s[...] = jnp.zeros((TQ, QK_PAD), F32)
        dv_s[...] = jnp.zeros((TQ, KV_LORA), F32)

        def step(i, masked):
            rows = pl.ds(pl.multiple_of(i * TQ, TQ), TQ)
            q = q_ref[:, rows, :].reshape(R, QK_PAD)
            do = do_ref[:, rows, :].reshape(R, KV_LORA)
            st = _dot_nt(k, q) * SM_SCALE
            if masked:
                st = jnp.where(_diag_mask(TQ, R), st, -jnp.inf)
            pt = jnp.exp(st - lser_ref[i])
            dv_s[...] += _dot(pt, do)
            dpt = _dot_nt(v, do)
            dst = (pt * (dpt - dr_ref[i])).astype(BF16)
            dk_s[...] += _dot(dst, q)
            dqt_ref[i] += _dot(kt, dst)

        def loop(i, carry):
            step(i, False)
            return carry

        step(j, True)
        lax.fori_loop(j + 1, nq, loop, 0)
        dk = dk_s[...] * SM_SCALE
        dk_ref[:, 0:KV_LORA] = dk[:, 0:KV_LORA] + dv_s[...]
        dk_ref[:, KV_LORA:QK_PAD] = dk[:, KV_LORA:QK_PAD]

    return pl.pallas_call(
        body, name="attention_bwd", grid=(nq,),
        out_shape=[pltpu.HBM((S, QK_PAD), F32), jax.ShapeDtypeStruct((nq, QK_PAD, R), F32)],
        in_specs=[pl.BlockSpec((TQ, QK_PAD), lambda j: (j, 0)), pl.BlockSpec((None, QK_PAD, TQ), lambda j: (j, 0, 0)),
                  VMEM_SPEC, VMEM_SPEC, VMEM_SPEC, VMEM_SPEC],
        out_specs=[pl.BlockSpec((TQ, QK_PAD), lambda j: (j, 0)), VMEM_SPEC],
        scratch_shapes=[pltpu.VMEM((TQ, QK_PAD), F32), pltpu.VMEM((TQ, KV_LORA), F32)],
        compiler_params=_params(("arbitrary",)),
    )(*_hbm(kc, kct), qc, do, lse_rows, delta_rows)


def _pre_attention_backward(x, dx1, proj, q, dqt, dkc, du, cos, sin, mod6, g_mix, g_q, g_kv, w_in, w_uq, w_uk_t, T, TQ):
    S = x.shape[0]

    def body(x_ref, dx1_ref, proj_ref, q_ref, dqt_ref, dkc_ref, du_ref, cos_ref, sin_ref, mod_ref, gm_ref, gq_ref,
             gkv_ref, win_ref, wuq_ref, wuk_ref, gx_ref, dproj_ref, h1_ref, guk_ref, guq_ref, st_ref, dq_ref):
        @pl.when(pl.program_id(0) == 0)
        def _():
            st_ref[...] = jnp.zeros_like(st_ref)
            guk_ref[...] = jnp.zeros_like(guk_ref)
            guq_ref[...] = jnp.zeros_like(guq_ref)

        cos_t, sin_t = cos_ref[...], sin_ref[...]
        low = lax.broadcasted_iota(jnp.int32, (T, 128), 1) < ROPE
        rope_parts = []
        for h in range(HEADS):
            dqc = jnp.concatenate([jnp.transpose(dqt_ref[s, :, h * TQ:(h + 1) * TQ]) for s in range(T // TQ)], axis=0)
            dqc = dqc * SM_SCALE
            dql = dqc[:, 0:KV_LORA].astype(BF16)
            guk_ref[h] += _dot_tn(dql, q_ref[:, h * NOPE:(h + 1) * NOPE])
            dq_ref[:, h * NOPE:(h + 1) * NOPE] = _dot(dql, wuk_ref[h]).astype(BF16)
            rope_parts.append(dqc[:, KV_LORA:QK_PAD])
        for pair in range(2):
            d = jnp.where(low, rope_parts[2 * pair], rope_parts[2 * pair + 1])
            dq_ref[:, O_QA + 128 * pair:O_QA + 128 * (pair + 1)] = _rope_bwd(d, cos_t, sin_t).astype(BF16)
        dq = dq_ref[...]
        dcq = _dot_nt(dq, wuq_ref[...])
        cqh, rq = _rms(proj_ref[:, 0:Q_LORA])
        guq_ref[...] += _dot_tn(cqh * gq_ref[...], dq)
        st_ref[3:4, 0:Q_LORA] += jnp.sum(dcq * cqh, axis=0, keepdims=True)
        dproj_ref[:, 0:Q_LORA] = _rms_bwd(dcq * gq_ref[...], cqh, rq).astype(BF16)
        dckv = dkc_ref[:, 0:KV_LORA]
        ckvh, rkv = _rms(proj_ref[:, O_CKV:O_KR])
        st_ref[4:5, 0:KV_LORA] += jnp.sum(dckv * ckvh, axis=0, keepdims=True)
        dproj_ref[:, O_CKV:O_KR] = _rms_bwd(dckv * gkv_ref[...], ckvh, rkv).astype(BF16)
        dkr = _rope_bwd(dkc_ref[:, KV_LORA:QK_PAD], cos_t, sin_t)
        dkr = jnp.where(low, dkr + pltpu.roll(dkr, ROPE, 1), 0.0)
        dproj_ref[:, O_KR:O_U] = dkr.astype(BF16)
        dproj_ref[:, O_U:PROJ_W] = du_ref[...].astype(BF16)
        dproj = dproj_ref[...]
        dh1 = jnp.concatenate([_dot(dproj, win_ref[j]) for j in range(N_CHIPS)], axis=1)
        xh, r1 = _rms(x_ref[...])
        n1 = xh * gm_ref[...]
        h1_ref[...] = (n1 * (1.0 + mod_ref[1:2, :]) + mod_ref[0:1, :]).astype(BF16)
        st_ref[0:1, :] += jnp.sum(dh1, axis=0, keepdims=True)
        st_ref[1:2, :] += jnp.sum(dh1 * n1, axis=0, keepdims=True)
        dn1 = dh1 * (1.0 + mod_ref[1:2, :])
        st_ref[2:3, :] += jnp.sum(dn1 * xh, axis=0, keepdims=True)
        gx_ref[...] = _rms_bwd(dn1 * gm_ref[...], xh, r1) + dx1_ref[...]

    row = lambda w: pl.BlockSpec((T, w), lambda i: (i, 0))
    return pl.pallas_call(
        body, name="pre_attention_backward", grid=(S // T,),
        out_shape=[jax.ShapeDtypeStruct((S, D_MODEL), F32), pltpu.HBM((S, PROJ_W), BF16),
                   pltpu.HBM((S, D_MODEL), BF16), jax.ShapeDtypeStruct((HEADS, KV_LORA, NOPE), F32),
                   jax.ShapeDtypeStruct((Q_LORA, Q_W), F32), jax.ShapeDtypeStruct((8, D_MODEL), F32)],
        in_specs=[row(D_MODEL), row(D_MODEL), row(PROJ_W), row(HEADS * NOPE),
                  pl.BlockSpec((T // TQ, QK_PAD, HEADS * TQ), lambda i: (i, 0, 0)),
                  row(QK_PAD), row(POOL_W), row(128), row(128), _full((N_MOD, D_MODEL)), _full((1, D_MODEL)),
                  _full((1, Q_LORA)), _full((1, KV_LORA)), _full((N_CHIPS, PROJ_W, D_MODEL // N_CHIPS)),
                  _full((Q_LORA, Q_W)), _full((HEADS, KV_LORA, NOPE))],
        out_specs=[row(D_MODEL), row(PROJ_W), row(D_MODEL), _full((HEADS, KV_LORA, NOPE)), _full((Q_LORA, Q_W)),
                   _full((8, D_MODEL))],
        scratch_shapes=[pltpu.VMEM((T, Q_W), BF16)],
        compiler_params=_params(("arbitrary",)),
    )(*_hbm(x, dx1, proj, q, dqt, dkc, du, cos, sin, mod6, g_mix, g_q, g_kv, w_in, w_uq, w_uk_t))


def _ada_grads(c_all, dmod_all, chip):
    cols = N_MOD * D_MODEL // N_CHIPS
    width = dmod_all.shape[1]

    def body(col_ref, c_ref, dcol_ref, dall_ref, gw_ref, gb_ref):
        call = c_ref[...]
        act = call * jax.nn.sigmoid(call)
        gw_ref[...] = _dot_tn(act, dcol_ref[...])
        d = dall_ref[...]
        acc = d[0:1, :]
        for b in range(1, 8):
            acc = acc + d[b:b + 1, :]
        gb_ref[...] = acc

    return pl.pallas_call(
        body, name="ada_grads",
        out_shape=[jax.ShapeDtypeStruct((D_MODEL, cols), F32), jax.ShapeDtypeStruct((1, width), F32)],
        grid_spec=pltpu.PrefetchScalarGridSpec(
            num_scalar_prefetch=1, grid=(1,),
            in_specs=[pl.BlockSpec((8, D_MODEL), lambda s, col_ref: (0, 0)),
                      pl.BlockSpec((8, cols), lambda s, col_ref: (0, col_ref[0])),
                      pl.BlockSpec((8, width), lambda s, col_ref: (0, 0))],
            out_specs=[pl.BlockSpec((D_MODEL, cols), lambda s, col_ref: (0, 0)),
                       pl.BlockSpec((1, width), lambda s, col_ref: (0, 0))]),
        compiler_params=_params(("arbitrary",)),
    )(chip, *_hbm(c_all, dmod_all, dmod_all))


def _adamw(w, g, m, v, name):
    rows, rest = w.shape[0], w.shape[1:]
    T = _row_tile(rows, 256)

    def body(w_ref, g_ref, m_ref, v_ref, go_ref, d_ref, nm_ref, nv_ref):
        g = g_ref[...]
        go_ref[...] = g
        m2 = ADAM_B1 * m_ref[...] + (1.0 - ADAM_B1) * g
        v2 = ADAM_B2 * v_ref[...] + (1.0 - ADAM_B2) * (g * g)
        m_hat = m2 / (1.0 - ADAM_B1 ** ADAM_STEP)
        v_hat = v2 / (1.0 - ADAM_B2 ** ADAM_STEP)
        d_ref[...] = -ADAM_LR * (m_hat / (jnp.sqrt(v_hat) + ADAM_EPS) + ADAM_WD * w_ref[...])
        nm_ref[...] = m2
        nv_ref[...] = v2

    zeros = (0,) * len(rest)
    spec = pl.BlockSpec((T,) + rest, lambda i: (i,) + zeros)
    return pl.pallas_call(
        body, name=name, grid=(rows // T,),
        out_shape=[jax.ShapeDtypeStruct(w.shape, F32)] * 4,
        in_specs=[spec] * 4, out_specs=[spec] * 4,
        compiler_params=_params(("parallel",)),
    )(*_hbm(w, g, m, v))


SMALL_NAMES = ("w_uk", "w_uv", "w_pool", "g_mix", "g_q", "g_kv", "pool_scale", "g_ffn", "g_final", "b_ada")
SMALL_ROWS = 1664


def _pack_rows(parts):
    flat = jnp.concatenate([p.reshape(-1) for p in parts])
    pad = (-flat.shape[0]) % 128
    if pad:
        flat = jnp.concatenate([flat, jnp.zeros((pad,), F32)])
    return flat.reshape(-1, 128)


def kernel(x, c, positions, w_ada, b_ada, g_mix, w_in, g_q, g_kv, w_uq, w_uk, w_uv, w_pool, pool_scale, w_o, g_ffn, w_gate, w_up, w_down, g_final, loss_target, m_w_ada, m_b_ada, m_g_mix, m_w_in, m_g_q, m_g_kv, m_w_uq, m_w_uk, m_w_uv, m_w_pool, m_pool_scale, m_w_o, m_g_ffn, m_w_gate, m_w_up, m_w_down, m_g_final, v_w_ada, v_b_ada, v_g_mix, v_w_in, v_g_q, v_g_kv, v_w_uq, v_w_uk, v_w_uv, v_w_pool, v_pool_scale, v_w_o, v_g_ffn, v_w_gate, v_w_up, v_w_down, v_g_final):
    S = x.shape[1]
    T = _row_tile(S, 512)
    TQ = _row_tile(S, 256)
    TW = _row_tile(S, 1024)
    ix, iy, ic = lax.axis_index("x"), lax.axis_index("y"), lax.axis_index("c")
    chip = (2 * ix + iy).astype(jnp.int32)
    chip_arr = chip.reshape(1)
    core_arr = ic.astype(jnp.int32).reshape(1)

    xs, tgt = x[0], loss_target[0]

    tr = lambda a: jnp.transpose(a[0])
    win_t = tr(w_in)
    win_p = jnp.concatenate([win_t[:O_KR + ROPE], win_t[O_KR:O_KR + ROPE], win_t[O_KR + ROPE:]], axis=0).astype(BF16)
    wuq = w_uq[0]
    wuq_p = jnp.concatenate([wuq[:, h, :NOPE] for h in range(HEADS)] + [wuq[:, h, NOPE:] for h in range(HEADS)],
                            axis=1).astype(BF16)
    w_uk_t = jnp.transpose(w_uk[0], (1, 0, 2)).astype(BF16)
    w_uv_t = jnp.transpose(w_uv[0], (1, 0, 2)).astype(BF16)
    w_pool_b = w_pool[0].astype(BF16)
    first = [win_p, wuq_p]
    later = [w_o[0].astype(BF16), tr(w_gate).astype(BF16), tr(w_up).astype(BF16), w_down[0].astype(BF16)]
    placed = _place_shards(chip_arr, first + later)
    a_send, a_recv, a_lands, token = _split_start("first_weights_start", first, placed[:2], 6, _plan_gather_start)
    half = ROPE // 2
    freqs = jnp.power(ROPE_THETA, -jnp.arange(half, dtype=F32) / half)
    cos, sin = _rope_tables(positions.reshape(S, 1), jnp.tile(freqs, 4).reshape(1, 128) + token[0, 0])
    a_send, a_recv, a_lands, token = _split_relay(
        "first_weights_relay", a_send, a_recv, first, a_lands, cos, 6, _plan_gather_landed, _plan_gather_relay)

    ada_cols = w_ada.shape[2]
    b_cols = lax.dynamic_slice(b_ada, (0, chip * ada_cols), (1, ada_cols))
    mod, c_all = _mod_exchange(c, w_ada[0], b_cols + token[0, 0])
    mod6 = mod.reshape(N_MOD, D_MODEL)
    a_lands = _split_wait("first_weights_wait", a_send, a_recv, [], a_lands, mod, _plan_gather_wait)
    w_in_f = a_lands[0]
    w_uq_f = a_lands[1].reshape(Q_LORA, Q_W)
    wg_lands, mod6, w_in_f = lax.optimization_barrier((placed[2:], mod6, w_in_f))
    wg_send, wg_recv, wg_lands, token = _split_start(
        "weights_start", later, wg_lands, 3 * len(later), _plan_gather_start)
    mod6 = mod6 + token[0, 0]

    proj, q, qc, kc, kct = _pre_attention(xs, mod6, g_mix, g_q, g_kv, w_in_f, w_uq_f, w_uk_t, cos, sin, T, TQ)
    o_lat, y_mla, lse_rows = _attention_fwd(qc, kc, kct, w_uv_t, TQ)
    wg_send, wg_recv, wg_lands, token = _split_relay(
        "weights_relay", wg_send, wg_recv, later, wg_lands, y_mla, 3 * len(later), _plan_gather_landed,
        _plan_gather_relay)
    pooled = _pool_forward(proj)
    wg_lands = _split_wait("weights_wait", wg_send, wg_recv, [], wg_lands, pooled, _plan_gather_wait)
    w_o_f = wg_lands[0].reshape(1024, D_MODEL)
    w_gate_f, w_up_f, w_down_f = wg_lands[1], wg_lands[2], wg_lands[3]
    x1, mix, mix_in = _mix_out(y_mla, pooled, w_pool_b, pool_scale, w_o_f, xs, mod6, T)
    gate, up, act, h2, dff, dx2, st_f = _ffn_forward(
        x1, mod6, g_ffn, g_final.reshape(1, D_MODEL), tgt, w_gate_f, w_up_f, w_down_f, T)

    dgate, dup, dx1, st_b = _ffn_backward(dx2, x1, dff, gate, up, mod6, g_ffn, w_gate_f, w_up_f, w_down_f, T)
    steps = S // TW
    chunk_spec = pl.BlockSpec((None, TW, FF_CHUNK), lambda g, i: (g, i, 0))
    wide_spec = pl.BlockSpec((TW, D_MODEL), lambda g, i: (i, 0))
    g_down = _tn_matmul(act, dff, chunk_spec, wide_spec, N_CHIPS, FF_CHUNK, D_MODEL, steps, "grad_w_down")
    g_gate = _tn_matmul(dgate, h2, chunk_spec, wide_spec, N_CHIPS, FF_CHUNK, D_MODEL, steps, "grad_w_gate")
    g_up = _tn_matmul(dup, h2, chunk_spec, wide_spec, N_CHIPS, FF_CHUNK, D_MODEL, steps, "grad_w_up")

    half_shapes = lambda gs: [jax.ShapeDtypeStruct((N_CHIPS, g.shape[1] // 2, g.shape[2]), F32) for g in gs]
    ffn_grads = [g_gate, g_up, g_down]
    f_send, f_recv, f_lands, token = _split_start(
        "ffn_swap_start", ffn_grads, half_shapes(ffn_grads), len(ffn_grads), _plan_swap_start)
    dmix, dpooled, do_lat, delta_rows, g_pool, g_uv_t, st_m = _mix_backward(
        dx1, mix, mod6 + token[0, 0], w_o_f, pooled, w_pool_b, pool_scale, w_uv_t, o_lat, T, TQ)
    g_o = [_tn_matmul(mix_in, dmix, wide_spec, wide_spec, 1, 1024, D_MODEL, steps, "grad_w_o").reshape(N_CHIPS, -1, D_MODEL)]
    o_send, o_recv, o_lands, token = _split_start("w_o_swap_start", g_o, half_shapes(g_o), 1, _plan_swap_start)
    du = _pool_backward(dpooled, token)
    f_got = _split_wait("ffn_swap_wait", f_send, f_recv, ffn_grads, f_lands, du, _plan_swap_wait)
    f_got += _split_wait("w_o_swap_wait", o_send, o_recv, g_o, o_lands, du, _plan_swap_wait)
    far_names = ("w_gate", "w_up", "w_down", "w_o")
    far_grads = ffn_grads + g_o
    f_sums = [_add_my_half(core_arr, a, b, "add_half_" + n) for a, b, n in zip(far_grads, f_got, far_names)]
    f_send, f_recv, f_lands, token = _split_start(
        "far_exchange_start", f_sums, [jax.ShapeDtypeStruct((3,) + s.shape[1:], F32) for s in f_sums],
        3 * len(f_sums), _plan_exchange_start)
    delta_rows = delta_rows + token[0, 0]
    dkc, dqt = _attention_bwd(qc, kc, kct, do_lat, lse_rows, delta_rows, TQ)
    grad_x, dproj, h1, g_uk_t, uq, st_p = _pre_attention_backward(
        xs, dx1, proj, q, dqt, dkc, du, cos, sin, mod6, g_mix, g_q, g_kv, w_in_f, w_uq_f, w_uk_t, T, TQ)
    rows_in = D_MODEL // N_CHIPS
    g_in_p = _tn_matmul(dproj, h1, pl.BlockSpec((TW, PROJ_W), lambda g, i: (i, 0)),
                        pl.BlockSpec((TW, rows_in), lambda g, i: (i, g)), N_CHIPS, PROJ_W, rows_in, steps, "grad_w_in")

    g_in = jnp.concatenate([g_in_p[:, :O_KR + ROPE], g_in_p[:, O_U:]], axis=1)
    g_uq = jnp.concatenate([jnp.concatenate([uq[:, h * NOPE:(h + 1) * NOPE], uq[:, O_QA + h * ROPE:O_QA + (h + 1) * ROPE]],
                                            axis=1) for h in range(HEADS)], axis=1).reshape(N_CHIPS, -1, HEADS * HEAD_QK)
    small = _pack_rows([g_uk_t, g_uv_t, g_pool, st_p[2], st_p[3, :Q_LORA], st_p[4, :KV_LORA], st_m[1, :POOL_W],
                        st_b[2], st_f[0]])
    small = jnp.concatenate([small, jnp.zeros((SMALL_ROWS - small.shape[0], 128), F32)]).reshape(N_CHIPS, -1, 128)
    grads = [g_in, g_uq, small]
    dmod = jnp.concatenate([jnp.stack([st_p[0], st_p[1], st_m[0], st_b[0], st_b[1], st_f[1]]).reshape(48, 128),
                            jnp.zeros((8, 128), F32).at[0, 0].set(st_f[2, 0])])

    f_others = _split_wait("far_exchange_wait", f_send, f_recv, f_sums, f_lands, g_in_p, _plan_exchange_wait)
    chip_core = jnp.concatenate([chip_arr, core_arr])
    f_pairs = [_add_chips_into_pair(chip_core, a, b, "add_chips_" + n) for a, b, n in zip(f_sums, f_others, far_names)]
    f_send, f_recv, f_pairs, token = _split_start("far_finish_start", [], f_pairs, len(f_pairs), _plan_finish_start)
    dmod = dmod + token[0, 0]

    names = ("w_in", "w_uq", "small")
    got, dmod_all = _grad_swap_halves(grads, dmod)
    chip_sums = [_add_my_half(core_arr, a, b, "add_half_" + n) for a, b, n in zip(grads, got, names)]
    n_send, n_recv, n_lands, token = _split_start(
        "near_exchange_start", chip_sums, [jax.ShapeDtypeStruct((3,) + s.shape[1:], F32) for s in chip_sums],
        3 * len(chip_sums), _plan_exchange_start)
    f_fulls = _split_wait("far_finish_wait", f_send, f_recv, [], f_pairs, token, _plan_finish_wait)
    gw_gate, gw_up, gw_down, gw_o = [f.reshape(-1, f.shape[2]) for f in f_fulls]
    gw_ada, gb_ada = _ada_grads(c_all, dmod_all.reshape(8, -1), chip_arr)
    loss = gb_ada[0, N_MOD * D_MODEL]
    gb_ada = gb_ada[:, :N_MOD * D_MODEL]

    untr = lambda a: jnp.transpose(a)[None]
    grad_out, delta_out, newm_out, newv_out = {}, {}, {}, {}

    def adam_sharded(n, w, g2, m, v, transposed):
        view = (lambda a: jnp.transpose(a[0])) if transposed else (lambda a: a[0])
        back = untr if transposed else (lambda a: a[None])
        g_, d_, m_, v_ = _adamw(view(w), g2.reshape(view(w).shape), view(m), view(v), "adamw_" + n)
        grad_out[n], delta_out[n], newm_out[n], newv_out[n] = back(g_), back(d_), back(m_), back(v_)
        return d_

    done = [adam_sharded("w_gate", w_gate, gw_gate, m_w_gate, v_w_gate, True),
            adam_sharded("w_up", w_up, gw_up, m_w_up, v_w_up, True),
            adam_sharded("w_down", w_down, gw_down, m_w_down, v_w_down, False),
            adam_sharded("w_o", w_o, gw_o, m_w_o, v_w_o, False),
            adam_sharded("w_ada", w_ada, gw_ada, m_w_ada, v_w_ada, False)]
    after_all = jnp.stack([d.reshape(-1)[0] for d in done])

    others = _split_wait("near_exchange_wait", n_send, n_recv, chip_sums, n_lands, after_all, _plan_exchange_wait)
    halves = [_add_chips(chip_arr, a, b, "add_chips_" + n) for a, b, n in zip(chip_sums, others, names)]
    fulls, small_all = _grad_finish(halves[:2], halves[2])
    gw_in, gw_uq = [f.reshape(-1, f.shape[2]) for f in fulls]
    small_all = small_all.reshape(SMALL_ROWS * 128)
    adam_sharded("w_in", w_in, gw_in, m_w_in, v_w_in, True)
    adam_sharded("w_uq", w_uq, gw_uq, m_w_uq, v_w_uq, False)

    n_sq = KV_LORA * HEADS * 128
    sizes = [n_sq, n_sq, n_sq, D_MODEL, Q_LORA, KV_LORA, POOL_W, D_MODEL, D_MODEL]
    offs = [0]
    for s_ in sizes:
        offs.append(offs[-1] + s_)
    piece = lambda k: small_all[offs[k]:offs[k + 1]]
    grads_small = {
        "w_uk": jnp.transpose(piece(0).reshape(HEADS, KV_LORA, NOPE), (1, 0, 2)),
        "w_uv": jnp.transpose(piece(1).reshape(HEADS, KV_LORA, 128), (1, 0, 2)),
        "w_pool": piece(2).reshape(4, POOL_GROUP, POOL_GROUP),
        "g_mix": piece(3), "g_q": piece(4), "g_kv": piece(5), "pool_scale": piece(6), "g_ffn": piece(7),
        "g_final": piece(8), "b_ada": gb_ada.reshape(-1),
    }
    weights_small = {"w_uk": w_uk, "w_uv": w_uv, "w_pool": w_pool, "g_mix": g_mix, "g_q": g_q, "g_kv": g_kv,
                     "pool_scale": pool_scale, "g_ffn": g_ffn, "g_final": g_final, "b_ada": b_ada}
    m_small = {"w_uk": m_w_uk, "w_uv": m_w_uv, "w_pool": m_w_pool, "g_mix": m_g_mix, "g_q": m_g_q, "g_kv": m_g_kv,
               "pool_scale": m_pool_scale, "g_ffn": m_g_ffn, "g_final": m_g_final, "b_ada": m_b_ada}
    v_small = {"w_uk": v_w_uk, "w_uv": v_w_uv, "w_pool": v_w_pool, "g_mix": v_g_mix, "g_q": v_g_q, "g_kv": v_g_kv,
               "pool_scale": v_pool_scale, "g_ffn": v_g_ffn, "g_final": v_g_final, "b_ada": v_b_ada}
    pack = lambda d: _pack_rows([d[n] for n in SMALL_NAMES])
    _, d_s, m_s, v_s = _adamw(pack(weights_small), pack(grads_small), pack(m_small), pack(v_small), "adamw_small")

    def unpack(flat2d):
        flat = flat2d.reshape(-1)
        out, o = {}, 0
        for n in SMALL_NAMES:
            size = weights_small[n].size
            out[n] = flat[o:o + size].reshape(weights_small[n].shape)
            o += size
        return out

    delta_s, newm_s, newv_s = unpack(d_s), unpack(m_s), unpack(v_s)

    for n in SMALL_NAMES:
        grad_out[n] = grads_small[n].reshape(weights_small[n].shape)
        delta_out[n], newm_out[n], newv_out[n] = delta_s[n], newm_s[n], newv_s[n]

    order = ("w_ada", "b_ada", "g_mix", "w_in", "g_q", "g_kv", "w_uq", "w_uk", "w_uv", "w_pool", "pool_scale", "w_o",
             "g_ffn", "w_gate", "w_up", "w_down", "g_final")
    return (loss, grad_x.reshape(x.shape), *[grad_out[n] for n in order], *[delta_out[n] for n in order],
            *[newm_out[n] for n in order], *[newv_out[n] for n in order])
```

```python
import functools

import jax
import jax.numpy as jnp
from jax import lax
from jax.experimental import pallas as pl
from jax.experimental.pallas import tpu as pltpu

F32 = jnp.float32
BF16 = jnp.bfloat16

D_MODEL = 1024
HEADS = 4
NOPE = 128
ROPE = 64
HEAD_QK = NOPE + ROPE
Q_LORA = 256
KV_LORA = 128
POOL_W = 512
POOL_WINDOWS = (2, 4, 8, 16)
POOL_GROUP = 128
POOL_PAD = 16
D_FF = 2816
N_CHIPS = 4
FF_CHUNK = D_FF // N_CHIPS
N_MOD = 6
EPS = 1e-6
SM_SCALE = HEAD_QK ** -0.5
ROPE_THETA = 10000.0
QK_PAD = 256
CHUNK = 64
CHUNK_SHIFT = 6

ADAM_LR = 0.001
ADAM_B1 = 0.9
ADAM_B2 = 0.999
ADAM_EPS = 1e-08
ADAM_WD = 0.01
ADAM_STEP = 10

VMEM_LIMIT = 48 * 1024 * 1024
MESH = pl.DeviceIdType.MESH
ANY = pl.BlockSpec(memory_space=pl.ANY)
VMEM_SPEC = pl.BlockSpec(memory_space=pltpu.VMEM)

PROJ_W = 1024
O_CKV = 256
O_KR = 384
O_U = 512
Q_W = 768
O_QA = 512
O_QB = 640


def _params(sem=None, vmem=VMEM_LIMIT):
    kw = dict(vmem_limit_bytes=vmem)
    if sem is not None:
        kw["dimension_semantics"] = sem
    return pltpu.CompilerParams(**kw)


def _dot(a, b):
    return jnp.dot(a.astype(BF16), b.astype(BF16), preferred_element_type=F32)


def _dot_nt(a, b):
    return lax.dot_general(a.astype(BF16), b.astype(BF16), (((1,), (1,)), ((), ())), preferred_element_type=F32)


def _dot_tn(a, b):
    return lax.dot_general(a.astype(BF16), b.astype(BF16), (((0,), (0,)), ((), ())), preferred_element_type=F32)


def _row_tile(rows, target):
    best = rows
    for t in range(8, min(rows, target) + 1, 8):
        if rows % t == 0:
            best = t
    return best if rows % best == 0 and best <= target else rows


def _rms(x):
    r = lax.rsqrt(jnp.mean(x * x, axis=-1, keepdims=True) + EPS)
    return x * r, r


def _rms_bwd(dxh, xh, r):
    return r * (dxh - xh * jnp.mean(dxh * xh, axis=-1, keepdims=True))


def _lane_first_half(shape):
    lane = lax.broadcasted_iota(jnp.int32, shape, 1)
    return (lane & (ROPE - 1)) < (ROPE // 2)


def _rope(a, cos, sin):
    first = _lane_first_half(a.shape)
    up = pltpu.roll(a, 96, 1)
    dn = pltpu.roll(a, 32, 1)
    return a * cos + jnp.where(first, -up, dn) * sin


def _rope_bwd(d, cos, sin):
    first = _lane_first_half(d.shape)
    up = pltpu.roll(d, 96, 1)
    dn = pltpu.roll(d, 32, 1)
    return d * cos + jnp.where(first, up, -dn) * sin


RELATIONS = tuple((dx, dy, dc) for dx in (0, 1) for dy in (0, 1) for dc in (0, 1) if (dx, dy, dc) != (0, 0, 0))
CHIP_RELATIONS = ((1, 0), (0, 1), (1, 1))


def _flip(v, d):
    return 1 - v if d else v


def _place():
    return lax.axis_index("x"), lax.axis_index("y"), lax.axis_index("c")


def _remote(src, dst, send_sem, recv_sem, target):
    return pltpu.make_async_remote_copy(src_ref=src, dst_ref=dst, send_sem=send_sem, recv_sem=recv_sem,
                                        device_id=target, device_id_type=MESH)


def _mod_exchange(c_row, w_ada, b_ada):
    cols = w_ada.shape[1]

    def body(c_ref, w_ref, b_ref, mod_ref, call_ref, part_ref, send1, recv1, loc1, send2, recv2, loc2):
        x, y, c = _place()
        me = 4 * x + 2 * y + c
        own = pltpu.make_async_copy(c_ref, call_ref.at[pl.ds(me, 1)], loc1)
        own.start()
        sends = []
        for k, (dx, dy, dc) in enumerate(RELATIONS):
            cp = _remote(c_ref, call_ref.at[pl.ds(me, 1)], send1.at[k], recv1.at[k],
                         (_flip(x, dx), _flip(y, dy), _flip(c, dc)))
            cp.start()
            sends.append(cp)
        for k, (dx, dy, dc) in enumerate(RELATIONS):
            src = 4 * _flip(x, dx) + 2 * _flip(y, dy) + _flip(c, dc)
            _remote(c_ref, call_ref.at[pl.ds(src, 1)], send1.at[k], recv1.at[k], (x, y, c)).wait_recv()
        own.wait()
        for cp in sends:
            cp.wait_send()
        call = call_ref[...]
        act = call * jax.nn.sigmoid(call)
        part_ref[...] = _dot(act, w_ref[...]) + b_ref[...]
        chip = 2 * x + y
        mine = pltpu.make_async_copy(part_ref.at[pl.ds(me, 1)], mod_ref.at[pl.ds(chip, 1)], loc2)
        mine.start()
        sends = []
        for k, (dx, dy) in enumerate(CHIP_RELATIONS):
            tx, ty = _flip(x, dx), _flip(y, dy)
            tb = 4 * tx + 2 * ty + c
            cp = _remote(part_ref.at[pl.ds(tb, 1)], mod_ref.at[pl.ds(chip, 1)], send2.at[k], recv2.at[k], (tx, ty, c))
            cp.start()
            sends.append(cp)
        for k, (dx, dy) in enumerate(CHIP_RELATIONS):
            src_chip = 2 * _flip(x, dx) + _flip(y, dy)
            _remote(part_ref.at[pl.ds(me, 1)], mod_ref.at[pl.ds(src_chip, 1)], send2.at[k], recv2.at[k],
                    (x, y, c)).wait_recv()
        mine.wait()
        for cp in sends:
            cp.wait_send()

    return pl.pallas_call(
        body, name="mod_exchange",
        out_shape=[jax.ShapeDtypeStruct((N_CHIPS, cols), F32), jax.ShapeDtypeStruct((8, D_MODEL), F32)],
        in_specs=[VMEM_SPEC, VMEM_SPEC, VMEM_SPEC], out_specs=[VMEM_SPEC, VMEM_SPEC],
        scratch_shapes=[pltpu.VMEM((8, cols), F32),
                        pltpu.SemaphoreType.DMA((7,)), pltpu.SemaphoreType.DMA((7,)), pltpu.SemaphoreType.DMA,
                        pltpu.SemaphoreType.DMA((3,)), pltpu.SemaphoreType.DMA((3,)), pltpu.SemaphoreType.DMA],
        compiler_params=_params(),
    )(c_row, w_ada, b_ada)


HBM_SPEC = pl.BlockSpec(memory_space=pltpu.HBM)
SEM_SPEC = pl.BlockSpec(memory_space=pltpu.SEMAPHORE)
DATAFLOW = pltpu.SideEffectType.DATAFLOW_SIDE_EFFECTING


def _in_hbm(a):
    return pltpu.with_memory_space_constraint(a, pltpu.HBM)


def _hbm(*arrays):
    return tuple(_in_hbm(a) for a in arrays)


def _hbm_like(arrays):
    return [pltpu.HBM(a.shape, a.dtype) for a in arrays]


def _split_start(name, srcs, lands, n_remote, plan):
    lands = [lax.empty(a.shape, a.dtype) if isinstance(a, jax.ShapeDtypeStruct) else a for a in lands]
    n, m = len(srcs), len(lands)

    def body(*refs):
        src_refs, land_refs = refs[:n], refs[n:n + m]
        send_sems, recv_sems, token = refs[n + m], refs[n + m + 1], refs[n + 2 * m + 2]
        remote = plan(_place(), src_refs, land_refs)
        assert len(remote) == n_remote
        for i, (s, d, target) in enumerate(remote):
            _remote(s, d, send_sems.at[i], recv_sems.at[i], target).start()
        token[...] = jnp.zeros_like(token)

    res = pl.pallas_call(
        body, name=name,
        out_shape=(pltpu.SemaphoreType.DMA((n_remote,)), pltpu.SemaphoreType.DMA((n_remote,)),
                   *_hbm_like(lands), jax.ShapeDtypeStruct((8, 128), F32)),
        in_specs=[HBM_SPEC] * (n + m),
        out_specs=(SEM_SPEC, SEM_SPEC, *([HBM_SPEC] * m), VMEM_SPEC),
        input_output_aliases={n + i: 2 + i for i in range(m)},
        compiler_params=pltpu.CompilerParams(has_side_effects=DATAFLOW),
    )(*[_in_hbm(a) for a in srcs], *[_in_hbm(a) for a in lands])
    return res[0], res[1], list(res[2:2 + m]), res[2 + m]


def _split_wait(name, send_sems, recv_sems, srcs, lands, after, plan):
    n, m = len(srcs), len(lands)

    def body(*refs):
        src_refs, land_refs = refs[:n], refs[n:n + m]
        send_sems, recv_sems = refs[n + m], refs[n + m + 1]
        place = _place()
        for i, (s, d) in enumerate(plan(place, src_refs, land_refs)):
            cp = _remote(s, d, send_sems.at[i], recv_sems.at[i], place)
            cp.wait_send()
            cp.wait_recv()

    res = pl.pallas_call(
        body, name=name,
        out_shape=tuple(_hbm_like(lands)),
        in_specs=[HBM_SPEC] * (n + m) + [SEM_SPEC, SEM_SPEC, ANY],
        out_specs=tuple([HBM_SPEC] * m),
        input_output_aliases={n + i: i for i in range(m)},
        compiler_params=pltpu.CompilerParams(has_side_effects=DATAFLOW),
    )(*srcs, *lands, send_sems, recv_sems, after)
    return list(res)


def _split_relay(name, send_sems, recv_sems, srcs, lands, after, n_remote, plan_wait, plan_send):
    n, m = len(srcs), len(lands)

    def body(*refs):
        src_refs, land_refs = refs[:n], refs[n:n + m]
        old_send, old_recv = refs[n + m], refs[n + m + 1]
        new_send, new_recv = refs[n + m + 3], refs[n + m + 4]
        token = refs[n + m + 5 + m]
        place = _place()
        for i, (s, d) in enumerate(plan_wait(place, src_refs, land_refs)):
            cp = _remote(s, d, old_send.at[i], old_recv.at[i], place)
            cp.wait_send()
            cp.wait_recv()
        for i, (s, d, target) in enumerate(plan_send(place, land_refs)):
            _remote(s, d, new_send.at[i], new_recv.at[i], target).start()
        token[...] = jnp.zeros_like(token)

    res = pl.pallas_call(
        body, name=name,
        out_shape=(pltpu.SemaphoreType.DMA((n_remote,)), pltpu.SemaphoreType.DMA((n_remote,)),
                   *_hbm_like(lands), jax.ShapeDtypeStruct((8, 128), F32)),
        in_specs=[HBM_SPEC] * (n + m) + [SEM_SPEC, SEM_SPEC, ANY],
        out_specs=(SEM_SPEC, SEM_SPEC, *([HBM_SPEC] * m), VMEM_SPEC),
        input_output_aliases={n + i: 2 + i for i in range(m)},
        compiler_params=pltpu.CompilerParams(has_side_effects=DATAFLOW),
    )(*srcs, *lands, send_sems, recv_sems, after)
    return res[0], res[1], list(res[2:2 + m]), res[2 + m]


def _half(ref, core, axis=0):
    hr = ref.shape[axis] // 2
    return pl.ds(core * hr, hr)


def _plan_gather_start(place, src, land):
    x, y, c = place
    chip = 2 * x + y
    return [(s.at[_half(s, c)], l.at[chip, _half(s, c)], (_flip(x, dx), _flip(y, dy), c))
            for s, l in zip(src, land) for dx, dy in CHIP_RELATIONS]


def _plan_gather_landed(place, src, land):
    x, y, c = place
    return [(s.at[_half(s, c)], l.at[2 * _flip(x, dx) + _flip(y, dy), _half(s, c)])
            for s, l in zip(src, land) for dx, dy in CHIP_RELATIONS]


def _plan_gather_relay(place, land):
    x, y, c = place
    out = []
    for l in land:
        for dx, dy in CHIP_RELATIONS:
            got = l.at[2 * _flip(x, dx) + _flip(y, dy), _half(l, c, 1)]
            out.append((got, got, (x, y, 1 - c)))
    return out


def _plan_gather_wait(place, src, land):
    x, y, c = place
    out = []
    for l in land:
        for dx, dy in CHIP_RELATIONS:
            got = l.at[2 * _flip(x, dx) + _flip(y, dy), _half(l, 1 - c, 1)]
            out.append((got, got))
    return out


def _plan_swap_start(place, src, land):
    x, y, c = place
    return [(s.at[:, _half(s, 1 - c, 1), :], l, (x, y, 1 - c)) for s, l in zip(src, land)]


def _plan_swap_wait(place, src, land):
    return [(s.at[:, _half(s, 0, 1), :], l) for s, l in zip(src, land)]


def _plan_exchange_start(place, src, land):
    x, y, c = place
    remote = []
    for s, l in zip(src, land):
        for k, (dx, dy) in enumerate(CHIP_RELATIONS):
            tx, ty = _flip(x, dx), _flip(y, dy)
            remote.append((s.at[2 * tx + ty], l.at[k], (tx, ty, c)))
    return remote


def _plan_exchange_wait(place, src, land):
    return [(s.at[0], l.at[k]) for s, l in zip(src, land) for k in range(3)]


def _plan_finish_start(place, src, land):
    x, y, c = place
    return [(l.at[c], l.at[c], (x, y, 1 - c)) for l in land]


def _plan_finish_wait(place, src, land):
    x, y, c = place
    return [(l.at[c], l.at[1 - c]) for l in land]


def _plan_near_finish_start(place, src, land):
    x, y, c = place
    mine = land[-1].at[2 * x + y, c]
    return (_plan_finish_start(place, src, land[:-1])
            + [(mine, mine, (_flip(x, dx), _flip(y, dy), _flip(c, dc))) for dx, dy, dc in RELATIONS])


def _plan_near_finish_wait(place, src, land):
    x, y, c = place
    mine = land[-1].at[2 * x + y, c]
    return (_plan_finish_wait(place, src, land[:-1])
            + [(mine, land[-1].at[2 * _flip(x, dx) + _flip(y, dy), _flip(c, dc)]) for dx, dy, dc in RELATIONS])


def _grad_swap_halves(grads, dmod):
    n = len(grads)

    def body(*refs):
        ins, dmod_ref = refs[:n], refs[n]
        outs, dall_ref = refs[n + 1:2 * n + 1], refs[2 * n + 1]
        send_sems, recv_sems, dsend, drecv, dloc = refs[2 * n + 2:]
        x, y, c = _place()
        me = 4 * x + 2 * y + c
        sends = []
        for w in range(n):
            hr = ins[w].shape[1] // 2
            cp = _remote(ins[w].at[:, pl.ds((1 - c) * hr, hr), :], outs[w], send_sems.at[w], recv_sems.at[w],
                         (x, y, 1 - c))
            cp.start()
            sends.append(cp)
        own = pltpu.make_async_copy(dmod_ref, dall_ref.at[me], dloc)
        own.start()
        for k, (dx, dy, dc) in enumerate(RELATIONS):
            cp = _remote(dmod_ref, dall_ref.at[me], dsend.at[k], drecv.at[k],
                         (_flip(x, dx), _flip(y, dy), _flip(c, dc)))
            cp.start()
            sends.append(cp)
        for k, (dx, dy, dc) in enumerate(RELATIONS):
            src = 4 * _flip(x, dx) + 2 * _flip(y, dy) + _flip(c, dc)
            _remote(dmod_ref, dall_ref.at[src], dsend.at[k], drecv.at[k], (x, y, c)).wait_recv()
        for w in range(n):
            _remote(outs[w], outs[w], send_sems.at[w], recv_sems.at[w], (x, y, c)).wait_recv()
        own.wait()
        for cp in sends:
            cp.wait_send()

    out_shape = [pltpu.HBM((N_CHIPS, g.shape[1] // 2, g.shape[2]), F32) for g in grads]
    out_shape.append(pltpu.HBM((8,) + dmod.shape, F32))
    res = pl.pallas_call(
        body, name="grad_swap_halves",
        out_shape=out_shape, in_specs=[ANY] * n + [VMEM_SPEC], out_specs=[ANY] * (n + 1),
        scratch_shapes=[pltpu.SemaphoreType.DMA((n,)), pltpu.SemaphoreType.DMA((n,)),
                        pltpu.SemaphoreType.DMA((7,)), pltpu.SemaphoreType.DMA((7,)), pltpu.SemaphoreType.DMA],
        compiler_params=_params(),
    )(*grads, dmod)
    return res[:n], res[n]


def _add_my_half(core, full, got, name):
    _, hr, cols = got.shape

    def body(core_ref, a_ref, b_ref, o_ref):
        o_ref[...] = a_ref[...] + b_ref[...]

    return pl.pallas_call(
        body, name=name,
        out_shape=pltpu.HBM(got.shape, F32),
        grid_spec=pltpu.PrefetchScalarGridSpec(
            num_scalar_prefetch=1, grid=(N_CHIPS,),
            in_specs=[pl.BlockSpec((None, hr, cols), lambda s, core_ref: (s, core_ref[0], 0)),
                      pl.BlockSpec((None, hr, cols), lambda s, core_ref: (s, 0, 0))],
            out_specs=pl.BlockSpec((None, hr, cols), lambda s, core_ref: (s, 0, 0))),
        compiler_params=_params(("arbitrary",)),
    )(core, *_hbm(full, got))


def _add_chips_into_pair(chip_core, mine, got, name):
    _, hr, cols = mine.shape

    def body(cc_ref, a_ref, b_ref, o_ref):
        o_ref[...] = ((a_ref[...] + b_ref[0]) + b_ref[1]) + b_ref[2]

    return pl.pallas_call(
        body, name=name,
        out_shape=pltpu.HBM((2, hr, cols), F32),
        grid_spec=pltpu.PrefetchScalarGridSpec(
            num_scalar_prefetch=1, grid=(1,),
            in_specs=[pl.BlockSpec((None, hr, cols), lambda s, cc_ref: (cc_ref[0], 0, 0)),
                      pl.BlockSpec((3, hr, cols), lambda s, cc_ref: (0, 0, 0))],
            out_specs=pl.BlockSpec((None, hr, cols), lambda s, cc_ref: (cc_ref[1], 0, 0))),
        compiler_params=_params(("arbitrary",)),
    )(chip_core, *_hbm(mine, got))


def _add_chips_into_grid(chip_core, mine, got, name):
    _, hr, cols = mine.shape

    def body(cc_ref, a_ref, b_ref, o_ref):
        o_ref[...] = ((a_ref[...] + b_ref[0]) + b_ref[1]) + b_ref[2]

    return pl.pallas_call(
        body, name=name,
        out_shape=pltpu.HBM((N_CHIPS, 2, hr, cols), F32),
        grid_spec=pltpu.PrefetchScalarGridSpec(
            num_scalar_prefetch=1, grid=(1,),
            in_specs=[pl.BlockSpec((None, hr, cols), lambda s, cc_ref: (cc_ref[0], 0, 0)),
                      pl.BlockSpec((3, hr, cols), lambda s, cc_ref: (0, 0, 0))],
            out_specs=pl.BlockSpec((None, None, hr, cols), lambda s, cc_ref: (cc_ref[0], cc_ref[1], 0, 0))),
        compiler_params=_params(("arbitrary",)),
    )(chip_core, *_hbm(mine, got))


def _place_shards(chip, shards):
    n = len(shards)

    def body(chip_ref, *refs):
        for w in range(n):
            refs[n + w][...] = refs[w][...]

    return pl.pallas_call(
        body, name="place_shards",
        out_shape=[pltpu.HBM((N_CHIPS,) + s.shape, s.dtype) for s in shards],
        grid_spec=pltpu.PrefetchScalarGridSpec(
            num_scalar_prefetch=1, grid=(1,),
            in_specs=[pl.BlockSpec(s.shape, lambda i, chip_ref: (0, 0)) for s in shards],
            out_specs=[pl.BlockSpec((None,) + s.shape, lambda i, chip_ref: (chip_ref[0], 0, 0)) for s in shards]),
        compiler_params=_params(("arbitrary",)),
    )(chip, *shards)


def _rope_tables(pos_col, freqs):
    S = pos_col.shape[0]
    T = _row_tile(S, 1024)

    def body(p_ref, f_ref, cos_ref, sin_ref):
        ang = p_ref[...].astype(F32) * f_ref[...]
        cos_ref[...] = jnp.cos(ang)
        sin_ref[...] = jnp.sin(ang)

    return pl.pallas_call(
        body, name="rope_tables", grid=(S // T,),
        out_shape=[pltpu.HBM((S, 128), F32)] * 2,
        in_specs=[pl.BlockSpec((T, 1), lambda i: (i, 0)), pl.BlockSpec((1, 128), lambda i: (0, 0))],
        out_specs=[pl.BlockSpec((T, 128), lambda i: (i, 0))] * 2,
        compiler_params=_params(("parallel",)),
    )(*_hbm(pos_col, freqs))


def _full(shape):
    zeros = (0,) * len(shape)
    return pl.BlockSpec(shape, lambda *_: zeros)


def _pre_attention(x, mod6, g_mix, g_q, g_kv, w_in, w_uq, w_uk_t, cos, sin, T, TQ):
    S = x.shape[0]

    def body(x_ref, mod_ref, gm_ref, gq_ref, gkv_ref, win_ref, wuq_ref, wuk_ref, cos_ref, sin_ref,
             proj_ref, q_ref, qc_ref, kc_ref, kct_ref):
        xh, _ = _rms(x_ref[...])
        h1 = ((xh * gm_ref[...]) * (1.0 + mod_ref[1:2, :]) + mod_ref[0:1, :]).astype(BF16)
        rows_in = D_MODEL // N_CHIPS
        proj = _dot_nt(h1[:, 0:rows_in], win_ref[0])
        for j in range(1, N_CHIPS):
            proj = proj + _dot_nt(h1[:, j * rows_in:(j + 1) * rows_in], win_ref[j])
        proj_ref[...] = proj
        cqh, _ = _rms(proj[:, :Q_LORA])
        c_q = cqh * gq_ref[...]
        ckvh, _ = _rms(proj[:, O_CKV:O_KR])
        c_kv = ckvh * gkv_ref[...]
        q = _dot(c_q, wuq_ref[...])
        q_ref[...] = q
        cos_t, sin_t = cos_ref[...], sin_ref[...]
        ropes = (_rope(q[:, O_QA:O_QB], cos_t, sin_t), _rope(q[:, O_QB:Q_W], cos_t, sin_t))
        low = lax.broadcasted_iota(jnp.int32, (T, 128), 1) < ROPE
        for h in range(HEADS):
            q_lat = _dot_nt(q[:, h * NOPE:(h + 1) * NOPE], wuk_ref[h])
            keep = low if h % 2 == 0 else jnp.logical_not(low)
            qc_ref[h, :, 0:KV_LORA] = q_lat.astype(BF16)
            qc_ref[h, :, KV_LORA:QK_PAD] = jnp.where(keep, ropes[h // 2], 0.0).astype(BF16)
        k_rope = _rope(proj[:, O_KR:O_U], cos_t, sin_t)
        kc_ref[:, 0:KV_LORA] = c_kv.astype(BF16)
        kc_ref[:, KV_LORA:QK_PAD] = k_rope.astype(BF16)
        lat_t, rope_t = jnp.transpose(c_kv), jnp.transpose(k_rope)
        for s in range(T // TQ):
            kct_ref[s, 0:KV_LORA, :] = lat_t[:, s * TQ:(s + 1) * TQ].astype(BF16)
            kct_ref[s, KV_LORA:QK_PAD, :] = rope_t[:, s * TQ:(s + 1) * TQ].astype(BF16)

    row = lambda w: pl.BlockSpec((T, w), lambda i: (i, 0))
    return pl.pallas_call(
        body, name="pre_attention", grid=(S // T,),
        out_shape=[pltpu.HBM((S, PROJ_W), F32), pltpu.HBM((S, Q_W), F32), pltpu.HBM((HEADS, S, QK_PAD), BF16),
                   pltpu.HBM((S, QK_PAD), BF16), pltpu.HBM((S // TQ, QK_PAD, TQ), BF16)],
        in_specs=[row(D_MODEL), _full((N_MOD, D_MODEL)), _full((1, D_MODEL)), _full((1, Q_LORA)), _full((1, KV_LORA)),
                  _full((N_CHIPS, PROJ_W, D_MODEL // N_CHIPS)), _full((Q_LORA, Q_W)), _full((HEADS, KV_LORA, NOPE)),
                  row(128), row(128)],
        out_specs=[row(PROJ_W), row(Q_W), pl.BlockSpec((HEADS, T, QK_PAD), lambda i: (0, i, 0)), row(QK_PAD),
                   pl.BlockSpec((T // TQ, QK_PAD, TQ), lambda i: (i, 0, 0))],
        compiler_params=_params(("parallel",)),
    )(*_hbm(x, mod6, g_mix, g_q, g_kv, w_in, w_uq, w_uk_t, cos, sin))


def _diag_mask(TQ, width):
    key = lax.broadcasted_iota(jnp.int32, (TQ, width), 0) >> CHUNK_SHIFT
    qry = (lax.broadcasted_iota(jnp.int32, (TQ, width), 1) & (TQ - 1)) >> CHUNK_SHIFT
    return key <= qry


def _col_to_row(col):
    return jnp.transpose(jnp.broadcast_to(col, (col.shape[0], 128)))[0:1, :]


def _attention_fwd(qc, kc, kct, w_uv_t, TQ):
    S = kc.shape[0]
    R = HEADS * TQ
    nq = S // TQ

    def body(q_ref, k_ref, kt_ref, wuv_ref, o_ref, y_ref, lser_ref, m_s, l_s, acc_s, st_s):
        i = pl.program_id(0)
        q = q_ref[...].reshape(R, QK_PAD)
        m_s[...] = jnp.full((1, R), -jnp.inf, F32)
        l_s[...] = jnp.zeros((1, R), F32)
        acc_s[...] = jnp.zeros((KV_LORA, R), F32)

        def scores(j):
            return _dot_nt(k_ref[pl.ds(pl.multiple_of(j * TQ, TQ), TQ), :], q) * SM_SCALE

        def update(j, st):
            m_old = m_s[...]
            m_new = jnp.maximum(m_old, jnp.max(st, axis=0, keepdims=True))
            pt = jnp.exp(st - m_new)
            alpha = jnp.exp(m_old - m_new)
            l_s[...] = alpha * l_s[...] + jnp.sum(pt, axis=0, keepdims=True)
            acc_s[...] = alpha * acc_s[...] + _dot(kt_ref[j, 0:KV_LORA, :], pt)
            m_s[...] = m_new

        st_s[...] = scores(0)

        def loop(j, carry):
            st = st_s[...]
            st_s[...] = scores(j + 1)
            update(j, st)
            return carry

        lax.fori_loop(0, i, loop, 0)
        update(i, jnp.where(_diag_mask(TQ, R), st_s[...], -jnp.inf))
        l = l_s[...]
        lser_ref[0] = m_s[...] + jnp.log(l)
        o = jnp.transpose(acc_s[...] / l).astype(BF16)
        for h in range(HEADS):
            oh = o[h * TQ:(h + 1) * TQ, :]
            o_ref[h] = oh
            y_ref[:, h * 128:(h + 1) * 128] = _dot(oh, wuv_ref[h]).astype(BF16)

    return pl.pallas_call(
        body, name="attention_fwd", grid=(nq,),
        out_shape=[pltpu.HBM((HEADS, S, KV_LORA), BF16), pltpu.HBM((S, HEADS * 128), BF16),
                   pltpu.HBM((nq, 1, R), F32)],
        in_specs=[pl.BlockSpec((HEADS, TQ, QK_PAD), lambda i: (0, i, 0)), _full((S, QK_PAD)),
                  _full((nq, QK_PAD, TQ)), _full((HEADS, KV_LORA, 128))],
        out_specs=[pl.BlockSpec((HEADS, TQ, KV_LORA), lambda i: (0, i, 0)), pl.BlockSpec((TQ, HEADS * 128), lambda i: (i, 0)),
                   pl.BlockSpec((1, 1, R), lambda i: (i, 0, 0))],
        scratch_shapes=[pltpu.VMEM((1, R), F32), pltpu.VMEM((1, R), F32), pltpu.VMEM((KV_LORA, R), F32),
                        pltpu.VMEM((TQ, R), F32)],
        compiler_params=_params(("parallel",)),
    )(*_hbm(qc, kc, kct, w_uv_t))


def _pool_forward(proj):
    S = proj.shape[0]
    RB = _row_tile(S, 256)

    def body(proj_ref, out_ref, pad_ref, sem):
        cp = pltpu.make_async_copy(proj_ref.at[:, pl.ds(O_U, POOL_W)], pad_ref.at[pl.ds(POOL_PAD, S)], sem)
        cp.start()
        pad_ref[0:POOL_PAD, :] = jnp.zeros((POOL_PAD, POOL_W), F32)
        cp.wait()
        for g, win in enumerate(POOL_WINDOWS):
            cols = slice(g * POOL_GROUP, (g + 1) * POOL_GROUP)
            for r0 in range(0, S, RB):
                u = pad_ref[POOL_PAD + r0:POOL_PAD + r0 + RB, cols]
                acc = u
                for k in range(1, win):
                    acc = acc + pad_ref[POOL_PAD + r0 - k:POOL_PAD + r0 - k + RB, cols]
                if r0 == 0:
                    t1 = (lax.broadcasted_iota(jnp.int32, (RB, POOL_GROUP), 0) + 1).astype(F32)
                    mean = acc / jnp.minimum(t1, float(win))
                else:
                    mean = acc * (1.0 / win)
                out_ref[r0:r0 + RB, cols] = (mean - u).astype(BF16)

    return pl.pallas_call(
        body, name="pool_forward",
        out_shape=jax.ShapeDtypeStruct((S, POOL_W), BF16),
        in_specs=[ANY], out_specs=VMEM_SPEC,
        scratch_shapes=[pltpu.VMEM((S + POOL_PAD, POOL_W), F32), pltpu.SemaphoreType.DMA],
        compiler_params=_params(),
    )(proj)


def _pool_backward(dpooled, after):
    S = dpooled.shape[0]
    RB = _row_tile(S, 256)

    def body(dp_ref, after_ref, out_ref, pad_ref, sem):
        cp = pltpu.make_async_copy(dp_ref, pad_ref.at[pl.ds(0, S)], sem)
        cp.start()
        pad_ref[S:S + POOL_PAD, :] = jnp.zeros((POOL_PAD, POOL_W), F32)
        cp.wait()
        for g, win in enumerate(POOL_WINDOWS):
            cols = slice(g * POOL_GROUP, (g + 1) * POOL_GROUP)
            head = pad_ref[0:POOL_PAD, cols]
            t1 = (lax.broadcasted_iota(jnp.int32, (POOL_PAD, POOL_GROUP), 0) + 1).astype(F32)
            pad_ref[0:POOL_PAD, cols] = head * (float(win) / jnp.minimum(t1, float(win)))
            for r0 in range(0, S, RB):
                acc = pad_ref[r0:r0 + RB, cols]
                for k in range(1, win):
                    acc = acc + pad_ref[r0 + k:r0 + k + RB, cols]
                own = pad_ref[r0:r0 + RB, cols]
                if r0 == 0:
                    own = jnp.concatenate([head, own[POOL_PAD:]], axis=0)
                out_ref[r0:r0 + RB, cols] = acc * (1.0 / win) - own

    return pl.pallas_call(
        body, name="pool_backward",
        out_shape=jax.ShapeDtypeStruct((S, POOL_W), F32),
        in_specs=[ANY, ANY], out_specs=VMEM_SPEC,
        scratch_shapes=[pltpu.VMEM((S + POOL_PAD, POOL_W), F32), pltpu.SemaphoreType.DMA],
        compiler_params=_params(),
    )(dpooled, after)


def _mix_out(y_mla, pooled, w_pool, pool_scale, w_o, x, mod6, T):
    S = x.shape[0]

    def body(ym_ref, pl_ref, wp_ref, ps_ref, wo_ref, x_ref, mod_ref, x1_ref, mix_ref, mi_ref):
        mi_ref[:, 0:512] = ym_ref[...]
        for g in range(len(POOL_WINDOWS)):
            cols = slice(g * POOL_GROUP, (g + 1) * POOL_GROUP)
            z = _dot(pl_ref[:, cols], wp_ref[g])
            mi_ref[:, 512 + g * POOL_GROUP:512 + (g + 1) * POOL_GROUP] = (z * ps_ref[:, cols]).astype(BF16)
        mix = _dot(mi_ref[...], wo_ref[...])
        mix_ref[...] = mix
        x1_ref[...] = x_ref[...] + mod_ref[2:3, :] * mix

    row = lambda w: pl.BlockSpec((T, w), lambda i: (i, 0))
    return pl.pallas_call(
        body, name="mix_out", grid=(S // T,),
        out_shape=[pltpu.HBM((S, D_MODEL), F32), pltpu.HBM((S, D_MODEL), F32), pltpu.HBM((S, 1024), BF16)],
        in_specs=[row(512), row(POOL_W), _full((4, POOL_GROUP, POOL_GROUP)), _full((1, POOL_W)),
                  _full((1024, D_MODEL)), row(D_MODEL), _full((N_MOD, D_MODEL))],
        out_specs=[row(D_MODEL), row(D_MODEL), row(1024)],
        compiler_params=_params(("parallel",)),
    )(*_hbm(y_mla, pooled, w_pool, pool_scale, w_o, x, mod6))


def _ffn_forward(x1, mod6, g_ffn, g_final, target, w_gate, w_up, w_down, T):
    S = x1.shape[0]

    def body(x1_ref, mod_ref, gf_ref, gl_ref, tgt_ref, wg_ref, wu_ref, wd_ref,
             gate_ref, up_ref, act_ref, h2_ref, dff_ref, dx2_ref, st_ref, acc_s):
        i, j = pl.program_id(0), pl.program_id(1)

        @pl.when(jnp.logical_and(i == 0, j == 0))
        def _():
            st_ref[...] = jnp.zeros_like(st_ref)

        @pl.when(j == 0)
        def _():
            xh, _ = _rms(x1_ref[...])
            h2_ref[...] = ((xh * gf_ref[...]) * (1.0 + mod_ref[4:5, :]) + mod_ref[3:4, :]).astype(BF16)
            acc_s[...] = jnp.zeros_like(acc_s)

        h2 = h2_ref[...]
        gate = _dot_nt(h2, wg_ref[j])
        up = _dot_nt(h2, wu_ref[j])
        gate_ref[...] = gate.astype(BF16)
        up_ref[...] = up.astype(BF16)
        act = (gate * jax.nn.sigmoid(gate) * up).astype(BF16)
        act_ref[...] = act
        acc_s[...] += _dot(act, wd_ref[j])

        @pl.when(j == N_CHIPS - 1)
        def _():
            ff = acc_s[...]
            x2 = x1_ref[...] + mod_ref[5:6, :] * ff
            xh, r3 = _rms(x2)
            err = xh * gl_ref[...] - tgt_ref[...]
            dy = err * (1.0 / D_MODEL)
            dx2 = _rms_bwd(dy * gl_ref[...], xh, r3)
            dx2_ref[...] = dx2
            dff_ref[...] = (dx2 * mod_ref[5:6, :]).astype(BF16)
            st_ref[0:1, :] += jnp.sum(dy * xh, axis=0, keepdims=True)
            st_ref[1:2, :] += jnp.sum(dx2 * ff, axis=0, keepdims=True)
            st_ref[2:3, :] += 0.5 * jnp.sum(err * dy)

    row = pl.BlockSpec((T, D_MODEL), lambda i, j: (i, 0))
    chunk_out = pl.BlockSpec((None, T, FF_CHUNK), lambda i, j: (j, i, 0))
    big = pltpu.HBM((N_CHIPS, S, FF_CHUNK), BF16)
    wide = pltpu.HBM((S, D_MODEL), BF16)
    return pl.pallas_call(
        body, name="ffn_forward", grid=(S // T, N_CHIPS),
        out_shape=[big, big, big, wide, wide, pltpu.HBM((S, D_MODEL), F32), jax.ShapeDtypeStruct((8, D_MODEL), F32)],
        in_specs=[row, _full((N_MOD, D_MODEL)), _full((1, D_MODEL)), _full((1, D_MODEL)), row,
                  VMEM_SPEC, VMEM_SPEC, VMEM_SPEC],
        out_specs=[chunk_out, chunk_out, chunk_out, row, row, row, _full((8, D_MODEL))],
        scratch_shapes=[pltpu.VMEM((T, D_MODEL), F32)],
        compiler_params=_params(("arbitrary", "arbitrary")),
    )(*_hbm(x1, mod6, g_ffn, g_final, target), w_gate, w_up, w_down)


def _ffn_backward(dx2, x1, dff, gate, up, mod6, g_ffn, w_gate, w_up, w_down, T):
    S = x1.shape[0]

    def body(dx2_ref, x1_ref, dff_ref, gate_ref, up_ref, mod_ref, gf_ref, wg_ref, wu_ref, wd_ref,
             dgate_ref, dup_ref, dx1_ref, st_ref, acc_s):
        i, j = pl.program_id(0), pl.program_id(1)

        @pl.when(jnp.logical_and(i == 0, j == 0))
        def _():
            st_ref[...] = jnp.zeros_like(st_ref)

        @pl.when(j == 0)
        def _():
            acc_s[...] = jnp.zeros_like(acc_s)

        for r0 in range(0, T, T // 2):
            rows = slice(r0, r0 + T // 2)
            gate, up = gate_ref[rows, :].astype(F32), up_ref[rows, :].astype(F32)
            sg = jax.nn.sigmoid(gate)
            dact = _dot_nt(dff_ref[rows, :], wd_ref[j])
            dup = (dact * (gate * sg)).astype(BF16)
            dgate = (dact * up * (sg * (1.0 + gate * (1.0 - sg)))).astype(BF16)
            dup_ref[rows, :] = dup
            dgate_ref[rows, :] = dgate
            acc_s[rows, :] += _dot(dgate, wg_ref[j]) + _dot(dup, wu_ref[j])

        @pl.when(j == N_CHIPS - 1)
        def _():
            dh2 = acc_s[...]
            xh, r2 = _rms(x1_ref[...])
            n2 = xh * gf_ref[...]
            st_ref[0:1, :] += jnp.sum(dh2, axis=0, keepdims=True)
            st_ref[1:2, :] += jnp.sum(dh2 * n2, axis=0, keepdims=True)
            dn2 = dh2 * (1.0 + mod_ref[4:5, :])
            st_ref[2:3, :] += jnp.sum(dn2 * xh, axis=0, keepdims=True)
            dx1_ref[...] = _rms_bwd(dn2 * gf_ref[...], xh, r2) + dx2_ref[...]

    row = pl.BlockSpec((T, D_MODEL), lambda i, j: (i, 0))
    chunk = pl.BlockSpec((None, T, FF_CHUNK), lambda i, j: (j, i, 0))
    big = pltpu.HBM((N_CHIPS, S, FF_CHUNK), BF16)
    return pl.pallas_call(
        body, name="ffn_backward", grid=(S // T, N_CHIPS),
        out_shape=[big, big, pltpu.HBM((S, D_MODEL), F32), jax.ShapeDtypeStruct((8, D_MODEL), F32)],
        in_specs=[row, row, row, chunk, chunk, _full((N_MOD, D_MODEL)), _full((1, D_MODEL)),
                  VMEM_SPEC, VMEM_SPEC, VMEM_SPEC],
        out_specs=[chunk, chunk, row, _full((8, D_MODEL))],
        scratch_shapes=[pltpu.VMEM((T, D_MODEL), F32)],
        compiler_params=_params(("arbitrary", "arbitrary")),
    )(*_hbm(dx2, x1, dff, gate, up, mod6, g_ffn), w_gate, w_up, w_down)


def _tn_matmul(a, b, a_spec, b_spec, groups, m, n, steps, name):
    def body(a_ref, b_ref, o_ref):
        @pl.when(pl.program_id(1) == 0)
        def _():
            o_ref[...] = jnp.zeros_like(o_ref)

        o_ref[...] += _dot_tn(a_ref[...], b_ref[...])

    return pl.pallas_call(
        body, name=name, grid=(groups, steps),
        out_shape=pltpu.HBM((groups, m, n), F32),
        in_specs=[a_spec, b_spec],
        out_specs=pl.BlockSpec((None, m, n), lambda g, i: (g, 0, 0)),
        compiler_params=_params(("parallel", "arbitrary")),
    )(*_hbm(a, b))


def _mix_backward(dx1, mix, mod6, w_o, pooled, w_pool, pool_scale, w_uv_t, o_lat, T, TQ):
    S = dx1.shape[0]

    def body(dx1_ref, mix_ref, mod_ref, wo_ref, pl_ref, wp_ref, ps_ref, wuv_ref, o_ref,
             dmix_ref, dp_ref, do_ref, dr_ref, gp_ref, guv_ref, st_ref):
        @pl.when(pl.program_id(0) == 0)
        def _():
            st_ref[...] = jnp.zeros_like(st_ref)
            gp_ref[...] = jnp.zeros_like(gp_ref)
            guv_ref[...] = jnp.zeros_like(guv_ref)

        dx1 = dx1_ref[...]
        st_ref[0:1, :] += jnp.sum(dx1 * mix_ref[...], axis=0, keepdims=True)
        dmix = (dx1 * mod_ref[2:3, :]).astype(BF16)
        dmix_ref[...] = dmix
        dmi = _dot_nt(dmix, wo_ref[...])
        dym = dmi[:, 0:512].astype(BF16)
        for g in range(len(POOL_WINDOWS)):
            cols = slice(g * POOL_GROUP, (g + 1) * POOL_GROUP)
            dyp = dmi[:, 512 + g * POOL_GROUP:512 + (g + 1) * POOL_GROUP]
            pooled_g = pl_ref[:, cols]
            z = _dot(pooled_g, wp_ref[g])
            st_ref[1:2, cols] += jnp.sum(dyp * z, axis=0, keepdims=True)
            dz = (dyp * ps_ref[:, cols]).astype(BF16)
            gp_ref[g] += _dot_tn(pooled_g, dz)
            dp_ref[:, cols] = _dot_nt(dz, wp_ref[g])
        for h in range(HEADS):
            dym_h = dym[:, h * 128:(h + 1) * 128]
            do = _dot_nt(dym_h, wuv_ref[h]).astype(BF16)
            do_ref[h] = do
            o_h = o_ref[h]
            guv_ref[h] += _dot_tn(o_h, dym_h)
            delta = _col_to_row(jnp.sum(do.astype(F32) * o_h.astype(F32), axis=1, keepdims=True))
            for s in range(T // TQ):
                dr_ref[s, :, h * TQ:(h + 1) * TQ] = delta[:, s * TQ:(s + 1) * TQ]

    row = lambda w: pl.BlockSpec((T, w), lambda i: (i, 0))
    heads = pl.BlockSpec((HEADS, T, KV_LORA), lambda i: (0, i, 0))
    square = jax.ShapeDtypeStruct((4, 128, 128), F32)
    return pl.pallas_call(
        body, name="mix_backward", grid=(S // T,),
        out_shape=[pltpu.HBM((S, D_MODEL), BF16), pltpu.HBM((S, POOL_W), F32), pltpu.HBM((HEADS, S, KV_LORA), BF16),
                   pltpu.HBM((S // TQ, 1, HEADS * TQ), F32), square, square, jax.ShapeDtypeStruct((8, D_MODEL), F32)],
        in_specs=[row(D_MODEL), row(D_MODEL), _full((N_MOD, D_MODEL)), _full((1024, D_MODEL)), row(POOL_W),
                  _full((4, POOL_GROUP, POOL_GROUP)), _full((1, POOL_W)), _full((HEADS, KV_LORA, 128)), heads],
        out_specs=[row(D_MODEL), row(POOL_W), heads,
                   pl.BlockSpec((T // TQ, 1, HEADS * TQ), lambda i: (i, 0, 0)), _full((4, 128, 128)),
                   _full((4, 128, 128)), _full((8, D_MODEL))],
        compiler_params=_params(("arbitrary",)),
    )(*_hbm(dx1, mix, mod6, w_o, pooled, w_pool, pool_scale, w_uv_t, o_lat))


def _attention_bwd(qc, kc, kct, do, lse_rows, delta_rows, TQ):
    S = kc.shape[0]
    R = HEADS * TQ
    nq = S // TQ

    def body(k_ref, kt_ref, q_ref, do_ref, lser_ref, dr_ref, dk_ref, dqt_ref, dk_s, dv_s):
        j = pl.program_id(0)

        @pl.when(j == 0)
        def _():
            def zero(i, carry):
                dqt_ref[i] = jnp.zeros((QK_PAD, R), F32)
                return carry
            lax.fori_loop(0, nq, zero, 0)

        k = k_ref[...]
        kt = kt_ref[...]
        v = k[:, :KV_LORA]
        dk_s[...] = jnp.zeros((TQ, QK_PAD), F32)
        dv_s[...] = jnp.zeros((TQ, KV_LORA), F32)

        def step(i, masked):
            rows = pl.ds(pl.multiple_of(i * TQ, TQ), TQ)
            q = q_ref[:, rows, :].reshape(R, QK_PAD)
            do = do_ref[:, rows, :].reshape(R, KV_LORA)
            st = _dot_nt(k, q) * SM_SCALE
            if masked:
                st = jnp.where(_diag_mask(TQ, R), st, -jnp.inf)
            pt = jnp.exp(st - lser_ref[i])
            dv_s[...] += _dot(pt, do)
            dpt = _dot_nt(v, do)
            dst = (pt * (dpt - dr_ref[i])).astype(BF16)
            dk_s[...] += _dot(dst, q)
            dqt_ref[i] += _dot(kt, dst)

        def loop(i, carry):
            step(i, False)
            return carry

        step(j, True)
        lax.fori_loop(j + 1, nq, loop, 0)
        dk = dk_s[...] * SM_SCALE
        dk_ref[:, 0:KV_LORA] = dk[:, 0:KV_LORA] + dv_s[...]
        dk_ref[:, KV_LORA:QK_PAD] = dk[:, KV_LORA:QK_PAD]

    return pl.pallas_call(
        body, name="attention_bwd", grid=(nq,),
        out_shape=[pltpu.HBM((S, QK_PAD), F32), jax.ShapeDtypeStruct((nq, QK_PAD, R), F32)],
        in_specs=[pl.BlockSpec((TQ, QK_PAD), lambda j: (j, 0)), pl.BlockSpec((None, QK_PAD, TQ), lambda j: (j, 0, 0)),
                  VMEM_SPEC, VMEM_SPEC, VMEM_SPEC, VMEM_SPEC],
        out_specs=[pl.BlockSpec((TQ, QK_PAD), lambda j: (j, 0)), VMEM_SPEC],
        scratch_shapes=[pltpu.VMEM((TQ, QK_PAD), F32), pltpu.VMEM((TQ, KV_LORA), F32)],
        compiler_params=_params(("arbitrary",)),
    )(*_hbm(kc, kct), qc, do, lse_rows, delta_rows)


def _pre_attention_backward(x, dx1, proj, q, dqt, dkc, du, cos, sin, mod6, g_mix, g_q, g_kv, w_in, w_uq, w_uk_t, T, TQ):
    S = x.shape[0]

    def body(x_ref, dx1_ref, proj_ref, q_ref, dqt_ref, dkc_ref, du_ref, cos_ref, sin_ref, mod_ref, gm_ref, gq_ref,
             gkv_ref, win_ref, wuq_ref, wuk_ref, gx_ref, dproj_ref, h1_ref, guk_ref, guq_ref, st_ref, dq_ref):
        @pl.when(pl.program_id(0) == 0)
        def _():
            st_ref[...] = jnp.zeros_like(st_ref)
            guk_ref[...] = jnp.zeros_like(guk_ref)
            guq_ref[...] = jnp.zeros_like(guq_ref)

        cos_t, sin_t = cos_ref[...], sin_ref[...]
        low = lax.broadcasted_iota(jnp.int32, (T, 128), 1) < ROPE
        rope_parts = []
        for h in range(HEADS):
            dqc = jnp.concatenate([jnp.transpose(dqt_ref[s, :, h * TQ:(h + 1) * TQ]) for s in range(T // TQ)], axis=0)
            dqc = dqc * SM_SCALE
            dql = dqc[:, 0:KV_LORA].astype(BF16)
            guk_ref[h] += _dot_tn(dql, q_ref[:, h * NOPE:(h + 1) * NOPE])
            dq_ref[:, h * NOPE:(h + 1) * NOPE] = _dot(dql, wuk_ref[h]).astype(BF16)
            rope_parts.append(dqc[:, KV_LORA:QK_PAD])
        for pair in range(2):
            d = jnp.where(low, rope_parts[2 * pair], rope_parts[2 * pair + 1])
            dq_ref[:, O_QA + 128 * pair:O_QA + 128 * (pair + 1)] = _rope_bwd(d, cos_t, sin_t).astype(BF16)
        dq = dq_ref[...]
        dcq = _dot_nt(dq, wuq_ref[...])
        cqh, rq = _rms(proj_ref[:, 0:Q_LORA])
        guq_ref[...] += _dot_tn(cqh * gq_ref[...], dq)
        st_ref[3:4, 0:Q_LORA] += jnp.sum(dcq * cqh, axis=0, keepdims=True)
        dproj_ref[:, 0:Q_LORA] = _rms_bwd(dcq * gq_ref[...], cqh, rq).astype(BF16)
        dckv = dkc_ref[:, 0:KV_LORA]
        ckvh, rkv = _rms(proj_ref[:, O_CKV:O_KR])
        st_ref[4:5, 0:KV_LORA] += jnp.sum(dckv * ckvh, axis=0, keepdims=True)
        dproj_ref[:, O_CKV:O_KR] = _rms_bwd(dckv * gkv_ref[...], ckvh, rkv).astype(BF16)
        dkr = _rope_bwd(dkc_ref[:, KV_LORA:QK_PAD], cos_t, sin_t)
        dkr = jnp.where(low, dkr + pltpu.roll(dkr, ROPE, 1), 0.0)
        dproj_ref[:, O_KR:O_U] = dkr.astype(BF16)
        dproj_ref[:, O_U:PROJ_W] = du_ref[...].astype(BF16)
        dproj = dproj_ref[...]
        dh1 = jnp.concatenate([_dot(dproj, win_ref[j]) for j in range(N_CHIPS)], axis=1)
        xh, r1 = _rms(x_ref[...])
        n1 = xh * gm_ref[...]
        h1_ref[...] = (n1 * (1.0 + mod_ref[1:2, :]) + mod_ref[0:1, :]).astype(BF16)
        st_ref[0:1, :] += jnp.sum(dh1, axis=0, keepdims=True)
        st_ref[1:2, :] += jnp.sum(dh1 * n1, axis=0, keepdims=True)
        dn1 = dh1 * (1.0 + mod_ref[1:2, :])
        st_ref[2:3, :] += jnp.sum(dn1 * xh, axis=0, keepdims=True)
        gx_ref[...] = _rms_bwd(dn1 * gm_ref[...], xh, r1) + dx1_ref[...]

    row = lambda w: pl.BlockSpec((T, w), lambda i: (i, 0))
    return pl.pallas_call(
        body, name="pre_attention_backward", grid=(S // T,),
        out_shape=[jax.ShapeDtypeStruct((S, D_MODEL), F32), pltpu.HBM((S, PROJ_W), BF16),
                   pltpu.HBM((S, D_MODEL), BF16), jax.ShapeDtypeStruct((HEADS, KV_LORA, NOPE), F32),
                   jax.ShapeDtypeStruct((Q_LORA, Q_W), F32), jax.ShapeDtypeStruct((8, D_MODEL), F32)],
        in_specs=[row(D_MODEL), row(D_MODEL), row(PROJ_W), row(HEADS * NOPE),
                  pl.BlockSpec((T // TQ, QK_PAD, HEADS * TQ), lambda i: (i, 0, 0)),
                  row(QK_PAD), row(POOL_W), row(128), row(128), _full((N_MOD, D_MODEL)), _full((1, D_MODEL)),
                  _full((1, Q_LORA)), _full((1, KV_LORA)), _full((N_CHIPS, PROJ_W, D_MODEL // N_CHIPS)),
                  _full((Q_LORA, Q_W)), _full((HEADS, KV_LORA, NOPE))],
        out_specs=[row(D_MODEL), row(PROJ_W), row(D_MODEL), _full((HEADS, KV_LORA, NOPE)), _full((Q_LORA, Q_W)),
                   _full((8, D_MODEL))],
        scratch_shapes=[pltpu.VMEM((T, Q_W), BF16)],
        compiler_params=_params(("arbitrary",)),
    )(*_hbm(x, dx1, proj, q, dqt, dkc, du, cos, sin, mod6, g_mix, g_q, g_kv, w_in, w_uq, w_uk_t))


def _ada_grads(c_all, dmod_all, chip):
    cols = N_MOD * D_MODEL // N_CHIPS
    width = dmod_all.shape[1]

    def body(col_ref, c_ref, dcol_ref, dall_ref, gw_ref, gb_ref):
        call = c_ref[...]
        act = call * jax.nn.sigmoid(call)
        gw_ref[...] = _dot_tn(act, dcol_ref[...])
        d = dall_ref[...]
        acc = d[0:1, :]
        for b in range(1, 8):
            acc = acc + d[b:b + 1, :]
        gb_ref[...] = acc

    return pl.pallas_call(
        body, name="ada_grads",
        out_shape=[jax.ShapeDtypeStruct((D_MODEL, cols), F32), jax.ShapeDtypeStruct((1, width), F32)],
        grid_spec=pltpu.PrefetchScalarGridSpec(
            num_scalar_prefetch=1, grid=(1,),
            in_specs=[pl.BlockSpec((8, D_MODEL), lambda s, col_ref: (0, 0)),
                      pl.BlockSpec((8, cols), lambda s, col_ref: (0, col_ref[0])),
                      pl.BlockSpec((8, width), lambda s, col_ref: (0, 0))],
            out_specs=[pl.BlockSpec((D_MODEL, cols), lambda s, col_ref: (0, 0)),
                       pl.BlockSpec((1, width), lambda s, col_ref: (0, 0))]),
        compiler_params=_params(("arbitrary",)),
    )(chip, *_hbm(c_all, dmod_all, dmod_all))


def _adamw(w, g, m, v, name):
    rows, rest = w.shape[0], w.shape[1:]
    T = _row_tile(rows, 256)

    def body(w_ref, g_ref, m_ref, v_ref, go_ref, d_ref, nm_ref, nv_ref):
        g = g_ref[...]
        go_ref[...] = g
        m2 = ADAM_B1 * m_ref[...] + (1.0 - ADAM_B1) * g
        v2 = ADAM_B2 * v_ref[...] + (1.0 - ADAM_B2) * (g * g)
        m_hat = m2 / (1.0 - ADAM_B1 ** ADAM_STEP)
        v_hat = v2 / (1.0 - ADAM_B2 ** ADAM_STEP)
        d_ref[...] = -ADAM_LR * (m_hat / (jnp.sqrt(v_hat) + ADAM_EPS) + ADAM_WD * w_ref[...])
        nm_ref[...] = m2
        nv_ref[...] = v2

    zeros = (0,) * len(rest)
    spec = pl.BlockSpec((T,) + rest, lambda i: (i,) + zeros)
    return pl.pallas_call(
        body, name=name, grid=(rows // T,),
        out_shape=[jax.ShapeDtypeStruct(w.shape, F32)] * 4,
        in_specs=[spec] * 4, out_specs=[spec] * 4,
        compiler_params=_params(("parallel",)),
    )(*_hbm(w, g, m, v))


SMALL_NAMES = ("w_uk", "w_uv", "w_pool", "g_mix", "g_q", "g_kv", "pool_scale", "g_ffn", "g_final", "b_ada")
SMALL_ROWS = 1664


def _pack_rows(parts):
    flat = jnp.concatenate([p.reshape(-1) for p in parts])
    pad = (-flat.shape[0]) % 128
    if pad:
        flat = jnp.concatenate([flat, jnp.zeros((pad,), F32)])
    return flat.reshape(-1, 128)


def kernel(x, c, positions, w_ada, b_ada, g_mix, w_in, g_q, g_kv, w_uq, w_uk, w_uv, w_pool, pool_scale, w_o, g_ffn, w_gate, w_up, w_down, g_final, loss_target, m_w_ada, m_b_ada, m_g_mix, m_w_in, m_g_q, m_g_kv, m_w_uq, m_w_uk, m_w_uv, m_w_pool, m_pool_scale, m_w_o, m_g_ffn, m_w_gate, m_w_up, m_w_down, m_g_final, v_w_ada, v_b_ada, v_g_mix, v_w_in, v_g_q, v_g_kv, v_w_uq, v_w_uk, v_w_uv, v_w_pool, v_pool_scale, v_w_o, v_g_ffn, v_w_gate, v_w_up, v_w_down, v_g_final):
    S = x.shape[1]
    T = _row_tile(S, 512)
    TQ = _row_tile(S, 256)
    TW = _row_tile(S, 2048)
    ix, iy, ic = lax.axis_index("x"), lax.axis_index("y"), lax.axis_index("c")
    chip = (2 * ix + iy).astype(jnp.int32)
    chip_arr = chip.reshape(1)
    core_arr = ic.astype(jnp.int32).reshape(1)

    xs, tgt = x[0], loss_target[0]

    tr = lambda a: jnp.transpose(a[0])
    win_t = tr(w_in)
    win_p = jnp.concatenate([win_t[:O_KR + ROPE], win_t[O_KR:O_KR + ROPE], win_t[O_KR + ROPE:]], axis=0).astype(BF16)
    wuq = w_uq[0]
    wuq_p = jnp.concatenate([wuq[:, h, :NOPE] for h in range(HEADS)] + [wuq[:, h, NOPE:] for h in range(HEADS)],
                            axis=1).astype(BF16)
    w_uk_t = jnp.transpose(w_uk[0], (1, 0, 2)).astype(BF16)
    w_uv_t = jnp.transpose(w_uv[0], (1, 0, 2)).astype(BF16)
    w_pool_b = w_pool[0].astype(BF16)
    first = [win_p, wuq_p]
    later = [w_o[0].astype(BF16), tr(w_gate).astype(BF16), tr(w_up).astype(BF16), w_down[0].astype(BF16)]
    placed = _place_shards(chip_arr, first + later)
    a_send, a_recv, a_lands, token = _split_start("first_weights_start", first, placed[:2], 6, _plan_gather_start)
    half = ROPE // 2
    freqs = jnp.power(ROPE_THETA, -jnp.arange(half, dtype=F32) / half)
    cos, sin = _rope_tables(positions.reshape(S, 1), jnp.tile(freqs, 4).reshape(1, 128) + token[0, 0])
    a_send, a_recv, a_lands, token = _split_relay(
        "first_weights_relay", a_send, a_recv, first, a_lands, cos, 6, _plan_gather_landed, _plan_gather_relay)

    ada_cols = w_ada.shape[2]
    b_cols = lax.dynamic_slice(b_ada, (0, chip * ada_cols), (1, ada_cols))
    mod, c_all = _mod_exchange(c, w_ada[0], b_cols + token[0, 0])
    mod6 = mod.reshape(N_MOD, D_MODEL)
    a_lands = _split_wait("first_weights_wait", a_send, a_recv, [], a_lands, mod, _plan_gather_wait)
    w_in_f = a_lands[0]
    w_uq_f = a_lands[1].reshape(Q_LORA, Q_W)
    wg_lands, mod6, w_in_f = lax.optimization_barrier((placed[2:], mod6, w_in_f))
    wg_send, wg_recv, wg_lands, token = _split_start(
        "weights_start", later, wg_lands, 3 * len(later), _plan_gather_start)
    mod6 = mod6 + token[0, 0]

    proj, q, qc, kc, kct = _pre_attention(xs, mod6, g_mix, g_q, g_kv, w_in_f, w_uq_f, w_uk_t, cos, sin, T, TQ)
    o_lat, y_mla, lse_rows = _attention_fwd(qc, kc, kct, w_uv_t, TQ)
    wg_send, wg_recv, wg_lands, token = _split_relay(
        "weights_relay", wg_send, wg_recv, later, wg_lands, y_mla, 3 * len(later), _plan_gather_landed,
        _plan_gather_relay)
    pooled = _pool_forward(proj)
    wg_lands = _split_wait("weights_wait", wg_send, wg_recv, [], wg_lands, pooled, _plan_gather_wait)
    w_o_f = wg_lands[0].reshape(1024, D_MODEL)
    w_gate_f, w_up_f, w_down_f = wg_lands[1], wg_lands[2], wg_lands[3]
    x1, mix, mix_in = _mix_out(y_mla, pooled, w_pool_b, pool_scale, w_o_f, xs, mod6, T)
    gate, up, act, h2, dff, dx2, st_f = _ffn_forward(
        x1, mod6, g_ffn, g_final.reshape(1, D_MODEL), tgt, w_gate_f, w_up_f, w_down_f, T)

    dgate, dup, dx1, st_b = _ffn_backward(dx2, x1, dff, gate, up, mod6, g_ffn, w_gate_f, w_up_f, w_down_f, T)
    steps = S // TW
    chunk_spec = pl.BlockSpec((None, TW, FF_CHUNK), lambda g, i: (g, i, 0))
    wide_spec = pl.BlockSpec((TW, D_MODEL), lambda g, i: (i, 0))
    g_down = _tn_matmul(act, dff, chunk_spec, wide_spec, N_CHIPS, FF_CHUNK, D_MODEL, steps, "grad_w_down")
    g_gate = _tn_matmul(dgate, h2, chunk_spec, wide_spec, N_CHIPS, FF_CHUNK, D_MODEL, steps, "grad_w_gate")
    g_up = _tn_matmul(dup, h2, chunk_spec, wide_spec, N_CHIPS, FF_CHUNK, D_MODEL, steps, "grad_w_up")

    half_shapes = lambda gs: [jax.ShapeDtypeStruct((N_CHIPS, g.shape[1] // 2, g.shape[2]), F32) for g in gs]
    ffn_grads = [g_gate, g_up, g_down]
    f_send, f_recv, f_lands, token = _split_start(
        "ffn_swap_start", ffn_grads, half_shapes(ffn_grads), len(ffn_grads), _plan_swap_start)
    dmix, dpooled, do_lat, delta_rows, g_pool, g_uv_t, st_m = _mix_backward(
        dx1, mix, mod6 + token[0, 0], w_o_f, pooled, w_pool_b, pool_scale, w_uv_t, o_lat, T, TQ)
    g_o = [_tn_matmul(mix_in, dmix, wide_spec, wide_spec, 1, 1024, D_MODEL, steps, "grad_w_o").reshape(N_CHIPS, -1, D_MODEL)]
    o_send, o_recv, o_lands, token = _split_start("w_o_swap_start", g_o, half_shapes(g_o), 1, _plan_swap_start)
    du = _pool_backward(dpooled, token)
    f_got = _split_wait("ffn_swap_wait", f_send, f_recv, ffn_grads, f_lands, du, _plan_swap_wait)
    f_got += _split_wait("w_o_swap_wait", o_send, o_recv, g_o, o_lands, du, _plan_swap_wait)
    far_names = ("w_gate", "w_up", "w_down", "w_o")
    far_grads = ffn_grads + g_o
    f_sums = [_add_my_half(core_arr, a, b, "add_half_" + n) for a, b, n in zip(far_grads, f_got, far_names)]
    f_send, f_recv, f_lands, token = _split_start(
        "far_exchange_start", f_sums, [jax.ShapeDtypeStruct((3,) + s.shape[1:], F32) for s in f_sums],
        3 * len(f_sums), _plan_exchange_start)
    delta_rows = delta_rows + token[0, 0]
    dkc, dqt = _attention_bwd(qc, kc, kct, do_lat, lse_rows, delta_rows, TQ)
    grad_x, dproj, h1, g_uk_t, uq, st_p = _pre_attention_backward(
        xs, dx1, proj, q, dqt, dkc, du, cos, sin, mod6, g_mix, g_q, g_kv, w_in_f, w_uq_f, w_uk_t, T, TQ)
    rows_in = D_MODEL // N_CHIPS
    g_in_p = _tn_matmul(dproj, h1, pl.BlockSpec((TW, PROJ_W), lambda g, i: (i, 0)),
                        pl.BlockSpec((TW, rows_in), lambda g, i: (i, g)), N_CHIPS, PROJ_W, rows_in, steps, "grad_w_in")

    g_in = jnp.concatenate([g_in_p[:, :O_KR + ROPE], g_in_p[:, O_U:]], axis=1)
    g_uq = jnp.concatenate([jnp.concatenate([uq[:, h * NOPE:(h + 1) * NOPE], uq[:, O_QA + h * ROPE:O_QA + (h + 1) * ROPE]],
                                            axis=1) for h in range(HEADS)], axis=1).reshape(N_CHIPS, -1, HEADS * HEAD_QK)
    small = _pack_rows([g_uk_t, g_uv_t, g_pool, st_p[2], st_p[3, :Q_LORA], st_p[4, :KV_LORA], st_m[1, :POOL_W],
                        st_b[2], st_f[0]])
    small = jnp.concatenate([small, jnp.zeros((SMALL_ROWS - small.shape[0], 128), F32)]).reshape(N_CHIPS, -1, 128)
    grads = [g_in, g_uq, small]
    dmod = jnp.concatenate([jnp.stack([st_p[0], st_p[1], st_m[0], st_b[0], st_b[1], st_f[1]]).reshape(48, 128),
                            jnp.zeros((8, 128), F32).at[0, 0].set(st_f[2, 0])])

    names = ("w_in", "w_uq", "small")
    got, dmod_all = _grad_swap_halves(grads, dmod)
    chip_sums = [_add_my_half(core_arr, a, b, "add_half_" + n) for a, b, n in zip(grads, got, names)]
    n_send, n_recv, n_lands, token = _split_start(
        "near_exchange_start", chip_sums, [jax.ShapeDtypeStruct((3,) + s.shape[1:], F32) for s in chip_sums],
        3 * len(chip_sums), _plan_exchange_start)

    f_others = _split_wait("far_exchange_wait", f_send, f_recv, f_sums, f_lands, token, _plan_exchange_wait)
    chip_core = jnp.concatenate([chip_arr, core_arr])
    f_pairs = [_add_chips_into_pair(chip_core, a, b, "add_chips_" + n) for a, b, n in zip(f_sums, f_others, far_names)]
    f_send, f_recv, f_pairs, token = _split_start("far_finish_start", [], f_pairs, len(f_pairs), _plan_finish_start)
    gw_ada, gb_ada = _ada_grads(c_all, dmod_all.reshape(8, -1) + token[0, 0], chip_arr)
    loss = gb_ada[0, N_MOD * D_MODEL]
    gb_ada = gb_ada[:, :N_MOD * D_MODEL]
    f_fulls = _split_wait("far_finish_wait", f_send, f_recv, [], f_pairs, gw_ada, _plan_finish_wait)
    gw_gate, gw_up, gw_down, gw_o = [f.reshape(-1, f.shape[2]) for f in f_fulls]

    untr = lambda a: jnp.transpose(a)[None]
    grad_out, delta_out, newm_out, newv_out = {}, {}, {}, {}

    def adam_sharded(n, w, g2, m, v, transposed):
        view = (lambda a: jnp.transpose(a[0])) if transposed else (lambda a: a[0])
        back = untr if transposed else (lambda a: a[None])
        g_, d_, m_, v_ = _adamw(view(w), g2.reshape(view(w).shape), view(m), view(v), "adamw_" + n)
        grad_out[n], delta_out[n], newm_out[n], newv_out[n] = back(g_), back(d_), back(m_), back(v_)
        return d_

    done = [adam_sharded("w_gate", w_gate, gw_gate, m_w_gate, v_w_gate, True),
            adam_sharded("w_up", w_up, gw_up, m_w_up, v_w_up, True),
            adam_sharded("w_down", w_down, gw_down, m_w_down, v_w_down, False),
            adam_sharded("w_o", w_o, gw_o, m_w_o, v_w_o, False)]
    after_all = jnp.stack([d.reshape(-1)[0] for d in done])

    others = _split_wait("near_exchange_wait", n_send, n_recv, chip_sums, n_lands, after_all, _plan_exchange_wait)
    n_pairs = [_add_chips_into_pair(chip_core, a, b, "add_chips_" + n)
               for a, b, n in zip(chip_sums[:2], others[:2], names[:2])]
    small_grid = _add_chips_into_grid(chip_core, chip_sums[2], others[2], "add_chips_small")
    n_send, n_recv, n_lands, token = _split_start(
        "near_finish_start", [], n_pairs + [small_grid], 2 + len(RELATIONS), _plan_near_finish_start)
    gw_ada, _ = lax.optimization_barrier((gw_ada, token))
    d_ada = adam_sharded("w_ada", w_ada, gw_ada, m_w_ada, v_w_ada, False)
    n_lands = _split_wait("near_finish_wait", n_send, n_recv, [], n_lands, d_ada, _plan_near_finish_wait)
    gw_in, gw_uq = [f.reshape(-1, f.shape[2]) for f in n_lands[:2]]
    small_all = n_lands[2].reshape(SMALL_ROWS * 128)
    adam_sharded("w_in", w_in, gw_in, m_w_in, v_w_in, True)
    adam_sharded("w_uq", w_uq, gw_uq, m_w_uq, v_w_uq, False)

    n_sq = KV_LORA * HEADS * 128
    sizes = [n_sq, n_sq, n_sq, D_MODEL, Q_LORA, KV_LORA, POOL_W, D_MODEL, D_MODEL]
    offs = [0]
    for s_ in sizes:
        offs.append(offs[-1] + s_)
    piece = lambda k: small_all[offs[k]:offs[k + 1]]
    grads_small = {
        "w_uk": jnp.transpose(piece(0).reshape(HEADS, KV_LORA, NOPE), (1, 0, 2)),
        "w_uv": jnp.transpose(piece(1).reshape(HEADS, KV_LORA, 128), (1, 0, 2)),
        "w_pool": piece(2).reshape(4, POOL_GROUP, POOL_GROUP),
        "g_mix": piece(3), "g_q": piece(4), "g_kv": piece(5), "pool_scale": piece(6), "g_ffn": piece(7),
        "g_final": piece(8), "b_ada": gb_ada.reshape(-1),
    }
    weights_small = {"w_uk": w_uk, "w_uv": w_uv, "w_pool": w_pool, "g_mix": g_mix, "g_q": g_q, "g_kv": g_kv,
                     "pool_scale": pool_scale, "g_ffn": g_ffn, "g_final": g_final, "b_ada": b_ada}
    m_small = {"w_uk": m_w_uk, "w_uv": m_w_uv, "w_pool": m_w_pool, "g_mix": m_g_mix, "g_q": m_g_q, "g_kv": m_g_kv,
               "pool_scale": m_pool_scale, "g_ffn": m_g_ffn, "g_final": m_g_final, "b_ada": m_b_ada}
    v_small = {"w_uk": v_w_uk, "w_uv": v_w_uv, "w_pool": v_w_pool, "g_mix": v_g_mix, "g_q": v_g_q, "g_kv": v_g_kv,
               "pool_scale": v_pool_scale, "g_ffn": v_g_ffn, "g_final": v_g_final, "b_ada": v_b_ada}
    pack = lambda d: _pack_rows([d[n] for n in SMALL_NAMES])
    _, d_s, m_s, v_s = _adamw(pack(weights_small), pack(grads_small), pack(m_small), pack(v_small), "adamw_small")

    def unpack(flat2d):
        flat = flat2d.reshape(-1)
        out, o = {}, 0
        for n in SMALL_NAMES:
            size = weights_small[n].size
            out[n] = flat[o:o + size].reshape(weights_small[n].shape)
            o += size
        return out

    delta_s, newm_s, newv_s = unpack(d_s), unpack(m_s), unpack(v_s)

    for n in SMALL_NAMES:
        grad_out[n] = grads_small[n].reshape(weights_small[n].shape)
        delta_out[n], newm_out[n], newv_out[n] = delta_s[n], newm_s[n], newv_s[n]

    order = ("w_ada", "b_ada", "g_mix", "w_in", "g_q", "g_kv", "w_uq", "w_uk", "w_uv", "w_pool", "pool_scale", "w_o",
             "g_ffn", "w_gate", "w_up", "w_down", "g_final")
    return (loss, grad_x.reshape(x.shape), *[grad_out[n] for n in order], *[delta_out[n] for n in order],
            *[newm_out[n] for n in order], *[newv_out[n] for n in order])
```

```python
import functools

import jax
import jax.numpy as jnp
from jax import lax
from jax.experimental import pallas as pl
from jax.experimental.pallas import tpu as pltpu

F32 = jnp.float32
BF16 = jnp.bfloat16

D_MODEL = 1024
HEADS = 4
NOPE = 128
ROPE = 64
HEAD_QK = NOPE + ROPE
Q_LORA = 256
KV_LORA = 128
POOL_W = 512
POOL_WINDOWS = (2, 4, 8, 16)
POOL_GROUP = 128
POOL_PAD = 16
D_FF = 2816
N_CHIPS = 4
FF_CHUNK = D_FF // N_CHIPS
N_MOD = 6
EPS = 1e-6
SM_SCALE = HEAD_QK ** -0.5
ROPE_THETA = 10000.0
QK_PAD = 256
CHUNK = 64
CHUNK_SHIFT = 6

ADAM_LR = 0.001
ADAM_B1 = 0.9
ADAM_B2 = 0.999
ADAM_EPS = 1e-08
ADAM_WD = 0.01
ADAM_STEP = 10

VMEM_LIMIT = 48 * 1024 * 1024
MESH = pl.DeviceIdType.MESH
ANY = pl.BlockSpec(memory_space=pl.ANY)
VMEM_SPEC = pl.BlockSpec(memory_space=pltpu.VMEM)

PROJ_W = 1024
O_CKV = 256
O_KR = 384
O_U = 512
Q_W = 768
O_QA = 512
O_QB = 640


def _params(sem=None, vmem=VMEM_LIMIT):
    kw = dict(vmem_limit_bytes=vmem)
    if sem is not None:
        kw["dimension_semantics"] = sem
    return pltpu.CompilerParams(**kw)


def _dot(a, b):
    return jnp.dot(a.astype(BF16), b.astype(BF16), preferred_element_type=F32)


def _dot_nt(a, b):
    return lax.dot_general(a.astype(BF16), b.astype(BF16), (((1,), (1,)), ((), ())), preferred_element_type=F32)


def _dot_tn(a, b):
    return lax.dot_general(a.astype(BF16), b.astype(BF16), (((0,), (0,)), ((), ())), preferred_element_type=F32)


def _row_tile(rows, target):
    best = rows
    for t in range(8, min(rows, target) + 1, 8):
        if rows % t == 0:
            best = t
    return best if rows % best == 0 and best <= target else rows


def _rms(x):
    r = lax.rsqrt(jnp.mean(x * x, axis=-1, keepdims=True) + EPS)
    return x * r, r


def _rms_bwd(dxh, xh, r):
    return r * (dxh - xh * jnp.mean(dxh * xh, axis=-1, keepdims=True))


def _lane_first_half(shape):
    lane = lax.broadcasted_iota(jnp.int32, shape, 1)
    return (lane & (ROPE - 1)) < (ROPE // 2)


def _rope(a, cos, sin):
    first = _lane_first_half(a.shape)
    up = pltpu.roll(a, 96, 1)
    dn = pltpu.roll(a, 32, 1)
    return a * cos + jnp.where(first, -up, dn) * sin


def _rope_bwd(d, cos, sin):
    first = _lane_first_half(d.shape)
    up = pltpu.roll(d, 96, 1)
    dn = pltpu.roll(d, 32, 1)
    return d * cos + jnp.where(first, up, -dn) * sin


RELATIONS = tuple((dx, dy, dc) for dx in (0, 1) for dy in (0, 1) for dc in (0, 1) if (dx, dy, dc) != (0, 0, 0))
CHIP_RELATIONS = ((1, 0), (0, 1), (1, 1))


def _flip(v, d):
    return 1 - v if d else v


def _place():
    return lax.axis_index("x"), lax.axis_index("y"), lax.axis_index("c")


def _remote(src, dst, send_sem, recv_sem, target):
    return pltpu.make_async_remote_copy(src_ref=src, dst_ref=dst, send_sem=send_sem, recv_sem=recv_sem,
                                        device_id=target, device_id_type=MESH)


def _mod_exchange(c_row, w_ada, b_ada):
    cols = w_ada.shape[1]

    def body(c_ref, w_ref, b_ref, mod_ref, call_ref, part_ref, send1, recv1, loc1, send2, recv2, loc2):
        x, y, c = _place()
        me = 4 * x + 2 * y + c
        own = pltpu.make_async_copy(c_ref, call_ref.at[pl.ds(me, 1)], loc1)
        own.start()
        sends = []
        for k, (dx, dy, dc) in enumerate(RELATIONS):
            cp = _remote(c_ref, call_ref.at[pl.ds(me, 1)], send1.at[k], recv1.at[k],
                         (_flip(x, dx), _flip(y, dy), _flip(c, dc)))
            cp.start()
            sends.append(cp)
        for k, (dx, dy, dc) in enumerate(RELATIONS):
            src = 4 * _flip(x, dx) + 2 * _flip(y, dy) + _flip(c, dc)
            _remote(c_ref, call_ref.at[pl.ds(src, 1)], send1.at[k], recv1.at[k], (x, y, c)).wait_recv()
        own.wait()
        for cp in sends:
            cp.wait_send()
        call = call_ref[...]
        act = call * jax.nn.sigmoid(call)
        part_ref[...] = _dot(act, w_ref[...]) + b_ref[...]
        chip = 2 * x + y
        mine = pltpu.make_async_copy(part_ref.at[pl.ds(me, 1)], mod_ref.at[pl.ds(chip, 1)], loc2)
        mine.start()
        sends = []
        for k, (dx, dy) in enumerate(CHIP_RELATIONS):
            tx, ty = _flip(x, dx), _flip(y, dy)
            tb = 4 * tx + 2 * ty + c
            cp = _remote(part_ref.at[pl.ds(tb, 1)], mod_ref.at[pl.ds(chip, 1)], send2.at[k], recv2.at[k], (tx, ty, c))
            cp.start()
            sends.append(cp)
        for k, (dx, dy) in enumerate(CHIP_RELATIONS):
            src_chip = 2 * _flip(x, dx) + _flip(y, dy)
            _remote(part_ref.at[pl.ds(me, 1)], mod_ref.at[pl.ds(src_chip, 1)], send2.at[k], recv2.at[k],
                    (x, y, c)).wait_recv()
        mine.wait()
        for cp in sends:
            cp.wait_send()

    return pl.pallas_call(
        body, name="mod_exchange",
        out_shape=[jax.ShapeDtypeStruct((N_CHIPS, cols), F32), jax.ShapeDtypeStruct((8, D_MODEL), F32)],
        in_specs=[VMEM_SPEC, VMEM_SPEC, VMEM_SPEC], out_specs=[VMEM_SPEC, VMEM_SPEC],
        scratch_shapes=[pltpu.VMEM((8, cols), F32),
                        pltpu.SemaphoreType.DMA((7,)), pltpu.SemaphoreType.DMA((7,)), pltpu.SemaphoreType.DMA,
                        pltpu.SemaphoreType.DMA((3,)), pltpu.SemaphoreType.DMA((3,)), pltpu.SemaphoreType.DMA],
        compiler_params=_params(),
    )(c_row, w_ada, b_ada)


HBM_SPEC = pl.BlockSpec(memory_space=pltpu.HBM)
SEM_SPEC = pl.BlockSpec(memory_space=pltpu.SEMAPHORE)
DATAFLOW = pltpu.SideEffectType.DATAFLOW_SIDE_EFFECTING


def _in_hbm(a):
    return pltpu.with_memory_space_constraint(a, pltpu.HBM)


def _hbm(*arrays):
    return tuple(_in_hbm(a) for a in arrays)


def _hbm_like(arrays):
    return [pltpu.HBM(a.shape, a.dtype) for a in arrays]


def _split_start(name, srcs, lands, n_remote, plan):
    lands = [lax.empty(a.shape, a.dtype) if isinstance(a, jax.ShapeDtypeStruct) else a for a in lands]
    n, m = len(srcs), len(lands)

    def body(*refs):
        src_refs, land_refs = refs[:n], refs[n:n + m]
        send_sems, recv_sems, token = refs[n + m], refs[n + m + 1], refs[n + 2 * m + 2]
        remote = plan(_place(), src_refs, land_refs)
        assert len(remote) == n_remote
        for i, (s, d, target) in enumerate(remote):
            _remote(s, d, send_sems.at[i], recv_sems.at[i], target).start()
        token[...] = jnp.zeros_like(token)

    res = pl.pallas_call(
        body, name=name,
        out_shape=(pltpu.SemaphoreType.DMA((n_remote,)), pltpu.SemaphoreType.DMA((n_remote,)),
                   *_hbm_like(lands), jax.ShapeDtypeStruct((8, 128), F32)),
        in_specs=[HBM_SPEC] * (n + m),
        out_specs=(SEM_SPEC, SEM_SPEC, *([HBM_SPEC] * m), VMEM_SPEC),
        input_output_aliases={n + i: 2 + i for i in range(m)},
        compiler_params=pltpu.CompilerParams(has_side_effects=DATAFLOW),
    )(*[_in_hbm(a) for a in srcs], *[_in_hbm(a) for a in lands])
    return res[0], res[1], list(res[2:2 + m]), res[2 + m]


def _split_wait(name, send_sems, recv_sems, srcs, lands, after, plan):
    n, m = len(srcs), len(lands)

    def body(*refs):
        src_refs, land_refs = refs[:n], refs[n:n + m]
        send_sems, recv_sems = refs[n + m], refs[n + m + 1]
        place = _place()
        for i, (s, d) in enumerate(plan(place, src_refs, land_refs)):
            cp = _remote(s, d, send_sems.at[i], recv_sems.at[i], place)
            cp.wait_send()
            cp.wait_recv()

    res = pl.pallas_call(
        body, name=name,
        out_shape=tuple(_hbm_like(lands)),
        in_specs=[HBM_SPEC] * (n + m) + [SEM_SPEC, SEM_SPEC, ANY],
        out_specs=tuple([HBM_SPEC] * m),
        input_output_aliases={n + i: i for i in range(m)},
        compiler_params=pltpu.CompilerParams(has_side_effects=DATAFLOW),
    )(*srcs, *lands, send_sems, recv_sems, after)
    return list(res)


def _split_relay(name, send_sems, recv_sems, srcs, lands, after, n_remote, plan_wait, plan_send):
    n, m = len(srcs), len(lands)

    def body(*refs):
        src_refs, land_refs = refs[:n], refs[n:n + m]
        old_send, old_recv = refs[n + m], refs[n + m + 1]
        new_send, new_recv = refs[n + m + 3], refs[n + m + 4]
        token = refs[n + m + 5 + m]
        place = _place()
        for i, (s, d) in enumerate(plan_wait(place, src_refs, land_refs)):
            cp = _remote(s, d, old_send.at[i], old_recv.at[i], place)
            cp.wait_send()
            cp.wait_recv()
        for i, (s, d, target) in enumerate(plan_send(place, land_refs)):
            _remote(s, d, new_send.at[i], new_recv.at[i], target).start()
        token[...] = jnp.zeros_like(token)

    res = pl.pallas_call(
        body, name=name,
        out_shape=(pltpu.SemaphoreType.DMA((n_remote,)), pltpu.SemaphoreType.DMA((n_remote,)),
                   *_hbm_like(lands), jax.ShapeDtypeStruct((8, 128), F32)),
        in_specs=[HBM_SPEC] * (n + m) + [SEM_SPEC, SEM_SPEC, ANY],
        out_specs=(SEM_SPEC, SEM_SPEC, *([HBM_SPEC] * m), VMEM_SPEC),
        input_output_aliases={n + i: 2 + i for i in range(m)},
        compiler_params=pltpu.CompilerParams(has_side_effects=DATAFLOW),
    )(*srcs, *lands, send_sems, recv_sems, after)
    return res[0], res[1], list(res[2:2 + m]), res[2 + m]


def _half(ref, core, axis=0):
    hr = ref.shape[axis] // 2
    return pl.ds(core * hr, hr)


def _plan_gather_start(place, src, land):
    x, y, c = place
    chip = 2 * x + y
    return [(s.at[_half(s, c)], l.at[chip, _half(s, c)], (_flip(x, dx), _flip(y, dy), c))
            for s, l in zip(src, land) for dx, dy in CHIP_RELATIONS]


def _plan_gather_landed(place, src, land):
    x, y, c = place
    return [(s.at[_half(s, c)], l.at[2 * _flip(x, dx) + _flip(y, dy), _half(s, c)])
            for s, l in zip(src, land) for dx, dy in CHIP_RELATIONS]


def _plan_gather_relay(place, land):
    x, y, c = place
    out = []
    for l in land:
        for dx, dy in CHIP_RELATIONS:
            got = l.at[2 * _flip(x, dx) + _flip(y, dy), _half(l, c, 1)]
            out.append((got, got, (x, y, 1 - c)))
    return out


def _plan_gather_wait(place, src, land):
    x, y, c = place
    out = []
    for l in land:
        for dx, dy in CHIP_RELATIONS:
            got = l.at[2 * _flip(x, dx) + _flip(y, dy), _half(l, 1 - c, 1)]
            out.append((got, got))
    return out


def _plan_swap_start(place, src, land):
    x, y, c = place
    return [(s.at[:, _half(s, 1 - c, 1), :], l, (x, y, 1 - c)) for s, l in zip(src, land)]


def _plan_swap_wait(place, src, land):
    return [(s.at[:, _half(s, 0, 1), :], l) for s, l in zip(src, land)]


def _plan_exchange_start(place, src, land):
    x, y, c = place
    remote = []
    for s, l in zip(src, land):
        for k, (dx, dy) in enumerate(CHIP_RELATIONS):
            tx, ty = _flip(x, dx), _flip(y, dy)
            remote.append((s.at[2 * tx + ty], l.at[k], (tx, ty, c)))
    return remote


def _plan_exchange_wait(place, src, land):
    return [(s.at[0], l.at[k]) for s, l in zip(src, land) for k in range(3)]


def _plan_finish_start(place, src, land):
    x, y, c = place
    return [(l.at[c], l.at[c], (x, y, 1 - c)) for l in land]


def _plan_finish_wait(place, src, land):
    x, y, c = place
    return [(l.at[c], l.at[1 - c]) for l in land]


def _plan_near_finish_start(place, src, land):
    x, y, c = place
    mine = land[-1].at[2 * x + y, c]
    return (_plan_finish_start(place, src, land[:-1])
            + [(mine, mine, (_flip(x, dx), _flip(y, dy), _flip(c, dc))) for dx, dy, dc in RELATIONS])


def _plan_near_finish_wait(place, src, land):
    x, y, c = place
    mine = land[-1].at[2 * x + y, c]
    return (_plan_finish_wait(place, src, land[:-1])
            + [(mine, land[-1].at[2 * _flip(x, dx) + _flip(y, dy), _flip(c, dc)]) for dx, dy, dc in RELATIONS])


def _grad_swap_halves(grads, dmod):
    n = len(grads)

    def body(*refs):
        ins, dmod_ref = refs[:n], refs[n]
        outs, dall_ref = refs[n + 1:2 * n + 1], refs[2 * n + 1]
        send_sems, recv_sems, dsend, drecv, dloc = refs[2 * n + 2:]
        x, y, c = _place()
        me = 4 * x + 2 * y + c
        sends = []
        for w in range(n):
            hr = ins[w].shape[1] // 2
            cp = _remote(ins[w].at[:, pl.ds((1 - c) * hr, hr), :], outs[w], send_sems.at[w], recv_sems.at[w],
                         (x, y, 1 - c))
            cp.start()
            sends.append(cp)
        own = pltpu.make_async_copy(dmod_ref, dall_ref.at[me], dloc)
        own.start()
        for k, (dx, dy, dc) in enumerate(RELATIONS):
            cp = _remote(dmod_ref, dall_ref.at[me], dsend.at[k], drecv.at[k],
                         (_flip(x, dx), _flip(y, dy), _flip(c, dc)))
            cp.start()
            sends.append(cp)
        for k, (dx, dy, dc) in enumerate(RELATIONS):
            src = 4 * _flip(x, dx) + 2 * _flip(y, dy) + _flip(c, dc)
            _remote(dmod_ref, dall_ref.at[src], dsend.at[k], drecv.at[k], (x, y, c)).wait_recv()
        for w in range(n):
            _remote(outs[w], outs[w], send_sems.at[w], recv_sems.at[w], (x, y, c)).wait_recv()
        own.wait()
        for cp in sends:
            cp.wait_send()

    out_shape = [pltpu.HBM((N_CHIPS, g.shape[1] // 2, g.shape[2]), F32) for g in grads]
    out_shape.append(pltpu.HBM((8,) + dmod.shape, F32))
    res = pl.pallas_call(
        body, name="grad_swap_halves",
        out_shape=out_shape, in_specs=[ANY] * n + [VMEM_SPEC], out_specs=[ANY] * (n + 1),
        scratch_shapes=[pltpu.SemaphoreType.DMA((n,)), pltpu.SemaphoreType.DMA((n,)),
                        pltpu.SemaphoreType.DMA((7,)), pltpu.SemaphoreType.DMA((7,)), pltpu.SemaphoreType.DMA],
        compiler_params=_params(),
    )(*grads, dmod)
    return res[:n], res[n]


def _add_my_half(core, full, got, name):
    _, hr, cols = got.shape

    def body(core_ref, a_ref, b_ref, o_ref):
        o_ref[...] = a_ref[...] + b_ref[...]

    return pl.pallas_call(
        body, name=name,
        out_shape=pltpu.HBM(got.shape, F32),
        grid_spec=pltpu.PrefetchScalarGridSpec(
            num_scalar_prefetch=1, grid=(N_CHIPS,),
            in_specs=[pl.BlockSpec((None, hr, cols), lambda s, core_ref: (s, core_ref[0], 0)),
                      pl.BlockSpec((None, hr, cols), lambda s, core_ref: (s, 0, 0))],
            out_specs=pl.BlockSpec((None, hr, cols), lambda s, core_ref: (s, 0, 0))),
        compiler_params=_params(("arbitrary",)),
    )(core, *_hbm(full, got))


def _add_chips_into_pair(chip_core, mine, got, name):
    _, hr, cols = mine.shape

    def body(cc_ref, a_ref, b_ref, o_ref):
        o_ref[...] = ((a_ref[...] + b_ref[0]) + b_ref[1]) + b_ref[2]

    return pl.pallas_call(
        body, name=name,
        out_shape=pltpu.HBM((2, hr, cols), F32),
        grid_spec=pltpu.PrefetchScalarGridSpec(
            num_scalar_prefetch=1, grid=(1,),
            in_specs=[pl.BlockSpec((None, hr, cols), lambda s, cc_ref: (cc_ref[0], 0, 0)),
                      pl.BlockSpec((3, hr, cols), lambda s, cc_ref: (0, 0, 0))],
            out_specs=pl.BlockSpec((None, hr, cols), lambda s, cc_ref: (cc_ref[1], 0, 0))),
        compiler_params=_params(("arbitrary",)),
    )(chip_core, *_hbm(mine, got))


def _add_chips_into_grid(chip_core, mine, got, name):
    _, hr, cols = mine.shape

    def body(cc_ref, a_ref, b_ref, o_ref):
        o_ref[...] = ((a_ref[...] + b_ref[0]) + b_ref[1]) + b_ref[2]

    return pl.pallas_call(
        body, name=name,
        out_shape=pltpu.HBM((N_CHIPS, 2, hr, cols), F32),
        grid_spec=pltpu.PrefetchScalarGridSpec(
            num_scalar_prefetch=1, grid=(1,),
            in_specs=[pl.BlockSpec((None, hr, cols), lambda s, cc_ref: (cc_ref[0], 0, 0)),
                      pl.BlockSpec((3, hr, cols), lambda s, cc_ref: (0, 0, 0))],
            out_specs=pl.BlockSpec((None, None, hr, cols), lambda s, cc_ref: (cc_ref[0], cc_ref[1], 0, 0))),
        compiler_params=_params(("arbitrary",)),
    )(chip_core, *_hbm(mine, got))


def _place_shards(chip, shards):
    n = len(shards)

    def body(chip_ref, *refs):
        for w in range(n):
            refs[n + w][...] = refs[w][...]

    return pl.pallas_call(
        body, name="place_shards",
        out_shape=[pltpu.HBM((N_CHIPS,) + s.shape, s.dtype) for s in shards],
        grid_spec=pltpu.PrefetchScalarGridSpec(
            num_scalar_prefetch=1, grid=(1,),
            in_specs=[pl.BlockSpec(s.shape, lambda i, chip_ref: (0, 0)) for s in shards],
            out_specs=[pl.BlockSpec((None,) + s.shape, lambda i, chip_ref: (chip_ref[0], 0, 0)) for s in shards]),
        compiler_params=_params(("arbitrary",)),
    )(chip, *shards)


def _rope_tables(pos_col, freqs):
    S = pos_col.shape[0]
    T = _row_tile(S, 1024)

    def body(p_ref, f_ref, cos_ref, sin_ref):
        ang = p_ref[...].astype(F32) * f_ref[...]
        cos_ref[...] = jnp.cos(ang)
        sin_ref[...] = jnp.sin(ang)

    return pl.pallas_call(
        body, name="rope_tables", grid=(S // T,),
        out_shape=[pltpu.HBM((S, 128), F32)] * 2,
        in_specs=[pl.BlockSpec((T, 1), lambda i: (i, 0)), pl.BlockSpec((1, 128), lambda i: (0, 0))],
        out_specs=[pl.BlockSpec((T, 128), lambda i: (i, 0))] * 2,
        compiler_params=_params(("parallel",)),
    )(*_hbm(pos_col, freqs))


def _full(shape):
    zeros = (0,) * len(shape)
    return pl.BlockSpec(shape, lambda *_: zeros)


def _pre_attention(x, mod6, g_mix, g_q, g_kv, w_in, w_uq, w_uk_t, cos, sin, T, TQ):
    S = x.shape[0]

    def body(x_ref, mod_ref, gm_ref, gq_ref, gkv_ref, win_ref, wuq_ref, wuk_ref, cos_ref, sin_ref,
             proj_ref, q_ref, qc_ref, kc_ref, kct_ref):
        xh, _ = _rms(x_ref[...])
        h1 = ((xh * gm_ref[...]) * (1.0 + mod_ref[1:2, :]) + mod_ref[0:1, :]).astype(BF16)
        rows_in = D_MODEL // N_CHIPS
        proj = _dot_nt(h1[:, 0:rows_in], win_ref[0])
        for j in range(1, N_CHIPS):
            proj = proj + _dot_nt(h1[:, j * rows_in:(j + 1) * rows_in], win_ref[j])
        proj_ref[...] = proj
        cqh, _ = _rms(proj[:, :Q_LORA])
        c_q = cqh * gq_ref[...]
        ckvh, _ = _rms(proj[:, O_CKV:O_KR])
        c_kv = ckvh * gkv_ref[...]
        q = _dot(c_q, wuq_ref[...])
        q_ref[...] = q.astype(BF16)
        cos_t, sin_t = cos_ref[...], sin_ref[...]
        ropes = (_rope(q[:, O_QA:O_QB], cos_t, sin_t), _rope(q[:, O_QB:Q_W], cos_t, sin_t))
        low = lax.broadcasted_iota(jnp.int32, (T, 128), 1) < ROPE
        for h in range(HEADS):
            q_lat = _dot_nt(q[:, h * NOPE:(h + 1) * NOPE], wuk_ref[h])
            keep = low if h % 2 == 0 else jnp.logical_not(low)
            qc_ref[h, :, 0:KV_LORA] = q_lat.astype(BF16)
            qc_ref[h, :, KV_LORA:QK_PAD] = jnp.where(keep, ropes[h // 2], 0.0).astype(BF16)
        k_rope = _rope(proj[:, O_KR:O_U], cos_t, sin_t)
        kc_ref[:, 0:KV_LORA] = c_kv.astype(BF16)
        kc_ref[:, KV_LORA:QK_PAD] = k_rope.astype(BF16)
        lat_t, rope_t = jnp.transpose(c_kv), jnp.transpose(k_rope)
        for s in range(T // TQ):
            kct_ref[s, 0:KV_LORA, :] = lat_t[:, s * TQ:(s + 1) * TQ].astype(BF16)
            kct_ref[s, KV_LORA:QK_PAD, :] = rope_t[:, s * TQ:(s + 1) * TQ].astype(BF16)

    row = lambda w: pl.BlockSpec((T, w), lambda i: (i, 0))
    return pl.pallas_call(
        body, name="pre_attention", grid=(S // T,),
        out_shape=[pltpu.HBM((S, PROJ_W), F32), pltpu.HBM((S, Q_W), BF16), pltpu.HBM((HEADS, S, QK_PAD), BF16),
                   pltpu.HBM((S, QK_PAD), BF16), pltpu.HBM((S // TQ, QK_PAD, TQ), BF16)],
        in_specs=[row(D_MODEL), _full((N_MOD, D_MODEL)), _full((1, D_MODEL)), _full((1, Q_LORA)), _full((1, KV_LORA)),
                  _full((N_CHIPS, PROJ_W, D_MODEL // N_CHIPS)), _full((Q_LORA, Q_W)), _full((HEADS, KV_LORA, NOPE)),
                  row(128), row(128)],
        out_specs=[row(PROJ_W), row(Q_W), pl.BlockSpec((HEADS, T, QK_PAD), lambda i: (0, i, 0)), row(QK_PAD),
                   pl.BlockSpec((T // TQ, QK_PAD, TQ), lambda i: (i, 0, 0))],
        compiler_params=_params(("parallel",)),
    )(*_hbm(x, mod6, g_mix, g_q, g_kv, w_in, w_uq, w_uk_t, cos, sin))


def _diag_mask(TQ, width):
    key = lax.broadcasted_iota(jnp.int32, (TQ, width), 0) >> CHUNK_SHIFT
    qry = (lax.broadcasted_iota(jnp.int32, (TQ, width), 1) & (TQ - 1)) >> CHUNK_SHIFT
    return key <= qry


def _col_to_row(col):
    return jnp.transpose(jnp.broadcast_to(col, (col.shape[0], 128)))[0:1, :]


def _attention_fwd(qc, kc, kct, w_uv_t, TQ):
    S = kc.shape[0]
    R = HEADS * TQ
    nq = S // TQ

    def body(q_ref, k_ref, kt_ref, wuv_ref, o_ref, y_ref, lser_ref, m_s, l_s, acc_s, st_s):
        i = pl.program_id(0)
        q = q_ref[...].reshape(R, QK_PAD)
        m_s[...] = jnp.full((1, R), -jnp.inf, F32)
        l_s[...] = jnp.zeros((1, R), F32)
        acc_s[...] = jnp.zeros((KV_LORA, R), F32)

        def scores(j):
            return _dot_nt(k_ref[pl.ds(pl.multiple_of(j * TQ, TQ), TQ), :], q) * SM_SCALE

        def update(j, st):
            m_old = m_s[...]
            m_new = jnp.maximum(m_old, jnp.max(st, axis=0, keepdims=True))
            pt = jnp.exp(st - m_new)
            alpha = jnp.exp(m_old - m_new)
            l_s[...] = alpha * l_s[...] + jnp.sum(pt, axis=0, keepdims=True)
            acc_s[...] = alpha * acc_s[...] + _dot(kt_ref[j, 0:KV_LORA, :], pt)
            m_s[...] = m_new

        st_s[...] = scores(0)

        def loop(j, carry):
            st = st_s[...]
            st_s[...] = scores(j + 1)
            update(j, st)
            return carry

        lax.fori_loop(0, i, loop, 0)
        update(i, jnp.where(_diag_mask(TQ, R), st_s[...], -jnp.inf))
        l = l_s[...]
        lser_ref[0] = m_s[...] + jnp.log(l)
        o = jnp.transpose(acc_s[...] / l).astype(BF16)
        for h in range(HEADS):
            oh = o[h * TQ:(h + 1) * TQ, :]
            o_ref[h] = oh
            y_ref[:, h * 128:(h + 1) * 128] = _dot(oh, wuv_ref[h]).astype(BF16)

    return pl.pallas_call(
        body, name="attention_fwd", grid=(nq,),
        out_shape=[pltpu.HBM((HEADS, S, KV_LORA), BF16), pltpu.HBM((S, HEADS * 128), BF16),
                   pltpu.HBM((nq, 1, R), F32)],
        in_specs=[pl.BlockSpec((HEADS, TQ, QK_PAD), lambda i: (0, i, 0)), _full((S, QK_PAD)),
                  _full((nq, QK_PAD, TQ)), _full((HEADS, KV_LORA, 128))],
        out_specs=[pl.BlockSpec((HEADS, TQ, KV_LORA), lambda i: (0, i, 0)), pl.BlockSpec((TQ, HEADS * 128), lambda i: (i, 0)),
                   pl.BlockSpec((1, 1, R), lambda i: (i, 0, 0))],
        scratch_shapes=[pltpu.VMEM((1, R), F32), pltpu.VMEM((1, R), F32), pltpu.VMEM((KV_LORA, R), F32),
                        pltpu.VMEM((TQ, R), F32)],
        compiler_params=_params(("parallel",)),
    )(*_hbm(qc, kc, kct, w_uv_t))


def _pool_forward(proj):
    S = proj.shape[0]
    RB = _row_tile(S, 256)

    def body(proj_ref, out_ref, pad_ref, sem):
        cp = pltpu.make_async_copy(proj_ref.at[:, pl.ds(O_U, POOL_W)], pad_ref.at[pl.ds(POOL_PAD, S)], sem)
        cp.start()
        pad_ref[0:POOL_PAD, :] = jnp.zeros((POOL_PAD, POOL_W), F32)
        cp.wait()
        for g, win in enumerate(POOL_WINDOWS):
            cols = slice(g * POOL_GROUP, (g + 1) * POOL_GROUP)
            for r0 in range(0, S, RB):
                u = pad_ref[POOL_PAD + r0:POOL_PAD + r0 + RB, cols]
                acc = u
                for k in range(1, win):
                    acc = acc + pad_ref[POOL_PAD + r0 - k:POOL_PAD + r0 - k + RB, cols]
                if r0 == 0:
                    t1 = (lax.broadcasted_iota(jnp.int32, (RB, POOL_GROUP), 0) + 1).astype(F32)
                    mean = acc / jnp.minimum(t1, float(win))
                else:
                    mean = acc * (1.0 / win)
                out_ref[r0:r0 + RB, cols] = (mean - u).astype(BF16)

    return pl.pallas_call(
        body, name="pool_forward",
        out_shape=jax.ShapeDtypeStruct((S, POOL_W), BF16),
        in_specs=[ANY], out_specs=VMEM_SPEC,
        scratch_shapes=[pltpu.VMEM((S + POOL_PAD, POOL_W), F32), pltpu.SemaphoreType.DMA],
        compiler_params=_params(),
    )(proj)


def _pool_backward(dpooled, after):
    S = dpooled.shape[0]
    RB = _row_tile(S, 256)

    def body(dp_ref, after_ref, out_ref, pad_ref, sem):
        cp = pltpu.make_async_copy(dp_ref, pad_ref.at[pl.ds(0, S)], sem)
        cp.start()
        pad_ref[S:S + POOL_PAD, :] = jnp.zeros((POOL_PAD, POOL_W), F32)
        cp.wait()
        for g, win in enumerate(POOL_WINDOWS):
            cols = slice(g * POOL_GROUP, (g + 1) * POOL_GROUP)
            head = pad_ref[0:POOL_PAD, cols]
            t1 = (lax.broadcasted_iota(jnp.int32, (POOL_PAD, POOL_GROUP), 0) + 1).astype(F32)
            pad_ref[0:POOL_PAD, cols] = head * (float(win) / jnp.minimum(t1, float(win)))
            for r0 in range(0, S, RB):
                acc = pad_ref[r0:r0 + RB, cols]
                for k in range(1, win):
                    acc = acc + pad_ref[r0 + k:r0 + k + RB, cols]
                own = pad_ref[r0:r0 + RB, cols]
                if r0 == 0:
                    own = jnp.concatenate([head, own[POOL_PAD:]], axis=0)
                out_ref[r0:r0 + RB, cols] = (acc * (1.0 / win) - own).astype(BF16)

    return pl.pallas_call(
        body, name="pool_backward",
        out_shape=jax.ShapeDtypeStruct((S, POOL_W), BF16),
        in_specs=[ANY, ANY], out_specs=VMEM_SPEC,
        scratch_shapes=[pltpu.VMEM((S + POOL_PAD, POOL_W), F32), pltpu.SemaphoreType.DMA],
        compiler_params=_params(),
    )(dpooled, after)


def _mix_out(y_mla, pooled, w_pool, pool_scale, w_o, x, mod6, T):
    S = x.shape[0]

    def body(ym_ref, pl_ref, wp_ref, ps_ref, wo_ref, x_ref, mod_ref, x1_ref, mix_ref, mi_ref):
        mi_ref[:, 0:512] = ym_ref[...]
        for g in range(len(POOL_WINDOWS)):
            cols = slice(g * POOL_GROUP, (g + 1) * POOL_GROUP)
            z = _dot(pl_ref[:, cols], wp_ref[g])
            mi_ref[:, 512 + g * POOL_GROUP:512 + (g + 1) * POOL_GROUP] = (z * ps_ref[:, cols]).astype(BF16)
        mix = _dot(mi_ref[...], wo_ref[...])
        mix_ref[...] = mix.astype(BF16)
        x1_ref[...] = x_ref[...] + mod_ref[2:3, :] * mix

    row = lambda w: pl.BlockSpec((T, w), lambda i: (i, 0))
    return pl.pallas_call(
        body, name="mix_out", grid=(S // T,),
        out_shape=[pltpu.HBM((S, D_MODEL), F32), pltpu.HBM((S, D_MODEL), BF16), pltpu.HBM((S, 1024), BF16)],
        in_specs=[row(512), row(POOL_W), _full((4, POOL_GROUP, POOL_GROUP)), _full((1, POOL_W)),
                  _full((1024, D_MODEL)), row(D_MODEL), _full((N_MOD, D_MODEL))],
        out_specs=[row(D_MODEL), row(D_MODEL), row(1024)],
        compiler_params=_params(("parallel",)),
    )(*_hbm(y_mla, pooled, w_pool, pool_scale, w_o, x, mod6))


def _ffn_forward(x1, mod6, g_ffn, g_final, target, w_gate, w_up, w_down, T):
    S = x1.shape[0]

    def body(x1_ref, mod_ref, gf_ref, gl_ref, tgt_ref, wg_ref, wu_ref, wd_ref,
             gate_ref, up_ref, act_ref, h2_ref, dff_ref, dx2_ref, st_ref, acc_s):
        i, j = pl.program_id(0), pl.program_id(1)

        @pl.when(jnp.logical_and(i == 0, j == 0))
        def _():
            st_ref[...] = jnp.zeros_like(st_ref)

        @pl.when(j == 0)
        def _():
            xh, _ = _rms(x1_ref[...])
            h2_ref[...] = ((xh * gf_ref[...]) * (1.0 + mod_ref[4:5, :]) + mod_ref[3:4, :]).astype(BF16)
            acc_s[...] = jnp.zeros_like(acc_s)

        h2 = h2_ref[...]
        gate = _dot_nt(h2, wg_ref[j])
        up = _dot_nt(h2, wu_ref[j])
        gate_ref[...] = gate.astype(BF16)
        up_ref[...] = up.astype(BF16)
        act = (gate * jax.nn.sigmoid(gate) * up).astype(BF16)
        act_ref[...] = act
        acc_s[...] += _dot(act, wd_ref[j])

        @pl.when(j == N_CHIPS - 1)
        def _():
            ff = acc_s[...]
            x2 = x1_ref[...] + mod_ref[5:6, :] * ff
            xh, r3 = _rms(x2)
            err = xh * gl_ref[...] - tgt_ref[...]
            dy = err * (1.0 / D_MODEL)
            dx2 = _rms_bwd(dy * gl_ref[...], xh, r3)
            dx2_ref[...] = dx2
            dff_ref[...] = (dx2 * mod_ref[5:6, :]).astype(BF16)
            st_ref[0:1, :] += jnp.sum(dy * xh, axis=0, keepdims=True)
            st_ref[1:2, :] += jnp.sum(dx2 * ff, axis=0, keepdims=True)
            st_ref[2:3, :] += 0.5 * jnp.sum(err * dy)

    row = pl.BlockSpec((T, D_MODEL), lambda i, j: (i, 0))
    chunk_out = pl.BlockSpec((None, T, FF_CHUNK), lambda i, j: (j, i, 0))
    big = pltpu.HBM((N_CHIPS, S, FF_CHUNK), BF16)
    wide = pltpu.HBM((S, D_MODEL), BF16)
    return pl.pallas_call(
        body, name="ffn_forward", grid=(S // T, N_CHIPS),
        out_shape=[big, big, big, wide, wide, pltpu.HBM((S, D_MODEL), F32), jax.ShapeDtypeStruct((8, D_MODEL), F32)],
        in_specs=[row, _full((N_MOD, D_MODEL)), _full((1, D_MODEL)), _full((1, D_MODEL)), row,
                  VMEM_SPEC, VMEM_SPEC, VMEM_SPEC],
        out_specs=[chunk_out, chunk_out, chunk_out, row, row, row, _full((8, D_MODEL))],
        scratch_shapes=[pltpu.VMEM((T, D_MODEL), F32)],
        compiler_params=_params(("arbitrary", "arbitrary")),
    )(*_hbm(x1, mod6, g_ffn, g_final, target), w_gate, w_up, w_down)


def _ffn_backward(dx2, x1, dff, gate, up, mod6, g_ffn, w_gate, w_up, w_down, T):
    S = x1.shape[0]

    def body(dx2_ref, x1_ref, dff_ref, gate_ref, up_ref, mod_ref, gf_ref, wg_ref, wu_ref, wd_ref,
             dgate_ref, dup_ref, dx1_ref, st_ref, acc_s):
        i, j = pl.program_id(0), pl.program_id(1)

        @pl.when(jnp.logical_and(i == 0, j == 0))
        def _():
            st_ref[...] = jnp.zeros_like(st_ref)

        @pl.when(j == 0)
        def _():
            acc_s[...] = jnp.zeros_like(acc_s)

        for r0 in range(0, T, T // 2):
            rows = slice(r0, r0 + T // 2)
            gate, up = gate_ref[rows, :].astype(F32), up_ref[rows, :].astype(F32)
            sg = jax.nn.sigmoid(gate)
            dact = _dot_nt(dff_ref[rows, :], wd_ref[j])
            dup = (dact * (gate * sg)).astype(BF16)
            dgate = (dact * up * (sg * (1.0 + gate * (1.0 - sg)))).astype(BF16)
            dup_ref[rows, :] = dup
            dgate_ref[rows, :] = dgate
            acc_s[rows, :] += _dot(dgate, wg_ref[j]) + _dot(dup, wu_ref[j])

        @pl.when(j == N_CHIPS - 1)
        def _():
            dh2 = acc_s[...]
            xh, r2 = _rms(x1_ref[...])
            n2 = xh * gf_ref[...]
            st_ref[0:1, :] += jnp.sum(dh2, axis=0, keepdims=True)
            st_ref[1:2, :] += jnp.sum(dh2 * n2, axis=0, keepdims=True)
            dn2 = dh2 * (1.0 + mod_ref[4:5, :])
            st_ref[2:3, :] += jnp.sum(dn2 * xh, axis=0, keepdims=True)
            dx1_ref[...] = _rms_bwd(dn2 * gf_ref[...], xh, r2) + dx2_ref[...]

    row = pl.BlockSpec((T, D_MODEL), lambda i, j: (i, 0))
    chunk = pl.BlockSpec((None, T, FF_CHUNK), lambda i, j: (j, i, 0))
    big = pltpu.HBM((N_CHIPS, S, FF_CHUNK), BF16)
    return pl.pallas_call(
        body, name="ffn_backward", grid=(S // T, N_CHIPS),
        out_shape=[big, big, pltpu.HBM((S, D_MODEL), F32), jax.ShapeDtypeStruct((8, D_MODEL), F32)],
        in_specs=[row, row, row, chunk, chunk, _full((N_MOD, D_MODEL)), _full((1, D_MODEL)),
                  VMEM_SPEC, VMEM_SPEC, VMEM_SPEC],
        out_specs=[chunk, chunk, row, _full((8, D_MODEL))],
        scratch_shapes=[pltpu.VMEM((T, D_MODEL), F32)],
        compiler_params=_params(("arbitrary", "arbitrary")),
    )(*_hbm(dx2, x1, dff, gate, up, mod6, g_ffn), w_gate, w_up, w_down)


def _tn_matmul(a, b, a_spec, b_spec, groups, m, n, steps, name):
    def body(a_ref, b_ref, o_ref):
        @pl.when(pl.program_id(1) == 0)
        def _():
            o_ref[...] = jnp.zeros_like(o_ref)

        o_ref[...] += _dot_tn(a_ref[...], b_ref[...])

    return pl.pallas_call(
        body, name=name, grid=(groups, steps),
        out_shape=pltpu.HBM((groups, m, n), F32),
        in_specs=[a_spec, b_spec],
        out_specs=pl.BlockSpec((None, m, n), lambda g, i: (g, 0, 0)),
        compiler_params=_params(("parallel", "arbitrary")),
    )(*_hbm(a, b))


def _mix_backward(dx1, mix, mod6, w_o, pooled, w_pool, pool_scale, w_uv_t, o_lat, T, TQ):
    S = dx1.shape[0]

    def body(dx1_ref, mix_ref, mod_ref, wo_ref, pl_ref, wp_ref, ps_ref, wuv_ref, o_ref,
             dmix_ref, dp_ref, do_ref, dr_ref, gp_ref, guv_ref, st_ref):
        @pl.when(pl.program_id(0) == 0)
        def _():
            st_ref[...] = jnp.zeros_like(st_ref)
            gp_ref[...] = jnp.zeros_like(gp_ref)
            guv_ref[...] = jnp.zeros_like(guv_ref)

        dx1 = dx1_ref[...]
        st_ref[0:1, :] += jnp.sum(dx1 * mix_ref[...].astype(F32), axis=0, keepdims=True)
        dmix = (dx1 * mod_ref[2:3, :]).astype(BF16)
        dmix_ref[...] = dmix
        dmi = _dot_nt(dmix, wo_ref[...])
        dym = dmi[:, 0:512].astype(BF16)
        for g in range(len(POOL_WINDOWS)):
            cols = slice(g * POOL_GROUP, (g + 1) * POOL_GROUP)
            dyp = dmi[:, 512 + g * POOL_GROUP:512 + (g + 1) * POOL_GROUP]
            pooled_g = pl_ref[:, cols]
            z = _dot(pooled_g, wp_ref[g])
            st_ref[1:2, cols] += jnp.sum(dyp * z, axis=0, keepdims=True)
            dz = (dyp * ps_ref[:, cols]).astype(BF16)
            gp_ref[g] += _dot_tn(pooled_g, dz)
            dp_ref[:, cols] = _dot_nt(dz, wp_ref[g])
        for h in range(HEADS):
            dym_h = dym[:, h * 128:(h + 1) * 128]
            do = _dot_nt(dym_h, wuv_ref[h]).astype(BF16)
            do_ref[h] = do
            o_h = o_ref[h]
            guv_ref[h] += _dot_tn(o_h, dym_h)
            delta = _col_to_row(jnp.sum(do.astype(F32) * o_h.astype(F32), axis=1, keepdims=True))
            for s in range(T // TQ):
                dr_ref[s, :, h * TQ:(h + 1) * TQ] = delta[:, s * TQ:(s + 1) * TQ]

    row = lambda w: pl.BlockSpec((T, w), lambda i: (i, 0))
    heads = pl.BlockSpec((HEADS, T, KV_LORA), lambda i: (0, i, 0))
    square = jax.ShapeDtypeStruct((4, 128, 128), F32)
    return pl.pallas_call(
        body, name="mix_backward", grid=(S // T,),
        out_shape=[pltpu.HBM((S, D_MODEL), BF16), pltpu.HBM((S, POOL_W), F32), pltpu.HBM((HEADS, S, KV_LORA), BF16),
                   pltpu.HBM((S // TQ, 1, HEADS * TQ), F32), square, square, jax.ShapeDtypeStruct((8, D_MODEL), F32)],
        in_specs=[row(D_MODEL), row(D_MODEL), _full((N_MOD, D_MODEL)), _full((1024, D_MODEL)), row(POOL_W),
                  _full((4, POOL_GROUP, POOL_GROUP)), _full((1, POOL_W)), _full((HEADS, KV_LORA, 128)), heads],
        out_specs=[row(D_MODEL), row(POOL_W), heads,
                   pl.BlockSpec((T // TQ, 1, HEADS * TQ), lambda i: (i, 0, 0)), _full((4, 128, 128)),
                   _full((4, 128, 128)), _full((8, D_MODEL))],
        compiler_params=_params(("arbitrary",)),
    )(*_hbm(dx1, mix, mod6, w_o, pooled, w_pool, pool_scale, w_uv_t, o_lat))


def _attention_bwd(qc, kc, kct, do, lse_rows, delta_rows, TQ):
    S = kc.shape[0]
    R = HEADS * TQ
    nq = S // TQ

    def body(k_ref, kt_ref, q_ref, do_ref, lser_ref, dr_ref, dk_ref, dqt_ref, dk_s, dv_s):
        j = pl.program_id(0)

        @pl.when(j == 0)
        def _():
            def zero(i, carry):
                dqt_ref[i] = jnp.zeros((QK_PAD, R), F32)
                return carry
            lax.fori_loop(0, nq, zero, 0)

        k = k_ref[...]
        kt = kt_ref[...]
        v = k[:, :KV_LORA]
        dk_s[...] = jnp.zeros((TQ, QK_PAD), F32)
        dv_s[...] = jnp.zeros((TQ, KV_LORA), F32)

        def step(i, masked):
            rows = pl.ds(pl.multiple_of(i * TQ, TQ), TQ)
            q = q_ref[:, rows, :].reshape(R, QK_PAD)
            do = do_ref[:, rows, :].reshape(R, KV_LORA)
            st = _dot_nt(k, q) * SM_SCALE
            if masked:
                st = jnp.where(_diag_mask(TQ, R), st, -jnp.inf)
            pt = jnp.exp(st - lser_ref[i])
            dv_s[...] += _dot(pt, do)
            dpt = _dot_nt(v, do)
            dst = (pt * (dpt - dr_ref[i])).astype(BF16)
            dk_s[...] += _dot(dst, q)
            dqt_ref[i] += _dot(kt, dst)

        def loop(i, carry):
            step(i, False)
            return carry

        step(j, True)
        lax.fori_loop(j + 1, nq, loop, 0)
        dk = dk_s[...] * SM_SCALE
        dk_ref[:, 0:KV_LORA] = dk[:, 0:KV_LORA] + dv_s[...]
        dk_ref[:, KV_LORA:QK_PAD] = dk[:, KV_LORA:QK_PAD]

    return pl.pallas_call(
        body, name="attention_bwd", grid=(nq,),
        out_shape=[pltpu.HBM((S, QK_PAD), F32), jax.ShapeDtypeStruct((nq, QK_PAD, R), F32)],
        in_specs=[pl.BlockSpec((TQ, QK_PAD), lambda j: (j, 0)), pl.BlockSpec((None, QK_PAD, TQ), lambda j: (j, 0, 0)),
                  VMEM_SPEC, VMEM_SPEC, VMEM_SPEC, VMEM_SPEC],
        out_specs=[pl.BlockSpec((TQ, QK_PAD), lambda j: (j, 0)), VMEM_SPEC],
        scratch_shapes=[pltpu.VMEM((TQ, QK_PAD), F32), pltpu.VMEM((TQ, KV_LORA), F32)],
        compiler_params=_params(("arbitrary",)),
    )(*_hbm(kc, kct), qc, do, lse_rows, delta_rows)


def _pre_attention_backward(x, dx1, proj, q, dqt, dkc, du, cos, sin, mod6, g_mix, g_q, g_kv, w_in, w_uq, w_uk_t, T, TQ):
    S = x.shape[0]

    def body(x_ref, dx1_ref, proj_ref, q_ref, dqt_ref, dkc_ref, du_ref, cos_ref, sin_ref, mod_ref, gm_ref, gq_ref,
             gkv_ref, win_ref, wuq_ref, wuk_ref, gx_ref, dproj_ref, h1_ref, guk_ref, guq_ref, st_ref, dq_ref):
        @pl.when(pl.program_id(0) == 0)
        def _():
            st_ref[...] = jnp.zeros_like(st_ref)
            guk_ref[...] = jnp.zeros_like(guk_ref)
            guq_ref[...] = jnp.zeros_like(guq_ref)

        cos_t, sin_t = cos_ref[...], sin_ref[...]
        low = lax.broadcasted_iota(jnp.int32, (T, 128), 1) < ROPE
        rope_parts = []
        for h in range(HEADS):
            dqc = jnp.concatenate([jnp.transpose(dqt_ref[s, :, h * TQ:(h + 1) * TQ]) for s in range(T // TQ)], axis=0)
            dqc = dqc * SM_SCALE
            dql = dqc[:, 0:KV_LORA].astype(BF16)
            guk_ref[h] += _dot_tn(dql, q_ref[:, h * NOPE:(h + 1) * NOPE])
            dq_ref[:, h * NOPE:(h + 1) * NOPE] = _dot(dql, wuk_ref[h]).astype(BF16)
            rope_parts.append(dqc[:, KV_LORA:QK_PAD])
        for pair in range(2):
            d = jnp.where(low, rope_parts[2 * pair], rope_parts[2 * pair + 1])
            dq_ref[:, O_QA + 128 * pair:O_QA + 128 * (pair + 1)] = _rope_bwd(d, cos_t, sin_t).astype(BF16)
        dq = dq_ref[...]
        dcq = _dot_nt(dq, wuq_ref[...])
        cqh, rq = _rms(proj_ref[:, 0:Q_LORA])
        guq_ref[...] += _dot_tn(cqh * gq_ref[...], dq)
        st_ref[3:4, 0:Q_LORA] += jnp.sum(dcq * cqh, axis=0, keepdims=True)
        dproj_ref[:, 0:Q_LORA] = _rms_bwd(dcq * gq_ref[...], cqh, rq).astype(BF16)
        dckv = dkc_ref[:, 0:KV_LORA]
        ckvh, rkv = _rms(proj_ref[:, O_CKV:O_KR])
        st_ref[4:5, 0:KV_LORA] += jnp.sum(dckv * ckvh, axis=0, keepdims=True)
        dproj_ref[:, O_CKV:O_KR] = _rms_bwd(dckv * gkv_ref[...], ckvh, rkv).astype(BF16)
        dkr = _rope_bwd(dkc_ref[:, KV_LORA:QK_PAD], cos_t, sin_t)
        dkr = jnp.where(low, dkr + pltpu.roll(dkr, ROPE, 1), 0.0)
        dproj_ref[:, O_KR:O_U] = dkr.astype(BF16)
        dproj_ref[:, O_U:PROJ_W] = du_ref[...].astype(BF16)
        dproj = dproj_ref[...]
        dh1 = jnp.concatenate([_dot(dproj, win_ref[j]) for j in range(N_CHIPS)], axis=1)
        xh, r1 = _rms(x_ref[...])
        n1 = xh * gm_ref[...]
        h1_ref[...] = (n1 * (1.0 + mod_ref[1:2, :]) + mod_ref[0:1, :]).astype(BF16)
        st_ref[0:1, :] += jnp.sum(dh1, axis=0, keepdims=True)
        st_ref[1:2, :] += jnp.sum(dh1 * n1, axis=0, keepdims=True)
        dn1 = dh1 * (1.0 + mod_ref[1:2, :])
        st_ref[2:3, :] += jnp.sum(dn1 * xh, axis=0, keepdims=True)
        gx_ref[...] = _rms_bwd(dn1 * gm_ref[...], xh, r1) + dx1_ref[...]

    row = lambda w: pl.BlockSpec((T, w), lambda i: (i, 0))
    return pl.pallas_call(
        body, name="pre_attention_backward", grid=(S // T,),
        out_shape=[jax.ShapeDtypeStruct((S, D_MODEL), F32), pltpu.HBM((S, PROJ_W), BF16),
                   pltpu.HBM((S, D_MODEL), BF16), jax.ShapeDtypeStruct((HEADS, KV_LORA, NOPE), F32),
                   jax.ShapeDtypeStruct((Q_LORA, Q_W), F32), jax.ShapeDtypeStruct((8, D_MODEL), F32)],
        in_specs=[row(D_MODEL), row(D_MODEL), row(PROJ_W), row(HEADS * NOPE),
                  pl.BlockSpec((T // TQ, QK_PAD, HEADS * TQ), lambda i: (i, 0, 0)),
                  row(QK_PAD), row(POOL_W), row(128), row(128), _full((N_MOD, D_MODEL)), _full((1, D_MODEL)),
                  _full((1, Q_LORA)), _full((1, KV_LORA)), _full((N_CHIPS, PROJ_W, D_MODEL // N_CHIPS)),
                  _full((Q_LORA, Q_W)), _full((HEADS, KV_LORA, NOPE))],
        out_specs=[row(D_MODEL), row(PROJ_W), row(D_MODEL), _full((HEADS, KV_LORA, NOPE)), _full((Q_LORA, Q_W)),
                   _full((8, D_MODEL))],
        scratch_shapes=[pltpu.VMEM((T, Q_W), BF16)],
        compiler_params=_params(("arbitrary",)),
    )(*_hbm(x, dx1, proj, q, dqt, dkc, du, cos, sin, mod6, g_mix, g_q, g_kv, w_in, w_uq, w_uk_t))


def _ada_grads(c_all, dmod_all, chip):
    cols = N_MOD * D_MODEL // N_CHIPS
    width = dmod_all.shape[1]

    def body(col_ref, c_ref, dcol_ref, dall_ref, gw_ref, gb_ref):
        call = c_ref[...]
        act = call * jax.nn.sigmoid(call)
        gw_ref[...] = _dot_tn(act, dcol_ref[...])
        d = dall_ref[...]
        acc = d[0:1, :]
        for b in range(1, 8):
            acc = acc + d[b:b + 1, :]
        gb_ref[...] = acc

    return pl.pallas_call(
        body, name="ada_grads",
        out_shape=[jax.ShapeDtypeStruct((D_MODEL, cols), F32), jax.ShapeDtypeStruct((1, width), F32)],
        grid_spec=pltpu.PrefetchScalarGridSpec(
            num_scalar_prefetch=1, grid=(1,),
            in_specs=[pl.BlockSpec((8, D_MODEL), lambda s, col_ref: (0, 0)),
                      pl.BlockSpec((8, cols), lambda s, col_ref: (0, col_ref[0])),
                      pl.BlockSpec((8, width), lambda s, col_ref: (0, 0))],
            out_specs=[pl.BlockSpec((D_MODEL, cols), lambda s, col_ref: (0, 0)),
                       pl.BlockSpec((1, width), lambda s, col_ref: (0, 0))]),
        compiler_params=_params(("arbitrary",)),
    )(chip, *_hbm(c_all, dmod_all, dmod_all))


def _adamw(w, g, m, v, name, g_is_landing_zone=True):
    rows, rest = w.shape[0], w.shape[1:]
    T = _row_tile(rows, 256)

    def body(w_ref, g_ref, m_ref, v_ref, *outs):
        d_ref, nm_ref, nv_ref = outs[-3:]
        g = g_ref[...]
        if g_is_landing_zone:
            outs[0][...] = g
        m2 = ADAM_B1 * m_ref[...] + (1.0 - ADAM_B1) * g
        v2 = ADAM_B2 * v_ref[...] + (1.0 - ADAM_B2) * (g * g)
        m_hat = m2 / (1.0 - ADAM_B1 ** ADAM_STEP)
        v_hat = v2 / (1.0 - ADAM_B2 ** ADAM_STEP)
        d_ref[...] = -ADAM_LR * (m_hat / (jnp.sqrt(v_hat) + ADAM_EPS) + ADAM_WD * w_ref[...])
        nm_ref[...] = m2
        nv_ref[...] = v2

    zeros = (0,) * len(rest)
    spec = pl.BlockSpec((T,) + rest, lambda i: (i,) + zeros)
    n_out = 4 if g_is_landing_zone else 3
    res = pl.pallas_call(
        body, name=name, grid=(rows // T,),
        out_shape=[jax.ShapeDtypeStruct(w.shape, F32)] * n_out,
        in_specs=[spec] * 4, out_specs=[spec] * n_out,
        compiler_params=_params(("parallel",)),
    )(*_hbm(w, g, m, v))
    return res if g_is_landing_zone else [g] + list(res)


SMALL_NAMES = ("w_uk", "w_uv", "w_pool", "g_mix", "g_q", "g_kv", "pool_scale", "g_ffn", "g_final", "b_ada")
SMALL_ROWS = 1664


def _pack_rows(parts):
    flat = jnp.concatenate([p.reshape(-1) for p in parts])
    pad = (-flat.shape[0]) % 128
    if pad:
        flat = jnp.concatenate([flat, jnp.zeros((pad,), F32)])
    return flat.reshape(-1, 128)


def kernel(x, c, positions, w_ada, b_ada, g_mix, w_in, g_q, g_kv, w_uq, w_uk, w_uv, w_pool, pool_scale, w_o, g_ffn, w_gate, w_up, w_down, g_final, loss_target, m_w_ada, m_b_ada, m_g_mix, m_w_in, m_g_q, m_g_kv, m_w_uq, m_w_uk, m_w_uv, m_w_pool, m_pool_scale, m_w_o, m_g_ffn, m_w_gate, m_w_up, m_w_down, m_g_final, v_w_ada, v_b_ada, v_g_mix, v_w_in, v_g_q, v_g_kv, v_w_uq, v_w_uk, v_w_uv, v_w_pool, v_pool_scale, v_w_o, v_g_ffn, v_w_gate, v_w_up, v_w_down, v_g_final):
    S = x.shape[1]
    T = _row_tile(S, 512)
    TQ = _row_tile(S, 256)
    TW = _row_tile(S, 2048)
    ix, iy, ic = lax.axis_index("x"), lax.axis_index("y"), lax.axis_index("c")
    chip = (2 * ix + iy).astype(jnp.int32)
    chip_arr = chip.reshape(1)
    core_arr = ic.astype(jnp.int32).reshape(1)

    xs, tgt = x[0], loss_target[0]

    tr = lambda a: jnp.transpose(a[0])
    win_t = tr(w_in)
    win_p = jnp.concatenate([win_t[:O_KR + ROPE], win_t[O_KR:O_KR + ROPE], win_t[O_KR + ROPE:]], axis=0).astype(BF16)
    wuq = w_uq[0]
    wuq_p = jnp.concatenate([wuq[:, h, :NOPE] for h in range(HEADS)] + [wuq[:, h, NOPE:] for h in range(HEADS)],
                            axis=1).astype(BF16)
    w_uk_t = jnp.transpose(w_uk[0], (1, 0, 2)).astype(BF16)
    w_uv_t = jnp.transpose(w_uv[0], (1, 0, 2)).astype(BF16)
    w_pool_b = w_pool[0].astype(BF16)
    first = [win_p, wuq_p]
    later = [w_o[0].astype(BF16), tr(w_gate).astype(BF16), tr(w_up).astype(BF16), w_down[0].astype(BF16)]
    placed = _place_shards(chip_arr, first + later)
    a_send, a_recv, a_lands, token = _split_start("first_weights_start", first, placed[:2], 6, _plan_gather_start)
    half = ROPE // 2
    freqs = jnp.power(ROPE_THETA, -jnp.arange(half, dtype=F32) / half)
    cos, sin = _rope_tables(positions.reshape(S, 1), jnp.tile(freqs, 4).reshape(1, 128) + token[0, 0])
    a_send, a_recv, a_lands, token = _split_relay(
        "first_weights_relay", a_send, a_recv, first, a_lands, cos, 6, _plan_gather_landed, _plan_gather_relay)

    ada_cols = w_ada.shape[2]
    b_cols = lax.dynamic_slice(b_ada, (0, chip * ada_cols), (1, ada_cols))
    mod, c_all = _mod_exchange(c, w_ada[0], b_cols + token[0, 0])
    mod6 = mod.reshape(N_MOD, D_MODEL)
    a_lands = _split_wait("first_weights_wait", a_send, a_recv, [], a_lands, mod, _plan_gather_wait)
    w_in_f = a_lands[0]
    w_uq_f = a_lands[1].reshape(Q_LORA, Q_W)
    wg_lands, mod6, w_in_f = lax.optimization_barrier((placed[2:], mod6, w_in_f))
    wg_send, wg_recv, wg_lands, token = _split_start(
        "weights_start", later, wg_lands, 3 * len(later), _plan_gather_start)
    mod6 = mod6 + token[0, 0]

    proj, q, qc, kc, kct = _pre_attention(xs, mod6, g_mix, g_q, g_kv, w_in_f, w_uq_f, w_uk_t, cos, sin, T, TQ)
    o_lat, y_mla, lse_rows = _attention_fwd(qc, kc, kct, w_uv_t, TQ)
    wg_send, wg_recv, wg_lands, token = _split_relay(
        "weights_relay", wg_send, wg_recv, later, wg_lands, y_mla, 3 * len(later), _plan_gather_landed,
        _plan_gather_relay)
    pooled = _pool_forward(proj)
    wg_lands = _split_wait("weights_wait", wg_send, wg_recv, [], wg_lands, pooled, _plan_gather_wait)
    w_o_f = wg_lands[0].reshape(1024, D_MODEL)
    w_gate_f, w_up_f, w_down_f = wg_lands[1], wg_lands[2], wg_lands[3]
    x1, mix, mix_in = _mix_out(y_mla, pooled, w_pool_b, pool_scale, w_o_f, xs, mod6, T)
    gate, up, act, h2, dff, dx2, st_f = _ffn_forward(
        x1, mod6, g_ffn, g_final.reshape(1, D_MODEL), tgt, w_gate_f, w_up_f, w_down_f, T)

    dgate, dup, dx1, st_b = _ffn_backward(dx2, x1, dff, gate, up, mod6, g_ffn, w_gate_f, w_up_f, w_down_f, T)
    steps = S // TW
    chunk_spec = pl.BlockSpec((None, TW, FF_CHUNK), lambda g, i: (g, i, 0))
    wide_spec = pl.BlockSpec((TW, D_MODEL), lambda g, i: (i, 0))
    g_down = _tn_matmul(act, dff, chunk_spec, wide_spec, N_CHIPS, FF_CHUNK, D_MODEL, steps, "grad_w_down")
    g_gate = _tn_matmul(dgate, h2, chunk_spec, wide_spec, N_CHIPS, FF_CHUNK, D_MODEL, steps, "grad_w_gate")
    g_up = _tn_matmul(dup, h2, chunk_spec, wide_spec, N_CHIPS, FF_CHUNK, D_MODEL, steps, "grad_w_up")

    half_shapes = lambda gs: [jax.ShapeDtypeStruct((N_CHIPS, g.shape[1] // 2, g.shape[2]), F32) for g in gs]
    ffn_grads = [g_gate, g_up, g_down]
    f_send, f_recv, f_lands, token = _split_start(
        "ffn_swap_start", ffn_grads, half_shapes(ffn_grads), len(ffn_grads), _plan_swap_start)
    dmix, dpooled, do_lat, delta_rows, g_pool, g_uv_t, st_m = _mix_backward(
        dx1, mix, mod6 + token[0, 0], w_o_f, pooled, w_pool_b, pool_scale, w_uv_t, o_lat, T, TQ)
    g_o = [_tn_matmul(mix_in, dmix, wide_spec, wide_spec, 1, 1024, D_MODEL, steps, "grad_w_o").reshape(N_CHIPS, -1, D_MODEL)]
    o_send, o_recv, o_lands, token = _split_start("w_o_swap_start", g_o, half_shapes(g_o), 1, _plan_swap_start)
    du = _pool_backward(dpooled, token)
    f_got = _split_wait("ffn_swap_wait", f_send, f_recv, ffn_grads, f_lands, du, _plan_swap_wait)
    f_got += _split_wait("w_o_swap_wait", o_send, o_recv, g_o, o_lands, du, _plan_swap_wait)
    far_names = ("w_gate", "w_up", "w_down", "w_o")
    far_grads = ffn_grads + g_o
    f_sums = [_add_my_half(core_arr, a, b, "add_half_" + n) for a, b, n in zip(far_grads, f_got, far_names)]
    f_send, f_recv, f_lands, token = _split_start(
        "far_exchange_start", f_sums, [jax.ShapeDtypeStruct((3,) + s.shape[1:], F32) for s in f_sums],
        3 * len(f_sums), _plan_exchange_start)
    delta_rows = delta_rows + token[0, 0]
    dkc, dqt = _attention_bwd(qc, kc, kct, do_lat, lse_rows, delta_rows, TQ)
    grad_x, dproj, h1, g_uk_t, uq, st_p = _pre_attention_backward(
        xs, dx1, proj, q, dqt, dkc, du, cos, sin, mod6, g_mix, g_q, g_kv, w_in_f, w_uq_f, w_uk_t, T, TQ)
    rows_in = D_MODEL // N_CHIPS
    g_in_p = _tn_matmul(dproj, h1, pl.BlockSpec((TW, PROJ_W), lambda g, i: (i, 0)),
                        pl.BlockSpec((TW, rows_in), lambda g, i: (i, g)), N_CHIPS, PROJ_W, rows_in, steps, "grad_w_in")

    g_in = jnp.concatenate([g_in_p[:, :O_KR + ROPE], g_in_p[:, O_U:]], axis=1)
    g_uq = jnp.concatenate([jnp.concatenate([uq[:, h * NOPE:(h + 1) * NOPE], uq[:, O_QA + h * ROPE:O_QA + (h + 1) * ROPE]],
                                            axis=1) for h in range(HEADS)], axis=1).reshape(N_CHIPS, -1, HEADS * HEAD_QK)
    small = _pack_rows([g_uk_t, g_uv_t, g_pool, st_p[2], st_p[3, :Q_LORA], st_p[4, :KV_LORA], st_m[1, :POOL_W],
                        st_b[2], st_f[0]])
    small = jnp.concatenate([small, jnp.zeros((SMALL_ROWS - small.shape[0], 128), F32)]).reshape(N_CHIPS, -1, 128)
    grads = [g_in, g_uq, small]
    dmod = jnp.concatenate([jnp.stack([st_p[0], st_p[1], st_m[0], st_b[0], st_b[1], st_f[1]]).reshape(48, 128),
                            jnp.zeros((8, 128), F32).at[0, 0].set(st_f[2, 0])])

    names = ("w_in", "w_uq", "small")
    got, dmod_all = _grad_swap_halves(grads, dmod)
    chip_sums = [_add_my_half(core_arr, a, b, "add_half_" + n) for a, b, n in zip(grads, got, names)]
    n_send, n_recv, n_lands, token = _split_start(
        "near_exchange_start", chip_sums, [jax.ShapeDtypeStruct((3,) + s.shape[1:], F32) for s in chip_sums],
        3 * len(chip_sums), _plan_exchange_start)

    f_others = _split_wait("far_exchange_wait", f_send, f_recv, f_sums, f_lands, token, _plan_exchange_wait)
    chip_core = jnp.concatenate([chip_arr, core_arr])
    f_pairs = [_add_chips_into_pair(chip_core, a, b, "add_chips_" + n) for a, b, n in zip(f_sums, f_others, far_names)]
    f_send, f_recv, f_pairs, token = _split_start("far_finish_start", [], f_pairs, len(f_pairs), _plan_finish_start)
    gw_ada, gb_ada = _ada_grads(c_all, dmod_all.reshape(8, -1) + token[0, 0], chip_arr)
    loss = gb_ada[0, N_MOD * D_MODEL]
    gb_ada = gb_ada[:, :N_MOD * D_MODEL]
    f_fulls = _split_wait("far_finish_wait", f_send, f_recv, [], f_pairs, gw_ada, _plan_finish_wait)
    gw_gate, gw_up, gw_down, gw_o = [f.reshape(-1, f.shape[2]) for f in f_fulls]

    untr = lambda a: jnp.transpose(a)[None]
    grad_out, delta_out, newm_out, newv_out = {}, {}, {}, {}

    def adam_sharded(n, w, g2, m, v, transposed, landed=True):
        view = (lambda a: jnp.transpose(a[0])) if transposed else (lambda a: a[0])
        back = untr if transposed else (lambda a: a[None])
        g_, d_, m_, v_ = _adamw(view(w), g2.reshape(view(w).shape), view(m), view(v), "adamw_" + n, landed)
        grad_out[n], delta_out[n], newm_out[n], newv_out[n] = back(g_), back(d_), back(m_), back(v_)
        return d_

    done = [adam_sharded("w_gate", w_gate, gw_gate, m_w_gate, v_w_gate, True),
            adam_sharded("w_up", w_up, gw_up, m_w_up, v_w_up, True),
            adam_sharded("w_down", w_down, gw_down, m_w_down, v_w_down, False),
            adam_sharded("w_o", w_o, gw_o, m_w_o, v_w_o, False)]
    after_all = jnp.stack([d[0, 0] for d in done])

    others = _split_wait("near_exchange_wait", n_send, n_recv, chip_sums, n_lands, after_all, _plan_exchange_wait)
    n_pairs = [_add_chips_into_pair(chip_core, a, b, "add_chips_" + n)
               for a, b, n in zip(chip_sums[:2], others[:2], names[:2])]
    small_grid = _add_chips_into_grid(chip_core, chip_sums[2], others[2], "add_chips_small")
    n_send, n_recv, n_lands, token = _split_start(
        "near_finish_start", [], n_pairs + [small_grid], 2 + len(RELATIONS), _plan_near_finish_start)
    gw_ada, _ = lax.optimization_barrier((gw_ada, token))
    d_ada = adam_sharded("w_ada", w_ada, gw_ada, m_w_ada, v_w_ada, False, landed=False)
    n_lands = _split_wait("near_finish_wait", n_send, n_recv, [], n_lands, d_ada, _plan_near_finish_wait)
    gw_in, gw_uq = [f.reshape(-1, f.shape[2]) for f in n_lands[:2]]
    small_all = n_lands[2].reshape(SMALL_ROWS * 128)
    adam_sharded("w_in", w_in, gw_in, m_w_in, v_w_in, True)
    adam_sharded("w_uq", w_uq, gw_uq, m_w_uq, v_w_uq, False)

    n_sq = KV_LORA * HEADS * 128
    sizes = [n_sq, n_sq, n_sq, D_MODEL, Q_LORA, KV_LORA, POOL_W, D_MODEL, D_MODEL]
    offs = [0]
    for s_ in sizes:
        offs.append(offs[-1] + s_)
    piece = lambda k: small_all[offs[k]:offs[k + 1]]
    grads_small = {
        "w_uk": jnp.transpose(piece(0).reshape(HEADS, KV_LORA, NOPE), (1, 0, 2)),
        "w_uv": jnp.transpose(piece(1).reshape(HEADS, KV_LORA, 128), (1, 0, 2)),
        "w_pool": piece(2).reshape(4, POOL_GROUP, POOL_GROUP),
        "g_mix": piece(3), "g_q": piece(4), "g_kv": piece(5), "pool_scale": piece(6), "g_ffn": piece(7),
        "g_final": piece(8), "b_ada": gb_ada.reshape(-1),
    }
    weights_small = {"w_uk": w_uk, "w_uv": w_uv, "w_pool": w_pool, "g_mix": g_mix, "g_q": g_q, "g_kv": g_kv,
                     "pool_scale": pool_scale, "g_ffn": g_ffn, "g_final": g_final, "b_ada": b_ada}
    m_small = {"w_uk": m_w_uk, "w_uv": m_w_uv, "w_pool": m_w_pool, "g_mix": m_g_mix, "g_q": m_g_q, "g_kv": m_g_kv,
               "pool_scale": m_pool_scale, "g_ffn": m_g_ffn, "g_final": m_g_final, "b_ada": m_b_ada}
    v_small = {"w_uk": v_w_uk, "w_uv": v_w_uv, "w_pool": v_w_pool, "g_mix": v_g_mix, "g_q": v_g_q, "g_kv": v_g_kv,
               "pool_scale": v_pool_scale, "g_ffn": v_g_ffn, "g_final": v_g_final, "b_ada": v_b_ada}
    pack = lambda d: _pack_rows([d[n] for n in SMALL_NAMES])
    _, d_s, m_s, v_s = _adamw(pack(weights_small), pack(grads_small), pack(m_small), pack(v_small), "adamw_small",
                              g_is_landing_zone=False)

    def unpack(flat2d):
        flat = flat2d.reshape(-1)
        out, o = {}, 0
        for n in SMALL_NAMES:
            size = weights_small[n].size
            out[n] = flat[o:o + size].reshape(weights_small[n].shape)
            o += size
        return out

    delta_s, newm_s, newv_s = unpack(d_s), unpack(m_s), unpack(v_s)

    for n in SMALL_NAMES:
        grad_out[n] = grads_small[n].reshape(weights_small[n].shape)
        delta_out[n], newm_out[n], newv_out[n] = delta_s[n], newm_s[n], newv_s[n]

    order = ("w_ada", "b_ada", "g_mix", "w_in", "g_q", "g_kv", "w_uq", "w_uk", "w_uv", "w_pool", "pool_scale", "w_o",
             "g_ffn", "w_gate", "w_up", "w_down", "g_final")
    return (loss, grad_x.reshape(x.shape), *[grad_out[n] for n in order], *[delta_out[n] for n in order],
            *[newm_out[n] for n in order], *[newv_out[n] for n in order])
```

```python
import functools

import jax
import jax.numpy as jnp
from jax import lax
from jax.experimental import pallas as pl
from jax.experimental.pallas import tpu as pltpu

F32 = jnp.float32
BF16 = jnp.bfloat16

D_MODEL = 1024
HEADS = 4
NOPE = 128
ROPE = 64
HEAD_QK = NOPE + ROPE
Q_LORA = 256
KV_LORA = 128
POOL_W = 512
POOL_WINDOWS = (2, 4, 8, 16)
POOL_GROUP = 128
POOL_PAD = 16
D_FF = 2816
N_CHIPS = 4
FF_CHUNK = D_FF // N_CHIPS
N_MOD = 6
EPS = 1e-6
SM_SCALE = HEAD_QK ** -0.5
ROPE_THETA = 10000.0
QK_PAD = 256
CHUNK = 64
CHUNK_SHIFT = 6

ADAM_LR = 0.001
ADAM_B1 = 0.9
ADAM_B2 = 0.999
ADAM_EPS = 1e-08
ADAM_WD = 0.01
ADAM_STEP = 10

VMEM_LIMIT = 48 * 1024 * 1024
MESH = pl.DeviceIdType.MESH
ANY = pl.BlockSpec(memory_space=pl.ANY)
VMEM_SPEC = pl.BlockSpec(memory_space=pltpu.VMEM)

PROJ_W = 1024
O_CKV = 256
O_KR = 384
O_U = 512
Q_W = 768
O_QA = 512
O_QB = 640


def _params(sem=None, vmem=VMEM_LIMIT):
    kw = dict(vmem_limit_bytes=vmem)
    if sem is not None:
        kw["dimension_semantics"] = sem
    return pltpu.CompilerParams(**kw)


def _dot(a, b):
    return jnp.dot(a.astype(BF16), b.astype(BF16), preferred_element_type=F32)


def _dot_nt(a, b):
    return lax.dot_general(a.astype(BF16), b.astype(BF16), (((1,), (1,)), ((), ())), preferred_element_type=F32)


def _dot_tn(a, b):
    return lax.dot_general(a.astype(BF16), b.astype(BF16), (((0,), (0,)), ((), ())), preferred_element_type=F32)


def _row_tile(rows, target):
    best = rows
    for t in range(8, min(rows, target) + 1, 8):
        if rows % t == 0:
            best = t
    return best if rows % best == 0 and best <= target else rows


def _rms(x):
    r = lax.rsqrt(jnp.mean(x * x, axis=-1, keepdims=True) + EPS)
    return x * r, r


def _rms_bwd(dxh, xh, r):
    return r * (dxh - xh * jnp.mean(dxh * xh, axis=-1, keepdims=True))


def _lane_first_half(shape):
    lane = lax.broadcasted_iota(jnp.int32, shape, 1)
    return (lane & (ROPE - 1)) < (ROPE // 2)


def _rope(a, cos, sin):
    first = _lane_first_half(a.shape)
    up = pltpu.roll(a, 96, 1)
    dn = pltpu.roll(a, 32, 1)
    return a * cos + jnp.where(first, -up, dn) * sin


def _rope_bwd(d, cos, sin):
    first = _lane_first_half(d.shape)
    up = pltpu.roll(d, 96, 1)
    dn = pltpu.roll(d, 32, 1)
    return d * cos + jnp.where(first, up, -dn) * sin


RELATIONS = tuple((dx, dy, dc) for dx in (0, 1) for dy in (0, 1) for dc in (0, 1) if (dx, dy, dc) != (0, 0, 0))
CHIP_RELATIONS = ((1, 0), (0, 1), (1, 1))


def _flip(v, d):
    return 1 - v if d else v


def _place():
    return lax.axis_index("x"), lax.axis_index("y"), lax.axis_index("c")


def _remote(src, dst, send_sem, recv_sem, target):
    return pltpu.make_async_remote_copy(src_ref=src, dst_ref=dst, send_sem=send_sem, recv_sem=recv_sem,
                                        device_id=target, device_id_type=MESH)


def _mod_exchange(c_row, w_ada, b_ada):
    cols = w_ada.shape[1]

    def body(c_ref, w_ref, b_ref, mod_ref, call_ref, part_ref, send1, recv1, loc1, send2, recv2, loc2):
        x, y, c = _place()
        me = 4 * x + 2 * y + c
        own = pltpu.make_async_copy(c_ref, call_ref.at[pl.ds(me, 1)], loc1)
        own.start()
        sends = []
        for k, (dx, dy, dc) in enumerate(RELATIONS):
            cp = _remote(c_ref, call_ref.at[pl.ds(me, 1)], send1.at[k], recv1.at[k],
                         (_flip(x, dx), _flip(y, dy), _flip(c, dc)))
            cp.start()
            sends.append(cp)
        for k, (dx, dy, dc) in enumerate(RELATIONS):
            src = 4 * _flip(x, dx) + 2 * _flip(y, dy) + _flip(c, dc)
            _remote(c_ref, call_ref.at[pl.ds(src, 1)], send1.at[k], recv1.at[k], (x, y, c)).wait_recv()
        own.wait()
        for cp in sends:
            cp.wait_send()
        call = call_ref[...]
        act = call * jax.nn.sigmoid(call)
        part_ref[...] = _dot(act, w_ref[...]) + b_ref[...]
        chip = 2 * x + y
        mine = pltpu.make_async_copy(part_ref.at[pl.ds(me, 1)], mod_ref.at[pl.ds(chip, 1)], loc2)
        mine.start()
        sends = []
        for k, (dx, dy) in enumerate(CHIP_RELATIONS):
            tx, ty = _flip(x, dx), _flip(y, dy)
            tb = 4 * tx + 2 * ty + c
            cp = _remote(part_ref.at[pl.ds(tb, 1)], mod_ref.at[pl.ds(chip, 1)], send2.at[k], recv2.at[k], (tx, ty, c))
            cp.start()
            sends.append(cp)
        for k, (dx, dy) in enumerate(CHIP_RELATIONS):
            src_chip = 2 * _flip(x, dx) + _flip(y, dy)
            _remote(part_ref.at[pl.ds(me, 1)], mod_ref.at[pl.ds(src_chip, 1)], send2.at[k], recv2.at[k],
                    (x, y, c)).wait_recv()
        mine.wait()
        for cp in sends:
            cp.wait_send()

    return pl.pallas_call(
        body, name="mod_exchange",
        out_shape=[jax.ShapeDtypeStruct((N_CHIPS, cols), F32), jax.ShapeDtypeStruct((8, D_MODEL), F32)],
        in_specs=[VMEM_SPEC, VMEM_SPEC, VMEM_SPEC], out_specs=[VMEM_SPEC, VMEM_SPEC],
        scratch_shapes=[pltpu.VMEM((8, cols), F32),
                        pltpu.SemaphoreType.DMA((7,)), pltpu.SemaphoreType.DMA((7,)), pltpu.SemaphoreType.DMA,
                        pltpu.SemaphoreType.DMA((3,)), pltpu.SemaphoreType.DMA((3,)), pltpu.SemaphoreType.DMA],
        compiler_params=_params(),
    )(c_row, w_ada, b_ada)


HBM_SPEC = pl.BlockSpec(memory_space=pltpu.HBM)
SEM_SPEC = pl.BlockSpec(memory_space=pltpu.SEMAPHORE)
DATAFLOW = pltpu.SideEffectType.DATAFLOW_SIDE_EFFECTING


def _in_hbm(a):
    return pltpu.with_memory_space_constraint(a, pltpu.HBM)


def _hbm(*arrays):
    return tuple(_in_hbm(a) for a in arrays)


def _hbm_like(arrays):
    return [pltpu.HBM(a.shape, a.dtype) for a in arrays]


def _split_start(name, srcs, lands, n_remote, plan):
    lands = [lax.empty(a.shape, a.dtype) if isinstance(a, jax.ShapeDtypeStruct) else a for a in lands]
    n, m = len(srcs), len(lands)

    def body(*refs):
        src_refs, land_refs = refs[:n], refs[n:n + m]
        send_sems, recv_sems, token = refs[n + m], refs[n + m + 1], refs[n + 2 * m + 2]
        remote = plan(_place(), src_refs, land_refs)
        assert len(remote) == n_remote
        for i, (s, d, target) in enumerate(remote):
            _remote(s, d, send_sems.at[i], recv_sems.at[i], target).start()
        token[...] = jnp.zeros_like(token)

    res = pl.pallas_call(
        body, name=name,
        out_shape=(pltpu.SemaphoreType.DMA((n_remote,)), pltpu.SemaphoreType.DMA((n_remote,)),
                   *_hbm_like(lands), jax.ShapeDtypeStruct((8, 128), F32)),
        in_specs=[HBM_SPEC] * (n + m),
        out_specs=(SEM_SPEC, SEM_SPEC, *([HBM_SPEC] * m), VMEM_SPEC),
        input_output_aliases={n + i: 2 + i for i in range(m)},
        compiler_params=pltpu.CompilerParams(has_side_effects=DATAFLOW),
    )(*[_in_hbm(a) for a in srcs], *[_in_hbm(a) for a in lands])
    return res[0], res[1], list(res[2:2 + m]), res[2 + m]


def _split_wait(name, send_sems, recv_sems, srcs, lands, after, plan):
    n, m = len(srcs), len(lands)

    def body(*refs):
        src_refs, land_refs = refs[:n], refs[n:n + m]
        send_sems, recv_sems = refs[n + m], refs[n + m + 1]
        place = _place()
        for i, (s, d) in enumerate(plan(place, src_refs, land_refs)):
            cp = _remote(s, d, send_sems.at[i], recv_sems.at[i], place)
            cp.wait_send()
            cp.wait_recv()

    res = pl.pallas_call(
        body, name=name,
        out_shape=tuple(_hbm_like(lands)),
        in_specs=[HBM_SPEC] * (n + m) + [SEM_SPEC, SEM_SPEC, ANY],
        out_specs=tuple([HBM_SPEC] * m),
        input_output_aliases={n + i: i for i in range(m)},
        compiler_params=pltpu.CompilerParams(has_side_effects=DATAFLOW),
    )(*srcs, *lands, send_sems, recv_sems, after)
    return list(res)


def _split_relay(name, send_sems, recv_sems, srcs, lands, after, n_remote, plan_wait, plan_send):
    n, m = len(srcs), len(lands)

    def body(*refs):
        src_refs, land_refs = refs[:n], refs[n:n + m]
        old_send, old_recv = refs[n + m], refs[n + m + 1]
        new_send, new_recv = refs[n + m + 3], refs[n + m + 4]
        token = refs[n + m + 5 + m]
        place = _place()
        for i, (s, d) in enumerate(plan_wait(place, src_refs, land_refs)):
            cp = _remote(s, d, old_send.at[i], old_recv.at[i], place)
            cp.wait_send()
            cp.wait_recv()
        for i, (s, d, target) in enumerate(plan_send(place, land_refs)):
            _remote(s, d, new_send.at[i], new_recv.at[i], target).start()
        token[...] = jnp.zeros_like(token)

    res = pl.pallas_call(
        body, name=name,
        out_shape=(pltpu.SemaphoreType.DMA((n_remote,)), pltpu.SemaphoreType.DMA((n_remote,)),
                   *_hbm_like(lands), jax.ShapeDtypeStruct((8, 128), F32)),
        in_specs=[HBM_SPEC] * (n + m) + [SEM_SPEC, SEM_SPEC, ANY],
        out_specs=(SEM_SPEC, SEM_SPEC, *([HBM_SPEC] * m), VMEM_SPEC),
        input_output_aliases={n + i: 2 + i for i in range(m)},
        compiler_params=pltpu.CompilerParams(has_side_effects=DATAFLOW),
    )(*srcs, *lands, send_sems, recv_sems, after)
    return res[0], res[1], list(res[2:2 + m]), res[2 + m]


def _half(ref, core, axis=0):
    hr = ref.shape[axis] // 2
    return pl.ds(core * hr, hr)


def _plan_gather_start(place, src, land):
    x, y, c = place
    chip = 2 * x + y
    return [(s.at[_half(s, c)], l.at[chip, _half(s, c)], (_flip(x, dx), _flip(y, dy), c))
            for s, l in zip(src, land) for dx, dy in CHIP_RELATIONS]


def _plan_gather_landed(place, src, land):
    x, y, c = place
    return [(s.at[_half(s, c)], l.at[2 * _flip(x, dx) + _flip(y, dy), _half(s, c)])
            for s, l in zip(src, land) for dx, dy in CHIP_RELATIONS]


def _plan_gather_relay(place, land):
    x, y, c = place
    out = []
    for l in land:
        for dx, dy in CHIP_RELATIONS:
            got = l.at[2 * _flip(x, dx) + _flip(y, dy), _half(l, c, 1)]
            out.append((got, got, (x, y, 1 - c)))
    return out


def _plan_gather_wait(place, src, land):
    x, y, c = place
    out = []
    for l in land:
        for dx, dy in CHIP_RELATIONS:
            got = l.at[2 * _flip(x, dx) + _flip(y, dy), _half(l, 1 - c, 1)]
            out.append((got, got))
    return out


def _plan_swap_start(place, src, land):
    x, y, c = place
    return [(s.at[:, _half(s, 1 - c, 1), :], l, (x, y, 1 - c)) for s, l in zip(src, land)]


def _plan_swap_wait(place, src, land):
    return [(s.at[:, _half(s, 0, 1), :], l) for s, l in zip(src, land)]


def _plan_exchange_start(place, src, land):
    x, y, c = place
    remote = []
    for s, l in zip(src, land):
        for k, (dx, dy) in enumerate(CHIP_RELATIONS):
            tx, ty = _flip(x, dx), _flip(y, dy)
            remote.append((s.at[2 * tx + ty], l.at[k], (tx, ty, c)))
    return remote


def _plan_exchange_wait(place, src, land):
    return [(s.at[0], l.at[k]) for s, l in zip(src, land) for k in range(3)]


def _plan_finish_start(place, src, land):
    x, y, c = place
    return [(l.at[c], l.at[c], (x, y, 1 - c)) for l in land]


def _plan_finish_wait(place, src, land):
    x, y, c = place
    return [(l.at[c], l.at[1 - c]) for l in land]


def _plan_near_finish_start(place, src, land):
    x, y, c = place
    mine = land[-1].at[2 * x + y, c]
    return (_plan_finish_start(place, src, land[:-1])
            + [(mine, mine, (_flip(x, dx), _flip(y, dy), _flip(c, dc))) for dx, dy, dc in RELATIONS])


def _plan_near_finish_wait(place, src, land):
    x, y, c = place
    mine = land[-1].at[2 * x + y, c]
    return (_plan_finish_wait(place, src, land[:-1])
            + [(mine, land[-1].at[2 * _flip(x, dx) + _flip(y, dy), _flip(c, dc)]) for dx, dy, dc in RELATIONS])


def _grad_swap_halves(grads, dmod):
    n = len(grads)

    def body(*refs):
        ins, dmod_ref = refs[:n], refs[n]
        outs, dall_ref = refs[n + 1:2 * n + 1], refs[2 * n + 1]
        send_sems, recv_sems, dsend, drecv, dloc = refs[2 * n + 2:]
        x, y, c = _place()
        me = 4 * x + 2 * y + c
        sends = []
        for w in range(n):
            hr = ins[w].shape[1] // 2
            cp = _remote(ins[w].at[:, pl.ds((1 - c) * hr, hr), :], outs[w], send_sems.at[w], recv_sems.at[w],
                         (x, y, 1 - c))
            cp.start()
            sends.append(cp)
        own = pltpu.make_async_copy(dmod_ref, dall_ref.at[me], dloc)
        own.start()
        for k, (dx, dy, dc) in enumerate(RELATIONS):
            cp = _remote(dmod_ref, dall_ref.at[me], dsend.at[k], drecv.at[k],
                         (_flip(x, dx), _flip(y, dy), _flip(c, dc)))
            cp.start()
            sends.append(cp)
        for k, (dx, dy, dc) in enumerate(RELATIONS):
            src = 4 * _flip(x, dx) + 2 * _flip(y, dy) + _flip(c, dc)
            _remote(dmod_ref, dall_ref.at[src], dsend.at[k], drecv.at[k], (x, y, c)).wait_recv()
        for w in range(n):
            _remote(outs[w], outs[w], send_sems.at[w], recv_sems.at[w], (x, y, c)).wait_recv()
        own.wait()
        for cp in sends:
            cp.wait_send()

    out_shape = [pltpu.HBM((N_CHIPS, g.shape[1] // 2, g.shape[2]), F32) for g in grads]
    out_shape.append(pltpu.HBM((8,) + dmod.shape, F32))
    res = pl.pallas_call(
        body, name="grad_swap_halves",
        out_shape=out_shape, in_specs=[ANY] * n + [VMEM_SPEC], out_specs=[ANY] * (n + 1),
        scratch_shapes=[pltpu.SemaphoreType.DMA((n,)), pltpu.SemaphoreType.DMA((n,)),
                        pltpu.SemaphoreType.DMA((7,)), pltpu.SemaphoreType.DMA((7,)), pltpu.SemaphoreType.DMA],
        compiler_params=_params(),
    )(*grads, dmod)
    return res[:n], res[n]


def _add_my_half(core, full, got, name):
    _, hr, cols = got.shape

    def body(core_ref, a_ref, b_ref, o_ref):
        o_ref[...] = a_ref[...] + b_ref[...]

    return pl.pallas_call(
        body, name=name,
        out_shape=pltpu.HBM(got.shape, F32),
        grid_spec=pltpu.PrefetchScalarGridSpec(
            num_scalar_prefetch=1, grid=(N_CHIPS,),
            in_specs=[pl.BlockSpec((None, hr, cols), lambda s, core_ref: (s, core_ref[0], 0)),
                      pl.BlockSpec((None, hr, cols), lambda s, core_ref: (s, 0, 0))],
            out_specs=pl.BlockSpec((None, hr, cols), lambda s, core_ref: (s, 0, 0))),
        compiler_params=_params(("arbitrary",)),
    )(core, *_hbm(full, got))


def _add_chips_into_pair(chip_core, mine, got, name):
    _, hr, cols = mine.shape

    def body(cc_ref, a_ref, b_ref, o_ref):
        o_ref[...] = ((a_ref[...] + b_ref[0]) + b_ref[1]) + b_ref[2]

    return pl.pallas_call(
        body, name=name,
        out_shape=pltpu.HBM((2, hr, cols), F32),
        grid_spec=pltpu.PrefetchScalarGridSpec(
            num_scalar_prefetch=1, grid=(1,),
            in_specs=[pl.BlockSpec((None, hr, cols), lambda s, cc_ref: (cc_ref[0], 0, 0)),
                      pl.BlockSpec((3, hr, cols), lambda s, cc_ref: (0, 0, 0))],
            out_specs=pl.BlockSpec((None, hr, cols), lambda s, cc_ref: (cc_ref[1], 0, 0))),
        compiler_params=_params(("arbitrary",)),
    )(chip_core, *_hbm(mine, got))


def _add_chips_into_grid(chip_core, mine, got, name):
    _, hr, cols = mine.shape

    def body(cc_ref, a_ref, b_ref, o_ref):
        o_ref[...] = ((a_ref[...] + b_ref[0]) + b_ref[1]) + b_ref[2]

    return pl.pallas_call(
        body, name=name,
        out_shape=pltpu.HBM((N_CHIPS, 2, hr, cols), F32),
        grid_spec=pltpu.PrefetchScalarGridSpec(
            num_scalar_prefetch=1, grid=(1,),
            in_specs=[pl.BlockSpec((None, hr, cols), lambda s, cc_ref: (cc_ref[0], 0, 0)),
                      pl.BlockSpec((3, hr, cols), lambda s, cc_ref: (0, 0, 0))],
            out_specs=pl.BlockSpec((None, None, hr, cols), lambda s, cc_ref: (cc_ref[0], cc_ref[1], 0, 0))),
        compiler_params=_params(("arbitrary",)),
    )(chip_core, *_hbm(mine, got))


def _place_shards(chip, shards):
    n = len(shards)

    def body(chip_ref, *refs):
        for w in range(n):
            refs[n + w][...] = refs[w][...]

    return pl.pallas_call(
        body, name="place_shards",
        out_shape=[pltpu.HBM((N_CHIPS,) + s.shape, s.dtype) for s in shards],
        grid_spec=pltpu.PrefetchScalarGridSpec(
            num_scalar_prefetch=1, grid=(1,),
            in_specs=[pl.BlockSpec(s.shape, lambda i, chip_ref: (0, 0)) for s in shards],
            out_specs=[pl.BlockSpec((None,) + s.shape, lambda i, chip_ref: (chip_ref[0], 0, 0)) for s in shards]),
        compiler_params=_params(("arbitrary",)),
    )(chip, *shards)


def _rope_tables(pos_col, freqs):
    S = pos_col.shape[0]
    T = _row_tile(S, 1024)

    def body(p_ref, f_ref, cos_ref, sin_ref):
        ang = p_ref[...].astype(F32) * f_ref[...]
        cos_ref[...] = jnp.cos(ang)
        sin_ref[...] = jnp.sin(ang)

    return pl.pallas_call(
        body, name="rope_tables", grid=(S // T,),
        out_shape=[pltpu.HBM((S, 128), F32)] * 2,
        in_specs=[pl.BlockSpec((T, 1), lambda i: (i, 0)), pl.BlockSpec((1, 128), lambda i: (0, 0))],
        out_specs=[pl.BlockSpec((T, 128), lambda i: (i, 0))] * 2,
        compiler_params=_params(("parallel",)),
    )(*_hbm(pos_col, freqs))


def _full(shape):
    zeros = (0,) * len(shape)
    return pl.BlockSpec(shape, lambda *_: zeros)


def _pre_attention(x, mod6, g_mix, g_q, g_kv, w_in, w_uq, w_uk_t, cos, sin, T, TQ):
    S = x.shape[0]

    def body(x_ref, mod_ref, gm_ref, gq_ref, gkv_ref, win_ref, wuq_ref, wuk_ref, cos_ref, sin_ref,
             proj_ref, q_ref, qc_ref, kc_ref, kct_ref):
        xh, _ = _rms(x_ref[...])
        h1 = ((xh * gm_ref[...]) * (1.0 + mod_ref[1:2, :]) + mod_ref[0:1, :]).astype(BF16)
        rows_in = D_MODEL // N_CHIPS
        proj = _dot_nt(h1[:, 0:rows_in], win_ref[0])
        for j in range(1, N_CHIPS):
            proj = proj + _dot_nt(h1[:, j * rows_in:(j + 1) * rows_in], win_ref[j])
        proj_ref[...] = proj
        cqh, _ = _rms(proj[:, :Q_LORA])
        c_q = cqh * gq_ref[...]
        ckvh, _ = _rms(proj[:, O_CKV:O_KR])
        c_kv = ckvh * gkv_ref[...]
        q = _dot(c_q, wuq_ref[...])
        q_ref[...] = q.astype(BF16)
        cos_t, sin_t = cos_ref[...], sin_ref[...]
        ropes = (_rope(q[:, O_QA:O_QB], cos_t, sin_t), _rope(q[:, O_QB:Q_W], cos_t, sin_t))
        low = lax.broadcasted_iota(jnp.int32, (T, 128), 1) < ROPE
        for h in range(HEADS):
            q_lat = _dot_nt(q[:, h * NOPE:(h + 1) * NOPE], wuk_ref[h])
            keep = low if h % 2 == 0 else jnp.logical_not(low)
            qc_ref[h, :, 0:KV_LORA] = q_lat.astype(BF16)
            qc_ref[h, :, KV_LORA:QK_PAD] = jnp.where(keep, ropes[h // 2], 0.0).astype(BF16)
        k_rope = _rope(proj[:, O_KR:O_U], cos_t, sin_t)
        kc_ref[:, 0:KV_LORA] = c_kv.astype(BF16)
        kc_ref[:, KV_LORA:QK_PAD] = k_rope.astype(BF16)
        lat_t, rope_t = jnp.transpose(c_kv), jnp.transpose(k_rope)
        for s in range(T // TQ):
            kct_ref[s, 0:KV_LORA, :] = lat_t[:, s * TQ:(s + 1) * TQ].astype(BF16)
            kct_ref[s, KV_LORA:QK_PAD, :] = rope_t[:, s * TQ:(s + 1) * TQ].astype(BF16)

    row = lambda w: pl.BlockSpec((T, w), lambda i: (i, 0))
    return pl.pallas_call(
        body, name="pre_attention", grid=(S // T,),
        out_shape=[pltpu.HBM((S, PROJ_W), F32), pltpu.HBM((S, Q_W), BF16), pltpu.HBM((HEADS, S, QK_PAD), BF16),
                   pltpu.HBM((S, QK_PAD), BF16), pltpu.HBM((S // TQ, QK_PAD, TQ), BF16)],
        in_specs=[row(D_MODEL), _full((N_MOD, D_MODEL)), _full((1, D_MODEL)), _full((1, Q_LORA)), _full((1, KV_LORA)),
                  _full((N_CHIPS, PROJ_W, D_MODEL // N_CHIPS)), _full((Q_LORA, Q_W)), _full((HEADS, KV_LORA, NOPE)),
                  row(128), row(128)],
        out_specs=[row(PROJ_W), row(Q_W), pl.BlockSpec((HEADS, T, QK_PAD), lambda i: (0, i, 0)), row(QK_PAD),
                   pl.BlockSpec((T // TQ, QK_PAD, TQ), lambda i: (i, 0, 0))],
        compiler_params=_params(("parallel",)),
    )(*_hbm(x, mod6, g_mix, g_q, g_kv, w_in, w_uq, w_uk_t, cos, sin))


def _diag_mask(TQ, width):
    key = lax.broadcasted_iota(jnp.int32, (TQ, width), 0) >> CHUNK_SHIFT
    qry = (lax.broadcasted_iota(jnp.int32, (TQ, width), 1) & (TQ - 1)) >> CHUNK_SHIFT
    return key <= qry


def _col_to_row(col):
    return jnp.transpose(jnp.broadcast_to(col, (col.shape[0], 128)))[0:1, :]


def _attention_fwd(qc, kc, kct, w_uv_t, TQ):
    S = kc.shape[0]
    R = HEADS * TQ
    nq = S // TQ

    def body(q_ref, k_ref, kt_ref, wuv_ref, o_ref, y_ref, lser_ref, m_s, l_s, acc_s, st_s):
        i = pl.program_id(0)
        q = q_ref[...].reshape(R, QK_PAD)
        m_s[...] = jnp.full((1, R), -jnp.inf, F32)
        l_s[...] = jnp.zeros((1, R), F32)
        acc_s[...] = jnp.zeros((KV_LORA, R), F32)

        def scores(j):
            return _dot_nt(k_ref[pl.ds(pl.multiple_of(j * TQ, TQ), TQ), :], q) * SM_SCALE

        def update(j, st):
            m_old = m_s[...]
            m_new = jnp.maximum(m_old, jnp.max(st, axis=0, keepdims=True))
            pt = jnp.exp(st - m_new)
            alpha = jnp.exp(m_old - m_new)
            l_s[...] = alpha * l_s[...] + jnp.sum(pt, axis=0, keepdims=True)
            acc_s[...] = alpha * acc_s[...] + _dot(kt_ref[j, 0:KV_LORA, :], pt)
            m_s[...] = m_new

        st_s[...] = scores(0)

        def loop(j, carry):
            st = st_s[...]
            st_s[...] = scores(j + 1)
            update(j, st)
            return carry

        lax.fori_loop(0, i, loop, 0)
        update(i, jnp.where(_diag_mask(TQ, R), st_s[...], -jnp.inf))
        l = l_s[...]
        lser_ref[0] = m_s[...] + jnp.log(l)
        o = jnp.transpose(acc_s[...] / l).astype(BF16)
        for h in range(HEADS):
            oh = o[h * TQ:(h + 1) * TQ, :]
            o_ref[h] = oh
            y_ref[:, h * 128:(h + 1) * 128] = _dot(oh, wuv_ref[h]).astype(BF16)

    return pl.pallas_call(
        body, name="attention_fwd", grid=(nq,),
        out_shape=[pltpu.HBM((HEADS, S, KV_LORA), BF16), pltpu.HBM((S, HEADS * 128), BF16),
                   pltpu.HBM((nq, 1, R), F32)],
        in_specs=[pl.BlockSpec((HEADS, TQ, QK_PAD), lambda i: (0, i, 0)), _full((S, QK_PAD)),
                  _full((nq, QK_PAD, TQ)), _full((HEADS, KV_LORA, 128))],
        out_specs=[pl.BlockSpec((HEADS, TQ, KV_LORA), lambda i: (0, i, 0)), pl.BlockSpec((TQ, HEADS * 128), lambda i: (i, 0)),
                   pl.BlockSpec((1, 1, R), lambda i: (i, 0, 0))],
        scratch_shapes=[pltpu.VMEM((1, R), F32), pltpu.VMEM((1, R), F32), pltpu.VMEM((KV_LORA, R), F32),
                        pltpu.VMEM((TQ, R), F32)],
        compiler_params=_params(("parallel",)),
    )(*_hbm(qc, kc, kct, w_uv_t))


def _pool_forward(proj):
    S = proj.shape[0]
    RB = _row_tile(S, 256)

    def body(proj_ref, out_ref, pad_ref, sem):
        cp = pltpu.make_async_copy(proj_ref.at[:, pl.ds(O_U, POOL_W)], pad_ref.at[pl.ds(POOL_PAD, S)], sem)
        cp.start()
        pad_ref[0:POOL_PAD, :] = jnp.zeros((POOL_PAD, POOL_W), F32)
        cp.wait()
        for g, win in enumerate(POOL_WINDOWS):
            cols = slice(g * POOL_GROUP, (g + 1) * POOL_GROUP)
            for r0 in range(0, S, RB):
                u = pad_ref[POOL_PAD + r0:POOL_PAD + r0 + RB, cols]
                acc = u
                for k in range(1, win):
                    acc = acc + pad_ref[POOL_PAD + r0 - k:POOL_PAD + r0 - k + RB, cols]
                if r0 == 0:
                    t1 = (lax.broadcasted_iota(jnp.int32, (RB, POOL_GROUP), 0) + 1).astype(F32)
                    mean = acc / jnp.minimum(t1, float(win))
                else:
                    mean = acc * (1.0 / win)
                out_ref[r0:r0 + RB, cols] = (mean - u).astype(BF16)

    return pl.pallas_call(
        body, name="pool_forward",
        out_shape=jax.ShapeDtypeStruct((S, POOL_W), BF16),
        in_specs=[ANY], out_specs=VMEM_SPEC,
        scratch_shapes=[pltpu.VMEM((S + POOL_PAD, POOL_W), F32), pltpu.SemaphoreType.DMA],
        compiler_params=_params(),
    )(proj)


def _pool_backward(dpooled, after):
    S = dpooled.shape[0]
    RB = _row_tile(S, 256)

    def body(dp_ref, after_ref, out_ref, pad_ref, sem):
        cp = pltpu.make_async_copy(dp_ref, pad_ref.at[pl.ds(0, S)], sem)
        cp.start()
        pad_ref[S:S + POOL_PAD, :] = jnp.zeros((POOL_PAD, POOL_W), F32)
        cp.wait()
        for g, win in enumerate(POOL_WINDOWS):
            cols = slice(g * POOL_GROUP, (g + 1) * POOL_GROUP)
            head = pad_ref[0:POOL_PAD, cols]
            t1 = (lax.broadcasted_iota(jnp.int32, (POOL_PAD, POOL_GROUP), 0) + 1).astype(F32)
            pad_ref[0:POOL_PAD, cols] = head * (float(win) / jnp.minimum(t1, float(win)))
            for r0 in range(0, S, RB):
                acc = pad_ref[r0:r0 + RB, cols]
                for k in range(1, win):
                    acc = acc + pad_ref[r0 + k:r0 + k + RB, cols]
                own = pad_ref[r0:r0 + RB, cols]
                if r0 == 0:
                    own = jnp.concatenate([head, own[POOL_PAD:]], axis=0)
                out_ref[r0:r0 + RB, cols] = (acc * (1.0 / win) - own).astype(BF16)

    return pl.pallas_call(
        body, name="pool_backward",
        out_shape=jax.ShapeDtypeStruct((S, POOL_W), BF16),
        in_specs=[ANY, ANY], out_specs=VMEM_SPEC,
        scratch_shapes=[pltpu.VMEM((S + POOL_PAD, POOL_W), F32), pltpu.SemaphoreType.DMA],
        compiler_params=_params(),
    )(dpooled, after)


def _mix_out(y_mla, pooled, w_pool, pool_scale, w_o, x, mod6, T):
    S = x.shape[0]

    def body(ym_ref, pl_ref, wp_ref, ps_ref, wo_ref, x_ref, mod_ref, x1_ref, mix_ref, mi_ref):
        mi_ref[:, 0:512] = ym_ref[...]
        for g in range(len(POOL_WINDOWS)):
            cols = slice(g * POOL_GROUP, (g + 1) * POOL_GROUP)
            z = _dot(pl_ref[:, cols], wp_ref[g])
            mi_ref[:, 512 + g * POOL_GROUP:512 + (g + 1) * POOL_GROUP] = (z * ps_ref[:, cols]).astype(BF16)
        mix = _dot(mi_ref[...], wo_ref[...])
        mix_ref[...] = mix.astype(BF16)
        x1_ref[...] = x_ref[...] + mod_ref[2:3, :] * mix

    row = lambda w: pl.BlockSpec((T, w), lambda i: (i, 0))
    return pl.pallas_call(
        body, name="mix_out", grid=(S // T,),
        out_shape=[pltpu.HBM((S, D_MODEL), F32), pltpu.HBM((S, D_MODEL), BF16), pltpu.HBM((S, 1024), BF16)],
        in_specs=[row(512), row(POOL_W), _full((4, POOL_GROUP, POOL_GROUP)), _full((1, POOL_W)),
                  _full((1024, D_MODEL)), row(D_MODEL), _full((N_MOD, D_MODEL))],
        out_specs=[row(D_MODEL), row(D_MODEL), row(1024)],
        compiler_params=_params(("parallel",)),
    )(*_hbm(y_mla, pooled, w_pool, pool_scale, w_o, x, mod6))


def _ffn_forward(x1, mod6, g_ffn, g_final, target, w_gate, w_up, w_down, T):
    S = x1.shape[0]

    def body(x1_ref, mod_ref, gf_ref, gl_ref, tgt_ref, wg_ref, wu_ref, wd_ref,
             gate_ref, up_ref, act_ref, h2_ref, dff_ref, dx2_ref, st_ref, acc_s):
        i, j = pl.program_id(0), pl.program_id(1)

        @pl.when(jnp.logical_and(i == 0, j == 0))
        def _():
            st_ref[...] = jnp.zeros_like(st_ref)

        @pl.when(j == 0)
        def _():
            xh, _ = _rms(x1_ref[...])
            h2_ref[...] = ((xh * gf_ref[...]) * (1.0 + mod_ref[4:5, :]) + mod_ref[3:4, :]).astype(BF16)
            acc_s[...] = jnp.zeros_like(acc_s)

        h2 = h2_ref[...]
        gate = _dot_nt(h2, wg_ref[j])
        up = _dot_nt(h2, wu_ref[j])
        gate_ref[...] = gate.astype(BF16)
        up_ref[...] = up.astype(BF16)
        act = (gate * jax.nn.sigmoid(gate) * up).astype(BF16)
        act_ref[...] = act
        acc_s[...] += _dot(act, wd_ref[j])

        @pl.when(j == N_CHIPS - 1)
        def _():
            ff = acc_s[...]
            x2 = x1_ref[...] + mod_ref[5:6, :] * ff
            xh, r3 = _rms(x2)
            err = xh * gl_ref[...] - tgt_ref[...]
            dy = err * (1.0 / D_MODEL)
            dx2 = _rms_bwd(dy * gl_ref[...], xh, r3)
            dx2_ref[...] = dx2
            dff_ref[...] = (dx2 * mod_ref[5:6, :]).astype(BF16)
            st_ref[0:1, :] += jnp.sum(dy * xh, axis=0, keepdims=True)
            st_ref[1:2, :] += jnp.sum(dx2 * ff, axis=0, keepdims=True)
            st_ref[2:3, :] += 0.5 * jnp.sum(err * dy)

    row = pl.BlockSpec((T, D_MODEL), lambda i, j: (i, 0))
    chunk_out = pl.BlockSpec((None, T, FF_CHUNK), lambda i, j: (j, i, 0))
    big = pltpu.HBM((N_CHIPS, S, FF_CHUNK), BF16)
    wide = pltpu.HBM((S, D_MODEL), BF16)
    return pl.pallas_call(
        body, name="ffn_forward", grid=(S // T, N_CHIPS),
        out_shape=[big, big, big, wide, wide, pltpu.HBM((S, D_MODEL), F32), jax.ShapeDtypeStruct((8, D_MODEL), F32)],
        in_specs=[row, _full((N_MOD, D_MODEL)), _full((1, D_MODEL)), _full((1, D_MODEL)), row,
                  VMEM_SPEC, VMEM_SPEC, VMEM_SPEC],
        out_specs=[chunk_out, chunk_out, chunk_out, row, row, row, _full((8, D_MODEL))],
        scratch_shapes=[pltpu.VMEM((T, D_MODEL), F32)],
        compiler_params=_params(("arbitrary", "arbitrary")),
    )(*_hbm(x1, mod6, g_ffn, g_final, target), w_gate, w_up, w_down)


def _ffn_backward(dx2, x1, dff, gate, up, mod6, g_ffn, w_gate, w_up, w_down, T):
    S = x1.shape[0]

    def body(dx2_ref, x1_ref, dff_ref, gate_ref, up_ref, mod_ref, gf_ref, wg_ref, wu_ref, wd_ref,
             dgate_ref, dup_ref, dx1_ref, st_ref, acc_s):
        i, j = pl.program_id(0), pl.program_id(1)

        @pl.when(jnp.logical_and(i == 0, j == 0))
        def _():
            st_ref[...] = jnp.zeros_like(st_ref)

        @pl.when(j == 0)
        def _():
            acc_s[...] = jnp.zeros_like(acc_s)

        for r0 in range(0, T, T // 2):
            rows = slice(r0, r0 + T // 2)
            gate, up = gate_ref[rows, :].astype(F32), up_ref[rows, :].astype(F32)
            sg = jax.nn.sigmoid(gate)
            dact = _dot_nt(dff_ref[rows, :], wd_ref[j])
            dup = (dact * (gate * sg)).astype(BF16)
            dgate = (dact * up * (sg * (1.0 + gate * (1.0 - sg)))).astype(BF16)
            dup_ref[rows, :] = dup
            dgate_ref[rows, :] = dgate
            acc_s[rows, :] += _dot(dgate, wg_ref[j]) + _dot(dup, wu_ref[j])

        @pl.when(j == N_CHIPS - 1)
        def _():
            dh2 = acc_s[...]
            xh, r2 = _rms(x1_ref[...])
            n2 = xh * gf_ref[...]
            st_ref[0:1, :] += jnp.sum(dh2, axis=0, keepdims=True)
            st_ref[1:2, :] += jnp.sum(dh2 * n2, axis=0, keepdims=True)
            dn2 = dh2 * (1.0 + mod_ref[4:5, :])
            st_ref[2:3, :] += jnp.sum(dn2 * xh, axis=0, keepdims=True)
            dx1_ref[...] = _rms_bwd(dn2 * gf_ref[...], xh, r2) + dx2_ref[...]

    row = pl.BlockSpec((T, D_MODEL), lambda i, j: (i, 0))
    chunk = pl.BlockSpec((None, T, FF_CHUNK), lambda i, j: (j, i, 0))
    big = pltpu.HBM((N_CHIPS, S, FF_CHUNK), BF16)
    return pl.pallas_call(
        body, name="ffn_backward", grid=(S // T, N_CHIPS),
        out_shape=[big, big, pltpu.HBM((S, D_MODEL), F32), jax.ShapeDtypeStruct((8, D_MODEL), F32)],
        in_specs=[row, row, row, chunk, chunk, _full((N_MOD, D_MODEL)), _full((1, D_MODEL)),
                  VMEM_SPEC, VMEM_SPEC, VMEM_SPEC],
        out_specs=[chunk, chunk, row, _full((8, D_MODEL))],
        scratch_shapes=[pltpu.VMEM((T, D_MODEL), F32)],
        compiler_params=_params(("arbitrary", "arbitrary")),
    )(*_hbm(dx2, x1, dff, gate, up, mod6, g_ffn), w_gate, w_up, w_down)


def _tn_matmul(a, b, a_spec, b_spec, groups, m, n, steps, name):
    def body(a_ref, b_ref, o_ref):
        @pl.when(pl.program_id(1) == 0)
        def _():
            o_ref[...] = jnp.zeros_like(o_ref)

        o_ref[...] += _dot_tn(a_ref[...], b_ref[...])

    return pl.pallas_call(
        body, name=name, grid=(groups, steps),
        out_shape=pltpu.HBM((groups, m, n), F32),
        in_specs=[a_spec, b_spec],
        out_specs=pl.BlockSpec((None, m, n), lambda g, i: (g, 0, 0)),
        compiler_params=_params(("parallel", "arbitrary")),
    )(*_hbm(a, b))


def _mix_backward(dx1, mix, mod6, w_o, pooled, w_pool, pool_scale, w_uv_t, o_lat, T, TQ):
    S = dx1.shape[0]

    def body(dx1_ref, mix_ref, mod_ref, wo_ref, pl_ref, wp_ref, ps_ref, wuv_ref, o_ref,
             dmix_ref, dp_ref, do_ref, dr_ref, gp_ref, guv_ref, st_ref):
        @pl.when(pl.program_id(0) == 0)
        def _():
            st_ref[...] = jnp.zeros_like(st_ref)
            gp_ref[...] = jnp.zeros_like(gp_ref)
            guv_ref[...] = jnp.zeros_like(guv_ref)

        dx1 = dx1_ref[...]
        st_ref[0:1, :] += jnp.sum(dx1 * mix_ref[...].astype(F32), axis=0, keepdims=True)
        dmix = (dx1 * mod_ref[2:3, :]).astype(BF16)
        dmix_ref[...] = dmix
        dmi = _dot_nt(dmix, wo_ref[...])
        dym = dmi[:, 0:512].astype(BF16)
        for g in range(len(POOL_WINDOWS)):
            cols = slice(g * POOL_GROUP, (g + 1) * POOL_GROUP)
            dyp = dmi[:, 512 + g * POOL_GROUP:512 + (g + 1) * POOL_GROUP]
            pooled_g = pl_ref[:, cols]
            z = _dot(pooled_g, wp_ref[g])
            st_ref[1:2, cols] += jnp.sum(dyp * z, axis=0, keepdims=True)
            dz = (dyp * ps_ref[:, cols]).astype(BF16)
            gp_ref[g] += _dot_tn(pooled_g, dz)
            dp_ref[:, cols] = _dot_nt(dz, wp_ref[g])
        for h in range(HEADS):
            dym_h = dym[:, h * 128:(h + 1) * 128]
            do = _dot_nt(dym_h, wuv_ref[h]).astype(BF16)
            do_ref[h] = do
            o_h = o_ref[h]
            guv_ref[h] += _dot_tn(o_h, dym_h)
            delta = _col_to_row(jnp.sum(do.astype(F32) * o_h.astype(F32), axis=1, keepdims=True))
            for s in range(T // TQ):
                dr_ref[s, :, h * TQ:(h + 1) * TQ] = delta[:, s * TQ:(s + 1) * TQ]

    row = lambda w: pl.BlockSpec((T, w), lambda i: (i, 0))
    heads = pl.BlockSpec((HEADS, T, KV_LORA), lambda i: (0, i, 0))
    square = jax.ShapeDtypeStruct((4, 128, 128), F32)
    return pl.pallas_call(
        body, name="mix_backward", grid=(S // T,),
        out_shape=[pltpu.HBM((S, D_MODEL), BF16), pltpu.HBM((S, POOL_W), F32), pltpu.HBM((HEADS, S, KV_LORA), BF16),
                   pltpu.HBM((S // TQ, 1, HEADS * TQ), F32), square, square, jax.ShapeDtypeStruct((8, D_MODEL), F32)],
        in_specs=[row(D_MODEL), row(D_MODEL), _full((N_MOD, D_MODEL)), _full((1024, D_MODEL)), row(POOL_W),
                  _full((4, POOL_GROUP, POOL_GROUP)), _full((1, POOL_W)), _full((HEADS, KV_LORA, 128)), heads],
        out_specs=[row(D_MODEL), row(POOL_W), heads,
                   pl.BlockSpec((T // TQ, 1, HEADS * TQ), lambda i: (i, 0, 0)), _full((4, 128, 128)),
                   _full((4, 128, 128)), _full((8, D_MODEL))],
        compiler_params=_params(("arbitrary",)),
    )(*_hbm(dx1, mix, mod6, w_o, pooled, w_pool, pool_scale, w_uv_t, o_lat))


def _attention_bwd(qc, kc, kct, do, lse_rows, delta_rows, TQ):
    S = kc.shape[0]
    R = HEADS * TQ
    nq = S // TQ

    def body(q_ref, do_ref, lser_ref, dr_ref, k_ref, kt_ref, dqt_ref, dk_ref, dqt_s, dv_s):
        i = pl.program_id(0)

        def key_rows(j):
            return pl.ds(pl.multiple_of(j * TQ, TQ), TQ)

        @pl.when(i == 0)
        def _():
            def zero(j, carry):
                dk_ref[key_rows(j), :] = jnp.zeros((TQ, QK_PAD), F32)
                dv_s[key_rows(j), :] = jnp.zeros((TQ, KV_LORA), F32)
                return carry
            lax.fori_loop(0, nq, zero, 0)

        q = q_ref[...].reshape(R, QK_PAD)
        do = do_ref[...].reshape(R, KV_LORA)
        lse, delta = lser_ref[0], dr_ref[0]
        dqt_s[...] = jnp.zeros((QK_PAD, R), F32)

        def step(j, masked):
            rows = key_rows(j)
            k = k_ref[rows, :]
            st = _dot_nt(k, q) * SM_SCALE
            if masked:
                st = jnp.where(_diag_mask(TQ, R), st, -jnp.inf)
            pt = jnp.exp(st - lse)
            dv_s[rows, :] += _dot(pt, do)
            dpt = _dot_nt(k[:, :KV_LORA], do)
            dst = (pt * (dpt - delta)).astype(BF16)
            dk_ref[rows, :] += _dot(dst, q)
            dqt_s[...] += _dot(kt_ref[j], dst)

        def loop(j, carry):
            step(j, False)
            return carry

        lax.fori_loop(0, i, loop, 0)
        step(i, True)
        dqt_ref[...] = dqt_s[...]

        @pl.when(i == nq - 1)
        def _():
            def finish(j, carry):
                rows = key_rows(j)
                dk = dk_ref[rows, :] * SM_SCALE
                dk_ref[rows, 0:KV_LORA] = dk[:, 0:KV_LORA] + dv_s[rows, :]
                dk_ref[rows, KV_LORA:QK_PAD] = dk[:, KV_LORA:QK_PAD]
                return carry
            lax.fori_loop(0, nq, finish, 0)

    tile = lambda w: pl.BlockSpec((HEADS, TQ, w), lambda i: (0, i, 0))
    row = pl.BlockSpec((1, 1, R), lambda i: (i, 0, 0))
    return pl.pallas_call(
        body, name="attention_bwd", grid=(nq,),
        out_shape=[pltpu.HBM((nq, QK_PAD, R), F32), jax.ShapeDtypeStruct((S, QK_PAD), F32)],
        in_specs=[tile(QK_PAD), tile(KV_LORA), row, row, VMEM_SPEC, VMEM_SPEC],
        out_specs=[pl.BlockSpec((None, QK_PAD, R), lambda i: (i, 0, 0)), VMEM_SPEC],
        scratch_shapes=[pltpu.VMEM((QK_PAD, R), F32), pltpu.VMEM((S, KV_LORA), F32)],
        compiler_params=_params(("arbitrary",)),
    )(*_hbm(qc, do, lse_rows, delta_rows), kc, kct)[::-1]


def _pre_attention_backward(x, dx1, proj, q, dqt, dkc, du, cos, sin, mod6, g_mix, g_q, g_kv, w_in, w_uq, w_uk_t, T, TQ):
    S = x.shape[0]

    def body(x_ref, dx1_ref, proj_ref, q_ref, dqt_ref, dkc_ref, du_ref, cos_ref, sin_ref, mod_ref, gm_ref, gq_ref,
             gkv_ref, win_ref, wuq_ref, wuk_ref, gx_ref, dproj_ref, h1_ref, guk_ref, guq_ref, st_ref, dq_ref):
        @pl.when(pl.program_id(0) == 0)
        def _():
            st_ref[...] = jnp.zeros_like(st_ref)
            guk_ref[...] = jnp.zeros_like(guk_ref)
            guq_ref[...] = jnp.zeros_like(guq_ref)

        cos_t, sin_t = cos_ref[...], sin_ref[...]
        low = lax.broadcasted_iota(jnp.int32, (T, 128), 1) < ROPE
        rope_parts = []
        for h in range(HEADS):
            dqc = jnp.concatenate([jnp.transpose(dqt_ref[s, :, h * TQ:(h + 1) * TQ]) for s in range(T // TQ)], axis=0)
            dqc = dqc * SM_SCALE
            dql = dqc[:, 0:KV_LORA].astype(BF16)
            guk_ref[h] += _dot_tn(dql, q_ref[:, h * NOPE:(h + 1) * NOPE])
            dq_ref[:, h * NOPE:(h + 1) * NOPE] = _dot(dql, wuk_ref[h]).astype(BF16)
            rope_parts.append(dqc[:, KV_LORA:QK_PAD])
        for pair in range(2):
            d = jnp.where(low, rope_parts[2 * pair], rope_parts[2 * pair + 1])
            dq_ref[:, O_QA + 128 * pair:O_QA + 128 * (pair + 1)] = _rope_bwd(d, cos_t, sin_t).astype(BF16)
        dq = dq_ref[...]
        dcq = _dot_nt(dq, wuq_ref[...])
        cqh, rq = _rms(proj_ref[:, 0:Q_LORA])
        guq_ref[...] += _dot_tn(cqh * gq_ref[...], dq)
        st_ref[3:4, 0:Q_LORA] += jnp.sum(dcq * cqh, axis=0, keepdims=True)
        dproj_ref[:, 0:Q_LORA] = _rms_bwd(dcq * gq_ref[...], cqh, rq).astype(BF16)
        dckv = dkc_ref[:, 0:KV_LORA]
        ckvh, rkv = _rms(proj_ref[:, O_CKV:O_KR])
        st_ref[4:5, 0:KV_LORA] += jnp.sum(dckv * ckvh, axis=0, keepdims=True)
        dproj_ref[:, O_CKV:O_KR] = _rms_bwd(dckv * gkv_ref[...], ckvh, rkv).astype(BF16)
        dkr = _rope_bwd(dkc_ref[:, KV_LORA:QK_PAD], cos_t, sin_t)
        dkr = jnp.where(low, dkr + pltpu.roll(dkr, ROPE, 1), 0.0)
        dproj_ref[:, O_KR:O_U] = dkr.astype(BF16)
        dproj_ref[:, O_U:PROJ_W] = du_ref[...].astype(BF16)
        dproj = dproj_ref[...]
        dh1 = jnp.concatenate([_dot(dproj, win_ref[j]) for j in range(N_CHIPS)], axis=1)
        xh, r1 = _rms(x_ref[...])
        n1 = xh * gm_ref[...]
        h1_ref[...] = (n1 * (1.0 + mod_ref[1:2, :]) + mod_ref[0:1, :]).astype(BF16)
        st_ref[0:1, :] += jnp.sum(dh1, axis=0, keepdims=True)
        st_ref[1:2, :] += jnp.sum(dh1 * n1, axis=0, keepdims=True)
        dn1 = dh1 * (1.0 + mod_ref[1:2, :])
        st_ref[2:3, :] += jnp.sum(dn1 * xh, axis=0, keepdims=True)
        gx_ref[...] = _rms_bwd(dn1 * gm_ref[...], xh, r1) + dx1_ref[...]

    row = lambda w: pl.BlockSpec((T, w), lambda i: (i, 0))
    return pl.pallas_call(
        body, name="pre_attention_backward", grid=(S // T,),
        out_shape=[jax.ShapeDtypeStruct((S, D_MODEL), F32), pltpu.HBM((S, PROJ_W), BF16),
                   pltpu.HBM((S, D_MODEL), BF16), jax.ShapeDtypeStruct((HEADS, KV_LORA, NOPE), F32),
                   jax.ShapeDtypeStruct((Q_LORA, Q_W), F32), jax.ShapeDtypeStruct((8, D_MODEL), F32)],
        in_specs=[row(D_MODEL), row(D_MODEL), row(O_KR), row(HEADS * NOPE),
                  pl.BlockSpec((T // TQ, QK_PAD, HEADS * TQ), lambda i: (i, 0, 0)),
                  row(QK_PAD), row(POOL_W), row(128), row(128), _full((N_MOD, D_MODEL)), _full((1, D_MODEL)),
                  _full((1, Q_LORA)), _full((1, KV_LORA)), _full((N_CHIPS, PROJ_W, D_MODEL // N_CHIPS)),
                  _full((Q_LORA, Q_W)), _full((HEADS, KV_LORA, NOPE))],
        out_specs=[row(D_MODEL), row(PROJ_W), row(D_MODEL), _full((HEADS, KV_LORA, NOPE)), _full((Q_LORA, Q_W)),
                   _full((8, D_MODEL))],
        scratch_shapes=[pltpu.VMEM((T, Q_W), BF16)],
        compiler_params=_params(("arbitrary",)),
    )(*_hbm(x, dx1, proj, q, dqt, dkc, du, cos, sin, mod6, g_mix, g_q, g_kv, w_in, w_uq, w_uk_t))


def _ada_grads(c_all, dmod_all, chip):
    cols = N_MOD * D_MODEL // N_CHIPS
    width = dmod_all.shape[1]

    def body(col_ref, c_ref, dcol_ref, dall_ref, gw_ref, gb_ref):
        call = c_ref[...]
        act = call * jax.nn.sigmoid(call)
        gw_ref[...] = _dot_tn(act, dcol_ref[...])
        d = dall_ref[...]
        acc = d[0:1, :]
        for b in range(1, 8):
            acc = acc + d[b:b + 1, :]
        gb_ref[...] = acc

    return pl.pallas_call(
        body, name="ada_grads",
        out_shape=[jax.ShapeDtypeStruct((D_MODEL, cols), F32), jax.ShapeDtypeStruct((1, width), F32)],
        grid_spec=pltpu.PrefetchScalarGridSpec(
            num_scalar_prefetch=1, grid=(1,),
            in_specs=[pl.BlockSpec((8, D_MODEL), lambda s, col_ref: (0, 0)),
                      pl.BlockSpec((8, cols), lambda s, col_ref: (0, col_ref[0])),
                      pl.BlockSpec((8, width), lambda s, col_ref: (0, 0))],
            out_specs=[pl.BlockSpec((D_MODEL, cols), lambda s, col_ref: (0, 0)),
                       pl.BlockSpec((1, width), lambda s, col_ref: (0, 0))]),
        compiler_params=_params(("arbitrary",)),
    )(chip, *_hbm(c_all, dmod_all, dmod_all))


def _adamw(w, g, m, v, name, g_is_landing_zone=True):
    rows, rest = w.shape[0], w.shape[1:]
    T = _row_tile(rows, 256)

    def body(w_ref, g_ref, m_ref, v_ref, *outs):
        d_ref, nm_ref, nv_ref = outs[-3:]
        g = g_ref[...]
        if g_is_landing_zone:
            outs[0][...] = g
        m2 = ADAM_B1 * m_ref[...] + (1.0 - ADAM_B1) * g
        v2 = ADAM_B2 * v_ref[...] + (1.0 - ADAM_B2) * (g * g)
        m_hat = m2 / (1.0 - ADAM_B1 ** ADAM_STEP)
        v_hat = v2 / (1.0 - ADAM_B2 ** ADAM_STEP)
        d_ref[...] = -ADAM_LR * (m_hat / (jnp.sqrt(v_hat) + ADAM_EPS) + ADAM_WD * w_ref[...])
        nm_ref[...] = m2
        nv_ref[...] = v2

    zeros = (0,) * len(rest)
    spec = pl.BlockSpec((T,) + rest, lambda i: (i,) + zeros)
    n_out = 4 if g_is_landing_zone else 3
    res = pl.pallas_call(
        body, name=name, grid=(rows // T,),
        out_shape=[jax.ShapeDtypeStruct(w.shape, F32)] * n_out,
        in_specs=[spec] * 4, out_specs=[spec] * n_out,
        compiler_params=_params(("parallel",)),
    )(*_hbm(w, g, m, v))
    return res if g_is_landing_zone else [g] + list(res)


SMALL_NAMES = ("w_uk", "w_uv", "w_pool", "g_mix", "g_q", "g_kv", "pool_scale", "g_ffn", "g_final", "b_ada")
SMALL_ROWS = 1664


def _pack_rows(parts):
    flat = jnp.concatenate([p.reshape(-1) for p in parts])
    pad = (-flat.shape[0]) % 128
    if pad:
        flat = jnp.concatenate([flat, jnp.zeros((pad,), F32)])
    return flat.reshape(-1, 128)


def kernel(x, c, positions, w_ada, b_ada, g_mix, w_in, g_q, g_kv, w_uq, w_uk, w_uv, w_pool, pool_scale, w_o, g_ffn, w_gate, w_up, w_down, g_final, loss_target, m_w_ada, m_b_ada, m_g_mix, m_w_in, m_g_q, m_g_kv, m_w_uq, m_w_uk, m_w_uv, m_w_pool, m_pool_scale, m_w_o, m_g_ffn, m_w_gate, m_w_up, m_w_down, m_g_final, v_w_ada, v_b_ada, v_g_mix, v_w_in, v_g_q, v_g_kv, v_w_uq, v_w_uk, v_w_uv, v_w_pool, v_pool_scale, v_w_o, v_g_ffn, v_w_gate, v_w_up, v_w_down, v_g_final):
    S = x.shape[1]
    T = _row_tile(S, 512)
    TQ = _row_tile(S, 256)
    TW = _row_tile(S, 2048)
    ix, iy, ic = lax.axis_index("x"), lax.axis_index("y"), lax.axis_index("c")
    chip = (2 * ix + iy).astype(jnp.int32)
    chip_arr = chip.reshape(1)
    core_arr = ic.astype(jnp.int32).reshape(1)

    xs, tgt = x[0], loss_target[0]

    tr = lambda a: jnp.transpose(a[0])
    win_t = tr(w_in)
    win_p = jnp.concatenate([win_t[:O_KR + ROPE], win_t[O_KR:O_KR + ROPE], win_t[O_KR + ROPE:]], axis=0).astype(BF16)
    wuq = w_uq[0]
    wuq_p = jnp.concatenate([wuq[:, h, :NOPE] for h in range(HEADS)] + [wuq[:, h, NOPE:] for h in range(HEADS)],
                            axis=1).astype(BF16)
    w_uk_t = jnp.transpose(w_uk[0], (1, 0, 2)).astype(BF16)
    w_uv_t = jnp.transpose(w_uv[0], (1, 0, 2)).astype(BF16)
    w_pool_b = w_pool[0].astype(BF16)
    first = [win_p, wuq_p]
    later = [w_o[0].astype(BF16), tr(w_gate).astype(BF16), tr(w_up).astype(BF16), w_down[0].astype(BF16)]
    placed = _place_shards(chip_arr, first + later)
    a_send, a_recv, a_lands, token = _split_start("first_weights_start", first, placed[:2], 6, _plan_gather_start)
    half = ROPE // 2
    freqs = jnp.power(ROPE_THETA, -jnp.arange(half, dtype=F32) / half)
    cos, sin = _rope_tables(positions.reshape(S, 1), jnp.tile(freqs, 4).reshape(1, 128) + token[0, 0])
    a_send, a_recv, a_lands, token = _split_relay(
        "first_weights_relay", a_send, a_recv, first, a_lands, cos, 6, _plan_gather_landed, _plan_gather_relay)

    ada_cols = w_ada.shape[2]
    b_cols = lax.dynamic_slice(b_ada, (0, chip * ada_cols), (1, ada_cols))
    mod, c_all = _mod_exchange(c, w_ada[0], b_cols + token[0, 0])
    mod6 = mod.reshape(N_MOD, D_MODEL)
    a_lands = _split_wait("first_weights_wait", a_send, a_recv, [], a_lands, mod, _plan_gather_wait)
    w_in_f = a_lands[0]
    w_uq_f = a_lands[1].reshape(Q_LORA, Q_W)
    wg_lands, mod6, w_in_f = lax.optimization_barrier((placed[2:], mod6, w_in_f))
    wg_send, wg_recv, wg_lands, token = _split_start(
        "weights_start", later, wg_lands, 3 * len(later), _plan_gather_start)
    mod6 = mod6 + token[0, 0]

    proj, q, qc, kc, kct = _pre_attention(xs, mod6, g_mix, g_q, g_kv, w_in_f, w_uq_f, w_uk_t, cos, sin, T, TQ)
    o_lat, y_mla, lse_rows = _attention_fwd(qc, kc, kct, w_uv_t, TQ)
    wg_send, wg_recv, wg_lands, token = _split_relay(
        "weights_relay", wg_send, wg_recv, later, wg_lands, y_mla, 3 * len(later), _plan_gather_landed,
        _plan_gather_relay)
    pooled = _pool_forward(proj)
    wg_lands = _split_wait("weights_wait", wg_send, wg_recv, [], wg_lands, pooled, _plan_gather_wait)
    w_o_f = wg_lands[0].reshape(1024, D_MODEL)
    w_gate_f, w_up_f, w_down_f = wg_lands[1], wg_lands[2], wg_lands[3]
    x1, mix, mix_in = _mix_out(y_mla, pooled, w_pool_b, pool_scale, w_o_f, xs, mod6, T)
    gate, up, act, h2, dff, dx2, st_f = _ffn_forward(
        x1, mod6, g_ffn, g_final.reshape(1, D_MODEL), tgt, w_gate_f, w_up_f, w_down_f, T)

    dgate, dup, dx1, st_b = _ffn_backward(dx2, x1, dff, gate, up, mod6, g_ffn, w_gate_f, w_up_f, w_down_f, T)
    steps = S // TW
    chunk_spec = pl.BlockSpec((None, TW, FF_CHUNK), lambda g, i: (g, i, 0))
    wide_spec = pl.BlockSpec((TW, D_MODEL), lambda g, i: (i, 0))
    g_down = _tn_matmul(act, dff, chunk_spec, wide_spec, N_CHIPS, FF_CHUNK, D_MODEL, steps, "grad_w_down")
    g_gate = _tn_matmul(dgate, h2, chunk_spec, wide_spec, N_CHIPS, FF_CHUNK, D_MODEL, steps, "grad_w_gate")
    g_up = _tn_matmul(dup, h2, chunk_spec, wide_spec, N_CHIPS, FF_CHUNK, D_MODEL, steps, "grad_w_up")

    half_shapes = lambda gs: [jax.ShapeDtypeStruct((N_CHIPS, g.shape[1] // 2, g.shape[2]), F32) for g in gs]
    ffn_grads = [g_gate, g_up, g_down]
    f_send, f_recv, f_lands, token = _split_start(
        "ffn_swap_start", ffn_grads, half_shapes(ffn_grads), len(ffn_grads), _plan_swap_start)
    dmix, dpooled, do_lat, delta_rows, g_pool, g_uv_t, st_m = _mix_backward(
        dx1, mix, mod6 + token[0, 0], w_o_f, pooled, w_pool_b, pool_scale, w_uv_t, o_lat, T, TQ)
    g_o = [_tn_matmul(mix_in, dmix, wide_spec, wide_spec, 1, 1024, D_MODEL, steps, "grad_w_o").reshape(N_CHIPS, -1, D_MODEL)]
    o_send, o_recv, o_lands, token = _split_start("w_o_swap_start", g_o, half_shapes(g_o), 1, _plan_swap_start)
    du = _pool_backward(dpooled, token)
    f_got = _split_wait("ffn_swap_wait", f_send, f_recv, ffn_grads, f_lands, du, _plan_swap_wait)
    f_got += _split_wait("w_o_swap_wait", o_send, o_recv, g_o, o_lands, du, _plan_swap_wait)
    far_names = ("w_gate", "w_up", "w_down", "w_o")
    far_grads = ffn_grads + g_o
    f_sums = [_add_my_half(core_arr, a, b, "add_half_" + n) for a, b, n in zip(far_grads, f_got, far_names)]
    f_send, f_recv, f_lands, token = _split_start(
        "far_exchange_start", f_sums, [jax.ShapeDtypeStruct((3,) + s.shape[1:], F32) for s in f_sums],
        3 * len(f_sums), _plan_exchange_start)
    delta_rows = delta_rows + token[0, 0]
    dkc, dqt = _attention_bwd(qc, kc, kct, do_lat, lse_rows, delta_rows, TQ)
    grad_x, dproj, h1, g_uk_t, uq, st_p = _pre_attention_backward(
        xs, dx1, proj, q, dqt, dkc, du, cos, sin, mod6, g_mix, g_q, g_kv, w_in_f, w_uq_f, w_uk_t, T, TQ)
    rows_in = D_MODEL // N_CHIPS
    g_in_p = _tn_matmul(dproj, h1, pl.BlockSpec((TW, PROJ_W), lambda g, i: (i, 0)),
                        pl.BlockSpec((TW, rows_in), lambda g, i: (i, g)), N_CHIPS, PROJ_W, rows_in, steps, "grad_w_in")

    g_in = jnp.concatenate([g_in_p[:, :O_KR + ROPE], g_in_p[:, O_U:]], axis=1)
    g_uq = jnp.concatenate([jnp.concatenate([uq[:, h * NOPE:(h + 1) * NOPE], uq[:, O_QA + h * ROPE:O_QA + (h + 1) * ROPE]],
                                            axis=1) for h in range(HEADS)], axis=1).reshape(N_CHIPS, -1, HEADS * HEAD_QK)
    small = _pack_rows([g_uk_t, g_uv_t, g_pool, st_p[2], st_p[3, :Q_LORA], st_p[4, :KV_LORA], st_m[1, :POOL_W],
                        st_b[2], st_f[0]])
    small = jnp.concatenate([small, jnp.zeros((SMALL_ROWS - small.shape[0], 128), F32)]).reshape(N_CHIPS, -1, 128)
    grads = [g_in, g_uq, small]
    dmod = jnp.concatenate([jnp.stack([st_p[0], st_p[1], st_m[0], st_b[0], st_b[1], st_f[1]]).reshape(48, 128),
                            jnp.zeros((8, 128), F32).at[0, 0].set(st_f[2, 0])])

    names = ("w_in", "w_uq", "small")
    got, dmod_all = _grad_swap_halves(grads, dmod)
    chip_sums = [_add_my_half(core_arr, a, b, "add_half_" + n) for a, b, n in zip(grads, got, names)]
    n_send, n_recv, n_lands, token = _split_start(
        "near_exchange_start", chip_sums, [jax.ShapeDtypeStruct((3,) + s.shape[1:], F32) for s in chip_sums],
        3 * len(chip_sums), _plan_exchange_start)

    f_others = _split_wait("far_exchange_wait", f_send, f_recv, f_sums, f_lands, token, _plan_exchange_wait)
    chip_core = jnp.concatenate([chip_arr, core_arr])
    f_pairs = [_add_chips_into_pair(chip_core, a, b, "add_chips_" + n) for a, b, n in zip(f_sums, f_others, far_names)]
    f_send, f_recv, f_pairs, token = _split_start("far_finish_start", [], f_pairs, len(f_pairs), _plan_finish_start)
    gw_ada, gb_ada = _ada_grads(c_all, dmod_all.reshape(8, -1) + token[0, 0], chip_arr)
    loss = gb_ada[0, N_MOD * D_MODEL]
    gb_ada = gb_ada[:, :N_MOD * D_MODEL]
    f_fulls = _split_wait("far_finish_wait", f_send, f_recv, [], f_pairs, gw_ada, _plan_finish_wait)
    gw_gate, gw_up, gw_down, gw_o = [f.reshape(-1, f.shape[2]) for f in f_fulls]

    untr = lambda a: jnp.transpose(a)[None]
    grad_out, delta_out, newm_out, newv_out = {}, {}, {}, {}

    def adam_sharded(n, w, g2, m, v, transposed, landed=True):
        view = (lambda a: jnp.transpose(a[0])) if transposed else (lambda a: a[0])
        back = untr if transposed else (lambda a: a[None])
        g_, d_, m_, v_ = _adamw(view(w), g2.reshape(view(w).shape), view(m), view(v), "adamw_" + n, landed)
        grad_out[n], delta_out[n], newm_out[n], newv_out[n] = back(g_), back(d_), back(m_), back(v_)
        return d_

    done = [adam_sharded("w_gate", w_gate, gw_gate, m_w_gate, v_w_gate, True),
            adam_sharded("w_up", w_up, gw_up, m_w_up, v_w_up, True),
            adam_sharded("w_down", w_down, gw_down, m_w_down, v_w_down, False),
            adam_sharded("w_o", w_o, gw_o, m_w_o, v_w_o, False)]
    after_all = jnp.stack([d[0, 0] for d in done])

    others = _split_wait("near_exchange_wait", n_send, n_recv, chip_sums, n_lands, after_all, _plan_exchange_wait)
    n_pairs = [_add_chips_into_pair(chip_core, a, b, "add_chips_" + n)
               for a, b, n in zip(chip_sums[:2], others[:2], names[:2])]
    small_grid = _add_chips_into_grid(chip_core, chip_sums[2], others[2], "add_chips_small")
    n_send, n_recv, n_lands, token = _split_start(
        "near_finish_start", [], n_pairs + [small_grid], 2 + len(RELATIONS), _plan_near_finish_start)
    gw_ada, _ = lax.optimization_barrier((gw_ada, token))
    d_ada = adam_sharded("w_ada", w_ada, gw_ada, m_w_ada, v_w_ada, False, landed=False)
    n_lands = _split_wait("near_finish_wait", n_send, n_recv, [], n_lands, d_ada, _plan_near_finish_wait)
    gw_in, gw_uq = [f.reshape(-1, f.shape[2]) for f in n_lands[:2]]
    small_all = n_lands[2].reshape(SMALL_ROWS * 128)
    adam_sharded("w_in", w_in, gw_in, m_w_in, v_w_in, True)
    adam_sharded("w_uq", w_uq, gw_uq, m_w_uq, v_w_uq, False)

    n_sq = KV_LORA * HEADS * 128
    sizes = [n_sq, n_sq, n_sq, D_MODEL, Q_LORA, KV_LORA, POOL_W, D_MODEL, D_MODEL]
    offs = [0]
    for s_ in sizes:
        offs.append(offs[-1] + s_)
    piece = lambda k: small_all[offs[k]:offs[k + 1]]
    grads_small = {
        "w_uk": jnp.transpose(piece(0).reshape(HEADS, KV_LORA, NOPE), (1, 0, 2)),
        "w_uv": jnp.transpose(piece(1).reshape(HEADS, KV_LORA, 128), (1, 0, 2)),
        "w_pool": piece(2).reshape(4, POOL_GROUP, POOL_GROUP),
        "g_mix": piece(3), "g_q": piece(4), "g_kv": piece(5), "pool_scale": piece(6), "g_ffn": piece(7),
        "g_final": piece(8), "b_ada": gb_ada.reshape(-1),
    }
    weights_small = {"w_uk": w_uk, "w_uv": w_uv, "w_pool": w_pool, "g_mix": g_mix, "g_q": g_q, "g_kv": g_kv,
                     "pool_scale": pool_scale, "g_ffn": g_ffn, "g_final": g_final, "b_ada": b_ada}
    m_small = {"w_uk": m_w_uk, "w_uv": m_w_uv, "w_pool": m_w_pool, "g_mix": m_g_mix, "g_q": m_g_q, "g_kv": m_g_kv,
               "pool_scale": m_pool_scale, "g_ffn": m_g_ffn, "g_final": m_g_final, "b_ada": m_b_ada}
    v_small = {"w_uk": v_w_uk, "w_uv": v_w_uv, "w_pool": v_w_pool, "g_mix": v_g_mix, "g_q": v_g_q, "g_kv": v_g_kv,
               "pool_scale": v_pool_scale, "g_ffn": v_g_ffn, "g_final": v_g_final, "b_ada": v_b_ada}
    pack = lambda d: _pack_rows([d[n] for n in SMALL_NAMES])
    _, d_s, m_s, v_s = _adamw(pack(weights_small), pack(grads_small), pack(m_small), pack(v_small), "adamw_small",
                              g_is_landing_zone=False)

    def unpack(flat2d):
        flat = flat2d.reshape(-1)
        out, o = {}, 0
        for n in SMALL_NAMES:
            size = weights_small[n].size
            out[n] = flat[o:o + size].reshape(weights_small[n].shape)
            o += size
        return out

    delta_s, newm_s, newv_s = unpack(d_s), unpack(m_s), unpack(v_s)

    for n in SMALL_NAMES:
        grad_out[n] = grads_small[n].reshape(weights_small[n].shape)
        delta_out[n], newm_out[n], newv_out[n] = delta_s[n], newm_s[n], newv_s[n]

    order = ("w_ada", "b_ada", "g_mix", "w_in", "g_q", "g_kv", "w_uq", "w_uk", "w_uv", "w_pool", "pool_scale", "w_o",
             "g_ffn", "w_gate", "w_up", "w_down", "g_final")
    return (loss, grad_x.reshape(x.shape), *[grad_out[n] for n in order], *[delta_out[n] for n in order],
            *[newm_out[n] for n in order], *[newv_out[n] for n in order])
```

```python
import functools

import jax
import jax.numpy as jnp
from jax import lax
from jax.experimental import pallas as pl
from jax.experimental.pallas import tpu as pltpu

F32 = jnp.float32
BF16 = jnp.bfloat16

D_MODEL = 1024
HEADS = 4
NOPE = 128
ROPE = 64
HEAD_QK = NOPE + ROPE
Q_LORA = 256
KV_LORA = 128
POOL_W = 512
POOL_WINDOWS = (2, 4, 8, 16)
POOL_GROUP = 128
POOL_PAD = 16
D_FF = 2816
N_CHIPS = 4
FF_CHUNK = D_FF // N_CHIPS
N_MOD = 6
EPS = 1e-6
SM_SCALE = HEAD_QK ** -0.5
ROPE_THETA = 10000.0
QK_PAD = 256
CHUNK = 64
CHUNK_SHIFT = 6

ADAM_LR = 0.001
ADAM_B1 = 0.9
ADAM_B2 = 0.999
ADAM_EPS = 1e-08
ADAM_WD = 0.01
ADAM_STEP = 10

VMEM_LIMIT = 48 * 1024 * 1024
MESH = pl.DeviceIdType.MESH
ANY = pl.BlockSpec(memory_space=pl.ANY)
VMEM_SPEC = pl.BlockSpec(memory_space=pltpu.VMEM)

PROJ_W = 1024
O_CKV = 256
O_KR = 384
O_U = 512
Q_W = 768
O_QA = 512
O_QB = 640


def _params(sem=None, vmem=VMEM_LIMIT):
    kw = dict(vmem_limit_bytes=vmem)
    if sem is not None:
        kw["dimension_semantics"] = sem
    return pltpu.CompilerParams(**kw)


def _dot(a, b):
    return jnp.dot(a.astype(BF16), b.astype(BF16), preferred_element_type=F32)


def _dot_nt(a, b):
    return lax.dot_general(a.astype(BF16), b.astype(BF16), (((1,), (1,)), ((), ())), preferred_element_type=F32)


def _dot_tn(a, b):
    return lax.dot_general(a.astype(BF16), b.astype(BF16), (((0,), (0,)), ((), ())), preferred_element_type=F32)


def _row_tile(rows, target):
    best = rows
    for t in range(8, min(rows, target) + 1, 8):
        if rows % t == 0:
            best = t
    return best if rows % best == 0 and best <= target else rows


def _rms(x):
    r = lax.rsqrt(jnp.mean(x * x, axis=-1, keepdims=True) + EPS)
    return x * r, r


def _rms_bwd(dxh, xh, r):
    return r * (dxh - xh * jnp.mean(dxh * xh, axis=-1, keepdims=True))


def _lane_first_half(shape):
    lane = lax.broadcasted_iota(jnp.int32, shape, 1)
    return (lane & (ROPE - 1)) < (ROPE // 2)


def _rope(a, cos, sin):
    first = _lane_first_half(a.shape)
    up = pltpu.roll(a, 96, 1)
    dn = pltpu.roll(a, 32, 1)
    return a * cos + jnp.where(first, -up, dn) * sin


def _rope_bwd(d, cos, sin):
    first = _lane_first_half(d.shape)
    up = pltpu.roll(d, 96, 1)
    dn = pltpu.roll(d, 32, 1)
    return d * cos + jnp.where(first, up, -dn) * sin


RELATIONS = tuple((dx, dy, dc) for dx in (0, 1) for dy in (0, 1) for dc in (0, 1) if (dx, dy, dc) != (0, 0, 0))
CHIP_RELATIONS = ((1, 0), (0, 1), (1, 1))


def _flip(v, d):
    return 1 - v if d else v


def _place():
    return lax.axis_index("x"), lax.axis_index("y"), lax.axis_index("c")


def _remote(src, dst, send_sem, recv_sem, target):
    return pltpu.make_async_remote_copy(src_ref=src, dst_ref=dst, send_sem=send_sem, recv_sem=recv_sem,
                                        device_id=target, device_id_type=MESH)


def _mod_exchange(c_row, w_ada, b_ada):
    cols = w_ada.shape[1]

    def body(c_ref, w_ref, b_ref, mod_ref, call_ref, part_ref, send1, recv1, loc1, send2, recv2, loc2):
        x, y, c = _place()
        me = 4 * x + 2 * y + c
        own = pltpu.make_async_copy(c_ref, call_ref.at[pl.ds(me, 1)], loc1)
        own.start()
        sends = []
        for k, (dx, dy, dc) in enumerate(RELATIONS):
            cp = _remote(c_ref, call_ref.at[pl.ds(me, 1)], send1.at[k], recv1.at[k],
                         (_flip(x, dx), _flip(y, dy), _flip(c, dc)))
            cp.start()
            sends.append(cp)
        for k, (dx, dy, dc) in enumerate(RELATIONS):
            src = 4 * _flip(x, dx) + 2 * _flip(y, dy) + _flip(c, dc)
            _remote(c_ref, call_ref.at[pl.ds(src, 1)], send1.at[k], recv1.at[k], (x, y, c)).wait_recv()
        own.wait()
        for cp in sends:
            cp.wait_send()
        call = call_ref[...]
        act = call * jax.nn.sigmoid(call)
        part_ref[...] = _dot(act, w_ref[...]) + b_ref[...]
        chip = 2 * x + y
        mine = pltpu.make_async_copy(part_ref.at[pl.ds(me, 1)], mod_ref.at[pl.ds(chip, 1)], loc2)
        mine.start()
        sends = []
        for k, (dx, dy) in enumerate(CHIP_RELATIONS):
            tx, ty = _flip(x, dx), _flip(y, dy)
            tb = 4 * tx + 2 * ty + c
            cp = _remote(part_ref.at[pl.ds(tb, 1)], mod_ref.at[pl.ds(chip, 1)], send2.at[k], recv2.at[k], (tx, ty, c))
            cp.start()
            sends.append(cp)
        for k, (dx, dy) in enumerate(CHIP_RELATIONS):
            src_chip = 2 * _flip(x, dx) + _flip(y, dy)
            _remote(part_ref.at[pl.ds(me, 1)], mod_ref.at[pl.ds(src_chip, 1)], send2.at[k], recv2.at[k],
                    (x, y, c)).wait_recv()
        mine.wait()
        for cp in sends:
            cp.wait_send()

    return pl.pallas_call(
        body, name="mod_exchange",
        out_shape=[jax.ShapeDtypeStruct((N_CHIPS, cols), F32), jax.ShapeDtypeStruct((8, D_MODEL), F32)],
        in_specs=[VMEM_SPEC, VMEM_SPEC, VMEM_SPEC], out_specs=[VMEM_SPEC, VMEM_SPEC],
        scratch_shapes=[pltpu.VMEM((8, cols), F32),
                        pltpu.SemaphoreType.DMA((7,)), pltpu.SemaphoreType.DMA((7,)), pltpu.SemaphoreType.DMA,
                        pltpu.SemaphoreType.DMA((3,)), pltpu.SemaphoreType.DMA((3,)), pltpu.SemaphoreType.DMA],
        compiler_params=_params(),
    )(c_row, w_ada, b_ada)


HBM_SPEC = pl.BlockSpec(memory_space=pltpu.HBM)
SEM_SPEC = pl.BlockSpec(memory_space=pltpu.SEMAPHORE)
DATAFLOW = pltpu.SideEffectType.DATAFLOW_SIDE_EFFECTING


def _in_hbm(a):
    return pltpu.with_memory_space_constraint(a, pltpu.HBM)


def _hbm(*arrays):
    return tuple(_in_hbm(a) for a in arrays)


def _hbm_like(arrays):
    return [pltpu.HBM(a.shape, a.dtype) for a in arrays]


def _split_start(name, srcs, lands, n_remote, plan):
    lands = [lax.empty(a.shape, a.dtype) if isinstance(a, jax.ShapeDtypeStruct) else a for a in lands]
    n, m = len(srcs), len(lands)

    def body(*refs):
        src_refs, land_refs = refs[:n], refs[n:n + m]
        send_sems, recv_sems, token = refs[n + m], refs[n + m + 1], refs[n + 2 * m + 2]
        remote = plan(_place(), src_refs, land_refs)
        assert len(remote) == n_remote
        for i, (s, d, target) in enumerate(remote):
            _remote(s, d, send_sems.at[i], recv_sems.at[i], target).start()
        token[...] = jnp.zeros_like(token)

    res = pl.pallas_call(
        body, name=name,
        out_shape=(pltpu.SemaphoreType.DMA((n_remote,)), pltpu.SemaphoreType.DMA((n_remote,)),
                   *_hbm_like(lands), jax.ShapeDtypeStruct((8, 128), F32)),
        in_specs=[HBM_SPEC] * (n + m),
        out_specs=(SEM_SPEC, SEM_SPEC, *([HBM_SPEC] * m), VMEM_SPEC),
        input_output_aliases={n + i: 2 + i for i in range(m)},
        compiler_params=pltpu.CompilerParams(has_side_effects=DATAFLOW),
    )(*[_in_hbm(a) for a in srcs], *[_in_hbm(a) for a in lands])
    return res[0], res[1], list(res[2:2 + m]), res[2 + m]


def _split_wait(name, send_sems, recv_sems, srcs, lands, after, plan):
    n, m = len(srcs), len(lands)

    def body(*refs):
        src_refs, land_refs = refs[:n], refs[n:n + m]
        send_sems, recv_sems = refs[n + m], refs[n + m + 1]
        place = _place()
        for i, (s, d) in enumerate(plan(place, src_refs, land_refs)):
            cp = _remote(s, d, send_sems.at[i], recv_sems.at[i], place)
            cp.wait_send()
            cp.wait_recv()

    res = pl.pallas_call(
        body, name=name,
        out_shape=tuple(_hbm_like(lands)),
        in_specs=[HBM_SPEC] * (n + m) + [SEM_SPEC, SEM_SPEC, ANY],
        out_specs=tuple([HBM_SPEC] * m),
        input_output_aliases={n + i: i for i in range(m)},
        compiler_params=pltpu.CompilerParams(has_side_effects=DATAFLOW),
    )(*srcs, *lands, send_sems, recv_sems, after)
    return list(res)


def _split_relay(name, send_sems, recv_sems, srcs, lands, after, n_remote, plan_wait, plan_send):
    n, m = len(srcs), len(lands)

    def body(*refs):
        src_refs, land_refs = refs[:n], refs[n:n + m]
        old_send, old_recv = refs[n + m], refs[n + m + 1]
        new_send, new_recv = refs[n + m + 3], refs[n + m + 4]
        token = refs[n + m + 5 + m]
        place = _place()
        for i, (s, d) in enumerate(plan_wait(place, src_refs, land_refs)):
            cp = _remote(s, d, old_send.at[i], old_recv.at[i], place)
            cp.wait_send()
            cp.wait_recv()
        for i, (s, d, target) in enumerate(plan_send(place, land_refs)):
            _remote(s, d, new_send.at[i], new_recv.at[i], target).start()
        token[...] = jnp.zeros_like(token)

    res = pl.pallas_call(
        body, name=name,
        out_shape=(pltpu.SemaphoreType.DMA((n_remote,)), pltpu.SemaphoreType.DMA((n_remote,)),
                   *_hbm_like(lands), jax.ShapeDtypeStruct((8, 128), F32)),
        in_specs=[HBM_SPEC] * (n + m) + [SEM_SPEC, SEM_SPEC, ANY],
        out_specs=(SEM_SPEC, SEM_SPEC, *([HBM_SPEC] * m), VMEM_SPEC),
        input_output_aliases={n + i: 2 + i for i in range(m)},
        compiler_params=pltpu.CompilerParams(has_side_effects=DATAFLOW),
    )(*srcs, *lands, send_sems, recv_sems, after)
    return res[0], res[1], list(res[2:2 + m]), res[2 + m]


def _half(ref, core, axis=0):
    hr = ref.shape[axis] // 2
    return pl.ds(core * hr, hr)


def _plan_gather_start(place, src, land):
    x, y, c = place
    chip = 2 * x + y
    return [(s.at[_half(s, c)], l.at[chip, _half(s, c)], (_flip(x, dx), _flip(y, dy), c))
            for s, l in zip(src, land) for dx, dy in CHIP_RELATIONS]


def _plan_gather_landed(place, src, land):
    x, y, c = place
    return [(s.at[_half(s, c)], l.at[2 * _flip(x, dx) + _flip(y, dy), _half(s, c)])
            for s, l in zip(src, land) for dx, dy in CHIP_RELATIONS]


def _plan_gather_relay(place, land):
    x, y, c = place
    out = []
    for l in land:
        for dx, dy in CHIP_RELATIONS:
            got = l.at[2 * _flip(x, dx) + _flip(y, dy), _half(l, c, 1)]
            out.append((got, got, (x, y, 1 - c)))
    return out


def _plan_gather_wait(place, src, land):
    x, y, c = place
    out = []
    for l in land:
        for dx, dy in CHIP_RELATIONS:
            got = l.at[2 * _flip(x, dx) + _flip(y, dy), _half(l, 1 - c, 1)]
            out.append((got, got))
    return out


def _plan_swap_start(place, src, land):
    x, y, c = place
    return [(s.at[:, _half(s, 1 - c, 1), :], l, (x, y, 1 - c)) for s, l in zip(src, land)]


def _plan_swap_wait(place, src, land):
    return [(s.at[:, _half(s, 0, 1), :], l) for s, l in zip(src, land)]


def _plan_exchange_start(place, src, land):
    x, y, c = place
    remote = []
    for s, l in zip(src, land):
        for k, (dx, dy) in enumerate(CHIP_RELATIONS):
            tx, ty = _flip(x, dx), _flip(y, dy)
            remote.append((s.at[2 * tx + ty], l.at[k], (tx, ty, c)))
    return remote


def _plan_exchange_wait(place, src, land):
    return [(s.at[0], l.at[k]) for s, l in zip(src, land) for k in range(3)]


def _plan_finish_start(place, src, land):
    x, y, c = place
    return [(l.at[c], l.at[c], (x, y, 1 - c)) for l in land]


def _plan_finish_wait(place, src, land):
    x, y, c = place
    return [(l.at[c], l.at[1 - c]) for l in land]


def _plan_near_finish_start(place, src, land):
    x, y, c = place
    mine = land[-1].at[2 * x + y, c]
    return (_plan_finish_start(place, src, land[:-1])
            + [(mine, mine, (_flip(x, dx), _flip(y, dy), _flip(c, dc))) for dx, dy, dc in RELATIONS])


def _plan_near_finish_wait(place, src, land):
    x, y, c = place
    mine = land[-1].at[2 * x + y, c]
    return (_plan_finish_wait(place, src, land[:-1])
            + [(mine, land[-1].at[2 * _flip(x, dx) + _flip(y, dy), _flip(c, dc)]) for dx, dy, dc in RELATIONS])


def _grad_swap_halves(grads, dmod):
    n = len(grads)

    def body(*refs):
        ins, dmod_ref = refs[:n], refs[n]
        outs, dall_ref = refs[n + 1:2 * n + 1], refs[2 * n + 1]
        send_sems, recv_sems, dsend, drecv, dloc = refs[2 * n + 2:]
        x, y, c = _place()
        me = 4 * x + 2 * y + c
        sends = []
        for w in range(n):
            hr = ins[w].shape[1] // 2
            cp = _remote(ins[w].at[:, pl.ds((1 - c) * hr, hr), :], outs[w], send_sems.at[w], recv_sems.at[w],
                         (x, y, 1 - c))
            cp.start()
            sends.append(cp)
        own = pltpu.make_async_copy(dmod_ref, dall_ref.at[me], dloc)
        own.start()
        for k, (dx, dy, dc) in enumerate(RELATIONS):
            cp = _remote(dmod_ref, dall_ref.at[me], dsend.at[k], drecv.at[k],
                         (_flip(x, dx), _flip(y, dy), _flip(c, dc)))
            cp.start()
            sends.append(cp)
        for k, (dx, dy, dc) in enumerate(RELATIONS):
            src = 4 * _flip(x, dx) + 2 * _flip(y, dy) + _flip(c, dc)
            _remote(dmod_ref, dall_ref.at[src], dsend.at[k], drecv.at[k], (x, y, c)).wait_recv()
        for w in range(n):
            _remote(outs[w], outs[w], send_sems.at[w], recv_sems.at[w], (x, y, c)).wait_recv()
        own.wait()
        for cp in sends:
            cp.wait_send()

    out_shape = [pltpu.HBM((N_CHIPS, g.shape[1] // 2, g.shape[2]), F32) for g in grads]
    out_shape.append(pltpu.HBM((8,) + dmod.shape, F32))
    res = pl.pallas_call(
        body, name="grad_swap_halves",
        out_shape=out_shape, in_specs=[ANY] * n + [VMEM_SPEC], out_specs=[ANY] * (n + 1),
        scratch_shapes=[pltpu.SemaphoreType.DMA((n,)), pltpu.SemaphoreType.DMA((n,)),
                        pltpu.SemaphoreType.DMA((7,)), pltpu.SemaphoreType.DMA((7,)), pltpu.SemaphoreType.DMA],
        compiler_params=_params(),
    )(*grads, dmod)
    return res[:n], res[n]


def _add_my_halves(core, fulls, gots, name):
    n = len(fulls)

    def body(core_ref, *refs):
        for w in range(n):
            refs[2 * n + w][...] = refs[w][...] + refs[n + w][...]

    mine = lambda g: pl.BlockSpec((None,) + g.shape[1:], lambda s, core_ref: (s, core_ref[0], 0))
    slab = lambda g: pl.BlockSpec((None,) + g.shape[1:], lambda s, core_ref: (s, 0, 0))
    return list(pl.pallas_call(
        body, name=name,
        out_shape=[pltpu.HBM(g.shape, F32) for g in gots],
        grid_spec=pltpu.PrefetchScalarGridSpec(
            num_scalar_prefetch=1, grid=(N_CHIPS,),
            in_specs=[mine(g) for g in gots] + [slab(g) for g in gots],
            out_specs=[slab(g) for g in gots]),
        compiler_params=_params(("arbitrary",)),
    )(core, *_hbm(*fulls, *gots)))


def _add_chips_into_pairs(chip_core, mines, gots, name):
    n = len(mines)

    def body(cc_ref, *refs):
        for w in range(n):
            b_ref = refs[n + w]
            refs[2 * n + w][...] = ((refs[w][...] + b_ref[0]) + b_ref[1]) + b_ref[2]

    return list(pl.pallas_call(
        body, name=name,
        out_shape=[pltpu.HBM((2,) + m.shape[1:], F32) for m in mines],
        grid_spec=pltpu.PrefetchScalarGridSpec(
            num_scalar_prefetch=1, grid=(1,),
            in_specs=[pl.BlockSpec((None,) + m.shape[1:], lambda s, cc_ref: (cc_ref[0], 0, 0)) for m in mines]
            + [pl.BlockSpec(g.shape, lambda s, cc_ref: (0, 0, 0)) for g in gots],
            out_specs=[pl.BlockSpec((None,) + m.shape[1:], lambda s, cc_ref: (cc_ref[1], 0, 0)) for m in mines]),
        compiler_params=_params(("arbitrary",)),
    )(chip_core, *_hbm(*mines, *gots)))


def _add_chips_into_grid(chip_core, mine, got, name):
    _, hr, cols = mine.shape

    def body(cc_ref, a_ref, b_ref, o_ref):
        o_ref[...] = ((a_ref[...] + b_ref[0]) + b_ref[1]) + b_ref[2]

    return pl.pallas_call(
        body, name=name,
        out_shape=pltpu.HBM((N_CHIPS, 2, hr, cols), F32),
        grid_spec=pltpu.PrefetchScalarGridSpec(
            num_scalar_prefetch=1, grid=(1,),
            in_specs=[pl.BlockSpec((None, hr, cols), lambda s, cc_ref: (cc_ref[0], 0, 0)),
                      pl.BlockSpec((3, hr, cols), lambda s, cc_ref: (0, 0, 0))],
            out_specs=pl.BlockSpec((None, None, hr, cols), lambda s, cc_ref: (cc_ref[0], cc_ref[1], 0, 0))),
        compiler_params=_params(("arbitrary",)),
    )(chip_core, *_hbm(mine, got))


def _place_shards(chip, shards):
    n = len(shards)

    def body(chip_ref, *refs):
        for w in range(n):
            refs[n + w][...] = refs[w][...]

    return pl.pallas_call(
        body, name="place_shards",
        out_shape=[pltpu.HBM((N_CHIPS,) + s.shape, s.dtype) for s in shards],
        grid_spec=pltpu.PrefetchScalarGridSpec(
            num_scalar_prefetch=1, grid=(1,),
            in_specs=[pl.BlockSpec(s.shape, lambda i, chip_ref: (0, 0)) for s in shards],
            out_specs=[pl.BlockSpec((None,) + s.shape, lambda i, chip_ref: (chip_ref[0], 0, 0)) for s in shards]),
        compiler_params=_params(("arbitrary",)),
    )(chip, *shards)


def _rope_tables(pos_col, freqs):
    S = pos_col.shape[0]
    T = _row_tile(S, 1024)

    def body(p_ref, f_ref, cos_ref, sin_ref):
        ang = p_ref[...].astype(F32) * f_ref[...]
        cos_ref[...] = jnp.cos(ang)
        sin_ref[...] = jnp.sin(ang)

    return pl.pallas_call(
        body, name="rope_tables", grid=(S // T,),
        out_shape=[pltpu.HBM((S, 128), F32)] * 2,
        in_specs=[pl.BlockSpec((T, 1), lambda i: (i, 0)), pl.BlockSpec((1, 128), lambda i: (0, 0))],
        out_specs=[pl.BlockSpec((T, 128), lambda i: (i, 0))] * 2,
        compiler_params=_params(("parallel",)),
    )(*_hbm(pos_col, freqs))


def _full(shape):
    zeros = (0,) * len(shape)
    return pl.BlockSpec(shape, lambda *_: zeros)


def _pre_attention(x, mod6, g_mix, g_q, g_kv, w_in, w_uq, w_uk_t, cos, sin, T, TQ):
    S = x.shape[0]

    def body(x_ref, mod_ref, gm_ref, gq_ref, gkv_ref, win_ref, wuq_ref, wuk_ref, cos_ref, sin_ref,
             proj_ref, q_ref, qc_ref, kc_ref, kct_ref):
        xh, _ = _rms(x_ref[...])
        h1 = ((xh * gm_ref[...]) * (1.0 + mod_ref[1:2, :]) + mod_ref[0:1, :]).astype(BF16)
        rows_in = D_MODEL // N_CHIPS
        proj = _dot_nt(h1[:, 0:rows_in], win_ref[0])
        for j in range(1, N_CHIPS):
            proj = proj + _dot_nt(h1[:, j * rows_in:(j + 1) * rows_in], win_ref[j])
        proj_ref[...] = proj
        cqh, _ = _rms(proj[:, :Q_LORA])
        c_q = cqh * gq_ref[...]
        ckvh, _ = _rms(proj[:, O_CKV:O_KR])
        c_kv = ckvh * gkv_ref[...]
        q = _dot(c_q, wuq_ref[...])
        q_ref[...] = q.astype(BF16)
        cos_t, sin_t = cos_ref[...], sin_ref[...]
        ropes = (_rope(q[:, O_QA:O_QB], cos_t, sin_t), _rope(q[:, O_QB:Q_W], cos_t, sin_t))
        low = lax.broadcasted_iota(jnp.int32, (T, 128), 1) < ROPE
        for h in range(HEADS):
            q_lat = _dot_nt(q[:, h * NOPE:(h + 1) * NOPE], wuk_ref[h])
            keep = low if h % 2 == 0 else jnp.logical_not(low)
            qc_ref[h, :, 0:KV_LORA] = q_lat.astype(BF16)
            qc_ref[h, :, KV_LORA:QK_PAD] = jnp.where(keep, ropes[h // 2], 0.0).astype(BF16)
        k_rope = _rope(proj[:, O_KR:O_U], cos_t, sin_t)
        kc_ref[:, 0:KV_LORA] = c_kv.astype(BF16)
        kc_ref[:, KV_LORA:QK_PAD] = k_rope.astype(BF16)
        lat_t, rope_t = jnp.transpose(c_kv), jnp.transpose(k_rope)
        for s in range(T // TQ):
            kct_ref[s, 0:KV_LORA, :] = lat_t[:, s * TQ:(s + 1) * TQ].astype(BF16)
            kct_ref[s, KV_LORA:QK_PAD, :] = rope_t[:, s * TQ:(s + 1) * TQ].astype(BF16)

    row = lambda w: pl.BlockSpec((T, w), lambda i: (i, 0))
    return pl.pallas_call(
        body, name="pre_attention", grid=(S // T,),
        out_shape=[pltpu.HBM((S, PROJ_W), F32), pltpu.HBM((S, Q_W), BF16), pltpu.HBM((HEADS, S, QK_PAD), BF16),
                   pltpu.HBM((S, QK_PAD), BF16), pltpu.HBM((S // TQ, QK_PAD, TQ), BF16)],
        in_specs=[row(D_MODEL), _full((N_MOD, D_MODEL)), _full((1, D_MODEL)), _full((1, Q_LORA)), _full((1, KV_LORA)),
                  _full((N_CHIPS, PROJ_W, D_MODEL // N_CHIPS)), _full((Q_LORA, Q_W)), _full((HEADS, KV_LORA, NOPE)),
                  row(128), row(128)],
        out_specs=[row(PROJ_W), row(Q_W), pl.BlockSpec((HEADS, T, QK_PAD), lambda i: (0, i, 0)), row(QK_PAD),
                   pl.BlockSpec((T // TQ, QK_PAD, TQ), lambda i: (i, 0, 0))],
        compiler_params=_params(("parallel",)),
    )(*_hbm(x, mod6, g_mix, g_q, g_kv, w_in, w_uq, w_uk_t, cos, sin))


def _diag_mask(TQ, width):
    key = lax.broadcasted_iota(jnp.int32, (TQ, width), 0) >> CHUNK_SHIFT
    qry = (lax.broadcasted_iota(jnp.int32, (TQ, width), 1) & (TQ - 1)) >> CHUNK_SHIFT
    return key <= qry


def _col_to_row(col):
    return jnp.transpose(jnp.broadcast_to(col, (col.shape[0], 128)))[0:1, :]


def _attention_fwd(qc, kc, kct, w_uv_t, TQ):
    S = kc.shape[0]
    R = HEADS * TQ
    nq = S // TQ

    def body(q_ref, k_ref, kt_ref, wuv_ref, o_ref, y_ref, lser_ref, m_s, l_s, acc_s, st_s):
        i = pl.program_id(0)
        q = q_ref[...].reshape(R, QK_PAD)
        m_s[...] = jnp.full((1, R), -jnp.inf, F32)
        l_s[...] = jnp.zeros((1, R), F32)
        acc_s[...] = jnp.zeros((KV_LORA, R), F32)

        def scores(j):
            return _dot_nt(k_ref[pl.ds(pl.multiple_of(j * TQ, TQ), TQ), :], q) * SM_SCALE

        def update(j, st):
            m_old = m_s[...]
            m_new = jnp.maximum(m_old, jnp.max(st, axis=0, keepdims=True))
            pt = jnp.exp(st - m_new)
            alpha = jnp.exp(m_old - m_new)
            l_s[...] = alpha * l_s[...] + jnp.sum(pt, axis=0, keepdims=True)
            acc_s[...] = alpha * acc_s[...] + _dot(kt_ref[j, 0:KV_LORA, :], pt)
            m_s[...] = m_new

        st_s[...] = scores(0)

        def loop(j, carry):
            st = st_s[...]
            st_s[...] = scores(j + 1)
            update(j, st)
            return carry

        lax.fori_loop(0, i, loop, 0)
        update(i, jnp.where(_diag_mask(TQ, R), st_s[...], -jnp.inf))
        l = l_s[...]
        lser_ref[0] = m_s[...] + jnp.log(l)
        o = jnp.transpose(acc_s[...] / l).astype(BF16)
        for h in range(HEADS):
            oh = o[h * TQ:(h + 1) * TQ, :]
            o_ref[h] = oh
            y_ref[:, h * 128:(h + 1) * 128] = _dot(oh, wuv_ref[h]).astype(BF16)

    return pl.pallas_call(
        body, name="attention_fwd", grid=(nq,),
        out_shape=[pltpu.HBM((HEADS, S, KV_LORA), BF16), pltpu.HBM((S, HEADS * 128), BF16),
                   pltpu.HBM((nq, 1, R), F32)],
        in_specs=[pl.BlockSpec((HEADS, TQ, QK_PAD), lambda i: (0, i, 0)), _full((S, QK_PAD)),
                  _full((nq, QK_PAD, TQ)), _full((HEADS, KV_LORA, 128))],
        out_specs=[pl.BlockSpec((HEADS, TQ, KV_LORA), lambda i: (0, i, 0)), pl.BlockSpec((TQ, HEADS * 128), lambda i: (i, 0)),
                   pl.BlockSpec((1, 1, R), lambda i: (i, 0, 0))],
        scratch_shapes=[pltpu.VMEM((1, R), F32), pltpu.VMEM((1, R), F32), pltpu.VMEM((KV_LORA, R), F32),
                        pltpu.VMEM((TQ, R), F32)],
        compiler_params=_params(("parallel",)),
    )(*_hbm(qc, kc, kct, w_uv_t))


def _pool_forward(proj):
    S = proj.shape[0]
    RB = _row_tile(S, 256)

    def body(proj_ref, out_ref, pad_ref, sem):
        cp = pltpu.make_async_copy(proj_ref.at[:, pl.ds(O_U, POOL_W)], pad_ref.at[pl.ds(POOL_PAD, S)], sem)
        cp.start()
        pad_ref[0:POOL_PAD, :] = jnp.zeros((POOL_PAD, POOL_W), F32)
        cp.wait()
        for g, win in enumerate(POOL_WINDOWS):
            cols = slice(g * POOL_GROUP, (g + 1) * POOL_GROUP)
            for r0 in range(0, S, RB):
                u = pad_ref[POOL_PAD + r0:POOL_PAD + r0 + RB, cols]
                acc = u
                for k in range(1, win):
                    acc = acc + pad_ref[POOL_PAD + r0 - k:POOL_PAD + r0 - k + RB, cols]
                if r0 == 0:
                    t1 = (lax.broadcasted_iota(jnp.int32, (RB, POOL_GROUP), 0) + 1).astype(F32)
                    mean = acc / jnp.minimum(t1, float(win))
                else:
                    mean = acc * (1.0 / win)
                out_ref[r0:r0 + RB, cols] = (mean - u).astype(BF16)

    return pl.pallas_call(
        body, name="pool_forward",
        out_shape=jax.ShapeDtypeStruct((S, POOL_W), BF16),
        in_specs=[ANY], out_specs=VMEM_SPEC,
        scratch_shapes=[pltpu.VMEM((S + POOL_PAD, POOL_W), F32), pltpu.SemaphoreType.DMA],
        compiler_params=_params(),
    )(proj)


def _pool_backward(dpooled, after):
    S = dpooled.shape[0]
    RB = _row_tile(S, 256)

    def body(dp_ref, after_ref, out_ref, pad_ref, sem):
        cp = pltpu.make_async_copy(dp_ref, pad_ref.at[pl.ds(0, S)], sem)
        cp.start()
        pad_ref[S:S + POOL_PAD, :] = jnp.zeros((POOL_PAD, POOL_W), F32)
        cp.wait()
        for g, win in enumerate(POOL_WINDOWS):
            cols = slice(g * POOL_GROUP, (g + 1) * POOL_GROUP)
            head = pad_ref[0:POOL_PAD, cols]
            t1 = (lax.broadcasted_iota(jnp.int32, (POOL_PAD, POOL_GROUP), 0) + 1).astype(F32)
            pad_ref[0:POOL_PAD, cols] = head * (float(win) / jnp.minimum(t1, float(win)))
            for r0 in range(0, S, RB):
                acc = pad_ref[r0:r0 + RB, cols]
                for k in range(1, win):
                    acc = acc + pad_ref[r0 + k:r0 + k + RB, cols]
                own = pad_ref[r0:r0 + RB, cols]
                if r0 == 0:
                    own = jnp.concatenate([head, own[POOL_PAD:]], axis=0)
                out_ref[r0:r0 + RB, cols] = (acc * (1.0 / win) - own).astype(BF16)

    return pl.pallas_call(
        body, name="pool_backward",
        out_shape=jax.ShapeDtypeStruct((S, POOL_W), BF16),
        in_specs=[ANY, ANY], out_specs=VMEM_SPEC,
        scratch_shapes=[pltpu.VMEM((S + POOL_PAD, POOL_W), F32), pltpu.SemaphoreType.DMA],
        compiler_params=_params(),
    )(dpooled, after)


def _mix_out(y_mla, pooled, w_pool, pool_scale, w_o, x, mod6, T):
    S = x.shape[0]

    def body(ym_ref, pl_ref, wp_ref, ps_ref, wo_ref, x_ref, mod_ref, x1_ref, mix_ref, mi_ref):
        mi_ref[:, 0:512] = ym_ref[...]
        for g in range(len(POOL_WINDOWS)):
            cols = slice(g * POOL_GROUP, (g + 1) * POOL_GROUP)
            z = _dot(pl_ref[:, cols], wp_ref[g])
            mi_ref[:, 512 + g * POOL_GROUP:512 + (g + 1) * POOL_GROUP] = (z * ps_ref[:, cols]).astype(BF16)
        mix = _dot(mi_ref[...], wo_ref[...])
        mix_ref[...] = mix.astype(BF16)
        x1_ref[...] = x_ref[...] + mod_ref[2:3, :] * mix

    row = lambda w: pl.BlockSpec((T, w), lambda i: (i, 0))
    return pl.pallas_call(
        body, name="mix_out", grid=(S // T,),
        out_shape=[pltpu.HBM((S, D_MODEL), F32), pltpu.HBM((S, D_MODEL), BF16), pltpu.HBM((S, 1024), BF16)],
        in_specs=[row(512), row(POOL_W), _full((4, POOL_GROUP, POOL_GROUP)), _full((1, POOL_W)),
                  _full((1024, D_MODEL)), row(D_MODEL), _full((N_MOD, D_MODEL))],
        out_specs=[row(D_MODEL), row(D_MODEL), row(1024)],
        compiler_params=_params(("parallel",)),
    )(*_hbm(y_mla, pooled, w_pool, pool_scale, w_o, x, mod6))


def _ffn_forward(x1, mod6, g_ffn, g_final, target, w_gate, w_up, w_down, T):
    S = x1.shape[0]

    def body(x1_ref, mod_ref, gf_ref, gl_ref, tgt_ref, wg_ref, wu_ref, wd_ref,
             gate_ref, up_ref, act_ref, h2_ref, dff_ref, dx2_ref, st_ref, acc_s):
        i, j = pl.program_id(0), pl.program_id(1)

        @pl.when(jnp.logical_and(i == 0, j == 0))
        def _():
            st_ref[...] = jnp.zeros_like(st_ref)

        @pl.when(j == 0)
        def _():
            xh, _ = _rms(x1_ref[...])
            h2_ref[...] = ((xh * gf_ref[...]) * (1.0 + mod_ref[4:5, :]) + mod_ref[3:4, :]).astype(BF16)
            acc_s[...] = jnp.zeros_like(acc_s)

        h2 = h2_ref[...]
        gate = _dot_nt(h2, wg_ref[j])
        up = _dot_nt(h2, wu_ref[j])
        gate_ref[...] = gate.astype(BF16)
        up_ref[...] = up.astype(BF16)
        act = (gate * jax.nn.sigmoid(gate) * up).astype(BF16)
        act_ref[...] = act
        acc_s[...] += _dot(act, wd_ref[j])

        @pl.when(j == N_CHIPS - 1)
        def _():
            ff = acc_s[...]
            x2 = x1_ref[...] + mod_ref[5:6, :] * ff
            xh, r3 = _rms(x2)
            err = xh * gl_ref[...] - tgt_ref[...]
            dy = err * (1.0 / D_MODEL)
            dx2 = _rms_bwd(dy * gl_ref[...], xh, r3)
            dx2_ref[...] = dx2
            dff_ref[...] = (dx2 * mod_ref[5:6, :]).astype(BF16)
            st_ref[0:1, :] += jnp.sum(dy * xh, axis=0, keepdims=True)
            st_ref[1:2, :] += jnp.sum(dx2 * ff, axis=0, keepdims=True)
            st_ref[2:3, :] += 0.5 * jnp.sum(err * dy)

    row = pl.BlockSpec((T, D_MODEL), lambda i, j: (i, 0))
    chunk_out = pl.BlockSpec((None, T, FF_CHUNK), lambda i, j: (j, i, 0))
    big = pltpu.HBM((N_CHIPS, S, FF_CHUNK), BF16)
    wide = pltpu.HBM((S, D_MODEL), BF16)
    return pl.pallas_call(
        body, name="ffn_forward", grid=(S // T, N_CHIPS),
        out_shape=[big, big, big, wide, wide, pltpu.HBM((S, D_MODEL), F32), jax.ShapeDtypeStruct((8, D_MODEL), F32)],
        in_specs=[row, _full((N_MOD, D_MODEL)), _full((1, D_MODEL)), _full((1, D_MODEL)), row,
                  VMEM_SPEC, VMEM_SPEC, VMEM_SPEC],
        out_specs=[chunk_out, chunk_out, chunk_out, row, row, row, _full((8, D_MODEL))],
        scratch_shapes=[pltpu.VMEM((T, D_MODEL), F32)],
        compiler_params=_params(("arbitrary", "arbitrary")),
    )(*_hbm(x1, mod6, g_ffn, g_final, target), w_gate, w_up, w_down)


def _ffn_backward(dx2, x1, dff, gate, up, mod6, g_ffn, w_gate, w_up, w_down, T):
    S = x1.shape[0]

    def body(dx2_ref, x1_ref, dff_ref, gate_ref, up_ref, mod_ref, gf_ref, wg_ref, wu_ref, wd_ref,
             dgate_ref, dup_ref, dx1_ref, st_ref, acc_s):
        i, j = pl.program_id(0), pl.program_id(1)

        @pl.when(jnp.logical_and(i == 0, j == 0))
        def _():
            st_ref[...] = jnp.zeros_like(st_ref)

        @pl.when(j == 0)
        def _():
            acc_s[...] = jnp.zeros_like(acc_s)

        for r0 in range(0, T, T // 2):
            rows = slice(r0, r0 + T // 2)
            gate, up = gate_ref[rows, :].astype(F32), up_ref[rows, :].astype(F32)
            sg = jax.nn.sigmoid(gate)
            dact = _dot_nt(dff_ref[rows, :], wd_ref[j])
            dup = (dact * (gate * sg)).astype(BF16)
            dgate = (dact * up * (sg * (1.0 + gate * (1.0 - sg)))).astype(BF16)
            dup_ref[rows, :] = dup
            dgate_ref[rows, :] = dgate
            acc_s[rows, :] += _dot(dgate, wg_ref[j]) + _dot(dup, wu_ref[j])

        @pl.when(j == N_CHIPS - 1)
        def _():
            dh2 = acc_s[...]
            xh, r2 = _rms(x1_ref[...])
            n2 = xh * gf_ref[...]
            st_ref[0:1, :] += jnp.sum(dh2, axis=0, keepdims=True)
            st_ref[1:2, :] += jnp.sum(dh2 * n2, axis=0, keepdims=True)
            dn2 = dh2 * (1.0 + mod_ref[4:5, :])
            st_ref[2:3, :] += jnp.sum(dn2 * xh, axis=0, keepdims=True)
            dx1_ref[...] = _rms_bwd(dn2 * gf_ref[...], xh, r2) + dx2_ref[...]

    row = pl.BlockSpec((T, D_MODEL), lambda i, j: (i, 0))
    chunk = pl.BlockSpec((None, T, FF_CHUNK), lambda i, j: (j, i, 0))
    big = pltpu.HBM((N_CHIPS, S, FF_CHUNK), BF16)
    return pl.pallas_call(
        body, name="ffn_backward", grid=(S // T, N_CHIPS),
        out_shape=[big, big, pltpu.HBM((S, D_MODEL), F32), jax.ShapeDtypeStruct((8, D_MODEL), F32)],
        in_specs=[row, row, row, chunk, chunk, _full((N_MOD, D_MODEL)), _full((1, D_MODEL)),
                  VMEM_SPEC, VMEM_SPEC, VMEM_SPEC],
        out_specs=[chunk, chunk, row, _full((8, D_MODEL))],
        scratch_shapes=[pltpu.VMEM((T, D_MODEL), F32)],
        compiler_params=_params(("arbitrary", "arbitrary")),
    )(*_hbm(dx2, x1, dff, gate, up, mod6, g_ffn), w_gate, w_up, w_down)


def _tn_matmul(a, b, a_spec, b_spec, groups, m, n, steps, name):
    def body(a_ref, b_ref, o_ref):
        @pl.when(pl.program_id(1) == 0)
        def _():
            o_ref[...] = jnp.zeros_like(o_ref)

        o_ref[...] += _dot_tn(a_ref[...], b_ref[...])

    return pl.pallas_call(
        body, name=name, grid=(groups, steps),
        out_shape=pltpu.HBM((groups, m, n), F32),
        in_specs=[a_spec, b_spec],
        out_specs=pl.BlockSpec((None, m, n), lambda g, i: (g, 0, 0)),
        compiler_params=_params(("parallel", "arbitrary")),
    )(*_hbm(a, b))


def _mix_backward(dx1, mix, mod6, w_o, pooled, w_pool, pool_scale, w_uv_t, o_lat, T, TQ):
    S = dx1.shape[0]

    def body(dx1_ref, mix_ref, mod_ref, wo_ref, pl_ref, wp_ref, ps_ref, wuv_ref, o_ref,
             dmix_ref, dp_ref, do_ref, dr_ref, gp_ref, guv_ref, st_ref):
        @pl.when(pl.program_id(0) == 0)
        def _():
            st_ref[...] = jnp.zeros_like(st_ref)
            gp_ref[...] = jnp.zeros_like(gp_ref)
            guv_ref[...] = jnp.zeros_like(guv_ref)

        dx1 = dx1_ref[...]
        st_ref[0:1, :] += jnp.sum(dx1 * mix_ref[...].astype(F32), axis=0, keepdims=True)
        dmix = (dx1 * mod_ref[2:3, :]).astype(BF16)
        dmix_ref[...] = dmix
        dmi = _dot_nt(dmix, wo_ref[...])
        dym = dmi[:, 0:512].astype(BF16)
        for g in range(len(POOL_WINDOWS)):
            cols = slice(g * POOL_GROUP, (g + 1) * POOL_GROUP)
            dyp = dmi[:, 512 + g * POOL_GROUP:512 + (g + 1) * POOL_GROUP]
            pooled_g = pl_ref[:, cols]
            z = _dot(pooled_g, wp_ref[g])
            st_ref[1:2, cols] += jnp.sum(dyp * z, axis=0, keepdims=True)
            dz = (dyp * ps_ref[:, cols]).astype(BF16)
            gp_ref[g] += _dot_tn(pooled_g, dz)
            dp_ref[:, cols] = _dot_nt(dz, wp_ref[g])
        for h in range(HEADS):
            dym_h = dym[:, h * 128:(h + 1) * 128]
            do = _dot_nt(dym_h, wuv_ref[h]).astype(BF16)
            do_ref[h] = do
            o_h = o_ref[h]
            guv_ref[h] += _dot_tn(o_h, dym_h)
            delta = _col_to_row(jnp.sum(do.astype(F32) * o_h.astype(F32), axis=1, keepdims=True))
            for s in range(T // TQ):
                dr_ref[s, :, h * TQ:(h + 1) * TQ] = delta[:, s * TQ:(s + 1) * TQ]

    row = lambda w: pl.BlockSpec((T, w), lambda i: (i, 0))
    heads = pl.BlockSpec((HEADS, T, KV_LORA), lambda i: (0, i, 0))
    square = jax.ShapeDtypeStruct((4, 128, 128), F32)
    return pl.pallas_call(
        body, name="mix_backward", grid=(S // T,),
        out_shape=[pltpu.HBM((S, D_MODEL), BF16), pltpu.HBM((S, POOL_W), F32), pltpu.HBM((HEADS, S, KV_LORA), BF16),
                   pltpu.HBM((S // TQ, 1, HEADS * TQ), F32), square, square, jax.ShapeDtypeStruct((8, D_MODEL), F32)],
        in_specs=[row(D_MODEL), row(D_MODEL), _full((N_MOD, D_MODEL)), _full((1024, D_MODEL)), row(POOL_W),
                  _full((4, POOL_GROUP, POOL_GROUP)), _full((1, POOL_W)), _full((HEADS, KV_LORA, 128)), heads],
        out_specs=[row(D_MODEL), row(POOL_W), heads,
                   pl.BlockSpec((T // TQ, 1, HEADS * TQ), lambda i: (i, 0, 0)), _full((4, 128, 128)),
                   _full((4, 128, 128)), _full((8, D_MODEL))],
        compiler_params=_params(("arbitrary",)),
    )(*_hbm(dx1, mix, mod6, w_o, pooled, w_pool, pool_scale, w_uv_t, o_lat))


def _attention_bwd(qc, kc, kct, do, lse_rows, delta_rows, TQ):
    S = kc.shape[0]
    R = HEADS * TQ
    nq = S // TQ

    def body(q_ref, do_ref, lser_ref, dr_ref, k_ref, kt_ref, dqt_ref, dk_ref, dqt_s, dv_s):
        i = pl.program_id(0)

        def key_rows(j):
            return pl.ds(pl.multiple_of(j * TQ, TQ), TQ)

        @pl.when(i == 0)
        def _():
            def zero(j, carry):
                dk_ref[key_rows(j), :] = jnp.zeros((TQ, QK_PAD), F32)
                dv_s[key_rows(j), :] = jnp.zeros((TQ, KV_LORA), F32)
                return carry
            lax.fori_loop(0, nq, zero, 0)

        q = q_ref[...].reshape(R, QK_PAD)
        do = do_ref[...].reshape(R, KV_LORA)
        lse, delta = lser_ref[0], dr_ref[0]
        dqt_s[...] = jnp.zeros((QK_PAD, R), F32)

        def step(j, masked):
            rows = key_rows(j)
            k = k_ref[rows, :]
            st = _dot_nt(k, q) * SM_SCALE
            if masked:
                st = jnp.where(_diag_mask(TQ, R), st, -jnp.inf)
            pt = jnp.exp(st - lse)
            dv_s[rows, :] += _dot(pt, do)
            dpt = _dot_nt(k[:, :KV_LORA], do)
            dst = (pt * (dpt - delta)).astype(BF16)
            dk_ref[rows, :] += _dot(dst, q)
            dqt_s[...] += _dot(kt_ref[j], dst)

        def loop(j, carry):
            step(j, False)
            return carry

        lax.fori_loop(0, i, loop, 0)
        step(i, True)
        dqt_ref[...] = dqt_s[...]

        @pl.when(i == nq - 1)
        def _():
            def finish(j, carry):
                rows = key_rows(j)
                dk = dk_ref[rows, :] * SM_SCALE
                dk_ref[rows, 0:KV_LORA] = dk[:, 0:KV_LORA] + dv_s[rows, :]
                dk_ref[rows, KV_LORA:QK_PAD] = dk[:, KV_LORA:QK_PAD]
                return carry
            lax.fori_loop(0, nq, finish, 0)

    tile = lambda w: pl.BlockSpec((HEADS, TQ, w), lambda i: (0, i, 0))
    row = pl.BlockSpec((1, 1, R), lambda i: (i, 0, 0))
    return pl.pallas_call(
        body, name="attention_bwd", grid=(nq,),
        out_shape=[pltpu.HBM((nq, QK_PAD, R), F32), jax.ShapeDtypeStruct((S, QK_PAD), F32)],
        in_specs=[tile(QK_PAD), tile(KV_LORA), row, row, VMEM_SPEC, VMEM_SPEC],
        out_specs=[pl.BlockSpec((None, QK_PAD, R), lambda i: (i, 0, 0)), VMEM_SPEC],
        scratch_shapes=[pltpu.VMEM((QK_PAD, R), F32), pltpu.VMEM((S, KV_LORA), F32)],
        compiler_params=_params(("arbitrary",)),
    )(*_hbm(qc, do, lse_rows, delta_rows), kc, kct)[::-1]


def _pre_attention_backward(x, dx1, proj, q, dqt, dkc, du, cos, sin, mod6, g_mix, g_q, g_kv, w_in, w_uq, w_uk_t, T, TQ):
    S = x.shape[0]

    def body(x_ref, dx1_ref, proj_ref, q_ref, dqt_ref, dkc_ref, du_ref, cos_ref, sin_ref, mod_ref, gm_ref, gq_ref,
             gkv_ref, win_ref, wuq_ref, wuk_ref, gx_ref, dproj_ref, h1_ref, guk_ref, guq_ref, st_ref, dq_ref):
        @pl.when(pl.program_id(0) == 0)
        def _():
            st_ref[...] = jnp.zeros_like(st_ref)
            guk_ref[...] = jnp.zeros_like(guk_ref)
            guq_ref[...] = jnp.zeros_like(guq_ref)

        cos_t, sin_t = cos_ref[...], sin_ref[...]
        low = lax.broadcasted_iota(jnp.int32, (T, 128), 1) < ROPE
        rope_parts = []
        for h in range(HEADS):
            dqc = jnp.concatenate([jnp.transpose(dqt_ref[s, :, h * TQ:(h + 1) * TQ]) for s in range(T // TQ)], axis=0)
            dqc = dqc * SM_SCALE
            dql = dqc[:, 0:KV_LORA].astype(BF16)
            guk_ref[h] += _dot_tn(dql, q_ref[:, h * NOPE:(h + 1) * NOPE])
            dq_ref[:, h * NOPE:(h + 1) * NOPE] = _dot(dql, wuk_ref[h]).astype(BF16)
            rope_parts.append(dqc[:, KV_LORA:QK_PAD])
        for pair in range(2):
            d = jnp.where(low, rope_parts[2 * pair], rope_parts[2 * pair + 1])
            dq_ref[:, O_QA + 128 * pair:O_QA + 128 * (pair + 1)] = _rope_bwd(d, cos_t, sin_t).astype(BF16)
        dq = dq_ref[...]
        dcq = _dot_nt(dq, wuq_ref[...])
        cqh, rq = _rms(proj_ref[:, 0:Q_LORA])
        guq_ref[...] += _dot_tn(cqh * gq_ref[...], dq)
        st_ref[3:4, 0:Q_LORA] += jnp.sum(dcq * cqh, axis=0, keepdims=True)
        dproj_ref[:, 0:Q_LORA] = _rms_bwd(dcq * gq_ref[...], cqh, rq).astype(BF16)
        dckv = dkc_ref[:, 0:KV_LORA]
        ckvh, rkv = _rms(proj_ref[:, O_CKV:O_KR])
        st_ref[4:5, 0:KV_LORA] += jnp.sum(dckv * ckvh, axis=0, keepdims=True)
        dproj_ref[:, O_CKV:O_KR] = _rms_bwd(dckv * gkv_ref[...], ckvh, rkv).astype(BF16)
        dkr = _rope_bwd(dkc_ref[:, KV_LORA:QK_PAD], cos_t, sin_t)
        dkr = jnp.where(low, dkr + pltpu.roll(dkr, ROPE, 1), 0.0)
        dproj_ref[:, O_KR:O_U] = dkr.astype(BF16)
        dproj_ref[:, O_U:PROJ_W] = du_ref[...].astype(BF16)
        dproj = dproj_ref[...]
        dh1 = jnp.concatenate([_dot(dproj, win_ref[j]) for j in range(N_CHIPS)], axis=1)
        xh, r1 = _rms(x_ref[...])
        n1 = xh * gm_ref[...]
        h1_ref[...] = (n1 * (1.0 + mod_ref[1:2, :]) + mod_ref[0:1, :]).astype(BF16)
        st_ref[0:1, :] += jnp.sum(dh1, axis=0, keepdims=True)
        st_ref[1:2, :] += jnp.sum(dh1 * n1, axis=0, keepdims=True)
        dn1 = dh1 * (1.0 + mod_ref[1:2, :])
        st_ref[2:3, :] += jnp.sum(dn1 * xh, axis=0, keepdims=True)
        gx_ref[...] = _rms_bwd(dn1 * gm_ref[...], xh, r1) + dx1_ref[...]

    row = lambda w: pl.BlockSpec((T, w), lambda i: (i, 0))
    return pl.pallas_call(
        body, name="pre_attention_backward", grid=(S // T,),
        out_shape=[jax.ShapeDtypeStruct((S, D_MODEL), F32), pltpu.HBM((S, PROJ_W), BF16),
                   pltpu.HBM((S, D_MODEL), BF16), jax.ShapeDtypeStruct((HEADS, KV_LORA, NOPE), F32),
                   jax.ShapeDtypeStruct((Q_LORA, Q_W), F32), jax.ShapeDtypeStruct((8, D_MODEL), F32)],
        in_specs=[row(D_MODEL), row(D_MODEL), row(O_KR), row(HEADS * NOPE),
                  pl.BlockSpec((T // TQ, QK_PAD, HEADS * TQ), lambda i: (i, 0, 0)),
                  row(QK_PAD), row(POOL_W), row(128), row(128), _full((N_MOD, D_MODEL)), _full((1, D_MODEL)),
                  _full((1, Q_LORA)), _full((1, KV_LORA)), _full((N_CHIPS, PROJ_W, D_MODEL // N_CHIPS)),
                  _full((Q_LORA, Q_W)), _full((HEADS, KV_LORA, NOPE))],
        out_specs=[row(D_MODEL), row(PROJ_W), row(D_MODEL), _full((HEADS, KV_LORA, NOPE)), _full((Q_LORA, Q_W)),
                   _full((8, D_MODEL))],
        scratch_shapes=[pltpu.VMEM((T, Q_W), BF16)],
        compiler_params=_params(("arbitrary",)),
    )(*_hbm(x, dx1, proj, q, dqt, dkc, du, cos, sin, mod6, g_mix, g_q, g_kv, w_in, w_uq, w_uk_t))


def _ada_grads(c_all, dmod_all, chip):
    cols = N_MOD * D_MODEL // N_CHIPS
    width = dmod_all.shape[1]

    def body(col_ref, c_ref, dcol_ref, dall_ref, gw_ref, gb_ref):
        call = c_ref[...]
        act = call * jax.nn.sigmoid(call)
        gw_ref[...] = _dot_tn(act, dcol_ref[...])
        d = dall_ref[...]
        acc = d[0:1, :]
        for b in range(1, 8):
            acc = acc + d[b:b + 1, :]
        gb_ref[...] = acc

    return pl.pallas_call(
        body, name="ada_grads",
        out_shape=[jax.ShapeDtypeStruct((D_MODEL, cols), F32), jax.ShapeDtypeStruct((1, width), F32)],
        grid_spec=pltpu.PrefetchScalarGridSpec(
            num_scalar_prefetch=1, grid=(1,),
            in_specs=[pl.BlockSpec((8, D_MODEL), lambda s, col_ref: (0, 0)),
                      pl.BlockSpec((8, cols), lambda s, col_ref: (0, col_ref[0])),
                      pl.BlockSpec((8, width), lambda s, col_ref: (0, 0))],
            out_specs=[pl.BlockSpec((D_MODEL, cols), lambda s, col_ref: (0, 0)),
                       pl.BlockSpec((1, width), lambda s, col_ref: (0, 0))]),
        compiler_params=_params(("arbitrary",)),
    )(chip, *_hbm(c_all, dmod_all, dmod_all))


def _adamw(w, g, m, v, name, g_is_landing_zone=True):
    rows, rest = w.shape[0], w.shape[1:]
    T = _row_tile(rows, 256)

    def body(w_ref, g_ref, m_ref, v_ref, *outs):
        d_ref, nm_ref, nv_ref = outs[-3:]
        g = g_ref[...]
        if g_is_landing_zone:
            outs[0][...] = g
        m2 = ADAM_B1 * m_ref[...] + (1.0 - ADAM_B1) * g
        v2 = ADAM_B2 * v_ref[...] + (1.0 - ADAM_B2) * (g * g)
        m_hat = m2 / (1.0 - ADAM_B1 ** ADAM_STEP)
        v_hat = v2 / (1.0 - ADAM_B2 ** ADAM_STEP)
        d_ref[...] = -ADAM_LR * (m_hat / (jnp.sqrt(v_hat) + ADAM_EPS) + ADAM_WD * w_ref[...])
        nm_ref[...] = m2
        nv_ref[...] = v2

    zeros = (0,) * len(rest)
    spec = pl.BlockSpec((T,) + rest, lambda i: (i,) + zeros)
    n_out = 4 if g_is_landing_zone else 3
    res = pl.pallas_call(
        body, name=name, grid=(rows // T,),
        out_shape=[jax.ShapeDtypeStruct(w.shape, F32)] * n_out,
        in_specs=[spec] * 4, out_specs=[spec] * n_out,
        compiler_params=_params(("parallel",)),
    )(*_hbm(w, g, m, v))
    return res if g_is_landing_zone else [g] + list(res)


SMALL_NAMES = ("w_uk", "w_uv", "w_pool", "g_mix", "g_q", "g_kv", "pool_scale", "g_ffn", "g_final", "b_ada")
SMALL_ROWS = 1664


def _pack_rows(parts):
    flat = jnp.concatenate([p.reshape(-1) for p in parts])
    pad = (-flat.shape[0]) % 128
    if pad:
        flat = jnp.concatenate([flat, jnp.zeros((pad,), F32)])
    return flat.reshape(-1, 128)


def kernel(x, c, positions, w_ada, b_ada, g_mix, w_in, g_q, g_kv, w_uq, w_uk, w_uv, w_pool, pool_scale, w_o, g_ffn, w_gate, w_up, w_down, g_final, loss_target, m_w_ada, m_b_ada, m_g_mix, m_w_in, m_g_q, m_g_kv, m_w_uq, m_w_uk, m_w_uv, m_w_pool, m_pool_scale, m_w_o, m_g_ffn, m_w_gate, m_w_up, m_w_down, m_g_final, v_w_ada, v_b_ada, v_g_mix, v_w_in, v_g_q, v_g_kv, v_w_uq, v_w_uk, v_w_uv, v_w_pool, v_pool_scale, v_w_o, v_g_ffn, v_w_gate, v_w_up, v_w_down, v_g_final):
    S = x.shape[1]
    T = _row_tile(S, 512)
    TQ = _row_tile(S, 256)
    TW = _row_tile(S, 2048)
    ix, iy, ic = lax.axis_index("x"), lax.axis_index("y"), lax.axis_index("c")
    chip = (2 * ix + iy).astype(jnp.int32)
    chip_arr = chip.reshape(1)
    core_arr = ic.astype(jnp.int32).reshape(1)

    xs, tgt = x[0], loss_target[0]

    tr = lambda a: jnp.transpose(a[0])
    win_t = tr(w_in)
    win_p = jnp.concatenate([win_t[:O_KR + ROPE], win_t[O_KR:O_KR + ROPE], win_t[O_KR + ROPE:]], axis=0).astype(BF16)
    wuq = w_uq[0]
    wuq_p = jnp.concatenate([wuq[:, h, :NOPE] for h in range(HEADS)] + [wuq[:, h, NOPE:] for h in range(HEADS)],
                            axis=1).astype(BF16)
    w_uk_t = jnp.transpose(w_uk[0], (1, 0, 2)).astype(BF16)
    w_uv_t = jnp.transpose(w_uv[0], (1, 0, 2)).astype(BF16)
    w_pool_b = w_pool[0].astype(BF16)
    first = [win_p, wuq_p]
    later = [w_o[0].astype(BF16), tr(w_gate).astype(BF16), tr(w_up).astype(BF16), w_down[0].astype(BF16)]
    placed = _place_shards(chip_arr, first + later)
    a_send, a_recv, a_lands, token = _split_start("first_weights_start", first, placed[:2], 6, _plan_gather_start)
    half = ROPE // 2
    freqs = jnp.power(ROPE_THETA, -jnp.arange(half, dtype=F32) / half)
    cos, sin = _rope_tables(positions.reshape(S, 1), jnp.tile(freqs, 4).reshape(1, 128) + token[0, 0])
    a_send, a_recv, a_lands, token = _split_relay(
        "first_weights_relay", a_send, a_recv, first, a_lands, cos, 6, _plan_gather_landed, _plan_gather_relay)

    ada_cols = w_ada.shape[2]
    b_cols = lax.dynamic_slice(b_ada, (0, chip * ada_cols), (1, ada_cols))
    mod, c_all = _mod_exchange(c, w_ada[0], b_cols + token[0, 0])
    mod6 = mod.reshape(N_MOD, D_MODEL)
    a_lands = _split_wait("first_weights_wait", a_send, a_recv, [], a_lands, mod, _plan_gather_wait)
    w_in_f = a_lands[0]
    w_uq_f = a_lands[1].reshape(Q_LORA, Q_W)
    wg_lands, mod6, w_in_f = lax.optimization_barrier((placed[2:], mod6, w_in_f))
    wg_send, wg_recv, wg_lands, token = _split_start(
        "weights_start", later, wg_lands, 3 * len(later), _plan_gather_start)
    mod6 = mod6 + token[0, 0]

    proj, q, qc, kc, kct = _pre_attention(xs, mod6, g_mix, g_q, g_kv, w_in_f, w_uq_f, w_uk_t, cos, sin, T, TQ)
    o_lat, y_mla, lse_rows = _attention_fwd(qc, kc, kct, w_uv_t, TQ)
    wg_send, wg_recv, wg_lands, token = _split_relay(
        "weights_relay", wg_send, wg_recv, later, wg_lands, y_mla, 3 * len(later), _plan_gather_landed,
        _plan_gather_relay)
    pooled = _pool_forward(proj)
    wg_lands = _split_wait("weights_wait", wg_send, wg_recv, [], wg_lands, pooled, _plan_gather_wait)
    w_o_f = wg_lands[0].reshape(1024, D_MODEL)
    w_gate_f, w_up_f, w_down_f = wg_lands[1], wg_lands[2], wg_lands[3]
    x1, mix, mix_in = _mix_out(y_mla, pooled, w_pool_b, pool_scale, w_o_f, xs, mod6, T)
    gate, up, act, h2, dff, dx2, st_f = _ffn_forward(
        x1, mod6, g_ffn, g_final.reshape(1, D_MODEL), tgt, w_gate_f, w_up_f, w_down_f, T)

    dgate, dup, dx1, st_b = _ffn_backward(dx2, x1, dff, gate, up, mod6, g_ffn, w_gate_f, w_up_f, w_down_f, T)
    steps = S // TW
    chunk_spec = pl.BlockSpec((None, TW, FF_CHUNK), lambda g, i: (g, i, 0))
    wide_spec = pl.BlockSpec((TW, D_MODEL), lambda g, i: (i, 0))
    g_down = _tn_matmul(act, dff, chunk_spec, wide_spec, N_CHIPS, FF_CHUNK, D_MODEL, steps, "grad_w_down")
    g_gate = _tn_matmul(dgate, h2, chunk_spec, wide_spec, N_CHIPS, FF_CHUNK, D_MODEL, steps, "grad_w_gate")
    g_up = _tn_matmul(dup, h2, chunk_spec, wide_spec, N_CHIPS, FF_CHUNK, D_MODEL, steps, "grad_w_up")

    half_shapes = lambda gs: [jax.ShapeDtypeStruct((N_CHIPS, g.shape[1] // 2, g.shape[2]), F32) for g in gs]
    ffn_grads = [g_gate, g_up, g_down]
    f_send, f_recv, f_lands, token = _split_start(
        "ffn_swap_start", ffn_grads, half_shapes(ffn_grads), len(ffn_grads), _plan_swap_start)
    dmix, dpooled, do_lat, delta_rows, g_pool, g_uv_t, st_m = _mix_backward(
        dx1, mix, mod6 + token[0, 0], w_o_f, pooled, w_pool_b, pool_scale, w_uv_t, o_lat, T, TQ)
    g_o = [_tn_matmul(mix_in, dmix, wide_spec, wide_spec, 1, 1024, D_MODEL, steps, "grad_w_o").reshape(N_CHIPS, -1, D_MODEL)]
    o_send, o_recv, o_lands, token = _split_start("w_o_swap_start", g_o, half_shapes(g_o), 1, _plan_swap_start)
    du = _pool_backward(dpooled, token)
    f_got = _split_wait("ffn_swap_wait", f_send, f_recv, ffn_grads, f_lands, du, _plan_swap_wait)
    f_got += _split_wait("w_o_swap_wait", o_send, o_recv, g_o, o_lands, du, _plan_swap_wait)
    far_grads = ffn_grads + g_o
    f_sums = _add_my_halves(core_arr, far_grads, f_got, "add_half_far")
    f_send, f_recv, f_lands, token = _split_start(
        "far_exchange_start", f_sums, [jax.ShapeDtypeStruct((3,) + s.shape[1:], F32) for s in f_sums],
        3 * len(f_sums), _plan_exchange_start)
    delta_rows = delta_rows + token[0, 0]
    dkc, dqt = _attention_bwd(qc, kc, kct, do_lat, lse_rows, delta_rows, TQ)
    grad_x, dproj, h1, g_uk_t, uq, st_p = _pre_attention_backward(
        xs, dx1, proj, q, dqt, dkc, du, cos, sin, mod6, g_mix, g_q, g_kv, w_in_f, w_uq_f, w_uk_t, T, TQ)
    rows_in = D_MODEL // N_CHIPS
    g_in_p = _tn_matmul(dproj, h1, pl.BlockSpec((TW, PROJ_W), lambda g, i: (i, 0)),
                        pl.BlockSpec((TW, rows_in), lambda g, i: (i, g)), N_CHIPS, PROJ_W, rows_in, steps, "grad_w_in")

    g_in = jnp.concatenate([g_in_p[:, :O_KR + ROPE], g_in_p[:, O_U:]], axis=1)
    g_uq = jnp.concatenate([jnp.concatenate([uq[:, h * NOPE:(h + 1) * NOPE], uq[:, O_QA + h * ROPE:O_QA + (h + 1) * ROPE]],
                                            axis=1) for h in range(HEADS)], axis=1).reshape(N_CHIPS, -1, HEADS * HEAD_QK)
    small = _pack_rows([g_uk_t, g_uv_t, g_pool, st_p[2], st_p[3, :Q_LORA], st_p[4, :KV_LORA], st_m[1, :POOL_W],
                        st_b[2], st_f[0]])
    small = jnp.concatenate([small, jnp.zeros((SMALL_ROWS - small.shape[0], 128), F32)]).reshape(N_CHIPS, -1, 128)
    grads = [g_in, g_uq, small]
    dmod = jnp.concatenate([jnp.stack([st_p[0], st_p[1], st_m[0], st_b[0], st_b[1], st_f[1]]).reshape(48, 128),
                            jnp.zeros((8, 128), F32).at[0, 0].set(st_f[2, 0])])

    got, dmod_all = _grad_swap_halves(grads, dmod)
    chip_sums = _add_my_halves(core_arr, grads, got, "add_half_near")
    n_send, n_recv, n_lands, token = _split_start(
        "near_exchange_start", chip_sums, [jax.ShapeDtypeStruct((3,) + s.shape[1:], F32) for s in chip_sums],
        3 * len(chip_sums), _plan_exchange_start)

    f_others = _split_wait("far_exchange_wait", f_send, f_recv, f_sums, f_lands, token, _plan_exchange_wait)
    chip_core = jnp.concatenate([chip_arr, core_arr])
    f_pairs = (_add_chips_into_pairs(chip_core, f_sums[:2], f_others[:2], "add_chips_gate_up")
               + _add_chips_into_pairs(chip_core, f_sums[2:], f_others[2:], "add_chips_down_o"))
    f_send, f_recv, f_pairs, token = _split_start("far_finish_start", [], f_pairs, len(f_pairs), _plan_finish_start)
    gw_ada, gb_ada = _ada_grads(c_all, dmod_all.reshape(8, -1) + token[0, 0], chip_arr)
    loss = gb_ada[0, N_MOD * D_MODEL]
    gb_ada = gb_ada[:, :N_MOD * D_MODEL]
    f_fulls = _split_wait("far_finish_wait", f_send, f_recv, [], f_pairs, gw_ada, _plan_finish_wait)
    gw_gate, gw_up, gw_down, gw_o = [f.reshape(-1, f.shape[2]) for f in f_fulls]

    untr = lambda a: jnp.transpose(a)[None]
    grad_out, delta_out, newm_out, newv_out = {}, {}, {}, {}

    def adam_sharded(n, w, g2, m, v, transposed, landed=True):
        view = (lambda a: jnp.transpose(a[0])) if transposed else (lambda a: a[0])
        back = untr if transposed else (lambda a: a[None])
        g_, d_, m_, v_ = _adamw(view(w), g2.reshape(view(w).shape), view(m), view(v), "adamw_" + n, landed)
        grad_out[n], delta_out[n], newm_out[n], newv_out[n] = back(g_), back(d_), back(m_), back(v_)
        return d_

    done = [adam_sharded("w_gate", w_gate, gw_gate, m_w_gate, v_w_gate, True),
            adam_sharded("w_up", w_up, gw_up, m_w_up, v_w_up, True),
            adam_sharded("w_down", w_down, gw_down, m_w_down, v_w_down, False),
            adam_sharded("w_o", w_o, gw_o, m_w_o, v_w_o, False)]
    after_all = jnp.stack([d[0, 0] for d in done])

    others = _split_wait("near_exchange_wait", n_send, n_recv, chip_sums, n_lands, after_all, _plan_exchange_wait)
    n_pairs = _add_chips_into_pairs(chip_core, chip_sums[:2], others[:2], "add_chips_in_uq")
    small_grid = _add_chips_into_grid(chip_core, chip_sums[2], others[2], "add_chips_small")
    n_send, n_recv, n_lands, token = _split_start(
        "near_finish_start", [], n_pairs + [small_grid], 2 + len(RELATIONS), _plan_near_finish_start)
    gw_ada, _ = lax.optimization_barrier((gw_ada, token))
    d_ada = adam_sharded("w_ada", w_ada, gw_ada, m_w_ada, v_w_ada, False, landed=False)
    n_lands = _split_wait("near_finish_wait", n_send, n_recv, [], n_lands, d_ada, _plan_near_finish_wait)
    gw_in, gw_uq = [f.reshape(-1, f.shape[2]) for f in n_lands[:2]]
    small_all = n_lands[2].reshape(SMALL_ROWS * 128)
    adam_sharded("w_in", w_in, gw_in, m_w_in, v_w_in, True)
    adam_sharded("w_uq", w_uq, gw_uq, m_w_uq, v_w_uq, False)

    n_sq = KV_LORA * HEADS * 128
    sizes = [n_sq, n_sq, n_sq, D_MODEL, Q_LORA, KV_LORA, POOL_W, D_MODEL, D_MODEL]
    offs = [0]
    for s_ in sizes:
        offs.append(offs[-1] + s_)
    piece = lambda k: small_all[offs[k]:offs[k + 1]]
    grads_small = {
        "w_uk": jnp.transpose(piece(0).reshape(HEADS, KV_LORA, NOPE), (1, 0, 2)),
        "w_uv": jnp.transpose(piece(1).reshape(HEADS, KV_LORA, 128), (1, 0, 2)),
        "w_pool": piece(2).reshape(4, POOL_GROUP, POOL_GROUP),
        "g_mix": piece(3), "g_q": piece(4), "g_kv": piece(5), "pool_scale": piece(6), "g_ffn": piece(7),
        "g_final": piece(8), "b_ada": gb_ada.reshape(-1),
    }
    weights_small = {"w_uk": w_uk, "w_uv": w_uv, "w_pool": w_pool, "g_mix": g_mix, "g_q": g_q, "g_kv": g_kv,
                     "pool_scale": pool_scale, "g_ffn": g_ffn, "g_final": g_final, "b_ada": b_ada}
    m_small = {"w_uk": m_w_uk, "w_uv": m_w_uv, "w_pool": m_w_pool, "g_mix": m_g_mix, "g_q": m_g_q, "g_kv": m_g_kv,
               "pool_scale": m_pool_scale, "g_ffn": m_g_ffn, "g_final": m_g_final, "b_ada": m_b_ada}
    v_small = {"w_uk": v_w_uk, "w_uv": v_w_uv, "w_pool": v_w_pool, "g_mix": v_g_mix, "g_q": v_g_q, "g_kv": v_g_kv,
               "pool_scale": v_pool_scale, "g_ffn": v_g_ffn, "g_final": v_g_final, "b_ada": v_b_ada}
    pack = lambda d: _pack_rows([d[n] for n in SMALL_NAMES])
    _, d_s, m_s, v_s = _adamw(pack(weights_small), pack(grads_small), pack(m_small), pack(v_small), "adamw_small",
                              g_is_landing_zone=False)

    def unpack(flat2d):
        flat = flat2d.reshape(-1)
        out, o = {}, 0
        for n in SMALL_NAMES:
            size = weights_small[n].size
            out[n] = flat[o:o + size].reshape(weights_small[n].shape)
            o += size
        return out

    delta_s, newm_s, newv_s = unpack(d_s), unpack(m_s), unpack(v_s)

    for n in SMALL_NAMES:
        grad_out[n] = grads_small[n].reshape(weights_small[n].shape)
        delta_out[n], newm_out[n], newv_out[n] = delta_s[n], newm_s[n], newv_s[n]

    order = ("w_ada", "b_ada", "g_mix", "w_in", "g_q", "g_kv", "w_uq", "w_uk", "w_uv", "w_pool", "pool_scale", "w_o",
             "g_ffn", "w_gate", "w_up", "w_down", "g_final")
    return (loss, grad_x.reshape(x.shape), *[grad_out[n] for n in order], *[delta_out[n] for n in order],
            *[newm_out[n] for n in order], *[newv_out[n] for n in order])
```

```python
import functools

import jax
import jax.numpy as jnp
from jax import lax
from jax.experimental import pallas as pl
from jax.experimental.pallas import tpu as pltpu

F32 = jnp.float32
BF16 = jnp.bfloat16

D_MODEL = 1024
HEADS = 4
NOPE = 128
ROPE = 64
HEAD_QK = NOPE + ROPE
Q_LORA = 256
KV_LORA = 128
POOL_W = 512
POOL_WINDOWS = (2, 4, 8, 16)
POOL_GROUP = 128
POOL_PAD = 16
D_FF = 2816
N_CHIPS = 4
FF_CHUNK = D_FF // N_CHIPS
N_MOD = 6
EPS = 1e-6
SM_SCALE = HEAD_QK ** -0.5
ROPE_THETA = 10000.0
QK_PAD = 256
CHUNK = 64
CHUNK_SHIFT = 6

ADAM_LR = 0.001
ADAM_B1 = 0.9
ADAM_B2 = 0.999
ADAM_EPS = 1e-08
ADAM_WD = 0.01
ADAM_STEP = 10

VMEM_LIMIT = 48 * 1024 * 1024
MESH = pl.DeviceIdType.MESH
ANY = pl.BlockSpec(memory_space=pl.ANY)
VMEM_SPEC = pl.BlockSpec(memory_space=pltpu.VMEM)

PROJ_W = 1024
O_CKV = 256
O_KR = 384
O_U = 512
Q_W = 768
O_QA = 512
O_QB = 640


def _params(sem=None, vmem=VMEM_LIMIT):
    kw = dict(vmem_limit_bytes=vmem)
    if sem is not None:
        kw["dimension_semantics"] = sem
    return pltpu.CompilerParams(**kw)


def _dot(a, b):
    return jnp.dot(a.astype(BF16), b.astype(BF16), preferred_element_type=F32)


def _dot_nt(a, b):
    return lax.dot_general(a.astype(BF16), b.astype(BF16), (((1,), (1,)), ((), ())), preferred_element_type=F32)


def _dot_tn(a, b):
    return lax.dot_general(a.astype(BF16), b.astype(BF16), (((0,), (0,)), ((), ())), preferred_element_type=F32)


def _row_tile(rows, target):
    best = rows
    for t in range(8, min(rows, target) + 1, 8):
        if rows % t == 0:
            best = t
    return best if rows % best == 0 and best <= target else rows


def _rms(x):
    r = lax.rsqrt(jnp.mean(x * x, axis=-1, keepdims=True) + EPS)
    return x * r, r


def _rms_bwd(dxh, xh, r):
    return r * (dxh - xh * jnp.mean(dxh * xh, axis=-1, keepdims=True))


def _lane_first_half(shape):
    lane = lax.broadcasted_iota(jnp.int32, shape, 1)
    return (lane & (ROPE - 1)) < (ROPE // 2)


def _rope(a, cos, sin):
    first = _lane_first_half(a.shape)
    up = pltpu.roll(a, 96, 1)
    dn = pltpu.roll(a, 32, 1)
    return a * cos + jnp.where(first, -up, dn) * sin


def _rope_bwd(d, cos, sin):
    first = _lane_first_half(d.shape)
    up = pltpu.roll(d, 96, 1)
    dn = pltpu.roll(d, 32, 1)
    return d * cos + jnp.where(first, up, -dn) * sin


RELATIONS = tuple((dx, dy, dc) for dx in (0, 1) for dy in (0, 1) for dc in (0, 1) if (dx, dy, dc) != (0, 0, 0))
CHIP_RELATIONS = ((1, 0), (0, 1), (1, 1))


def _flip(v, d):
    return 1 - v if d else v


def _place():
    return lax.axis_index("x"), lax.axis_index("y"), lax.axis_index("c")


def _remote(src, dst, send_sem, recv_sem, target):
    return pltpu.make_async_remote_copy(src_ref=src, dst_ref=dst, send_sem=send_sem, recv_sem=recv_sem,
                                        device_id=target, device_id_type=MESH)


def _mod_exchange(c_row, w_ada, b_ada):
    cols = w_ada.shape[1]

    def body(c_ref, w_ref, b_ref, mod_ref, call_ref, part_ref, send1, recv1, loc1, send2, recv2, loc2):
        x, y, c = _place()
        me = 4 * x + 2 * y + c
        own = pltpu.make_async_copy(c_ref, call_ref.at[pl.ds(me, 1)], loc1)
        own.start()
        sends = []
        for k, (dx, dy, dc) in enumerate(RELATIONS):
            cp = _remote(c_ref, call_ref.at[pl.ds(me, 1)], send1.at[k], recv1.at[k],
                         (_flip(x, dx), _flip(y, dy), _flip(c, dc)))
            cp.start()
            sends.append(cp)
        for k, (dx, dy, dc) in enumerate(RELATIONS):
            src = 4 * _flip(x, dx) + 2 * _flip(y, dy) + _flip(c, dc)
            _remote(c_ref, call_ref.at[pl.ds(src, 1)], send1.at[k], recv1.at[k], (x, y, c)).wait_recv()
        own.wait()
        for cp in sends:
            cp.wait_send()
        call = call_ref[...]
        act = call * jax.nn.sigmoid(call)
        part_ref[...] = _dot(act, w_ref[...]) + b_ref[...]
        chip = 2 * x + y
        mine = pltpu.make_async_copy(part_ref.at[pl.ds(me, 1)], mod_ref.at[pl.ds(chip, 1)], loc2)
        mine.start()
        sends = []
        for k, (dx, dy) in enumerate(CHIP_RELATIONS):
            tx, ty = _flip(x, dx), _flip(y, dy)
            tb = 4 * tx + 2 * ty + c
            cp = _remote(part_ref.at[pl.ds(tb, 1)], mod_ref.at[pl.ds(chip, 1)], send2.at[k], recv2.at[k], (tx, ty, c))
            cp.start()
            sends.append(cp)
        for k, (dx, dy) in enumerate(CHIP_RELATIONS):
            src_chip = 2 * _flip(x, dx) + _flip(y, dy)
            _remote(part_ref.at[pl.ds(me, 1)], mod_ref.at[pl.ds(src_chip, 1)], send2.at[k], recv2.at[k],
                    (x, y, c)).wait_recv()
        mine.wait()
        for cp in sends:
            cp.wait_send()

    return pl.pallas_call(
        body, name="mod_exchange",
        out_shape=[jax.ShapeDtypeStruct((N_CHIPS, cols), F32), jax.ShapeDtypeStruct((8, D_MODEL), F32)],
        in_specs=[VMEM_SPEC, VMEM_SPEC, VMEM_SPEC], out_specs=[VMEM_SPEC, VMEM_SPEC],
        scratch_shapes=[pltpu.VMEM((8, cols), F32),
                        pltpu.SemaphoreType.DMA((7,)), pltpu.SemaphoreType.DMA((7,)), pltpu.SemaphoreType.DMA,
                        pltpu.SemaphoreType.DMA((3,)), pltpu.SemaphoreType.DMA((3,)), pltpu.SemaphoreType.DMA],
        compiler_params=_params(),
    )(c_row, w_ada, b_ada)


HBM_SPEC = pl.BlockSpec(memory_space=pltpu.HBM)
SEM_SPEC = pl.BlockSpec(memory_space=pltpu.SEMAPHORE)
DATAFLOW = pltpu.SideEffectType.DATAFLOW_SIDE_EFFECTING


def _in_hbm(a):
    return pltpu.with_memory_space_constraint(a, pltpu.HBM)


def _hbm(*arrays):
    return tuple(_in_hbm(a) for a in arrays)


def _hbm_like(arrays):
    return [pltpu.HBM(a.shape, a.dtype) for a in arrays]


def _split_start(name, srcs, lands, n_remote, plan):
    lands = [lax.empty(a.shape, a.dtype) if isinstance(a, jax.ShapeDtypeStruct) else a for a in lands]
    n, m = len(srcs), len(lands)

    def body(*refs):
        src_refs, land_refs = refs[:n], refs[n:n + m]
        send_sems, recv_sems, token = refs[n + m], refs[n + m + 1], refs[n + 2 * m + 2]
        remote = plan(_place(), src_refs, land_refs)
        assert len(remote) == n_remote
        for i, (s, d, target) in enumerate(remote):
            _remote(s, d, send_sems.at[i], recv_sems.at[i], target).start()
        token[...] = jnp.zeros_like(token)

    res = pl.pallas_call(
        body, name=name,
        out_shape=(pltpu.SemaphoreType.DMA((n_remote,)), pltpu.SemaphoreType.DMA((n_remote,)),
                   *_hbm_like(lands), jax.ShapeDtypeStruct((8, 128), F32)),
        in_specs=[HBM_SPEC] * (n + m),
        out_specs=(SEM_SPEC, SEM_SPEC, *([HBM_SPEC] * m), VMEM_SPEC),
        input_output_aliases={n + i: 2 + i for i in range(m)},
        compiler_params=pltpu.CompilerParams(has_side_effects=DATAFLOW),
    )(*[_in_hbm(a) for a in srcs], *[_in_hbm(a) for a in lands])
    return res[0], res[1], list(res[2:2 + m]), res[2 + m]


def _split_wait(name, send_sems, recv_sems, srcs, lands, after, plan):
    n, m = len(srcs), len(lands)

    def body(*refs):
        src_refs, land_refs = refs[:n], refs[n:n + m]
        send_sems, recv_sems = refs[n + m], refs[n + m + 1]
        place = _place()
        for i, (s, d) in enumerate(plan(place, src_refs, land_refs)):
            cp = _remote(s, d, send_sems.at[i], recv_sems.at[i], place)
            cp.wait_send()
            cp.wait_recv()

    res = pl.pallas_call(
        body, name=name,
        out_shape=tuple(_hbm_like(lands)),
        in_specs=[HBM_SPEC] * (n + m) + [SEM_SPEC, SEM_SPEC, ANY],
        out_specs=tuple([HBM_SPEC] * m),
        input_output_aliases={n + i: i for i in range(m)},
        compiler_params=pltpu.CompilerParams(has_side_effects=DATAFLOW),
    )(*srcs, *lands, send_sems, recv_sems, after)
    return list(res)


def _split_relay(name, send_sems, recv_sems, srcs, lands, after, n_remote, plan_wait, plan_send):
    n, m = len(srcs), len(lands)

    def body(*refs):
        src_refs, land_refs = refs[:n], refs[n:n + m]
        old_send, old_recv = refs[n + m], refs[n + m + 1]
        new_send, new_recv = refs[n + m + 3], refs[n + m + 4]
        token = refs[n + m + 5 + m]
        place = _place()
        for i, (s, d) in enumerate(plan_wait(place, src_refs, land_refs)):
            cp = _remote(s, d, old_send.at[i], old_recv.at[i], place)
            cp.wait_send()
            cp.wait_recv()
        for i, (s, d, target) in enumerate(plan_send(place, land_refs)):
            _remote(s, d, new_send.at[i], new_recv.at[i], target).start()
        token[...] = jnp.zeros_like(token)

    res = pl.pallas_call(
        body, name=name,
        out_shape=(pltpu.SemaphoreType.DMA((n_remote,)), pltpu.SemaphoreType.DMA((n_remote,)),
                   *_hbm_like(lands), jax.ShapeDtypeStruct((8, 128), F32)),
        in_specs=[HBM_SPEC] * (n + m) + [SEM_SPEC, SEM_SPEC, ANY],
        out_specs=(SEM_SPEC, SEM_SPEC, *([HBM_SPEC] * m), VMEM_SPEC),
        input_output_aliases={n + i: 2 + i for i in range(m)},
        compiler_params=pltpu.CompilerParams(has_side_effects=DATAFLOW),
    )(*srcs, *lands, send_sems, recv_sems, after)
    return res[0], res[1], list(res[2:2 + m]), res[2 + m]


def _half(ref, core, axis=0):
    hr = ref.shape[axis] // 2
    return pl.ds(core * hr, hr)


def _plan_gather_start(place, src, land):
    x, y, c = place
    chip = 2 * x + y
    return [(s.at[_half(s, c)], l.at[chip, _half(s, c)], (_flip(x, dx), _flip(y, dy), c))
            for s, l in zip(src, land) for dx, dy in CHIP_RELATIONS]


def _plan_gather_landed(place, src, land):
    x, y, c = place
    return [(s.at[_half(s, c)], l.at[2 * _flip(x, dx) + _flip(y, dy), _half(s, c)])
            for s, l in zip(src, land) for dx, dy in CHIP_RELATIONS]


def _plan_gather_relay(place, land):
    x, y, c = place
    out = []
    for l in land:
        for dx, dy in CHIP_RELATIONS:
            got = l.at[2 * _flip(x, dx) + _flip(y, dy), _half(l, c, 1)]
            out.append((got, got, (x, y, 1 - c)))
    return out


def _plan_gather_wait(place, src, land):
    x, y, c = place
    out = []
    for l in land:
        for dx, dy in CHIP_RELATIONS:
            got = l.at[2 * _flip(x, dx) + _flip(y, dy), _half(l, 1 - c, 1)]
            out.append((got, got))
    return out


def _plan_swap_start(place, src, land):
    x, y, c = place
    return [(s.at[:, _half(s, 1 - c, 1), :], l, (x, y, 1 - c)) for s, l in zip(src, land)]


def _plan_swap_wait(place, src, land):
    return [(s.at[:, _half(s, 0, 1), :], l) for s, l in zip(src, land)]


def _plan_exchange_start(place, src, land):
    x, y, c = place
    remote = []
    for s, l in zip(src, land):
        for k, (dx, dy) in enumerate(CHIP_RELATIONS):
            tx, ty = _flip(x, dx), _flip(y, dy)
            remote.append((s.at[2 * tx + ty], l.at[k], (tx, ty, c)))
    return remote


def _plan_exchange_wait(place, src, land):
    return [(s.at[0], l.at[k]) for s, l in zip(src, land) for k in range(3)]


def _plan_finish_start(place, src, land):
    x, y, c = place
    return [(l.at[c], l.at[c], (x, y, 1 - c)) for l in land]


def _plan_finish_wait(place, src, land):
    x, y, c = place
    return [(l.at[c], l.at[1 - c]) for l in land]


def _plan_near_finish_start(place, src, land):
    x, y, c = place
    mine = land[-1].at[2 * x + y, c]
    return (_plan_finish_start(place, src, land[:-1])
            + [(mine, mine, (_flip(x, dx), _flip(y, dy), _flip(c, dc))) for dx, dy, dc in RELATIONS])


def _plan_near_finish_wait(place, src, land):
    x, y, c = place
    mine = land[-1].at[2 * x + y, c]
    return (_plan_finish_wait(place, src, land[:-1])
            + [(mine, land[-1].at[2 * _flip(x, dx) + _flip(y, dy), _flip(c, dc)]) for dx, dy, dc in RELATIONS])


def _grad_swap_halves(grads, dmod):
    n = len(grads)

    def body(*refs):
        ins, dmod_ref = refs[:n], refs[n]
        outs, dall_ref = refs[n + 1:2 * n + 1], refs[2 * n + 1]
        send_sems, recv_sems, dsend, drecv, dloc = refs[2 * n + 2:]
        x, y, c = _place()
        me = 4 * x + 2 * y + c
        sends = []
        for w in range(n):
            hr = ins[w].shape[1] // 2
            cp = _remote(ins[w].at[:, pl.ds((1 - c) * hr, hr), :], outs[w], send_sems.at[w], recv_sems.at[w],
                         (x, y, 1 - c))
            cp.start()
            sends.append(cp)
        own = pltpu.make_async_copy(dmod_ref, dall_ref.at[me], dloc)
        own.start()
        for k, (dx, dy, dc) in enumerate(RELATIONS):
            cp = _remote(dmod_ref, dall_ref.at[me], dsend.at[k], drecv.at[k],
                         (_flip(x, dx), _flip(y, dy), _flip(c, dc)))
            cp.start()
            sends.append(cp)
        for k, (dx, dy, dc) in enumerate(RELATIONS):
            src = 4 * _flip(x, dx) + 2 * _flip(y, dy) + _flip(c, dc)
            _remote(dmod_ref, dall_ref.at[src], dsend.at[k], drecv.at[k], (x, y, c)).wait_recv()
        for w in range(n):
            _remote(outs[w], outs[w], send_sems.at[w], recv_sems.at[w], (x, y, c)).wait_recv()
        own.wait()
        for cp in sends:
            cp.wait_send()

    out_shape = [pltpu.HBM((N_CHIPS, g.shape[1] // 2, g.shape[2]), F32) for g in grads]
    out_shape.append(pltpu.HBM((8,) + dmod.shape, F32))
    res = pl.pallas_call(
        body, name="grad_swap_halves",
        out_shape=out_shape, in_specs=[ANY] * n + [VMEM_SPEC], out_specs=[ANY] * (n + 1),
        scratch_shapes=[pltpu.SemaphoreType.DMA((n,)), pltpu.SemaphoreType.DMA((n,)),
                        pltpu.SemaphoreType.DMA((7,)), pltpu.SemaphoreType.DMA((7,)), pltpu.SemaphoreType.DMA],
        compiler_params=_params(),
    )(*grads, dmod)
    return res[:n], res[n]


def _add_my_halves(core, fulls, gots, name):
    n = len(fulls)

    def body(core_ref, *refs):
        for w in range(n):
            refs[2 * n + w][...] = refs[w][...] + refs[n + w][...]

    mine = lambda g: pl.BlockSpec((None,) + g.shape[1:], lambda s, core_ref: (s, core_ref[0], 0))
    slab = lambda g: pl.BlockSpec((None,) + g.shape[1:], lambda s, core_ref: (s, 0, 0))
    return list(pl.pallas_call(
        body, name=name,
        out_shape=[pltpu.HBM(g.shape, F32) for g in gots],
        grid_spec=pltpu.PrefetchScalarGridSpec(
            num_scalar_prefetch=1, grid=(N_CHIPS,),
            in_specs=[mine(g) for g in gots] + [slab(g) for g in gots],
            out_specs=[slab(g) for g in gots]),
        compiler_params=_params(("arbitrary",)),
    )(core, *_hbm(*fulls, *gots)))


def _add_chips_into_pairs(chip_core, mines, gots, name):
    n = len(mines)

    def body(cc_ref, *refs):
        for w in range(n):
            b_ref = refs[n + w]
            refs[2 * n + w][...] = ((refs[w][...] + b_ref[0]) + b_ref[1]) + b_ref[2]

    return list(pl.pallas_call(
        body, name=name,
        out_shape=[pltpu.HBM((2,) + m.shape[1:], F32) for m in mines],
        grid_spec=pltpu.PrefetchScalarGridSpec(
            num_scalar_prefetch=1, grid=(1,),
            in_specs=[pl.BlockSpec((None,) + m.shape[1:], lambda s, cc_ref: (cc_ref[0], 0, 0)) for m in mines]
            + [pl.BlockSpec(g.shape, lambda s, cc_ref: (0, 0, 0)) for g in gots],
            out_specs=[pl.BlockSpec((None,) + m.shape[1:], lambda s, cc_ref: (cc_ref[1], 0, 0)) for m in mines]),
        compiler_params=_params(("arbitrary",)),
    )(chip_core, *_hbm(*mines, *gots)))


def _add_chips_into_grid(chip_core, mine, got, name):
    _, hr, cols = mine.shape

    def body(cc_ref, a_ref, b_ref, o_ref):
        o_ref[...] = ((a_ref[...] + b_ref[0]) + b_ref[1]) + b_ref[2]

    return pl.pallas_call(
        body, name=name,
        out_shape=pltpu.HBM((N_CHIPS, 2, hr, cols), F32),
        grid_spec=pltpu.PrefetchScalarGridSpec(
            num_scalar_prefetch=1, grid=(1,),
            in_specs=[pl.BlockSpec((None, hr, cols), lambda s, cc_ref: (cc_ref[0], 0, 0)),
                      pl.BlockSpec((3, hr, cols), lambda s, cc_ref: (0, 0, 0))],
            out_specs=pl.BlockSpec((None, None, hr, cols), lambda s, cc_ref: (cc_ref[0], cc_ref[1], 0, 0))),
        compiler_params=_params(("arbitrary",)),
    )(chip_core, *_hbm(mine, got))


def _place_shards(chip, shards):
    n = len(shards)

    def body(chip_ref, *refs):
        for w in range(n):
            refs[n + w][...] = refs[w][...]

    return pl.pallas_call(
        body, name="place_shards",
        out_shape=[pltpu.HBM((N_CHIPS,) + s.shape, s.dtype) for s in shards],
        grid_spec=pltpu.PrefetchScalarGridSpec(
            num_scalar_prefetch=1, grid=(1,),
            in_specs=[pl.BlockSpec(s.shape, lambda i, chip_ref: (0, 0)) for s in shards],
            out_specs=[pl.BlockSpec((None,) + s.shape, lambda i, chip_ref: (chip_ref[0], 0, 0)) for s in shards]),
        compiler_params=_params(("arbitrary",)),
    )(chip, *shards)


def _rope_tables(pos_col, freqs):
    S = pos_col.shape[0]
    T = _row_tile(S, 1024)

    def body(p_ref, f_ref, cos_ref, sin_ref):
        ang = p_ref[...].astype(F32) * f_ref[...]
        cos_ref[...] = jnp.cos(ang)
        sin_ref[...] = jnp.sin(ang)

    return pl.pallas_call(
        body, name="rope_tables", grid=(S // T,),
        out_shape=[pltpu.HBM((S, 128), F32)] * 2,
        in_specs=[pl.BlockSpec((T, 1), lambda i: (i, 0)), pl.BlockSpec((1, 128), lambda i: (0, 0))],
        out_specs=[pl.BlockSpec((T, 128), lambda i: (i, 0))] * 2,
        compiler_params=_params(("parallel",)),
    )(*_hbm(pos_col, freqs))


def _full(shape):
    zeros = (0,) * len(shape)
    return pl.BlockSpec(shape, lambda *_: zeros)


def _pre_attention(x, mod6, g_mix, g_q, g_kv, w_in, w_uq, w_uk_t, cos, sin, T, TQ):
    S = x.shape[0]

    def body(x_ref, mod_ref, gm_ref, gq_ref, gkv_ref, win_ref, wuq_ref, wuk_ref, cos_ref, sin_ref,
             proj_ref, q_ref, qc_ref, kc_ref, kct_ref):
        xh, _ = _rms(x_ref[...])
        h1 = ((xh * gm_ref[...]) * (1.0 + mod_ref[1:2, :]) + mod_ref[0:1, :]).astype(BF16)
        rows_in = D_MODEL // N_CHIPS
        proj = _dot_nt(h1[:, 0:rows_in], win_ref[0])
        for j in range(1, N_CHIPS):
            proj = proj + _dot_nt(h1[:, j * rows_in:(j + 1) * rows_in], win_ref[j])
        proj_ref[...] = proj
        cqh, _ = _rms(proj[:, :Q_LORA])
        c_q = cqh * gq_ref[...]
        ckvh, _ = _rms(proj[:, O_CKV:O_KR])
        c_kv = ckvh * gkv_ref[...]
        q = _dot(c_q, wuq_ref[...])
        q_ref[...] = q.astype(BF16)
        cos_t, sin_t = cos_ref[...], sin_ref[...]
        ropes = (_rope(q[:, O_QA:O_QB], cos_t, sin_t), _rope(q[:, O_QB:Q_W], cos_t, sin_t))
        low = lax.broadcasted_iota(jnp.int32, (T, 128), 1) < ROPE
        for h in range(HEADS):
            q_lat = _dot_nt(q[:, h * NOPE:(h + 1) * NOPE], wuk_ref[h])
            keep = low if h % 2 == 0 else jnp.logical_not(low)
            qc_ref[h, :, 0:KV_LORA] = q_lat.astype(BF16)
            qc_ref[h, :, KV_LORA:QK_PAD] = jnp.where(keep, ropes[h // 2], 0.0).astype(BF16)
        k_rope = _rope(proj[:, O_KR:O_U], cos_t, sin_t)
        kc_ref[:, 0:KV_LORA] = c_kv.astype(BF16)
        kc_ref[:, KV_LORA:QK_PAD] = k_rope.astype(BF16)
        lat_t, rope_t = jnp.transpose(c_kv), jnp.transpose(k_rope)
        for s in range(T // TQ):
            kct_ref[s, 0:KV_LORA, :] = lat_t[:, s * TQ:(s + 1) * TQ].astype(BF16)
            kct_ref[s, KV_LORA:QK_PAD, :] = rope_t[:, s * TQ:(s + 1) * TQ].astype(BF16)

    row = lambda w: pl.BlockSpec((T, w), lambda i: (i, 0))
    return pl.pallas_call(
        body, name="pre_attention", grid=(S // T,),
        out_shape=[pltpu.HBM((S, PROJ_W), F32), pltpu.HBM((S, Q_W), BF16), pltpu.HBM((HEADS, S, QK_PAD), BF16),
                   pltpu.HBM((S, QK_PAD), BF16), pltpu.HBM((S // TQ, QK_PAD, TQ), BF16)],
        in_specs=[row(D_MODEL), _full((N_MOD, D_MODEL)), _full((1, D_MODEL)), _full((1, Q_LORA)), _full((1, KV_LORA)),
                  _full((N_CHIPS, PROJ_W, D_MODEL // N_CHIPS)), _full((Q_LORA, Q_W)), _full((HEADS, KV_LORA, NOPE)),
                  row(128), row(128)],
        out_specs=[row(PROJ_W), row(Q_W), pl.BlockSpec((HEADS, T, QK_PAD), lambda i: (0, i, 0)), row(QK_PAD),
                   pl.BlockSpec((T // TQ, QK_PAD, TQ), lambda i: (i, 0, 0))],
        compiler_params=_params(("parallel",)),
    )(*_hbm(x, mod6, g_mix, g_q, g_kv, w_in, w_uq, w_uk_t, cos, sin))


def _diag_mask(TQ, width):
    key = lax.broadcasted_iota(jnp.int32, (TQ, width), 0) >> CHUNK_SHIFT
    qry = (lax.broadcasted_iota(jnp.int32, (TQ, width), 1) & (TQ - 1)) >> CHUNK_SHIFT
    return key <= qry


def _col_to_row(col):
    return jnp.transpose(jnp.broadcast_to(col, (col.shape[0], 128)))[0:1, :]


def _attention_fwd(qc, kc, kct, w_uv_t, TQ):
    S = kc.shape[0]
    R = HEADS * TQ
    nq = S // TQ

    def body(q_ref, k_ref, kt_ref, wuv_ref, o_ref, y_ref, lser_ref, m_s, l_s, acc_s, st_s):
        i = pl.program_id(0)
        q = q_ref[...].reshape(R, QK_PAD)
        m_s[...] = jnp.full((1, R), -jnp.inf, F32)
        l_s[...] = jnp.zeros((1, R), F32)
        acc_s[...] = jnp.zeros((KV_LORA, R), F32)

        def scores(j):
            return _dot_nt(k_ref[pl.ds(pl.multiple_of(j * TQ, TQ), TQ), :], q) * SM_SCALE

        def update(j, st):
            m_old = m_s[...]
            m_new = jnp.maximum(m_old, jnp.max(st, axis=0, keepdims=True))
            pt = jnp.exp(st - m_new)
            alpha = jnp.exp(m_old - m_new)
            l_s[...] = alpha * l_s[...] + jnp.sum(pt, axis=0, keepdims=True)
            acc_s[...] = alpha * acc_s[...] + _dot(kt_ref[j, 0:KV_LORA, :], pt)
            m_s[...] = m_new

        st_s[...] = scores(0)

        def loop(j, carry):
            st = st_s[...]
            st_s[...] = scores(j + 1)
            update(j, st)
            return carry

        lax.fori_loop(0, i, loop, 0)
        update(i, jnp.where(_diag_mask(TQ, R), st_s[...], -jnp.inf))
        l = l_s[...]
        lser_ref[0] = m_s[...] + jnp.log(l)
        o = jnp.transpose(acc_s[...] / l).astype(BF16)
        for h in range(HEADS):
            oh = o[h * TQ:(h + 1) * TQ, :]
            o_ref[h] = oh
            y_ref[:, h * 128:(h + 1) * 128] = _dot(oh, wuv_ref[h]).astype(BF16)

    return pl.pallas_call(
        body, name="attention_fwd", grid=(nq,),
        out_shape=[pltpu.HBM((HEADS, S, KV_LORA), BF16), pltpu.HBM((S, HEADS * 128), BF16),
                   pltpu.HBM((nq, 1, R), F32)],
        in_specs=[pl.BlockSpec((HEADS, TQ, QK_PAD), lambda i: (0, i, 0)), _full((S, QK_PAD)),
                  _full((nq, QK_PAD, TQ)), _full((HEADS, KV_LORA, 128))],
        out_specs=[pl.BlockSpec((HEADS, TQ, KV_LORA), lambda i: (0, i, 0)), pl.BlockSpec((TQ, HEADS * 128), lambda i: (i, 0)),
                   pl.BlockSpec((1, 1, R), lambda i: (i, 0, 0))],
        scratch_shapes=[pltpu.VMEM((1, R), F32), pltpu.VMEM((1, R), F32), pltpu.VMEM((KV_LORA, R), F32),
                        pltpu.VMEM((TQ, R), F32)],
        compiler_params=_params(("parallel",)),
    )(*_hbm(qc, kc, kct, w_uv_t))


def _pool_forward(proj):
    S = proj.shape[0]
    RB = _row_tile(S, 256)

    def body(proj_ref, out_ref, pad_ref, sem):
        cp = pltpu.make_async_copy(proj_ref.at[:, pl.ds(O_U, POOL_W)], pad_ref.at[pl.ds(POOL_PAD, S)], sem)
        cp.start()
        pad_ref[0:POOL_PAD, :] = jnp.zeros((POOL_PAD, POOL_W), F32)
        cp.wait()
        for g, win in enumerate(POOL_WINDOWS):
            cols = slice(g * POOL_GROUP, (g + 1) * POOL_GROUP)
            for r0 in range(0, S, RB):
                u = pad_ref[POOL_PAD + r0:POOL_PAD + r0 + RB, cols]
                acc = u
                for k in range(1, win):
                    acc = acc + pad_ref[POOL_PAD + r0 - k:POOL_PAD + r0 - k + RB, cols]
                if r0 == 0:
                    t1 = (lax.broadcasted_iota(jnp.int32, (RB, POOL_GROUP), 0) + 1).astype(F32)
                    mean = acc / jnp.minimum(t1, float(win))
                else:
                    mean = acc * (1.0 / win)
                out_ref[r0:r0 + RB, cols] = (mean - u).astype(BF16)

    return pl.pallas_call(
        body, name="pool_forward",
        out_shape=jax.ShapeDtypeStruct((S, POOL_W), BF16),
        in_specs=[ANY], out_specs=VMEM_SPEC,
        scratch_shapes=[pltpu.VMEM((S + POOL_PAD, POOL_W), F32), pltpu.SemaphoreType.DMA],
        compiler_params=_params(),
    )(proj)


def _pool_backward(dpooled, after):
    S = dpooled.shape[0]
    RB = _row_tile(S, 256)

    def body(dp_ref, after_ref, out_ref, pad_ref, sem):
        cp = pltpu.make_async_copy(dp_ref, pad_ref.at[pl.ds(0, S)], sem)
        cp.start()
        pad_ref[S:S + POOL_PAD, :] = jnp.zeros((POOL_PAD, POOL_W), F32)
        cp.wait()
        for g, win in enumerate(POOL_WINDOWS):
            cols = slice(g * POOL_GROUP, (g + 1) * POOL_GROUP)
            head = pad_ref[0:POOL_PAD, cols]
            t1 = (lax.broadcasted_iota(jnp.int32, (POOL_PAD, POOL_GROUP), 0) + 1).astype(F32)
            pad_ref[0:POOL_PAD, cols] = head * (float(win) / jnp.minimum(t1, float(win)))
            for r0 in range(0, S, RB):
                acc = pad_ref[r0:r0 + RB, cols]
                for k in range(1, win):
                    acc = acc + pad_ref[r0 + k:r0 + k + RB, cols]
                own = pad_ref[r0:r0 + RB, cols]
                if r0 == 0:
                    own = jnp.concatenate([head, own[POOL_PAD:]], axis=0)
                out_ref[r0:r0 + RB, cols] = (acc * (1.0 / win) - own).astype(BF16)

    return pl.pallas_call(
        body, name="pool_backward",
        out_shape=jax.ShapeDtypeStruct((S, POOL_W), BF16),
        in_specs=[ANY, ANY], out_specs=VMEM_SPEC,
        scratch_shapes=[pltpu.VMEM((S + POOL_PAD, POOL_W), F32), pltpu.SemaphoreType.DMA],
        compiler_params=_params(),
    )(dpooled, after)


def _mix_out(y_mla, pooled, w_pool, pool_scale, w_o, x, mod6, T):
    S = x.shape[0]

    def body(ym_ref, pl_ref, wp_ref, ps_ref, wo_ref, x_ref, mod_ref, x1_ref, mix_ref, mi_ref):
        mi_ref[:, 0:512] = ym_ref[...]
        for g in range(len(POOL_WINDOWS)):
            cols = slice(g * POOL_GROUP, (g + 1) * POOL_GROUP)
            z = _dot(pl_ref[:, cols], wp_ref[g])
            mi_ref[:, 512 + g * POOL_GROUP:512 + (g + 1) * POOL_GROUP] = (z * ps_ref[:, cols]).astype(BF16)
        mix = _dot(mi_ref[...], wo_ref[...])
        mix_ref[...] = mix.astype(BF16)
        x1_ref[...] = x_ref[...] + mod_ref[2:3, :] * mix

    row = lambda w: pl.BlockSpec((T, w), lambda i: (i, 0))
    return pl.pallas_call(
        body, name="mix_out", grid=(S // T,),
        out_shape=[pltpu.HBM((S, D_MODEL), F32), pltpu.HBM((S, D_MODEL), BF16), pltpu.HBM((S, 1024), BF16)],
        in_specs=[row(512), row(POOL_W), _full((4, POOL_GROUP, POOL_GROUP)), _full((1, POOL_W)),
                  _full((1024, D_MODEL)), row(D_MODEL), _full((N_MOD, D_MODEL))],
        out_specs=[row(D_MODEL), row(D_MODEL), row(1024)],
        compiler_params=_params(("parallel",)),
    )(*_hbm(y_mla, pooled, w_pool, pool_scale, w_o, x, mod6))


def _ffn_forward(x1, mod6, g_ffn, g_final, target, w_gate, w_up, w_down, T):
    S = x1.shape[0]

    def body(x1_ref, mod_ref, gf_ref, gl_ref, tgt_ref, wg_ref, wu_ref, wd_ref,
             gate_ref, up_ref, act_ref, h2_ref, dff_ref, dx2_ref, st_ref, acc_s):
        i, j = pl.program_id(0), pl.program_id(1)

        @pl.when(jnp.logical_and(i == 0, j == 0))
        def _():
            st_ref[...] = jnp.zeros_like(st_ref)

        @pl.when(j == 0)
        def _():
            xh, _ = _rms(x1_ref[...])
            h2_ref[...] = ((xh * gf_ref[...]) * (1.0 + mod_ref[4:5, :]) + mod_ref[3:4, :]).astype(BF16)
            acc_s[...] = jnp.zeros_like(acc_s)

        h2 = h2_ref[...]
        gate = _dot_nt(h2, wg_ref[j])
        up = _dot_nt(h2, wu_ref[j])
        gate_ref[...] = gate.astype(BF16)
        up_ref[...] = up.astype(BF16)
        act = (gate * jax.nn.sigmoid(gate) * up).astype(BF16)
        act_ref[...] = act
        acc_s[...] += _dot(act, wd_ref[j])

        @pl.when(j == N_CHIPS - 1)
        def _():
            ff = acc_s[...]
            x2 = x1_ref[...] + mod_ref[5:6, :] * ff
            xh, r3 = _rms(x2)
            err = xh * gl_ref[...] - tgt_ref[...]
            dy = err * (1.0 / D_MODEL)
            dx2 = _rms_bwd(dy * gl_ref[...], xh, r3)
            dx2_ref[...] = dx2
            dff_ref[...] = (dx2 * mod_ref[5:6, :]).astype(BF16)
            st_ref[0:1, :] += jnp.sum(dy * xh, axis=0, keepdims=True)
            st_ref[1:2, :] += jnp.sum(dx2 * ff, axis=0, keepdims=True)
            st_ref[2:3, :] += 0.5 * jnp.sum(err * dy)

    row = pl.BlockSpec((T, D_MODEL), lambda i, j: (i, 0))
    chunk_out = pl.BlockSpec((None, T, FF_CHUNK), lambda i, j: (j, i, 0))
    big = pltpu.HBM((N_CHIPS, S, FF_CHUNK), BF16)
    wide = pltpu.HBM((S, D_MODEL), BF16)
    return pl.pallas_call(
        body, name="ffn_forward", grid=(S // T, N_CHIPS),
        out_shape=[big, big, big, wide, wide, pltpu.HBM((S, D_MODEL), F32), jax.ShapeDtypeStruct((8, D_MODEL), F32)],
        in_specs=[row, _full((N_MOD, D_MODEL)), _full((1, D_MODEL)), _full((1, D_MODEL)), row,
                  VMEM_SPEC, VMEM_SPEC, VMEM_SPEC],
        out_specs=[chunk_out, chunk_out, chunk_out, row, row, row, _full((8, D_MODEL))],
        scratch_shapes=[pltpu.VMEM((T, D_MODEL), F32)],
        compiler_params=_params(("arbitrary", "arbitrary")),
    )(*_hbm(x1, mod6, g_ffn, g_final, target), w_gate, w_up, w_down)


def _ffn_backward(dx2, x1, dff, gate, up, mod6, g_ffn, w_gate, w_up, w_down, T):
    S = x1.shape[0]

    def body(dx2_ref, x1_ref, dff_ref, gate_ref, up_ref, mod_ref, gf_ref, wg_ref, wu_ref, wd_ref,
             dgate_ref, dup_ref, dx1_ref, st_ref, acc_s):
        i, j = pl.program_id(0), pl.program_id(1)

        @pl.when(jnp.logical_and(i == 0, j == 0))
        def _():
            st_ref[...] = jnp.zeros_like(st_ref)

        @pl.when(j == 0)
        def _():
            acc_s[...] = jnp.zeros_like(acc_s)

        for r0 in range(0, T, T // 2):
            rows = slice(r0, r0 + T // 2)
            gate, up = gate_ref[rows, :].astype(F32), up_ref[rows, :].astype(F32)
            sg = jax.nn.sigmoid(gate)
            dact = _dot_nt(dff_ref[rows, :], wd_ref[j])
            dup = (dact * (gate * sg)).astype(BF16)
            dgate = (dact * up * (sg * (1.0 + gate * (1.0 - sg)))).astype(BF16)
            dup_ref[rows, :] = dup
            dgate_ref[rows, :] = dgate
            acc_s[rows, :] += _dot(dgate, wg_ref[j]) + _dot(dup, wu_ref[j])

        @pl.when(j == N_CHIPS - 1)
        def _():
            dh2 = acc_s[...]
            xh, r2 = _rms(x1_ref[...])
            n2 = xh * gf_ref[...]
            st_ref[0:1, :] += jnp.sum(dh2, axis=0, keepdims=True)
            st_ref[1:2, :] += jnp.sum(dh2 * n2, axis=0, keepdims=True)
            dn2 = dh2 * (1.0 + mod_ref[4:5, :])
            st_ref[2:3, :] += jnp.sum(dn2 * xh, axis=0, keepdims=True)
            dx1_ref[...] = _rms_bwd(dn2 * gf_ref[...], xh, r2) + dx2_ref[...]

    row = pl.BlockSpec((T, D_MODEL), lambda i, j: (i, 0))
    chunk = pl.BlockSpec((None, T, FF_CHUNK), lambda i, j: (j, i, 0))
    big = pltpu.HBM((N_CHIPS, S, FF_CHUNK), BF16)
    return pl.pallas_call(
        body, name="ffn_backward", grid=(S // T, N_CHIPS),
        out_shape=[big, big, pltpu.HBM((S, D_MODEL), F32), jax.ShapeDtypeStruct((8, D_MODEL), F32)],
        in_specs=[row, row, row, chunk, chunk, _full((N_MOD, D_MODEL)), _full((1, D_MODEL)),
                  VMEM_SPEC, VMEM_SPEC, VMEM_SPEC],
        out_specs=[chunk, chunk, row, _full((8, D_MODEL))],
        scratch_shapes=[pltpu.VMEM((T, D_MODEL), F32)],
        compiler_params=_params(("arbitrary", "arbitrary")),
    )(*_hbm(dx2, x1, dff, gate, up, mod6, g_ffn), w_gate, w_up, w_down)


def _tn_matmul(a, b, a_spec, b_spec, groups, m, n, steps, name):
    def body(a_ref, b_ref, o_ref):
        @pl.when(pl.program_id(1) == 0)
        def _():
            o_ref[...] = jnp.zeros_like(o_ref)

        o_ref[...] += _dot_tn(a_ref[...], b_ref[...])

    return pl.pallas_call(
        body, name=name, grid=(groups, steps),
        out_shape=pltpu.HBM((groups, m, n), F32),
        in_specs=[a_spec, b_spec],
        out_specs=pl.BlockSpec((None, m, n), lambda g, i: (g, 0, 0)),
        compiler_params=_params(("parallel", "arbitrary")),
    )(*_hbm(a, b))


def _mix_backward(dx1, mix, mod6, w_o, pooled, w_pool, pool_scale, w_uv_t, o_lat, T, TQ):
    S = dx1.shape[0]

    def body(dx1_ref, mix_ref, mod_ref, wo_ref, pl_ref, wp_ref, ps_ref, wuv_ref, o_ref,
             dmix_ref, dp_ref, do_ref, dr_ref, gp_ref, guv_ref, st_ref):
        @pl.when(pl.program_id(0) == 0)
        def _():
            st_ref[...] = jnp.zeros_like(st_ref)
            gp_ref[...] = jnp.zeros_like(gp_ref)
            guv_ref[...] = jnp.zeros_like(guv_ref)

        dx1 = dx1_ref[...]
        st_ref[0:1, :] += jnp.sum(dx1 * mix_ref[...].astype(F32), axis=0, keepdims=True)
        dmix = (dx1 * mod_ref[2:3, :]).astype(BF16)
        dmix_ref[...] = dmix
        dmi = _dot_nt(dmix, wo_ref[...])
        dym = dmi[:, 0:512].astype(BF16)
        for g in range(len(POOL_WINDOWS)):
            cols = slice(g * POOL_GROUP, (g + 1) * POOL_GROUP)
            dyp = dmi[:, 512 + g * POOL_GROUP:512 + (g + 1) * POOL_GROUP]
            pooled_g = pl_ref[:, cols]
            z = _dot(pooled_g, wp_ref[g])
            st_ref[1:2, cols] += jnp.sum(dyp * z, axis=0, keepdims=True)
            dz = (dyp * ps_ref[:, cols]).astype(BF16)
            gp_ref[g] += _dot_tn(pooled_g, dz)
            dp_ref[:, cols] = _dot_nt(dz, wp_ref[g])
        for h in range(HEADS):
            dym_h = dym[:, h * 128:(h + 1) * 128]
            do = _dot_nt(dym_h, wuv_ref[h]).astype(BF16)
            do_ref[h] = do
            o_h = o_ref[h]
            guv_ref[h] += _dot_tn(o_h, dym_h)
            delta = _col_to_row(jnp.sum(do.astype(F32) * o_h.astype(F32), axis=1, keepdims=True))
            for s in range(T // TQ):
                dr_ref[s, :, h * TQ:(h + 1) * TQ] = delta[:, s * TQ:(s + 1) * TQ]

    row = lambda w: pl.BlockSpec((T, w), lambda i: (i, 0))
    heads = pl.BlockSpec((HEADS, T, KV_LORA), lambda i: (0, i, 0))
    square = jax.ShapeDtypeStruct((4, 128, 128), F32)
    return pl.pallas_call(
        body, name="mix_backward", grid=(S // T,),
        out_shape=[pltpu.HBM((S, D_MODEL), BF16), pltpu.HBM((S, POOL_W), F32), pltpu.HBM((HEADS, S, KV_LORA), BF16),
                   pltpu.HBM((S // TQ, 1, HEADS * TQ), F32), square, square, jax.ShapeDtypeStruct((8, D_MODEL), F32)],
        in_specs=[row(D_MODEL), row(D_MODEL), _full((N_MOD, D_MODEL)), _full((1024, D_MODEL)), row(POOL_W),
                  _full((4, POOL_GROUP, POOL_GROUP)), _full((1, POOL_W)), _full((HEADS, KV_LORA, 128)), heads],
        out_specs=[row(D_MODEL), row(POOL_W), heads,
                   pl.BlockSpec((T // TQ, 1, HEADS * TQ), lambda i: (i, 0, 0)), _full((4, 128, 128)),
                   _full((4, 128, 128)), _full((8, D_MODEL))],
        compiler_params=_params(("arbitrary",)),
    )(*_hbm(dx1, mix, mod6, w_o, pooled, w_pool, pool_scale, w_uv_t, o_lat))


def _attention_bwd(qc, kc, kct, do, lse_rows, delta_rows, TQ):
    S = kc.shape[0]
    R = HEADS * TQ
    nq = S // TQ

    def body(q_ref, do_ref, lser_ref, dr_ref, k_ref, kt_ref, dqt_ref, dk_ref, dqt_s, dv_s):
        i = pl.program_id(0)

        def key_rows(j):
            return pl.ds(pl.multiple_of(j * TQ, TQ), TQ)

        @pl.when(i == 0)
        def _():
            def zero(j, carry):
                dk_ref[key_rows(j), :] = jnp.zeros((TQ, QK_PAD), F32)
                dv_s[key_rows(j), :] = jnp.zeros((TQ, KV_LORA), F32)
                return carry
            lax.fori_loop(0, nq, zero, 0)

        q = q_ref[...].reshape(R, QK_PAD)
        do = do_ref[...].reshape(R, KV_LORA)
        lse, delta = lser_ref[0], dr_ref[0]
        dqt_s[...] = jnp.zeros((QK_PAD, R), F32)

        def step(j, masked):
            rows = key_rows(j)
            k = k_ref[rows, :]
            st = _dot_nt(k, q) * SM_SCALE
            if masked:
                st = jnp.where(_diag_mask(TQ, R), st, -jnp.inf)
            pt = jnp.exp(st - lse)
            dv_s[rows, :] += _dot(pt, do)
            dpt = _dot_nt(k[:, :KV_LORA], do)
            dst = (pt * (dpt - delta)).astype(BF16)
            dk_ref[rows, :] += _dot(dst, q)
            dqt_s[...] += _dot(kt_ref[j], dst)

        def loop(j, carry):
            step(j, False)
            return carry

        lax.fori_loop(0, i, loop, 0)
        step(i, True)
        dqt_ref[...] = dqt_s[...]

        @pl.when(i == nq - 1)
        def _():
            def finish(j, carry):
                rows = key_rows(j)
                dk = dk_ref[rows, :] * SM_SCALE
                dk_ref[rows, 0:KV_LORA] = dk[:, 0:KV_LORA] + dv_s[rows, :]
                dk_ref[rows, KV_LORA:QK_PAD] = dk[:, KV_LORA:QK_PAD]
                return carry
            lax.fori_loop(0, nq, finish, 0)

    tile = lambda w: pl.BlockSpec((HEADS, TQ, w), lambda i: (0, i, 0))
    row = pl.BlockSpec((1, 1, R), lambda i: (i, 0, 0))
    return pl.pallas_call(
        body, name="attention_bwd", grid=(nq,),
        out_shape=[pltpu.HBM((nq, QK_PAD, R), F32), jax.ShapeDtypeStruct((S, QK_PAD), F32)],
        in_specs=[tile(QK_PAD), tile(KV_LORA), row, row, VMEM_SPEC, VMEM_SPEC],
        out_specs=[pl.BlockSpec((None, QK_PAD, R), lambda i: (i, 0, 0)), VMEM_SPEC],
        scratch_shapes=[pltpu.VMEM((QK_PAD, R), F32), pltpu.VMEM((S, KV_LORA), F32)],
        compiler_params=_params(("arbitrary",)),
    )(*_hbm(qc, do, lse_rows, delta_rows), kc, kct)[::-1]


def _pre_attention_backward(x, dx1, proj, q, dqt, dkc, du, cos, sin, mod6, g_mix, g_q, g_kv, w_in, w_uq, w_uk_t, T, TQ):
    S = x.shape[0]

    def body(x_ref, dx1_ref, proj_ref, q_ref, dqt_ref, dkc_ref, du_ref, cos_ref, sin_ref, mod_ref, gm_ref, gq_ref,
             gkv_ref, win_ref, wuq_ref, wuk_ref, gx_ref, dproj_ref, h1_ref, guk_ref, guq_ref, st_ref, dq_ref):
        @pl.when(pl.program_id(0) == 0)
        def _():
            st_ref[...] = jnp.zeros_like(st_ref)
            guk_ref[...] = jnp.zeros_like(guk_ref)
            guq_ref[...] = jnp.zeros_like(guq_ref)

        cos_t, sin_t = cos_ref[...], sin_ref[...]
        low = lax.broadcasted_iota(jnp.int32, (T, 128), 1) < ROPE
        rope_parts = []
        for h in range(HEADS):
            dqc = jnp.concatenate([jnp.transpose(dqt_ref[s, :, h * TQ:(h + 1) * TQ]) for s in range(T // TQ)], axis=0)
            dqc = dqc * SM_SCALE
            dql = dqc[:, 0:KV_LORA].astype(BF16)
            guk_ref[h] += _dot_tn(dql, q_ref[:, h * NOPE:(h + 1) * NOPE])
            dq_ref[:, h * NOPE:(h + 1) * NOPE] = _dot(dql, wuk_ref[h]).astype(BF16)
            rope_parts.append(dqc[:, KV_LORA:QK_PAD])
        for pair in range(2):
            d = jnp.where(low, rope_parts[2 * pair], rope_parts[2 * pair + 1])
            dq_ref[:, O_QA + 128 * pair:O_QA + 128 * (pair + 1)] = _rope_bwd(d, cos_t, sin_t).astype(BF16)
        dq = dq_ref[...]
        dcq = _dot_nt(dq, wuq_ref[...])
        cqh, rq = _rms(proj_ref[:, 0:Q_LORA])
        guq_ref[...] += _dot_tn(cqh * gq_ref[...], dq)
        st_ref[3:4, 0:Q_LORA] += jnp.sum(dcq * cqh, axis=0, keepdims=True)
        dproj_ref[:, 0:Q_LORA] = _rms_bwd(dcq * gq_ref[...], cqh, rq).astype(BF16)
        dckv = dkc_ref[:, 0:KV_LORA]
        ckvh, rkv = _rms(proj_ref[:, O_CKV:O_KR])
        st_ref[4:5, 0:KV_LORA] += jnp.sum(dckv * ckvh, axis=0, keepdims=True)
        dproj_ref[:, O_CKV:O_KR] = _rms_bwd(dckv * gkv_ref[...], ckvh, rkv).astype(BF16)
        dkr = _rope_bwd(dkc_ref[:, KV_LORA:QK_PAD], cos_t, sin_t)
        dkr = jnp.where(low, dkr + pltpu.roll(dkr, ROPE, 1), 0.0)
        dproj_ref[:, O_KR:O_U] = dkr.astype(BF16)
        dproj_ref[:, O_U:PROJ_W] = du_ref[...].astype(BF16)
        dproj = dproj_ref[...]
        dh1 = jnp.concatenate([_dot(dproj, win_ref[j]) for j in range(N_CHIPS)], axis=1)
        xh, r1 = _rms(x_ref[...])
        n1 = xh * gm_ref[...]
        h1_ref[...] = (n1 * (1.0 + mod_ref[1:2, :]) + mod_ref[0:1, :]).astype(BF16)
        st_ref[0:1, :] += jnp.sum(dh1, axis=0, keepdims=True)
        st_ref[1:2, :] += jnp.sum(dh1 * n1, axis=0, keepdims=True)
        dn1 = dh1 * (1.0 + mod_ref[1:2, :])
        st_ref[2:3, :] += jnp.sum(dn1 * xh, axis=0, keepdims=True)
        gx_ref[...] = _rms_bwd(dn1 * gm_ref[...], xh, r1) + dx1_ref[...]

    row = lambda w: pl.BlockSpec((T, w), lambda i: (i, 0))
    return pl.pallas_call(
        body, name="pre_attention_backward", grid=(S // T,),
        out_shape=[jax.ShapeDtypeStruct((S, D_MODEL), F32), pltpu.HBM((S, PROJ_W), BF16),
                   pltpu.HBM((S, D_MODEL), BF16), jax.ShapeDtypeStruct((HEADS, KV_LORA, NOPE), F32),
                   jax.ShapeDtypeStruct((Q_LORA, Q_W), F32), jax.ShapeDtypeStruct((8, D_MODEL), F32)],
        in_specs=[row(D_MODEL), row(D_MODEL), row(O_KR), row(HEADS * NOPE),
                  pl.BlockSpec((T // TQ, QK_PAD, HEADS * TQ), lambda i: (i, 0, 0)),
                  row(QK_PAD), row(POOL_W), row(128), row(128), _full((N_MOD, D_MODEL)), _full((1, D_MODEL)),
                  _full((1, Q_LORA)), _full((1, KV_LORA)), _full((N_CHIPS, PROJ_W, D_MODEL // N_CHIPS)),
                  _full((Q_LORA, Q_W)), _full((HEADS, KV_LORA, NOPE))],
        out_specs=[row(D_MODEL), row(PROJ_W), row(D_MODEL), _full((HEADS, KV_LORA, NOPE)), _full((Q_LORA, Q_W)),
                   _full((8, D_MODEL))],
        scratch_shapes=[pltpu.VMEM((T, Q_W), BF16)],
        compiler_params=_params(("arbitrary",)),
    )(*_hbm(x, dx1, proj, q, dqt, dkc, du, cos, sin, mod6, g_mix, g_q, g_kv, w_in, w_uq, w_uk_t))


def _ada_grads(c_all, dmod_all, chip):
    cols = N_MOD * D_MODEL // N_CHIPS
    width = dmod_all.shape[1]

    def body(col_ref, c_ref, dcol_ref, dall_ref, gw_ref, gb_ref):
        call = c_ref[...]
        act = call * jax.nn.sigmoid(call)
        gw_ref[...] = _dot_tn(act, dcol_ref[...])
        d = dall_ref[...]
        acc = d[0:1, :]
        for b in range(1, 8):
            acc = acc + d[b:b + 1, :]
        gb_ref[...] = acc

    return pl.pallas_call(
        body, name="ada_grads",
        out_shape=[jax.ShapeDtypeStruct((D_MODEL, cols), F32), jax.ShapeDtypeStruct((1, width), F32)],
        grid_spec=pltpu.PrefetchScalarGridSpec(
            num_scalar_prefetch=1, grid=(1,),
            in_specs=[pl.BlockSpec((8, D_MODEL), lambda s, col_ref: (0, 0)),
                      pl.BlockSpec((8, cols), lambda s, col_ref: (0, col_ref[0])),
                      pl.BlockSpec((8, width), lambda s, col_ref: (0, 0))],
            out_specs=[pl.BlockSpec((D_MODEL, cols), lambda s, col_ref: (0, 0)),
                       pl.BlockSpec((1, width), lambda s, col_ref: (0, 0))]),
        compiler_params=_params(("arbitrary",)),
    )(chip, *_hbm(c_all, dmod_all, dmod_all))


def _adamw(w, g, m, v, name, g_is_landing_zone=True):
    rows, rest = w.shape[0], w.shape[1:]
    T = _row_tile(rows, 256)

    def body(w_ref, g_ref, m_ref, v_ref, *outs):
        d_ref, nm_ref, nv_ref = outs[-3:]
        g = g_ref[...]
        if g_is_landing_zone:
            outs[0][...] = g
        m2 = ADAM_B1 * m_ref[...] + (1.0 - ADAM_B1) * g
        v2 = ADAM_B2 * v_ref[...] + (1.0 - ADAM_B2) * (g * g)
        m_hat = m2 / (1.0 - ADAM_B1 ** ADAM_STEP)
        v_hat = v2 / (1.0 - ADAM_B2 ** ADAM_STEP)
        d_ref[...] = -ADAM_LR * (m_hat / (jnp.sqrt(v_hat) + ADAM_EPS) + ADAM_WD * w_ref[...])
        nm_ref[...] = m2
        nv_ref[...] = v2

    zeros = (0,) * len(rest)
    spec = pl.BlockSpec((T,) + rest, lambda i: (i,) + zeros)
    n_out = 4 if g_is_landing_zone else 3
    res = pl.pallas_call(
        body, name=name, grid=(rows // T,),
        out_shape=[jax.ShapeDtypeStruct(w.shape, F32)] * n_out,
        in_specs=[spec] * 4, out_specs=[spec] * n_out,
        compiler_params=_params(("parallel",)),
    )(*_hbm(w, g, m, v))
    return res if g_is_landing_zone else [g] + list(res)


SMALL_NAMES = ("w_uk", "w_uv", "w_pool", "g_mix", "g_q", "g_kv", "pool_scale", "g_ffn", "g_final", "b_ada")
SMALL_ROWS = 1664


def _pack_rows(parts):
    flat = jnp.concatenate([p.reshape(-1) for p in parts])
    pad = (-flat.shape[0]) % 128
    if pad:
        flat = jnp.concatenate([flat, jnp.zeros((pad,), F32)])
    return flat.reshape(-1, 128)


def kernel(x, c, positions, w_ada, b_ada, g_mix, w_in, g_q, g_kv, w_uq, w_uk, w_uv, w_pool, pool_scale, w_o, g_ffn, w_gate, w_up, w_down, g_final, loss_target, m_w_ada, m_b_ada, m_g_mix, m_w_in, m_g_q, m_g_kv, m_w_uq, m_w_uk, m_w_uv, m_w_pool, m_pool_scale, m_w_o, m_g_ffn, m_w_gate, m_w_up, m_w_down, m_g_final, v_w_ada, v_b_ada, v_g_mix, v_w_in, v_g_q, v_g_kv, v_w_uq, v_w_uk, v_w_uv, v_w_pool, v_pool_scale, v_w_o, v_g_ffn, v_w_gate, v_w_up, v_w_down, v_g_final):
    S = x.shape[1]
    T = _row_tile(S, 512)
    TQ = _row_tile(S, 512)
    TW = _row_tile(S, 2048)
    ix, iy, ic = lax.axis_index("x"), lax.axis_index("y"), lax.axis_index("c")
    chip = (2 * ix + iy).astype(jnp.int32)
    chip_arr = chip.reshape(1)
    core_arr = ic.astype(jnp.int32).reshape(1)

    xs, tgt = x[0], loss_target[0]

    tr = lambda a: jnp.transpose(a[0])
    win_t = tr(w_in)
    win_p = jnp.concatenate([win_t[:O_KR + ROPE], win_t[O_KR:O_KR + ROPE], win_t[O_KR + ROPE:]], axis=0).astype(BF16)
    wuq = w_uq[0]
    wuq_p = jnp.concatenate([wuq[:, h, :NOPE] for h in range(HEADS)] + [wuq[:, h, NOPE:] for h in range(HEADS)],
                            axis=1).astype(BF16)
    w_uk_t = jnp.transpose(w_uk[0], (1, 0, 2)).astype(BF16)
    w_uv_t = jnp.transpose(w_uv[0], (1, 0, 2)).astype(BF16)
    w_pool_b = w_pool[0].astype(BF16)
    first = [win_p, wuq_p]
    later = [w_o[0].astype(BF16), tr(w_gate).astype(BF16), tr(w_up).astype(BF16), w_down[0].astype(BF16)]
    placed = _place_shards(chip_arr, first + later)
    a_send, a_recv, a_lands, token = _split_start("first_weights_start", first, placed[:2], 6, _plan_gather_start)
    half = ROPE // 2
    freqs = jnp.power(ROPE_THETA, -jnp.arange(half, dtype=F32) / half)
    cos, sin = _rope_tables(positions.reshape(S, 1), jnp.tile(freqs, 4).reshape(1, 128) + token[0, 0])
    a_send, a_recv, a_lands, token = _split_relay(
        "first_weights_relay", a_send, a_recv, first, a_lands, cos, 6, _plan_gather_landed, _plan_gather_relay)

    ada_cols = w_ada.shape[2]
    b_cols = lax.dynamic_slice(b_ada, (0, chip * ada_cols), (1, ada_cols))
    mod, c_all = _mod_exchange(c, w_ada[0], b_cols + token[0, 0])
    mod6 = mod.reshape(N_MOD, D_MODEL)
    a_lands = _split_wait("first_weights_wait", a_send, a_recv, [], a_lands, mod, _plan_gather_wait)
    w_in_f = a_lands[0]
    w_uq_f = a_lands[1].reshape(Q_LORA, Q_W)
    wg_lands, mod6, w_in_f = lax.optimization_barrier((placed[2:], mod6, w_in_f))
    wg_send, wg_recv, wg_lands, token = _split_start(
        "weights_start", later, wg_lands, 3 * len(later), _plan_gather_start)
    mod6 = mod6 + token[0, 0]

    proj, q, qc, kc, kct = _pre_attention(xs, mod6, g_mix, g_q, g_kv, w_in_f, w_uq_f, w_uk_t, cos, sin, T, TQ)
    o_lat, y_mla, lse_rows = _attention_fwd(qc, kc, kct, w_uv_t, TQ)
    wg_send, wg_recv, wg_lands, token = _split_relay(
        "weights_relay", wg_send, wg_recv, later, wg_lands, y_mla, 3 * len(later), _plan_gather_landed,
        _plan_gather_relay)
    pooled = _pool_forward(proj)
    wg_lands = _split_wait("weights_wait", wg_send, wg_recv, [], wg_lands, pooled, _plan_gather_wait)
    w_o_f = wg_lands[0].reshape(1024, D_MODEL)
    w_gate_f, w_up_f, w_down_f = wg_lands[1], wg_lands[2], wg_lands[3]
    x1, mix, mix_in = _mix_out(y_mla, pooled, w_pool_b, pool_scale, w_o_f, xs, mod6, T)
    gate, up, act, h2, dff, dx2, st_f = _ffn_forward(
        x1, mod6, g_ffn, g_final.reshape(1, D_MODEL), tgt, w_gate_f, w_up_f, w_down_f, T)

    dgate, dup, dx1, st_b = _ffn_backward(dx2, x1, dff, gate, up, mod6, g_ffn, w_gate_f, w_up_f, w_down_f, T)
    steps = S // TW
    chunk_spec = pl.BlockSpec((None, TW, FF_CHUNK), lambda g, i: (g, i, 0))
    wide_spec = pl.BlockSpec((TW, D_MODEL), lambda g, i: (i, 0))
    g_down = _tn_matmul(act, dff, chunk_spec, wide_spec, N_CHIPS, FF_CHUNK, D_MODEL, steps, "grad_w_down")
    g_gate = _tn_matmul(dgate, h2, chunk_spec, wide_spec, N_CHIPS, FF_CHUNK, D_MODEL, steps, "grad_w_gate")
    g_up = _tn_matmul(dup, h2, chunk_spec, wide_spec, N_CHIPS, FF_CHUNK, D_MODEL, steps, "grad_w_up")

    half_shapes = lambda gs: [jax.ShapeDtypeStruct((N_CHIPS, g.shape[1] // 2, g.shape[2]), F32) for g in gs]
    ffn_grads = [g_gate, g_up, g_down]
    f_send, f_recv, f_lands, token = _split_start(
        "ffn_swap_start", ffn_grads, half_shapes(ffn_grads), len(ffn_grads), _plan_swap_start)
    dmix, dpooled, do_lat, delta_rows, g_pool, g_uv_t, st_m = _mix_backward(
        dx1, mix, mod6 + token[0, 0], w_o_f, pooled, w_pool_b, pool_scale, w_uv_t, o_lat, T, TQ)
    g_o = [_tn_matmul(mix_in, dmix, wide_spec, wide_spec, 1, 1024, D_MODEL, steps, "grad_w_o").reshape(N_CHIPS, -1, D_MODEL)]
    o_send, o_recv, o_lands, token = _split_start("w_o_swap_start", g_o, half_shapes(g_o), 1, _plan_swap_start)
    du = _pool_backward(dpooled, token)
    f_got = _split_wait("ffn_swap_wait", f_send, f_recv, ffn_grads, f_lands, du, _plan_swap_wait)
    f_got += _split_wait("w_o_swap_wait", o_send, o_recv, g_o, o_lands, du, _plan_swap_wait)
    far_grads = ffn_grads + g_o
    f_sums = _add_my_halves(core_arr, far_grads, f_got, "add_half_far")
    f_send, f_recv, f_lands, token = _split_start(
        "far_exchange_start", f_sums, [jax.ShapeDtypeStruct((3,) + s.shape[1:], F32) for s in f_sums],
        3 * len(f_sums), _plan_exchange_start)
    delta_rows = delta_rows + token[0, 0]
    dkc, dqt = _attention_bwd(qc, kc, kct, do_lat, lse_rows, delta_rows, TQ)
    grad_x, dproj, h1, g_uk_t, uq, st_p = _pre_attention_backward(
        xs, dx1, proj, q, dqt, dkc, du, cos, sin, mod6, g_mix, g_q, g_kv, w_in_f, w_uq_f, w_uk_t, T, TQ)
    rows_in = D_MODEL // N_CHIPS
    g_in_p = _tn_matmul(dproj, h1, pl.BlockSpec((TW, PROJ_W), lambda g, i: (i, 0)),
                        pl.BlockSpec((TW, rows_in), lambda g, i: (i, g)), N_CHIPS, PROJ_W, rows_in, steps, "grad_w_in")

    g_in = jnp.concatenate([g_in_p[:, :O_KR + ROPE], g_in_p[:, O_U:]], axis=1)
    g_uq = jnp.concatenate([jnp.concatenate([uq[:, h * NOPE:(h + 1) * NOPE], uq[:, O_QA + h * ROPE:O_QA + (h + 1) * ROPE]],
                                            axis=1) for h in range(HEADS)], axis=1).reshape(N_CHIPS, -1, HEADS * HEAD_QK)
    small = _pack_rows([g_uk_t, g_uv_t, g_pool, st_p[2], st_p[3, :Q_LORA], st_p[4, :KV_LORA], st_m[1, :POOL_W],
                        st_b[2], st_f[0]])
    small = jnp.concatenate([small, jnp.zeros((SMALL_ROWS - small.shape[0], 128), F32)]).reshape(N_CHIPS, -1, 128)
    grads = [g_in, g_uq, small]
    dmod = jnp.concatenate([jnp.stack([st_p[0], st_p[1], st_m[0], st_b[0], st_b[1], st_f[1]]).reshape(48, 128),
                            jnp.zeros((8, 128), F32).at[0, 0].set(st_f[2, 0])])

    got, dmod_all = _grad_swap_halves(grads, dmod)
    chip_sums = _add_my_halves(core_arr, grads, got, "add_half_near")
    n_send, n_recv, n_lands, token = _split_start(
        "near_exchange_start", chip_sums, [jax.ShapeDtypeStruct((3,) + s.shape[1:], F32) for s in chip_sums],
        3 * len(chip_sums), _plan_exchange_start)

    f_others = _split_wait("far_exchange_wait", f_send, f_recv, f_sums, f_lands, token, _plan_exchange_wait)
    chip_core = jnp.concatenate([chip_arr, core_arr])
    f_pairs = (_add_chips_into_pairs(chip_core, f_sums[:2], f_others[:2], "add_chips_gate_up")
               + _add_chips_into_pairs(chip_core, f_sums[2:], f_others[2:], "add_chips_down_o"))
    f_send, f_recv, f_pairs, token = _split_start("far_finish_start", [], f_pairs, len(f_pairs), _plan_finish_start)
    gw_ada, gb_ada = _ada_grads(c_all, dmod_all.reshape(8, -1) + token[0, 0], chip_arr)
    loss = gb_ada[0, N_MOD * D_MODEL]
    gb_ada = gb_ada[:, :N_MOD * D_MODEL]
    f_fulls = _split_wait("far_finish_wait", f_send, f_recv, [], f_pairs, gw_ada, _plan_finish_wait)
    gw_gate, gw_up, gw_down, gw_o = [f.reshape(-1, f.shape[2]) for f in f_fulls]

    untr = lambda a: jnp.transpose(a)[None]
    grad_out, delta_out, newm_out, newv_out = {}, {}, {}, {}

    def adam_sharded(n, w, g2, m, v, transposed, landed=True):
        view = (lambda a: jnp.transpose(a[0])) if transposed else (lambda a: a[0])
        back = untr if transposed else (lambda a: a[None])
        g_, d_, m_, v_ = _adamw(view(w), g2.reshape(view(w).shape), view(m), view(v), "adamw_" + n, landed)
        grad_out[n], delta_out[n], newm_out[n], newv_out[n] = back(g_), back(d_), back(m_), back(v_)
        return d_

    done = [adam_sharded("w_gate", w_gate, gw_gate, m_w_gate, v_w_gate, True),
            adam_sharded("w_up", w_up, gw_up, m_w_up, v_w_up, True),
            adam_sharded("w_down", w_down, gw_down, m_w_down, v_w_down, False),
            adam_sharded("w_o", w_o, gw_o, m_w_o, v_w_o, False)]
    after_all = jnp.stack([d[0, 0] for d in done])

    others = _split_wait("near_exchange_wait", n_send, n_recv, chip_sums, n_lands, after_all, _plan_exchange_wait)
    n_pairs = _add_chips_into_pairs(chip_core, chip_sums[:2], others[:2], "add_chips_in_uq")
    small_grid = _add_chips_into_grid(chip_core, chip_sums[2], others[2], "add_chips_small")
    n_send, n_recv, n_lands, token = _split_start(
        "near_finish_start", [], n_pairs + [small_grid], 2 + len(RELATIONS), _plan_near_finish_start)
    gw_ada, _ = lax.optimization_barrier((gw_ada, token))
    d_ada = adam_sharded("w_ada", w_ada, gw_ada, m_w_ada, v_w_ada, False, landed=False)
    n_lands = _split_wait("near_finish_wait", n_send, n_recv, [], n_lands, d_ada, _plan_near_finish_wait)
    gw_in, gw_uq = [f.reshape(-1, f.shape[2]) for f in n_lands[:2]]
    small_all = n_lands[2].reshape(SMALL_ROWS * 128)
    adam_sharded("w_in", w_in, gw_in, m_w_in, v_w_in, True)
    adam_sharded("w_uq", w_uq, gw_uq, m_w_uq, v_w_uq, False)

    n_sq = KV_LORA * HEADS * 128
    sizes = [n_sq, n_sq, n_sq, D_MODEL, Q_LORA, KV_LORA, POOL_W, D_MODEL, D_MODEL]
    offs = [0]
    for s_ in sizes:
        offs.append(offs[-1] + s_)
    piece = lambda k: small_all[offs[k]:offs[k + 1]]
    grads_small = {
        "w_uk": jnp.transpose(piece(0).reshape(HEADS, KV_LORA, NOPE), (1, 0, 2)),
        "w_uv": jnp.transpose(piece(1).reshape(HEADS, KV_LORA, 128), (1, 0, 2)),
        "w_pool": piece(2).reshape(4, POOL_GROUP, POOL_GROUP),
        "g_mix": piece(3), "g_q": piece(4), "g_kv": piece(5), "pool_scale": piece(6), "g_ffn": piece(7),
        "g_final": piece(8), "b_ada": gb_ada.reshape(-1),
    }
    weights_small = {"w_uk": w_uk, "w_uv": w_uv, "w_pool": w_pool, "g_mix": g_mix, "g_q": g_q, "g_kv": g_kv,
                     "pool_scale": pool_scale, "g_ffn": g_ffn, "g_final": g_final, "b_ada": b_ada}
    m_small = {"w_uk": m_w_uk, "w_uv": m_w_uv, "w_pool": m_w_pool, "g_mix": m_g_mix, "g_q": m_g_q, "g_kv": m_g_kv,
               "pool_scale": m_pool_scale, "g_ffn": m_g_ffn, "g_final": m_g_final, "b_ada": m_b_ada}
    v_small = {"w_uk": v_w_uk, "w_uv": v_w_uv, "w_pool": v_w_pool, "g_mix": v_g_mix, "g_q": v_g_q, "g_kv": v_g_kv,
               "pool_scale": v_pool_scale, "g_ffn": v_g_ffn, "g_final": v_g_final, "b_ada": v_b_ada}
    pack = lambda d: _pack_rows([d[n] for n in SMALL_NAMES])
    _, d_s, m_s, v_s = _adamw(pack(weights_small), pack(grads_small), pack(m_small), pack(v_small), "adamw_small",
                              g_is_landing_zone=False)

    def unpack(flat2d):
        flat = flat2d.reshape(-1)
        out, o = {}, 0
        for n in SMALL_NAMES:
            size = weights_small[n].size
            out[n] = flat[o:o + size].reshape(weights_small[n].shape)
            o += size
        return out

    delta_s, newm_s, newv_s = unpack(d_s), unpack(m_s), unpack(v_s)

    for n in SMALL_NAMES:
        grad_out[n] = grads_small[n].reshape(weights_small[n].shape)
        delta_out[n], newm_out[n], newv_out[n] = delta_s[n], newm_s[n], newv_s[n]

    order = ("w_ada", "b_ada", "g_mix", "w_in", "g_q", "g_kv", "w_uq", "w_uk", "w_uv", "w_pool", "pool_scale", "w_o",
             "g_ffn", "w_gate", "w_up", "w_down", "g_final")
    return (loss, grad_x.reshape(x.shape), *[grad_out[n] for n in order], *[delta_out[n] for n in order],
            *[newm_out[n] for n in order], *[newv_out[n] for n in order])
```

```python
import functools

import jax
import jax.numpy as jnp
from jax import lax
from jax.experimental import pallas as pl
from jax.experimental.pallas import tpu as pltpu

F32 = jnp.float32
BF16 = jnp.bfloat16

D_MODEL = 1024
HEADS = 4
NOPE = 128
ROPE = 64
HEAD_QK = NOPE + ROPE
Q_LORA = 256
KV_LORA = 128
POOL_W = 512
POOL_WINDOWS = (2, 4, 8, 16)
POOL_GROUP = 128
POOL_PAD = 16
D_FF = 2816
N_CHIPS = 4
FF_CHUNK = D_FF // N_CHIPS
N_MOD = 6
EPS = 1e-6
SM_SCALE = HEAD_QK ** -0.5
ROPE_THETA = 10000.0
QK_PAD = 256
CHUNK = 64
CHUNK_SHIFT = 6

ADAM_LR = 0.001
ADAM_B1 = 0.9
ADAM_B2 = 0.999
ADAM_EPS = 1e-08
ADAM_WD = 0.01
ADAM_STEP = 10

VMEM_LIMIT = 48 * 1024 * 1024
MESH = pl.DeviceIdType.MESH
ANY = pl.BlockSpec(memory_space=pl.ANY)
VMEM_SPEC = pl.BlockSpec(memory_space=pltpu.VMEM)

PROJ_W = 1024
O_CKV = 256
O_KR = 384
O_U = 512
Q_W = 768
O_QA = 512
O_QB = 640


def _params(sem=None, vmem=VMEM_LIMIT):
    kw = dict(vmem_limit_bytes=vmem)
    if sem is not None:
        kw["dimension_semantics"] = sem
    return pltpu.CompilerParams(**kw)


def _dot(a, b):
    return jnp.dot(a.astype(BF16), b.astype(BF16), preferred_element_type=F32)


def _dot_nt(a, b):
    return lax.dot_general(a.astype(BF16), b.astype(BF16), (((1,), (1,)), ((), ())), preferred_element_type=F32)


def _dot_tn(a, b):
    return lax.dot_general(a.astype(BF16), b.astype(BF16), (((0,), (0,)), ((), ())), preferred_element_type=F32)


def _row_tile(rows, target):
    best = rows
    for t in range(8, min(rows, target) + 1, 8):
        if rows % t == 0:
            best = t
    return best if rows % best == 0 and best <= target else rows


def _rms(x):
    r = lax.rsqrt(jnp.mean(x * x, axis=-1, keepdims=True) + EPS)
    return x * r, r


def _rms_bwd(dxh, xh, r):
    return r * (dxh - xh * jnp.mean(dxh * xh, axis=-1, keepdims=True))


def _lane_first_half(shape):
    lane = lax.broadcasted_iota(jnp.int32, shape, 1)
    return (lane & (ROPE - 1)) < (ROPE // 2)


def _rope(a, cos, sin):
    first = _lane_first_half(a.shape)
    up = pltpu.roll(a, 96, 1)
    dn = pltpu.roll(a, 32, 1)
    return a * cos + jnp.where(first, -up, dn) * sin


def _rope_bwd(d, cos, sin):
    first = _lane_first_half(d.shape)
    up = pltpu.roll(d, 96, 1)
    dn = pltpu.roll(d, 32, 1)
    return d * cos + jnp.where(first, up, -dn) * sin


RELATIONS = tuple((dx, dy, dc) for dx in (0, 1) for dy in (0, 1) for dc in (0, 1) if (dx, dy, dc) != (0, 0, 0))
CHIP_RELATIONS = ((1, 0), (0, 1), (1, 1))


def _flip(v, d):
    return 1 - v if d else v


def _place():
    return lax.axis_index("x"), lax.axis_index("y"), lax.axis_index("c")


def _remote(src, dst, send_sem, recv_sem, target):
    return pltpu.make_async_remote_copy(src_ref=src, dst_ref=dst, send_sem=send_sem, recv_sem=recv_sem,
                                        device_id=target, device_id_type=MESH)


def _mod_exchange(c_row, w_ada, b_ada):
    cols = w_ada.shape[1]

    def body(c_ref, w_ref, b_ref, mod_ref, call_ref, part_ref, send1, recv1, loc1, send2, recv2, loc2):
        x, y, c = _place()
        me = 4 * x + 2 * y + c
        own = pltpu.make_async_copy(c_ref, call_ref.at[pl.ds(me, 1)], loc1)
        own.start()
        sends = []
        for k, (dx, dy, dc) in enumerate(RELATIONS):
            cp = _remote(c_ref, call_ref.at[pl.ds(me, 1)], send1.at[k], recv1.at[k],
                         (_flip(x, dx), _flip(y, dy), _flip(c, dc)))
            cp.start()
            sends.append(cp)
        for k, (dx, dy, dc) in enumerate(RELATIONS):
            src = 4 * _flip(x, dx) + 2 * _flip(y, dy) + _flip(c, dc)
            _remote(c_ref, call_ref.at[pl.ds(src, 1)], send1.at[k], recv1.at[k], (x, y, c)).wait_recv()
        own.wait()
        for cp in sends:
            cp.wait_send()
        call = call_ref[...]
        act = call * jax.nn.sigmoid(call)
        part_ref[...] = _dot(act, w_ref[...]) + b_ref[...]
        chip = 2 * x + y
        mine = pltpu.make_async_copy(part_ref.at[pl.ds(me, 1)], mod_ref.at[pl.ds(chip, 1)], loc2)
        mine.start()
        sends = []
        for k, (dx, dy) in enumerate(CHIP_RELATIONS):
            tx, ty = _flip(x, dx), _flip(y, dy)
            tb = 4 * tx + 2 * ty + c
            cp = _remote(part_ref.at[pl.ds(tb, 1)], mod_ref.at[pl.ds(chip, 1)], send2.at[k], recv2.at[k], (tx, ty, c))
            cp.start()
            sends.append(cp)
        for k, (dx, dy) in enumerate(CHIP_RELATIONS):
            src_chip = 2 * _flip(x, dx) + _flip(y, dy)
            _remote(part_ref.at[pl.ds(me, 1)], mod_ref.at[pl.ds(src_chip, 1)], send2.at[k], recv2.at[k],
                    (x, y, c)).wait_recv()
        mine.wait()
        for cp in sends:
            cp.wait_send()

    return pl.pallas_call(
        body, name="mod_exchange",
        out_shape=[jax.ShapeDtypeStruct((N_CHIPS, cols), F32), jax.ShapeDtypeStruct((8, D_MODEL), F32)],
        in_specs=[VMEM_SPEC, VMEM_SPEC, VMEM_SPEC], out_specs=[VMEM_SPEC, VMEM_SPEC],
        scratch_shapes=[pltpu.VMEM((8, cols), F32),
                        pltpu.SemaphoreType.DMA((7,)), pltpu.SemaphoreType.DMA((7,)), pltpu.SemaphoreType.DMA,
                        pltpu.SemaphoreType.DMA((3,)), pltpu.SemaphoreType.DMA((3,)), pltpu.SemaphoreType.DMA],
        compiler_params=_params(),
    )(c_row, w_ada, b_ada)


HBM_SPEC = pl.BlockSpec(memory_space=pltpu.HBM)
SEM_SPEC = pl.BlockSpec(memory_space=pltpu.SEMAPHORE)
DATAFLOW = pltpu.SideEffectType.DATAFLOW_SIDE_EFFECTING


def _in_hbm(a):
    return pltpu.with_memory_space_constraint(a, pltpu.HBM)


def _hbm(*arrays):
    return tuple(_in_hbm(a) for a in arrays)


def _hbm_like(arrays):
    return [pltpu.HBM(a.shape, a.dtype) for a in arrays]


def _split_start(name, srcs, lands, n_remote, plan):
    lands = [lax.empty(a.shape, a.dtype) if isinstance(a, jax.ShapeDtypeStruct) else a for a in lands]
    n, m = len(srcs), len(lands)

    def body(*refs):
        src_refs, land_refs = refs[:n], refs[n:n + m]
        send_sems, recv_sems, token = refs[n + m], refs[n + m + 1], refs[n + 2 * m + 2]
        remote = plan(_place(), src_refs, land_refs)
        assert len(remote) == n_remote
        for i, (s, d, target) in enumerate(remote):
            _remote(s, d, send_sems.at[i], recv_sems.at[i], target).start()
        token[...] = jnp.zeros_like(token)

    res = pl.pallas_call(
        body, name=name,
        out_shape=(pltpu.SemaphoreType.DMA((n_remote,)), pltpu.SemaphoreType.DMA((n_remote,)),
                   *_hbm_like(lands), jax.ShapeDtypeStruct((8, 128), F32)),
        in_specs=[HBM_SPEC] * (n + m),
        out_specs=(SEM_SPEC, SEM_SPEC, *([HBM_SPEC] * m), VMEM_SPEC),
        input_output_aliases={n + i: 2 + i for i in range(m)},
        compiler_params=pltpu.CompilerParams(has_side_effects=DATAFLOW),
    )(*[_in_hbm(a) for a in srcs], *[_in_hbm(a) for a in lands])
    return res[0], res[1], list(res[2:2 + m]), res[2 + m]


def _split_wait(name, send_sems, recv_sems, srcs, lands, after, plan):
    n, m = len(srcs), len(lands)

    def body(*refs):
        src_refs, land_refs = refs[:n], refs[n:n + m]
        send_sems, recv_sems = refs[n + m], refs[n + m + 1]
        place = _place()
        for i, (s, d) in enumerate(plan(place, src_refs, land_refs)):
            cp = _remote(s, d, send_sems.at[i], recv_sems.at[i], place)
            cp.wait_send()
            cp.wait_recv()

    res = pl.pallas_call(
        body, name=name,
        out_shape=tuple(_hbm_like(lands)),
        in_specs=[HBM_SPEC] * (n + m) + [SEM_SPEC, SEM_SPEC, ANY],
        out_specs=tuple([HBM_SPEC] * m),
        input_output_aliases={n + i: i for i in range(m)},
        compiler_params=pltpu.CompilerParams(has_side_effects=DATAFLOW),
    )(*srcs, *lands, send_sems, recv_sems, after)
    return list(res)


def _split_relay(name, send_sems, recv_sems, srcs, lands, after, n_remote, plan_wait, plan_send):
    n, m = len(srcs), len(lands)

    def body(*refs):
        src_refs, land_refs = refs[:n], refs[n:n + m]
        old_send, old_recv = refs[n + m], refs[n + m + 1]
        new_send, new_recv = refs[n + m + 3], refs[n + m + 4]
        token = refs[n + m + 5 + m]
        place = _place()
        for i, (s, d) in enumerate(plan_wait(place, src_refs, land_refs)):
            cp = _remote(s, d, old_send.at[i], old_recv.at[i], place)
            cp.wait_send()
            cp.wait_recv()
        for i, (s, d, target) in enumerate(plan_send(place, land_refs)):
            _remote(s, d, new_send.at[i], new_recv.at[i], target).start()
        token[...] = jnp.zeros_like(token)

    res = pl.pallas_call(
        body, name=name,
        out_shape=(pltpu.SemaphoreType.DMA((n_remote,)), pltpu.SemaphoreType.DMA((n_remote,)),
                   *_hbm_like(lands), jax.ShapeDtypeStruct((8, 128), F32)),
        in_specs=[HBM_SPEC] * (n + m) + [SEM_SPEC, SEM_SPEC, ANY],
        out_specs=(SEM_SPEC, SEM_SPEC, *([HBM_SPEC] * m), VMEM_SPEC),
        input_output_aliases={n + i: 2 + i for i in range(m)},
        compiler_params=pltpu.CompilerParams(has_side_effects=DATAFLOW),
    )(*srcs, *lands, send_sems, recv_sems, after)
    return res[0], res[1], list(res[2:2 + m]), res[2 + m]


def _half(ref, core, axis=0):
    hr = ref.shape[axis] // 2
    return pl.ds(core * hr, hr)


def _plan_gather_start(place, src, land):
    x, y, c = place
    chip = 2 * x + y
    return [(s.at[_half(s, c)], l.at[chip, _half(s, c)], (_flip(x, dx), _flip(y, dy), c))
            for s, l in zip(src, land) for dx, dy in CHIP_RELATIONS]


def _plan_gather_landed(place, src, land):
    x, y, c = place
    return [(s.at[_half(s, c)], l.at[2 * _flip(x, dx) + _flip(y, dy), _half(s, c)])
            for s, l in zip(src, land) for dx, dy in CHIP_RELATIONS]


def _plan_gather_relay(place, land):
    x, y, c = place
    out = []
    for l in land:
        for dx, dy in CHIP_RELATIONS:
            got = l.at[2 * _flip(x, dx) + _flip(y, dy), _half(l, c, 1)]
            out.append((got, got, (x, y, 1 - c)))
    return out


def _plan_gather_wait(place, src, land):
    x, y, c = place
    out = []
    for l in land:
        for dx, dy in CHIP_RELATIONS:
            got = l.at[2 * _flip(x, dx) + _flip(y, dy), _half(l, 1 - c, 1)]
            out.append((got, got))
    return out


def _plan_swap_start(place, src, land):
    x, y, c = place
    return [(s.at[:, _half(s, 1 - c, 1), :], l, (x, y, 1 - c)) for s, l in zip(src, land)]


def _plan_swap_wait(place, src, land):
    return [(s.at[:, _half(s, 0, 1), :], l) for s, l in zip(src, land)]


def _plan_exchange_start(place, src, land):
    x, y, c = place
    remote = []
    for s, l in zip(src, land):
        for k, (dx, dy) in enumerate(CHIP_RELATIONS):
            tx, ty = _flip(x, dx), _flip(y, dy)
            remote.append((s.at[2 * tx + ty], l.at[k], (tx, ty, c)))
    return remote


def _plan_exchange_wait(place, src, land):
    return [(s.at[0], l.at[k]) for s, l in zip(src, land) for k in range(3)]


def _plan_finish_start(place, src, land):
    x, y, c = place
    return [(l.at[c], l.at[c], (x, y, 1 - c)) for l in land]


def _plan_finish_wait(place, src, land):
    x, y, c = place
    return [(l.at[c], l.at[1 - c]) for l in land]


def _plan_near_finish_start(place, src, land):
    x, y, c = place
    mine = land[-1].at[2 * x + y, c]
    return (_plan_finish_start(place, src, land[:-1])
            + [(mine, mine, (_flip(x, dx), _flip(y, dy), _flip(c, dc))) for dx, dy, dc in RELATIONS])


def _plan_near_finish_wait(place, src, land):
    x, y, c = place
    mine = land[-1].at[2 * x + y, c]
    return (_plan_finish_wait(place, src, land[:-1])
            + [(mine, land[-1].at[2 * _flip(x, dx) + _flip(y, dy), _flip(c, dc)]) for dx, dy, dc in RELATIONS])


def _grad_swap_halves(grads, dmod):
    n = len(grads)

    def body(*refs):
        ins, dmod_ref = refs[:n], refs[n]
        outs, dall_ref = refs[n + 1:2 * n + 1], refs[2 * n + 1]
        send_sems, recv_sems, dsend, drecv, dloc = refs[2 * n + 2:]
        x, y, c = _place()
        me = 4 * x + 2 * y + c
        sends = []
        for w in range(n):
            hr = ins[w].shape[1] // 2
            cp = _remote(ins[w].at[:, pl.ds((1 - c) * hr, hr), :], outs[w], send_sems.at[w], recv_sems.at[w],
                         (x, y, 1 - c))
            cp.start()
            sends.append(cp)
        own = pltpu.make_async_copy(dmod_ref, dall_ref.at[me], dloc)
        own.start()
        for k, (dx, dy, dc) in enumerate(RELATIONS):
            cp = _remote(dmod_ref, dall_ref.at[me], dsend.at[k], drecv.at[k],
                         (_flip(x, dx), _flip(y, dy), _flip(c, dc)))
            cp.start()
            sends.append(cp)
        for k, (dx, dy, dc) in enumerate(RELATIONS):
            src = 4 * _flip(x, dx) + 2 * _flip(y, dy) + _flip(c, dc)
            _remote(dmod_ref, dall_ref.at[src], dsend.at[k], drecv.at[k], (x, y, c)).wait_recv()
        for w in range(n):
            _remote(outs[w], outs[w], send_sems.at[w], recv_sems.at[w], (x, y, c)).wait_recv()
        own.wait()
        for cp in sends:
            cp.wait_send()

    out_shape = [pltpu.HBM((N_CHIPS, g.shape[1] // 2, g.shape[2]), F32) for g in grads]
    out_shape.append(pltpu.HBM((8,) + dmod.shape, F32))
    res = pl.pallas_call(
        body, name="grad_swap_halves",
        out_shape=out_shape, in_specs=[ANY] * n + [VMEM_SPEC], out_specs=[ANY] * (n + 1),
        scratch_shapes=[pltpu.SemaphoreType.DMA((n,)), pltpu.SemaphoreType.DMA((n,)),
                        pltpu.SemaphoreType.DMA((7,)), pltpu.SemaphoreType.DMA((7,)), pltpu.SemaphoreType.DMA],
        compiler_params=_params(),
    )(*grads, dmod)
    return res[:n], res[n]


def _add_my_halves(core, fulls, gots, name):
    n = len(fulls)

    def body(core_ref, *refs):
        for w in range(n):
            refs[2 * n + w][...] = refs[w][...] + refs[n + w][...]

    mine = lambda g: pl.BlockSpec((None,) + g.shape[1:], lambda s, core_ref: (s, core_ref[0], 0))
    slab = lambda g: pl.BlockSpec((None,) + g.shape[1:], lambda s, core_ref: (s, 0, 0))
    return list(pl.pallas_call(
        body, name=name,
        out_shape=[pltpu.HBM(g.shape, F32) for g in gots],
        grid_spec=pltpu.PrefetchScalarGridSpec(
            num_scalar_prefetch=1, grid=(N_CHIPS,),
            in_specs=[mine(g) for g in gots] + [slab(g) for g in gots],
            out_specs=[slab(g) for g in gots]),
        compiler_params=_params(("arbitrary",)),
    )(core, *_hbm(*fulls, *gots)))


def _add_chips_into_pairs(chip_core, mines, gots, name):
    n = len(mines)

    def body(cc_ref, *refs):
        for w in range(n):
            b_ref = refs[n + w]
            refs[2 * n + w][...] = ((refs[w][...] + b_ref[0]) + b_ref[1]) + b_ref[2]

    return list(pl.pallas_call(
        body, name=name,
        out_shape=[pltpu.HBM((2,) + m.shape[1:], F32) for m in mines],
        grid_spec=pltpu.PrefetchScalarGridSpec(
            num_scalar_prefetch=1, grid=(1,),
            in_specs=[pl.BlockSpec((None,) + m.shape[1:], lambda s, cc_ref: (cc_ref[0], 0, 0)) for m in mines]
            + [pl.BlockSpec(g.shape, lambda s, cc_ref: (0, 0, 0)) for g in gots],
            out_specs=[pl.BlockSpec((None,) + m.shape[1:], lambda s, cc_ref: (cc_ref[1], 0, 0)) for m in mines]),
        compiler_params=_params(("arbitrary",)),
    )(chip_core, *_hbm(*mines, *gots)))


def _add_chips_into_grid(chip_core, mine, got, name):
    _, hr, cols = mine.shape

    def body(cc_ref, a_ref, b_ref, o_ref):
        o_ref[...] = ((a_ref[...] + b_ref[0]) + b_ref[1]) + b_ref[2]

    return pl.pallas_call(
        body, name=name,
        out_shape=pltpu.HBM((N_CHIPS, 2, hr, cols), F32),
        grid_spec=pltpu.PrefetchScalarGridSpec(
            num_scalar_prefetch=1, grid=(1,),
            in_specs=[pl.BlockSpec((None, hr, cols), lambda s, cc_ref: (cc_ref[0], 0, 0)),
                      pl.BlockSpec((3, hr, cols), lambda s, cc_ref: (0, 0, 0))],
            out_specs=pl.BlockSpec((None, None, hr, cols), lambda s, cc_ref: (cc_ref[0], cc_ref[1], 0, 0))),
        compiler_params=_params(("arbitrary",)),
    )(chip_core, *_hbm(mine, got))


def _place_shards(chip, shards):
    n = len(shards)

    def body(chip_ref, *refs):
        for w in range(n):
            refs[n + w][...] = refs[w][...]

    return pl.pallas_call(
        body, name="place_shards",
        out_shape=[pltpu.HBM((N_CHIPS,) + s.shape, s.dtype) for s in shards],
        grid_spec=pltpu.PrefetchScalarGridSpec(
            num_scalar_prefetch=1, grid=(1,),
            in_specs=[pl.BlockSpec(s.shape, lambda i, chip_ref: (0, 0)) for s in shards],
            out_specs=[pl.BlockSpec((None,) + s.shape, lambda i, chip_ref: (chip_ref[0], 0, 0)) for s in shards]),
        compiler_params=_params(("arbitrary",)),
    )(chip, *shards)


def _rope_tables(pos_col, freqs):
    S = pos_col.shape[0]
    T = _row_tile(S, 1024)

    def body(p_ref, f_ref, cos_ref, sin_ref):
        ang = p_ref[...].astype(F32) * f_ref[...]
        cos_ref[...] = jnp.cos(ang)
        sin_ref[...] = jnp.sin(ang)

    return pl.pallas_call(
        body, name="rope_tables", grid=(S // T,),
        out_shape=[pltpu.HBM((S, 128), F32)] * 2,
        in_specs=[pl.BlockSpec((T, 1), lambda i: (i, 0)), pl.BlockSpec((1, 128), lambda i: (0, 0))],
        out_specs=[pl.BlockSpec((T, 128), lambda i: (i, 0))] * 2,
        compiler_params=_params(("parallel",)),
    )(*_hbm(pos_col, freqs))


def _full(shape):
    zeros = (0,) * len(shape)
    return pl.BlockSpec(shape, lambda *_: zeros)


def _pre_attention(x, mod6, g_mix, g_q, g_kv, w_in, w_uq, w_uk_t, cos, sin, T, TQ):
    S = x.shape[0]

    def body(x_ref, mod_ref, gm_ref, gq_ref, gkv_ref, win_ref, wuq_ref, wuk_ref, cos_ref, sin_ref,
             proj_ref, q_ref, qc_ref, kc_ref, kct_ref):
        xh, _ = _rms(x_ref[...])
        h1 = ((xh * gm_ref[...]) * (1.0 + mod_ref[1:2, :]) + mod_ref[0:1, :]).astype(BF16)
        rows_in = D_MODEL // N_CHIPS
        proj = _dot_nt(h1[:, 0:rows_in], win_ref[0])
        for j in range(1, N_CHIPS):
            proj = proj + _dot_nt(h1[:, j * rows_in:(j + 1) * rows_in], win_ref[j])
        proj_ref[...] = proj
        cqh, _ = _rms(proj[:, :Q_LORA])
        c_q = cqh * gq_ref[...]
        ckvh, _ = _rms(proj[:, O_CKV:O_KR])
        c_kv = ckvh * gkv_ref[...]
        q = _dot(c_q, wuq_ref[...])
        q_ref[...] = q.astype(BF16)
        cos_t, sin_t = cos_ref[...], sin_ref[...]
        ropes = (_rope(q[:, O_QA:O_QB], cos_t, sin_t), _rope(q[:, O_QB:Q_W], cos_t, sin_t))
        low = lax.broadcasted_iota(jnp.int32, (T, 128), 1) < ROPE
        for h in range(HEADS):
            q_lat = _dot_nt(q[:, h * NOPE:(h + 1) * NOPE], wuk_ref[h])
            keep = low if h % 2 == 0 else jnp.logical_not(low)
            qc_ref[h, :, 0:KV_LORA] = q_lat.astype(BF16)
            qc_ref[h, :, KV_LORA:QK_PAD] = jnp.where(keep, ropes[h // 2], 0.0).astype(BF16)
        k_rope = _rope(proj[:, O_KR:O_U], cos_t, sin_t)
        kc_ref[:, 0:KV_LORA] = c_kv.astype(BF16)
        kc_ref[:, KV_LORA:QK_PAD] = k_rope.astype(BF16)
        lat_t, rope_t = jnp.transpose(c_kv), jnp.transpose(k_rope)
        for s in range(T // TQ):
            kct_ref[s, 0:KV_LORA, :] = lat_t[:, s * TQ:(s + 1) * TQ].astype(BF16)
            kct_ref[s, KV_LORA:QK_PAD, :] = rope_t[:, s * TQ:(s + 1) * TQ].astype(BF16)

    row = lambda w: pl.BlockSpec((T, w), lambda i: (i, 0))
    return pl.pallas_call(
        body, name="pre_attention", grid=(S // T,),
        out_shape=[pltpu.HBM((S, PROJ_W), F32), pltpu.HBM((S, Q_W), BF16), pltpu.HBM((HEADS, S, QK_PAD), BF16),
                   pltpu.HBM((S, QK_PAD), BF16), pltpu.HBM((S // TQ, QK_PAD, TQ), BF16)],
        in_specs=[row(D_MODEL), _full((N_MOD, D_MODEL)), _full((1, D_MODEL)), _full((1, Q_LORA)), _full((1, KV_LORA)),
                  _full((N_CHIPS, PROJ_W, D_MODEL // N_CHIPS)), _full((Q_LORA, Q_W)), _full((HEADS, KV_LORA, NOPE)),
                  row(128), row(128)],
        out_specs=[row(PROJ_W), row(Q_W), pl.BlockSpec((HEADS, T, QK_PAD), lambda i: (0, i, 0)), row(QK_PAD),
                   pl.BlockSpec((T // TQ, QK_PAD, TQ), lambda i: (i, 0, 0))],
        compiler_params=_params(("parallel",)),
    )(*_hbm(x, mod6, g_mix, g_q, g_kv, w_in, w_uq, w_uk_t, cos, sin))


def _diag_mask(TQ, width):
    key = lax.broadcasted_iota(jnp.int32, (TQ, width), 0) >> CHUNK_SHIFT
    qry = (lax.broadcasted_iota(jnp.int32, (TQ, width), 1) & (TQ - 1)) >> CHUNK_SHIFT
    return key <= qry


def _col_to_row(col):
    return jnp.transpose(jnp.broadcast_to(col, (col.shape[0], 128)))[0:1, :]


def _attention_fwd(qc, kc, kct, w_uv_t, TQ):
    S = kc.shape[0]
    R = HEADS * TQ
    nq = S // TQ

    def body(q_ref, k_ref, kt_ref, wuv_ref, o_ref, y_ref, lser_ref, m_s, l_s, acc_s, st_s):
        i = pl.program_id(0)
        q = q_ref[...].reshape(R, QK_PAD)
        m_s[...] = jnp.full((1, R), -jnp.inf, F32)
        l_s[...] = jnp.zeros((1, R), F32)
        acc_s[...] = jnp.zeros((KV_LORA, R), F32)

        def scores(j):
            return _dot_nt(k_ref[pl.ds(pl.multiple_of(j * TQ, TQ), TQ), :], q) * SM_SCALE

        def update(j, st):
            m_old = m_s[...]
            m_new = jnp.maximum(m_old, jnp.max(st, axis=0, keepdims=True))
            pt = jnp.exp(st - m_new)
            alpha = jnp.exp(m_old - m_new)
            l_s[...] = alpha * l_s[...] + jnp.sum(pt, axis=0, keepdims=True)
            acc_s[...] = alpha * acc_s[...] + _dot(kt_ref[j, 0:KV_LORA, :], pt)
            m_s[...] = m_new

        st_s[...] = scores(0)

        def loop(j, carry):
            st = st_s[...]
            st_s[...] = scores(j + 1)
            update(j, st)
            return carry

        lax.fori_loop(0, i, loop, 0)
        update(i, jnp.where(_diag_mask(TQ, R), st_s[...], -jnp.inf))
        l = l_s[...]
        lser_ref[0] = m_s[...] + jnp.log(l)
        o = jnp.transpose(acc_s[...] / l).astype(BF16)
        for h in range(HEADS):
            oh = o[h * TQ:(h + 1) * TQ, :]
            o_ref[h] = oh
            y_ref[:, h * 128:(h + 1) * 128] = _dot(oh, wuv_ref[h]).astype(BF16)

    return pl.pallas_call(
        body, name="attention_fwd", grid=(nq,),
        out_shape=[pltpu.HBM((HEADS, S, KV_LORA), BF16), pltpu.HBM((S, HEADS * 128), BF16),
                   pltpu.HBM((nq, 1, R), F32)],
        in_specs=[pl.BlockSpec((HEADS, TQ, QK_PAD), lambda i: (0, i, 0)), _full((S, QK_PAD)),
                  _full((nq, QK_PAD, TQ)), _full((HEADS, KV_LORA, 128))],
        out_specs=[pl.BlockSpec((HEADS, TQ, KV_LORA), lambda i: (0, i, 0)), pl.BlockSpec((TQ, HEADS * 128), lambda i: (i, 0)),
                   pl.BlockSpec((1, 1, R), lambda i: (i, 0, 0))],
        scratch_shapes=[pltpu.VMEM((1, R), F32), pltpu.VMEM((1, R), F32), pltpu.VMEM((KV_LORA, R), F32),
                        pltpu.VMEM((TQ, R), F32)],
        compiler_params=_params(("parallel",)),
    )(*_hbm(qc, kc, kct, w_uv_t))


def _pool_forward(proj):
    S = proj.shape[0]
    RB = _row_tile(S, 256)

    def body(proj_ref, out_ref, pad_ref, sem):
        cp = pltpu.make_async_copy(proj_ref.at[:, pl.ds(O_U, POOL_W)], pad_ref.at[pl.ds(POOL_PAD, S)], sem)
        cp.start()
        pad_ref[0:POOL_PAD, :] = jnp.zeros((POOL_PAD, POOL_W), F32)
        cp.wait()
        for g, win in enumerate(POOL_WINDOWS):
            cols = slice(g * POOL_GROUP, (g + 1) * POOL_GROUP)
            for r0 in range(0, S, RB):
                u = pad_ref[POOL_PAD + r0:POOL_PAD + r0 + RB, cols]
                acc = u
                for k in range(1, win):
                    acc = acc + pad_ref[POOL_PAD + r0 - k:POOL_PAD + r0 - k + RB, cols]
                if r0 == 0:
                    t1 = (lax.broadcasted_iota(jnp.int32, (RB, POOL_GROUP), 0) + 1).astype(F32)
                    mean = acc / jnp.minimum(t1, float(win))
                else:
                    mean = acc * (1.0 / win)
                out_ref[r0:r0 + RB, cols] = (mean - u).astype(BF16)

    return pl.pallas_call(
        body, name="pool_forward",
        out_shape=jax.ShapeDtypeStruct((S, POOL_W), BF16),
        in_specs=[ANY], out_specs=VMEM_SPEC,
        scratch_shapes=[pltpu.VMEM((S + POOL_PAD, POOL_W), F32), pltpu.SemaphoreType.DMA],
        compiler_params=_params(),
    )(proj)


def _pool_backward(dpooled, after):
    S = dpooled.shape[0]
    RB = _row_tile(S, 256)

    def body(dp_ref, after_ref, out_ref, pad_ref, sem):
        cp = pltpu.make_async_copy(dp_ref, pad_ref.at[pl.ds(0, S)], sem)
        cp.start()
        pad_ref[S:S + POOL_PAD, :] = jnp.zeros((POOL_PAD, POOL_W), F32)
        cp.wait()
        for g, win in enumerate(POOL_WINDOWS):
            cols = slice(g * POOL_GROUP, (g + 1) * POOL_GROUP)
            head = pad_ref[0:POOL_PAD, cols]
            t1 = (lax.broadcasted_iota(jnp.int32, (POOL_PAD, POOL_GROUP), 0) + 1).astype(F32)
            pad_ref[0:POOL_PAD, cols] = head * (float(win) / jnp.minimum(t1, float(win)))
            for r0 in range(0, S, RB):
                acc = pad_ref[r0:r0 + RB, cols]
                for k in range(1, win):
                    acc = acc + pad_ref[r0 + k:r0 + k + RB, cols]
                own = pad_ref[r0:r0 + RB, cols]
                if r0 == 0:
                    own = jnp.concatenate([head, own[POOL_PAD:]], axis=0)
                out_ref[r0:r0 + RB, cols] = (acc * (1.0 / win) - own).astype(BF16)

    return pl.pallas_call(
        body, name="pool_backward",
        out_shape=jax.ShapeDtypeStruct((S, POOL_W), BF16),
        in_specs=[ANY, ANY], out_specs=VMEM_SPEC,
        scratch_shapes=[pltpu.VMEM((S + POOL_PAD, POOL_W), F32), pltpu.SemaphoreType.DMA],
        compiler_params=_params(),
    )(dpooled, after)


def _mix_out(y_mla, pooled, w_pool, pool_scale, w_o, x, mod6, T):
    S = x.shape[0]

    def body(ym_ref, pl_ref, wp_ref, ps_ref, wo_ref, x_ref, mod_ref, x1_ref, mix_ref, mi_ref):
        mi_ref[:, 0:512] = ym_ref[...]
        for g in range(len(POOL_WINDOWS)):
            cols = slice(g * POOL_GROUP, (g + 1) * POOL_GROUP)
            z = _dot(pl_ref[:, cols], wp_ref[g])
            mi_ref[:, 512 + g * POOL_GROUP:512 + (g + 1) * POOL_GROUP] = (z * ps_ref[:, cols]).astype(BF16)
        mix = _dot(mi_ref[...], wo_ref[...])
        mix_ref[...] = mix.astype(BF16)
        x1_ref[...] = x_ref[...] + mod_ref[2:3, :] * mix

    row = lambda w: pl.BlockSpec((T, w), lambda i: (i, 0))
    return pl.pallas_call(
        body, name="mix_out", grid=(S // T,),
        out_shape=[pltpu.HBM((S, D_MODEL), F32), pltpu.HBM((S, D_MODEL), BF16), pltpu.HBM((S, 1024), BF16)],
        in_specs=[row(512), row(POOL_W), _full((4, POOL_GROUP, POOL_GROUP)), _full((1, POOL_W)),
                  _full((1024, D_MODEL)), row(D_MODEL), _full((N_MOD, D_MODEL))],
        out_specs=[row(D_MODEL), row(D_MODEL), row(1024)],
        compiler_params=_params(("parallel",)),
    )(*_hbm(y_mla, pooled, w_pool, pool_scale, w_o, x, mod6))


def _ffn_forward(x1, mod6, g_ffn, g_final, target, w_gate, w_up, w_down, T):
    S = x1.shape[0]

    def body(x1_ref, mod_ref, gf_ref, gl_ref, tgt_ref, wg_ref, wu_ref, wd_ref,
             gate_ref, up_ref, act_ref, h2_ref, dff_ref, dx2_ref, st_ref, acc_s):
        i, j = pl.program_id(0), pl.program_id(1)

        @pl.when(jnp.logical_and(i == 0, j == 0))
        def _():
            st_ref[...] = jnp.zeros_like(st_ref)

        @pl.when(j == 0)
        def _():
            xh, _ = _rms(x1_ref[...])
            h2_ref[...] = ((xh * gf_ref[...]) * (1.0 + mod_ref[4:5, :]) + mod_ref[3:4, :]).astype(BF16)
            acc_s[...] = jnp.zeros_like(acc_s)

        h2 = h2_ref[...]
        gate = _dot_nt(h2, wg_ref[j])
        up = _dot_nt(h2, wu_ref[j])
        gate_ref[...] = gate.astype(BF16)
        up_ref[...] = up.astype(BF16)
        act = (gate * jax.nn.sigmoid(gate) * up).astype(BF16)
        act_ref[...] = act
        acc_s[...] += _dot(act, wd_ref[j])

        @pl.when(j == N_CHIPS - 1)
        def _():
            ff = acc_s[...]
            x2 = x1_ref[...] + mod_ref[5:6, :] * ff
            xh, r3 = _rms(x2)
            err = xh * gl_ref[...] - tgt_ref[...]
            dy = err * (1.0 / D_MODEL)
            dx2 = _rms_bwd(dy * gl_ref[...], xh, r3)
            dx2_ref[...] = dx2
            dff_ref[...] = (dx2 * mod_ref[5:6, :]).astype(BF16)
            st_ref[0:1, :] += jnp.sum(dy * xh, axis=0, keepdims=True)
            st_ref[1:2, :] += jnp.sum(dx2 * ff, axis=0, keepdims=True)
            st_ref[2:3, :] += 0.5 * jnp.sum(err * dy)

    row = pl.BlockSpec((T, D_MODEL), lambda i, j: (i, 0))
    chunk_out = pl.BlockSpec((None, T, FF_CHUNK), lambda i, j: (j, i, 0))
    big = pltpu.HBM((N_CHIPS, S, FF_CHUNK), BF16)
    wide = pltpu.HBM((S, D_MODEL), BF16)
    return pl.pallas_call(
        body, name="ffn_forward", grid=(S // T, N_CHIPS),
        out_shape=[big, big, big, wide, wide, pltpu.HBM((S, D_MODEL), F32), jax.ShapeDtypeStruct((8, D_MODEL), F32)],
        in_specs=[row, _full((N_MOD, D_MODEL)), _full((1, D_MODEL)), _full((1, D_MODEL)), row,
                  VMEM_SPEC, VMEM_SPEC, VMEM_SPEC],
        out_specs=[chunk_out, chunk_out, chunk_out, row, row, row, _full((8, D_MODEL))],
        scratch_shapes=[pltpu.VMEM((T, D_MODEL), F32)],
        compiler_params=_params(("arbitrary", "arbitrary")),
    )(*_hbm(x1, mod6, g_ffn, g_final, target), w_gate, w_up, w_down)


def _ffn_backward(dx2, x1, dff, gate, up, mod6, g_ffn, w_gate, w_up, w_down, T):
    S = x1.shape[0]

    def body(dx2_ref, x1_ref, dff_ref, gate_ref, up_ref, mod_ref, gf_ref, wg_ref, wu_ref, wd_ref,
             dgate_ref, dup_ref, dx1_ref, st_ref, acc_s):
        i, j = pl.program_id(0), pl.program_id(1)

        @pl.when(jnp.logical_and(i == 0, j == 0))
        def _():
            st_ref[...] = jnp.zeros_like(st_ref)

        @pl.when(j == 0)
        def _():
            acc_s[...] = jnp.zeros_like(acc_s)

        for r0 in range(0, T, T // 2):
            rows = slice(r0, r0 + T // 2)
            gate, up = gate_ref[rows, :].astype(F32), up_ref[rows, :].astype(F32)
            sg = jax.nn.sigmoid(gate)
            dact = _dot_nt(dff_ref[rows, :], wd_ref[j])
            dup = (dact * (gate * sg)).astype(BF16)
            dgate = (dact * up * (sg * (1.0 + gate * (1.0 - sg)))).astype(BF16)
            dup_ref[rows, :] = dup
            dgate_ref[rows, :] = dgate
            acc_s[rows, :] += _dot(dgate, wg_ref[j]) + _dot(dup, wu_ref[j])

        @pl.when(j == N_CHIPS - 1)
        def _():
            dh2 = acc_s[...]
            xh, r2 = _rms(x1_ref[...])
            n2 = xh * gf_ref[...]
            st_ref[0:1, :] += jnp.sum(dh2, axis=0, keepdims=True)
            st_ref[1:2, :] += jnp.sum(dh2 * n2, axis=0, keepdims=True)
            dn2 = dh2 * (1.0 + mod_ref[4:5, :])
            st_ref[2:3, :] += jnp.sum(dn2 * xh, axis=0, keepdims=True)
            dx1_ref[...] = _rms_bwd(dn2 * gf_ref[...], xh, r2) + dx2_ref[...]

    row = pl.BlockSpec((T, D_MODEL), lambda i, j: (i, 0))
    chunk = pl.BlockSpec((None, T, FF_CHUNK), lambda i, j: (j, i, 0))
    big = pltpu.HBM((N_CHIPS, S, FF_CHUNK), BF16)
    return pl.pallas_call(
        body, name="ffn_backward", grid=(S // T, N_CHIPS),
        out_shape=[big, big, pltpu.HBM((S, D_MODEL), F32), jax.ShapeDtypeStruct((8, D_MODEL), F32)],
        in_specs=[row, row, row, chunk, chunk, _full((N_MOD, D_MODEL)), _full((1, D_MODEL)),
                  VMEM_SPEC, VMEM_SPEC, VMEM_SPEC],
        out_specs=[chunk, chunk, row, _full((8, D_MODEL))],
        scratch_shapes=[pltpu.VMEM((T, D_MODEL), F32)],
        compiler_params=_params(("arbitrary", "arbitrary")),
    )(*_hbm(dx2, x1, dff, gate, up, mod6, g_ffn), w_gate, w_up, w_down)


def _tn_matmul(a, b, a_spec, b_spec, groups, m, n, steps, name):
    def body(a_ref, b_ref, o_ref):
        @pl.when(pl.program_id(1) == 0)
        def _():
            o_ref[...] = jnp.zeros_like(o_ref)

        o_ref[...] += _dot_tn(a_ref[...], b_ref[...])

    return pl.pallas_call(
        body, name=name, grid=(groups, steps),
        out_shape=pltpu.HBM((groups, m, n), F32),
        in_specs=[a_spec, b_spec],
        out_specs=pl.BlockSpec((None, m, n), lambda g, i: (g, 0, 0)),
        compiler_params=_params(("parallel", "arbitrary")),
    )(*_hbm(a, b))


def _mix_backward(dx1, mix, mod6, w_o, pooled, w_pool, pool_scale, w_uv_t, o_lat, T, TQ):
    S = dx1.shape[0]

    def body(dx1_ref, mix_ref, mod_ref, wo_ref, pl_ref, wp_ref, ps_ref, wuv_ref, o_ref,
             dmix_ref, dp_ref, do_ref, dr_ref, gp_ref, guv_ref, st_ref):
        @pl.when(pl.program_id(0) == 0)
        def _():
            st_ref[...] = jnp.zeros_like(st_ref)
            gp_ref[...] = jnp.zeros_like(gp_ref)
            guv_ref[...] = jnp.zeros_like(guv_ref)

        dx1 = dx1_ref[...]
        st_ref[0:1, :] += jnp.sum(dx1 * mix_ref[...].astype(F32), axis=0, keepdims=True)
        dmix = (dx1 * mod_ref[2:3, :]).astype(BF16)
        dmix_ref[...] = dmix
        dmi = _dot_nt(dmix, wo_ref[...])
        dym = dmi[:, 0:512].astype(BF16)
        for g in range(len(POOL_WINDOWS)):
            cols = slice(g * POOL_GROUP, (g + 1) * POOL_GROUP)
            dyp = dmi[:, 512 + g * POOL_GROUP:512 + (g + 1) * POOL_GROUP]
            pooled_g = pl_ref[:, cols]
            z = _dot(pooled_g, wp_ref[g])
            st_ref[1:2, cols] += jnp.sum(dyp * z, axis=0, keepdims=True)
            dz = (dyp * ps_ref[:, cols]).astype(BF16)
            gp_ref[g] += _dot_tn(pooled_g, dz)
            dp_ref[:, cols] = _dot_nt(dz, wp_ref[g])
        for h in range(HEADS):
            dym_h = dym[:, h * 128:(h + 1) * 128]
            do = _dot_nt(dym_h, wuv_ref[h]).astype(BF16)
            do_ref[h] = do
            o_h = o_ref[h]
            guv_ref[h] += _dot_tn(o_h, dym_h)
            delta = _col_to_row(jnp.sum(do.astype(F32) * o_h.astype(F32), axis=1, keepdims=True))
            for s in range(T // TQ):
                dr_ref[s, :, h * TQ:(h + 1) * TQ] = delta[:, s * TQ:(s + 1) * TQ]

    row = lambda w: pl.BlockSpec((T, w), lambda i: (i, 0))
    heads = pl.BlockSpec((HEADS, T, KV_LORA), lambda i: (0, i, 0))
    square = jax.ShapeDtypeStruct((4, 128, 128), F32)
    return pl.pallas_call(
        body, name="mix_backward", grid=(S // T,),
        out_shape=[pltpu.HBM((S, D_MODEL), BF16), pltpu.HBM((S, POOL_W), F32), pltpu.HBM((HEADS, S, KV_LORA), BF16),
                   pltpu.HBM((S // TQ, 1, HEADS * TQ), F32), square, square, jax.ShapeDtypeStruct((8, D_MODEL), F32)],
        in_specs=[row(D_MODEL), row(D_MODEL), _full((N_MOD, D_MODEL)), _full((1024, D_MODEL)), row(POOL_W),
                  _full((4, POOL_GROUP, POOL_GROUP)), _full((1, POOL_W)), _full((HEADS, KV_LORA, 128)), heads],
        out_specs=[row(D_MODEL), row(POOL_W), heads,
                   pl.BlockSpec((T // TQ, 1, HEADS * TQ), lambda i: (i, 0, 0)), _full((4, 128, 128)),
                   _full((4, 128, 128)), _full((8, D_MODEL))],
        compiler_params=_params(("arbitrary",)),
    )(*_hbm(dx1, mix, mod6, w_o, pooled, w_pool, pool_scale, w_uv_t, o_lat))


def _attention_bwd(qc, kc, kct, do, lse_rows, delta_rows, TQ):
    S = kc.shape[0]
    R = HEADS * TQ
    nq = S // TQ

    def body(q_ref, do_ref, lser_ref, dr_ref, k_ref, kt_ref, dqt_ref, dk_ref, dqt_s, dv_s):
        i = pl.program_id(0)

        def key_rows(j):
            return pl.ds(pl.multiple_of(j * TQ, TQ), TQ)

        @pl.when(i == 0)
        def _():
            def zero(j, carry):
                dk_ref[key_rows(j), :] = jnp.zeros((TQ, QK_PAD), F32)
                dv_s[key_rows(j), :] = jnp.zeros((TQ, KV_LORA), F32)
                return carry
            lax.fori_loop(0, nq, zero, 0)

        q = q_ref[...].reshape(R, QK_PAD)
        do = do_ref[...].reshape(R, KV_LORA)
        lse, delta = lser_ref[0], dr_ref[0]
        dqt_s[...] = jnp.zeros((QK_PAD, R), F32)

        def step(j, masked):
            rows = key_rows(j)
            k = k_ref[rows, :]
            st = _dot_nt(k, q) * SM_SCALE
            if masked:
                st = jnp.where(_diag_mask(TQ, R), st, -jnp.inf)
            pt = jnp.exp(st - lse)
            dv_s[rows, :] += _dot(pt, do)
            dpt = _dot_nt(k[:, :KV_LORA], do)
            dst = (pt * (dpt - delta)).astype(BF16)
            dk_ref[rows, :] += _dot(dst, q)
            dqt_s[...] += _dot(kt_ref[j], dst)

        def loop(j, carry):
            step(j, False)
            return carry

        lax.fori_loop(0, i, loop, 0)
        step(i, True)
        dqt_ref[...] = dqt_s[...]

        @pl.when(i == nq - 1)
        def _():
            def finish(j, carry):
                rows = key_rows(j)
                dk = dk_ref[rows, :] * SM_SCALE
                dk_ref[rows, 0:KV_LORA] = dk[:, 0:KV_LORA] + dv_s[rows, :]
                dk_ref[rows, KV_LORA:QK_PAD] = dk[:, KV_LORA:QK_PAD]
                return carry
            lax.fori_loop(0, nq, finish, 0)

    tile = lambda w: pl.BlockSpec((HEADS, TQ, w), lambda i: (0, i, 0))
    row = pl.BlockSpec((1, 1, R), lambda i: (i, 0, 0))
    return pl.pallas_call(
        body, name="attention_bwd", grid=(nq,),
        out_shape=[pltpu.HBM((nq, QK_PAD, R), F32), jax.ShapeDtypeStruct((S, QK_PAD), F32)],
        in_specs=[tile(QK_PAD), tile(KV_LORA), row, row, VMEM_SPEC, VMEM_SPEC],
        out_specs=[pl.BlockSpec((None, QK_PAD, R), lambda i: (i, 0, 0)), VMEM_SPEC],
        scratch_shapes=[pltpu.VMEM((QK_PAD, R), F32), pltpu.VMEM((S, KV_LORA), F32)],
        compiler_params=_params(("arbitrary",)),
    )(*_hbm(qc, do, lse_rows, delta_rows), kc, kct)[::-1]


def _pre_attention_backward(x, dx1, proj, q, dqt, dkc, du, cos, sin, mod6, g_mix, g_q, g_kv, w_in, w_uq, w_uk_t, T, TQ):
    S = x.shape[0]

    def body(x_ref, dx1_ref, proj_ref, q_ref, dqt_ref, dkc_ref, du_ref, cos_ref, sin_ref, mod_ref, gm_ref, gq_ref,
             gkv_ref, win_ref, wuq_ref, wuk_ref, gx_ref, dproj_ref, h1_ref, guk_ref, guq_ref, st_ref, dq_ref):
        @pl.when(pl.program_id(0) == 0)
        def _():
            st_ref[...] = jnp.zeros_like(st_ref)
            guk_ref[...] = jnp.zeros_like(guk_ref)
            guq_ref[...] = jnp.zeros_like(guq_ref)

        cos_t, sin_t = cos_ref[...], sin_ref[...]
        low = lax.broadcasted_iota(jnp.int32, (T, 128), 1) < ROPE
        rope_parts = []
        for h in range(HEADS):
            dqc = jnp.concatenate([jnp.transpose(dqt_ref[s, :, h * TQ:(h + 1) * TQ]) for s in range(T // TQ)], axis=0)
            dqc = dqc * SM_SCALE
            dql = dqc[:, 0:KV_LORA].astype(BF16)
            guk_ref[h] += _dot_tn(dql, q_ref[:, h * NOPE:(h + 1) * NOPE])
            dq_ref[:, h * NOPE:(h + 1) * NOPE] = _dot(dql, wuk_ref[h]).astype(BF16)
            rope_parts.append(dqc[:, KV_LORA:QK_PAD])
        for pair in range(2):
            d = jnp.where(low, rope_parts[2 * pair], rope_parts[2 * pair + 1])
            dq_ref[:, O_QA + 128 * pair:O_QA + 128 * (pair + 1)] = _rope_bwd(d, cos_t, sin_t).astype(BF16)
        dq = dq_ref[...]
        dcq = _dot_nt(dq, wuq_ref[...])
        cqh, rq = _rms(proj_ref[:, 0:Q_LORA])
        guq_ref[...] += _dot_tn(cqh * gq_ref[...], dq)
        st_ref[3:4, 0:Q_LORA] += jnp.sum(dcq * cqh, axis=0, keepdims=True)
        dproj_ref[:, 0:Q_LORA] = _rms_bwd(dcq * gq_ref[...], cqh, rq).astype(BF16)
        dckv = dkc_ref[:, 0:KV_LORA]
        ckvh, rkv = _rms(proj_ref[:, O_CKV:O_KR])
        st_ref[4:5, 0:KV_LORA] += jnp.sum(dckv * ckvh, axis=0, keepdims=True)
        dproj_ref[:, O_CKV:O_KR] = _rms_bwd(dckv * gkv_ref[...], ckvh, rkv).astype(BF16)
        dkr = _rope_bwd(dkc_ref[:, KV_LORA:QK_PAD], cos_t, sin_t)
        dkr = jnp.where(low, dkr + pltpu.roll(dkr, ROPE, 1), 0.0)
        dproj_ref[:, O_KR:O_U] = dkr.astype(BF16)
        dproj_ref[:, O_U:PROJ_W] = du_ref[...].astype(BF16)
        dproj = dproj_ref[...]
        dh1 = jnp.concatenate([_dot(dproj, win_ref[j]) for j in range(N_CHIPS)], axis=1)
        xh, r1 = _rms(x_ref[...])
        n1 = xh * gm_ref[...]
        h1_ref[...] = (n1 * (1.0 + mod_ref[1:2, :]) + mod_ref[0:1, :]).astype(BF16)
        st_ref[0:1, :] += jnp.sum(dh1, axis=0, keepdims=True)
        st_ref[1:2, :] += jnp.sum(dh1 * n1, axis=0, keepdims=True)
        dn1 = dh1 * (1.0 + mod_ref[1:2, :])
        st_ref[2:3, :] += jnp.sum(dn1 * xh, axis=0, keepdims=True)
        gx_ref[...] = _rms_bwd(dn1 * gm_ref[...], xh, r1) + dx1_ref[...]

    row = lambda w: pl.BlockSpec((T, w), lambda i: (i, 0))
    return pl.pallas_call(
        body, name="pre_attention_backward", grid=(S // T,),
        out_shape=[jax.ShapeDtypeStruct((S, D_MODEL), F32), pltpu.HBM((S, PROJ_W), BF16),
                   pltpu.HBM((S, D_MODEL), BF16), jax.ShapeDtypeStruct((HEADS, KV_LORA, NOPE), F32),
                   jax.ShapeDtypeStruct((Q_LORA, Q_W), F32), jax.ShapeDtypeStruct((8, D_MODEL), F32)],
        in_specs=[row(D_MODEL), row(D_MODEL), row(O_KR), row(HEADS * NOPE),
                  pl.BlockSpec((T // TQ, QK_PAD, HEADS * TQ), lambda i: (i, 0, 0)),
                  row(QK_PAD), row(POOL_W), row(128), row(128), _full((N_MOD, D_MODEL)), _full((1, D_MODEL)),
                  _full((1, Q_LORA)), _full((1, KV_LORA)), _full((N_CHIPS, PROJ_W, D_MODEL // N_CHIPS)),
                  _full((Q_LORA, Q_W)), _full((HEADS, KV_LORA, NOPE))],
        out_specs=[row(D_MODEL), row(PROJ_W), row(D_MODEL), _full((HEADS, KV_LORA, NOPE)), _full((Q_LORA, Q_W)),
                   _full((8, D_MODEL))],
        scratch_shapes=[pltpu.VMEM((T, Q_W), BF16)],
        compiler_params=_params(("arbitrary",)),
    )(*_hbm(x, dx1, proj, q, dqt, dkc, du, cos, sin, mod6, g_mix, g_q, g_kv, w_in, w_uq, w_uk_t))


def _ada_grads(c_all, dmod_all, chip):
    cols = N_MOD * D_MODEL // N_CHIPS
    width = dmod_all.shape[1]

    def body(col_ref, c_ref, dcol_ref, dall_ref, gw_ref, gb_ref):
        call = c_ref[...]
        act = call * jax.nn.sigmoid(call)
        gw_ref[...] = _dot_tn(act, dcol_ref[...])
        d = dall_ref[...]
        acc = d[0:1, :]
        for b in range(1, 8):
            acc = acc + d[b:b + 1, :]
        gb_ref[...] = acc

    return pl.pallas_call(
        body, name="ada_grads",
        out_shape=[jax.ShapeDtypeStruct((D_MODEL, cols), F32), jax.ShapeDtypeStruct((1, width), F32)],
        grid_spec=pltpu.PrefetchScalarGridSpec(
            num_scalar_prefetch=1, grid=(1,),
            in_specs=[pl.BlockSpec((8, D_MODEL), lambda s, col_ref: (0, 0)),
                      pl.BlockSpec((8, cols), lambda s, col_ref: (0, col_ref[0])),
                      pl.BlockSpec((8, width), lambda s, col_ref: (0, 0))],
            out_specs=[pl.BlockSpec((D_MODEL, cols), lambda s, col_ref: (0, 0)),
                       pl.BlockSpec((1, width), lambda s, col_ref: (0, 0))]),
        compiler_params=_params(("arbitrary",)),
    )(chip, *_hbm(c_all, dmod_all, dmod_all))


def _adamw(w, g, m, v, name, g_is_landing_zone=True):
    rows, rest = w.shape[0], w.shape[1:]
    T = _row_tile(rows, 256)

    def body(w_ref, g_ref, m_ref, v_ref, *outs):
        d_ref, nm_ref, nv_ref = outs[-3:]
        g = g_ref[...]
        if g_is_landing_zone:
            outs[0][...] = g
        m2 = ADAM_B1 * m_ref[...] + (1.0 - ADAM_B1) * g
        v2 = ADAM_B2 * v_ref[...] + (1.0 - ADAM_B2) * (g * g)
        m_hat = m2 / (1.0 - ADAM_B1 ** ADAM_STEP)
        v_hat = v2 / (1.0 - ADAM_B2 ** ADAM_STEP)
        d_ref[...] = -ADAM_LR * (m_hat / (jnp.sqrt(v_hat) + ADAM_EPS) + ADAM_WD * w_ref[...])
        nm_ref[...] = m2
        nv_ref[...] = v2

    zeros = (0,) * len(rest)
    spec = pl.BlockSpec((T,) + rest, lambda i: (i,) + zeros)
    n_out = 4 if g_is_landing_zone else 3
    res = pl.pallas_call(
        body, name=name, grid=(rows // T,),
        out_shape=[jax.ShapeDtypeStruct(w.shape, F32)] * n_out,
        in_specs=[spec] * 4, out_specs=[spec] * n_out,
        compiler_params=_params(("parallel",)),
    )(*_hbm(w, g, m, v))
    return res if g_is_landing_zone else [g] + list(res)


SMALL_NAMES = ("w_uk", "w_uv", "w_pool", "g_mix", "g_q", "g_kv", "pool_scale", "g_ffn", "g_final", "b_ada")
SMALL_ROWS = 1664


def _pack_rows(parts):
    flat = jnp.concatenate([p.reshape(-1) for p in parts])
    pad = (-flat.shape[0]) % 128
    if pad:
        flat = jnp.concatenate([flat, jnp.zeros((pad,), F32)])
    return flat.reshape(-1, 128)


def kernel(x, c, positions, w_ada, b_ada, g_mix, w_in, g_q, g_kv, w_uq, w_uk, w_uv, w_pool, pool_scale, w_o, g_ffn, w_gate, w_up, w_down, g_final, loss_target, m_w_ada, m_b_ada, m_g_mix, m_w_in, m_g_q, m_g_kv, m_w_uq, m_w_uk, m_w_uv, m_w_pool, m_pool_scale, m_w_o, m_g_ffn, m_w_gate, m_w_up, m_w_down, m_g_final, v_w_ada, v_b_ada, v_g_mix, v_w_in, v_g_q, v_g_kv, v_w_uq, v_w_uk, v_w_uv, v_w_pool, v_pool_scale, v_w_o, v_g_ffn, v_w_gate, v_w_up, v_w_down, v_g_final):
    S = x.shape[1]
    T = _row_tile(S, 512)
    TQ = _row_tile(S, 512)
    TW = _row_tile(S, 4096)
    ix, iy, ic = lax.axis_index("x"), lax.axis_index("y"), lax.axis_index("c")
    chip = (2 * ix + iy).astype(jnp.int32)
    chip_arr = chip.reshape(1)
    core_arr = ic.astype(jnp.int32).reshape(1)

    xs, tgt = x[0], loss_target[0]

    tr = lambda a: jnp.transpose(a[0])
    win_t = tr(w_in)
    win_p = jnp.concatenate([win_t[:O_KR + ROPE], win_t[O_KR:O_KR + ROPE], win_t[O_KR + ROPE:]], axis=0).astype(BF16)
    wuq = w_uq[0]
    wuq_p = jnp.concatenate([wuq[:, h, :NOPE] for h in range(HEADS)] + [wuq[:, h, NOPE:] for h in range(HEADS)],
                            axis=1).astype(BF16)
    w_uk_t = jnp.transpose(w_uk[0], (1, 0, 2)).astype(BF16)
    w_uv_t = jnp.transpose(w_uv[0], (1, 0, 2)).astype(BF16)
    w_pool_b = w_pool[0].astype(BF16)
    first = [win_p, wuq_p]
    later = [w_o[0].astype(BF16), tr(w_gate).astype(BF16), tr(w_up).astype(BF16), w_down[0].astype(BF16)]
    placed = _place_shards(chip_arr, first + later)
    a_send, a_recv, a_lands, token = _split_start("first_weights_start", first, placed[:2], 6, _plan_gather_start)
    half = ROPE // 2
    freqs = jnp.power(ROPE_THETA, -jnp.arange(half, dtype=F32) / half)
    cos, sin = _rope_tables(positions.reshape(S, 1), jnp.tile(freqs, 4).reshape(1, 128) + token[0, 0])
    a_send, a_recv, a_lands, token = _split_relay(
        "first_weights_relay", a_send, a_recv, first, a_lands, cos, 6, _plan_gather_landed, _plan_gather_relay)

    ada_cols = w_ada.shape[2]
    b_cols = lax.dynamic_slice(b_ada, (0, chip * ada_cols), (1, ada_cols))
    mod, c_all = _mod_exchange(c, w_ada[0], b_cols + token[0, 0])
    mod6 = mod.reshape(N_MOD, D_MODEL)
    a_lands = _split_wait("first_weights_wait", a_send, a_recv, [], a_lands, mod, _plan_gather_wait)
    w_in_f = a_lands[0]
    w_uq_f = a_lands[1].reshape(Q_LORA, Q_W)
    wg_lands, mod6, w_in_f = lax.optimization_barrier((placed[2:], mod6, w_in_f))
    wg_send, wg_recv, wg_lands, token = _split_start(
        "weights_start", later, wg_lands, 3 * len(later), _plan_gather_start)
    mod6 = mod6 + token[0, 0]

    proj, q, qc, kc, kct = _pre_attention(xs, mod6, g_mix, g_q, g_kv, w_in_f, w_uq_f, w_uk_t, cos, sin, T, TQ)
    o_lat, y_mla, lse_rows = _attention_fwd(qc, kc, kct, w_uv_t, TQ)
    wg_send, wg_recv, wg_lands, token = _split_relay(
        "weights_relay", wg_send, wg_recv, later, wg_lands, y_mla, 3 * len(later), _plan_gather_landed,
        _plan_gather_relay)
    pooled = _pool_forward(proj)
    wg_lands = _split_wait("weights_wait", wg_send, wg_recv, [], wg_lands, pooled, _plan_gather_wait)
    w_o_f = wg_lands[0].reshape(1024, D_MODEL)
    w_gate_f, w_up_f, w_down_f = wg_lands[1], wg_lands[2], wg_lands[3]
    x1, mix, mix_in = _mix_out(y_mla, pooled, w_pool_b, pool_scale, w_o_f, xs, mod6, T)
    gate, up, act, h2, dff, dx2, st_f = _ffn_forward(
        x1, mod6, g_ffn, g_final.reshape(1, D_MODEL), tgt, w_gate_f, w_up_f, w_down_f, T)

    dgate, dup, dx1, st_b = _ffn_backward(dx2, x1, dff, gate, up, mod6, g_ffn, w_gate_f, w_up_f, w_down_f, T)
    steps = S // TW
    chunk_spec = pl.BlockSpec((None, TW, FF_CHUNK), lambda g, i: (g, i, 0))
    wide_spec = pl.BlockSpec((TW, D_MODEL), lambda g, i: (i, 0))
    g_down = _tn_matmul(act, dff, chunk_spec, wide_spec, N_CHIPS, FF_CHUNK, D_MODEL, steps, "grad_w_down")
    g_gate = _tn_matmul(dgate, h2, chunk_spec, wide_spec, N_CHIPS, FF_CHUNK, D_MODEL, steps, "grad_w_gate")
    g_up = _tn_matmul(dup, h2, chunk_spec, wide_spec, N_CHIPS, FF_CHUNK, D_MODEL, steps, "grad_w_up")

    half_shapes = lambda gs: [jax.ShapeDtypeStruct((N_CHIPS, g.shape[1] // 2, g.shape[2]), F32) for g in gs]
    ffn_grads = [g_gate, g_up, g_down]
    f_send, f_recv, f_lands, token = _split_start(
        "ffn_swap_start", ffn_grads, half_shapes(ffn_grads), len(ffn_grads), _plan_swap_start)
    dmix, dpooled, do_lat, delta_rows, g_pool, g_uv_t, st_m = _mix_backward(
        dx1, mix, mod6 + token[0, 0], w_o_f, pooled, w_pool_b, pool_scale, w_uv_t, o_lat, T, TQ)
    g_o = [_tn_matmul(mix_in, dmix, wide_spec, wide_spec, 1, 1024, D_MODEL, steps, "grad_w_o").reshape(N_CHIPS, -1, D_MODEL)]
    o_send, o_recv, o_lands, token = _split_start("w_o_swap_start", g_o, half_shapes(g_o), 1, _plan_swap_start)
    du = _pool_backward(dpooled, token)
    f_got = _split_wait("ffn_swap_wait", f_send, f_recv, ffn_grads, f_lands, du, _plan_swap_wait)
    f_got += _split_wait("w_o_swap_wait", o_send, o_recv, g_o, o_lands, du, _plan_swap_wait)
    far_grads = ffn_grads + g_o
    f_sums = _add_my_halves(core_arr, far_grads, f_got, "add_half_far")
    f_send, f_recv, f_lands, token = _split_start(
        "far_exchange_start", f_sums, [jax.ShapeDtypeStruct((3,) + s.shape[1:], F32) for s in f_sums],
        3 * len(f_sums), _plan_exchange_start)
    delta_rows = delta_rows + token[0, 0]
    dkc, dqt = _attention_bwd(qc, kc, kct, do_lat, lse_rows, delta_rows, TQ)
    grad_x, dproj, h1, g_uk_t, uq, st_p = _pre_attention_backward(
        xs, dx1, proj, q, dqt, dkc, du, cos, sin, mod6, g_mix, g_q, g_kv, w_in_f, w_uq_f, w_uk_t, T, TQ)
    rows_in = D_MODEL // N_CHIPS
    g_in_p = _tn_matmul(dproj, h1, pl.BlockSpec((TW, PROJ_W), lambda g, i: (i, 0)),
                        pl.BlockSpec((TW, rows_in), lambda g, i: (i, g)), N_CHIPS, PROJ_W, rows_in, steps, "grad_w_in")

    g_in = jnp.concatenate([g_in_p[:, :O_KR + ROPE], g_in_p[:, O_U:]], axis=1)
    g_uq = jnp.concatenate([jnp.concatenate([uq[:, h * NOPE:(h + 1) * NOPE], uq[:, O_QA + h * ROPE:O_QA + (h + 1) * ROPE]],
                                            axis=1) for h in range(HEADS)], axis=1).reshape(N_CHIPS, -1, HEADS * HEAD_QK)
    small = _pack_rows([g_uk_t, g_uv_t, g_pool, st_p[2], st_p[3, :Q_LORA], st_p[4, :KV_LORA], st_m[1, :POOL_W],
                        st_b[2], st_f[0]])
    small = jnp.concatenate([small, jnp.zeros((SMALL_ROWS - small.shape[0], 128), F32)]).reshape(N_CHIPS, -1, 128)
    grads = [g_in, g_uq, small]
    dmod = jnp.concatenate([jnp.stack([st_p[0], st_p[1], st_m[0], st_b[0], st_b[1], st_f[1]]).reshape(48, 128),
                            jnp.zeros((8, 128), F32).at[0, 0].set(st_f[2, 0])])

    got, dmod_all = _grad_swap_halves(grads, dmod)
    chip_sums = _add_my_halves(core_arr, grads, got, "add_half_near")
    n_send, n_recv, n_lands, token = _split_start(
        "near_exchange_start", chip_sums, [jax.ShapeDtypeStruct((3,) + s.shape[1:], F32) for s in chip_sums],
        3 * len(chip_sums), _plan_exchange_start)

    f_others = _split_wait("far_exchange_wait", f_send, f_recv, f_sums, f_lands, token, _plan_exchange_wait)
    chip_core = jnp.concatenate([chip_arr, core_arr])
    f_pairs = (_add_chips_into_pairs(chip_core, f_sums[:2], f_others[:2], "add_chips_gate_up")
               + _add_chips_into_pairs(chip_core, f_sums[2:], f_others[2:], "add_chips_down_o"))
    f_send, f_recv, f_pairs, token = _split_start("far_finish_start", [], f_pairs, len(f_pairs), _plan_finish_start)
    gw_ada, gb_ada = _ada_grads(c_all, dmod_all.reshape(8, -1) + token[0, 0], chip_arr)
    loss = gb_ada[0, N_MOD * D_MODEL]
    gb_ada = gb_ada[:, :N_MOD * D_MODEL]
    f_fulls = _split_wait("far_finish_wait", f_send, f_recv, [], f_pairs, gw_ada, _plan_finish_wait)
    gw_gate, gw_up, gw_down, gw_o = [f.reshape(-1, f.shape[2]) for f in f_fulls]

    untr = lambda a: jnp.transpose(a)[None]
    grad_out, delta_out, newm_out, newv_out = {}, {}, {}, {}

    def adam_sharded(n, w, g2, m, v, transposed, landed=True):
        view = (lambda a: jnp.transpose(a[0])) if transposed else (lambda a: a[0])
        back = untr if transposed else (lambda a: a[None])
        g_, d_, m_, v_ = _adamw(view(w), g2.reshape(view(w).shape), view(m), view(v), "adamw_" + n, landed)
        grad_out[n], delta_out[n], newm_out[n], newv_out[n] = back(g_), back(d_), back(m_), back(v_)
        return d_

    done = [adam_sharded("w_gate", w_gate, gw_gate, m_w_gate, v_w_gate, True),
            adam_sharded("w_up", w_up, gw_up, m_w_up, v_w_up, True),
            adam_sharded("w_down", w_down, gw_down, m_w_down, v_w_down, False),
            adam_sharded("w_o", w_o, gw_o, m_w_o, v_w_o, False)]
    after_all = jnp.stack([d[0, 0] for d in done])

    others = _split_wait("near_exchange_wait", n_send, n_recv, chip_sums, n_lands, after_all, _plan_exchange_wait)
    n_pairs = _add_chips_into_pairs(chip_core, chip_sums[:2], others[:2], "add_chips_in_uq")
    small_grid = _add_chips_into_grid(chip_core, chip_sums[2], others[2], "add_chips_small")
    n_send, n_recv, n_lands, token = _split_start(
        "near_finish_start", [], n_pairs + [small_grid], 2 + len(RELATIONS), _plan_near_finish_start)
    gw_ada, _ = lax.optimization_barrier((gw_ada, token))
    d_ada = adam_sharded("w_ada", w_ada, gw_ada, m_w_ada, v_w_ada, False, landed=False)
    n_lands = _split_wait("near_finish_wait", n_send, n_recv, [], n_lands, d_ada, _plan_near_finish_wait)
    gw_in, gw_uq = [f.reshape(-1, f.shape[2]) for f in n_lands[:2]]
    small_all = n_lands[2].reshape(SMALL_ROWS * 128)
    adam_sharded("w_in", w_in, gw_in, m_w_in, v_w_in, True)
    adam_sharded("w_uq", w_uq, gw_uq, m_w_uq, v_w_uq, False)

    n_sq = KV_LORA * HEADS * 128
    sizes = [n_sq, n_sq, n_sq, D_MODEL, Q_LORA, KV_LORA, POOL_W, D_MODEL, D_MODEL]
    offs = [0]
    for s_ in sizes:
        offs.append(offs[-1] + s_)
    piece = lambda k: small_all[offs[k]:offs[k + 1]]
    grads_small = {
        "w_uk": jnp.transpose(piece(0).reshape(HEADS, KV_LORA, NOPE), (1, 0, 2)),
        "w_uv": jnp.transpose(piece(1).reshape(HEADS, KV_LORA, 128), (1, 0, 2)),
        "w_pool": piece(2).reshape(4, POOL_GROUP, POOL_GROUP),
        "g_mix": piece(3), "g_q": piece(4), "g_kv": piece(5), "pool_scale": piece(6), "g_ffn": piece(7),
        "g_final": piece(8), "b_ada": gb_ada.reshape(-1),
    }
    weights_small = {"w_uk": w_uk, "w_uv": w_uv, "w_pool": w_pool, "g_mix": g_mix, "g_q": g_q, "g_kv": g_kv,
                     "pool_scale": pool_scale, "g_ffn": g_ffn, "g_final": g_final, "b_ada": b_ada}
    m_small = {"w_uk": m_w_uk, "w_uv": m_w_uv, "w_pool": m_w_pool, "g_mix": m_g_mix, "g_q": m_g_q, "g_kv": m_g_kv,
               "pool_scale": m_pool_scale, "g_ffn": m_g_ffn, "g_final": m_g_final, "b_ada": m_b_ada}
    v_small = {"w_uk": v_w_uk, "w_uv": v_w_uv, "w_pool": v_w_pool, "g_mix": v_g_mix, "g_q": v_g_q, "g_kv": v_g_kv,
               "pool_scale": v_pool_scale, "g_ffn": v_g_ffn, "g_final": v_g_final, "b_ada": v_b_ada}
    pack = lambda d: _pack_rows([d[n] for n in SMALL_NAMES])
    _, d_s, m_s, v_s = _adamw(pack(weights_small), pack(grads_small), pack(m_small), pack(v_small), "adamw_small",
                              g_is_landing_zone=False)

    def unpack(flat2d):
        flat = flat2d.reshape(-1)
        out, o = {}, 0
        for n in SMALL_NAMES:
            size = weights_small[n].size
            out[n] = flat[o:o + size].reshape(weights_small[n].shape)
            o += size
        return out

    delta_s, newm_s, newv_s = unpack(d_s), unpack(m_s), unpack(v_s)

    for n in SMALL_NAMES:
        grad_out[n] = grads_small[n].reshape(weights_small[n].shape)
        delta_out[n], newm_out[n], newv_out[n] = delta_s[n], newm_s[n], newv_s[n]

    order = ("w_ada", "b_ada", "g_mix", "w_in", "g_q", "g_kv", "w_uq", "w_uk", "w_uv", "w_pool", "pool_scale", "w_o",
             "g_ffn", "w_gate", "w_up", "w_down", "g_final")
    return (loss, grad_x.reshape(x.shape), *[grad_out[n] for n in order], *[delta_out[n] for n in order],
            *[newm_out[n] for n in order], *[newv_out[n] for n in order])
```

```python
import functools

import jax
import jax.numpy as jnp
from jax import lax
from jax.experimental import pallas as pl
from jax.experimental.pallas import tpu as pltpu

F32 = jnp.float32
BF16 = jnp.bfloat16

D_MODEL = 1024
HEADS = 4
NOPE = 128
ROPE = 64
HEAD_QK = NOPE + ROPE
Q_LORA = 256
KV_LORA = 128
POOL_W = 512
POOL_WINDOWS = (2, 4, 8, 16)
POOL_GROUP = 128
POOL_PAD = 16
D_FF = 2816
N_CHIPS = 4
FF_CHUNK = D_FF // N_CHIPS
N_MOD = 6
EPS = 1e-6
SM_SCALE = HEAD_QK ** -0.5
ROPE_THETA = 10000.0
QK_PAD = 256
CHUNK = 64
CHUNK_SHIFT = 6

ADAM_LR = 0.001
ADAM_B1 = 0.9
ADAM_B2 = 0.999
ADAM_EPS = 1e-08
ADAM_WD = 0.01
ADAM_STEP = 10

VMEM_LIMIT = 48 * 1024 * 1024
MESH = pl.DeviceIdType.MESH
ANY = pl.BlockSpec(memory_space=pl.ANY)
VMEM_SPEC = pl.BlockSpec(memory_space=pltpu.VMEM)

PROJ_W = 1024
O_CKV = 256
O_KR = 384
O_U = 512
Q_W = 768
O_QA = 512
O_QB = 640


def _params(sem=None, vmem=VMEM_LIMIT):
    kw = dict(vmem_limit_bytes=vmem)
    if sem is not None:
        kw["dimension_semantics"] = sem
    return pltpu.CompilerParams(**kw)


def _dot(a, b):
    return jnp.dot(a.astype(BF16), b.astype(BF16), preferred_element_type=F32)


def _dot_nt(a, b):
    return lax.dot_general(a.astype(BF16), b.astype(BF16), (((1,), (1,)), ((), ())), preferred_element_type=F32)


def _dot_tn(a, b):
    return lax.dot_general(a.astype(BF16), b.astype(BF16), (((0,), (0,)), ((), ())), preferred_element_type=F32)


def _row_tile(rows, target):
    best = rows
    for t in range(8, min(rows, target) + 1, 8):
        if rows % t == 0:
            best = t
    return best if rows % best == 0 and best <= target else rows


def _rms(x):
    r = lax.rsqrt(jnp.mean(x * x, axis=-1, keepdims=True) + EPS)
    return x * r, r


def _rms_bwd(dxh, xh, r):
    return r * (dxh - xh * jnp.mean(dxh * xh, axis=-1, keepdims=True))


def _lane_first_half(shape):
    lane = lax.broadcasted_iota(jnp.int32, shape, 1)
    return (lane & (ROPE - 1)) < (ROPE // 2)


def _rope(a, cos, sin):
    first = _lane_first_half(a.shape)
    up = pltpu.roll(a, 96, 1)
    dn = pltpu.roll(a, 32, 1)
    return a * cos + jnp.where(first, -up, dn) * sin


def _rope_bwd(d, cos, sin):
    first = _lane_first_half(d.shape)
    up = pltpu.roll(d, 96, 1)
    dn = pltpu.roll(d, 32, 1)
    return d * cos + jnp.where(first, up, -dn) * sin


RELATIONS = tuple((dx, dy, dc) for dx in (0, 1) for dy in (0, 1) for dc in (0, 1) if (dx, dy, dc) != (0, 0, 0))
CHIP_RELATIONS = ((1, 0), (0, 1), (1, 1))


def _flip(v, d):
    return 1 - v if d else v


def _place():
    return lax.axis_index("x"), lax.axis_index("y"), lax.axis_index("c")


def _remote(src, dst, send_sem, recv_sem, target):
    return pltpu.make_async_remote_copy(src_ref=src, dst_ref=dst, send_sem=send_sem, recv_sem=recv_sem,
                                        device_id=target, device_id_type=MESH)


def _mod_exchange(c_row, w_ada, b_ada):
    cols = w_ada.shape[1]

    def body(c_ref, w_ref, b_ref, mod_ref, call_ref, part_ref, send1, recv1, loc1, send2, recv2, loc2):
        x, y, c = _place()
        me = 4 * x + 2 * y + c
        own = pltpu.make_async_copy(c_ref, call_ref.at[pl.ds(me, 1)], loc1)
        own.start()
        sends = []
        for k, (dx, dy, dc) in enumerate(RELATIONS):
            cp = _remote(c_ref, call_ref.at[pl.ds(me, 1)], send1.at[k], recv1.at[k],
                         (_flip(x, dx), _flip(y, dy), _flip(c, dc)))
            cp.start()
            sends.append(cp)
        for k, (dx, dy, dc) in enumerate(RELATIONS):
            src = 4 * _flip(x, dx) + 2 * _flip(y, dy) + _flip(c, dc)
            _remote(c_ref, call_ref.at[pl.ds(src, 1)], send1.at[k], recv1.at[k], (x, y, c)).wait_recv()
        own.wait()
        for cp in sends:
            cp.wait_send()
        call = call_ref[...]
        act = call * jax.nn.sigmoid(call)
        part_ref[...] = _dot(act, w_ref[...]) + b_ref[...]
        chip = 2 * x + y
        mine = pltpu.make_async_copy(part_ref.at[pl.ds(me, 1)], mod_ref.at[pl.ds(chip, 1)], loc2)
        mine.start()
        sends = []
        for k, (dx, dy) in enumerate(CHIP_RELATIONS):
            tx, ty = _flip(x, dx), _flip(y, dy)
            tb = 4 * tx + 2 * ty + c
            cp = _remote(part_ref.at[pl.ds(tb, 1)], mod_ref.at[pl.ds(chip, 1)], send2.at[k], recv2.at[k], (tx, ty, c))
            cp.start()
            sends.append(cp)
        for k, (dx, dy) in enumerate(CHIP_RELATIONS):
            src_chip = 2 * _flip(x, dx) + _flip(y, dy)
            _remote(part_ref.at[pl.ds(me, 1)], mod_ref.at[pl.ds(src_chip, 1)], send2.at[k], recv2.at[k],
                    (x, y, c)).wait_recv()
        mine.wait()
        for cp in sends:
            cp.wait_send()

    return pl.pallas_call(
        body, name="mod_exchange",
        out_shape=[jax.ShapeDtypeStruct((N_CHIPS, cols), F32), jax.ShapeDtypeStruct((8, D_MODEL), F32)],
        in_specs=[VMEM_SPEC, VMEM_SPEC, VMEM_SPEC], out_specs=[VMEM_SPEC, VMEM_SPEC],
        scratch_shapes=[pltpu.VMEM((8, cols), F32),
                        pltpu.SemaphoreType.DMA((7,)), pltpu.SemaphoreType.DMA((7,)), pltpu.SemaphoreType.DMA,
                        pltpu.SemaphoreType.DMA((3,)), pltpu.SemaphoreType.DMA((3,)), pltpu.SemaphoreType.DMA],
        compiler_params=_params(),
    )(c_row, w_ada, b_ada)


HBM_SPEC = pl.BlockSpec(memory_space=pltpu.HBM)
SEM_SPEC = pl.BlockSpec(memory_space=pltpu.SEMAPHORE)
DATAFLOW = pltpu.SideEffectType.DATAFLOW_SIDE_EFFECTING


def _in_hbm(a):
    return pltpu.with_memory_space_constraint(a, pltpu.HBM)


def _hbm(*arrays):
    return tuple(_in_hbm(a) for a in arrays)


def _hbm_like(arrays):
    return [pltpu.HBM(a.shape, a.dtype) for a in arrays]


def _split_start(name, srcs, lands, n_remote, plan):
    lands = [lax.empty(a.shape, a.dtype) if isinstance(a, jax.ShapeDtypeStruct) else a for a in lands]
    n, m = len(srcs), len(lands)

    def body(*refs):
        src_refs, land_refs = refs[:n], refs[n:n + m]
        send_sems, recv_sems, token = refs[n + m], refs[n + m + 1], refs[n + 2 * m + 2]
        remote = plan(_place(), src_refs, land_refs)
        assert len(remote) == n_remote
        for i, (s, d, target) in enumerate(remote):
            _remote(s, d, send_sems.at[i], recv_sems.at[i], target).start()
        token[...] = jnp.zeros_like(token)

    res = pl.pallas_call(
        body, name=name,
        out_shape=(pltpu.SemaphoreType.DMA((n_remote,)), pltpu.SemaphoreType.DMA((n_remote,)),
                   *_hbm_like(lands), jax.ShapeDtypeStruct((8, 128), F32)),
        in_specs=[HBM_SPEC] * (n + m),
        out_specs=(SEM_SPEC, SEM_SPEC, *([HBM_SPEC] * m), VMEM_SPEC),
        input_output_aliases={n + i: 2 + i for i in range(m)},
        compiler_params=pltpu.CompilerParams(has_side_effects=DATAFLOW),
    )(*[_in_hbm(a) for a in srcs], *[_in_hbm(a) for a in lands])
    return res[0], res[1], list(res[2:2 + m]), res[2 + m]


def _split_wait(name, send_sems, recv_sems, srcs, lands, after, plan):
    n, m = len(srcs), len(lands)

    def body(*refs):
        src_refs, land_refs = refs[:n], refs[n:n + m]
        send_sems, recv_sems = refs[n + m], refs[n + m + 1]
        place = _place()
        for i, (s, d) in enumerate(plan(place, src_refs, land_refs)):
            cp = _remote(s, d, send_sems.at[i], recv_sems.at[i], place)
            cp.wait_send()
            cp.wait_recv()

    res = pl.pallas_call(
        body, name=name,
        out_shape=tuple(_hbm_like(lands)),
        in_specs=[HBM_SPEC] * (n + m) + [SEM_SPEC, SEM_SPEC, ANY],
        out_specs=tuple([HBM_SPEC] * m),
        input_output_aliases={n + i: i for i in range(m)},
        compiler_params=pltpu.CompilerParams(has_side_effects=DATAFLOW),
    )(*srcs, *lands, send_sems, recv_sems, after)
    return list(res)


def _split_relay(name, send_sems, recv_sems, srcs, lands, after, n_remote, plan_wait, plan_send):
    n, m = len(srcs), len(lands)

    def body(*refs):
        src_refs, land_refs = refs[:n], refs[n:n + m]
        old_send, old_recv = refs[n + m], refs[n + m + 1]
        new_send, new_recv = refs[n + m + 3], refs[n + m + 4]
        token = refs[n + m + 5 + m]
        place = _place()
        for i, (s, d) in enumerate(plan_wait(place, src_refs, land_refs)):
            cp = _remote(s, d, old_send.at[i], old_recv.at[i], place)
            cp.wait_send()
            cp.wait_recv()
        for i, (s, d, target) in enumerate(plan_send(place, land_refs)):
            _remote(s, d, new_send.at[i], new_recv.at[i], target).start()
        token[...] = jnp.zeros_like(token)

    res = pl.pallas_call(
        body, name=name,
        out_shape=(pltpu.SemaphoreType.DMA((n_remote,)), pltpu.SemaphoreType.DMA((n_remote,)),
                   *_hbm_like(lands), jax.ShapeDtypeStruct((8, 128), F32)),
        in_specs=[HBM_SPEC] * (n + m) + [SEM_SPEC, SEM_SPEC, ANY],
        out_specs=(SEM_SPEC, SEM_SPEC, *([HBM_SPEC] * m), VMEM_SPEC),
        input_output_aliases={n + i: 2 + i for i in range(m)},
        compiler_params=pltpu.CompilerParams(has_side_effects=DATAFLOW),
    )(*srcs, *lands, send_sems, recv_sems, after)
    return res[0], res[1], list(res[2:2 + m]), res[2 + m]


def _half(ref, core, axis=0):
    hr = ref.shape[axis] // 2
    return pl.ds(core * hr, hr)


def _plan_gather_start(place, src, land):
    x, y, c = place
    chip = 2 * x + y
    return [(s.at[_half(s, c)], l.at[chip, _half(s, c)], (_flip(x, dx), _flip(y, dy), c))
            for s, l in zip(src, land) for dx, dy in CHIP_RELATIONS]


def _plan_gather_landed(place, src, land):
    x, y, c = place
    return [(s.at[_half(s, c)], l.at[2 * _flip(x, dx) + _flip(y, dy), _half(s, c)])
            for s, l in zip(src, land) for dx, dy in CHIP_RELATIONS]


def _plan_gather_relay(place, land):
    x, y, c = place
    out = []
    for l in land:
        for dx, dy in CHIP_RELATIONS:
            got = l.at[2 * _flip(x, dx) + _flip(y, dy), _half(l, c, 1)]
            out.append((got, got, (x, y, 1 - c)))
    return out


def _plan_gather_wait(place, src, land):
    x, y, c = place
    out = []
    for l in land:
        for dx, dy in CHIP_RELATIONS:
            got = l.at[2 * _flip(x, dx) + _flip(y, dy), _half(l, 1 - c, 1)]
            out.append((got, got))
    return out


def _plan_swap_start(place, src, land):
    x, y, c = place
    return [(s.at[:, _half(s, 1 - c, 1), :], l, (x, y, 1 - c)) for s, l in zip(src, land)]


def _plan_swap_wait(place, src, land):
    return [(s.at[:, _half(s, 0, 1), :], l) for s, l in zip(src, land)]


def _plan_exchange_start(place, src, land):
    x, y, c = place
    remote = []
    for s, l in zip(src, land):
        for k, (dx, dy) in enumerate(CHIP_RELATIONS):
            tx, ty = _flip(x, dx), _flip(y, dy)
            remote.append((s.at[2 * tx + ty], l.at[k], (tx, ty, c)))
    return remote


def _plan_exchange_wait(place, src, land):
    return [(s.at[0], l.at[k]) for s, l in zip(src, land) for k in range(3)]


def _plan_finish_start(place, src, land):
    x, y, c = place
    return [(l.at[c], l.at[c], (x, y, 1 - c)) for l in land]


def _plan_finish_wait(place, src, land):
    x, y, c = place
    return [(l.at[c], l.at[1 - c]) for l in land]


def _plan_near_finish_start(place, src, land):
    x, y, c = place
    mine = land[-1].at[2 * x + y, c]
    return (_plan_finish_start(place, src, land[:-1])
            + [(mine, mine, (_flip(x, dx), _flip(y, dy), _flip(c, dc))) for dx, dy, dc in RELATIONS])


def _plan_near_finish_wait(place, src, land):
    x, y, c = place
    mine = land[-1].at[2 * x + y, c]
    return (_plan_finish_wait(place, src, land[:-1])
            + [(mine, land[-1].at[2 * _flip(x, dx) + _flip(y, dy), _flip(c, dc)]) for dx, dy, dc in RELATIONS])


def _grad_swap_halves(grads, dmod):
    n = len(grads)

    def body(*refs):
        ins, dmod_ref = refs[:n], refs[n]
        outs, dall_ref = refs[n + 1:2 * n + 1], refs[2 * n + 1]
        send_sems, recv_sems, dsend, drecv, dloc = refs[2 * n + 2:]
        x, y, c = _place()
        me = 4 * x + 2 * y + c
        sends = []
        for w in range(n):
            hr = ins[w].shape[1] // 2
            cp = _remote(ins[w].at[:, pl.ds((1 - c) * hr, hr), :], outs[w], send_sems.at[w], recv_sems.at[w],
                         (x, y, 1 - c))
            cp.start()
            sends.append(cp)
        own = pltpu.make_async_copy(dmod_ref, dall_ref.at[me], dloc)
        own.start()
        for k, (dx, dy, dc) in enumerate(RELATIONS):
            cp = _remote(dmod_ref, dall_ref.at[me], dsend.at[k], drecv.at[k],
                         (_flip(x, dx), _flip(y, dy), _flip(c, dc)))
            cp.start()
            sends.append(cp)
        for k, (dx, dy, dc) in enumerate(RELATIONS):
            src = 4 * _flip(x, dx) + 2 * _flip(y, dy) + _flip(c, dc)
            _remote(dmod_ref, dall_ref.at[src], dsend.at[k], drecv.at[k], (x, y, c)).wait_recv()
        for w in range(n):
            _remote(outs[w], outs[w], send_sems.at[w], recv_sems.at[w], (x, y, c)).wait_recv()
        own.wait()
        for cp in sends:
            cp.wait_send()

    out_shape = [pltpu.HBM((N_CHIPS, g.shape[1] // 2, g.shape[2]), F32) for g in grads]
    out_shape.append(pltpu.HBM((8,) + dmod.shape, F32))
    res = pl.pallas_call(
        body, name="grad_swap_halves",
        out_shape=out_shape, in_specs=[ANY] * n + [VMEM_SPEC], out_specs=[ANY] * (n + 1),
        scratch_shapes=[pltpu.SemaphoreType.DMA((n,)), pltpu.SemaphoreType.DMA((n,)),
                        pltpu.SemaphoreType.DMA((7,)), pltpu.SemaphoreType.DMA((7,)), pltpu.SemaphoreType.DMA],
        compiler_params=_params(),
    )(*grads, dmod)
    return res[:n], res[n]


def _add_my_halves(core, fulls, gots, name):
    n = len(fulls)

    def body(core_ref, *refs):
        for w in range(n):
            refs[2 * n + w][...] = refs[w][...] + refs[n + w][...]

    mine = lambda g: pl.BlockSpec((None,) + g.shape[1:], lambda s, core_ref: (s, core_ref[0], 0))
    slab = lambda g: pl.BlockSpec((None,) + g.shape[1:], lambda s, core_ref: (s, 0, 0))
    return list(pl.pallas_call(
        body, name=name,
        out_shape=[pltpu.HBM(g.shape, F32) for g in gots],
        grid_spec=pltpu.PrefetchScalarGridSpec(
            num_scalar_prefetch=1, grid=(N_CHIPS,),
            in_specs=[mine(g) for g in gots] + [slab(g) for g in gots],
            out_specs=[slab(g) for g in gots]),
        compiler_params=_params(("arbitrary",)),
    )(core, *_hbm(*fulls, *gots)))


def _add_chips_into_pairs(chip_core, mines, gots, name):
    n = len(mines)

    def body(cc_ref, *refs):
        for w in range(n):
            b_ref = refs[n + w]
            refs[2 * n + w][...] = ((refs[w][...] + b_ref[0]) + b_ref[1]) + b_ref[2]

    return list(pl.pallas_call(
        body, name=name,
        out_shape=[pltpu.HBM((2,) + m.shape[1:], F32) for m in mines],
        grid_spec=pltpu.PrefetchScalarGridSpec(
            num_scalar_prefetch=1, grid=(1,),
            in_specs=[pl.BlockSpec((None,) + m.shape[1:], lambda s, cc_ref: (cc_ref[0], 0, 0)) for m in mines]
            + [pl.BlockSpec(g.shape, lambda s, cc_ref: (0, 0, 0)) for g in gots],
            out_specs=[pl.BlockSpec((None,) + m.shape[1:], lambda s, cc_ref: (cc_ref[1], 0, 0)) for m in mines]),
        compiler_params=_params(("arbitrary",)),
    )(chip_core, *_hbm(*mines, *gots)))


def _add_chips_into_grid(chip_core, mine, got, name):
    _, hr, cols = mine.shape

    def body(cc_ref, a_ref, b_ref, o_ref):
        o_ref[...] = ((a_ref[...] + b_ref[0]) + b_ref[1]) + b_ref[2]

    return pl.pallas_call(
        body, name=name,
        out_shape=pltpu.HBM((N_CHIPS, 2, hr, cols), F32),
        grid_spec=pltpu.PrefetchScalarGridSpec(
            num_scalar_prefetch=1, grid=(1,),
            in_specs=[pl.BlockSpec((None, hr, cols), lambda s, cc_ref: (cc_ref[0], 0, 0)),
                      pl.BlockSpec((3, hr, cols), lambda s, cc_ref: (0, 0, 0))],
            out_specs=pl.BlockSpec((None, None, hr, cols), lambda s, cc_ref: (cc_ref[0], cc_ref[1], 0, 0))),
        compiler_params=_params(("arbitrary",)),
    )(chip_core, *_hbm(mine, got))


def _place_shards(chip, shards):
    n = len(shards)

    def body(chip_ref, *refs):
        for w in range(n):
            refs[n + w][...] = refs[w][...]

    return pl.pallas_call(
        body, name="place_shards",
        out_shape=[pltpu.HBM((N_CHIPS,) + s.shape, s.dtype) for s in shards],
        grid_spec=pltpu.PrefetchScalarGridSpec(
            num_scalar_prefetch=1, grid=(1,),
            in_specs=[pl.BlockSpec(s.shape, lambda i, chip_ref: (0, 0)) for s in shards],
            out_specs=[pl.BlockSpec((None,) + s.shape, lambda i, chip_ref: (chip_ref[0], 0, 0)) for s in shards]),
        compiler_params=_params(("arbitrary",)),
    )(chip, *shards)


def _rope_tables(pos_col, freqs):
    S = pos_col.shape[0]
    T = _row_tile(S, 1024)

    def body(p_ref, f_ref, cos_ref, sin_ref):
        ang = p_ref[...].astype(F32) * f_ref[...]
        cos_ref[...] = jnp.cos(ang)
        sin_ref[...] = jnp.sin(ang)

    return pl.pallas_call(
        body, name="rope_tables", grid=(S // T,),
        out_shape=[pltpu.HBM((S, 128), F32)] * 2,
        in_specs=[pl.BlockSpec((T, 1), lambda i: (i, 0)), pl.BlockSpec((1, 128), lambda i: (0, 0))],
        out_specs=[pl.BlockSpec((T, 128), lambda i: (i, 0))] * 2,
        compiler_params=_params(("parallel",)),
    )(*_hbm(pos_col, freqs))


def _full(shape):
    zeros = (0,) * len(shape)
    return pl.BlockSpec(shape, lambda *_: zeros)


def _pre_attention(x, mod6, g_mix, g_q, g_kv, w_in, w_uq, w_uk_t, cos, sin, T, TQ):
    S = x.shape[0]

    def body(x_ref, mod_ref, gm_ref, gq_ref, gkv_ref, win_ref, wuq_ref, wuk_ref, cos_ref, sin_ref,
             proj_ref, q_ref, qc_ref, kc_ref, kct_ref):
        xh, _ = _rms(x_ref[...])
        h1 = ((xh * gm_ref[...]) * (1.0 + mod_ref[1:2, :]) + mod_ref[0:1, :]).astype(BF16)
        rows_in = D_MODEL // N_CHIPS
        proj = _dot_nt(h1[:, 0:rows_in], win_ref[0])
        for j in range(1, N_CHIPS):
            proj = proj + _dot_nt(h1[:, j * rows_in:(j + 1) * rows_in], win_ref[j])
        proj_ref[...] = proj
        cqh, _ = _rms(proj[:, :Q_LORA])
        c_q = cqh * gq_ref[...]
        ckvh, _ = _rms(proj[:, O_CKV:O_KR])
        c_kv = ckvh * gkv_ref[...]
        q = _dot(c_q, wuq_ref[...])
        q_ref[...] = q.astype(BF16)
        cos_t, sin_t = cos_ref[...], sin_ref[...]
        ropes = (_rope(q[:, O_QA:O_QB], cos_t, sin_t), _rope(q[:, O_QB:Q_W], cos_t, sin_t))
        low = lax.broadcasted_iota(jnp.int32, (T, 128), 1) < ROPE
        for h in range(HEADS):
            q_lat = _dot_nt(q[:, h * NOPE:(h + 1) * NOPE], wuk_ref[h])
            keep = low if h % 2 == 0 else jnp.logical_not(low)
            qc_ref[h, :, 0:KV_LORA] = q_lat.astype(BF16)
            qc_ref[h, :, KV_LORA:QK_PAD] = jnp.where(keep, ropes[h // 2], 0.0).astype(BF16)
        k_rope = _rope(proj[:, O_KR:O_U], cos_t, sin_t)
        kc_ref[:, 0:KV_LORA] = c_kv.astype(BF16)
        kc_ref[:, KV_LORA:QK_PAD] = k_rope.astype(BF16)
        lat_t, rope_t = jnp.transpose(c_kv), jnp.transpose(k_rope)
        for s in range(T // TQ):
            kct_ref[s, 0:KV_LORA, :] = lat_t[:, s * TQ:(s + 1) * TQ].astype(BF16)
            kct_ref[s, KV_LORA:QK_PAD, :] = rope_t[:, s * TQ:(s + 1) * TQ].astype(BF16)

    row = lambda w: pl.BlockSpec((T, w), lambda i: (i, 0))
    return pl.pallas_call(
        body, name="pre_attention", grid=(S // T,),
        out_shape=[pltpu.HBM((S, PROJ_W), F32), pltpu.HBM((S, Q_W), BF16), pltpu.HBM((HEADS, S, QK_PAD), BF16),
                   pltpu.HBM((S, QK_PAD), BF16), pltpu.HBM((S // TQ, QK_PAD, TQ), BF16)],
        in_specs=[row(D_MODEL), _full((N_MOD, D_MODEL)), _full((1, D_MODEL)), _full((1, Q_LORA)), _full((1, KV_LORA)),
                  _full((N_CHIPS, PROJ_W, D_MODEL // N_CHIPS)), _full((Q_LORA, Q_W)), _full((HEADS, KV_LORA, NOPE)),
                  row(128), row(128)],
        out_specs=[row(PROJ_W), row(Q_W), pl.BlockSpec((HEADS, T, QK_PAD), lambda i: (0, i, 0)), row(QK_PAD),
                   pl.BlockSpec((T // TQ, QK_PAD, TQ), lambda i: (i, 0, 0))],
        compiler_params=_params(("parallel",)),
    )(*_hbm(x, mod6, g_mix, g_q, g_kv, w_in, w_uq, w_uk_t, cos, sin))


def _diag_mask(TQ, width):
    key = lax.broadcasted_iota(jnp.int32, (TQ, width), 0) >> CHUNK_SHIFT
    qry = (lax.broadcasted_iota(jnp.int32, (TQ, width), 1) & (TQ - 1)) >> CHUNK_SHIFT
    return key <= qry


def _col_to_row(col):
    return jnp.transpose(jnp.broadcast_to(col, (col.shape[0], 128)))[0:1, :]


def _attention_fwd(qc, kc, kct, w_uv_t, TQ):
    S = kc.shape[0]
    R = HEADS * TQ
    nq = S // TQ

    def body(q_ref, k_ref, kt_ref, wuv_ref, o_ref, y_ref, lser_ref, m_s, l_s, acc_s, st_s):
        i = pl.program_id(0)
        q = q_ref[...].reshape(R, QK_PAD)
        m_s[...] = jnp.full((1, R), -jnp.inf, F32)
        l_s[...] = jnp.zeros((1, R), F32)
        acc_s[...] = jnp.zeros((KV_LORA, R), F32)

        def scores(j):
            return _dot_nt(k_ref[pl.ds(pl.multiple_of(j * TQ, TQ), TQ), :], q) * SM_SCALE

        def update(j, st):
            m_old = m_s[...]
            m_new = jnp.maximum(m_old, jnp.max(st, axis=0, keepdims=True))
            pt = jnp.exp(st - m_new)
            alpha = jnp.exp(m_old - m_new)
            l_s[...] = alpha * l_s[...] + jnp.sum(pt, axis=0, keepdims=True)
            acc_s[...] = alpha * acc_s[...] + _dot(kt_ref[j, 0:KV_LORA, :], pt)
            m_s[...] = m_new

        st_s[...] = scores(0)

        def loop(j, carry):
            st = st_s[...]
            st_s[...] = scores(j + 1)
            update(j, st)
            return carry

        lax.fori_loop(0, i, loop, 0)
        update(i, jnp.where(_diag_mask(TQ, R), st_s[...], -jnp.inf))
        l = l_s[...]
        lser_ref[0] = m_s[...] + jnp.log(l)
        o = jnp.transpose(acc_s[...] / l).astype(BF16)
        for h in range(HEADS):
            oh = o[h * TQ:(h + 1) * TQ, :]
            o_ref[h] = oh
            y_ref[:, h * 128:(h + 1) * 128] = _dot(oh, wuv_ref[h]).astype(BF16)

    return pl.pallas_call(
        body, name="attention_fwd", grid=(nq,),
        out_shape=[pltpu.HBM((HEADS, S, KV_LORA), BF16), pltpu.HBM((S, HEADS * 128), BF16),
                   pltpu.HBM((nq, 1, R), F32)],
        in_specs=[pl.BlockSpec((HEADS, TQ, QK_PAD), lambda i: (0, i, 0)), _full((S, QK_PAD)),
                  _full((nq, QK_PAD, TQ)), _full((HEADS, KV_LORA, 128))],
        out_specs=[pl.BlockSpec((HEADS, TQ, KV_LORA), lambda i: (0, i, 0)), pl.BlockSpec((TQ, HEADS * 128), lambda i: (i, 0)),
                   pl.BlockSpec((1, 1, R), lambda i: (i, 0, 0))],
        scratch_shapes=[pltpu.VMEM((1, R), F32), pltpu.VMEM((1, R), F32), pltpu.VMEM((KV_LORA, R), F32),
                        pltpu.VMEM((TQ, R), F32)],
        compiler_params=_params(("parallel",)),
    )(*_hbm(qc, kc, kct, w_uv_t))


def _pool_forward(proj):
    S = proj.shape[0]
    RB = _row_tile(S, 256)

    def body(proj_ref, out_ref, pad_ref, sem):
        cp = pltpu.make_async_copy(proj_ref.at[:, pl.ds(O_U, POOL_W)], pad_ref.at[pl.ds(POOL_PAD, S)], sem)
        cp.start()
        pad_ref[0:POOL_PAD, :] = jnp.zeros((POOL_PAD, POOL_W), F32)
        cp.wait()
        for g, win in enumerate(POOL_WINDOWS):
            cols = slice(g * POOL_GROUP, (g + 1) * POOL_GROUP)
            for r0 in range(0, S, RB):
                u = pad_ref[POOL_PAD + r0:POOL_PAD + r0 + RB, cols]
                acc = u
                for k in range(1, win):
                    acc = acc + pad_ref[POOL_PAD + r0 - k:POOL_PAD + r0 - k + RB, cols]
                if r0 == 0:
                    t1 = (lax.broadcasted_iota(jnp.int32, (RB, POOL_GROUP), 0) + 1).astype(F32)
                    mean = acc / jnp.minimum(t1, float(win))
                else:
                    mean = acc * (1.0 / win)
                out_ref[r0:r0 + RB, cols] = (mean - u).astype(BF16)

    return pl.pallas_call(
        body, name="pool_forward",
        out_shape=jax.ShapeDtypeStruct((S, POOL_W), BF16),
        in_specs=[ANY], out_specs=VMEM_SPEC,
        scratch_shapes=[pltpu.VMEM((S + POOL_PAD, POOL_W), F32), pltpu.SemaphoreType.DMA],
        compiler_params=_params(),
    )(proj)


def _pool_backward(dpooled, after):
    S = dpooled.shape[0]
    RB = _row_tile(S, 256)

    def body(dp_ref, after_ref, out_ref, pad_ref, sem):
        cp = pltpu.make_async_copy(dp_ref, pad_ref.at[pl.ds(0, S)], sem)
        cp.start()
        pad_ref[S:S + POOL_PAD, :] = jnp.zeros((POOL_PAD, POOL_W), F32)
        cp.wait()
        for g, win in enumerate(POOL_WINDOWS):
            cols = slice(g * POOL_GROUP, (g + 1) * POOL_GROUP)
            head = pad_ref[0:POOL_PAD, cols]
            t1 = (lax.broadcasted_iota(jnp.int32, (POOL_PAD, POOL_GROUP), 0) + 1).astype(F32)
            pad_ref[0:POOL_PAD, cols] = head * (float(win) / jnp.minimum(t1, float(win)))
            for r0 in range(0, S, RB):
                acc = pad_ref[r0:r0 + RB, cols]
                for k in range(1, win):
                    acc = acc + pad_ref[r0 + k:r0 + k + RB, cols]
                own = pad_ref[r0:r0 + RB, cols]
                if r0 == 0:
                    own = jnp.concatenate([head, own[POOL_PAD:]], axis=0)
                out_ref[r0:r0 + RB, cols] = (acc * (1.0 / win) - own).astype(BF16)

    return pl.pallas_call(
        body, name="pool_backward",
        out_shape=jax.ShapeDtypeStruct((S, POOL_W), BF16),
        in_specs=[ANY, ANY], out_specs=VMEM_SPEC,
        scratch_shapes=[pltpu.VMEM((S + POOL_PAD, POOL_W), F32), pltpu.SemaphoreType.DMA],
        compiler_params=_params(),
    )(dpooled, after)


def _mix_out(y_mla, pooled, w_pool, pool_scale, w_o, x, mod6, T):
    S = x.shape[0]

    def body(ym_ref, pl_ref, wp_ref, ps_ref, wo_ref, x_ref, mod_ref, x1_ref, mix_ref, mi_ref):
        mi_ref[:, 0:512] = ym_ref[...]
        for g in range(len(POOL_WINDOWS)):
            cols = slice(g * POOL_GROUP, (g + 1) * POOL_GROUP)
            z = _dot(pl_ref[:, cols], wp_ref[g])
            mi_ref[:, 512 + g * POOL_GROUP:512 + (g + 1) * POOL_GROUP] = (z * ps_ref[:, cols]).astype(BF16)
        mix = _dot(mi_ref[...], wo_ref[...])
        mix_ref[...] = mix.astype(BF16)
        x1_ref[...] = x_ref[...] + mod_ref[2:3, :] * mix

    row = lambda w: pl.BlockSpec((T, w), lambda i: (i, 0))
    return pl.pallas_call(
        body, name="mix_out", grid=(S // T,),
        out_shape=[pltpu.HBM((S, D_MODEL), F32), pltpu.HBM((S, D_MODEL), BF16), pltpu.HBM((S, 1024), BF16)],
        in_specs=[row(512), row(POOL_W), _full((4, POOL_GROUP, POOL_GROUP)), _full((1, POOL_W)),
                  _full((1024, D_MODEL)), row(D_MODEL), _full((N_MOD, D_MODEL))],
        out_specs=[row(D_MODEL), row(D_MODEL), row(1024)],
        compiler_params=_params(("parallel",)),
    )(*_hbm(y_mla, pooled, w_pool, pool_scale, w_o, x, mod6))


def _ffn_forward(x1, mod6, g_ffn, g_final, target, w_gate, w_up, w_down, T):
    S = x1.shape[0]

    def body(x1_ref, mod_ref, gf_ref, gl_ref, tgt_ref, wg_ref, wu_ref, wd_ref,
             gate_ref, up_ref, act_ref, h2_ref, dff_ref, dx2_ref, st_ref, acc_s):
        i, j = pl.program_id(0), pl.program_id(1)

        @pl.when(jnp.logical_and(i == 0, j == 0))
        def _():
            st_ref[...] = jnp.zeros_like(st_ref)

        @pl.when(j == 0)
        def _():
            xh, _ = _rms(x1_ref[...])
            h2_ref[...] = ((xh * gf_ref[...]) * (1.0 + mod_ref[4:5, :]) + mod_ref[3:4, :]).astype(BF16)
            acc_s[...] = jnp.zeros_like(acc_s)

        h2 = h2_ref[...]
        gate = _dot_nt(h2, wg_ref[j])
        up = _dot_nt(h2, wu_ref[j])
        gate_ref[...] = gate.astype(BF16)
        up_ref[...] = up.astype(BF16)
        act = (gate * jax.nn.sigmoid(gate) * up).astype(BF16)
        act_ref[...] = act
        acc_s[...] += _dot(act, wd_ref[j])

        @pl.when(j == N_CHIPS - 1)
        def _():
            ff = acc_s[...]
            x2 = x1_ref[...] + mod_ref[5:6, :] * ff
            xh, r3 = _rms(x2)
            err = xh * gl_ref[...] - tgt_ref[...]
            dy = err * (1.0 / D_MODEL)
            dx2 = _rms_bwd(dy * gl_ref[...], xh, r3)
            dx2_ref[...] = dx2
            dff_ref[...] = (dx2 * mod_ref[5:6, :]).astype(BF16)
            st_ref[0:1, :] += jnp.sum(dy * xh, axis=0, keepdims=True)
            st_ref[1:2, :] += jnp.sum(dx2 * ff, axis=0, keepdims=True)
            st_ref[2:3, :] += 0.5 * jnp.sum(err * dy)

    row = pl.BlockSpec((T, D_MODEL), lambda i, j: (i, 0))
    chunk_out = pl.BlockSpec((None, T, FF_CHUNK), lambda i, j: (j, i, 0))
    big = pltpu.HBM((N_CHIPS, S, FF_CHUNK), BF16)
    wide = pltpu.HBM((S, D_MODEL), BF16)
    return pl.pallas_call(
        body, name="ffn_forward", grid=(S // T, N_CHIPS),
        out_shape=[big, big, big, wide, wide, pltpu.HBM((S, D_MODEL), F32), jax.ShapeDtypeStruct((8, D_MODEL), F32)],
        in_specs=[row, _full((N_MOD, D_MODEL)), _full((1, D_MODEL)), _full((1, D_MODEL)), row,
                  VMEM_SPEC, VMEM_SPEC, VMEM_SPEC],
        out_specs=[chunk_out, chunk_out, chunk_out, row, row, row, _full((8, D_MODEL))],
        scratch_shapes=[pltpu.VMEM((T, D_MODEL), F32)],
        compiler_params=_params(("arbitrary", "arbitrary")),
    )(*_hbm(x1, mod6, g_ffn, g_final, target), w_gate, w_up, w_down)


def _ffn_backward(dx2, x1, dff, gate, up, mod6, g_ffn, w_gate, w_up, w_down, T):
    S = x1.shape[0]

    def body(dx2_ref, x1_ref, dff_ref, gate_ref, up_ref, mod_ref, gf_ref, wg_ref, wu_ref, wd_ref,
             dgate_ref, dup_ref, dx1_ref, st_ref, acc_s):
        i, j = pl.program_id(0), pl.program_id(1)

        @pl.when(jnp.logical_and(i == 0, j == 0))
        def _():
            st_ref[...] = jnp.zeros_like(st_ref)

        @pl.when(j == 0)
        def _():
            acc_s[...] = jnp.zeros_like(acc_s)

        for r0 in range(0, T, T // 2):
            rows = slice(r0, r0 + T // 2)
            gate, up = gate_ref[rows, :].astype(F32), up_ref[rows, :].astype(F32)
            sg = jax.nn.sigmoid(gate)
            dact = _dot_nt(dff_ref[rows, :], wd_ref[j])
            dup = (dact * (gate * sg)).astype(BF16)
            dgate = (dact * up * (sg * (1.0 + gate * (1.0 - sg)))).astype(BF16)
            dup_ref[rows, :] = dup
            dgate_ref[rows, :] = dgate
            acc_s[rows, :] += _dot(dgate, wg_ref[j]) + _dot(dup, wu_ref[j])

        @pl.when(j == N_CHIPS - 1)
        def _():
            dh2 = acc_s[...]
            xh, r2 = _rms(x1_ref[...])
            n2 = xh * gf_ref[...]
            st_ref[0:1, :] += jnp.sum(dh2, axis=0, keepdims=True)
            st_ref[1:2, :] += jnp.sum(dh2 * n2, axis=0, keepdims=True)
            dn2 = dh2 * (1.0 + mod_ref[4:5, :])
            st_ref[2:3, :] += jnp.sum(dn2 * xh, axis=0, keepdims=True)
            dx1_ref[...] = _rms_bwd(dn2 * gf_ref[...], xh, r2) + dx2_ref[...]

    row = pl.BlockSpec((T, D_MODEL), lambda i, j: (i, 0))
    chunk = pl.BlockSpec((None, T, FF_CHUNK), lambda i, j: (j, i, 0))
    big = pltpu.HBM((N_CHIPS, S, FF_CHUNK), BF16)
    return pl.pallas_call(
        body, name="ffn_backward", grid=(S // T, N_CHIPS),
        out_shape=[big, big, pltpu.HBM((S, D_MODEL), F32), jax.ShapeDtypeStruct((8, D_MODEL), F32)],
        in_specs=[row, row, row, chunk, chunk, _full((N_MOD, D_MODEL)), _full((1, D_MODEL)),
                  VMEM_SPEC, VMEM_SPEC, VMEM_SPEC],
        out_specs=[chunk, chunk, row, _full((8, D_MODEL))],
        scratch_shapes=[pltpu.VMEM((T, D_MODEL), F32)],
        compiler_params=_params(("arbitrary", "arbitrary")),
    )(*_hbm(dx2, x1, dff, gate, up, mod6, g_ffn), w_gate, w_up, w_down)


def _tn_matmul(a, b, a_spec, b_spec, groups, m, n, steps, name):
    def body(a_ref, b_ref, o_ref):
        @pl.when(pl.program_id(1) == 0)
        def _():
            o_ref[...] = jnp.zeros_like(o_ref)

        o_ref[...] += _dot_tn(a_ref[...], b_ref[...])

    return pl.pallas_call(
        body, name=name, grid=(groups, steps),
        out_shape=pltpu.HBM((groups, m, n), F32),
        in_specs=[a_spec, b_spec],
        out_specs=pl.BlockSpec((None, m, n), lambda g, i: (g, 0, 0)),
        compiler_params=_params(("parallel", "arbitrary")),
    )(*_hbm(a, b))


def _mix_backward(dx1, mix, mod6, w_o, pooled, w_pool, pool_scale, w_uv_t, o_lat, T, TQ):
    S = dx1.shape[0]

    def body(dx1_ref, mix_ref, mod_ref, wo_ref, pl_ref, wp_ref, ps_ref, wuv_ref, o_ref,
             dmix_ref, dp_ref, do_ref, dr_ref, gp_ref, guv_ref, st_ref):
        @pl.when(pl.program_id(0) == 0)
        def _():
            st_ref[...] = jnp.zeros_like(st_ref)
            gp_ref[...] = jnp.zeros_like(gp_ref)
            guv_ref[...] = jnp.zeros_like(guv_ref)

        dx1 = dx1_ref[...]
        st_ref[0:1, :] += jnp.sum(dx1 * mix_ref[...].astype(F32), axis=0, keepdims=True)
        dmix = (dx1 * mod_ref[2:3, :]).astype(BF16)
        dmix_ref[...] = dmix
        dmi = _dot_nt(dmix, wo_ref[...])
        dym = dmi[:, 0:512].astype(BF16)
        for g in range(len(POOL_WINDOWS)):
            cols = slice(g * POOL_GROUP, (g + 1) * POOL_GROUP)
            dyp = dmi[:, 512 + g * POOL_GROUP:512 + (g + 1) * POOL_GROUP]
            pooled_g = pl_ref[:, cols]
            z = _dot(pooled_g, wp_ref[g])
            st_ref[1:2, cols] += jnp.sum(dyp * z, axis=0, keepdims=True)
            dz = (dyp * ps_ref[:, cols]).astype(BF16)
            gp_ref[g] += _dot_tn(pooled_g, dz)
            dp_ref[:, cols] = _dot_nt(dz, wp_ref[g])
        for h in range(HEADS):
            dym_h = dym[:, h * 128:(h + 1) * 128]
            do = _dot_nt(dym_h, wuv_ref[h]).astype(BF16)
            do_ref[h] = do
            o_h = o_ref[h]
            guv_ref[h] += _dot_tn(o_h, dym_h)
            delta = _col_to_row(jnp.sum(do.astype(F32) * o_h.astype(F32), axis=1, keepdims=True))
            for s in range(T // TQ):
                dr_ref[s, :, h * TQ:(h + 1) * TQ] = delta[:, s * TQ:(s + 1) * TQ]

    row = lambda w: pl.BlockSpec((T, w), lambda i: (i, 0))
    heads = pl.BlockSpec((HEADS, T, KV_LORA), lambda i: (0, i, 0))
    square = jax.ShapeDtypeStruct((4, 128, 128), F32)
    return pl.pallas_call(
        body, name="mix_backward", grid=(S // T,),
        out_shape=[pltpu.HBM((S, D_MODEL), BF16), pltpu.HBM((S, POOL_W), F32), pltpu.HBM((HEADS, S, KV_LORA), BF16),
                   pltpu.HBM((S // TQ, 1, HEADS * TQ), F32), square, square, jax.ShapeDtypeStruct((8, D_MODEL), F32)],
        in_specs=[row(D_MODEL), row(D_MODEL), _full((N_MOD, D_MODEL)), _full((1024, D_MODEL)), row(POOL_W),
                  _full((4, POOL_GROUP, POOL_GROUP)), _full((1, POOL_W)), _full((HEADS, KV_LORA, 128)), heads],
        out_specs=[row(D_MODEL), row(POOL_W), heads,
                   pl.BlockSpec((T // TQ, 1, HEADS * TQ), lambda i: (i, 0, 0)), _full((4, 128, 128)),
                   _full((4, 128, 128)), _full((8, D_MODEL))],
        compiler_params=_params(("arbitrary",)),
    )(*_hbm(dx1, mix, mod6, w_o, pooled, w_pool, pool_scale, w_uv_t, o_lat))


def _attention_bwd(qc, kc, kct, do, lse_rows, delta_rows, TQ):
    S = kc.shape[0]
    R = HEADS * TQ
    nq = S // TQ

    def body(q_ref, do_ref, lser_ref, dr_ref, k_ref, kt_ref, dqt_ref, dk_ref, dqt_s, dv_s):
        i = pl.program_id(0)

        def key_rows(j):
            return pl.ds(pl.multiple_of(j * TQ, TQ), TQ)

        @pl.when(i == 0)
        def _():
            def zero(j, carry):
                dk_ref[key_rows(j), :] = jnp.zeros((TQ, QK_PAD), F32)
                dv_s[key_rows(j), :] = jnp.zeros((TQ, KV_LORA), F32)
                return carry
            lax.fori_loop(0, nq, zero, 0)

        q = q_ref[...].reshape(R, QK_PAD)
        do = do_ref[...].reshape(R, KV_LORA)
        lse, delta = lser_ref[0], dr_ref[0]
        dqt_s[...] = jnp.zeros((QK_PAD, R), F32)

        def step(j, masked):
            rows = key_rows(j)
            k = k_ref[rows, :]
            st = _dot_nt(k, q) * SM_SCALE
            if masked:
                st = jnp.where(_diag_mask(TQ, R), st, -jnp.inf)
            pt = jnp.exp(st - lse)
            dv_s[rows, :] += _dot(pt, do)
            dpt = _dot_nt(k[:, :KV_LORA], do)
            dst = (pt * (dpt - delta)).astype(BF16)
            dk_ref[rows, :] += _dot(dst, q)
            dqt_s[...] += _dot(kt_ref[j], dst)

        def loop(j, carry):
            step(j, False)
            return carry

        lax.fori_loop(0, i, loop, 0)
        step(i, True)
        dqt_ref[...] = dqt_s[...]

        @pl.when(i == nq - 1)
        def _():
            def finish(j, carry):
                rows = key_rows(j)
                dk = dk_ref[rows, :] * SM_SCALE
                dk_ref[rows, 0:KV_LORA] = dk[:, 0:KV_LORA] + dv_s[rows, :]
                dk_ref[rows, KV_LORA:QK_PAD] = dk[:, KV_LORA:QK_PAD]
                return carry
            lax.fori_loop(0, nq, finish, 0)

    tile = lambda w: pl.BlockSpec((HEADS, TQ, w), lambda i: (0, i, 0))
    row = pl.BlockSpec((1, 1, R), lambda i: (i, 0, 0))
    return pl.pallas_call(
        body, name="attention_bwd", grid=(nq,),
        out_shape=[pltpu.HBM((nq, QK_PAD, R), F32), jax.ShapeDtypeStruct((S, QK_PAD), F32)],
        in_specs=[tile(QK_PAD), tile(KV_LORA), row, row, VMEM_SPEC, VMEM_SPEC],
        out_specs=[pl.BlockSpec((None, QK_PAD, R), lambda i: (i, 0, 0)), VMEM_SPEC],
        scratch_shapes=[pltpu.VMEM((QK_PAD, R), F32), pltpu.VMEM((S, KV_LORA), F32)],
        compiler_params=_params(("arbitrary",)),
    )(*_hbm(qc, do, lse_rows, delta_rows), kc, kct)[::-1]


def _pre_attention_backward(x, dx1, proj, q, dqt, dkc, du, cos, sin, mod6, g_mix, g_q, g_kv, w_in, w_uq, w_uk_t, T, TQ):
    S = x.shape[0]

    def body(x_ref, dx1_ref, proj_ref, q_ref, dqt_ref, dkc_ref, du_ref, cos_ref, sin_ref, mod_ref, gm_ref, gq_ref,
             gkv_ref, win_ref, wuq_ref, wuk_ref, gx_ref, dproj_ref, h1_ref, guk_ref, guq_ref, st_ref, dq_ref):
        @pl.when(pl.program_id(0) == 0)
        def _():
            st_ref[...] = jnp.zeros_like(st_ref)
            guk_ref[...] = jnp.zeros_like(guk_ref)
            guq_ref[...] = jnp.zeros_like(guq_ref)

        cos_t, sin_t = cos_ref[...], sin_ref[...]
        low = lax.broadcasted_iota(jnp.int32, (T, 128), 1) < ROPE
        rope_parts = []
        for h in range(HEADS):
            dqc = jnp.concatenate([jnp.transpose(dqt_ref[s, :, h * TQ:(h + 1) * TQ]) for s in range(T // TQ)], axis=0)
            dqc = dqc * SM_SCALE
            dql = dqc[:, 0:KV_LORA].astype(BF16)
            guk_ref[h] += _dot_tn(dql, q_ref[:, h * NOPE:(h + 1) * NOPE])
            dq_ref[:, h * NOPE:(h + 1) * NOPE] = _dot(dql, wuk_ref[h]).astype(BF16)
            rope_parts.append(dqc[:, KV_LORA:QK_PAD])
        for pair in range(2):
            d = jnp.where(low, rope_parts[2 * pair], rope_parts[2 * pair + 1])
            dq_ref[:, O_QA + 128 * pair:O_QA + 128 * (pair + 1)] = _rope_bwd(d, cos_t, sin_t).astype(BF16)
        dq = dq_ref[...]
        dcq = _dot_nt(dq, wuq_ref[...])
        cqh, rq = _rms(proj_ref[:, 0:Q_LORA])
        guq_ref[...] += _dot_tn(cqh * gq_ref[...], dq)
        st_ref[3:4, 0:Q_LORA] += jnp.sum(dcq * cqh, axis=0, keepdims=True)
        dproj_ref[:, 0:Q_LORA] = _rms_bwd(dcq * gq_ref[...], cqh, rq).astype(BF16)
        dckv = dkc_ref[:, 0:KV_LORA]
        ckvh, rkv = _rms(proj_ref[:, O_CKV:O_KR])
        st_ref[4:5, 0:KV_LORA] += jnp.sum(dckv * ckvh, axis=0, keepdims=True)
        dproj_ref[:, O_CKV:O_KR] = _rms_bwd(dckv * gkv_ref[...], ckvh, rkv).astype(BF16)
        dkr = _rope_bwd(dkc_ref[:, KV_LORA:QK_PAD], cos_t, sin_t)
        dkr = jnp.where(low, dkr + pltpu.roll(dkr, ROPE, 1), 0.0)
        dproj_ref[:, O_KR:O_U] = dkr.astype(BF16)
        dproj_ref[:, O_U:PROJ_W] = du_ref[...].astype(BF16)
        dproj = dproj_ref[...]
        dh1 = jnp.concatenate([_dot(dproj, win_ref[j]) for j in range(N_CHIPS)], axis=1)
        xh, r1 = _rms(x_ref[...])
        n1 = xh * gm_ref[...]
        h1_ref[...] = (n1 * (1.0 + mod_ref[1:2, :]) + mod_ref[0:1, :]).astype(BF16)
        st_ref[0:1, :] += jnp.sum(dh1, axis=0, keepdims=True)
        st_ref[1:2, :] += jnp.sum(dh1 * n1, axis=0, keepdims=True)
        dn1 = dh1 * (1.0 + mod_ref[1:2, :])
        st_ref[2:3, :] += jnp.sum(dn1 * xh, axis=0, keepdims=True)
        gx_ref[...] = _rms_bwd(dn1 * gm_ref[...], xh, r1) + dx1_ref[...]

    row = lambda w: pl.BlockSpec((T, w), lambda i: (i, 0))
    return pl.pallas_call(
        body, name="pre_attention_backward", grid=(S // T,),
        out_shape=[jax.ShapeDtypeStruct((S, D_MODEL), F32), pltpu.HBM((S, PROJ_W), BF16),
                   pltpu.HBM((S, D_MODEL), BF16), jax.ShapeDtypeStruct((HEADS, KV_LORA, NOPE), F32),
                   jax.ShapeDtypeStruct((Q_LORA, Q_W), F32), jax.ShapeDtypeStruct((8, D_MODEL), F32)],
        in_specs=[row(D_MODEL), row(D_MODEL), row(O_KR), row(HEADS * NOPE),
                  pl.BlockSpec((T // TQ, QK_PAD, HEADS * TQ), lambda i: (i, 0, 0)),
                  row(QK_PAD), row(POOL_W), row(128), row(128), _full((N_MOD, D_MODEL)), _full((1, D_MODEL)),
                  _full((1, Q_LORA)), _full((1, KV_LORA)), _full((N_CHIPS, PROJ_W, D_MODEL // N_CHIPS)),
                  _full((Q_LORA, Q_W)), _full((HEADS, KV_LORA, NOPE))],
        out_specs=[row(D_MODEL), row(PROJ_W), row(D_MODEL), _full((HEADS, KV_LORA, NOPE)), _full((Q_LORA, Q_W)),
                   _full((8, D_MODEL))],
        scratch_shapes=[pltpu.VMEM((T, Q_W), BF16)],
        compiler_params=_params(("arbitrary",)),
    )(*_hbm(x, dx1, proj, q, dqt, dkc, du, cos, sin, mod6, g_mix, g_q, g_kv, w_in, w_uq, w_uk_t))


def _ada_grads(c_all, dmod_all, chip):
    cols = N_MOD * D_MODEL // N_CHIPS
    width = dmod_all.shape[1]

    def body(col_ref, c_ref, dcol_ref, dall_ref, gw_ref, gb_ref):
        call = c_ref[...]
        act = call * jax.nn.sigmoid(call)
        gw_ref[...] = _dot_tn(act, dcol_ref[...])
        d = dall_ref[...]
        acc = d[0:1, :]
        for b in range(1, 8):
            acc = acc + d[b:b + 1, :]
        gb_ref[...] = acc

    return pl.pallas_call(
        body, name="ada_grads",
        out_shape=[jax.ShapeDtypeStruct((D_MODEL, cols), F32), jax.ShapeDtypeStruct((1, width), F32)],
        grid_spec=pltpu.PrefetchScalarGridSpec(
            num_scalar_prefetch=1, grid=(1,),
            in_specs=[pl.BlockSpec((8, D_MODEL), lambda s, col_ref: (0, 0)),
                      pl.BlockSpec((8, cols), lambda s, col_ref: (0, col_ref[0])),
                      pl.BlockSpec((8, width), lambda s, col_ref: (0, 0))],
            out_specs=[pl.BlockSpec((D_MODEL, cols), lambda s, col_ref: (0, 0)),
                       pl.BlockSpec((1, width), lambda s, col_ref: (0, 0))]),
        compiler_params=_params(("arbitrary",)),
    )(chip, *_hbm(c_all, dmod_all, dmod_all))


def _adamw(w, g, m, v, name, g_is_landing_zone=True):
    rows, rest = w.shape[0], w.shape[1:]
    T = _row_tile(rows, 256)

    def body(w_ref, g_ref, m_ref, v_ref, *outs):
        d_ref, nm_ref, nv_ref = outs[-3:]
        g = g_ref[...]
        if g_is_landing_zone:
            outs[0][...] = g
        m2 = ADAM_B1 * m_ref[...] + (1.0 - ADAM_B1) * g
        v2 = ADAM_B2 * v_ref[...] + (1.0 - ADAM_B2) * (g * g)
        m_hat = m2 / (1.0 - ADAM_B1 ** ADAM_STEP)
        v_hat = v2 / (1.0 - ADAM_B2 ** ADAM_STEP)
        d_ref[...] = -ADAM_LR * (m_hat / (jnp.sqrt(v_hat) + ADAM_EPS) + ADAM_WD * w_ref[...])
        nm_ref[...] = m2
        nv_ref[...] = v2

    zeros = (0,) * len(rest)
    spec = pl.BlockSpec((T,) + rest, lambda i: (i,) + zeros)
    n_out = 4 if g_is_landing_zone else 3
    res = pl.pallas_call(
        body, name=name, grid=(rows // T,),
        out_shape=[jax.ShapeDtypeStruct(w.shape, F32)] * n_out,
        in_specs=[spec] * 4, out_specs=[spec] * n_out,
        compiler_params=_params(("parallel",)),
    )(*_hbm(w, g, m, v))
    return res if g_is_landing_zone else [g] + list(res)


SMALL_NAMES = ("w_uk", "w_uv", "w_pool", "g_mix", "g_q", "g_kv", "pool_scale", "g_ffn", "g_final", "b_ada")
SMALL_ROWS = 1664


def _pack_rows(parts):
    flat = jnp.concatenate([p.reshape(-1) for p in parts])
    pad = (-flat.shape[0]) % 128
    if pad:
        flat = jnp.concatenate([flat, jnp.zeros((pad,), F32)])
    return flat.reshape(-1, 128)


def kernel(x, c, positions, w_ada, b_ada, g_mix, w_in, g_q, g_kv, w_uq, w_uk, w_uv, w_pool, pool_scale, w_o, g_ffn, w_gate, w_up, w_down, g_final, loss_target, m_w_ada, m_b_ada, m_g_mix, m_w_in, m_g_q, m_g_kv, m_w_uq, m_w_uk, m_w_uv, m_w_pool, m_pool_scale, m_w_o, m_g_ffn, m_w_gate, m_w_up, m_w_down, m_g_final, v_w_ada, v_b_ada, v_g_mix, v_w_in, v_g_q, v_g_kv, v_w_uq, v_w_uk, v_w_uv, v_w_pool, v_pool_scale, v_w_o, v_g_ffn, v_w_gate, v_w_up, v_w_down, v_g_final):
    S = x.shape[1]
    T = _row_tile(S, 512)
    TQ = _row_tile(S, 512)
    TW = _row_tile(S, 4096)
    ix, iy, ic = lax.axis_index("x"), lax.axis_index("y"), lax.axis_index("c")
    chip = (2 * ix + iy).astype(jnp.int32)
    chip_arr = chip.reshape(1)
    core_arr = ic.astype(jnp.int32).reshape(1)

    xs, tgt = x[0], loss_target[0]

    tr = lambda a: jnp.transpose(a[0])
    win_t = tr(w_in)
    win_p = jnp.concatenate([win_t[:O_KR + ROPE], win_t[O_KR:O_KR + ROPE], win_t[O_KR + ROPE:]], axis=0).astype(BF16)
    wuq = w_uq[0]
    wuq_p = jnp.concatenate([wuq[:, h, :NOPE] for h in range(HEADS)] + [wuq[:, h, NOPE:] for h in range(HEADS)],
                            axis=1).astype(BF16)
    w_uk_t = jnp.transpose(w_uk[0], (1, 0, 2)).astype(BF16)
    w_uv_t = jnp.transpose(w_uv[0], (1, 0, 2)).astype(BF16)
    w_pool_b = w_pool[0].astype(BF16)
    first = [win_p, wuq_p]
    later = [w_o[0].astype(BF16), tr(w_gate).astype(BF16), tr(w_up).astype(BF16), w_down[0].astype(BF16)]
    placed = _place_shards(chip_arr, first + later)
    a_send, a_recv, a_lands, token = _split_start("first_weights_start", first, placed[:2], 6, _plan_gather_start)
    half = ROPE // 2
    freqs = jnp.power(ROPE_THETA, -jnp.arange(half, dtype=F32) / half)
    cos, sin = _rope_tables(positions.reshape(S, 1), jnp.tile(freqs, 4).reshape(1, 128) + token[0, 0])
    a_send, a_recv, a_lands, token = _split_relay(
        "first_weights_relay", a_send, a_recv, first, a_lands, cos, 6, _plan_gather_landed, _plan_gather_relay)

    ada_cols = w_ada.shape[2]
    b_cols = lax.dynamic_slice(b_ada, (0, chip * ada_cols), (1, ada_cols))
    mod, c_all = _mod_exchange(c, w_ada[0], b_cols + token[0, 0])
    mod6 = mod.reshape(N_MOD, D_MODEL)
    a_lands = _split_wait("first_weights_wait", a_send, a_recv, [], a_lands, mod, _plan_gather_wait)
    w_in_f = a_lands[0]
    w_uq_f = a_lands[1].reshape(Q_LORA, Q_W)
    wg_lands, mod6, w_in_f = lax.optimization_barrier((placed[2:], mod6, w_in_f))
    wg_send, wg_recv, wg_lands, token = _split_start(
        "weights_start", later, wg_lands, 3 * len(later), _plan_gather_start)
    mod6 = mod6 + token[0, 0]

    proj, q, qc, kc, kct = _pre_attention(xs, mod6, g_mix, g_q, g_kv, w_in_f, w_uq_f, w_uk_t, cos, sin, T, TQ)
    o_lat, y_mla, lse_rows = _attention_fwd(qc, kc, kct, w_uv_t, TQ)
    wg_send, wg_recv, wg_lands, token = _split_relay(
        "weights_relay", wg_send, wg_recv, later, wg_lands, y_mla, 3 * len(later), _plan_gather_landed,
        _plan_gather_relay)
    pooled = _pool_forward(proj)
    wg_lands = _split_wait("weights_wait", wg_send, wg_recv, [], wg_lands, pooled, _plan_gather_wait)
    w_o_f = wg_lands[0].reshape(1024, D_MODEL)
    w_gate_f, w_up_f, w_down_f = wg_lands[1], wg_lands[2], wg_lands[3]
    x1, mix, mix_in = _mix_out(y_mla, pooled, w_pool_b, pool_scale, w_o_f, xs, mod6, _row_tile(S, 1024))
    gate, up, act, h2, dff, dx2, st_f = _ffn_forward(
        x1, mod6, g_ffn, g_final.reshape(1, D_MODEL), tgt, w_gate_f, w_up_f, w_down_f, T)

    dgate, dup, dx1, st_b = _ffn_backward(dx2, x1, dff, gate, up, mod6, g_ffn, w_gate_f, w_up_f, w_down_f, T)
    steps = S // TW
    chunk_spec = pl.BlockSpec((None, TW, FF_CHUNK), lambda g, i: (g, i, 0))
    wide_spec = pl.BlockSpec((TW, D_MODEL), lambda g, i: (i, 0))
    g_down = _tn_matmul(act, dff, chunk_spec, wide_spec, N_CHIPS, FF_CHUNK, D_MODEL, steps, "grad_w_down")
    g_gate = _tn_matmul(dgate, h2, chunk_spec, wide_spec, N_CHIPS, FF_CHUNK, D_MODEL, steps, "grad_w_gate")
    g_up = _tn_matmul(dup, h2, chunk_spec, wide_spec, N_CHIPS, FF_CHUNK, D_MODEL, steps, "grad_w_up")

    half_shapes = lambda gs: [jax.ShapeDtypeStruct((N_CHIPS, g.shape[1] // 2, g.shape[2]), F32) for g in gs]
    ffn_grads = [g_gate, g_up, g_down]
    f_send, f_recv, f_lands, token = _split_start(
        "ffn_swap_start", ffn_grads, half_shapes(ffn_grads), len(ffn_grads), _plan_swap_start)
    dmix, dpooled, do_lat, delta_rows, g_pool, g_uv_t, st_m = _mix_backward(
        dx1, mix, mod6 + token[0, 0], w_o_f, pooled, w_pool_b, pool_scale, w_uv_t, o_lat, T, TQ)
    g_o = [_tn_matmul(mix_in, dmix, wide_spec, wide_spec, 1, 1024, D_MODEL, steps, "grad_w_o").reshape(N_CHIPS, -1, D_MODEL)]
    o_send, o_recv, o_lands, token = _split_start("w_o_swap_start", g_o, half_shapes(g_o), 1, _plan_swap_start)
    du = _pool_backward(dpooled, token)
    f_got = _split_wait("ffn_swap_wait", f_send, f_recv, ffn_grads, f_lands, du, _plan_swap_wait)
    f_got += _split_wait("w_o_swap_wait", o_send, o_recv, g_o, o_lands, du, _plan_swap_wait)
    far_grads = ffn_grads + g_o
    f_sums = _add_my_halves(core_arr, far_grads, f_got, "add_half_far")
    f_send, f_recv, f_lands, token = _split_start(
        "far_exchange_start", f_sums, [jax.ShapeDtypeStruct((3,) + s.shape[1:], F32) for s in f_sums],
        3 * len(f_sums), _plan_exchange_start)
    delta_rows = delta_rows + token[0, 0]
    dkc, dqt = _attention_bwd(qc, kc, kct, do_lat, lse_rows, delta_rows, TQ)
    grad_x, dproj, h1, g_uk_t, uq, st_p = _pre_attention_backward(
        xs, dx1, proj, q, dqt, dkc, du, cos, sin, mod6, g_mix, g_q, g_kv, w_in_f, w_uq_f, w_uk_t, T, TQ)
    rows_in = D_MODEL // N_CHIPS
    g_in_p = _tn_matmul(dproj, h1, pl.BlockSpec((TW, PROJ_W), lambda g, i: (i, 0)),
                        pl.BlockSpec((TW, rows_in), lambda g, i: (i, g)), N_CHIPS, PROJ_W, rows_in, steps, "grad_w_in")

    g_in = jnp.concatenate([g_in_p[:, :O_KR + ROPE], g_in_p[:, O_U:]], axis=1)
    g_uq = jnp.concatenate([jnp.concatenate([uq[:, h * NOPE:(h + 1) * NOPE], uq[:, O_QA + h * ROPE:O_QA + (h + 1) * ROPE]],
                                            axis=1) for h in range(HEADS)], axis=1).reshape(N_CHIPS, -1, HEADS * HEAD_QK)
    small = _pack_rows([g_uk_t, g_uv_t, g_pool, st_p[2], st_p[3, :Q_LORA], st_p[4, :KV_LORA], st_m[1, :POOL_W],
                        st_b[2], st_f[0]])
    small = jnp.concatenate([small, jnp.zeros((SMALL_ROWS - small.shape[0], 128), F32)]).reshape(N_CHIPS, -1, 128)
    grads = [g_in, g_uq, small]
    dmod = jnp.concatenate([jnp.stack([st_p[0], st_p[1], st_m[0], st_b[0], st_b[1], st_f[1]]).reshape(48, 128),
                            jnp.zeros((8, 128), F32).at[0, 0].set(st_f[2, 0])])

    got, dmod_all = _grad_swap_halves(grads, dmod)
    chip_sums = _add_my_halves(core_arr, grads, got, "add_half_near")
    n_send, n_recv, n_lands, token = _split_start(
        "near_exchange_start", chip_sums, [jax.ShapeDtypeStruct((3,) + s.shape[1:], F32) for s in chip_sums],
        3 * len(chip_sums), _plan_exchange_start)

    f_others = _split_wait("far_exchange_wait", f_send, f_recv, f_sums, f_lands, token, _plan_exchange_wait)
    chip_core = jnp.concatenate([chip_arr, core_arr])
    f_pairs = (_add_chips_into_pairs(chip_core, f_sums[:2], f_others[:2], "add_chips_gate_up")
               + _add_chips_into_pairs(chip_core, f_sums[2:], f_others[2:], "add_chips_down_o"))
    f_send, f_recv, f_pairs, token = _split_start("far_finish_start", [], f_pairs, len(f_pairs), _plan_finish_start)
    gw_ada, gb_ada = _ada_grads(c_all, dmod_all.reshape(8, -1) + token[0, 0], chip_arr)
    loss = gb_ada[0, N_MOD * D_MODEL]
    gb_ada = gb_ada[:, :N_MOD * D_MODEL]
    f_fulls = _split_wait("far_finish_wait", f_send, f_recv, [], f_pairs, gw_ada, _plan_finish_wait)
    gw_gate, gw_up, gw_down, gw_o = [f.reshape(-1, f.shape[2]) for f in f_fulls]

    untr = lambda a: jnp.transpose(a)[None]
    grad_out, delta_out, newm_out, newv_out = {}, {}, {}, {}

    def adam_sharded(n, w, g2, m, v, transposed, landed=True):
        view = (lambda a: jnp.transpose(a[0])) if transposed else (lambda a: a[0])
        back = untr if transposed else (lambda a: a[None])
        g_, d_, m_, v_ = _adamw(view(w), g2.reshape(view(w).shape), view(m), view(v), "adamw_" + n, landed)
        grad_out[n], delta_out[n], newm_out[n], newv_out[n] = back(g_), back(d_), back(m_), back(v_)
        return d_

    done = [adam_sharded("w_gate", w_gate, gw_gate, m_w_gate, v_w_gate, True),
            adam_sharded("w_up", w_up, gw_up, m_w_up, v_w_up, True),
            adam_sharded("w_down", w_down, gw_down, m_w_down, v_w_down, False),
            adam_sharded("w_o", w_o, gw_o, m_w_o, v_w_o, False)]
    after_all = jnp.stack([d[0, 0] for d in done])

    others = _split_wait("near_exchange_wait", n_send, n_recv, chip_sums, n_lands, after_all, _plan_exchange_wait)
    n_pairs = _add_chips_into_pairs(chip_core, chip_sums[:2], others[:2], "add_chips_in_uq")
    small_grid = _add_chips_into_grid(chip_core, chip_sums[2], others[2], "add_chips_small")
    n_send, n_recv, n_lands, token = _split_start(
        "near_finish_start", [], n_pairs + [small_grid], 2 + len(RELATIONS), _plan_near_finish_start)
    gw_ada, _ = lax.optimization_barrier((gw_ada, token))
    d_ada = adam_sharded("w_ada", w_ada, gw_ada, m_w_ada, v_w_ada, False, landed=False)
    n_lands = _split_wait("near_finish_wait", n_send, n_recv, [], n_lands, d_ada, _plan_near_finish_wait)
    gw_in, gw_uq = [f.reshape(-1, f.shape[2]) for f in n_lands[:2]]
    small_all = n_lands[2].reshape(SMALL_ROWS * 128)
    adam_sharded("w_in", w_in, gw_in, m_w_in, v_w_in, True)
    adam_sharded("w_uq", w_uq, gw_uq, m_w_uq, v_w_uq, False)

    n_sq = KV_LORA * HEADS * 128
    sizes = [n_sq, n_sq, n_sq, D_MODEL, Q_LORA, KV_LORA, POOL_W, D_MODEL, D_MODEL]
    offs = [0]
    for s_ in sizes:
        offs.append(offs[-1] + s_)
    piece = lambda k: small_all[offs[k]:offs[k + 1]]
    grads_small = {
        "w_uk": jnp.transpose(piece(0).reshape(HEADS, KV_LORA, NOPE), (1, 0, 2)),
        "w_uv": jnp.transpose(piece(1).reshape(HEADS, KV_LORA, 128), (1, 0, 2)),
        "w_pool": piece(2).reshape(4, POOL_GROUP, POOL_GROUP),
        "g_mix": piece(3), "g_q": piece(4), "g_kv": piece(5), "pool_scale": piece(6), "g_ffn": piece(7),
        "g_final": piece(8), "b_ada": gb_ada.reshape(-1),
    }
    weights_small = {"w_uk": w_uk, "w_uv": w_uv, "w_pool": w_pool, "g_mix": g_mix, "g_q": g_q, "g_kv": g_kv,
                     "pool_scale": pool_scale, "g_ffn": g_ffn, "g_final": g_final, "b_ada": b_ada}
    m_small = {"w_uk": m_w_uk, "w_uv": m_w_uv, "w_pool": m_w_pool, "g_mix": m_g_mix, "g_q": m_g_q, "g_kv": m_g_kv,
               "pool_scale": m_pool_scale, "g_ffn": m_g_ffn, "g_final": m_g_final, "b_ada": m_b_ada}
    v_small = {"w_uk": v_w_uk, "w_uv": v_w_uv, "w_pool": v_w_pool, "g_mix": v_g_mix, "g_q": v_g_q, "g_kv": v_g_kv,
               "pool_scale": v_pool_scale, "g_ffn": v_g_ffn, "g_final": v_g_final, "b_ada": v_b_ada}
    pack = lambda d: _pack_rows([d[n] for n in SMALL_NAMES])
    _, d_s, m_s, v_s = _adamw(pack(weights_small), pack(grads_small), pack(m_small), pack(v_small), "adamw_small",
                              g_is_landing_zone=False)

    def unpack(flat2d):
        flat = flat2d.reshape(-1)
        out, o = {}, 0
        for n in SMALL_NAMES:
            size = weights_small[n].size
            out[n] = flat[o:o + size].reshape(weights_small[n].shape)
            o += size
        return out

    delta_s, newm_s, newv_s = unpack(d_s), unpack(m_s), unpack(v_s)

    for n in SMALL_NAMES:
        grad_out[n] = grads_small[n].reshape(weights_small[n].shape)
        delta_out[n], newm_out[n], newv_out[n] = delta_s[n], newm_s[n], newv_s[n]

    order = ("w_ada", "b_ada", "g_mix", "w_in", "g_q", "g_kv", "w_uq", "w_uk", "w_uv", "w_pool", "pool_scale", "w_o",
             "g_ffn", "w_gate", "w_up", "w_down", "g_final")
    return (loss, grad_x.reshape(x.shape), *[grad_out[n] for n in order], *[delta_out[n] for n in order],
            *[newm_out[n] for n in order], *[newv_out[n] for n in order])
```

```python
import functools

import jax
import jax.numpy as jnp
from jax import lax
from jax.experimental import pallas as pl
from jax.experimental.pallas import tpu as pltpu

F32 = jnp.float32
BF16 = jnp.bfloat16

D_MODEL = 1024
HEADS = 4
NOPE = 128
ROPE = 64
HEAD_QK = NOPE + ROPE
Q_LORA = 256
KV_LORA = 128
POOL_W = 512
POOL_WINDOWS = (2, 4, 8, 16)
POOL_GROUP = 128
POOL_PAD = 16
D_FF = 2816
N_CHIPS = 4
FF_CHUNK = D_FF // N_CHIPS
N_MOD = 6
EPS = 1e-6
SM_SCALE = HEAD_QK ** -0.5
ROPE_THETA = 10000.0
QK_PAD = 256
CHUNK = 64
CHUNK_SHIFT = 6

ADAM_LR = 0.001
ADAM_B1 = 0.9
ADAM_B2 = 0.999
ADAM_EPS = 1e-08
ADAM_WD = 0.01
ADAM_STEP = 10

VMEM_LIMIT = 48 * 1024 * 1024
MESH = pl.DeviceIdType.MESH
ANY = pl.BlockSpec(memory_space=pl.ANY)
VMEM_SPEC = pl.BlockSpec(memory_space=pltpu.VMEM)

PROJ_W = 1024
O_CKV = 256
O_KR = 384
O_U = 512
Q_W = 768
O_QA = 512
O_QB = 640


def _params(sem=None, vmem=VMEM_LIMIT):
    kw = dict(vmem_limit_bytes=vmem)
    if sem is not None:
        kw["dimension_semantics"] = sem
    return pltpu.CompilerParams(**kw)


def _dot(a, b):
    return jnp.dot(a.astype(BF16), b.astype(BF16), preferred_element_type=F32)


def _dot_nt(a, b):
    return lax.dot_general(a.astype(BF16), b.astype(BF16), (((1,), (1,)), ((), ())), preferred_element_type=F32)


def _dot_tn(a, b):
    return lax.dot_general(a.astype(BF16), b.astype(BF16), (((0,), (0,)), ((), ())), preferred_element_type=F32)


def _row_tile(rows, target):
    best = rows
    for t in range(8, min(rows, target) + 1, 8):
        if rows % t == 0:
            best = t
    return best if rows % best == 0 and best <= target else rows


def _rms(x):
    r = lax.rsqrt(jnp.mean(x * x, axis=-1, keepdims=True) + EPS)
    return x * r, r


def _rms_bwd(dxh, xh, r):
    return r * (dxh - xh * jnp.mean(dxh * xh, axis=-1, keepdims=True))


def _lane_first_half(shape):
    lane = lax.broadcasted_iota(jnp.int32, shape, 1)
    return (lane & (ROPE - 1)) < (ROPE // 2)


def _rope(a, cos, sin):
    first = _lane_first_half(a.shape)
    up = pltpu.roll(a, 96, 1)
    dn = pltpu.roll(a, 32, 1)
    return a * cos + jnp.where(first, -up, dn) * sin


def _rope_bwd(d, cos, sin):
    first = _lane_first_half(d.shape)
    up = pltpu.roll(d, 96, 1)
    dn = pltpu.roll(d, 32, 1)
    return d * cos + jnp.where(first, up, -dn) * sin


RELATIONS = tuple((dx, dy, dc) for dx in (0, 1) for dy in (0, 1) for dc in (0, 1) if (dx, dy, dc) != (0, 0, 0))
CHIP_RELATIONS = ((1, 0), (0, 1), (1, 1))


def _flip(v, d):
    return 1 - v if d else v


def _place():
    return lax.axis_index("x"), lax.axis_index("y"), lax.axis_index("c")


def _remote(src, dst, send_sem, recv_sem, target):
    return pltpu.make_async_remote_copy(src_ref=src, dst_ref=dst, send_sem=send_sem, recv_sem=recv_sem,
                                        device_id=target, device_id_type=MESH)


def _mod_exchange(c_row, w_ada, b_ada):
    cols = w_ada.shape[1]

    def body(c_ref, w_ref, b_ref, mod_ref, call_ref, part_ref, send1, recv1, loc1, send2, recv2, loc2):
        x, y, c = _place()
        me = 4 * x + 2 * y + c
        own = pltpu.make_async_copy(c_ref, call_ref.at[pl.ds(me, 1)], loc1)
        own.start()
        sends = []
        for k, (dx, dy, dc) in enumerate(RELATIONS):
            cp = _remote(c_ref, call_ref.at[pl.ds(me, 1)], send1.at[k], recv1.at[k],
                         (_flip(x, dx), _flip(y, dy), _flip(c, dc)))
            cp.start()
            sends.append(cp)
        for k, (dx, dy, dc) in enumerate(RELATIONS):
            src = 4 * _flip(x, dx) + 2 * _flip(y, dy) + _flip(c, dc)
            _remote(c_ref, call_ref.at[pl.ds(src, 1)], send1.at[k], recv1.at[k], (x, y, c)).wait_recv()
        own.wait()
        for cp in sends:
            cp.wait_send()
        call = call_ref[...]
        act = call * jax.nn.sigmoid(call)
        part_ref[...] = _dot(act, w_ref[...]) + b_ref[...]
        chip = 2 * x + y
        mine = pltpu.make_async_copy(part_ref.at[pl.ds(me, 1)], mod_ref.at[pl.ds(chip, 1)], loc2)
        mine.start()
        sends = []
        for k, (dx, dy) in enumerate(CHIP_RELATIONS):
            tx, ty = _flip(x, dx), _flip(y, dy)
            tb = 4 * tx + 2 * ty + c
            cp = _remote(part_ref.at[pl.ds(tb, 1)], mod_ref.at[pl.ds(chip, 1)], send2.at[k], recv2.at[k], (tx, ty, c))
            cp.start()
            sends.append(cp)
        for k, (dx, dy) in enumerate(CHIP_RELATIONS):
            src_chip = 2 * _flip(x, dx) + _flip(y, dy)
            _remote(part_ref.at[pl.ds(me, 1)], mod_ref.at[pl.ds(src_chip, 1)], send2.at[k], recv2.at[k],
                    (x, y, c)).wait_recv()
        mine.wait()
        for cp in sends:
            cp.wait_send()

    return pl.pallas_call(
        body, name="mod_exchange",
        out_shape=[jax.ShapeDtypeStruct((N_CHIPS, cols), F32), jax.ShapeDtypeStruct((8, D_MODEL), F32)],
        in_specs=[VMEM_SPEC, VMEM_SPEC, VMEM_SPEC], out_specs=[VMEM_SPEC, VMEM_SPEC],
        scratch_shapes=[pltpu.VMEM((8, cols), F32),
                        pltpu.SemaphoreType.DMA((7,)), pltpu.SemaphoreType.DMA((7,)), pltpu.SemaphoreType.DMA,
                        pltpu.SemaphoreType.DMA((3,)), pltpu.SemaphoreType.DMA((3,)), pltpu.SemaphoreType.DMA],
        compiler_params=_params(),
    )(c_row, w_ada, b_ada)


HBM_SPEC = pl.BlockSpec(memory_space=pltpu.HBM)
SEM_SPEC = pl.BlockSpec(memory_space=pltpu.SEMAPHORE)
DATAFLOW = pltpu.SideEffectType.DATAFLOW_SIDE_EFFECTING


def _in_hbm(a):
    return pltpu.with_memory_space_constraint(a, pltpu.HBM)


def _hbm(*arrays):
    return tuple(_in_hbm(a) for a in arrays)


def _hbm_like(arrays):
    return [pltpu.HBM(a.shape, a.dtype) for a in arrays]


def _split_start(name, srcs, lands, n_remote, plan):
    lands = [lax.empty(a.shape, a.dtype) if isinstance(a, jax.ShapeDtypeStruct) else a for a in lands]
    n, m = len(srcs), len(lands)

    def body(*refs):
        src_refs, land_refs = refs[:n], refs[n:n + m]
        send_sems, recv_sems, token = refs[n + m], refs[n + m + 1], refs[n + 2 * m + 2]
        remote = plan(_place(), src_refs, land_refs)
        assert len(remote) == n_remote
        for i, (s, d, target) in enumerate(remote):
            _remote(s, d, send_sems.at[i], recv_sems.at[i], target).start()
        token[...] = jnp.zeros_like(token)

    res = pl.pallas_call(
        body, name=name,
        out_shape=(pltpu.SemaphoreType.DMA((n_remote,)), pltpu.SemaphoreType.DMA((n_remote,)),
                   *_hbm_like(lands), jax.ShapeDtypeStruct((8, 128), F32)),
        in_specs=[HBM_SPEC] * (n + m),
        out_specs=(SEM_SPEC, SEM_SPEC, *([HBM_SPEC] * m), VMEM_SPEC),
        input_output_aliases={n + i: 2 + i for i in range(m)},
        compiler_params=pltpu.CompilerParams(has_side_effects=DATAFLOW),
    )(*[_in_hbm(a) for a in srcs], *[_in_hbm(a) for a in lands])
    return res[0], res[1], list(res[2:2 + m]), res[2 + m]


def _split_wait(name, send_sems, recv_sems, srcs, lands, after, plan):
    n, m = len(srcs), len(lands)

    def body(*refs):
        src_refs, land_refs = refs[:n], refs[n:n + m]
        send_sems, recv_sems = refs[n + m], refs[n + m + 1]
        place = _place()
        for i, (s, d) in enumerate(plan(place, src_refs, land_refs)):
            cp = _remote(s, d, send_sems.at[i], recv_sems.at[i], place)
            cp.wait_send()
            cp.wait_recv()

    res = pl.pallas_call(
        body, name=name,
        out_shape=tuple(_hbm_like(lands)),
        in_specs=[HBM_SPEC] * (n + m) + [SEM_SPEC, SEM_SPEC, ANY],
        out_specs=tuple([HBM_SPEC] * m),
        input_output_aliases={n + i: i for i in range(m)},
        compiler_params=pltpu.CompilerParams(has_side_effects=DATAFLOW),
    )(*srcs, *lands, send_sems, recv_sems, after)
    return list(res)


def _split_relay(name, send_sems, recv_sems, srcs, lands, after, n_remote, plan_wait, plan_send):
    n, m = len(srcs), len(lands)

    def body(*refs):
        src_refs, land_refs = refs[:n], refs[n:n + m]
        old_send, old_recv = refs[n + m], refs[n + m + 1]
        new_send, new_recv = refs[n + m + 3], refs[n + m + 4]
        token = refs[n + m + 5 + m]
        place = _place()
        for i, (s, d) in enumerate(plan_wait(place, src_refs, land_refs)):
            cp = _remote(s, d, old_send.at[i], old_recv.at[i], place)
            cp.wait_send()
            cp.wait_recv()
        for i, (s, d, target) in enumerate(plan_send(place, land_refs)):
            _remote(s, d, new_send.at[i], new_recv.at[i], target).start()
        token[...] = jnp.zeros_like(token)

    res = pl.pallas_call(
        body, name=name,
        out_shape=(pltpu.SemaphoreType.DMA((n_remote,)), pltpu.SemaphoreType.DMA((n_remote,)),
                   *_hbm_like(lands), jax.ShapeDtypeStruct((8, 128), F32)),
        in_specs=[HBM_SPEC] * (n + m) + [SEM_SPEC, SEM_SPEC, ANY],
        out_specs=(SEM_SPEC, SEM_SPEC, *([HBM_SPEC] * m), VMEM_SPEC),
        input_output_aliases={n + i: 2 + i for i in range(m)},
        compiler_params=pltpu.CompilerParams(has_side_effects=DATAFLOW),
    )(*srcs, *lands, send_sems, recv_sems, after)
    return res[0], res[1], list(res[2:2 + m]), res[2 + m]


def _half(ref, core, axis=0):
    hr = ref.shape[axis] // 2
    return pl.ds(core * hr, hr)


def _plan_gather_start(place, src, land):
    x, y, c = place
    chip = 2 * x + y
    return [(s.at[_half(s, c)], l.at[chip, _half(s, c)], (_flip(x, dx), _flip(y, dy), c))
            for s, l in zip(src, land) for dx, dy in CHIP_RELATIONS]


def _plan_gather_landed(place, src, land):
    x, y, c = place
    return [(s.at[_half(s, c)], l.at[2 * _flip(x, dx) + _flip(y, dy), _half(s, c)])
            for s, l in zip(src, land) for dx, dy in CHIP_RELATIONS]


def _plan_gather_relay(place, land):
    x, y, c = place
    out = []
    for l in land:
        for dx, dy in CHIP_RELATIONS:
            got = l.at[2 * _flip(x, dx) + _flip(y, dy), _half(l, c, 1)]
            out.append((got, got, (x, y, 1 - c)))
    return out


def _plan_gather_wait(place, src, land):
    x, y, c = place
    out = []
    for l in land:
        for dx, dy in CHIP_RELATIONS:
            got = l.at[2 * _flip(x, dx) + _flip(y, dy), _half(l, 1 - c, 1)]
            out.append((got, got))
    return out


def _plan_swap_start(place, src, land):
    x, y, c = place
    return [(s.at[:, _half(s, 1 - c, 1), :], l, (x, y, 1 - c)) for s, l in zip(src, land)]


def _plan_swap_wait(place, src, land):
    return [(s.at[:, _half(s, 0, 1), :], l) for s, l in zip(src, land)]


def _plan_exchange_start(place, src, land):
    x, y, c = place
    remote = []
    for s, l in zip(src, land):
        for k, (dx, dy) in enumerate(CHIP_RELATIONS):
            tx, ty = _flip(x, dx), _flip(y, dy)
            remote.append((s.at[2 * tx + ty], l.at[k], (tx, ty, c)))
    return remote


def _plan_exchange_wait(place, src, land):
    return [(s.at[0], l.at[k]) for s, l in zip(src, land) for k in range(3)]


def _plan_finish_start(place, src, land):
    x, y, c = place
    return [(l.at[c], l.at[c], (x, y, 1 - c)) for l in land]


def _plan_finish_wait(place, src, land):
    x, y, c = place
    return [(l.at[c], l.at[1 - c]) for l in land]


def _plan_near_finish_start(place, src, land):
    x, y, c = place
    mine = land[-1].at[2 * x + y, c]
    return (_plan_finish_start(place, src, land[:-1])
            + [(mine, mine, (_flip(x, dx), _flip(y, dy), _flip(c, dc))) for dx, dy, dc in RELATIONS])


def _plan_near_finish_wait(place, src, land):
    x, y, c = place
    mine = land[-1].at[2 * x + y, c]
    return (_plan_finish_wait(place, src, land[:-1])
            + [(mine, land[-1].at[2 * _flip(x, dx) + _flip(y, dy), _flip(c, dc)]) for dx, dy, dc in RELATIONS])


def _grad_swap_halves(grads, dmod):
    n = len(grads)

    def body(*refs):
        ins, dmod_ref = refs[:n], refs[n]
        outs, dall_ref = refs[n + 1:2 * n + 1], refs[2 * n + 1]
        send_sems, recv_sems, dsend, drecv, dloc = refs[2 * n + 2:]
        x, y, c = _place()
        me = 4 * x + 2 * y + c
        sends = []
        for w in range(n):
            hr = ins[w].shape[1] // 2
            cp = _remote(ins[w].at[:, pl.ds((1 - c) * hr, hr), :], outs[w], send_sems.at[w], recv_sems.at[w],
                         (x, y, 1 - c))
            cp.start()
            sends.append(cp)
        own = pltpu.make_async_copy(dmod_ref, dall_ref.at[me], dloc)
        own.start()
        for k, (dx, dy, dc) in enumerate(RELATIONS):
            cp = _remote(dmod_ref, dall_ref.at[me], dsend.at[k], drecv.at[k],
                         (_flip(x, dx), _flip(y, dy), _flip(c, dc)))
            cp.start()
            sends.append(cp)
        for k, (dx, dy, dc) in enumerate(RELATIONS):
            src = 4 * _flip(x, dx) + 2 * _flip(y, dy) + _flip(c, dc)
            _remote(dmod_ref, dall_ref.at[src], dsend.at[k], drecv.at[k], (x, y, c)).wait_recv()
        for w in range(n):
            _remote(outs[w], outs[w], send_sems.at[w], recv_sems.at[w], (x, y, c)).wait_recv()
        own.wait()
        for cp in sends:
            cp.wait_send()

    out_shape = [pltpu.HBM((N_CHIPS, g.shape[1] // 2, g.shape[2]), F32) for g in grads]
    out_shape.append(pltpu.HBM((8,) + dmod.shape, F32))
    res = pl.pallas_call(
        body, name="grad_swap_halves",
        out_shape=out_shape, in_specs=[ANY] * n + [VMEM_SPEC], out_specs=[ANY] * (n + 1),
        scratch_shapes=[pltpu.SemaphoreType.DMA((n,)), pltpu.SemaphoreType.DMA((n,)),
                        pltpu.SemaphoreType.DMA((7,)), pltpu.SemaphoreType.DMA((7,)), pltpu.SemaphoreType.DMA],
        compiler_params=_params(),
    )(*grads, dmod)
    return res[:n], res[n]


def _add_my_halves(core, fulls, gots, name):
    n = len(fulls)

    def body(core_ref, *refs):
        for w in range(n):
            refs[2 * n + w][...] = refs[w][...] + refs[n + w][...]

    mine = lambda g: pl.BlockSpec((None,) + g.shape[1:], lambda s, core_ref: (s, core_ref[0], 0))
    slab = lambda g: pl.BlockSpec((None,) + g.shape[1:], lambda s, core_ref: (s, 0, 0))
    return list(pl.pallas_call(
        body, name=name,
        out_shape=[pltpu.HBM(g.shape, F32) for g in gots],
        grid_spec=pltpu.PrefetchScalarGridSpec(
            num_scalar_prefetch=1, grid=(N_CHIPS,),
            in_specs=[mine(g) for g in gots] + [slab(g) for g in gots],
            out_specs=[slab(g) for g in gots]),
        compiler_params=_params(("arbitrary",)),
    )(core, *_hbm(*fulls, *gots)))


def _add_chips_into_pairs(chip_core, mines, gots, name):
    n = len(mines)

    def body(cc_ref, *refs):
        for w in range(n):
            b_ref = refs[n + w]
            refs[2 * n + w][...] = ((refs[w][...] + b_ref[0]) + b_ref[1]) + b_ref[2]

    return list(pl.pallas_call(
        body, name=name,
        out_shape=[pltpu.HBM((2,) + m.shape[1:], F32) for m in mines],
        grid_spec=pltpu.PrefetchScalarGridSpec(
            num_scalar_prefetch=1, grid=(1,),
            in_specs=[pl.BlockSpec((None,) + m.shape[1:], lambda s, cc_ref: (cc_ref[0], 0, 0)) for m in mines]
            + [pl.BlockSpec(g.shape, lambda s, cc_ref: (0, 0, 0)) for g in gots],
            out_specs=[pl.BlockSpec((None,) + m.shape[1:], lambda s, cc_ref: (cc_ref[1], 0, 0)) for m in mines]),
        compiler_params=_params(("arbitrary",)),
    )(chip_core, *_hbm(*mines, *gots)))


def _add_chips_into_grid(chip_core, mine, got, name):
    _, hr, cols = mine.shape

    def body(cc_ref, a_ref, b_ref, o_ref):
        o_ref[...] = ((a_ref[...] + b_ref[0]) + b_ref[1]) + b_ref[2]

    return pl.pallas_call(
        body, name=name,
        out_shape=pltpu.HBM((N_CHIPS, 2, hr, cols), F32),
        grid_spec=pltpu.PrefetchScalarGridSpec(
            num_scalar_prefetch=1, grid=(1,),
            in_specs=[pl.BlockSpec((None, hr, cols), lambda s, cc_ref: (cc_ref[0], 0, 0)),
                      pl.BlockSpec((3, hr, cols), lambda s, cc_ref: (0, 0, 0))],
            out_specs=pl.BlockSpec((None, None, hr, cols), lambda s, cc_ref: (cc_ref[0], cc_ref[1], 0, 0))),
        compiler_params=_params(("arbitrary",)),
    )(chip_core, *_hbm(mine, got))


def _place_shards(chip, shards):
    n = len(shards)

    def body(chip_ref, *refs):
        for w in range(n):
            refs[n + w][...] = refs[w][...]

    return pl.pallas_call(
        body, name="place_shards",
        out_shape=[pltpu.HBM((N_CHIPS,) + s.shape, s.dtype) for s in shards],
        grid_spec=pltpu.PrefetchScalarGridSpec(
            num_scalar_prefetch=1, grid=(1,),
            in_specs=[pl.BlockSpec(s.shape, lambda i, chip_ref: (0, 0)) for s in shards],
            out_specs=[pl.BlockSpec((None,) + s.shape, lambda i, chip_ref: (chip_ref[0], 0, 0)) for s in shards]),
        compiler_params=_params(("arbitrary",)),
    )(chip, *shards)


def _rope_tables(pos_col, freqs):
    S = pos_col.shape[0]
    T = _row_tile(S, 1024)

    def body(p_ref, f_ref, cos_ref, sin_ref):
        ang = p_ref[...].astype(F32) * f_ref[...]
        cos_ref[...] = jnp.cos(ang)
        sin_ref[...] = jnp.sin(ang)

    return pl.pallas_call(
        body, name="rope_tables", grid=(S // T,),
        out_shape=[pltpu.HBM((S, 128), F32)] * 2,
        in_specs=[pl.BlockSpec((T, 1), lambda i: (i, 0)), pl.BlockSpec((1, 128), lambda i: (0, 0))],
        out_specs=[pl.BlockSpec((T, 128), lambda i: (i, 0))] * 2,
        compiler_params=_params(("parallel",)),
    )(*_hbm(pos_col, freqs))


def _full(shape):
    zeros = (0,) * len(shape)
    return pl.BlockSpec(shape, lambda *_: zeros)


def _pre_attention(x, mod6, g_mix, g_q, g_kv, w_in, w_uq, w_uk_t, cos, sin, T, TQ):
    S = x.shape[0]

    def body(x_ref, mod_ref, gm_ref, gq_ref, gkv_ref, win_ref, wuq_ref, wuk_ref, cos_ref, sin_ref,
             proj_ref, q_ref, qc_ref, kc_ref, kct_ref):
        xh, _ = _rms(x_ref[...])
        h1 = ((xh * gm_ref[...]) * (1.0 + mod_ref[1:2, :]) + mod_ref[0:1, :]).astype(BF16)
        rows_in = D_MODEL // N_CHIPS
        proj = _dot_nt(h1[:, 0:rows_in], win_ref[0])
        for j in range(1, N_CHIPS):
            proj = proj + _dot_nt(h1[:, j * rows_in:(j + 1) * rows_in], win_ref[j])
        proj_ref[...] = proj
        cqh, _ = _rms(proj[:, :Q_LORA])
        c_q = cqh * gq_ref[...]
        ckvh, _ = _rms(proj[:, O_CKV:O_KR])
        c_kv = ckvh * gkv_ref[...]
        q = _dot(c_q, wuq_ref[...])
        q_ref[...] = q.astype(BF16)
        cos_t, sin_t = cos_ref[...], sin_ref[...]
        ropes = (_rope(q[:, O_QA:O_QB], cos_t, sin_t), _rope(q[:, O_QB:Q_W], cos_t, sin_t))
        low = lax.broadcasted_iota(jnp.int32, (T, 128), 1) < ROPE
        for h in range(HEADS):
            q_lat = _dot_nt(q[:, h * NOPE:(h + 1) * NOPE], wuk_ref[h])
            keep = low if h % 2 == 0 else jnp.logical_not(low)
            qc_ref[h, :, 0:KV_LORA] = q_lat.astype(BF16)
            qc_ref[h, :, KV_LORA:QK_PAD] = jnp.where(keep, ropes[h // 2], 0.0).astype(BF16)
        k_rope = _rope(proj[:, O_KR:O_U], cos_t, sin_t)
        kc_ref[:, 0:KV_LORA] = c_kv.astype(BF16)
        kc_ref[:, KV_LORA:QK_PAD] = k_rope.astype(BF16)
        lat_t, rope_t = jnp.transpose(c_kv), jnp.transpose(k_rope)
        for s in range(T // TQ):
            kct_ref[s, 0:KV_LORA, :] = lat_t[:, s * TQ:(s + 1) * TQ].astype(BF16)
            kct_ref[s, KV_LORA:QK_PAD, :] = rope_t[:, s * TQ:(s + 1) * TQ].astype(BF16)

    row = lambda w: pl.BlockSpec((T, w), lambda i: (i, 0))
    return pl.pallas_call(
        body, name="pre_attention", grid=(S // T,),
        out_shape=[pltpu.HBM((S, PROJ_W), F32), pltpu.HBM((S, Q_W), BF16), pltpu.HBM((HEADS, S, QK_PAD), BF16),
                   pltpu.HBM((S, QK_PAD), BF16), pltpu.HBM((S // TQ, QK_PAD, TQ), BF16)],
        in_specs=[row(D_MODEL), _full((N_MOD, D_MODEL)), _full((1, D_MODEL)), _full((1, Q_LORA)), _full((1, KV_LORA)),
                  _full((N_CHIPS, PROJ_W, D_MODEL // N_CHIPS)), _full((Q_LORA, Q_W)), _full((HEADS, KV_LORA, NOPE)),
                  row(128), row(128)],
        out_specs=[row(PROJ_W), row(Q_W), pl.BlockSpec((HEADS, T, QK_PAD), lambda i: (0, i, 0)), row(QK_PAD),
                   pl.BlockSpec((T // TQ, QK_PAD, TQ), lambda i: (i, 0, 0))],
        compiler_params=_params(("parallel",)),
    )(*_hbm(x, mod6, g_mix, g_q, g_kv, w_in, w_uq, w_uk_t, cos, sin))


def _diag_mask(TQ, width):
    key = lax.broadcasted_iota(jnp.int32, (TQ, width), 0) >> CHUNK_SHIFT
    qry = (lax.broadcasted_iota(jnp.int32, (TQ, width), 1) & (TQ - 1)) >> CHUNK_SHIFT
    return key <= qry


def _col_to_row(col):
    return jnp.transpose(jnp.broadcast_to(col, (col.shape[0], 128)))[0:1, :]


def _attention_fwd(qc, kc, kct, w_uv_t, TQ):
    S = kc.shape[0]
    R = HEADS * TQ
    nq = S // TQ

    def body(q_ref, k_ref, kt_ref, wuv_ref, o_ref, y_ref, lser_ref, m_s, l_s, acc_s, st_s):
        i = pl.program_id(0)
        q = q_ref[...].reshape(R, QK_PAD)
        m_s[...] = jnp.full((1, R), -jnp.inf, F32)
        l_s[...] = jnp.zeros((1, R), F32)
        acc_s[...] = jnp.zeros((KV_LORA, R), F32)

        def scores(j):
            return _dot_nt(k_ref[pl.ds(pl.multiple_of(j * TQ, TQ), TQ), :], q) * SM_SCALE

        def update(j, st):
            m_old = m_s[...]
            m_new = jnp.maximum(m_old, jnp.max(st, axis=0, keepdims=True))
            pt = jnp.exp(st - m_new)
            alpha = jnp.exp(m_old - m_new)
            l_s[...] = alpha * l_s[...] + jnp.sum(pt, axis=0, keepdims=True)
            acc_s[...] = alpha * acc_s[...] + _dot(kt_ref[j, 0:KV_LORA, :], pt)
            m_s[...] = m_new

        st_s[...] = scores(0)

        def loop(j, carry):
            st = st_s[...]
            st_s[...] = scores(j + 1)
            update(j, st)
            return carry

        lax.fori_loop(0, i, loop, 0)
        update(i, jnp.where(_diag_mask(TQ, R), st_s[...], -jnp.inf))
        l = l_s[...]
        lser_ref[0] = m_s[...] + jnp.log(l)
        o = jnp.transpose(acc_s[...] / l).astype(BF16)
        for h in range(HEADS):
            oh = o[h * TQ:(h + 1) * TQ, :]
            o_ref[h] = oh
            y_ref[:, h * 128:(h + 1) * 128] = _dot(oh, wuv_ref[h]).astype(BF16)

    return pl.pallas_call(
        body, name="attention_fwd", grid=(nq,),
        out_shape=[pltpu.HBM((HEADS, S, KV_LORA), BF16), pltpu.HBM((S, HEADS * 128), BF16),
                   pltpu.HBM((nq, 1, R), F32)],
        in_specs=[pl.BlockSpec((HEADS, TQ, QK_PAD), lambda i: (0, i, 0)), _full((S, QK_PAD)),
                  _full((nq, QK_PAD, TQ)), _full((HEADS, KV_LORA, 128))],
        out_specs=[pl.BlockSpec((HEADS, TQ, KV_LORA), lambda i: (0, i, 0)), pl.BlockSpec((TQ, HEADS * 128), lambda i: (i, 0)),
                   pl.BlockSpec((1, 1, R), lambda i: (i, 0, 0))],
        scratch_shapes=[pltpu.VMEM((1, R), F32), pltpu.VMEM((1, R), F32), pltpu.VMEM((KV_LORA, R), F32),
                        pltpu.VMEM((TQ, R), F32)],
        compiler_params=_params(("parallel",)),
    )(*_hbm(qc, kc, kct, w_uv_t))


def _pool_forward(proj):
    S = proj.shape[0]
    RB = _row_tile(S, 256)

    def body(proj_ref, out_ref, pad_ref, sem):
        cp = pltpu.make_async_copy(proj_ref.at[:, pl.ds(O_U, POOL_W)], pad_ref.at[pl.ds(POOL_PAD, S)], sem)
        cp.start()
        pad_ref[0:POOL_PAD, :] = jnp.zeros((POOL_PAD, POOL_W), F32)
        cp.wait()
        for g, win in enumerate(POOL_WINDOWS):
            cols = slice(g * POOL_GROUP, (g + 1) * POOL_GROUP)
            for r0 in range(0, S, RB):
                u = pad_ref[POOL_PAD + r0:POOL_PAD + r0 + RB, cols]
                acc = u
                for k in range(1, win):
                    acc = acc + pad_ref[POOL_PAD + r0 - k:POOL_PAD + r0 - k + RB, cols]
                if r0 == 0:
                    t1 = (lax.broadcasted_iota(jnp.int32, (RB, POOL_GROUP), 0) + 1).astype(F32)
                    mean = acc / jnp.minimum(t1, float(win))
                else:
                    mean = acc * (1.0 / win)
                out_ref[r0:r0 + RB, cols] = (mean - u).astype(BF16)

    return pl.pallas_call(
        body, name="pool_forward",
        out_shape=jax.ShapeDtypeStruct((S, POOL_W), BF16),
        in_specs=[ANY], out_specs=VMEM_SPEC,
        scratch_shapes=[pltpu.VMEM((S + POOL_PAD, POOL_W), F32), pltpu.SemaphoreType.DMA],
        compiler_params=_params(),
    )(proj)


def _pool_backward(dpooled, after):
    S = dpooled.shape[0]
    RB = _row_tile(S, 256)

    def body(dp_ref, after_ref, out_ref, pad_ref, sem):
        cp = pltpu.make_async_copy(dp_ref, pad_ref.at[pl.ds(0, S)], sem)
        cp.start()
        pad_ref[S:S + POOL_PAD, :] = jnp.zeros((POOL_PAD, POOL_W), F32)
        cp.wait()
        for g, win in enumerate(POOL_WINDOWS):
            cols = slice(g * POOL_GROUP, (g + 1) * POOL_GROUP)
            head = pad_ref[0:POOL_PAD, cols]
            t1 = (lax.broadcasted_iota(jnp.int32, (POOL_PAD, POOL_GROUP), 0) + 1).astype(F32)
            pad_ref[0:POOL_PAD, cols] = head * (float(win) / jnp.minimum(t1, float(win)))
            for r0 in range(0, S, RB):
                acc = pad_ref[r0:r0 + RB, cols]
                for k in range(1, win):
                    acc = acc + pad_ref[r0 + k:r0 + k + RB, cols]
                own = pad_ref[r0:r0 + RB, cols]
                if r0 == 0:
                    own = jnp.concatenate([head, own[POOL_PAD:]], axis=0)
                out_ref[r0:r0 + RB, cols] = (acc * (1.0 / win) - own).astype(BF16)

    return pl.pallas_call(
        body, name="pool_backward",
        out_shape=jax.ShapeDtypeStruct((S, POOL_W), BF16),
        in_specs=[ANY, ANY], out_specs=VMEM_SPEC,
        scratch_shapes=[pltpu.VMEM((S + POOL_PAD, POOL_W), F32), pltpu.SemaphoreType.DMA],
        compiler_params=_params(),
    )(dpooled, after)


def _mix_out(y_mla, pooled, w_pool, pool_scale, w_o, x, mod6, T):
    S = x.shape[0]

    def body(ym_ref, pl_ref, wp_ref, ps_ref, wo_ref, x_ref, mod_ref, x1_ref, mix_ref, mi_ref):
        mi_ref[:, 0:512] = ym_ref[...]
        for g in range(len(POOL_WINDOWS)):
            cols = slice(g * POOL_GROUP, (g + 1) * POOL_GROUP)
            z = _dot(pl_ref[:, cols], wp_ref[g])
            mi_ref[:, 512 + g * POOL_GROUP:512 + (g + 1) * POOL_GROUP] = (z * ps_ref[:, cols]).astype(BF16)
        mix = _dot(mi_ref[...], wo_ref[...])
        mix_ref[...] = mix.astype(BF16)
        x1_ref[...] = x_ref[...] + mod_ref[2:3, :] * mix

    row = lambda w: pl.BlockSpec((T, w), lambda i: (i, 0))
    return pl.pallas_call(
        body, name="mix_out", grid=(S // T,),
        out_shape=[pltpu.HBM((S, D_MODEL), F32), pltpu.HBM((S, D_MODEL), BF16), pltpu.HBM((S, 1024), BF16)],
        in_specs=[row(512), row(POOL_W), _full((4, POOL_GROUP, POOL_GROUP)), _full((1, POOL_W)),
                  _full((1024, D_MODEL)), row(D_MODEL), _full((N_MOD, D_MODEL))],
        out_specs=[row(D_MODEL), row(D_MODEL), row(1024)],
        compiler_params=_params(("parallel",)),
    )(*_hbm(y_mla, pooled, w_pool, pool_scale, w_o, x, mod6))


def _ffn_forward(x1, mod6, g_ffn, g_final, target, w_gate, w_up, w_down, T):
    S = x1.shape[0]

    def body(x1_ref, mod_ref, gf_ref, gl_ref, tgt_ref, wg_ref, wu_ref, wd_ref,
             gate_ref, up_ref, act_ref, h2_ref, dff_ref, dx2_ref, st_ref, acc_s):
        i, j = pl.program_id(0), pl.program_id(1)

        @pl.when(jnp.logical_and(i == 0, j == 0))
        def _():
            st_ref[...] = jnp.zeros_like(st_ref)

        @pl.when(j == 0)
        def _():
            xh, _ = _rms(x1_ref[...])
            h2_ref[...] = ((xh * gf_ref[...]) * (1.0 + mod_ref[4:5, :]) + mod_ref[3:4, :]).astype(BF16)
            acc_s[...] = jnp.zeros_like(acc_s)

        h2 = h2_ref[...]
        gate = _dot_nt(h2, wg_ref[j])
        up = _dot_nt(h2, wu_ref[j])
        gate_ref[...] = gate.astype(BF16)
        up_ref[...] = up.astype(BF16)
        act = (gate * jax.nn.sigmoid(gate) * up).astype(BF16)
        act_ref[...] = act
        acc_s[...] += _dot(act, wd_ref[j])

        @pl.when(j == N_CHIPS - 1)
        def _():
            ff = acc_s[...]
            x2 = x1_ref[...] + mod_ref[5:6, :] * ff
            xh, r3 = _rms(x2)
            err = xh * gl_ref[...] - tgt_ref[...]
            dy = err * (1.0 / D_MODEL)
            dx2 = _rms_bwd(dy * gl_ref[...], xh, r3)
            dx2_ref[...] = dx2
            dff_ref[...] = (dx2 * mod_ref[5:6, :]).astype(BF16)
            st_ref[0:1, :] += jnp.sum(dy * xh, axis=0, keepdims=True)
            st_ref[1:2, :] += jnp.sum(dx2 * ff, axis=0, keepdims=True)
            st_ref[2:3, :] += 0.5 * jnp.sum(err * dy)

    row = pl.BlockSpec((T, D_MODEL), lambda i, j: (i, 0))
    chunk_out = pl.BlockSpec((None, T, FF_CHUNK), lambda i, j: (j, i, 0))
    big = pltpu.HBM((N_CHIPS, S, FF_CHUNK), BF16)
    wide = pltpu.HBM((S, D_MODEL), BF16)
    return pl.pallas_call(
        body, name="ffn_forward", grid=(S // T, N_CHIPS),
        out_shape=[big, big, big, wide, wide, pltpu.HBM((S, D_MODEL), F32), jax.ShapeDtypeStruct((8, D_MODEL), F32)],
        in_specs=[row, _full((N_MOD, D_MODEL)), _full((1, D_MODEL)), _full((1, D_MODEL)), row,
                  VMEM_SPEC, VMEM_SPEC, VMEM_SPEC],
        out_specs=[chunk_out, chunk_out, chunk_out, row, row, row, _full((8, D_MODEL))],
        scratch_shapes=[pltpu.VMEM((T, D_MODEL), F32)],
        compiler_params=_params(("arbitrary", "arbitrary")),
    )(*_hbm(x1, mod6, g_ffn, g_final, target), w_gate, w_up, w_down)


def _ffn_backward(dx2, x1, dff, gate, up, mod6, g_ffn, w_gate, w_up, w_down, T):
    S = x1.shape[0]

    def body(dx2_ref, x1_ref, dff_ref, gate_ref, up_ref, mod_ref, gf_ref, wg_ref, wu_ref, wd_ref,
             dgate_ref, dup_ref, dx1_ref, st_ref, acc_s):
        i, j = pl.program_id(0), pl.program_id(1)

        @pl.when(jnp.logical_and(i == 0, j == 0))
        def _():
            st_ref[...] = jnp.zeros_like(st_ref)

        @pl.when(j == 0)
        def _():
            acc_s[...] = jnp.zeros_like(acc_s)

        for r0 in range(0, T, T // 2):
            rows = slice(r0, r0 + T // 2)
            gate, up = gate_ref[rows, :].astype(F32), up_ref[rows, :].astype(F32)
            sg = jax.nn.sigmoid(gate)
            dact = _dot_nt(dff_ref[rows, :], wd_ref[j])
            dup = (dact * (gate * sg)).astype(BF16)
            dgate = (dact * up * (sg * (1.0 + gate * (1.0 - sg)))).astype(BF16)
            dup_ref[rows, :] = dup
            dgate_ref[rows, :] = dgate
            acc_s[rows, :] += _dot(dgate, wg_ref[j]) + _dot(dup, wu_ref[j])

        @pl.when(j == N_CHIPS - 1)
        def _():
            dh2 = acc_s[...]
            xh, r2 = _rms(x1_ref[...])
            n2 = xh * gf_ref[...]
            st_ref[0:1, :] += jnp.sum(dh2, axis=0, keepdims=True)
            st_ref[1:2, :] += jnp.sum(dh2 * n2, axis=0, keepdims=True)
            dn2 = dh2 * (1.0 + mod_ref[4:5, :])
            st_ref[2:3, :] += jnp.sum(dn2 * xh, axis=0, keepdims=True)
            dx1_ref[...] = _rms_bwd(dn2 * gf_ref[...], xh, r2) + dx2_ref[...]

    row = pl.BlockSpec((T, D_MODEL), lambda i, j: (i, 0))
    chunk = pl.BlockSpec((None, T, FF_CHUNK), lambda i, j: (j, i, 0))
    big = pltpu.HBM((N_CHIPS, S, FF_CHUNK), BF16)
    return pl.pallas_call(
        body, name="ffn_backward", grid=(S // T, N_CHIPS),
        out_shape=[big, big, pltpu.HBM((S, D_MODEL), F32), jax.ShapeDtypeStruct((8, D_MODEL), F32)],
        in_specs=[row, row, row, chunk, chunk, _full((N_MOD, D_MODEL)), _full((1, D_MODEL)),
                  VMEM_SPEC, VMEM_SPEC, VMEM_SPEC],
        out_specs=[chunk, chunk, row, _full((8, D_MODEL))],
        scratch_shapes=[pltpu.VMEM((T, D_MODEL), F32)],
        compiler_params=_params(("arbitrary", "arbitrary")),
    )(*_hbm(dx2, x1, dff, gate, up, mod6, g_ffn), w_gate, w_up, w_down)


def _tn_matmul(a, b, a_spec, b_spec, groups, m, n, steps, name):
    def body(a_ref, b_ref, o_ref):
        @pl.when(pl.program_id(1) == 0)
        def _():
            o_ref[...] = jnp.zeros_like(o_ref)

        o_ref[...] += _dot_tn(a_ref[...], b_ref[...])

    return pl.pallas_call(
        body, name=name, grid=(groups, steps),
        out_shape=pltpu.HBM((groups, m, n), F32),
        in_specs=[a_spec, b_spec],
        out_specs=pl.BlockSpec((None, m, n), lambda g, i: (g, 0, 0)),
        compiler_params=_params(("parallel", "arbitrary")),
    )(*_hbm(a, b))


def _mix_backward(dx1, mix, mod6, w_o, pooled, w_pool, pool_scale, w_uv_t, o_lat, T, TQ):
    S = dx1.shape[0]

    def body(dx1_ref, mix_ref, mod_ref, wo_ref, pl_ref, wp_ref, ps_ref, wuv_ref, o_ref,
             dmix_ref, dp_ref, do_ref, dr_ref, gp_ref, guv_ref, st_ref):
        @pl.when(pl.program_id(0) == 0)
        def _():
            st_ref[...] = jnp.zeros_like(st_ref)
            gp_ref[...] = jnp.zeros_like(gp_ref)
            guv_ref[...] = jnp.zeros_like(guv_ref)

        dx1 = dx1_ref[...]
        st_ref[0:1, :] += jnp.sum(dx1 * mix_ref[...].astype(F32), axis=0, keepdims=True)
        dmix = (dx1 * mod_ref[2:3, :]).astype(BF16)
        dmix_ref[...] = dmix
        dmi = _dot_nt(dmix, wo_ref[...])
        dym = dmi[:, 0:512].astype(BF16)
        for g in range(len(POOL_WINDOWS)):
            cols = slice(g * POOL_GROUP, (g + 1) * POOL_GROUP)
            dyp = dmi[:, 512 + g * POOL_GROUP:512 + (g + 1) * POOL_GROUP]
            pooled_g = pl_ref[:, cols]
            z = _dot(pooled_g, wp_ref[g])
            st_ref[1:2, cols] += jnp.sum(dyp * z, axis=0, keepdims=True)
            dz = (dyp * ps_ref[:, cols]).astype(BF16)
            gp_ref[g] += _dot_tn(pooled_g, dz)
            dp_ref[:, cols] = _dot_nt(dz, wp_ref[g])
        for h in range(HEADS):
            dym_h = dym[:, h * 128:(h + 1) * 128]
            do = _dot_nt(dym_h, wuv_ref[h]).astype(BF16)
            do_ref[h] = do
            o_h = o_ref[h]
            guv_ref[h] += _dot_tn(o_h, dym_h)
            delta = _col_to_row(jnp.sum(do.astype(F32) * o_h.astype(F32), axis=1, keepdims=True))
            for s in range(T // TQ):
                dr_ref[s, :, h * TQ:(h + 1) * TQ] = delta[:, s * TQ:(s + 1) * TQ]

    row = lambda w: pl.BlockSpec((T, w), lambda i: (i, 0))
    heads = pl.BlockSpec((HEADS, T, KV_LORA), lambda i: (0, i, 0))
    square = jax.ShapeDtypeStruct((4, 128, 128), F32)
    return pl.pallas_call(
        body, name="mix_backward", grid=(S // T,),
        out_shape=[pltpu.HBM((S, D_MODEL), BF16), pltpu.HBM((S, POOL_W), F32), pltpu.HBM((HEADS, S, KV_LORA), BF16),
                   pltpu.HBM((S // TQ, 1, HEADS * TQ), F32), square, square, jax.ShapeDtypeStruct((8, D_MODEL), F32)],
        in_specs=[row(D_MODEL), row(D_MODEL), _full((N_MOD, D_MODEL)), _full((1024, D_MODEL)), row(POOL_W),
                  _full((4, POOL_GROUP, POOL_GROUP)), _full((1, POOL_W)), _full((HEADS, KV_LORA, 128)), heads],
        out_specs=[row(D_MODEL), row(POOL_W), heads,
                   pl.BlockSpec((T // TQ, 1, HEADS * TQ), lambda i: (i, 0, 0)), _full((4, 128, 128)),
                   _full((4, 128, 128)), _full((8, D_MODEL))],
        compiler_params=_params(("arbitrary",)),
    )(*_hbm(dx1, mix, mod6, w_o, pooled, w_pool, pool_scale, w_uv_t, o_lat))


def _attention_bwd(qc, kc, kct, do, lse_rows, delta_rows, TQ):
    S = kc.shape[0]
    R = HEADS * TQ
    nq = S // TQ

    def body(q_ref, do_ref, lser_ref, dr_ref, k_ref, kt_ref, dqt_ref, dk_ref, dqt_s, dv_s):
        i = pl.program_id(0)

        def key_rows(j):
            return pl.ds(pl.multiple_of(j * TQ, TQ), TQ)

        @pl.when(i == 0)
        def _():
            def zero(j, carry):
                dk_ref[key_rows(j), :] = jnp.zeros((TQ, QK_PAD), F32)
                dv_s[key_rows(j), :] = jnp.zeros((TQ, KV_LORA), F32)
                return carry
            lax.fori_loop(0, nq, zero, 0)

        q = q_ref[...].reshape(R, QK_PAD)
        do = do_ref[...].reshape(R, KV_LORA)
        lse, delta = lser_ref[0], dr_ref[0]
        dqt_s[...] = jnp.zeros((QK_PAD, R), F32)

        def step(j, masked):
            rows = key_rows(j)
            k = k_ref[rows, :]
            st = _dot_nt(k, q) * SM_SCALE
            if masked:
                st = jnp.where(_diag_mask(TQ, R), st, -jnp.inf)
            pt = jnp.exp(st - lse)
            dv_s[rows, :] += _dot(pt, do)
            dpt = _dot_nt(k[:, :KV_LORA], do)
            dst = (pt * (dpt - delta)).astype(BF16)
            dk_ref[rows, :] += _dot(dst, q)
            dqt_s[...] += _dot(kt_ref[j], dst)

        def loop(j, carry):
            step(j, False)
            return carry

        lax.fori_loop(0, i, loop, 0)
        step(i, True)
        dqt_ref[...] = dqt_s[...]

        @pl.when(i == nq - 1)
        def _():
            def finish(j, carry):
                rows = key_rows(j)
                dk = dk_ref[rows, :] * SM_SCALE
                dk_ref[rows, 0:KV_LORA] = dk[:, 0:KV_LORA] + dv_s[rows, :]
                dk_ref[rows, KV_LORA:QK_PAD] = dk[:, KV_LORA:QK_PAD]
                return carry
            lax.fori_loop(0, nq, finish, 0)

    tile = lambda w: pl.BlockSpec((HEADS, TQ, w), lambda i: (0, i, 0))
    row = pl.BlockSpec((1, 1, R), lambda i: (i, 0, 0))
    return pl.pallas_call(
        body, name="attention_bwd", grid=(nq,),
        out_shape=[pltpu.HBM((nq, QK_PAD, R), F32), jax.ShapeDtypeStruct((S, QK_PAD), F32)],
        in_specs=[tile(QK_PAD), tile(KV_LORA), row, row, VMEM_SPEC, VMEM_SPEC],
        out_specs=[pl.BlockSpec((None, QK_PAD, R), lambda i: (i, 0, 0)), VMEM_SPEC],
        scratch_shapes=[pltpu.VMEM((QK_PAD, R), F32), pltpu.VMEM((S, KV_LORA), F32)],
        compiler_params=_params(("arbitrary",)),
    )(*_hbm(qc, do, lse_rows, delta_rows), kc, kct)[::-1]


def _pre_attention_backward(x, dx1, proj, q, dqt, dkc, du, cos, sin, mod6, g_mix, g_q, g_kv, w_in, w_uq, w_uk_t, T, TQ):
    S = x.shape[0]

    def body(x_ref, dx1_ref, proj_ref, q_ref, dqt_ref, dkc_ref, du_ref, cos_ref, sin_ref, mod_ref, gm_ref, gq_ref,
             gkv_ref, win_ref, wuq_ref, wuk_ref, gx_ref, dproj_ref, h1_ref, guk_ref, guq_ref, st_ref, dq_ref):
        @pl.when(pl.program_id(0) == 0)
        def _():
            st_ref[...] = jnp.zeros_like(st_ref)
            guk_ref[...] = jnp.zeros_like(guk_ref)
            guq_ref[...] = jnp.zeros_like(guq_ref)

        cos_t, sin_t = cos_ref[...], sin_ref[...]
        low = lax.broadcasted_iota(jnp.int32, (T, 128), 1) < ROPE
        rope_parts = []
        for h in range(HEADS):
            dqc = jnp.concatenate([jnp.transpose(dqt_ref[s, :, h * TQ:(h + 1) * TQ]) for s in range(T // TQ)], axis=0)
            dqc = dqc * SM_SCALE
            dql = dqc[:, 0:KV_LORA].astype(BF16)
            guk_ref[h] += _dot_tn(dql, q_ref[:, h * NOPE:(h + 1) * NOPE])
            dq_ref[:, h * NOPE:(h + 1) * NOPE] = _dot(dql, wuk_ref[h]).astype(BF16)
            rope_parts.append(dqc[:, KV_LORA:QK_PAD])
        for pair in range(2):
            d = jnp.where(low, rope_parts[2 * pair], rope_parts[2 * pair + 1])
            dq_ref[:, O_QA + 128 * pair:O_QA + 128 * (pair + 1)] = _rope_bwd(d, cos_t, sin_t).astype(BF16)
        dq = dq_ref[...]
        dcq = _dot_nt(dq, wuq_ref[...])
        cqh, rq = _rms(proj_ref[:, 0:Q_LORA])
        guq_ref[...] += _dot_tn(cqh * gq_ref[...], dq)
        st_ref[3:4, 0:Q_LORA] += jnp.sum(dcq * cqh, axis=0, keepdims=True)
        dproj_ref[:, 0:Q_LORA] = _rms_bwd(dcq * gq_ref[...], cqh, rq).astype(BF16)
        dckv = dkc_ref[:, 0:KV_LORA]
        ckvh, rkv = _rms(proj_ref[:, O_CKV:O_KR])
        st_ref[4:5, 0:KV_LORA] += jnp.sum(dckv * ckvh, axis=0, keepdims=True)
        dproj_ref[:, O_CKV:O_KR] = _rms_bwd(dckv * gkv_ref[...], ckvh, rkv).astype(BF16)
        dkr = _rope_bwd(dkc_ref[:, KV_LORA:QK_PAD], cos_t, sin_t)
        dkr = jnp.where(low, dkr + pltpu.roll(dkr, ROPE, 1), 0.0)
        dproj_ref[:, O_KR:O_U] = dkr.astype(BF16)
        dproj_ref[:, O_U:PROJ_W] = du_ref[...].astype(BF16)
        dproj = dproj_ref[...]
        dh1 = jnp.concatenate([_dot(dproj, win_ref[j]) for j in range(N_CHIPS)], axis=1)
        xh, r1 = _rms(x_ref[...])
        n1 = xh * gm_ref[...]
        h1_ref[...] = (n1 * (1.0 + mod_ref[1:2, :]) + mod_ref[0:1, :]).astype(BF16)
        st_ref[0:1, :] += jnp.sum(dh1, axis=0, keepdims=True)
        st_ref[1:2, :] += jnp.sum(dh1 * n1, axis=0, keepdims=True)
        dn1 = dh1 * (1.0 + mod_ref[1:2, :])
        st_ref[2:3, :] += jnp.sum(dn1 * xh, axis=0, keepdims=True)
        gx_ref[...] = _rms_bwd(dn1 * gm_ref[...], xh, r1) + dx1_ref[...]

    row = lambda w: pl.BlockSpec((T, w), lambda i: (i, 0))
    return pl.pallas_call(
        body, name="pre_attention_backward", grid=(S // T,),
        out_shape=[jax.ShapeDtypeStruct((S, D_MODEL), F32), pltpu.HBM((S, PROJ_W), BF16),
                   pltpu.HBM((S, D_MODEL), BF16), jax.ShapeDtypeStruct((HEADS, KV_LORA, NOPE), F32),
                   jax.ShapeDtypeStruct((Q_LORA, Q_W), F32), jax.ShapeDtypeStruct((8, D_MODEL), F32)],
        in_specs=[row(D_MODEL), row(D_MODEL), row(O_KR), row(HEADS * NOPE),
                  pl.BlockSpec((T // TQ, QK_PAD, HEADS * TQ), lambda i: (i, 0, 0)),
                  row(QK_PAD), row(POOL_W), row(128), row(128), _full((N_MOD, D_MODEL)), _full((1, D_MODEL)),
                  _full((1, Q_LORA)), _full((1, KV_LORA)), _full((N_CHIPS, PROJ_W, D_MODEL // N_CHIPS)),
                  _full((Q_LORA, Q_W)), _full((HEADS, KV_LORA, NOPE))],
        out_specs=[row(D_MODEL), row(PROJ_W), row(D_MODEL), _full((HEADS, KV_LORA, NOPE)), _full((Q_LORA, Q_W)),
                   _full((8, D_MODEL))],
        scratch_shapes=[pltpu.VMEM((T, Q_W), BF16)],
        compiler_params=_params(("arbitrary",)),
    )(*_hbm(x, dx1, proj, q, dqt, dkc, du, cos, sin, mod6, g_mix, g_q, g_kv, w_in, w_uq, w_uk_t))


def _ada_grads(c_all, dmod_all, chip):
    cols = N_MOD * D_MODEL // N_CHIPS
    width = dmod_all.shape[1]

    def body(col_ref, c_ref, dcol_ref, dall_ref, gw_ref, gb_ref):
        call = c_ref[...]
        act = call * jax.nn.sigmoid(call)
        gw_ref[...] = _dot_tn(act, dcol_ref[...])
        d = dall_ref[...]
        acc = d[0:1, :]
        for b in range(1, 8):
            acc = acc + d[b:b + 1, :]
        gb_ref[...] = acc

    return pl.pallas_call(
        body, name="ada_grads",
        out_shape=[jax.ShapeDtypeStruct((D_MODEL, cols), F32), jax.ShapeDtypeStruct((1, width), F32)],
        grid_spec=pltpu.PrefetchScalarGridSpec(
            num_scalar_prefetch=1, grid=(1,),
            in_specs=[pl.BlockSpec((8, D_MODEL), lambda s, col_ref: (0, 0)),
                      pl.BlockSpec((8, cols), lambda s, col_ref: (0, col_ref[0])),
                      pl.BlockSpec((8, width), lambda s, col_ref: (0, 0))],
            out_specs=[pl.BlockSpec((D_MODEL, cols), lambda s, col_ref: (0, 0)),
                       pl.BlockSpec((1, width), lambda s, col_ref: (0, 0))]),
        compiler_params=_params(("arbitrary",)),
    )(chip, *_hbm(c_all, dmod_all, dmod_all))


def _adamw(w, g, m, v, name, g_is_landing_zone=True):
    rows, rest = w.shape[0], w.shape[1:]
    T = _row_tile(rows, 256)

    def body(w_ref, g_ref, m_ref, v_ref, *outs):
        d_ref, nm_ref, nv_ref = outs[-3:]
        g = g_ref[...]
        if g_is_landing_zone:
            outs[0][...] = g
        m2 = ADAM_B1 * m_ref[...] + (1.0 - ADAM_B1) * g
        v2 = ADAM_B2 * v_ref[...] + (1.0 - ADAM_B2) * (g * g)
        m_hat = m2 / (1.0 - ADAM_B1 ** ADAM_STEP)
        v_hat = v2 / (1.0 - ADAM_B2 ** ADAM_STEP)
        d_ref[...] = -ADAM_LR * (m_hat / (jnp.sqrt(v_hat) + ADAM_EPS) + ADAM_WD * w_ref[...])
        nm_ref[...] = m2
        nv_ref[...] = v2

    zeros = (0,) * len(rest)
    spec = pl.BlockSpec((T,) + rest, lambda i: (i,) + zeros)
    n_out = 4 if g_is_landing_zone else 3
    res = pl.pallas_call(
        body, name=name, grid=(rows // T,),
        out_shape=[jax.ShapeDtypeStruct(w.shape, F32)] * n_out,
        in_specs=[spec] * 4, out_specs=[spec] * n_out,
        compiler_params=_params(("parallel",)),
    )(*_hbm(w, g, m, v))
    return res if g_is_landing_zone else [g] + list(res)


SMALL_NAMES = ("w_uk", "w_uv", "w_pool", "g_mix", "g_q", "g_kv", "pool_scale", "g_ffn", "g_final", "b_ada")
SMALL_ROWS = 1664


def _pack_rows(parts):
    flat = jnp.concatenate([p.reshape(-1) for p in parts])
    pad = (-flat.shape[0]) % 128
    if pad:
        flat = jnp.concatenate([flat, jnp.zeros((pad,), F32)])
    return flat.reshape(-1, 128)


def kernel(x, c, positions, w_ada, b_ada, g_mix, w_in, g_q, g_kv, w_uq, w_uk, w_uv, w_pool, pool_scale, w_o, g_ffn, w_gate, w_up, w_down, g_final, loss_target, m_w_ada, m_b_ada, m_g_mix, m_w_in, m_g_q, m_g_kv, m_w_uq, m_w_uk, m_w_uv, m_w_pool, m_pool_scale, m_w_o, m_g_ffn, m_w_gate, m_w_up, m_w_down, m_g_final, v_w_ada, v_b_ada, v_g_mix, v_w_in, v_g_q, v_g_kv, v_w_uq, v_w_uk, v_w_uv, v_w_pool, v_pool_scale, v_w_o, v_g_ffn, v_w_gate, v_w_up, v_w_down, v_g_final):
    S = x.shape[1]
    T = _row_tile(S, 512)
    TQ = _row_tile(S, 512)
    TW = _row_tile(S, 4096)
    ix, iy, ic = lax.axis_index("x"), lax.axis_index("y"), lax.axis_index("c")
    chip = (2 * ix + iy).astype(jnp.int32)
    chip_arr = chip.reshape(1)
    core_arr = ic.astype(jnp.int32).reshape(1)

    xs, tgt = x[0], loss_target[0]

    tr = lambda a: jnp.transpose(a[0])
    win_t = tr(w_in)
    win_p = jnp.concatenate([win_t[:O_KR + ROPE], win_t[O_KR:O_KR + ROPE], win_t[O_KR + ROPE:]], axis=0).astype(BF16)
    wuq = w_uq[0]
    wuq_p = jnp.concatenate([wuq[:, h, :NOPE] for h in range(HEADS)] + [wuq[:, h, NOPE:] for h in range(HEADS)],
                            axis=1).astype(BF16)
    w_uk_t = jnp.transpose(w_uk[0], (1, 0, 2)).astype(BF16)
    w_uv_t = jnp.transpose(w_uv[0], (1, 0, 2)).astype(BF16)
    w_pool_b = w_pool[0].astype(BF16)
    first = [win_p, wuq_p]
    later = [w_o[0].astype(BF16), tr(w_gate).astype(BF16), tr(w_up).astype(BF16), w_down[0].astype(BF16)]
    placed = _place_shards(chip_arr, first + later)
    a_send, a_recv, a_lands, token = _split_start("first_weights_start", first, placed[:2], 6, _plan_gather_start)
    half = ROPE // 2
    freqs = jnp.power(ROPE_THETA, -jnp.arange(half, dtype=F32) / half)
    cos, sin = _rope_tables(positions.reshape(S, 1), jnp.tile(freqs, 4).reshape(1, 128) + token[0, 0])
    a_send, a_recv, a_lands, token = _split_relay(
        "first_weights_relay", a_send, a_recv, first, a_lands, cos, 6, _plan_gather_landed, _plan_gather_relay)

    ada_cols = w_ada.shape[2]
    b_cols = lax.dynamic_slice(b_ada, (0, chip * ada_cols), (1, ada_cols))
    mod, c_all = _mod_exchange(c, w_ada[0], b_cols + token[0, 0])
    mod6 = mod.reshape(N_MOD, D_MODEL)
    a_lands = _split_wait("first_weights_wait", a_send, a_recv, [], a_lands, mod, _plan_gather_wait)
    w_in_f = a_lands[0]
    w_uq_f = a_lands[1].reshape(Q_LORA, Q_W)
    wg_lands, mod6, w_in_f = lax.optimization_barrier((placed[2:], mod6, w_in_f))
    wg_send, wg_recv, wg_lands, token = _split_start(
        "weights_start", later, wg_lands, 3 * len(later), _plan_gather_start)
    mod6 = mod6 + token[0, 0]

    proj, q, qc, kc, kct = _pre_attention(xs, mod6, g_mix, g_q, g_kv, w_in_f, w_uq_f, w_uk_t, cos, sin,
                                          _row_tile(S, 1024), TQ)
    o_lat, y_mla, lse_rows = _attention_fwd(qc, kc, kct, w_uv_t, TQ)
    wg_send, wg_recv, wg_lands, token = _split_relay(
        "weights_relay", wg_send, wg_recv, later, wg_lands, y_mla, 3 * len(later), _plan_gather_landed,
        _plan_gather_relay)
    pooled = _pool_forward(proj)
    wg_lands = _split_wait("weights_wait", wg_send, wg_recv, [], wg_lands, pooled, _plan_gather_wait)
    w_o_f = wg_lands[0].reshape(1024, D_MODEL)
    w_gate_f, w_up_f, w_down_f = wg_lands[1], wg_lands[2], wg_lands[3]
    x1, mix, mix_in = _mix_out(y_mla, pooled, w_pool_b, pool_scale, w_o_f, xs, mod6, _row_tile(S, 1024))
    gate, up, act, h2, dff, dx2, st_f = _ffn_forward(
        x1, mod6, g_ffn, g_final.reshape(1, D_MODEL), tgt, w_gate_f, w_up_f, w_down_f, T)

    dgate, dup, dx1, st_b = _ffn_backward(dx2, x1, dff, gate, up, mod6, g_ffn, w_gate_f, w_up_f, w_down_f, T)
    steps = S // TW
    chunk_spec = pl.BlockSpec((None, TW, FF_CHUNK), lambda g, i: (g, i, 0))
    wide_spec = pl.BlockSpec((TW, D_MODEL), lambda g, i: (i, 0))
    g_down = _tn_matmul(act, dff, chunk_spec, wide_spec, N_CHIPS, FF_CHUNK, D_MODEL, steps, "grad_w_down")
    g_gate = _tn_matmul(dgate, h2, chunk_spec, wide_spec, N_CHIPS, FF_CHUNK, D_MODEL, steps, "grad_w_gate")
    g_up = _tn_matmul(dup, h2, chunk_spec, wide_spec, N_CHIPS, FF_CHUNK, D_MODEL, steps, "grad_w_up")

    half_shapes = lambda gs: [jax.ShapeDtypeStruct((N_CHIPS, g.shape[1] // 2, g.shape[2]), F32) for g in gs]
    ffn_grads = [g_gate, g_up, g_down]
    f_send, f_recv, f_lands, token = _split_start(
        "ffn_swap_start", ffn_grads, half_shapes(ffn_grads), len(ffn_grads), _plan_swap_start)
    dmix, dpooled, do_lat, delta_rows, g_pool, g_uv_t, st_m = _mix_backward(
        dx1, mix, mod6 + token[0, 0], w_o_f, pooled, w_pool_b, pool_scale, w_uv_t, o_lat, T, TQ)
    g_o = [_tn_matmul(mix_in, dmix, wide_spec, wide_spec, 1, 1024, D_MODEL, steps, "grad_w_o").reshape(N_CHIPS, -1, D_MODEL)]
    o_send, o_recv, o_lands, token = _split_start("w_o_swap_start", g_o, half_shapes(g_o), 1, _plan_swap_start)
    du = _pool_backward(dpooled, token)
    f_got = _split_wait("ffn_swap_wait", f_send, f_recv, ffn_grads, f_lands, du, _plan_swap_wait)
    f_got += _split_wait("w_o_swap_wait", o_send, o_recv, g_o, o_lands, du, _plan_swap_wait)
    far_grads = ffn_grads + g_o
    f_sums = _add_my_halves(core_arr, far_grads, f_got, "add_half_far")
    f_send, f_recv, f_lands, token = _split_start(
        "far_exchange_start", f_sums, [jax.ShapeDtypeStruct((3,) + s.shape[1:], F32) for s in f_sums],
        3 * len(f_sums), _plan_exchange_start)
    delta_rows = delta_rows + token[0, 0]
    dkc, dqt = _attention_bwd(qc, kc, kct, do_lat, lse_rows, delta_rows, TQ)
    grad_x, dproj, h1, g_uk_t, uq, st_p = _pre_attention_backward(
        xs, dx1, proj, q, dqt, dkc, du, cos, sin, mod6, g_mix, g_q, g_kv, w_in_f, w_uq_f, w_uk_t, T, TQ)
    rows_in = D_MODEL // N_CHIPS
    g_in_p = _tn_matmul(dproj, h1, pl.BlockSpec((TW, PROJ_W), lambda g, i: (i, 0)),
                        pl.BlockSpec((TW, rows_in), lambda g, i: (i, g)), N_CHIPS, PROJ_W, rows_in, steps, "grad_w_in")

    g_in = jnp.concatenate([g_in_p[:, :O_KR + ROPE], g_in_p[:, O_U:]], axis=1)
    g_uq = jnp.concatenate([jnp.concatenate([uq[:, h * NOPE:(h + 1) * NOPE], uq[:, O_QA + h * ROPE:O_QA + (h + 1) * ROPE]],
                                            axis=1) for h in range(HEADS)], axis=1).reshape(N_CHIPS, -1, HEADS * HEAD_QK)
    small = _pack_rows([g_uk_t, g_uv_t, g_pool, st_p[2], st_p[3, :Q_LORA], st_p[4, :KV_LORA], st_m[1, :POOL_W],
                        st_b[2], st_f[0]])
    small = jnp.concatenate([small, jnp.zeros((SMALL_ROWS - small.shape[0], 128), F32)]).reshape(N_CHIPS, -1, 128)
    grads = [g_in, g_uq, small]
    dmod = jnp.concatenate([jnp.stack([st_p[0], st_p[1], st_m[0], st_b[0], st_b[1], st_f[1]]).reshape(48, 128),
                            jnp.zeros((8, 128), F32).at[0, 0].set(st_f[2, 0])])

    got, dmod_all = _grad_swap_halves(grads, dmod)
    chip_sums = _add_my_halves(core_arr, grads, got, "add_half_near")
    n_send, n_recv, n_lands, token = _split_start(
        "near_exchange_start", chip_sums, [jax.ShapeDtypeStruct((3,) + s.shape[1:], F32) for s in chip_sums],
        3 * len(chip_sums), _plan_exchange_start)

    f_others = _split_wait("far_exchange_wait", f_send, f_recv, f_sums, f_lands, token, _plan_exchange_wait)
    chip_core = jnp.concatenate([chip_arr, core_arr])
    f_pairs = (_add_chips_into_pairs(chip_core, f_sums[:2], f_others[:2], "add_chips_gate_up")
               + _add_chips_into_pairs(chip_core, f_sums[2:], f_others[2:], "add_chips_down_o"))
    f_send, f_recv, f_pairs, token = _split_start("far_finish_start", [], f_pairs, len(f_pairs), _plan_finish_start)
    gw_ada, gb_ada = _ada_grads(c_all, dmod_all.reshape(8, -1) + token[0, 0], chip_arr)
    loss = gb_ada[0, N_MOD * D_MODEL]
    gb_ada = gb_ada[:, :N_MOD * D_MODEL]
    f_fulls = _split_wait("far_finish_wait", f_send, f_recv, [], f_pairs, gw_ada, _plan_finish_wait)
    gw_gate, gw_up, gw_down, gw_o = [f.reshape(-1, f.shape[2]) for f in f_fulls]

    untr = lambda a: jnp.transpose(a)[None]
    grad_out, delta_out, newm_out, newv_out = {}, {}, {}, {}

    def adam_sharded(n, w, g2, m, v, transposed, landed=True):
        view = (lambda a: jnp.transpose(a[0])) if transposed else (lambda a: a[0])
        back = untr if transposed else (lambda a: a[None])
        g_, d_, m_, v_ = _adamw(view(w), g2.reshape(view(w).shape), view(m), view(v), "adamw_" + n, landed)
        grad_out[n], delta_out[n], newm_out[n], newv_out[n] = back(g_), back(d_), back(m_), back(v_)
        return d_

    done = [adam_sharded("w_gate", w_gate, gw_gate, m_w_gate, v_w_gate, True),
            adam_sharded("w_up", w_up, gw_up, m_w_up, v_w_up, True),
            adam_sharded("w_down", w_down, gw_down, m_w_down, v_w_down, False),
            adam_sharded("w_o", w_o, gw_o, m_w_o, v_w_o, False)]
    after_all = jnp.stack([d[0, 0] for d in done])

    others = _split_wait("near_exchange_wait", n_send, n_recv, chip_sums, n_lands, after_all, _plan_exchange_wait)
    n_pairs = _add_chips_into_pairs(chip_core, chip_sums[:2], others[:2], "add_chips_in_uq")
    small_grid = _add_chips_into_grid(chip_core, chip_sums[2], others[2], "add_chips_small")
    n_send, n_recv, n_lands, token = _split_start(
        "near_finish_start", [], n_pairs + [small_grid], 2 + len(RELATIONS), _plan_near_finish_start)
    gw_ada, _ = lax.optimization_barrier((gw_ada, token))
    d_ada = adam_sharded("w_ada", w_ada, gw_ada, m_w_ada, v_w_ada, False, landed=False)
    n_lands = _split_wait("near_finish_wait", n_send, n_recv, [], n_lands, d_ada, _plan_near_finish_wait)
    gw_in, gw_uq = [f.reshape(-1, f.shape[2]) for f in n_lands[:2]]
    small_all = n_lands[2].reshape(SMALL_ROWS * 128)
    adam_sharded("w_in", w_in, gw_in, m_w_in, v_w_in, True)
    adam_sharded("w_uq", w_uq, gw_uq, m_w_uq, v_w_uq, False)

    n_sq = KV_LORA * HEADS * 128
    sizes = [n_sq, n_sq, n_sq, D_MODEL, Q_LORA, KV_LORA, POOL_W, D_MODEL, D_MODEL]
    offs = [0]
    for s_ in sizes:
        offs.append(offs[-1] + s_)
    piece = lambda k: small_all[offs[k]:offs[k + 1]]
    grads_small = {
        "w_uk": jnp.transpose(piece(0).reshape(HEADS, KV_LORA, NOPE), (1, 0, 2)),
        "w_uv": jnp.transpose(piece(1).reshape(HEADS, KV_LORA, 128), (1, 0, 2)),
        "w_pool": piece(2).reshape(4, POOL_GROUP, POOL_GROUP),
        "g_mix": piece(3), "g_q": piece(4), "g_kv": piece(5), "pool_scale": piece(6), "g_ffn": piece(7),
        "g_final": piece(8), "b_ada": gb_ada.reshape(-1),
    }
    weights_small = {"w_uk": w_uk, "w_uv": w_uv, "w_pool": w_pool, "g_mix": g_mix, "g_q": g_q, "g_kv": g_kv,
                     "pool_scale": pool_scale, "g_ffn": g_ffn, "g_final": g_final, "b_ada": b_ada}
    m_small = {"w_uk": m_w_uk, "w_uv": m_w_uv, "w_pool": m_w_pool, "g_mix": m_g_mix, "g_q": m_g_q, "g_kv": m_g_kv,
               "pool_scale": m_pool_scale, "g_ffn": m_g_ffn, "g_final": m_g_final, "b_ada": m_b_ada}
    v_small = {"w_uk": v_w_uk, "w_uv": v_w_uv, "w_pool": v_w_pool, "g_mix": v_g_mix, "g_q": v_g_q, "g_kv": v_g_kv,
               "pool_scale": v_pool_scale, "g_ffn": v_g_ffn, "g_final": v_g_final, "b_ada": v_b_ada}
    pack = lambda d: _pack_rows([d[n] for n in SMALL_NAMES])
    _, d_s, m_s, v_s = _adamw(pack(weights_small), pack(grads_small), pack(m_small), pack(v_small), "adamw_small",
                              g_is_landing_zone=False)

    def unpack(flat2d):
        flat = flat2d.reshape(-1)
        out, o = {}, 0
        for n in SMALL_NAMES:
            size = weights_small[n].size
            out[n] = flat[o:o + size].reshape(weights_small[n].shape)
            o += size
        return out

    delta_s, newm_s, newv_s = unpack(d_s), unpack(m_s), unpack(v_s)

    for n in SMALL_NAMES:
        grad_out[n] = grads_small[n].reshape(weights_small[n].shape)
        delta_out[n], newm_out[n], newv_out[n] = delta_s[n], newm_s[n], newv_s[n]

    order = ("w_ada", "b_ada", "g_mix", "w_in", "g_q", "g_kv", "w_uq", "w_uk", "w_uv", "w_pool", "pool_scale", "w_o",
             "g_ffn", "w_gate", "w_up", "w_down", "g_final")
    return (loss, grad_x.reshape(x.shape), *[grad_out[n] for n in order], *[delta_out[n] for n in order],
            *[newm_out[n] for n in order], *[newv_out[n] for n in order])
```

```python
import functools

import jax
import jax.numpy as jnp
from jax import lax
from jax.experimental import pallas as pl
from jax.experimental.pallas import tpu as pltpu

F32 = jnp.float32
BF16 = jnp.bfloat16

D_MODEL = 1024
HEADS = 4
NOPE = 128
ROPE = 64
HEAD_QK = NOPE + ROPE
Q_LORA = 256
KV_LORA = 128
POOL_W = 512
POOL_WINDOWS = (2, 4, 8, 16)
POOL_GROUP = 128
POOL_PAD = 16
D_FF = 2816
N_CHIPS = 4
FF_CHUNK = D_FF // N_CHIPS
N_MOD = 6
EPS = 1e-6
SM_SCALE = HEAD_QK ** -0.5
ROPE_THETA = 10000.0
QK_PAD = 256
CHUNK = 64
CHUNK_SHIFT = 6

ADAM_LR = 0.001
ADAM_B1 = 0.9
ADAM_B2 = 0.999
ADAM_EPS = 1e-08
ADAM_WD = 0.01
ADAM_STEP = 10

VMEM_LIMIT = 48 * 1024 * 1024
MESH = pl.DeviceIdType.MESH
ANY = pl.BlockSpec(memory_space=pl.ANY)
VMEM_SPEC = pl.BlockSpec(memory_space=pltpu.VMEM)

PROJ_W = 1024
O_CKV = 256
O_KR = 384
O_U = 512
Q_W = 768
O_QA = 512
O_QB = 640


def _params(sem=None, vmem=VMEM_LIMIT):
    kw = dict(vmem_limit_bytes=vmem)
    if sem is not None:
        kw["dimension_semantics"] = sem
    return pltpu.CompilerParams(**kw)


def _dot(a, b):
    return jnp.dot(a.astype(BF16), b.astype(BF16), preferred_element_type=F32)


def _dot_nt(a, b):
    return lax.dot_general(a.astype(BF16), b.astype(BF16), (((1,), (1,)), ((), ())), preferred_element_type=F32)


def _dot_tn(a, b):
    return lax.dot_general(a.astype(BF16), b.astype(BF16), (((0,), (0,)), ((), ())), preferred_element_type=F32)


def _row_tile(rows, target):
    best = rows
    for t in range(8, min(rows, target) + 1, 8):
        if rows % t == 0:
            best = t
    return best if rows % best == 0 and best <= target else rows


def _rms(x):
    r = lax.rsqrt(jnp.mean(x * x, axis=-1, keepdims=True) + EPS)
    return x * r, r


def _rms_bwd(dxh, xh, r):
    return r * (dxh - xh * jnp.mean(dxh * xh, axis=-1, keepdims=True))


def _lane_first_half(shape):
    lane = lax.broadcasted_iota(jnp.int32, shape, 1)
    return (lane & (ROPE - 1)) < (ROPE // 2)


def _rope(a, cos, sin):
    first = _lane_first_half(a.shape)
    up = pltpu.roll(a, 96, 1)
    dn = pltpu.roll(a, 32, 1)
    return a * cos + jnp.where(first, -up, dn) * sin


def _rope_bwd(d, cos, sin):
    first = _lane_first_half(d.shape)
    up = pltpu.roll(d, 96, 1)
    dn = pltpu.roll(d, 32, 1)
    return d * cos + jnp.where(first, up, -dn) * sin


RELATIONS = tuple((dx, dy, dc) for dx in (0, 1) for dy in (0, 1) for dc in (0, 1) if (dx, dy, dc) != (0, 0, 0))
CHIP_RELATIONS = ((1, 0), (0, 1), (1, 1))


def _flip(v, d):
    return 1 - v if d else v


def _place():
    return lax.axis_index("x"), lax.axis_index("y"), lax.axis_index("c")


def _remote(src, dst, send_sem, recv_sem, target):
    return pltpu.make_async_remote_copy(src_ref=src, dst_ref=dst, send_sem=send_sem, recv_sem=recv_sem,
                                        device_id=target, device_id_type=MESH)


def _mod_exchange(c_row, w_ada, b_ada):
    cols = w_ada.shape[1]

    def body(c_ref, w_ref, b_ref, mod_ref, call_ref, part_ref, send1, recv1, loc1, send2, recv2, loc2):
        x, y, c = _place()
        me = 4 * x + 2 * y + c
        own = pltpu.make_async_copy(c_ref, call_ref.at[pl.ds(me, 1)], loc1)
        own.start()
        sends = []
        for k, (dx, dy, dc) in enumerate(RELATIONS):
            cp = _remote(c_ref, call_ref.at[pl.ds(me, 1)], send1.at[k], recv1.at[k],
                         (_flip(x, dx), _flip(y, dy), _flip(c, dc)))
            cp.start()
            sends.append(cp)
        for k, (dx, dy, dc) in enumerate(RELATIONS):
            src = 4 * _flip(x, dx) + 2 * _flip(y, dy) + _flip(c, dc)
            _remote(c_ref, call_ref.at[pl.ds(src, 1)], send1.at[k], recv1.at[k], (x, y, c)).wait_recv()
        own.wait()
        for cp in sends:
            cp.wait_send()
        call = call_ref[...]
        act = call * jax.nn.sigmoid(call)
        part_ref[...] = _dot(act, w_ref[...]) + b_ref[...]
        chip = 2 * x + y
        mine = pltpu.make_async_copy(part_ref.at[pl.ds(me, 1)], mod_ref.at[pl.ds(chip, 1)], loc2)
        mine.start()
        sends = []
        for k, (dx, dy) in enumerate(CHIP_RELATIONS):
            tx, ty = _flip(x, dx), _flip(y, dy)
            tb = 4 * tx + 2 * ty + c
            cp = _remote(part_ref.at[pl.ds(tb, 1)], mod_ref.at[pl.ds(chip, 1)], send2.at[k], recv2.at[k], (tx, ty, c))
            cp.start()
            sends.append(cp)
        for k, (dx, dy) in enumerate(CHIP_RELATIONS):
            src_chip = 2 * _flip(x, dx) + _flip(y, dy)
            _remote(part_ref.at[pl.ds(me, 1)], mod_ref.at[pl.ds(src_chip, 1)], send2.at[k], recv2.at[k],
                    (x, y, c)).wait_recv()
        mine.wait()
        for cp in sends:
            cp.wait_send()

    return pl.pallas_call(
        body, name="mod_exchange",
        out_shape=[jax.ShapeDtypeStruct((N_CHIPS, cols), F32), jax.ShapeDtypeStruct((8, D_MODEL), F32)],
        in_specs=[VMEM_SPEC, VMEM_SPEC, VMEM_SPEC], out_specs=[VMEM_SPEC, VMEM_SPEC],
        scratch_shapes=[pltpu.VMEM((8, cols), F32),
                        pltpu.SemaphoreType.DMA((7,)), pltpu.SemaphoreType.DMA((7,)), pltpu.SemaphoreType.DMA,
                        pltpu.SemaphoreType.DMA((3,)), pltpu.SemaphoreType.DMA((3,)), pltpu.SemaphoreType.DMA],
        compiler_params=_params(),
    )(c_row, w_ada, b_ada)


HBM_SPEC = pl.BlockSpec(memory_space=pltpu.HBM)
SEM_SPEC = pl.BlockSpec(memory_space=pltpu.SEMAPHORE)
DATAFLOW = pltpu.SideEffectType.DATAFLOW_SIDE_EFFECTING


def _in_hbm(a):
    return pltpu.with_memory_space_constraint(a, pltpu.HBM)


def _hbm(*arrays):
    return tuple(_in_hbm(a) for a in arrays)


def _hbm_like(arrays):
    return [pltpu.HBM(a.shape, a.dtype) for a in arrays]


def _split_start(name, srcs, lands, n_remote, plan):
    lands = [lax.empty(a.shape, a.dtype) if isinstance(a, jax.ShapeDtypeStruct) else a for a in lands]
    n, m = len(srcs), len(lands)

    def body(*refs):
        src_refs, land_refs = refs[:n], refs[n:n + m]
        send_sems, recv_sems, token = refs[n + m], refs[n + m + 1], refs[n + 2 * m + 2]
        remote = plan(_place(), src_refs, land_refs)
        assert len(remote) == n_remote
        for i, (s, d, target) in enumerate(remote):
            _remote(s, d, send_sems.at[i], recv_sems.at[i], target).start()
        token[...] = jnp.zeros_like(token)

    res = pl.pallas_call(
        body, name=name,
        out_shape=(pltpu.SemaphoreType.DMA((n_remote,)), pltpu.SemaphoreType.DMA((n_remote,)),
                   *_hbm_like(lands), jax.ShapeDtypeStruct((8, 128), F32)),
        in_specs=[HBM_SPEC] * (n + m),
        out_specs=(SEM_SPEC, SEM_SPEC, *([HBM_SPEC] * m), VMEM_SPEC),
        input_output_aliases={n + i: 2 + i for i in range(m)},
        compiler_params=pltpu.CompilerParams(has_side_effects=DATAFLOW),
    )(*[_in_hbm(a) for a in srcs], *[_in_hbm(a) for a in lands])
    return res[0], res[1], list(res[2:2 + m]), res[2 + m]


def _split_wait(name, send_sems, recv_sems, srcs, lands, after, plan):
    n, m = len(srcs), len(lands)

    def body(*refs):
        src_refs, land_refs = refs[:n], refs[n:n + m]
        send_sems, recv_sems = refs[n + m], refs[n + m + 1]
        place = _place()
        for i, (s, d) in enumerate(plan(place, src_refs, land_refs)):
            cp = _remote(s, d, send_sems.at[i], recv_sems.at[i], place)
            cp.wait_send()
            cp.wait_recv()

    res = pl.pallas_call(
        body, name=name,
        out_shape=tuple(_hbm_like(lands)),
        in_specs=[HBM_SPEC] * (n + m) + [SEM_SPEC, SEM_SPEC, ANY],
        out_specs=tuple([HBM_SPEC] * m),
        input_output_aliases={n + i: i for i in range(m)},
        compiler_params=pltpu.CompilerParams(has_side_effects=DATAFLOW),
    )(*srcs, *lands, send_sems, recv_sems, after)
    return list(res)


def _split_relay(name, send_sems, recv_sems, srcs, lands, after, n_remote, plan_wait, plan_send):
    n, m = len(srcs), len(lands)

    def body(*refs):
        src_refs, land_refs = refs[:n], refs[n:n + m]
        old_send, old_recv = refs[n + m], refs[n + m + 1]
        new_send, new_recv = refs[n + m + 3], refs[n + m + 4]
        token = refs[n + m + 5 + m]
        place = _place()
        for i, (s, d) in enumerate(plan_wait(place, src_refs, land_refs)):
            cp = _remote(s, d, old_send.at[i], old_recv.at[i], place)
            cp.wait_send()
            cp.wait_recv()
        for i, (s, d, target) in enumerate(plan_send(place, land_refs)):
            _remote(s, d, new_send.at[i], new_recv.at[i], target).start()
        token[...] = jnp.zeros_like(token)

    res = pl.pallas_call(
        body, name=name,
        out_shape=(pltpu.SemaphoreType.DMA((n_remote,)), pltpu.SemaphoreType.DMA((n_remote,)),
                   *_hbm_like(lands), jax.ShapeDtypeStruct((8, 128), F32)),
        in_specs=[HBM_SPEC] * (n + m) + [SEM_SPEC, SEM_SPEC, ANY],
        out_specs=(SEM_SPEC, SEM_SPEC, *([HBM_SPEC] * m), VMEM_SPEC),
        input_output_aliases={n + i: 2 + i for i in range(m)},
        compiler_params=pltpu.CompilerParams(has_side_effects=DATAFLOW),
    )(*srcs, *lands, send_sems, recv_sems, after)
    return res[0], res[1], list(res[2:2 + m]), res[2 + m]


def _half(ref, core, axis=0):
    hr = ref.shape[axis] // 2
    return pl.ds(core * hr, hr)


def _plan_gather_start(place, src, land):
    x, y, c = place
    chip = 2 * x + y
    return [(s.at[_half(s, c)], l.at[chip, _half(s, c)], (_flip(x, dx), _flip(y, dy), c))
            for s, l in zip(src, land) for dx, dy in CHIP_RELATIONS]


def _plan_gather_landed(place, src, land):
    x, y, c = place
    return [(s.at[_half(s, c)], l.at[2 * _flip(x, dx) + _flip(y, dy), _half(s, c)])
            for s, l in zip(src, land) for dx, dy in CHIP_RELATIONS]


def _plan_gather_relay(place, land):
    x, y, c = place
    out = []
    for l in land:
        for dx, dy in CHIP_RELATIONS:
            got = l.at[2 * _flip(x, dx) + _flip(y, dy), _half(l, c, 1)]
            out.append((got, got, (x, y, 1 - c)))
    return out


def _plan_gather_wait(place, src, land):
    x, y, c = place
    out = []
    for l in land:
        for dx, dy in CHIP_RELATIONS:
            got = l.at[2 * _flip(x, dx) + _flip(y, dy), _half(l, 1 - c, 1)]
            out.append((got, got))
    return out


def _plan_swap_start(place, src, land):
    x, y, c = place
    return [(s.at[:, _half(s, 1 - c, 1), :], l, (x, y, 1 - c)) for s, l in zip(src, land)]


def _plan_swap_wait(place, src, land):
    return [(s.at[:, _half(s, 0, 1), :], l) for s, l in zip(src, land)]


def _plan_exchange_start(place, src, land):
    x, y, c = place
    remote = []
    for s, l in zip(src, land):
        for k, (dx, dy) in enumerate(CHIP_RELATIONS):
            tx, ty = _flip(x, dx), _flip(y, dy)
            remote.append((s.at[2 * tx + ty], l.at[k], (tx, ty, c)))
    return remote


def _plan_exchange_wait(place, src, land):
    return [(s.at[0], l.at[k]) for s, l in zip(src, land) for k in range(3)]


def _plan_finish_start(place, src, land):
    x, y, c = place
    return [(l.at[c], l.at[c], (x, y, 1 - c)) for l in land]


def _plan_finish_wait(place, src, land):
    x, y, c = place
    return [(l.at[c], l.at[1 - c]) for l in land]


def _plan_near_finish_start(place, src, land):
    x, y, c = place
    mine = land[-1].at[2 * x + y, c]
    return (_plan_finish_start(place, src, land[:-1])
            + [(mine, mine, (_flip(x, dx), _flip(y, dy), _flip(c, dc))) for dx, dy, dc in RELATIONS])


def _plan_near_finish_wait(place, src, land):
    x, y, c = place
    mine = land[-1].at[2 * x + y, c]
    return (_plan_finish_wait(place, src, land[:-1])
            + [(mine, land[-1].at[2 * _flip(x, dx) + _flip(y, dy), _flip(c, dc)]) for dx, dy, dc in RELATIONS])


def _grad_swap_halves(grads, dmod):
    n = len(grads)

    def body(*refs):
        ins, dmod_ref = refs[:n], refs[n]
        outs, dall_ref = refs[n + 1:2 * n + 1], refs[2 * n + 1]
        send_sems, recv_sems, dsend, drecv, dloc = refs[2 * n + 2:]
        x, y, c = _place()
        me = 4 * x + 2 * y + c
        sends = []
        for w in range(n):
            hr = ins[w].shape[1] // 2
            cp = _remote(ins[w].at[:, pl.ds((1 - c) * hr, hr), :], outs[w], send_sems.at[w], recv_sems.at[w],
                         (x, y, 1 - c))
            cp.start()
            sends.append(cp)
        own = pltpu.make_async_copy(dmod_ref, dall_ref.at[me], dloc)
        own.start()
        for k, (dx, dy, dc) in enumerate(RELATIONS):
            cp = _remote(dmod_ref, dall_ref.at[me], dsend.at[k], drecv.at[k],
                         (_flip(x, dx), _flip(y, dy), _flip(c, dc)))
            cp.start()
            sends.append(cp)
        for k, (dx, dy, dc) in enumerate(RELATIONS):
            src = 4 * _flip(x, dx) + 2 * _flip(y, dy) + _flip(c, dc)
            _remote(dmod_ref, dall_ref.at[src], dsend.at[k], drecv.at[k], (x, y, c)).wait_recv()
        for w in range(n):
            _remote(outs[w], outs[w], send_sems.at[w], recv_sems.at[w], (x, y, c)).wait_recv()
        own.wait()
        for cp in sends:
            cp.wait_send()

    out_shape = [pltpu.HBM((N_CHIPS, g.shape[1] // 2, g.shape[2]), F32) for g in grads]
    out_shape.append(pltpu.HBM((8,) + dmod.shape, F32))
    res = pl.pallas_call(
        body, name="grad_swap_halves",
        out_shape=out_shape, in_specs=[ANY] * n + [VMEM_SPEC], out_specs=[ANY] * (n + 1),
        scratch_shapes=[pltpu.SemaphoreType.DMA((n,)), pltpu.SemaphoreType.DMA((n,)),
                        pltpu.SemaphoreType.DMA((7,)), pltpu.SemaphoreType.DMA((7,)), pltpu.SemaphoreType.DMA],
        compiler_params=_params(),
    )(*grads, dmod)
    return res[:n], res[n]


def _add_my_halves(core, fulls, gots, name):
    n = len(fulls)

    def body(core_ref, *refs):
        for w in range(n):
            refs[2 * n + w][...] = refs[w][...] + refs[n + w][...]

    mine = lambda g: pl.BlockSpec((None,) + g.shape[1:], lambda s, core_ref: (s, core_ref[0], 0))
    slab = lambda g: pl.BlockSpec((None,) + g.shape[1:], lambda s, core_ref: (s, 0, 0))
    return list(pl.pallas_call(
        body, name=name,
        out_shape=[pltpu.HBM(g.shape, F32) for g in gots],
        grid_spec=pltpu.PrefetchScalarGridSpec(
            num_scalar_prefetch=1, grid=(N_CHIPS,),
            in_specs=[mine(g) for g in gots] + [slab(g) for g in gots],
            out_specs=[slab(g) for g in gots]),
        compiler_params=_params(("arbitrary",)),
    )(core, *_hbm(*fulls, *gots)))


def _add_chips_into_pairs(chip_core, mines, gots, name):
    n = len(mines)

    def body(cc_ref, *refs):
        for w in range(n):
            b_ref = refs[n + w]
            refs[2 * n + w][...] = ((refs[w][...] + b_ref[0]) + b_ref[1]) + b_ref[2]

    return list(pl.pallas_call(
        body, name=name,
        out_shape=[pltpu.HBM((2,) + m.shape[1:], F32) for m in mines],
        grid_spec=pltpu.PrefetchScalarGridSpec(
            num_scalar_prefetch=1, grid=(1,),
            in_specs=[pl.BlockSpec((None,) + m.shape[1:], lambda s, cc_ref: (cc_ref[0], 0, 0)) for m in mines]
            + [pl.BlockSpec(g.shape, lambda s, cc_ref: (0, 0, 0)) for g in gots],
            out_specs=[pl.BlockSpec((None,) + m.shape[1:], lambda s, cc_ref: (cc_ref[1], 0, 0)) for m in mines]),
        compiler_params=_params(("arbitrary",)),
    )(chip_core, *_hbm(*mines, *gots)))


def _add_chips_into_grid(chip_core, mine, got, name):
    _, hr, cols = mine.shape

    def body(cc_ref, a_ref, b_ref, o_ref):
        o_ref[...] = ((a_ref[...] + b_ref[0]) + b_ref[1]) + b_ref[2]

    return pl.pallas_call(
        body, name=name,
        out_shape=pltpu.HBM((N_CHIPS, 2, hr, cols), F32),
        grid_spec=pltpu.PrefetchScalarGridSpec(
            num_scalar_prefetch=1, grid=(1,),
            in_specs=[pl.BlockSpec((None, hr, cols), lambda s, cc_ref: (cc_ref[0], 0, 0)),
                      pl.BlockSpec((3, hr, cols), lambda s, cc_ref: (0, 0, 0))],
            out_specs=pl.BlockSpec((None, None, hr, cols), lambda s, cc_ref: (cc_ref[0], cc_ref[1], 0, 0))),
        compiler_params=_params(("arbitrary",)),
    )(chip_core, *_hbm(mine, got))


def _place_shards(chip, shards):
    n = len(shards)

    def body(chip_ref, *refs):
        for w in range(n):
            refs[n + w][...] = refs[w][...]

    return pl.pallas_call(
        body, name="place_shards",
        out_shape=[pltpu.HBM((N_CHIPS,) + s.shape, s.dtype) for s in shards],
        grid_spec=pltpu.PrefetchScalarGridSpec(
            num_scalar_prefetch=1, grid=(1,),
            in_specs=[pl.BlockSpec(s.shape, lambda i, chip_ref: (0, 0)) for s in shards],
            out_specs=[pl.BlockSpec((None,) + s.shape, lambda i, chip_ref: (chip_ref[0], 0, 0)) for s in shards]),
        compiler_params=_params(("arbitrary",)),
    )(chip, *shards)


def _rope_tables(pos_col, freqs):
    S = pos_col.shape[0]
    T = _row_tile(S, 1024)

    def body(p_ref, f_ref, cos_ref, sin_ref):
        ang = p_ref[...].astype(F32) * f_ref[...]
        cos_ref[...] = jnp.cos(ang)
        sin_ref[...] = jnp.sin(ang)

    return pl.pallas_call(
        body, name="rope_tables", grid=(S // T,),
        out_shape=[pltpu.HBM((S, 128), F32)] * 2,
        in_specs=[pl.BlockSpec((T, 1), lambda i: (i, 0)), pl.BlockSpec((1, 128), lambda i: (0, 0))],
        out_specs=[pl.BlockSpec((T, 128), lambda i: (i, 0))] * 2,
        compiler_params=_params(("parallel",)),
    )(*_hbm(pos_col, freqs))


def _full(shape):
    zeros = (0,) * len(shape)
    return pl.BlockSpec(shape, lambda *_: zeros)


def _pre_attention(x, mod6, g_mix, g_q, g_kv, w_in, w_uq, w_uk_t, cos, sin, T, TQ):
    S = x.shape[0]

    def body(x_ref, mod_ref, gm_ref, gq_ref, gkv_ref, win_ref, wuq_ref, wuk_ref, cos_ref, sin_ref,
             proj_ref, q_ref, qc_ref, kc_ref, kct_ref):
        xh, _ = _rms(x_ref[...])
        h1 = ((xh * gm_ref[...]) * (1.0 + mod_ref[1:2, :]) + mod_ref[0:1, :]).astype(BF16)
        rows_in = D_MODEL // N_CHIPS
        proj = _dot_nt(h1[:, 0:rows_in], win_ref[0])
        for j in range(1, N_CHIPS):
            proj = proj + _dot_nt(h1[:, j * rows_in:(j + 1) * rows_in], win_ref[j])
        proj_ref[...] = proj
        cqh, _ = _rms(proj[:, :Q_LORA])
        c_q = cqh * gq_ref[...]
        ckvh, _ = _rms(proj[:, O_CKV:O_KR])
        c_kv = ckvh * gkv_ref[...]
        q = _dot(c_q, wuq_ref[...])
        q_ref[...] = q.astype(BF16)
        cos_t, sin_t = cos_ref[...], sin_ref[...]
        ropes = (_rope(q[:, O_QA:O_QB], cos_t, sin_t), _rope(q[:, O_QB:Q_W], cos_t, sin_t))
        low = lax.broadcasted_iota(jnp.int32, (T, 128), 1) < ROPE
        for h in range(HEADS):
            q_lat = _dot_nt(q[:, h * NOPE:(h + 1) * NOPE], wuk_ref[h])
            keep = low if h % 2 == 0 else jnp.logical_not(low)
            qc_ref[h, :, 0:KV_LORA] = q_lat.astype(BF16)
            qc_ref[h, :, KV_LORA:QK_PAD] = jnp.where(keep, ropes[h // 2], 0.0).astype(BF16)
        k_rope = _rope(proj[:, O_KR:O_U], cos_t, sin_t)
        kc_ref[:, 0:KV_LORA] = c_kv.astype(BF16)
        kc_ref[:, KV_LORA:QK_PAD] = k_rope.astype(BF16)
        lat_t, rope_t = jnp.transpose(c_kv), jnp.transpose(k_rope)
        for s in range(T // TQ):
            kct_ref[s, 0:KV_LORA, :] = lat_t[:, s * TQ:(s + 1) * TQ].astype(BF16)
            kct_ref[s, KV_LORA:QK_PAD, :] = rope_t[:, s * TQ:(s + 1) * TQ].astype(BF16)

    row = lambda w: pl.BlockSpec((T, w), lambda i: (i, 0))
    return pl.pallas_call(
        body, name="pre_attention", grid=(S // T,),
        out_shape=[pltpu.HBM((S, PROJ_W), F32), pltpu.HBM((S, Q_W), BF16), pltpu.HBM((HEADS, S, QK_PAD), BF16),
                   pltpu.HBM((S, QK_PAD), BF16), pltpu.HBM((S // TQ, QK_PAD, TQ), BF16)],
        in_specs=[row(D_MODEL), _full((N_MOD, D_MODEL)), _full((1, D_MODEL)), _full((1, Q_LORA)), _full((1, KV_LORA)),
                  _full((N_CHIPS, PROJ_W, D_MODEL // N_CHIPS)), _full((Q_LORA, Q_W)), _full((HEADS, KV_LORA, NOPE)),
                  row(128), row(128)],
        out_specs=[row(PROJ_W), row(Q_W), pl.BlockSpec((HEADS, T, QK_PAD), lambda i: (0, i, 0)), row(QK_PAD),
                   pl.BlockSpec((T // TQ, QK_PAD, TQ), lambda i: (i, 0, 0))],
        compiler_params=_params(("parallel",)),
    )(*_hbm(x, mod6, g_mix, g_q, g_kv, w_in, w_uq, w_uk_t, cos, sin))


def _diag_mask(TQ, width):
    key = lax.broadcasted_iota(jnp.int32, (TQ, width), 0) >> CHUNK_SHIFT
    qry = (lax.broadcasted_iota(jnp.int32, (TQ, width), 1) & (TQ - 1)) >> CHUNK_SHIFT
    return key <= qry


def _col_to_row(col):
    return jnp.transpose(jnp.broadcast_to(col, (col.shape[0], 128)))[0:1, :]


def _attention_fwd(qc, kc, kct, w_uv_t, TQ):
    S = kc.shape[0]
    R = HEADS * TQ
    nq = S // TQ

    def body(q_ref, k_ref, kt_ref, wuv_ref, o_ref, y_ref, lser_ref, m_s, l_s, acc_s, st_s):
        i = pl.program_id(0)
        q = q_ref[...].reshape(R, QK_PAD)
        m_s[...] = jnp.full((1, R), -jnp.inf, F32)
        l_s[...] = jnp.zeros((1, R), F32)
        acc_s[...] = jnp.zeros((KV_LORA, R), F32)

        def scores(j):
            return _dot_nt(k_ref[pl.ds(pl.multiple_of(j * TQ, TQ), TQ), :], q) * SM_SCALE

        def update(j, st):
            m_old = m_s[...]
            m_new = jnp.maximum(m_old, jnp.max(st, axis=0, keepdims=True))
            pt = jnp.exp(st - m_new)
            alpha = jnp.exp(m_old - m_new)
            l_s[...] = alpha * l_s[...] + jnp.sum(pt, axis=0, keepdims=True)
            acc_s[...] = alpha * acc_s[...] + _dot(kt_ref[j, 0:KV_LORA, :], pt)
            m_s[...] = m_new

        st_s[...] = scores(0)

        def loop(j, carry):
            st = st_s[...]
            st_s[...] = scores(j + 1)
            update(j, st)
            return carry

        lax.fori_loop(0, i, loop, 0)
        update(i, jnp.where(_diag_mask(TQ, R), st_s[...], -jnp.inf))
        l = l_s[...]
        lser_ref[0] = m_s[...] + jnp.log(l)
        o = jnp.transpose(acc_s[...] / l).astype(BF16)
        for h in range(HEADS):
            oh = o[h * TQ:(h + 1) * TQ, :]
            o_ref[h] = oh
            y_ref[:, h * 128:(h + 1) * 128] = _dot(oh, wuv_ref[h]).astype(BF16)

    return pl.pallas_call(
        body, name="attention_fwd", grid=(nq,),
        out_shape=[pltpu.HBM((HEADS, S, KV_LORA), BF16), pltpu.HBM((S, HEADS * 128), BF16),
                   pltpu.HBM((nq, 1, R), F32)],
        in_specs=[pl.BlockSpec((HEADS, TQ, QK_PAD), lambda i: (0, i, 0)), _full((S, QK_PAD)),
                  _full((nq, QK_PAD, TQ)), _full((HEADS, KV_LORA, 128))],
        out_specs=[pl.BlockSpec((HEADS, TQ, KV_LORA), lambda i: (0, i, 0)), pl.BlockSpec((TQ, HEADS * 128), lambda i: (i, 0)),
                   pl.BlockSpec((1, 1, R), lambda i: (i, 0, 0))],
        scratch_shapes=[pltpu.VMEM((1, R), F32), pltpu.VMEM((1, R), F32), pltpu.VMEM((KV_LORA, R), F32),
                        pltpu.VMEM((TQ, R), F32)],
        compiler_params=_params(("parallel",)),
    )(*_hbm(qc, kc, kct, w_uv_t))


def _pool_forward(proj):
    S = proj.shape[0]
    RB = _row_tile(S, 256)

    def body(proj_ref, out_ref, pad_ref, sem):
        cp = pltpu.make_async_copy(proj_ref.at[:, pl.ds(O_U, POOL_W)], pad_ref.at[pl.ds(POOL_PAD, S)], sem)
        cp.start()
        pad_ref[0:POOL_PAD, :] = jnp.zeros((POOL_PAD, POOL_W), F32)
        cp.wait()
        for g, win in enumerate(POOL_WINDOWS):
            cols = slice(g * POOL_GROUP, (g + 1) * POOL_GROUP)
            for r0 in range(0, S, RB):
                u = pad_ref[POOL_PAD + r0:POOL_PAD + r0 + RB, cols]
                acc = u
                for k in range(1, win):
                    acc = acc + pad_ref[POOL_PAD + r0 - k:POOL_PAD + r0 - k + RB, cols]
                if r0 == 0:
                    t1 = (lax.broadcasted_iota(jnp.int32, (RB, POOL_GROUP), 0) + 1).astype(F32)
                    mean = acc / jnp.minimum(t1, float(win))
                else:
                    mean = acc * (1.0 / win)
                out_ref[r0:r0 + RB, cols] = (mean - u).astype(BF16)

    return pl.pallas_call(
        body, name="pool_forward",
        out_shape=jax.ShapeDtypeStruct((S, POOL_W), BF16),
        in_specs=[ANY], out_specs=VMEM_SPEC,
        scratch_shapes=[pltpu.VMEM((S + POOL_PAD, POOL_W), F32), pltpu.SemaphoreType.DMA],
        compiler_params=_params(),
    )(proj)


def _pool_backward(dpooled, after):
    S = dpooled.shape[0]
    RB = _row_tile(S, 256)

    def body(dp_ref, after_ref, out_ref, pad_ref, sem):
        cp = pltpu.make_async_copy(dp_ref, pad_ref.at[pl.ds(0, S)], sem)
        cp.start()
        pad_ref[S:S + POOL_PAD, :] = jnp.zeros((POOL_PAD, POOL_W), F32)
        cp.wait()
        for g, win in enumerate(POOL_WINDOWS):
            cols = slice(g * POOL_GROUP, (g + 1) * POOL_GROUP)
            head = pad_ref[0:POOL_PAD, cols]
            t1 = (lax.broadcasted_iota(jnp.int32, (POOL_PAD, POOL_GROUP), 0) + 1).astype(F32)
            pad_ref[0:POOL_PAD, cols] = head * (float(win) / jnp.minimum(t1, float(win)))
            for r0 in range(0, S, RB):
                acc = pad_ref[r0:r0 + RB, cols]
                for k in range(1, win):
                    acc = acc + pad_ref[r0 + k:r0 + k + RB, cols]
                own = pad_ref[r0:r0 + RB, cols]
                if r0 == 0:
                    own = jnp.concatenate([head, own[POOL_PAD:]], axis=0)
                out_ref[r0:r0 + RB, cols] = (acc * (1.0 / win) - own).astype(BF16)

    return pl.pallas_call(
        body, name="pool_backward",
        out_shape=jax.ShapeDtypeStruct((S, POOL_W), BF16),
        in_specs=[ANY, ANY], out_specs=VMEM_SPEC,
        scratch_shapes=[pltpu.VMEM((S + POOL_PAD, POOL_W), F32), pltpu.SemaphoreType.DMA],
        compiler_params=_params(),
    )(dpooled, after)


def _mix_out(y_mla, pooled, w_pool, pool_scale, w_o, x, mod6, T):
    S = x.shape[0]

    def body(ym_ref, pl_ref, wp_ref, ps_ref, wo_ref, x_ref, mod_ref, x1_ref, mix_ref, mi_ref):
        mi_ref[:, 0:512] = ym_ref[...]
        for g in range(len(POOL_WINDOWS)):
            cols = slice(g * POOL_GROUP, (g + 1) * POOL_GROUP)
            z = _dot(pl_ref[:, cols], wp_ref[g])
            mi_ref[:, 512 + g * POOL_GROUP:512 + (g + 1) * POOL_GROUP] = (z * ps_ref[:, cols]).astype(BF16)
        mix = _dot(mi_ref[...], wo_ref[...])
        mix_ref[...] = mix.astype(BF16)
        x1_ref[...] = x_ref[...] + mod_ref[2:3, :] * mix

    row = lambda w: pl.BlockSpec((T, w), lambda i: (i, 0))
    return pl.pallas_call(
        body, name="mix_out", grid=(S // T,),
        out_shape=[pltpu.HBM((S, D_MODEL), F32), pltpu.HBM((S, D_MODEL), BF16), pltpu.HBM((S, 1024), BF16)],
        in_specs=[row(512), row(POOL_W), _full((4, POOL_GROUP, POOL_GROUP)), _full((1, POOL_W)),
                  _full((1024, D_MODEL)), row(D_MODEL), _full((N_MOD, D_MODEL))],
        out_specs=[row(D_MODEL), row(D_MODEL), row(1024)],
        compiler_params=_params(("parallel",)),
    )(*_hbm(y_mla, pooled, w_pool, pool_scale, w_o, x, mod6))


def _ffn_forward(x1, mod6, g_ffn, g_final, target, w_gate, w_up, w_down, T):
    S = x1.shape[0]

    def body(x1_ref, mod_ref, gf_ref, gl_ref, tgt_ref, wg_ref, wu_ref, wd_ref,
             gate_ref, up_ref, act_ref, h2_ref, dff_ref, dx2_ref, st_ref, acc_s):
        i, j = pl.program_id(0), pl.program_id(1)

        @pl.when(jnp.logical_and(i == 0, j == 0))
        def _():
            st_ref[...] = jnp.zeros_like(st_ref)

        @pl.when(j == 0)
        def _():
            xh, _ = _rms(x1_ref[...])
            h2_ref[...] = ((xh * gf_ref[...]) * (1.0 + mod_ref[4:5, :]) + mod_ref[3:4, :]).astype(BF16)
            acc_s[...] = jnp.zeros_like(acc_s)

        h2 = h2_ref[...]
        gate = _dot_nt(h2, wg_ref[j])
        up = _dot_nt(h2, wu_ref[j])
        gate_ref[...] = gate.astype(BF16)
        up_ref[...] = up.astype(BF16)
        act = (gate * jax.nn.sigmoid(gate) * up).astype(BF16)
        act_ref[...] = act
        acc_s[...] += _dot(act, wd_ref[j])

        @pl.when(j == N_CHIPS - 1)
        def _():
            ff = acc_s[...]
            x2 = x1_ref[...] + mod_ref[5:6, :] * ff
            xh, r3 = _rms(x2)
            err = xh * gl_ref[...] - tgt_ref[...]
            dy = err * (1.0 / D_MODEL)
            dx2 = _rms_bwd(dy * gl_ref[...], xh, r3)
            dx2_ref[...] = dx2
            dff_ref[...] = (dx2 * mod_ref[5:6, :]).astype(BF16)
            st_ref[0:1, :] += jnp.sum(dy * xh, axis=0, keepdims=True)
            st_ref[1:2, :] += jnp.sum(dx2 * ff, axis=0, keepdims=True)
            st_ref[2:3, :] += 0.5 * jnp.sum(err * dy)

    row = pl.BlockSpec((T, D_MODEL), lambda i, j: (i, 0))
    chunk_out = pl.BlockSpec((None, T, FF_CHUNK), lambda i, j: (j, i, 0))
    big = pltpu.HBM((N_CHIPS, S, FF_CHUNK), BF16)
    wide = pltpu.HBM((S, D_MODEL), BF16)
    return pl.pallas_call(
        body, name="ffn_forward", grid=(S // T, N_CHIPS),
        out_shape=[big, big, big, wide, wide, pltpu.HBM((S, D_MODEL), F32), jax.ShapeDtypeStruct((8, D_MODEL), F32)],
        in_specs=[row, _full((N_MOD, D_MODEL)), _full((1, D_MODEL)), _full((1, D_MODEL)), row,
                  VMEM_SPEC, VMEM_SPEC, VMEM_SPEC],
        out_specs=[chunk_out, chunk_out, chunk_out, row, row, row, _full((8, D_MODEL))],
        scratch_shapes=[pltpu.VMEM((T, D_MODEL), F32)],
        compiler_params=_params(("arbitrary", "arbitrary")),
    )(*_hbm(x1, mod6, g_ffn, g_final, target), w_gate, w_up, w_down)


def _ffn_backward(dx2, x1, dff, gate, up, mod6, g_ffn, w_gate, w_up, w_down, T):
    S = x1.shape[0]

    def body(dx2_ref, x1_ref, dff_ref, gate_ref, up_ref, mod_ref, gf_ref, wg_ref, wu_ref, wd_ref,
             dgate_ref, dup_ref, dx1_ref, st_ref, acc_s):
        i, j = pl.program_id(0), pl.program_id(1)

        @pl.when(jnp.logical_and(i == 0, j == 0))
        def _():
            st_ref[...] = jnp.zeros_like(st_ref)

        @pl.when(j == 0)
        def _():
            acc_s[...] = jnp.zeros_like(acc_s)

        for r0 in range(0, T, T // 2):
            rows = slice(r0, r0 + T // 2)
            gate, up = gate_ref[rows, :].astype(F32), up_ref[rows, :].astype(F32)
            sg = jax.nn.sigmoid(gate)
            dact = _dot_nt(dff_ref[rows, :], wd_ref[j])
            dup = (dact * (gate * sg)).astype(BF16)
            dgate = (dact * up * (sg * (1.0 + gate * (1.0 - sg)))).astype(BF16)
            dup_ref[rows, :] = dup
            dgate_ref[rows, :] = dgate
            acc_s[rows, :] += _dot(dgate, wg_ref[j]) + _dot(dup, wu_ref[j])

        @pl.when(j == N_CHIPS - 1)
        def _():
            dh2 = acc_s[...]
            xh, r2 = _rms(x1_ref[...])
            n2 = xh * gf_ref[...]
            st_ref[0:1, :] += jnp.sum(dh2, axis=0, keepdims=True)
            st_ref[1:2, :] += jnp.sum(dh2 * n2, axis=0, keepdims=True)
            dn2 = dh2 * (1.0 + mod_ref[4:5, :])
            st_ref[2:3, :] += jnp.sum(dn2 * xh, axis=0, keepdims=True)
            dx1_ref[...] = _rms_bwd(dn2 * gf_ref[...], xh, r2) + dx2_ref[...]

    row = pl.BlockSpec((T, D_MODEL), lambda i, j: (i, 0))
    chunk = pl.BlockSpec((None, T, FF_CHUNK), lambda i, j: (j, i, 0))
    big = pltpu.HBM((N_CHIPS, S, FF_CHUNK), BF16)
    return pl.pallas_call(
        body, name="ffn_backward", grid=(S // T, N_CHIPS),
        out_shape=[big, big, pltpu.HBM((S, D_MODEL), F32), jax.ShapeDtypeStruct((8, D_MODEL), F32)],
        in_specs=[row, row, row, chunk, chunk, _full((N_MOD, D_MODEL)), _full((1, D_MODEL)),
                  VMEM_SPEC, VMEM_SPEC, VMEM_SPEC],
        out_specs=[chunk, chunk, row, _full((8, D_MODEL))],
        scratch_shapes=[pltpu.VMEM((T, D_MODEL), F32)],
        compiler_params=_params(("arbitrary", "arbitrary")),
    )(*_hbm(dx2, x1, dff, gate, up, mod6, g_ffn), w_gate, w_up, w_down)


def _tn_matmul(a, b, a_spec, b_spec, groups, m, n, steps, name):
    def body(a_ref, b_ref, o_ref):
        @pl.when(pl.program_id(1) == 0)
        def _():
            o_ref[...] = jnp.zeros_like(o_ref)

        o_ref[...] += _dot_tn(a_ref[...], b_ref[...])

    return pl.pallas_call(
        body, name=name, grid=(groups, steps),
        out_shape=pltpu.HBM((groups, m, n), F32),
        in_specs=[a_spec, b_spec],
        out_specs=pl.BlockSpec((None, m, n), lambda g, i: (g, 0, 0)),
        compiler_params=_params(("parallel", "arbitrary")),
    )(*_hbm(a, b))


def _mix_backward(dx1, mix, mod6, w_o, pooled, w_pool, pool_scale, w_uv_t, o_lat, T, TQ):
    S = dx1.shape[0]

    def body(dx1_ref, mix_ref, mod_ref, wo_ref, pl_ref, wp_ref, ps_ref, wuv_ref, o_ref,
             dmix_ref, dp_ref, do_ref, dr_ref, gp_ref, guv_ref, st_ref):
        @pl.when(pl.program_id(0) == 0)
        def _():
            st_ref[...] = jnp.zeros_like(st_ref)
            gp_ref[...] = jnp.zeros_like(gp_ref)
            guv_ref[...] = jnp.zeros_like(guv_ref)

        dx1 = dx1_ref[...]
        st_ref[0:1, :] += jnp.sum(dx1 * mix_ref[...].astype(F32), axis=0, keepdims=True)
        dmix = (dx1 * mod_ref[2:3, :]).astype(BF16)
        dmix_ref[...] = dmix
        dmi = _dot_nt(dmix, wo_ref[...])
        dym = dmi[:, 0:512].astype(BF16)
        for g in range(len(POOL_WINDOWS)):
            cols = slice(g * POOL_GROUP, (g + 1) * POOL_GROUP)
            dyp = dmi[:, 512 + g * POOL_GROUP:512 + (g + 1) * POOL_GROUP]
            pooled_g = pl_ref[:, cols]
            z = _dot(pooled_g, wp_ref[g])
            st_ref[1:2, cols] += jnp.sum(dyp * z, axis=0, keepdims=True)
            dz = (dyp * ps_ref[:, cols]).astype(BF16)
            gp_ref[g] += _dot_tn(pooled_g, dz)
            dp_ref[:, cols] = _dot_nt(dz, wp_ref[g])
        for h in range(HEADS):
            dym_h = dym[:, h * 128:(h + 1) * 128]
            do = _dot_nt(dym_h, wuv_ref[h]).astype(BF16)
            do_ref[h] = do
            o_h = o_ref[h]
            guv_ref[h] += _dot_tn(o_h, dym_h)
            delta = _col_to_row(jnp.sum(do.astype(F32) * o_h.astype(F32), axis=1, keepdims=True))
            for s in range(T // TQ):
                dr_ref[s, :, h * TQ:(h + 1) * TQ] = delta[:, s * TQ:(s + 1) * TQ]

    row = lambda w: pl.BlockSpec((T, w), lambda i: (i, 0))
    heads = pl.BlockSpec((HEADS, T, KV_LORA), lambda i: (0, i, 0))
    square = jax.ShapeDtypeStruct((4, 128, 128), F32)
    return pl.pallas_call(
        body, name="mix_backward", grid=(S // T,),
        out_shape=[pltpu.HBM((S, D_MODEL), BF16), pltpu.HBM((S, POOL_W), F32), pltpu.HBM((HEADS, S, KV_LORA), BF16),
                   pltpu.HBM((S // TQ, 1, HEADS * TQ), F32), square, square, jax.ShapeDtypeStruct((8, D_MODEL), F32)],
        in_specs=[row(D_MODEL), row(D_MODEL), _full((N_MOD, D_MODEL)), _full((1024, D_MODEL)), row(POOL_W),
                  _full((4, POOL_GROUP, POOL_GROUP)), _full((1, POOL_W)), _full((HEADS, KV_LORA, 128)), heads],
        out_specs=[row(D_MODEL), row(POOL_W), heads,
                   pl.BlockSpec((T // TQ, 1, HEADS * TQ), lambda i: (i, 0, 0)), _full((4, 128, 128)),
                   _full((4, 128, 128)), _full((8, D_MODEL))],
        compiler_params=_params(("arbitrary",)),
    )(*_hbm(dx1, mix, mod6, w_o, pooled, w_pool, pool_scale, w_uv_t, o_lat))


def _attention_bwd(qc, kc, kct, do, lse_rows, delta_rows, TQ):
    S = kc.shape[0]
    R = HEADS * TQ
    nq = S // TQ

    def body(q_ref, do_ref, lser_ref, dr_ref, k_ref, kt_ref, dqt_ref, dk_ref, dqt_s, dv_s):
        i = pl.program_id(0)

        def key_rows(j):
            return pl.ds(pl.multiple_of(j * TQ, TQ), TQ)

        @pl.when(i == 0)
        def _():
            def zero(j, carry):
                dk_ref[key_rows(j), :] = jnp.zeros((TQ, QK_PAD), F32)
                dv_s[key_rows(j), :] = jnp.zeros((TQ, KV_LORA), F32)
                return carry
            lax.fori_loop(0, nq, zero, 0)

        q = q_ref[...].reshape(R, QK_PAD)
        do = do_ref[...].reshape(R, KV_LORA)
        lse, delta = lser_ref[0], dr_ref[0]
        dqt_s[...] = jnp.zeros((QK_PAD, R), F32)

        def step(j, masked):
            rows = key_rows(j)
            k = k_ref[rows, :]
            st = _dot_nt(k, q) * SM_SCALE
            if masked:
                st = jnp.where(_diag_mask(TQ, R), st, -jnp.inf)
            pt = jnp.exp(st - lse)
            dv_s[rows, :] += _dot(pt, do)
            dpt = _dot_nt(k[:, :KV_LORA], do)
            dst = (pt * (dpt - delta)).astype(BF16)
            dk_ref[rows, :] += _dot(dst, q)
            dqt_s[...] += _dot(kt_ref[j], dst)

        def loop(j, carry):
            step(j, False)
            return carry

        lax.fori_loop(0, i, loop, 0)
        step(i, True)
        dqt_ref[...] = dqt_s[...]

        @pl.when(i == nq - 1)
        def _():
            def finish(j, carry):
                rows = key_rows(j)
                dk = dk_ref[rows, :] * SM_SCALE
                dk_ref[rows, 0:KV_LORA] = dk[:, 0:KV_LORA] + dv_s[rows, :]
                dk_ref[rows, KV_LORA:QK_PAD] = dk[:, KV_LORA:QK_PAD]
                return carry
            lax.fori_loop(0, nq, finish, 0)

    tile = lambda w: pl.BlockSpec((HEADS, TQ, w), lambda i: (0, i, 0))
    row = pl.BlockSpec((1, 1, R), lambda i: (i, 0, 0))
    return pl.pallas_call(
        body, name="attention_bwd", grid=(nq,),
        out_shape=[pltpu.HBM((nq, QK_PAD, R), F32), jax.ShapeDtypeStruct((S, QK_PAD), F32)],
        in_specs=[tile(QK_PAD), tile(KV_LORA), row, row, VMEM_SPEC, VMEM_SPEC],
        out_specs=[pl.BlockSpec((None, QK_PAD, R), lambda i: (i, 0, 0)), VMEM_SPEC],
        scratch_shapes=[pltpu.VMEM((QK_PAD, R), F32), pltpu.VMEM((S, KV_LORA), F32)],
        compiler_params=_params(("arbitrary",)),
    )(*_hbm(qc, do, lse_rows, delta_rows), kc, kct)[::-1]


def _pre_attention_backward(x, dx1, proj, q, dqt, dkc, du, cos, sin, mod6, g_mix, g_q, g_kv, w_in, w_uq, w_uk_t, T, TQ):
    S = x.shape[0]

    def body(x_ref, dx1_ref, proj_ref, q_ref, dqt_ref, dkc_ref, du_ref, cos_ref, sin_ref, mod_ref, gm_ref, gq_ref,
             gkv_ref, win_ref, wuq_ref, wuk_ref, gx_ref, dproj_ref, h1_ref, guk_ref, guq_ref, st_ref, dq_ref):
        @pl.when(pl.program_id(0) == 0)
        def _():
            st_ref[...] = jnp.zeros_like(st_ref)
            guk_ref[...] = jnp.zeros_like(guk_ref)
            guq_ref[...] = jnp.zeros_like(guq_ref)

        cos_t, sin_t = cos_ref[...], sin_ref[...]
        low = lax.broadcasted_iota(jnp.int32, (T, 128), 1) < ROPE
        rope_parts = []
        for h in range(HEADS):
            dqc = jnp.concatenate([jnp.transpose(dqt_ref[s, :, h * TQ:(h + 1) * TQ]) for s in range(T // TQ)], axis=0)
            dqc = dqc * SM_SCALE
            dql = dqc[:, 0:KV_LORA].astype(BF16)
            guk_ref[h] += _dot_tn(dql, q_ref[:, h * NOPE:(h + 1) * NOPE])
            dq_ref[:, h * NOPE:(h + 1) * NOPE] = _dot(dql, wuk_ref[h]).astype(BF16)
            rope_parts.append(dqc[:, KV_LORA:QK_PAD])
        for pair in range(2):
            d = jnp.where(low, rope_parts[2 * pair], rope_parts[2 * pair + 1])
            dq_ref[:, O_QA + 128 * pair:O_QA + 128 * (pair + 1)] = _rope_bwd(d, cos_t, sin_t).astype(BF16)
        dq = dq_ref[...]
        dcq = _dot_nt(dq, wuq_ref[...])
        cqh, rq = _rms(proj_ref[:, 0:Q_LORA])
        guq_ref[...] += _dot_tn(cqh * gq_ref[...], dq)
        st_ref[3:4, 0:Q_LORA] += jnp.sum(dcq * cqh, axis=0, keepdims=True)
        dproj_ref[:, 0:Q_LORA] = _rms_bwd(dcq * gq_ref[...], cqh, rq).astype(BF16)
        dckv = dkc_ref[:, 0:KV_LORA]
        ckvh, rkv = _rms(proj_ref[:, O_CKV:O_KR])
        st_ref[4:5, 0:KV_LORA] += jnp.sum(dckv * ckvh, axis=0, keepdims=True)
        dproj_ref[:, O_CKV:O_KR] = _rms_bwd(dckv * gkv_ref[...], ckvh, rkv).astype(BF16)
        dkr = _rope_bwd(dkc_ref[:, KV_LORA:QK_PAD], cos_t, sin_t)
        dkr = jnp.where(low, dkr + pltpu.roll(dkr, ROPE, 1), 0.0)
        dproj_ref[:, O_KR:O_U] = dkr.astype(BF16)
        dproj_ref[:, O_U:PROJ_W] = du_ref[...].astype(BF16)
        dproj = dproj_ref[...]
        dh1 = jnp.concatenate([_dot(dproj, win_ref[j]) for j in range(N_CHIPS)], axis=1)
        xh, r1 = _rms(x_ref[...])
        n1 = xh * gm_ref[...]
        h1_ref[...] = (n1 * (1.0 + mod_ref[1:2, :]) + mod_ref[0:1, :]).astype(BF16)
        st_ref[0:1, :] += jnp.sum(dh1, axis=0, keepdims=True)
        st_ref[1:2, :] += jnp.sum(dh1 * n1, axis=0, keepdims=True)
        dn1 = dh1 * (1.0 + mod_ref[1:2, :])
        st_ref[2:3, :] += jnp.sum(dn1 * xh, axis=0, keepdims=True)
        gx_ref[...] = _rms_bwd(dn1 * gm_ref[...], xh, r1) + dx1_ref[...]

    row = lambda w: pl.BlockSpec((T, w), lambda i: (i, 0))
    return pl.pallas_call(
        body, name="pre_attention_backward", grid=(S // T,),
        out_shape=[jax.ShapeDtypeStruct((S, D_MODEL), F32), pltpu.HBM((S, PROJ_W), BF16),
                   pltpu.HBM((S, D_MODEL), BF16), jax.ShapeDtypeStruct((HEADS, KV_LORA, NOPE), F32),
                   jax.ShapeDtypeStruct((Q_LORA, Q_W), F32), jax.ShapeDtypeStruct((8, D_MODEL), F32)],
        in_specs=[row(D_MODEL), row(D_MODEL), row(O_KR), row(HEADS * NOPE),
                  pl.BlockSpec((T // TQ, QK_PAD, HEADS * TQ), lambda i: (i, 0, 0)),
                  row(QK_PAD), row(POOL_W), row(128), row(128), _full((N_MOD, D_MODEL)), _full((1, D_MODEL)),
                  _full((1, Q_LORA)), _full((1, KV_LORA)), _full((N_CHIPS, PROJ_W, D_MODEL // N_CHIPS)),
                  _full((Q_LORA, Q_W)), _full((HEADS, KV_LORA, NOPE))],
        out_specs=[row(D_MODEL), row(PROJ_W), row(D_MODEL), _full((HEADS, KV_LORA, NOPE)), _full((Q_LORA, Q_W)),
                   _full((8, D_MODEL))],
        scratch_shapes=[pltpu.VMEM((T, Q_W), BF16)],
        compiler_params=_params(("arbitrary",)),
    )(*_hbm(x, dx1, proj, q, dqt, dkc, du, cos, sin, mod6, g_mix, g_q, g_kv, w_in, w_uq, w_uk_t))


def _ada_grads(c_all, dmod_all, chip):
    cols = N_MOD * D_MODEL // N_CHIPS
    width = dmod_all.shape[1]

    def body(col_ref, c_ref, dcol_ref, dall_ref, gw_ref, gb_ref):
        call = c_ref[...]
        act = call * jax.nn.sigmoid(call)
        gw_ref[...] = _dot_tn(act, dcol_ref[...])
        d = dall_ref[...]
        acc = d[0:1, :]
        for b in range(1, 8):
            acc = acc + d[b:b + 1, :]
        gb_ref[...] = acc

    return pl.pallas_call(
        body, name="ada_grads",
        out_shape=[jax.ShapeDtypeStruct((D_MODEL, cols), F32), jax.ShapeDtypeStruct((1, width), F32)],
        grid_spec=pltpu.PrefetchScalarGridSpec(
            num_scalar_prefetch=1, grid=(1,),
            in_specs=[pl.BlockSpec((8, D_MODEL), lambda s, col_ref: (0, 0)),
                      pl.BlockSpec((8, cols), lambda s, col_ref: (0, col_ref[0])),
                      pl.BlockSpec((8, width), lambda s, col_ref: (0, 0))],
            out_specs=[pl.BlockSpec((D_MODEL, cols), lambda s, col_ref: (0, 0)),
                       pl.BlockSpec((1, width), lambda s, col_ref: (0, 0))]),
        compiler_params=_params(("arbitrary",)),
    )(chip, *_hbm(c_all, dmod_all, dmod_all))


def _adamw(w, g, m, v, name, g_is_landing_zone=True):
    rows, rest = w.shape[0], w.shape[1:]
    T = _row_tile(rows, 256)

    def body(w_ref, g_ref, m_ref, v_ref, *outs):
        d_ref, nm_ref, nv_ref = outs[-3:]
        g = g_ref[...]
        if g_is_landing_zone:
            outs[0][...] = g
        m2 = ADAM_B1 * m_ref[...] + (1.0 - ADAM_B1) * g
        v2 = ADAM_B2 * v_ref[...] + (1.0 - ADAM_B2) * (g * g)
        m_hat = m2 / (1.0 - ADAM_B1 ** ADAM_STEP)
        v_hat = v2 / (1.0 - ADAM_B2 ** ADAM_STEP)
        d_ref[...] = -ADAM_LR * (m_hat / (jnp.sqrt(v_hat) + ADAM_EPS) + ADAM_WD * w_ref[...])
        nm_ref[...] = m2
        nv_ref[...] = v2

    zeros = (0,) * len(rest)
    spec = pl.BlockSpec((T,) + rest, lambda i: (i,) + zeros)
    n_out = 4 if g_is_landing_zone else 3
    res = pl.pallas_call(
        body, name=name, grid=(rows // T,),
        out_shape=[jax.ShapeDtypeStruct(w.shape, F32)] * n_out,
        in_specs=[spec] * 4, out_specs=[spec] * n_out,
        compiler_params=_params(("parallel",)),
    )(*_hbm(w, g, m, v))
    return res if g_is_landing_zone else [g] + list(res)


SMALL_NAMES = ("w_uk", "w_uv", "w_pool", "g_mix", "g_q", "g_kv", "pool_scale", "g_ffn", "g_final", "b_ada")
SMALL_ROWS = 1664


def _pack_rows(parts):
    flat = jnp.concatenate([p.reshape(-1) for p in parts])
    pad = (-flat.shape[0]) % 128
    if pad:
        flat = jnp.concatenate([flat, jnp.zeros((pad,), F32)])
    return flat.reshape(-1, 128)


def kernel(x, c, positions, w_ada, b_ada, g_mix, w_in, g_q, g_kv, w_uq, w_uk, w_uv, w_pool, pool_scale, w_o, g_ffn, w_gate, w_up, w_down, g_final, loss_target, m_w_ada, m_b_ada, m_g_mix, m_w_in, m_g_q, m_g_kv, m_w_uq, m_w_uk, m_w_uv, m_w_pool, m_pool_scale, m_w_o, m_g_ffn, m_w_gate, m_w_up, m_w_down, m_g_final, v_w_ada, v_b_ada, v_g_mix, v_w_in, v_g_q, v_g_kv, v_w_uq, v_w_uk, v_w_uv, v_w_pool, v_pool_scale, v_w_o, v_g_ffn, v_w_gate, v_w_up, v_w_down, v_g_final):
    S = x.shape[1]
    T = _row_tile(S, 512)
    TQ = _row_tile(S, 512)
    TW = _row_tile(S, 4096)
    ix, iy, ic = lax.axis_index("x"), lax.axis_index("y"), lax.axis_index("c")
    chip = (2 * ix + iy).astype(jnp.int32)
    chip_arr = chip.reshape(1)
    core_arr = ic.astype(jnp.int32).reshape(1)

    xs, tgt = x[0], loss_target[0]

    tr = lambda a: jnp.transpose(a[0])
    win_t = tr(w_in)
    win_p = jnp.concatenate([win_t[:O_KR + ROPE], win_t[O_KR:O_KR + ROPE], win_t[O_KR + ROPE:]], axis=0).astype(BF16)
    wuq = w_uq[0]
    wuq_p = jnp.concatenate([wuq[:, h, :NOPE] for h in range(HEADS)] + [wuq[:, h, NOPE:] for h in range(HEADS)],
                            axis=1).astype(BF16)
    w_uk_t = jnp.transpose(w_uk[0], (1, 0, 2)).astype(BF16)
    w_uv_t = jnp.transpose(w_uv[0], (1, 0, 2)).astype(BF16)
    w_pool_b = w_pool[0].astype(BF16)
    first = [win_p, wuq_p]
    later = [w_o[0].astype(BF16), tr(w_gate).astype(BF16), tr(w_up).astype(BF16), w_down[0].astype(BF16)]
    placed = _place_shards(chip_arr, first + later)
    a_send, a_recv, a_lands, token = _split_start("first_weights_start", first, placed[:2], 6, _plan_gather_start)
    half = ROPE // 2
    freqs = jnp.power(ROPE_THETA, -jnp.arange(half, dtype=F32) / half)
    cos, sin = _rope_tables(positions.reshape(S, 1), jnp.tile(freqs, 4).reshape(1, 128) + token[0, 0])
    a_send, a_recv, a_lands, token = _split_relay(
        "first_weights_relay", a_send, a_recv, first, a_lands, cos, 6, _plan_gather_landed, _plan_gather_relay)

    ada_cols = w_ada.shape[2]
    b_cols = lax.dynamic_slice(b_ada, (0, chip * ada_cols), (1, ada_cols))
    mod, c_all = _mod_exchange(c, w_ada[0], b_cols + token[0, 0])
    mod6 = mod.reshape(N_MOD, D_MODEL)
    a_lands = _split_wait("first_weights_wait", a_send, a_recv, [], a_lands, mod, _plan_gather_wait)
    w_in_f = a_lands[0]
    w_uq_f = a_lands[1].reshape(Q_LORA, Q_W)
    wg_lands, mod6, w_in_f = lax.optimization_barrier((placed[2:], mod6, w_in_f))
    wg_send, wg_recv, wg_lands, token = _split_start(
        "weights_start", later, wg_lands, 3 * len(later), _plan_gather_start)
    mod6 = mod6 + token[0, 0]

    proj, q, qc, kc, kct = _pre_attention(xs, mod6, g_mix, g_q, g_kv, w_in_f, w_uq_f, w_uk_t, cos, sin,
                                          _row_tile(S, 1024), TQ)
    o_lat, y_mla, lse_rows = _attention_fwd(qc, kc, kct, w_uv_t, TQ)
    wg_send, wg_recv, wg_lands, token = _split_relay(
        "weights_relay", wg_send, wg_recv, later, wg_lands, y_mla, 3 * len(later), _plan_gather_landed,
        _plan_gather_relay)
    pooled = _pool_forward(proj)
    wg_lands = _split_wait("weights_wait", wg_send, wg_recv, [], wg_lands, pooled, _plan_gather_wait)
    w_o_f = wg_lands[0].reshape(1024, D_MODEL)
    w_gate_f, w_up_f, w_down_f = wg_lands[1], wg_lands[2], wg_lands[3]
    x1, mix, mix_in = _mix_out(y_mla, pooled, w_pool_b, pool_scale, w_o_f, xs, mod6, _row_tile(S, 1024))
    gate, up, act, h2, dff, dx2, st_f = _ffn_forward(
        x1, mod6, g_ffn, g_final.reshape(1, D_MODEL), tgt, w_gate_f, w_up_f, w_down_f, T)

    dgate, dup, dx1, st_b = _ffn_backward(dx2, x1, dff, gate, up, mod6, g_ffn, w_gate_f, w_up_f, w_down_f, T)
    steps = S // TW
    chunk_spec = pl.BlockSpec((None, TW, FF_CHUNK), lambda g, i: (g, i, 0))
    wide_spec = pl.BlockSpec((TW, D_MODEL), lambda g, i: (i, 0))
    g_down = _tn_matmul(act, dff, chunk_spec, wide_spec, N_CHIPS, FF_CHUNK, D_MODEL, steps, "grad_w_down")
    g_gate = _tn_matmul(dgate, h2, chunk_spec, wide_spec, N_CHIPS, FF_CHUNK, D_MODEL, steps, "grad_w_gate")
    g_up = _tn_matmul(dup, h2, chunk_spec, wide_spec, N_CHIPS, FF_CHUNK, D_MODEL, steps, "grad_w_up")

    half_shapes = lambda gs: [jax.ShapeDtypeStruct((N_CHIPS, g.shape[1] // 2, g.shape[2]), F32) for g in gs]
    ffn_grads = [g_gate, g_up, g_down]
    f_send, f_recv, f_lands, token = _split_start(
        "ffn_swap_start", ffn_grads, half_shapes(ffn_grads), len(ffn_grads), _plan_swap_start)
    dmix, dpooled, do_lat, delta_rows, g_pool, g_uv_t, st_m = _mix_backward(
        dx1, mix, mod6 + token[0, 0], w_o_f, pooled, w_pool_b, pool_scale, w_uv_t, o_lat, _row_tile(S, 1024), TQ)
    g_o = [_tn_matmul(mix_in, dmix, wide_spec, wide_spec, 1, 1024, D_MODEL, steps, "grad_w_o").reshape(N_CHIPS, -1, D_MODEL)]
    o_send, o_recv, o_lands, token = _split_start("w_o_swap_start", g_o, half_shapes(g_o), 1, _plan_swap_start)
    du = _pool_backward(dpooled, token)
    f_got = _split_wait("ffn_swap_wait", f_send, f_recv, ffn_grads, f_lands, du, _plan_swap_wait)
    f_got += _split_wait("w_o_swap_wait", o_send, o_recv, g_o, o_lands, du, _plan_swap_wait)
    far_grads = ffn_grads + g_o
    f_sums = _add_my_halves(core_arr, far_grads, f_got, "add_half_far")
    f_send, f_recv, f_lands, token = _split_start(
        "far_exchange_start", f_sums, [jax.ShapeDtypeStruct((3,) + s.shape[1:], F32) for s in f_sums],
        3 * len(f_sums), _plan_exchange_start)
    delta_rows = delta_rows + token[0, 0]
    dkc, dqt = _attention_bwd(qc, kc, kct, do_lat, lse_rows, delta_rows, TQ)
    grad_x, dproj, h1, g_uk_t, uq, st_p = _pre_attention_backward(
        xs, dx1, proj, q, dqt, dkc, du, cos, sin, mod6, g_mix, g_q, g_kv, w_in_f, w_uq_f, w_uk_t, T, TQ)
    rows_in = D_MODEL // N_CHIPS
    g_in_p = _tn_matmul(dproj, h1, pl.BlockSpec((TW, PROJ_W), lambda g, i: (i, 0)),
                        pl.BlockSpec((TW, rows_in), lambda g, i: (i, g)), N_CHIPS, PROJ_W, rows_in, steps, "grad_w_in")

    g_in = jnp.concatenate([g_in_p[:, :O_KR + ROPE], g_in_p[:, O_U:]], axis=1)
    g_uq = jnp.concatenate([jnp.concatenate([uq[:, h * NOPE:(h + 1) * NOPE], uq[:, O_QA + h * ROPE:O_QA + (h + 1) * ROPE]],
                                            axis=1) for h in range(HEADS)], axis=1).reshape(N_CHIPS, -1, HEADS * HEAD_QK)
    small = _pack_rows([g_uk_t, g_uv_t, g_pool, st_p[2], st_p[3, :Q_LORA], st_p[4, :KV_LORA], st_m[1, :POOL_W],
                        st_b[2], st_f[0]])
    small = jnp.concatenate([small, jnp.zeros((SMALL_ROWS - small.shape[0], 128), F32)]).reshape(N_CHIPS, -1, 128)
    grads = [g_in, g_uq, small]
    dmod = jnp.concatenate([jnp.stack([st_p[0], st_p[1], st_m[0], st_b[0], st_b[1], st_f[1]]).reshape(48, 128),
                            jnp.zeros((8, 128), F32).at[0, 0].set(st_f[2, 0])])

    got, dmod_all = _grad_swap_halves(grads, dmod)
    chip_sums = _add_my_halves(core_arr, grads, got, "add_half_near")
    n_send, n_recv, n_lands, token = _split_start(
        "near_exchange_start", chip_sums, [jax.ShapeDtypeStruct((3,) + s.shape[1:], F32) for s in chip_sums],
        3 * len(chip_sums), _plan_exchange_start)

    f_others = _split_wait("far_exchange_wait", f_send, f_recv, f_sums, f_lands, token, _plan_exchange_wait)
    chip_core = jnp.concatenate([chip_arr, core_arr])
    f_pairs = (_add_chips_into_pairs(chip_core, f_sums[:2], f_others[:2], "add_chips_gate_up")
               + _add_chips_into_pairs(chip_core, f_sums[2:], f_others[2:], "add_chips_down_o"))
    f_send, f_recv, f_pairs, token = _split_start("far_finish_start", [], f_pairs, len(f_pairs), _plan_finish_start)
    gw_ada, gb_ada = _ada_grads(c_all, dmod_all.reshape(8, -1) + token[0, 0], chip_arr)
    loss = gb_ada[0, N_MOD * D_MODEL]
    gb_ada = gb_ada[:, :N_MOD * D_MODEL]
    f_fulls = _split_wait("far_finish_wait", f_send, f_recv, [], f_pairs, gw_ada, _plan_finish_wait)
    gw_gate, gw_up, gw_down, gw_o = [f.reshape(-1, f.shape[2]) for f in f_fulls]

    untr = lambda a: jnp.transpose(a)[None]
    grad_out, delta_out, newm_out, newv_out = {}, {}, {}, {}

    def adam_sharded(n, w, g2, m, v, transposed, landed=True):
        view = (lambda a: jnp.transpose(a[0])) if transposed else (lambda a: a[0])
        back = untr if transposed else (lambda a: a[None])
        g_, d_, m_, v_ = _adamw(view(w), g2.reshape(view(w).shape), view(m), view(v), "adamw_" + n, landed)
        grad_out[n], delta_out[n], newm_out[n], newv_out[n] = back(g_), back(d_), back(m_), back(v_)
        return d_

    done = [adam_sharded("w_gate", w_gate, gw_gate, m_w_gate, v_w_gate, True),
            adam_sharded("w_up", w_up, gw_up, m_w_up, v_w_up, True),
            adam_sharded("w_down", w_down, gw_down, m_w_down, v_w_down, False),
            adam_sharded("w_o", w_o, gw_o, m_w_o, v_w_o, False)]
    after_all = jnp.stack([d[0, 0] for d in done])

    others = _split_wait("near_exchange_wait", n_send, n_recv, chip_sums, n_lands, after_all, _plan_exchange_wait)
    n_pairs = _add_chips_into_pairs(chip_core, chip_sums[:2], others[:2], "add_chips_in_uq")
    small_grid = _add_chips_into_grid(chip_core, chip_sums[2], others[2], "add_chips_small")
    n_send, n_recv, n_lands, token = _split_start(
        "near_finish_start", [], n_pairs + [small_grid], 2 + len(RELATIONS), _plan_near_finish_start)
    gw_ada, _ = lax.optimization_barrier((gw_ada, token))
    d_ada = adam_sharded("w_ada", w_ada, gw_ada, m_w_ada, v_w_ada, False, landed=False)
    n_lands = _split_wait("near_finish_wait", n_send, n_recv, [], n_lands, d_ada, _plan_near_finish_wait)
    gw_in, gw_uq = [f.reshape(-1, f.shape[2]) for f in n_lands[:2]]
    small_all = n_lands[2].reshape(SMALL_ROWS * 128)
    adam_sharded("w_in", w_in, gw_in, m_w_in, v_w_in, True)
    adam_sharded("w_uq", w_uq, gw_uq, m_w_uq, v_w_uq, False)

    n_sq = KV_LORA * HEADS * 128
    sizes = [n_sq, n_sq, n_sq, D_MODEL, Q_LORA, KV_LORA, POOL_W, D_MODEL, D_MODEL]
    offs = [0]
    for s_ in sizes:
        offs.append(offs[-1] + s_)
    piece = lambda k: small_all[offs[k]:offs[k + 1]]
    grads_small = {
        "w_uk": jnp.transpose(piece(0).reshape(HEADS, KV_LORA, NOPE), (1, 0, 2)),
        "w_uv": jnp.transpose(piece(1).reshape(HEADS, KV_LORA, 128), (1, 0, 2)),
        "w_pool": piece(2).reshape(4, POOL_GROUP, POOL_GROUP),
        "g_mix": piece(3), "g_q": piece(4), "g_kv": piece(5), "pool_scale": piece(6), "g_ffn": piece(7),
        "g_final": piece(8), "b_ada": gb_ada.reshape(-1),
    }
    weights_small = {"w_uk": w_uk, "w_uv": w_uv, "w_pool": w_pool, "g_mix": g_mix, "g_q": g_q, "g_kv": g_kv,
                     "pool_scale": pool_scale, "g_ffn": g_ffn, "g_final": g_final, "b_ada": b_ada}
    m_small = {"w_uk": m_w_uk, "w_uv": m_w_uv, "w_pool": m_w_pool, "g_mix": m_g_mix, "g_q": m_g_q, "g_kv": m_g_kv,
               "pool_scale": m_pool_scale, "g_ffn": m_g_ffn, "g_final": m_g_final, "b_ada": m_b_ada}
    v_small = {"w_uk": v_w_uk, "w_uv": v_w_uv, "w_pool": v_w_pool, "g_mix": v_g_mix, "g_q": v_g_q, "g_kv": v_g_kv,
               "pool_scale": v_pool_scale, "g_ffn": v_g_ffn, "g_final": v_g_final, "b_ada": v_b_ada}
    pack = lambda d: _pack_rows([d[n] for n in SMALL_NAMES])
    _, d_s, m_s, v_s = _adamw(pack(weights_small), pack(grads_small), pack(m_small), pack(v_small), "adamw_small",
                              g_is_landing_zone=False)

    def unpack(flat2d):
        flat = flat2d.reshape(-1)
        out, o = {}, 0
        for n in SMALL_NAMES:
            size = weights_small[n].size
            out[n] = flat[o:o + size].reshape(weights_small[n].shape)
            o += size
        return out

    delta_s, newm_s, newv_s = unpack(d_s), unpack(m_s), unpack(v_s)

    for n in SMALL_NAMES:
        grad_out[n] = grads_small[n].reshape(weights_small[n].shape)
        delta_out[n], newm_out[n], newv_out[n] = delta_s[n], newm_s[n], newv_s[n]

    order = ("w_ada", "b_ada", "g_mix", "w_in", "g_q", "g_kv", "w_uq", "w_uk", "w_uv", "w_pool", "pool_scale", "w_o",
             "g_ffn", "w_gate", "w_up", "w_down", "g_final")
    return (loss, grad_x.reshape(x.shape), *[grad_out[n] for n in order], *[delta_out[n] for n in order],
            *[newm_out[n] for n in order], *[newv_out[n] for n in order])
```

```python
import functools

import jax
import jax.numpy as jnp
from jax import lax
from jax.experimental import pallas as pl
from jax.experimental.pallas import tpu as pltpu

F32 = jnp.float32
BF16 = jnp.bfloat16

D_MODEL = 1024
HEADS = 4
NOPE = 128
ROPE = 64
HEAD_QK = NOPE + ROPE
Q_LORA = 256
KV_LORA = 128
POOL_W = 512
POOL_WINDOWS = (2, 4, 8, 16)
POOL_GROUP = 128
POOL_PAD = 16
D_FF = 2816
N_CHIPS = 4
FF_CHUNK = D_FF // N_CHIPS
N_MOD = 6
EPS = 1e-6
SM_SCALE = HEAD_QK ** -0.5
ROPE_THETA = 10000.0
QK_PAD = 256
CHUNK = 64
CHUNK_SHIFT = 6

ADAM_LR = 0.001
ADAM_B1 = 0.9
ADAM_B2 = 0.999
ADAM_EPS = 1e-08
ADAM_WD = 0.01
ADAM_STEP = 10

VMEM_LIMIT = 48 * 1024 * 1024
MESH = pl.DeviceIdType.MESH
ANY = pl.BlockSpec(memory_space=pl.ANY)
VMEM_SPEC = pl.BlockSpec(memory_space=pltpu.VMEM)

PROJ_W = 1024
O_CKV = 256
O_KR = 384
O_U = 512
Q_W = 768
O_QA = 512
O_QB = 640


def _params(sem=None, vmem=VMEM_LIMIT):
    kw = dict(vmem_limit_bytes=vmem)
    if sem is not None:
        kw["dimension_semantics"] = sem
    return pltpu.CompilerParams(**kw)


def _dot(a, b):
    return jnp.dot(a.astype(BF16), b.astype(BF16), preferred_element_type=F32)


def _dot_nt(a, b):
    return lax.dot_general(a.astype(BF16), b.astype(BF16), (((1,), (1,)), ((), ())), preferred_element_type=F32)


def _dot_tn(a, b):
    return lax.dot_general(a.astype(BF16), b.astype(BF16), (((0,), (0,)), ((), ())), preferred_element_type=F32)


def _row_tile(rows, target):
    best = rows
    for t in range(8, min(rows, target) + 1, 8):
        if rows % t == 0:
            best = t
    return best if rows % best == 0 and best <= target else rows


def _rms(x):
    r = lax.rsqrt(jnp.mean(x * x, axis=-1, keepdims=True) + EPS)
    return x * r, r


def _rms_bwd(dxh, xh, r):
    return r * (dxh - xh * jnp.mean(dxh * xh, axis=-1, keepdims=True))


def _lane_first_half(shape):
    lane = lax.broadcasted_iota(jnp.int32, shape, 1)
    return (lane & (ROPE - 1)) < (ROPE // 2)


def _rope(a, cos, sin):
    first = _lane_first_half(a.shape)
    up = pltpu.roll(a, 96, 1)
    dn = pltpu.roll(a, 32, 1)
    return a * cos + jnp.where(first, -up, dn) * sin


def _rope_bwd(d, cos, sin):
    first = _lane_first_half(d.shape)
    up = pltpu.roll(d, 96, 1)
    dn = pltpu.roll(d, 32, 1)
    return d * cos + jnp.where(first, up, -dn) * sin


RELATIONS = tuple((dx, dy, dc) for dx in (0, 1) for dy in (0, 1) for dc in (0, 1) if (dx, dy, dc) != (0, 0, 0))
CHIP_RELATIONS = ((1, 0), (0, 1), (1, 1))


def _flip(v, d):
    return 1 - v if d else v


def _place():
    return lax.axis_index("x"), lax.axis_index("y"), lax.axis_index("c")


def _remote(src, dst, send_sem, recv_sem, target):
    return pltpu.make_async_remote_copy(src_ref=src, dst_ref=dst, send_sem=send_sem, recv_sem=recv_sem,
                                        device_id=target, device_id_type=MESH)


def _mod_exchange(c_row, w_ada, b_ada):
    cols = w_ada.shape[1]

    def body(c_ref, w_ref, b_ref, mod_ref, call_ref, part_ref, send1, recv1, loc1, send2, recv2, loc2):
        x, y, c = _place()
        me = 4 * x + 2 * y + c
        own = pltpu.make_async_copy(c_ref, call_ref.at[pl.ds(me, 1)], loc1)
        own.start()
        sends = []
        for k, (dx, dy, dc) in enumerate(RELATIONS):
            cp = _remote(c_ref, call_ref.at[pl.ds(me, 1)], send1.at[k], recv1.at[k],
                         (_flip(x, dx), _flip(y, dy), _flip(c, dc)))
            cp.start()
            sends.append(cp)
        for k, (dx, dy, dc) in enumerate(RELATIONS):
            src = 4 * _flip(x, dx) + 2 * _flip(y, dy) + _flip(c, dc)
            _remote(c_ref, call_ref.at[pl.ds(src, 1)], send1.at[k], recv1.at[k], (x, y, c)).wait_recv()
        own.wait()
        for cp in sends:
            cp.wait_send()
        call = call_ref[...]
        act = call * jax.nn.sigmoid(call)
        part_ref[...] = _dot(act, w_ref[...]) + b_ref[...]
        chip = 2 * x + y
        mine = pltpu.make_async_copy(part_ref.at[pl.ds(me, 1)], mod_ref.at[pl.ds(chip, 1)], loc2)
        mine.start()
        sends = []
        for k, (dx, dy) in enumerate(CHIP_RELATIONS):
            tx, ty = _flip(x, dx), _flip(y, dy)
            tb = 4 * tx + 2 * ty + c
            cp = _remote(part_ref.at[pl.ds(tb, 1)], mod_ref.at[pl.ds(chip, 1)], send2.at[k], recv2.at[k], (tx, ty, c))
            cp.start()
            sends.append(cp)
        for k, (dx, dy) in enumerate(CHIP_RELATIONS):
            src_chip = 2 * _flip(x, dx) + _flip(y, dy)
            _remote(part_ref.at[pl.ds(me, 1)], mod_ref.at[pl.ds(src_chip, 1)], send2.at[k], recv2.at[k],
                    (x, y, c)).wait_recv()
        mine.wait()
        for cp in sends:
            cp.wait_send()

    return pl.pallas_call(
        body, name="mod_exchange",
        out_shape=[jax.ShapeDtypeStruct((N_CHIPS, cols), F32), jax.ShapeDtypeStruct((8, D_MODEL), F32)],
        in_specs=[VMEM_SPEC, VMEM_SPEC, VMEM_SPEC], out_specs=[VMEM_SPEC, VMEM_SPEC],
        scratch_shapes=[pltpu.VMEM((8, cols), F32),
                        pltpu.SemaphoreType.DMA((7,)), pltpu.SemaphoreType.DMA((7,)), pltpu.SemaphoreType.DMA,
                        pltpu.SemaphoreType.DMA((3,)), pltpu.SemaphoreType.DMA((3,)), pltpu.SemaphoreType.DMA],
        compiler_params=_params(),
    )(c_row, w_ada, b_ada)


HBM_SPEC = pl.BlockSpec(memory_space=pltpu.HBM)
SEM_SPEC = pl.BlockSpec(memory_space=pltpu.SEMAPHORE)
DATAFLOW = pltpu.SideEffectType.DATAFLOW_SIDE_EFFECTING


def _in_hbm(a):
    return pltpu.with_memory_space_constraint(a, pltpu.HBM)


def _hbm(*arrays):
    return tuple(_in_hbm(a) for a in arrays)


def _hbm_like(arrays):
    return [pltpu.HBM(a.shape, a.dtype) for a in arrays]


def _split_start(name, srcs, lands, n_remote, plan):
    lands = [lax.empty(a.shape, a.dtype) if isinstance(a, jax.ShapeDtypeStruct) else a for a in lands]
    n, m = len(srcs), len(lands)

    def body(*refs):
        src_refs, land_refs = refs[:n], refs[n:n + m]
        send_sems, recv_sems, token = refs[n + m], refs[n + m + 1], refs[n + 2 * m + 2]
        remote = plan(_place(), src_refs, land_refs)
        assert len(remote) == n_remote
        for i, (s, d, target) in enumerate(remote):
            _remote(s, d, send_sems.at[i], recv_sems.at[i], target).start()
        token[...] = jnp.zeros_like(token)

    res = pl.pallas_call(
        body, name=name,
        out_shape=(pltpu.SemaphoreType.DMA((n_remote,)), pltpu.SemaphoreType.DMA((n_remote,)),
                   *_hbm_like(lands), jax.ShapeDtypeStruct((8, 128), F32)),
        in_specs=[HBM_SPEC] * (n + m),
        out_specs=(SEM_SPEC, SEM_SPEC, *([HBM_SPEC] * m), VMEM_SPEC),
        input_output_aliases={n + i: 2 + i for i in range(m)},
        compiler_params=pltpu.CompilerParams(has_side_effects=DATAFLOW),
    )(*[_in_hbm(a) for a in srcs], *[_in_hbm(a) for a in lands])
    return res[0], res[1], list(res[2:2 + m]), res[2 + m]


def _split_wait(name, send_sems, recv_sems, srcs, lands, after, plan):
    n, m = len(srcs), len(lands)

    def body(*refs):
        src_refs, land_refs = refs[:n], refs[n:n + m]
        send_sems, recv_sems = refs[n + m], refs[n + m + 1]
        place = _place()
        for i, (s, d) in enumerate(plan(place, src_refs, land_refs)):
            cp = _remote(s, d, send_sems.at[i], recv_sems.at[i], place)
            cp.wait_send()
            cp.wait_recv()

    res = pl.pallas_call(
        body, name=name,
        out_shape=tuple(_hbm_like(lands)),
        in_specs=[HBM_SPEC] * (n + m) + [SEM_SPEC, SEM_SPEC, ANY],
        out_specs=tuple([HBM_SPEC] * m),
        input_output_aliases={n + i: i for i in range(m)},
        compiler_params=pltpu.CompilerParams(has_side_effects=DATAFLOW),
    )(*srcs, *lands, send_sems, recv_sems, after)
    return list(res)


def _split_relay(name, send_sems, recv_sems, srcs, lands, after, n_remote, plan_wait, plan_send):
    n, m = len(srcs), len(lands)

    def body(*refs):
        src_refs, land_refs = refs[:n], refs[n:n + m]
        old_send, old_recv = refs[n + m], refs[n + m + 1]
        new_send, new_recv = refs[n + m + 3], refs[n + m + 4]
        token = refs[n + m + 5 + m]
        place = _place()
        for i, (s, d) in enumerate(plan_wait(place, src_refs, land_refs)):
            cp = _remote(s, d, old_send.at[i], old_recv.at[i], place)
            cp.wait_send()
            cp.wait_recv()
        for i, (s, d, target) in enumerate(plan_send(place, land_refs)):
            _remote(s, d, new_send.at[i], new_recv.at[i], target).start()
        token[...] = jnp.zeros_like(token)

    res = pl.pallas_call(
        body, name=name,
        out_shape=(pltpu.SemaphoreType.DMA((n_remote,)), pltpu.SemaphoreType.DMA((n_remote,)),
                   *_hbm_like(lands), jax.ShapeDtypeStruct((8, 128), F32)),
        in_specs=[HBM_SPEC] * (n + m) + [SEM_SPEC, SEM_SPEC, ANY],
        out_specs=(SEM_SPEC, SEM_SPEC, *([HBM_SPEC] * m), VMEM_SPEC),
        input_output_aliases={n + i: 2 + i for i in range(m)},
        compiler_params=pltpu.CompilerParams(has_side_effects=DATAFLOW),
    )(*srcs, *lands, send_sems, recv_sems, after)
    return res[0], res[1], list(res[2:2 + m]), res[2 + m]


def _half(ref, core, axis=0):
    hr = ref.shape[axis] // 2
    return pl.ds(core * hr, hr)


def _plan_gather_start(place, src, land):
    x, y, c = place
    chip = 2 * x + y
    return [(s.at[_half(s, c)], l.at[chip, _half(s, c)], (_flip(x, dx), _flip(y, dy), c))
            for s, l in zip(src, land) for dx, dy in CHIP_RELATIONS]


def _plan_gather_landed(place, src, land):
    x, y, c = place
    return [(s.at[_half(s, c)], l.at[2 * _flip(x, dx) + _flip(y, dy), _half(s, c)])
            for s, l in zip(src, land) for dx, dy in CHIP_RELATIONS]


def _plan_gather_relay(place, land):
    x, y, c = place
    out = []
    for l in land:
        for dx, dy in CHIP_RELATIONS:
            got = l.at[2 * _flip(x, dx) + _flip(y, dy), _half(l, c, 1)]
            out.append((got, got, (x, y, 1 - c)))
    return out


def _plan_gather_wait(place, src, land):
    x, y, c = place
    out = []
    for l in land:
        for dx, dy in CHIP_RELATIONS:
            got = l.at[2 * _flip(x, dx) + _flip(y, dy), _half(l, 1 - c, 1)]
            out.append((got, got))
    return out


def _plan_swap_start(place, src, land):
    x, y, c = place
    return [(s.at[:, _half(s, 1 - c, 1), :], l, (x, y, 1 - c)) for s, l in zip(src, land)]


def _plan_swap_wait(place, src, land):
    return [(s.at[:, _half(s, 0, 1), :], l) for s, l in zip(src, land)]


def _plan_exchange_start(place, src, land):
    x, y, c = place
    remote = []
    for s, l in zip(src, land):
        for k, (dx, dy) in enumerate(CHIP_RELATIONS):
            tx, ty = _flip(x, dx), _flip(y, dy)
            remote.append((s.at[2 * tx + ty], l.at[k], (tx, ty, c)))
    return remote


def _plan_exchange_wait(place, src, land):
    return [(s.at[0], l.at[k]) for s, l in zip(src, land) for k in range(3)]


def _plan_finish_start(place, src, land):
    x, y, c = place
    return [(l.at[c], l.at[c], (x, y, 1 - c)) for l in land]


def _plan_finish_wait(place, src, land):
    x, y, c = place
    return [(l.at[c], l.at[1 - c]) for l in land]


def _plan_near_finish_start(place, src, land):
    x, y, c = place
    mine = land[-1].at[2 * x + y, c]
    return (_plan_finish_start(place, src, land[:-1])
            + [(mine, mine, (_flip(x, dx), _flip(y, dy), _flip(c, dc))) for dx, dy, dc in RELATIONS])


def _plan_near_finish_wait(place, src, land):
    x, y, c = place
    mine = land[-1].at[2 * x + y, c]
    return (_plan_finish_wait(place, src, land[:-1])
            + [(mine, land[-1].at[2 * _flip(x, dx) + _flip(y, dy), _flip(c, dc)]) for dx, dy, dc in RELATIONS])


def _grad_swap_halves(grads, dmod):
    n = len(grads)

    def body(*refs):
        ins, dmod_ref = refs[:n], refs[n]
        outs, dall_ref = refs[n + 1:2 * n + 1], refs[2 * n + 1]
        send_sems, recv_sems, dsend, drecv, dloc = refs[2 * n + 2:]
        x, y, c = _place()
        me = 4 * x + 2 * y + c
        sends = []
        for w in range(n):
            hr = ins[w].shape[1] // 2
            cp = _remote(ins[w].at[:, pl.ds((1 - c) * hr, hr), :], outs[w], send_sems.at[w], recv_sems.at[w],
                         (x, y, 1 - c))
            cp.start()
            sends.append(cp)
        own = pltpu.make_async_copy(dmod_ref, dall_ref.at[me], dloc)
        own.start()
        for k, (dx, dy, dc) in enumerate(RELATIONS):
            cp = _remote(dmod_ref, dall_ref.at[me], dsend.at[k], drecv.at[k],
                         (_flip(x, dx), _flip(y, dy), _flip(c, dc)))
            cp.start()
            sends.append(cp)
        for k, (dx, dy, dc) in enumerate(RELATIONS):
            src = 4 * _flip(x, dx) + 2 * _flip(y, dy) + _flip(c, dc)
            _remote(dmod_ref, dall_ref.at[src], dsend.at[k], drecv.at[k], (x, y, c)).wait_recv()
        for w in range(n):
            _remote(outs[w], outs[w], send_sems.at[w], recv_sems.at[w], (x, y, c)).wait_recv()
        own.wait()
        for cp in sends:
            cp.wait_send()

    out_shape = [pltpu.HBM((N_CHIPS, g.shape[1] // 2, g.shape[2]), F32) for g in grads]
    out_shape.append(pltpu.HBM((8,) + dmod.shape, F32))
    res = pl.pallas_call(
        body, name="grad_swap_halves",
        out_shape=out_shape, in_specs=[ANY] * n + [VMEM_SPEC], out_specs=[ANY] * (n + 1),
        scratch_shapes=[pltpu.SemaphoreType.DMA((n,)), pltpu.SemaphoreType.DMA((n,)),
                        pltpu.SemaphoreType.DMA((7,)), pltpu.SemaphoreType.DMA((7,)), pltpu.SemaphoreType.DMA],
        compiler_params=_params(),
    )(*grads, dmod)
    return res[:n], res[n]


def _add_my_halves(core, fulls, gots, name):
    n = len(fulls)

    def body(core_ref, *refs):
        for w in range(n):
            refs[2 * n + w][...] = refs[w][...] + refs[n + w][...]

    mine = lambda g: pl.BlockSpec((None,) + g.shape[1:], lambda s, core_ref: (s, core_ref[0], 0))
    slab = lambda g: pl.BlockSpec((None,) + g.shape[1:], lambda s, core_ref: (s, 0, 0))
    return list(pl.pallas_call(
        body, name=name,
        out_shape=[pltpu.HBM(g.shape, F32) for g in gots],
        grid_spec=pltpu.PrefetchScalarGridSpec(
            num_scalar_prefetch=1, grid=(N_CHIPS,),
            in_specs=[mine(g) for g in gots] + [slab(g) for g in gots],
            out_specs=[slab(g) for g in gots]),
        compiler_params=_params(("arbitrary",)),
    )(core, *_hbm(*fulls, *gots)))


def _add_chips_into_pairs(chip_core, mines, gots, name):
    n = len(mines)

    def body(cc_ref, *refs):
        for w in range(n):
            b_ref = refs[n + w]
            refs[2 * n + w][...] = ((refs[w][...] + b_ref[0]) + b_ref[1]) + b_ref[2]

    return list(pl.pallas_call(
        body, name=name,
        out_shape=[pltpu.HBM((2,) + m.shape[1:], F32) for m in mines],
        grid_spec=pltpu.PrefetchScalarGridSpec(
            num_scalar_prefetch=1, grid=(1,),
            in_specs=[pl.BlockSpec((None,) + m.shape[1:], lambda s, cc_ref: (cc_ref[0], 0, 0)) for m in mines]
            + [pl.BlockSpec(g.shape, lambda s, cc_ref: (0, 0, 0)) for g in gots],
            out_specs=[pl.BlockSpec((None,) + m.shape[1:], lambda s, cc_ref: (cc_ref[1], 0, 0)) for m in mines]),
        compiler_params=_params(("arbitrary",)),
    )(chip_core, *_hbm(*mines, *gots)))


def _add_chips_into_grid(chip_core, mine, got, name):
    _, hr, cols = mine.shape

    def body(cc_ref, a_ref, b_ref, o_ref):
        o_ref[...] = ((a_ref[...] + b_ref[0]) + b_ref[1]) + b_ref[2]

    return pl.pallas_call(
        body, name=name,
        out_shape=pltpu.HBM((N_CHIPS, 2, hr, cols), F32),
        grid_spec=pltpu.PrefetchScalarGridSpec(
            num_scalar_prefetch=1, grid=(1,),
            in_specs=[pl.BlockSpec((None, hr, cols), lambda s, cc_ref: (cc_ref[0], 0, 0)),
                      pl.BlockSpec((3, hr, cols), lambda s, cc_ref: (0, 0, 0))],
            out_specs=pl.BlockSpec((None, None, hr, cols), lambda s, cc_ref: (cc_ref[0], cc_ref[1], 0, 0))),
        compiler_params=_params(("arbitrary",)),
    )(chip_core, *_hbm(mine, got))


def _place_shards(chip, shards):
    n = len(shards)

    def body(chip_ref, *refs):
        for w in range(n):
            refs[n + w][...] = refs[w][...]

    return pl.pallas_call(
        body, name="place_shards",
        out_shape=[pltpu.HBM((N_CHIPS,) + s.shape, s.dtype) for s in shards],
        grid_spec=pltpu.PrefetchScalarGridSpec(
            num_scalar_prefetch=1, grid=(1,),
            in_specs=[pl.BlockSpec(s.shape, lambda i, chip_ref: (0, 0)) for s in shards],
            out_specs=[pl.BlockSpec((None,) + s.shape, lambda i, chip_ref: (chip_ref[0], 0, 0)) for s in shards]),
        compiler_params=_params(("arbitrary",)),
    )(chip, *shards)


def _rope_tables(pos_col, freqs):
    S = pos_col.shape[0]
    T = _row_tile(S, 1024)

    def body(p_ref, f_ref, cos_ref, sin_ref):
        ang = p_ref[...].astype(F32) * f_ref[...]
        cos_ref[...] = jnp.cos(ang)
        sin_ref[...] = jnp.sin(ang)

    return pl.pallas_call(
        body, name="rope_tables", grid=(S // T,),
        out_shape=[pltpu.HBM((S, 128), F32)] * 2,
        in_specs=[pl.BlockSpec((T, 1), lambda i: (i, 0)), pl.BlockSpec((1, 128), lambda i: (0, 0))],
        out_specs=[pl.BlockSpec((T, 128), lambda i: (i, 0))] * 2,
        compiler_params=_params(("parallel",)),
    )(*_hbm(pos_col, freqs))


def _full(shape):
    zeros = (0,) * len(shape)
    return pl.BlockSpec(shape, lambda *_: zeros)


def _pre_attention(x, mod6, g_mix, g_q, g_kv, w_in, w_uq, w_uk_t, cos, sin, T, TQ):
    S = x.shape[0]

    def body(x_ref, mod_ref, gm_ref, gq_ref, gkv_ref, win_ref, wuq_ref, wuk_ref, cos_ref, sin_ref,
             proj_ref, q_ref, qc_ref, kc_ref, kct_ref):
        xh, _ = _rms(x_ref[...])
        h1 = ((xh * gm_ref[...]) * (1.0 + mod_ref[1:2, :]) + mod_ref[0:1, :]).astype(BF16)
        rows_in = D_MODEL // N_CHIPS
        proj = _dot_nt(h1[:, 0:rows_in], win_ref[0])
        for j in range(1, N_CHIPS):
            proj = proj + _dot_nt(h1[:, j * rows_in:(j + 1) * rows_in], win_ref[j])
        proj_ref[...] = proj
        cqh, _ = _rms(proj[:, :Q_LORA])
        c_q = cqh * gq_ref[...]
        ckvh, _ = _rms(proj[:, O_CKV:O_KR])
        c_kv = ckvh * gkv_ref[...]
        q = _dot(c_q, wuq_ref[...])
        q_ref[...] = q.astype(BF16)
        cos_t, sin_t = cos_ref[...], sin_ref[...]
        ropes = (_rope(q[:, O_QA:O_QB], cos_t, sin_t), _rope(q[:, O_QB:Q_W], cos_t, sin_t))
        low = lax.broadcasted_iota(jnp.int32, (T, 128), 1) < ROPE
        for h in range(HEADS):
            q_lat = _dot_nt(q[:, h * NOPE:(h + 1) * NOPE], wuk_ref[h])
            keep = low if h % 2 == 0 else jnp.logical_not(low)
            qc_ref[h, :, 0:KV_LORA] = q_lat.astype(BF16)
            qc_ref[h, :, KV_LORA:QK_PAD] = jnp.where(keep, ropes[h // 2], 0.0).astype(BF16)
        k_rope = _rope(proj[:, O_KR:O_U], cos_t, sin_t)
        kc_ref[:, 0:KV_LORA] = c_kv.astype(BF16)
        kc_ref[:, KV_LORA:QK_PAD] = k_rope.astype(BF16)
        lat_t, rope_t = jnp.transpose(c_kv), jnp.transpose(k_rope)
        for s in range(T // TQ):
            kct_ref[s, 0:KV_LORA, :] = lat_t[:, s * TQ:(s + 1) * TQ].astype(BF16)
            kct_ref[s, KV_LORA:QK_PAD, :] = rope_t[:, s * TQ:(s + 1) * TQ].astype(BF16)

    row = lambda w: pl.BlockSpec((T, w), lambda i: (i, 0))
    return pl.pallas_call(
        body, name="pre_attention", grid=(S // T,),
        out_shape=[pltpu.HBM((S, PROJ_W), F32), pltpu.HBM((S, Q_W), BF16), pltpu.HBM((HEADS, S, QK_PAD), BF16),
                   pltpu.HBM((S, QK_PAD), BF16), pltpu.HBM((S // TQ, QK_PAD, TQ), BF16)],
        in_specs=[row(D_MODEL), _full((N_MOD, D_MODEL)), _full((1, D_MODEL)), _full((1, Q_LORA)), _full((1, KV_LORA)),
                  _full((N_CHIPS, PROJ_W, D_MODEL // N_CHIPS)), _full((Q_LORA, Q_W)), _full((HEADS, KV_LORA, NOPE)),
                  row(128), row(128)],
        out_specs=[row(PROJ_W), row(Q_W), pl.BlockSpec((HEADS, T, QK_PAD), lambda i: (0, i, 0)), row(QK_PAD),
                   pl.BlockSpec((T // TQ, QK_PAD, TQ), lambda i: (i, 0, 0))],
        compiler_params=_params(("parallel",)),
    )(*_hbm(x, mod6, g_mix, g_q, g_kv, w_in, w_uq, w_uk_t, cos, sin))


def _diag_mask(TQ, width):
    key = lax.broadcasted_iota(jnp.int32, (TQ, width), 0) >> CHUNK_SHIFT
    qry = (lax.broadcasted_iota(jnp.int32, (TQ, width), 1) & (TQ - 1)) >> CHUNK_SHIFT
    return key <= qry


def _col_to_row(col):
    return jnp.transpose(jnp.broadcast_to(col, (col.shape[0], 128)))[0:1, :]


def _attention_fwd(qc, kc, kct, w_uv_t, TQ):
    S = kc.shape[0]
    R = HEADS * TQ
    nq = S // TQ

    def body(q_ref, k_ref, kt_ref, wuv_ref, o_ref, y_ref, lser_ref, m_s, l_s, acc_s, st_s):
        i = pl.program_id(0)
        q = q_ref[...].reshape(R, QK_PAD)
        m_s[...] = jnp.full((1, R), -jnp.inf, F32)
        l_s[...] = jnp.zeros((1, R), F32)
        acc_s[...] = jnp.zeros((KV_LORA, R), F32)

        def scores(j):
            return _dot_nt(k_ref[pl.ds(pl.multiple_of(j * TQ, TQ), TQ), :], q) * SM_SCALE

        def update(j, st):
            m_old = m_s[...]
            m_new = jnp.maximum(m_old, jnp.max(st, axis=0, keepdims=True))
            pt = jnp.exp(st - m_new)
            alpha = jnp.exp(m_old - m_new)
            l_s[...] = alpha * l_s[...] + jnp.sum(pt, axis=0, keepdims=True)
            acc_s[...] = alpha * acc_s[...] + _dot(kt_ref[j, 0:KV_LORA, :], pt)
            m_s[...] = m_new

        st_s[...] = scores(0)

        def loop(j, carry):
            st = st_s[...]
            st_s[...] = scores(j + 1)
            update(j, st)
            return carry

        lax.fori_loop(0, i, loop, 0)
        update(i, jnp.where(_diag_mask(TQ, R), st_s[...], -jnp.inf))
        l = l_s[...]
        lser_ref[0] = m_s[...] + jnp.log(l)
        o = jnp.transpose(acc_s[...] / l).astype(BF16)
        for h in range(HEADS):
            oh = o[h * TQ:(h + 1) * TQ, :]
            o_ref[h] = oh
            y_ref[:, h * 128:(h + 1) * 128] = _dot(oh, wuv_ref[h]).astype(BF16)

    return pl.pallas_call(
        body, name="attention_fwd", grid=(nq,),
        out_shape=[pltpu.HBM((HEADS, S, KV_LORA), BF16), pltpu.HBM((S, HEADS * 128), BF16),
                   pltpu.HBM((nq, 1, R), F32)],
        in_specs=[pl.BlockSpec((HEADS, TQ, QK_PAD), lambda i: (0, i, 0)), _full((S, QK_PAD)),
                  _full((nq, QK_PAD, TQ)), _full((HEADS, KV_LORA, 128))],
        out_specs=[pl.BlockSpec((HEADS, TQ, KV_LORA), lambda i: (0, i, 0)), pl.BlockSpec((TQ, HEADS * 128), lambda i: (i, 0)),
                   pl.BlockSpec((1, 1, R), lambda i: (i, 0, 0))],
        scratch_shapes=[pltpu.VMEM((1, R), F32), pltpu.VMEM((1, R), F32), pltpu.VMEM((KV_LORA, R), F32),
                        pltpu.VMEM((TQ, R), F32)],
        compiler_params=_params(("parallel",)),
    )(*_hbm(qc, kc, kct, w_uv_t))


def _pool_forward(proj):
    S = proj.shape[0]
    RB = _row_tile(S, 256)

    def body(proj_ref, out_ref, pad_ref, sem):
        cp = pltpu.make_async_copy(proj_ref.at[:, pl.ds(O_U, POOL_W)], pad_ref.at[pl.ds(POOL_PAD, S)], sem)
        cp.start()
        pad_ref[0:POOL_PAD, :] = jnp.zeros((POOL_PAD, POOL_W), F32)
        cp.wait()
        for g, win in enumerate(POOL_WINDOWS):
            cols = slice(g * POOL_GROUP, (g + 1) * POOL_GROUP)
            for r0 in range(0, S, RB):
                u = pad_ref[POOL_PAD + r0:POOL_PAD + r0 + RB, cols]
                acc = u
                for k in range(1, win):
                    acc = acc + pad_ref[POOL_PAD + r0 - k:POOL_PAD + r0 - k + RB, cols]
                if r0 == 0:
                    t1 = (lax.broadcasted_iota(jnp.int32, (RB, POOL_GROUP), 0) + 1).astype(F32)
                    mean = acc / jnp.minimum(t1, float(win))
                else:
                    mean = acc * (1.0 / win)
                out_ref[r0:r0 + RB, cols] = (mean - u).astype(BF16)

    return pl.pallas_call(
        body, name="pool_forward",
        out_shape=jax.ShapeDtypeStruct((S, POOL_W), BF16),
        in_specs=[ANY], out_specs=VMEM_SPEC,
        scratch_shapes=[pltpu.VMEM((S + POOL_PAD, POOL_W), F32), pltpu.SemaphoreType.DMA],
        compiler_params=_params(),
    )(proj)


def _pool_backward(dpooled, after):
    S = dpooled.shape[0]
    RB = _row_tile(S, 256)

    def body(dp_ref, after_ref, out_ref, pad_ref, sem):
        cp = pltpu.make_async_copy(dp_ref, pad_ref.at[pl.ds(0, S)], sem)
        cp.start()
        pad_ref[S:S + POOL_PAD, :] = jnp.zeros((POOL_PAD, POOL_W), F32)
        cp.wait()
        for g, win in enumerate(POOL_WINDOWS):
            cols = slice(g * POOL_GROUP, (g + 1) * POOL_GROUP)
            head = pad_ref[0:POOL_PAD, cols]
            t1 = (lax.broadcasted_iota(jnp.int32, (POOL_PAD, POOL_GROUP), 0) + 1).astype(F32)
            pad_ref[0:POOL_PAD, cols] = head * (float(win) / jnp.minimum(t1, float(win)))
            for r0 in range(0, S, RB):
                acc = pad_ref[r0:r0 + RB, cols]
                for k in range(1, win):
                    acc = acc + pad_ref[r0 + k:r0 + k + RB, cols]
                own = pad_ref[r0:r0 + RB, cols]
                if r0 == 0:
                    own = jnp.concatenate([head, own[POOL_PAD:]], axis=0)
                out_ref[r0:r0 + RB, cols] = (acc * (1.0 / win) - own).astype(BF16)

    return pl.pallas_call(
        body, name="pool_backward",
        out_shape=jax.ShapeDtypeStruct((S, POOL_W), BF16),
        in_specs=[ANY, ANY], out_specs=VMEM_SPEC,
        scratch_shapes=[pltpu.VMEM((S + POOL_PAD, POOL_W), F32), pltpu.SemaphoreType.DMA],
        compiler_params=_params(),
    )(dpooled, after)


def _mix_out(y_mla, pooled, w_pool, pool_scale, w_o, x, mod6, T):
    S = x.shape[0]

    def body(ym_ref, pl_ref, wp_ref, ps_ref, wo_ref, x_ref, mod_ref, x1_ref, mix_ref, mi_ref):
        mi_ref[:, 0:512] = ym_ref[...]
        for g in range(len(POOL_WINDOWS)):
            cols = slice(g * POOL_GROUP, (g + 1) * POOL_GROUP)
            z = _dot(pl_ref[:, cols], wp_ref[g])
            mi_ref[:, 512 + g * POOL_GROUP:512 + (g + 1) * POOL_GROUP] = (z * ps_ref[:, cols]).astype(BF16)
        mix = _dot(mi_ref[...], wo_ref[...])
        mix_ref[...] = mix.astype(BF16)
        x1_ref[...] = x_ref[...] + mod_ref[2:3, :] * mix

    row = lambda w: pl.BlockSpec((T, w), lambda i: (i, 0))
    return pl.pallas_call(
        body, name="mix_out", grid=(S // T,),
        out_shape=[pltpu.HBM((S, D_MODEL), F32), pltpu.HBM((S, D_MODEL), BF16), pltpu.HBM((S, 1024), BF16)],
        in_specs=[row(512), row(POOL_W), _full((4, POOL_GROUP, POOL_GROUP)), _full((1, POOL_W)),
                  _full((1024, D_MODEL)), row(D_MODEL), _full((N_MOD, D_MODEL))],
        out_specs=[row(D_MODEL), row(D_MODEL), row(1024)],
        compiler_params=_params(("parallel",)),
    )(*_hbm(y_mla, pooled, w_pool, pool_scale, w_o, x, mod6))


def _ffn_forward(x1, mod6, g_ffn, g_final, target, w_gate, w_up, w_down, T):
    S = x1.shape[0]

    def body(x1_ref, mod_ref, gf_ref, gl_ref, tgt_ref, wg_ref, wu_ref, wd_ref,
             gate_ref, up_ref, act_ref, h2_ref, dff_ref, dx2_ref, st_ref, acc_s):
        i, j = pl.program_id(0), pl.program_id(1)

        @pl.when(jnp.logical_and(i == 0, j == 0))
        def _():
            st_ref[...] = jnp.zeros_like(st_ref)

        @pl.when(j == 0)
        def _():
            xh, _ = _rms(x1_ref[...])
            h2_ref[...] = ((xh * gf_ref[...]) * (1.0 + mod_ref[4:5, :]) + mod_ref[3:4, :]).astype(BF16)
            acc_s[...] = jnp.zeros_like(acc_s)

        h2 = h2_ref[...]
        gate = _dot_nt(h2, wg_ref[j])
        up = _dot_nt(h2, wu_ref[j])
        gate_ref[...] = gate.astype(BF16)
        up_ref[...] = up.astype(BF16)
        act = (gate * jax.nn.sigmoid(gate) * up).astype(BF16)
        act_ref[...] = act
        acc_s[...] += _dot(act, wd_ref[j])

        @pl.when(j == N_CHIPS - 1)
        def _():
            ff = acc_s[...]
            x2 = x1_ref[...] + mod_ref[5:6, :] * ff
            xh, r3 = _rms(x2)
            err = xh * gl_ref[...] - tgt_ref[...]
            dy = err * (1.0 / D_MODEL)
            dx2 = _rms_bwd(dy * gl_ref[...], xh, r3)
            dx2_ref[...] = dx2
            dff_ref[...] = (dx2 * mod_ref[5:6, :]).astype(BF16)
            st_ref[0:1, :] += jnp.sum(dy * xh, axis=0, keepdims=True)
            st_ref[1:2, :] += jnp.sum(dx2 * ff, axis=0, keepdims=True)
            st_ref[2:3, :] += 0.5 * jnp.sum(err * dy)

    row = pl.BlockSpec((T, D_MODEL), lambda i, j: (i, 0))
    chunk_out = pl.BlockSpec((None, T, FF_CHUNK), lambda i, j: (j, i, 0))
    big = pltpu.HBM((N_CHIPS, S, FF_CHUNK), BF16)
    wide = pltpu.HBM((S, D_MODEL), BF16)
    return pl.pallas_call(
        body, name="ffn_forward", grid=(S // T, N_CHIPS),
        out_shape=[big, big, big, wide, wide, pltpu.HBM((S, D_MODEL), F32), jax.ShapeDtypeStruct((8, D_MODEL), F32)],
        in_specs=[row, _full((N_MOD, D_MODEL)), _full((1, D_MODEL)), _full((1, D_MODEL)), row,
                  VMEM_SPEC, VMEM_SPEC, VMEM_SPEC],
        out_specs=[chunk_out, chunk_out, chunk_out, row, row, row, _full((8, D_MODEL))],
        scratch_shapes=[pltpu.VMEM((T, D_MODEL), F32)],
        compiler_params=_params(("arbitrary", "arbitrary")),
    )(*_hbm(x1, mod6, g_ffn, g_final, target), w_gate, w_up, w_down)


def _ffn_backward(dx2, x1, dff, gate, up, mod6, g_ffn, w_gate, w_up, w_down, T):
    S = x1.shape[0]

    def body(dx2_ref, x1_ref, dff_ref, gate_ref, up_ref, mod_ref, gf_ref, wg_ref, wu_ref, wd_ref,
             dgate_ref, dup_ref, dx1_ref, st_ref, acc_s):
        i, j = pl.program_id(0), pl.program_id(1)

        @pl.when(jnp.logical_and(i == 0, j == 0))
        def _():
            st_ref[...] = jnp.zeros_like(st_ref)

        @pl.when(j == 0)
        def _():
            acc_s[...] = jnp.zeros_like(acc_s)

        for r0 in range(0, T, T // 2):
            rows = slice(r0, r0 + T // 2)
            gate, up = gate_ref[rows, :].astype(F32), up_ref[rows, :].astype(F32)
            sg = jax.nn.sigmoid(gate)
            dact = _dot_nt(dff_ref[rows, :], wd_ref[j])
            dup = (dact * (gate * sg)).astype(BF16)
            dgate = (dact * up * (sg * (1.0 + gate * (1.0 - sg)))).astype(BF16)
            dup_ref[rows, :] = dup
            dgate_ref[rows, :] = dgate
            acc_s[rows, :] += _dot(dgate, wg_ref[j]) + _dot(dup, wu_ref[j])

        @pl.when(j == N_CHIPS - 1)
        def _():
            dh2 = acc_s[...]
            xh, r2 = _rms(x1_ref[...])
            n2 = xh * gf_ref[...]
            st_ref[0:1, :] += jnp.sum(dh2, axis=0, keepdims=True)
            st_ref[1:2, :] += jnp.sum(dh2 * n2, axis=0, keepdims=True)
            dn2 = dh2 * (1.0 + mod_ref[4:5, :])
            st_ref[2:3, :] += jnp.sum(dn2 * xh, axis=0, keepdims=True)
            dx1_ref[...] = _rms_bwd(dn2 * gf_ref[...], xh, r2) + dx2_ref[...]

    row = pl.BlockSpec((T, D_MODEL), lambda i, j: (i, 0))
    chunk = pl.BlockSpec((None, T, FF_CHUNK), lambda i, j: (j, i, 0))
    big = pltpu.HBM((N_CHIPS, S, FF_CHUNK), BF16)
    return pl.pallas_call(
        body, name="ffn_backward", grid=(S // T, N_CHIPS),
        out_shape=[big, big, pltpu.HBM((S, D_MODEL), F32), jax.ShapeDtypeStruct((8, D_MODEL), F32)],
        in_specs=[row, row, row, chunk, chunk, _full((N_MOD, D_MODEL)), _full((1, D_MODEL)),
                  VMEM_SPEC, VMEM_SPEC, VMEM_SPEC],
        out_specs=[chunk, chunk, row, _full((8, D_MODEL))],
        scratch_shapes=[pltpu.VMEM((T, D_MODEL), F32)],
        compiler_params=_params(("arbitrary", "arbitrary")),
    )(*_hbm(dx2, x1, dff, gate, up, mod6, g_ffn), w_gate, w_up, w_down)


def _tn_matmul(a, b, a_spec, b_spec, groups, m, n, steps, name):
    def body(a_ref, b_ref, o_ref):
        @pl.when(pl.program_id(1) == 0)
        def _():
            o_ref[...] = jnp.zeros_like(o_ref)

        o_ref[...] += _dot_tn(a_ref[...], b_ref[...])

    return pl.pallas_call(
        body, name=name, grid=(groups, steps),
        out_shape=pltpu.HBM((groups, m, n), F32),
        in_specs=[a_spec, b_spec],
        out_specs=pl.BlockSpec((None, m, n), lambda g, i: (g, 0, 0)),
        compiler_params=_params(("parallel", "arbitrary")),
    )(*_hbm(a, b))


def _mix_backward(dx1, mix, mod6, w_o, pooled, w_pool, pool_scale, w_uv_t, o_lat, T, TQ):
    S = dx1.shape[0]

    def body(dx1_ref, mix_ref, mod_ref, wo_ref, pl_ref, wp_ref, ps_ref, wuv_ref, o_ref,
             dmix_ref, dp_ref, do_ref, dr_ref, gp_ref, guv_ref, st_ref):
        @pl.when(pl.program_id(0) == 0)
        def _():
            st_ref[...] = jnp.zeros_like(st_ref)
            gp_ref[...] = jnp.zeros_like(gp_ref)
            guv_ref[...] = jnp.zeros_like(guv_ref)

        dx1 = dx1_ref[...]
        st_ref[0:1, :] += jnp.sum(dx1 * mix_ref[...].astype(F32), axis=0, keepdims=True)
        dmix = (dx1 * mod_ref[2:3, :]).astype(BF16)
        dmix_ref[...] = dmix
        dmi = _dot_nt(dmix, wo_ref[...])
        dym = dmi[:, 0:512].astype(BF16)
        for g in range(len(POOL_WINDOWS)):
            cols = slice(g * POOL_GROUP, (g + 1) * POOL_GROUP)
            dyp = dmi[:, 512 + g * POOL_GROUP:512 + (g + 1) * POOL_GROUP]
            pooled_g = pl_ref[:, cols]
            z = _dot(pooled_g, wp_ref[g])
            st_ref[1:2, cols] += jnp.sum(dyp * z, axis=0, keepdims=True)
            dz = (dyp * ps_ref[:, cols]).astype(BF16)
            gp_ref[g] += _dot_tn(pooled_g, dz)
            dp_ref[:, cols] = _dot_nt(dz, wp_ref[g])
        for h in range(HEADS):
            dym_h = dym[:, h * 128:(h + 1) * 128]
            do = _dot_nt(dym_h, wuv_ref[h]).astype(BF16)
            do_ref[h] = do
            o_h = o_ref[h]
            guv_ref[h] += _dot_tn(o_h, dym_h)
            delta = _col_to_row(jnp.sum(do.astype(F32) * o_h.astype(F32), axis=1, keepdims=True))
            for s in range(T // TQ):
                dr_ref[s, :, h * TQ:(h + 1) * TQ] = delta[:, s * TQ:(s + 1) * TQ]

    row = lambda w: pl.BlockSpec((T, w), lambda i: (i, 0))
    heads = pl.BlockSpec((HEADS, T, KV_LORA), lambda i: (0, i, 0))
    square = jax.ShapeDtypeStruct((4, 128, 128), F32)
    return pl.pallas_call(
        body, name="mix_backward", grid=(S // T,),
        out_shape=[pltpu.HBM((S, D_MODEL), BF16), pltpu.HBM((S, POOL_W), F32), pltpu.HBM((HEADS, S, KV_LORA), BF16),
                   pltpu.HBM((S // TQ, 1, HEADS * TQ), F32), square, square, jax.ShapeDtypeStruct((8, D_MODEL), F32)],
        in_specs=[row(D_MODEL), row(D_MODEL), _full((N_MOD, D_MODEL)), _full((1024, D_MODEL)), row(POOL_W),
                  _full((4, POOL_GROUP, POOL_GROUP)), _full((1, POOL_W)), _full((HEADS, KV_LORA, 128)), heads],
        out_specs=[row(D_MODEL), row(POOL_W), heads,
                   pl.BlockSpec((T // TQ, 1, HEADS * TQ), lambda i: (i, 0, 0)), _full((4, 128, 128)),
                   _full((4, 128, 128)), _full((8, D_MODEL))],
        compiler_params=_params(("arbitrary",)),
    )(*_hbm(dx1, mix, mod6, w_o, pooled, w_pool, pool_scale, w_uv_t, o_lat))


def _attention_bwd(qc, kc, kct, do, lse_rows, delta_rows, TQ):
    S = kc.shape[0]
    R = HEADS * TQ
    nq = S // TQ

    def body(q_ref, do_ref, lser_ref, dr_ref, k_ref, kt_ref, dqt_ref, dk_ref, dqt_s, dv_s):
        i = pl.program_id(0)

        def key_rows(j):
            return pl.ds(pl.multiple_of(j * TQ, TQ), TQ)

        @pl.when(i == 0)
        def _():
            def zero(j, carry):
                dk_ref[key_rows(j), :] = jnp.zeros((TQ, QK_PAD), F32)
                dv_s[key_rows(j), :] = jnp.zeros((TQ, KV_LORA), F32)
                return carry
            lax.fori_loop(0, nq, zero, 0)

        q = q_ref[...].reshape(R, QK_PAD)
        do = do_ref[...].reshape(R, KV_LORA)
        lse, delta = lser_ref[0], dr_ref[0]
        dqt_s[...] = jnp.zeros((QK_PAD, R), F32)

        def step(j, masked):
            rows = key_rows(j)
            k = k_ref[rows, :]
            st = _dot_nt(k, q) * SM_SCALE
            if masked:
                st = jnp.where(_diag_mask(TQ, R), st, -jnp.inf)
            pt = jnp.exp(st - lse)
            dv_s[rows, :] += _dot(pt, do)
            dpt = _dot_nt(k[:, :KV_LORA], do)
            dst = (pt * (dpt - delta)).astype(BF16)
            dk_ref[rows, :] += _dot(dst, q)
            dqt_s[...] += _dot(kt_ref[j], dst)

        def loop(j, carry):
            step(j, False)
            return carry

        lax.fori_loop(0, i, loop, 0)
        step(i, True)
        dqt_ref[...] = dqt_s[...]

        @pl.when(i == nq - 1)
        def _():
            def finish(j, carry):
                rows = key_rows(j)
                dk = dk_ref[rows, :] * SM_SCALE
                dk_ref[rows, 0:KV_LORA] = dk[:, 0:KV_LORA] + dv_s[rows, :]
                dk_ref[rows, KV_LORA:QK_PAD] = dk[:, KV_LORA:QK_PAD]
                return carry
            lax.fori_loop(0, nq, finish, 0)

    tile = lambda w: pl.BlockSpec((HEADS, TQ, w), lambda i: (0, i, 0))
    row = pl.BlockSpec((1, 1, R), lambda i: (i, 0, 0))
    return pl.pallas_call(
        body, name="attention_bwd", grid=(nq,),
        out_shape=[pltpu.HBM((nq, QK_PAD, R), F32), jax.ShapeDtypeStruct((S, QK_PAD), F32)],
        in_specs=[tile(QK_PAD), tile(KV_LORA), row, row, VMEM_SPEC, VMEM_SPEC],
        out_specs=[pl.BlockSpec((None, QK_PAD, R), lambda i: (i, 0, 0)), VMEM_SPEC],
        scratch_shapes=[pltpu.VMEM((QK_PAD, R), F32), pltpu.VMEM((S, KV_LORA), F32)],
        compiler_params=_params(("arbitrary",)),
    )(*_hbm(qc, do, lse_rows, delta_rows), kc, kct)[::-1]


def _pre_attention_backward(x, dx1, proj, q, dqt, dkc, du, cos, sin, mod6, g_mix, g_q, g_kv, w_in, w_uq, w_uk_t, T, TQ):
    S = x.shape[0]

    def body(x_ref, dx1_ref, proj_ref, q_ref, dqt_ref, dkc_ref, du_ref, cos_ref, sin_ref, mod_ref, gm_ref, gq_ref,
             gkv_ref, win_ref, wuq_ref, wuk_ref, gx_ref, dproj_ref, h1_ref, guk_ref, guq_ref, st_ref, dq_ref):
        @pl.when(pl.program_id(0) == 0)
        def _():
            st_ref[...] = jnp.zeros_like(st_ref)
            guk_ref[...] = jnp.zeros_like(guk_ref)
            guq_ref[...] = jnp.zeros_like(guq_ref)

        cos_t, sin_t = cos_ref[...], sin_ref[...]
        low = lax.broadcasted_iota(jnp.int32, (T, 128), 1) < ROPE
        rope_parts = []
        for h in range(HEADS):
            dqc = jnp.concatenate([jnp.transpose(dqt_ref[s, :, h * TQ:(h + 1) * TQ]) for s in range(T // TQ)], axis=0)
            dqc = dqc * SM_SCALE
            dql = dqc[:, 0:KV_LORA].astype(BF16)
            guk_ref[h] += _dot_tn(dql, q_ref[:, h * NOPE:(h + 1) * NOPE])
            dq_ref[:, h * NOPE:(h + 1) * NOPE] = _dot(dql, wuk_ref[h]).astype(BF16)
            rope_parts.append(dqc[:, KV_LORA:QK_PAD])
        for pair in range(2):
            d = jnp.where(low, rope_parts[2 * pair], rope_parts[2 * pair + 1])
            dq_ref[:, O_QA + 128 * pair:O_QA + 128 * (pair + 1)] = _rope_bwd(d, cos_t, sin_t).astype(BF16)
        dq = dq_ref[...]
        dcq = _dot_nt(dq, wuq_ref[...])
        cqh, rq = _rms(proj_ref[:, 0:Q_LORA])
        guq_ref[...] += _dot_tn(cqh * gq_ref[...], dq)
        st_ref[3:4, 0:Q_LORA] += jnp.sum(dcq * cqh, axis=0, keepdims=True)
        dproj_ref[:, 0:Q_LORA] = _rms_bwd(dcq * gq_ref[...], cqh, rq).astype(BF16)
        dckv = dkc_ref[:, 0:KV_LORA]
        ckvh, rkv = _rms(proj_ref[:, O_CKV:O_KR])
        st_ref[4:5, 0:KV_LORA] += jnp.sum(dckv * ckvh, axis=0, keepdims=True)
        dproj_ref[:, O_CKV:O_KR] = _rms_bwd(dckv * gkv_ref[...], ckvh, rkv).astype(BF16)
        dkr = _rope_bwd(dkc_ref[:, KV_LORA:QK_PAD], cos_t, sin_t)
        dkr = jnp.where(low, dkr + pltpu.roll(dkr, ROPE, 1), 0.0)
        dproj_ref[:, O_KR:O_U] = dkr.astype(BF16)
        dproj_ref[:, O_U:PROJ_W] = du_ref[...].astype(BF16)
        dproj = dproj_ref[...]
        dh1 = jnp.concatenate([_dot(dproj, win_ref[j]) for j in range(N_CHIPS)], axis=1)
        xh, r1 = _rms(x_ref[...])
        n1 = xh * gm_ref[...]
        h1_ref[...] = (n1 * (1.0 + mod_ref[1:2, :]) + mod_ref[0:1, :]).astype(BF16)
        st_ref[0:1, :] += jnp.sum(dh1, axis=0, keepdims=True)
        st_ref[1:2, :] += jnp.sum(dh1 * n1, axis=0, keepdims=True)
        dn1 = dh1 * (1.0 + mod_ref[1:2, :])
        st_ref[2:3, :] += jnp.sum(dn1 * xh, axis=0, keepdims=True)
        gx_ref[...] = _rms_bwd(dn1 * gm_ref[...], xh, r1) + dx1_ref[...]

    row = lambda w: pl.BlockSpec((T, w), lambda i: (i, 0))
    return pl.pallas_call(
        body, name="pre_attention_backward", grid=(S // T,),
        out_shape=[jax.ShapeDtypeStruct((S, D_MODEL), F32), pltpu.HBM((S, PROJ_W), BF16),
                   pltpu.HBM((S, D_MODEL), BF16), jax.ShapeDtypeStruct((HEADS, KV_LORA, NOPE), F32),
                   jax.ShapeDtypeStruct((Q_LORA, Q_W), F32), jax.ShapeDtypeStruct((8, D_MODEL), F32)],
        in_specs=[row(D_MODEL), row(D_MODEL), row(O_KR), row(HEADS * NOPE),
                  pl.BlockSpec((T // TQ, QK_PAD, HEADS * TQ), lambda i: (i, 0, 0)),
                  row(QK_PAD), row(POOL_W), row(128), row(128), _full((N_MOD, D_MODEL)), _full((1, D_MODEL)),
                  _full((1, Q_LORA)), _full((1, KV_LORA)), _full((N_CHIPS, PROJ_W, D_MODEL // N_CHIPS)),
                  _full((Q_LORA, Q_W)), _full((HEADS, KV_LORA, NOPE))],
        out_specs=[row(D_MODEL), row(PROJ_W), row(D_MODEL), _full((HEADS, KV_LORA, NOPE)), _full((Q_LORA, Q_W)),
                   _full((8, D_MODEL))],
        scratch_shapes=[pltpu.VMEM((T, Q_W), BF16)],
        compiler_params=_params(("arbitrary",)),
    )(*_hbm(x, dx1, proj, q, dqt, dkc, du, cos, sin, mod6, g_mix, g_q, g_kv, w_in, w_uq, w_uk_t))


def _ada_grads(c_all, dmod_all, chip):
    cols = N_MOD * D_MODEL // N_CHIPS
    width = dmod_all.shape[1]

    def body(col_ref, c_ref, dcol_ref, dall_ref, gw_ref, gb_ref):
        call = c_ref[...]
        act = call * jax.nn.sigmoid(call)
        gw_ref[...] = _dot_tn(act, dcol_ref[...])
        d = dall_ref[...]
        acc = d[0:1, :]
        for b in range(1, 8):
            acc = acc + d[b:b + 1, :]
        gb_ref[...] = acc

    return pl.pallas_call(
        body, name="ada_grads",
        out_shape=[jax.ShapeDtypeStruct((D_MODEL, cols), F32), jax.ShapeDtypeStruct((1, width), F32)],
        grid_spec=pltpu.PrefetchScalarGridSpec(
            num_scalar_prefetch=1, grid=(1,),
            in_specs=[pl.BlockSpec((8, D_MODEL), lambda s, col_ref: (0, 0)),
                      pl.BlockSpec((8, cols), lambda s, col_ref: (0, col_ref[0])),
                      pl.BlockSpec((8, width), lambda s, col_ref: (0, 0))],
            out_specs=[pl.BlockSpec((D_MODEL, cols), lambda s, col_ref: (0, 0)),
                       pl.BlockSpec((1, width), lambda s, col_ref: (0, 0))]),
        compiler_params=_params(("arbitrary",)),
    )(chip, *_hbm(c_all, dmod_all, dmod_all))


def _adamw(w, g, m, v, name, g_is_landing_zone=True):
    rows, rest = w.shape[0], w.shape[1:]
    row_elems = 1
    for d in rest:
        row_elems *= d
    T = _row_tile(rows, 512 if row_elems <= 1024 else 256)

    def body(w_ref, g_ref, m_ref, v_ref, *outs):
        d_ref, nm_ref, nv_ref = outs[-3:]
        g = g_ref[...]
        if g_is_landing_zone:
            outs[0][...] = g
        m2 = ADAM_B1 * m_ref[...] + (1.0 - ADAM_B1) * g
        v2 = ADAM_B2 * v_ref[...] + (1.0 - ADAM_B2) * (g * g)
        m_hat = m2 / (1.0 - ADAM_B1 ** ADAM_STEP)
        v_hat = v2 / (1.0 - ADAM_B2 ** ADAM_STEP)
        d_ref[...] = -ADAM_LR * (m_hat / (jnp.sqrt(v_hat) + ADAM_EPS) + ADAM_WD * w_ref[...])
        nm_ref[...] = m2
        nv_ref[...] = v2

    zeros = (0,) * len(rest)
    spec = pl.BlockSpec((T,) + rest, lambda i: (i,) + zeros)
    n_out = 4 if g_is_landing_zone else 3
    res = pl.pallas_call(
        body, name=name, grid=(rows // T,),
        out_shape=[jax.ShapeDtypeStruct(w.shape, F32)] * n_out,
        in_specs=[spec] * 4, out_specs=[spec] * n_out,
        compiler_params=_params(("parallel",)),
    )(*_hbm(w, g, m, v))
    return res if g_is_landing_zone else [g] + list(res)


SMALL_NAMES = ("w_uk", "w_uv", "w_pool", "g_mix", "g_q", "g_kv", "pool_scale", "g_ffn", "g_final", "b_ada")
SMALL_ROWS = 1664


def _pack_rows(parts):
    flat = jnp.concatenate([p.reshape(-1) for p in parts])
    pad = (-flat.shape[0]) % 128
    if pad:
        flat = jnp.concatenate([flat, jnp.zeros((pad,), F32)])
    return flat.reshape(-1, 128)


def kernel(x, c, positions, w_ada, b_ada, g_mix, w_in, g_q, g_kv, w_uq, w_uk, w_uv, w_pool, pool_scale, w_o, g_ffn, w_gate, w_up, w_down, g_final, loss_target, m_w_ada, m_b_ada, m_g_mix, m_w_in, m_g_q, m_g_kv, m_w_uq, m_w_uk, m_w_uv, m_w_pool, m_pool_scale, m_w_o, m_g_ffn, m_w_gate, m_w_up, m_w_down, m_g_final, v_w_ada, v_b_ada, v_g_mix, v_w_in, v_g_q, v_g_kv, v_w_uq, v_w_uk, v_w_uv, v_w_pool, v_pool_scale, v_w_o, v_g_ffn, v_w_gate, v_w_up, v_w_down, v_g_final):
    S = x.shape[1]
    T = _row_tile(S, 512)
    TQ = _row_tile(S, 512)
    TW = _row_tile(S, 4096)
    ix, iy, ic = lax.axis_index("x"), lax.axis_index("y"), lax.axis_index("c")
    chip = (2 * ix + iy).astype(jnp.int32)
    chip_arr = chip.reshape(1)
    core_arr = ic.astype(jnp.int32).reshape(1)

    xs, tgt = x[0], loss_target[0]

    tr = lambda a: jnp.transpose(a[0])
    win_t = tr(w_in)
    win_p = jnp.concatenate([win_t[:O_KR + ROPE], win_t[O_KR:O_KR + ROPE], win_t[O_KR + ROPE:]], axis=0).astype(BF16)
    wuq = w_uq[0]
    wuq_p = jnp.concatenate([wuq[:, h, :NOPE] for h in range(HEADS)] + [wuq[:, h, NOPE:] for h in range(HEADS)],
                            axis=1).astype(BF16)
    w_uk_t = jnp.transpose(w_uk[0], (1, 0, 2)).astype(BF16)
    w_uv_t = jnp.transpose(w_uv[0], (1, 0, 2)).astype(BF16)
    w_pool_b = w_pool[0].astype(BF16)
    first = [win_p, wuq_p]
    later = [w_o[0].astype(BF16), tr(w_gate).astype(BF16), tr(w_up).astype(BF16), w_down[0].astype(BF16)]
    placed = _place_shards(chip_arr, first + later)
    a_send, a_recv, a_lands, token = _split_start("first_weights_start", first, placed[:2], 6, _plan_gather_start)
    half = ROPE // 2
    freqs = jnp.power(ROPE_THETA, -jnp.arange(half, dtype=F32) / half)
    cos, sin = _rope_tables(positions.reshape(S, 1), jnp.tile(freqs, 4).reshape(1, 128) + token[0, 0])
    a_send, a_recv, a_lands, token = _split_relay(
        "first_weights_relay", a_send, a_recv, first, a_lands, cos, 6, _plan_gather_landed, _plan_gather_relay)

    ada_cols = w_ada.shape[2]
    b_cols = lax.dynamic_slice(b_ada, (0, chip * ada_cols), (1, ada_cols))
    mod, c_all = _mod_exchange(c, w_ada[0], b_cols + token[0, 0])
    mod6 = mod.reshape(N_MOD, D_MODEL)
    a_lands = _split_wait("first_weights_wait", a_send, a_recv, [], a_lands, mod, _plan_gather_wait)
    w_in_f = a_lands[0]
    w_uq_f = a_lands[1].reshape(Q_LORA, Q_W)
    wg_lands, mod6, w_in_f = lax.optimization_barrier((placed[2:], mod6, w_in_f))
    wg_send, wg_recv, wg_lands, token = _split_start(
        "weights_start", later, wg_lands, 3 * len(later), _plan_gather_start)
    mod6 = mod6 + token[0, 0]

    proj, q, qc, kc, kct = _pre_attention(xs, mod6, g_mix, g_q, g_kv, w_in_f, w_uq_f, w_uk_t, cos, sin,
                                          _row_tile(S, 1024), TQ)
    o_lat, y_mla, lse_rows = _attention_fwd(qc, kc, kct, w_uv_t, TQ)
    wg_send, wg_recv, wg_lands, token = _split_relay(
        "weights_relay", wg_send, wg_recv, later, wg_lands, y_mla, 3 * len(later), _plan_gather_landed,
        _plan_gather_relay)
    pooled = _pool_forward(proj)
    wg_lands = _split_wait("weights_wait", wg_send, wg_recv, [], wg_lands, pooled, _plan_gather_wait)
    w_o_f = wg_lands[0].reshape(1024, D_MODEL)
    w_gate_f, w_up_f, w_down_f = wg_lands[1], wg_lands[2], wg_lands[3]
    x1, mix, mix_in = _mix_out(y_mla, pooled, w_pool_b, pool_scale, w_o_f, xs, mod6, _row_tile(S, 1024))
    gate, up, act, h2, dff, dx2, st_f = _ffn_forward(
        x1, mod6, g_ffn, g_final.reshape(1, D_MODEL), tgt, w_gate_f, w_up_f, w_down_f, T)

    dgate, dup, dx1, st_b = _ffn_backward(dx2, x1, dff, gate, up, mod6, g_ffn, w_gate_f, w_up_f, w_down_f, T)
    steps = S // TW
    chunk_spec = pl.BlockSpec((None, TW, FF_CHUNK), lambda g, i: (g, i, 0))
    wide_spec = pl.BlockSpec((TW, D_MODEL), lambda g, i: (i, 0))
    g_down = _tn_matmul(act, dff, chunk_spec, wide_spec, N_CHIPS, FF_CHUNK, D_MODEL, steps, "grad_w_down")
    g_gate = _tn_matmul(dgate, h2, chunk_spec, wide_spec, N_CHIPS, FF_CHUNK, D_MODEL, steps, "grad_w_gate")
    g_up = _tn_matmul(dup, h2, chunk_spec, wide_spec, N_CHIPS, FF_CHUNK, D_MODEL, steps, "grad_w_up")

    half_shapes = lambda gs: [jax.ShapeDtypeStruct((N_CHIPS, g.shape[1] // 2, g.shape[2]), F32) for g in gs]
    ffn_grads = [g_gate, g_up, g_down]
    f_send, f_recv, f_lands, token = _split_start(
        "ffn_swap_start", ffn_grads, half_shapes(ffn_grads), len(ffn_grads), _plan_swap_start)
    dmix, dpooled, do_lat, delta_rows, g_pool, g_uv_t, st_m = _mix_backward(
        dx1, mix, mod6 + token[0, 0], w_o_f, pooled, w_pool_b, pool_scale, w_uv_t, o_lat, _row_tile(S, 1024), TQ)
    g_o = [_tn_matmul(mix_in, dmix, wide_spec, wide_spec, 1, 1024, D_MODEL, steps, "grad_w_o").reshape(N_CHIPS, -1, D_MODEL)]
    o_send, o_recv, o_lands, token = _split_start("w_o_swap_start", g_o, half_shapes(g_o), 1, _plan_swap_start)
    du = _pool_backward(dpooled, token)
    f_got = _split_wait("ffn_swap_wait", f_send, f_recv, ffn_grads, f_lands, du, _plan_swap_wait)
    f_got += _split_wait("w_o_swap_wait", o_send, o_recv, g_o, o_lands, du, _plan_swap_wait)
    far_grads = ffn_grads + g_o
    f_sums = _add_my_halves(core_arr, far_grads, f_got, "add_half_far")
    f_send, f_recv, f_lands, token = _split_start(
        "far_exchange_start", f_sums, [jax.ShapeDtypeStruct((3,) + s.shape[1:], F32) for s in f_sums],
        3 * len(f_sums), _plan_exchange_start)
    delta_rows = delta_rows + token[0, 0]
    dkc, dqt = _attention_bwd(qc, kc, kct, do_lat, lse_rows, delta_rows, TQ)
    grad_x, dproj, h1, g_uk_t, uq, st_p = _pre_attention_backward(
        xs, dx1, proj, q, dqt, dkc, du, cos, sin, mod6, g_mix, g_q, g_kv, w_in_f, w_uq_f, w_uk_t, T, TQ)
    rows_in = D_MODEL // N_CHIPS
    g_in_p = _tn_matmul(dproj, h1, pl.BlockSpec((TW, PROJ_W), lambda g, i: (i, 0)),
                        pl.BlockSpec((TW, rows_in), lambda g, i: (i, g)), N_CHIPS, PROJ_W, rows_in, steps, "grad_w_in")

    g_in = jnp.concatenate([g_in_p[:, :O_KR + ROPE], g_in_p[:, O_U:]], axis=1)
    g_uq = jnp.concatenate([jnp.concatenate([uq[:, h * NOPE:(h + 1) * NOPE], uq[:, O_QA + h * ROPE:O_QA + (h + 1) * ROPE]],
                                            axis=1) for h in range(HEADS)], axis=1).reshape(N_CHIPS, -1, HEADS * HEAD_QK)
    small = _pack_rows([g_uk_t, g_uv_t, g_pool, st_p[2], st_p[3, :Q_LORA], st_p[4, :KV_LORA], st_m[1, :POOL_W],
                        st_b[2], st_f[0]])
    small = jnp.concatenate([small, jnp.zeros((SMALL_ROWS - small.shape[0], 128), F32)]).reshape(N_CHIPS, -1, 128)
    grads = [g_in, g_uq, small]
    dmod = jnp.concatenate([jnp.stack([st_p[0], st_p[1], st_m[0], st_b[0], st_b[1], st_f[1]]).reshape(48, 128),
                            jnp.zeros((8, 128), F32).at[0, 0].set(st_f[2, 0])])

    got, dmod_all = _grad_swap_halves(grads, dmod)
    chip_sums = _add_my_halves(core_arr, grads, got, "add_half_near")
    n_send, n_recv, n_lands, token = _split_start(
        "near_exchange_start", chip_sums, [jax.ShapeDtypeStruct((3,) + s.shape[1:], F32) for s in chip_sums],
        3 * len(chip_sums), _plan_exchange_start)

    f_others = _split_wait("far_exchange_wait", f_send, f_recv, f_sums, f_lands, token, _plan_exchange_wait)
    chip_core = jnp.concatenate([chip_arr, core_arr])
    f_pairs = (_add_chips_into_pairs(chip_core, f_sums[:2], f_others[:2], "add_chips_gate_up")
               + _add_chips_into_pairs(chip_core, f_sums[2:], f_others[2:], "add_chips_down_o"))
    f_send, f_recv, f_pairs, token = _split_start("far_finish_start", [], f_pairs, len(f_pairs), _plan_finish_start)
    gw_ada, gb_ada = _ada_grads(c_all, dmod_all.reshape(8, -1) + token[0, 0], chip_arr)
    loss = gb_ada[0, N_MOD * D_MODEL]
    gb_ada = gb_ada[:, :N_MOD * D_MODEL]
    f_fulls = _split_wait("far_finish_wait", f_send, f_recv, [], f_pairs, gw_ada, _plan_finish_wait)
    gw_gate, gw_up, gw_down, gw_o = [f.reshape(-1, f.shape[2]) for f in f_fulls]

    untr = lambda a: jnp.transpose(a)[None]
    grad_out, delta_out, newm_out, newv_out = {}, {}, {}, {}

    def adam_sharded(n, w, g2, m, v, transposed, landed=True):
        view = (lambda a: jnp.transpose(a[0])) if transposed else (lambda a: a[0])
        back = untr if transposed else (lambda a: a[None])
        g_, d_, m_, v_ = _adamw(view(w), g2.reshape(view(w).shape), view(m), view(v), "adamw_" + n, landed)
        grad_out[n], delta_out[n], newm_out[n], newv_out[n] = back(g_), back(d_), back(m_), back(v_)
        return d_

    done = [adam_sharded("w_gate", w_gate, gw_gate, m_w_gate, v_w_gate, True),
            adam_sharded("w_up", w_up, gw_up, m_w_up, v_w_up, True),
            adam_sharded("w_down", w_down, gw_down, m_w_down, v_w_down, False),
            adam_sharded("w_o", w_o, gw_o, m_w_o, v_w_o, False)]
    after_all = jnp.stack([d[0, 0] for d in done])

    others = _split_wait("near_exchange_wait", n_send, n_recv, chip_sums, n_lands, after_all, _plan_exchange_wait)
    n_pairs = _add_chips_into_pairs(chip_core, chip_sums[:2], others[:2], "add_chips_in_uq")
    small_grid = _add_chips_into_grid(chip_core, chip_sums[2], others[2], "add_chips_small")
    n_send, n_recv, n_lands, token = _split_start(
        "near_finish_start", [], n_pairs + [small_grid], 2 + len(RELATIONS), _plan_near_finish_start)
    gw_ada, _ = lax.optimization_barrier((gw_ada, token))
    d_ada = adam_sharded("w_ada", w_ada, gw_ada, m_w_ada, v_w_ada, False, landed=False)
    n_lands = _split_wait("near_finish_wait", n_send, n_recv, [], n_lands, d_ada, _plan_near_finish_wait)
    gw_in, gw_uq = [f.reshape(-1, f.shape[2]) for f in n_lands[:2]]
    small_all = n_lands[2].reshape(SMALL_ROWS * 128)
    adam_sharded("w_in", w_in, gw_in, m_w_in, v_w_in, True)
    adam_sharded("w_uq", w_uq, gw_uq, m_w_uq, v_w_uq, False)

    n_sq = KV_LORA * HEADS * 128
    sizes = [n_sq, n_sq, n_sq, D_MODEL, Q_LORA, KV_LORA, POOL_W, D_MODEL, D_MODEL]
    offs = [0]
    for s_ in sizes:
        offs.append(offs[-1] + s_)
    piece = lambda k: small_all[offs[k]:offs[k + 1]]
    grads_small = {
        "w_uk": jnp.transpose(piece(0).reshape(HEADS, KV_LORA, NOPE), (1, 0, 2)),
        "w_uv": jnp.transpose(piece(1).reshape(HEADS, KV_LORA, 128), (1, 0, 2)),
        "w_pool": piece(2).reshape(4, POOL_GROUP, POOL_GROUP),
        "g_mix": piece(3), "g_q": piece(4), "g_kv": piece(5), "pool_scale": piece(6), "g_ffn": piece(7),
        "g_final": piece(8), "b_ada": gb_ada.reshape(-1),
    }
    weights_small = {"w_uk": w_uk, "w_uv": w_uv, "w_pool": w_pool, "g_mix": g_mix, "g_q": g_q, "g_kv": g_kv,
                     "pool_scale": pool_scale, "g_ffn": g_ffn, "g_final": g_final, "b_ada": b_ada}
    m_small = {"w_uk": m_w_uk, "w_uv": m_w_uv, "w_pool": m_w_pool, "g_mix": m_g_mix, "g_q": m_g_q, "g_kv": m_g_kv,
               "pool_scale": m_pool_scale, "g_ffn": m_g_ffn, "g_final": m_g_final, "b_ada": m_b_ada}
    v_small = {"w_uk": v_w_uk, "w_uv": v_w_uv, "w_pool": v_w_pool, "g_mix": v_g_mix, "g_q": v_g_q, "g_kv": v_g_kv,
               "pool_scale": v_pool_scale, "g_ffn": v_g_ffn, "g_final": v_g_final, "b_ada": v_b_ada}
    pack = lambda d: _pack_rows([d[n] for n in SMALL_NAMES])
    _, d_s, m_s, v_s = _adamw(pack(weights_small), pack(grads_small), pack(m_small), pack(v_small), "adamw_small",
                              g_is_landing_zone=False)

    def unpack(flat2d):
        flat = flat2d.reshape(-1)
        out, o = {}, 0
        for n in SMALL_NAMES:
            size = weights_small[n].size
            out[n] = flat[o:o + size].reshape(weights_small[n].shape)
            o += size
        return out

    delta_s, newm_s, newv_s = unpack(d_s), unpack(m_s), unpack(v_s)

    for n in SMALL_NAMES:
        grad_out[n] = grads_small[n].reshape(weights_small[n].shape)
        delta_out[n], newm_out[n], newv_out[n] = delta_s[n], newm_s[n], newv_s[n]

    order = ("w_ada", "b_ada", "g_mix", "w_in", "g_q", "g_kv", "w_uq", "w_uk", "w_uv", "w_pool", "pool_scale", "w_o",
             "g_ffn", "w_gate", "w_up", "w_down", "g_final")
    return (loss, grad_x.reshape(x.shape), *[grad_out[n] for n in order], *[delta_out[n] for n in order],
            *[newm_out[n] for n in order], *[newv_out[n] for n in order])
```
